```python
import jax, jax.numpy as jnp
from jax import lax
import numpy as np

D_MODEL = 1024
BATCH = 16
SEQ = 2048
DEPTH = 1

GDN_HEADS = 8
GDN_DK = 128
GDN_DV = 128
GDN_CONV = 4
GDN_CHUNK = 64
FOX_HEADS = 8
FOX_DH = 128
FOX_BLOCK = 128
D_FF = 4 * D_MODEL
EPS = 1e-6

GDN_QK_W = GDN_HEADS * GDN_DK
GDN_V_W = GDN_HEADS * GDN_DV
FOX_W = FOX_HEADS * FOX_DH
IN_SPLITS = (GDN_QK_W, GDN_QK_W, GDN_V_W, GDN_V_W, GDN_HEADS, GDN_HEADS,
             FOX_W, FOX_W, FOX_W, FOX_HEADS, D_MODEL, D_MODEL)
N_IN = sum(IN_SPLITS)

kernel_name = "hybrid_gdn_fox_gated_merge_block"


def rmsnorm(x, g):
    xf = x.astype(jnp.float32)
    y = xf * lax.rsqrt(jnp.mean(xf * xf, axis=-1, keepdims=True) + EPS)
    return (y * g.astype(jnp.float32)).astype(x.dtype)


def l2norm(x):
    xf = x.astype(jnp.float32)
    return xf * lax.rsqrt(jnp.sum(xf * xf, axis=-1, keepdims=True) + EPS)


def to_heads(x, n_heads):
    b, t, w = x.shape
    return x.reshape(b, t, n_heads, w // n_heads).transpose(0, 2, 1, 3)


def causal_depthwise_conv(x, w):
    k_width = w.shape[0]
    t = x.shape[1]
    xp = jnp.pad(x, ((0, 0), (k_width - 1, 0), (0, 0)))
    y = xp[:, 0:t] * w[0]
    for i in range(1, k_width):
        y = y + xp[:, i:i + t] * w[i]
    return y


def gated_delta_rule_chunked(q, k, v, g, beta):
    b, h, t, dk = q.shape
    dv = v.shape[-1]
    c = GDN_CHUNK
    n = t // c
    q = q.reshape(b, h, n, c, dk)
    k = k.reshape(b, h, n, c, dk)
    v = v.reshape(b, h, n, c, dv)
    g = g.reshape(b, h, n, c)
    beta = beta.reshape(b, h, n, c)

    G = jnp.cumsum(g, axis=-1)
    diff = G[..., :, None] - G[..., None, :]
    tri_incl = jnp.tril(jnp.ones((c, c), dtype=bool))
    tri_strict = jnp.tril(jnp.ones((c, c), dtype=bool), -1)
    decay = jnp.where(tri_incl, jnp.exp(jnp.where(tri_incl, diff, 0.0)), 0.0)

    kk = jnp.einsum('bhncd,bhnsd->bhncs', k, k)
    a_mat = jnp.where(tri_strict, beta[..., None] * kk * decay, 0.0)
    eye = jnp.eye(c, dtype=jnp.float32)
    rhs = jnp.concatenate([v * beta[..., None], k * (beta * jnp.exp(G))[..., None]], axis=-1)
    a_full = jnp.broadcast_to(a_mat + eye, (b, h, n, c, c))
    sol = lax.linalg.triangular_solve(a_full, rhs, left_side=True, lower=True,
                                      unit_diagonal=True)
    value, k_cum = sol[..., :dv], sol[..., dv:]

    attn_intra = jnp.einsum('bhncd,bhnsd->bhncs', q, k) * decay
    q_dec = q * jnp.exp(G)[..., None]
    g_last = G[..., -1:]
    k_dec = k * jnp.exp(g_last - G)[..., None]
    chunk_decay = jnp.exp(g_last[..., 0])

    def step(s, inp):
        value_c, kcum_c, attn_c, qdec_c, kdec_c, dec_c = inp
        v_new = value_c - jnp.einsum('bhcd,bhde->bhce', kcum_c, s)
        o = (jnp.einsum('bhcd,bhde->bhce', qdec_c, s)
             + jnp.einsum('bhcs,bhse->bhce', attn_c, v_new))
        s = s * dec_c[..., None, None] + jnp.einsum('bhcd,bhce->bhde', kdec_c, v_new)
        return s, o

    xs = tuple(jnp.moveaxis(a, 2, 0) for a in (value, k_cum, attn_intra, q_dec, k_dec, chunk_decay))
    s0 = jnp.zeros((b, h, dk, dv), jnp.float32)
    _, o = lax.scan(step, s0, xs)
    return jnp.moveaxis(o, 0, 2).reshape(b, h, t, dv)


def forgetting_attention(q, k, v, log_f):
    t = q.shape[2]
    cum = jnp.cumsum(log_f, axis=-1)
    scale = FOX_DH ** -0.5
    outs = []
    for i in range(t // FOX_BLOCK):
        q0 = i * FOX_BLOCK
        q1 = q0 + FOX_BLOCK
        s = (jnp.einsum('bhqd,bhkd->bhqk', q[:, :, q0:q1], k[:, :, :q1]) * scale
             + cum[:, :, q0:q1, None] - cum[:, :, None, :q1])
        mask = (q0 + jnp.arange(FOX_BLOCK))[:, None] >= jnp.arange(q1)[None, :]
        p = jax.nn.softmax(jnp.where(mask, s, -jnp.inf), axis=-1)
        outs.append(jnp.einsum('bhqk,bhkd->bhqd', p, v[:, :, :q1]))
    return jnp.concatenate(outs, axis=2)


def _fwd_setup_inputs(seed: int = 0) -> dict:
    key = jax.random.key(seed)
    ks = jax.random.split(key, 20)
    f32 = jnp.float32

    def nrm(k, shape, fan_in):
        return jax.random.normal(k, shape, f32) * (fan_in ** -0.5)

    def gain(k, shape):
        return 1.0 + 0.02 * jax.random.normal(k, shape, f32)

    dt = jnp.exp(jax.random.uniform(ks[5], (DEPTH, GDN_HEADS), f32,
                                    minval=float(np.log(1e-3)), maxval=float(np.log(1e-1))))
    return {
        "x": jax.random.normal(ks[0], (BATCH, SEQ, D_MODEL), f32),
        "norm_mix_g": gain(ks[1], (DEPTH, D_MODEL)),
        "w_in": nrm(ks[2], (DEPTH, D_MODEL, N_IN), D_MODEL),
        "gdn_conv_w": nrm(ks[3], (DEPTH, GDN_CONV, 2 * GDN_QK_W + GDN_V_W), GDN_CONV),
        "gdn_a_log": jnp.log(jax.random.uniform(ks[4], (DEPTH, GDN_HEADS), f32, minval=1.0, maxval=16.0)),
        "gdn_dt_bias": dt + jnp.log(-jnp.expm1(-dt)),
        "gdn_norm_g": gain(ks[6], (DEPTH, GDN_DV)),
        "fox_q_norm_g": gain(ks[7], (DEPTH, FOX_DH)),
        "fox_k_norm_g": gain(ks[8], (DEPTH, FOX_DH)),
        "fox_f_bias": jax.random.uniform(ks[9], (DEPTH, FOX_HEADS), f32, minval=2.0, maxval=6.0),
        "w_proj_gdn": nrm(ks[10], (DEPTH, GDN_V_W, D_MODEL), GDN_V_W),
        "w_proj_fox": nrm(ks[11], (DEPTH, FOX_W, D_MODEL), FOX_W),
        "w_out": nrm(ks[12], (DEPTH, D_MODEL, D_MODEL), D_MODEL),
        "norm_mlp_g": gain(ks[13], (DEPTH, D_MODEL)),
        "w_up": nrm(ks[14], (DEPTH, D_MODEL, D_FF), D_MODEL),
        "w_down": nrm(ks[15], (DEPTH, D_FF, D_MODEL), D_FF),
    }


def _fwd_reference(x, norm_mix_g, w_in, gdn_conv_w, gdn_a_log, gdn_dt_bias, gdn_norm_g,
              fox_q_norm_g, fox_k_norm_g, fox_f_bias, w_proj_gdn, w_proj_fox, w_out,
              norm_mlp_g, w_up, w_down):
    f32 = jnp.float32
    b, t, _ = x.shape
    split_idx = [int(i) for i in np.cumsum(IN_SPLITS)[:-1]]
    for l in range(DEPTH):
        u = rmsnorm(x, norm_mix_g[l])
        proj = u @ w_in[l]
        (gq, gk, gv, gz, ga, gb, fq, fk, fv, ff, gate_a, gate_b) = jnp.split(proj, split_idx, axis=-1)

        qkv = jax.nn.silu(causal_depthwise_conv(jnp.concatenate([gq, gk, gv], axis=-1), gdn_conv_w[l]))
        cq, ck, cv = jnp.split(qkv, [GDN_QK_W, 2 * GDN_QK_W], axis=-1)
        qh = l2norm(to_heads(cq, GDN_HEADS)) * (GDN_DK ** -0.5)
        kh = l2norm(to_heads(ck, GDN_HEADS))
        vh = to_heads(cv, GDN_HEADS).astype(f32)
        beta = jax.nn.sigmoid(gb.astype(f32)).transpose(0, 2, 1)
        g = (-jnp.exp(gdn_a_log[l].astype(f32))
             * jax.nn.softplus(ga.astype(f32) + gdn_dt_bias[l].astype(f32))).transpose(0, 2, 1)
        o_a = gated_delta_rule_chunked(qh, kh, vh, g, beta).transpose(0, 2, 1, 3)
        z = gz.reshape(b, t, GDN_HEADS, GDN_DV).astype(f32)
        o_a = rmsnorm(o_a, gdn_norm_g[l]) * jax.nn.silu(z)
        y_a = o_a.reshape(b, t, GDN_V_W).astype(x.dtype) @ w_proj_gdn[l]

        fqh = rmsnorm(to_heads(fq, FOX_HEADS), fox_q_norm_g[l]).astype(f32)
        fkh = rmsnorm(to_heads(fk, FOX_HEADS), fox_k_norm_g[l]).astype(f32)
        fvh = to_heads(fv, FOX_HEADS).astype(f32)
        log_f = jax.nn.log_sigmoid(ff.astype(f32) + fox_f_bias[l].astype(f32)).transpose(0, 2, 1)
        o_b = forgetting_attention(fqh, fkh, fvh, log_f).transpose(0, 2, 1, 3)
        y_b = o_b.reshape(b, t, FOX_W).astype(x.dtype) @ w_proj_fox[l]

        merged = jax.nn.sigmoid(gate_a) * y_a + jax.nn.sigmoid(gate_b) * y_b
        h = x + merged @ w_out[l]

        hn = rmsnorm(h, norm_mlp_g[l])
        x = h + jnp.square(jax.nn.relu(hn @ w_up[l])) @ w_down[l]
    return x


import jax as _jax
import jax.numpy as _jnp

TWIN_FORMAT = 'train_step'
FWD_PARAMS = ['x', 'norm_mix_g', 'w_in', 'gdn_conv_w', 'gdn_a_log', 'gdn_dt_bias', 'gdn_norm_g', 'fox_q_norm_g', 'fox_k_norm_g', 'fox_f_bias', 'w_proj_gdn', 'w_proj_fox', 'w_out', 'norm_mlp_g', 'w_up', 'w_down']
TWIN_WEIGHTS = ['norm_mix_g', 'w_in', 'gdn_conv_w', 'gdn_a_log', 'gdn_dt_bias', 'gdn_norm_g', 'fox_q_norm_g', 'fox_k_norm_g', 'fox_f_bias', 'w_proj_gdn', 'w_proj_fox', 'w_out', 'norm_mlp_g', 'w_up', 'w_down']
TWIN_DIFF_INPUT = 'x'
TWIN_INPUTS = ['x', 'norm_mix_g', 'w_in', 'gdn_conv_w', 'gdn_a_log', 'gdn_dt_bias', 'gdn_norm_g', 'fox_q_norm_g', 'fox_k_norm_g', 'fox_f_bias', 'w_proj_gdn', 'w_proj_fox', 'w_out', 'norm_mlp_g', 'w_up', 'w_down', 'loss_target', 'm_norm_mix_g', 'm_w_in', 'm_gdn_conv_w', 'm_gdn_a_log', 'm_gdn_dt_bias', 'm_gdn_norm_g', 'm_fox_q_norm_g', 'm_fox_k_norm_g', 'm_fox_f_bias', 'm_w_proj_gdn', 'm_w_proj_fox', 'm_w_out', 'm_norm_mlp_g', 'm_w_up', 'm_w_down', 'v_norm_mix_g', 'v_w_in', 'v_gdn_conv_w', 'v_gdn_a_log', 'v_gdn_dt_bias', 'v_gdn_norm_g', 'v_fox_q_norm_g', 'v_fox_k_norm_g', 'v_fox_f_bias', 'v_w_proj_gdn', 'v_w_proj_fox', 'v_w_out', 'v_norm_mlp_g', 'v_w_up', 'v_w_down']
TWIN_OUTPUTS = ['loss', 'grad_x', 'grad_norm_mix_g', 'grad_w_in', 'grad_gdn_conv_w', 'grad_gdn_a_log', 'grad_gdn_dt_bias', 'grad_gdn_norm_g', 'grad_fox_q_norm_g', 'grad_fox_k_norm_g', 'grad_fox_f_bias', 'grad_w_proj_gdn', 'grad_w_proj_fox', 'grad_w_out', 'grad_norm_mlp_g', 'grad_w_up', 'grad_w_down', 'delta_norm_mix_g', 'delta_w_in', 'delta_gdn_conv_w', 'delta_gdn_a_log', 'delta_gdn_dt_bias', 'delta_gdn_norm_g', 'delta_fox_q_norm_g', 'delta_fox_k_norm_g', 'delta_fox_f_bias', 'delta_w_proj_gdn', 'delta_w_proj_fox', 'delta_w_out', 'delta_norm_mlp_g', 'delta_w_up', 'delta_w_down', 'new_m_norm_mix_g', 'new_m_w_in', 'new_m_gdn_conv_w', 'new_m_gdn_a_log', 'new_m_gdn_dt_bias', 'new_m_gdn_norm_g', 'new_m_fox_q_norm_g', 'new_m_fox_k_norm_g', 'new_m_fox_f_bias', 'new_m_w_proj_gdn', 'new_m_w_proj_fox', 'new_m_w_out', 'new_m_norm_mlp_g', 'new_m_w_up', 'new_m_w_down', 'new_v_norm_mix_g', 'new_v_w_in', 'new_v_gdn_conv_w', 'new_v_gdn_a_log', 'new_v_gdn_dt_bias', 'new_v_gdn_norm_g', 'new_v_fox_q_norm_g', 'new_v_fox_k_norm_g', 'new_v_fox_f_bias', 'new_v_w_proj_gdn', 'new_v_w_proj_fox', 'new_v_w_out', 'new_v_norm_mlp_g', 'new_v_w_up', 'new_v_w_down']
TWIN_LEAF_KINDS = {'loss': 'loss', 'grad_x': 'grad_x', 'grad_norm_mix_g': 'grad_w', 'grad_w_in': 'grad_w', 'grad_gdn_conv_w': 'grad_w', 'grad_gdn_a_log': 'grad_w', 'grad_gdn_dt_bias': 'grad_w', 'grad_gdn_norm_g': 'grad_w', 'grad_fox_q_norm_g': 'grad_w', 'grad_fox_k_norm_g': 'grad_w', 'grad_fox_f_bias': 'grad_w', 'grad_w_proj_gdn': 'grad_w', 'grad_w_proj_fox': 'grad_w', 'grad_w_out': 'grad_w', 'grad_norm_mlp_g': 'grad_w', 'grad_w_up': 'grad_w', 'grad_w_down': 'grad_w', 'delta_norm_mix_g': 'delta_w', 'delta_w_in': 'delta_w', 'delta_gdn_conv_w': 'delta_w', 'delta_gdn_a_log': 'delta_w', 'delta_gdn_dt_bias': 'delta_w', 'delta_gdn_norm_g': 'delta_w', 'delta_fox_q_norm_g': 'delta_w', 'delta_fox_k_norm_g': 'delta_w', 'delta_fox_f_bias': 'delta_w', 'delta_w_proj_gdn': 'delta_w', 'delta_w_proj_fox': 'delta_w', 'delta_w_out': 'delta_w', 'delta_norm_mlp_g': 'delta_w', 'delta_w_up': 'delta_w', 'delta_w_down': 'delta_w', 'new_m_norm_mix_g': 'new_m', 'new_m_w_in': 'new_m', 'new_m_gdn_conv_w': 'new_m', 'new_m_gdn_a_log': 'new_m', 'new_m_gdn_dt_bias': 'new_m', 'new_m_gdn_norm_g': 'new_m', 'new_m_fox_q_norm_g': 'new_m', 'new_m_fox_k_norm_g': 'new_m', 'new_m_fox_f_bias': 'new_m', 'new_m_w_proj_gdn': 'new_m', 'new_m_w_proj_fox': 'new_m', 'new_m_w_out': 'new_m', 'new_m_norm_mlp_g': 'new_m', 'new_m_w_up': 'new_m', 'new_m_w_down': 'new_m', 'new_v_norm_mix_g': 'new_v', 'new_v_w_in': 'new_v', 'new_v_gdn_conv_w': 'new_v', 'new_v_gdn_a_log': 'new_v', 'new_v_gdn_dt_bias': 'new_v', 'new_v_gdn_norm_g': 'new_v', 'new_v_fox_q_norm_g': 'new_v', 'new_v_fox_k_norm_g': 'new_v', 'new_v_fox_f_bias': 'new_v', 'new_v_w_proj_gdn': 'new_v', 'new_v_w_proj_fox': 'new_v', 'new_v_w_out': 'new_v', 'new_v_norm_mlp_g': 'new_v', 'new_v_w_up': 'new_v', 'new_v_w_down': 'new_v'}


def _forward(args):
    return _fwd_reference(*[args[k] for k in FWD_PARAMS])


def _output_shape():
    out = _jax.eval_shape(lambda: _forward(_fwd_setup_inputs(0)))
    return out.shape, out.dtype

N_MICROBATCH = 1
ADAM_LR = 0.001
ADAM_B1 = 0.9
ADAM_B2 = 0.999
ADAM_EPS = 1e-08
ADAM_WD = 0.01
ADAM_STEP = 10
PER_EXAMPLE_BATCH_AXIS = {'x': 0, 'loss_target': 0}
SHARED_INPUTS = []
_WEIGHT_DTYPES = {'norm_mix_g': _jnp.float32, 'w_in': _jnp.float32, 'gdn_conv_w': _jnp.float32, 'gdn_a_log': _jnp.float32, 'gdn_dt_bias': _jnp.float32, 'gdn_norm_g': _jnp.float32, 'fox_q_norm_g': _jnp.float32, 'fox_k_norm_g': _jnp.float32, 'fox_f_bias': _jnp.float32, 'w_proj_gdn': _jnp.float32, 'w_proj_fox': _jnp.float32, 'w_out': _jnp.float32, 'norm_mlp_g': _jnp.float32, 'w_up': _jnp.float32, 'w_down': _jnp.float32}
MOMENT_SCALE = {'norm_mix_g': 4.108477e+00, 'w_in': 1.632037e-01, 'gdn_conv_w': 6.211632e-01, 'gdn_a_log': 1.249477e+01, 'gdn_dt_bias': 1.196472e+01, 'gdn_norm_g': 2.568979e+01, 'fox_q_norm_g': 1.758401e+00, 'fox_k_norm_g': 1.769527e+00, 'fox_f_bias': 2.021467e+01, 'w_proj_gdn': 1.673711e+00, 'w_proj_fox': 1.471509e-01, 'w_out': 1.544018e+00, 'norm_mlp_g': 9.607146e+01, 'w_up': 9.048936e-01, 'w_down': 8.026486e+00}


def _to_microbatches(a, axis):
    t = _jnp.moveaxis(a, axis, 0)
    t = t.reshape((N_MICROBATCH, t.shape[0] // N_MICROBATCH) + t.shape[1:])
    return _jnp.moveaxis(t, 1, axis + 1)


def setup_inputs(seed: int = 0) -> dict:
    inp = _fwd_setup_inputs(seed)
    key = _jax.random.fold_in(_jax.random.key(seed), 7919)
    shape, _ = _output_shape()
    out = dict(inp)
    out["loss_target"] = _jax.random.normal(_jax.random.fold_in(key, 0), shape, _jnp.float32)
    for i, name in enumerate(TWIN_WEIGHTS):
        w = inp[name].astype(_jnp.float32)
        if MOMENT_SCALE is None:
            s = _jnp.sqrt(_jnp.mean(_jnp.square(w)) + 1e-30)
        else:
            s = MOMENT_SCALE[name]
        km, kv = _jax.random.split(_jax.random.fold_in(key, i + 1))
        out[name] = w
        out["m_" + name] = s * _jax.random.normal(km, w.shape, _jnp.float32)
        out["v_" + name] = (s * s) * _jax.random.uniform(kv, w.shape, _jnp.float32, 0.5, 1.5)
    if N_MICROBATCH > 1:
        for name, axis in PER_EXAMPLE_BATCH_AXIS.items():
            out[name] = _to_microbatches(out[name], axis)
    return {'x': out['x'], 'norm_mix_g': out['norm_mix_g'], 'w_in': out['w_in'], 'gdn_conv_w': out['gdn_conv_w'], 'gdn_a_log': out['gdn_a_log'], 'gdn_dt_bias': out['gdn_dt_bias'], 'gdn_norm_g': out['gdn_norm_g'], 'fox_q_norm_g': out['fox_q_norm_g'], 'fox_k_norm_g': out['fox_k_norm_g'], 'fox_f_bias': out['fox_f_bias'], 'w_proj_gdn': out['w_proj_gdn'], 'w_proj_fox': out['w_proj_fox'], 'w_out': out['w_out'], 'norm_mlp_g': out['norm_mlp_g'], 'w_up': out['w_up'], 'w_down': out['w_down'], 'loss_target': out['loss_target'], 'm_norm_mix_g': out['m_norm_mix_g'], 'm_w_in': out['m_w_in'], 'm_gdn_conv_w': out['m_gdn_conv_w'], 'm_gdn_a_log': out['m_gdn_a_log'], 'm_gdn_dt_bias': out['m_gdn_dt_bias'], 'm_gdn_norm_g': out['m_gdn_norm_g'], 'm_fox_q_norm_g': out['m_fox_q_norm_g'], 'm_fox_k_norm_g': out['m_fox_k_norm_g'], 'm_fox_f_bias': out['m_fox_f_bias'], 'm_w_proj_gdn': out['m_w_proj_gdn'], 'm_w_proj_fox': out['m_w_proj_fox'], 'm_w_out': out['m_w_out'], 'm_norm_mlp_g': out['m_norm_mlp_g'], 'm_w_up': out['m_w_up'], 'm_w_down': out['m_w_down'], 'v_norm_mix_g': out['v_norm_mix_g'], 'v_w_in': out['v_w_in'], 'v_gdn_conv_w': out['v_gdn_conv_w'], 'v_gdn_a_log': out['v_gdn_a_log'], 'v_gdn_dt_bias': out['v_gdn_dt_bias'], 'v_gdn_norm_g': out['v_gdn_norm_g'], 'v_fox_q_norm_g': out['v_fox_q_norm_g'], 'v_fox_k_norm_g': out['v_fox_k_norm_g'], 'v_fox_f_bias': out['v_fox_f_bias'], 'v_w_proj_gdn': out['v_w_proj_gdn'], 'v_w_proj_fox': out['v_w_proj_fox'], 'v_w_out': out['v_w_out'], 'v_norm_mlp_g': out['v_norm_mlp_g'], 'v_w_up': out['v_w_up'], 'v_w_down': out['v_w_down']}


def _loss(weights, diff, rest, loss_target):
    with _jax.named_scope("forward"):
        args = {**rest, TWIN_DIFF_INPUT: diff, **{k: w.astype(_WEIGHT_DTYPES[k]) for k, w in weights.items()}}
        y = _forward(args)
    with _jax.named_scope("loss_head"):
        err = _jnp.square(y.astype(_jnp.float32) - loss_target)
        return 0.5 * _jnp.sum(_jnp.mean(err, axis=-1)) if err.ndim else 0.5 * err


def _adamw(w, g, m, v):
    m = ADAM_B1 * m + (1.0 - ADAM_B1) * g
    v = ADAM_B2 * v + (1.0 - ADAM_B2) * _jnp.square(g)
    m_hat = m / (1.0 - ADAM_B1 ** ADAM_STEP)
    v_hat = v / (1.0 - ADAM_B2 ** ADAM_STEP)
    delta = -ADAM_LR * (m_hat / (_jnp.sqrt(v_hat) + ADAM_EPS) + ADAM_WD * w)
    return delta, m, v


def reference(x, norm_mix_g, w_in, gdn_conv_w, gdn_a_log, gdn_dt_bias, gdn_norm_g, fox_q_norm_g, fox_k_norm_g, fox_f_bias, w_proj_gdn, w_proj_fox, w_out, norm_mlp_g, w_up, w_down, loss_target, m_norm_mix_g, m_w_in, m_gdn_conv_w, m_gdn_a_log, m_gdn_dt_bias, m_gdn_norm_g, m_fox_q_norm_g, m_fox_k_norm_g, m_fox_f_bias, m_w_proj_gdn, m_w_proj_fox, m_w_out, m_norm_mlp_g, m_w_up, m_w_down, v_norm_mix_g, v_w_in, v_gdn_conv_w, v_gdn_a_log, v_gdn_dt_bias, v_gdn_norm_g, v_fox_q_norm_g, v_fox_k_norm_g, v_fox_f_bias, v_w_proj_gdn, v_w_proj_fox, v_w_out, v_norm_mlp_g, v_w_up, v_w_down):
    given = dict(x=x, norm_mix_g=norm_mix_g, w_in=w_in, gdn_conv_w=gdn_conv_w, gdn_a_log=gdn_a_log, gdn_dt_bias=gdn_dt_bias, gdn_norm_g=gdn_norm_g, fox_q_norm_g=fox_q_norm_g, fox_k_norm_g=fox_k_norm_g, fox_f_bias=fox_f_bias, w_proj_gdn=w_proj_gdn, w_proj_fox=w_proj_fox, w_out=w_out, norm_mlp_g=norm_mlp_g, w_up=w_up, w_down=w_down, loss_target=loss_target, m_norm_mix_g=m_norm_mix_g, m_w_in=m_w_in, m_gdn_conv_w=m_gdn_conv_w, m_gdn_a_log=m_gdn_a_log, m_gdn_dt_bias=m_gdn_dt_bias, m_gdn_norm_g=m_gdn_norm_g, m_fox_q_norm_g=m_fox_q_norm_g, m_fox_k_norm_g=m_fox_k_norm_g, m_fox_f_bias=m_fox_f_bias, m_w_proj_gdn=m_w_proj_gdn, m_w_proj_fox=m_w_proj_fox, m_w_out=m_w_out, m_norm_mlp_g=m_norm_mlp_g, m_w_up=m_w_up, m_w_down=m_w_down, v_norm_mix_g=v_norm_mix_g, v_w_in=v_w_in, v_gdn_conv_w=v_gdn_conv_w, v_gdn_a_log=v_gdn_a_log, v_gdn_dt_bias=v_gdn_dt_bias, v_gdn_norm_g=v_gdn_norm_g, v_fox_q_norm_g=v_fox_q_norm_g, v_fox_k_norm_g=v_fox_k_norm_g, v_fox_f_bias=v_fox_f_bias, v_w_proj_gdn=v_w_proj_gdn, v_w_proj_fox=v_w_proj_fox, v_w_out=v_w_out, v_norm_mlp_g=v_norm_mlp_g, v_w_up=v_w_up, v_w_down=v_w_down)
    weights = {n: given[n] for n in TWIN_WEIGHTS}
    shared = {n: given[n] for n in SHARED_INPUTS}
    per_example = {n: given[n] for n in ['x']}
    grad_fn = _jax.value_and_grad(_loss, argnums=(0, 1))

    def one_microbatch(ex, loss_target):
        ex = dict(ex)
        diff = ex.pop(TWIN_DIFF_INPUT)
        return grad_fn(weights, diff, {**shared, **ex}, loss_target)

    if N_MICROBATCH == 1:
        loss, (grad_w, grad_x) = one_microbatch(per_example, given["loss_target"])
    else:
        def body(carry, xs):
            loss_sum, grad_sum = carry
            l_k, (gw_k, gx_k) = one_microbatch(xs[0], xs[1])
            with _jax.named_scope("update"):
                return (loss_sum + l_k, _jax.tree.map(_jnp.add, grad_sum, gw_k)), gx_k

        init = (_jnp.zeros((), _jnp.float32), _jax.tree.map(_jnp.zeros_like, weights))
        (loss, grad_w), grad_x = _jax.lax.scan(body, init, (per_example, given["loss_target"]))
    with _jax.named_scope("update"):
        delta_w, new_m, new_v = {}, {}, {}
        for n in TWIN_WEIGHTS:
            delta_w[n], new_m[n], new_v[n] = _adamw(weights[n], grad_w[n], given["m_" + n], given["v_" + n])
    return (loss, grad_x, *[grad_w[n] for n in TWIN_WEIGHTS], *[delta_w[n] for n in TWIN_WEIGHTS],
            *[new_m[n] for n in TWIN_WEIGHTS], *[new_v[n] for n in TWIN_WEIGHTS])
```

```python
import functools

import jax
import jax.numpy as jnp
from jax import lax
from jax.experimental import pallas as pl
from jax.experimental.pallas import tpu as pltpu

F32 = jnp.float32
BF16 = jnp.bfloat16
HI = lax.Precision.HIGHEST
MESH = pl.DeviceIdType.MESH

N_DEV = 8
D_MODEL = 1024
HEADS = 8
DH = 128
CONV_K = 4
CHUNK = 64
FOX_BLOCK = 128
D_FF = 4 * D_MODEL
EPS = 1e-6
LANES = 128
NEG = -1e30
IN_OFF = {"gq": 0, "gk": 1024, "gv": 2048, "gz": 3072, "ga": 4096, "gb": 4104, "fq": 4112, "fk": 5136,
          "fv": 6160, "ff": 7184, "gate_a": 7192, "gate_b": 8216, "end": 9240}
LANE_GA, LANE_GB, LANE_FF = 0, 8, 16

ADAM_LR = 0.001
ADAM_B1 = 0.9
ADAM_B2 = 0.999
ADAM_EPS = 1e-08
ADAM_WD = 0.01
ADAM_STEP = 10

VMEM_LIMIT = 56 * 1024 * 1024


def _cparams(sem=None):
    return pltpu.CompilerParams(dimension_semantics=sem, vmem_limit_bytes=VMEM_LIMIT)


def _sigmoid(x):
    return 1.0 / (1.0 + jnp.exp(-x))


def _softplus(x):
    return jnp.maximum(x, 0.0) + jnp.log(1.0 + jnp.exp(-jnp.abs(x)))


def _dot(a, b, prec=None):
    return lax.dot_general(a, b, (((1,), (0,)), ((), ())), precision=prec, preferred_element_type=F32)


def _dot_nt(a, b, prec=None):
    return lax.dot_general(a, b, (((1,), (1,)), ((), ())), precision=prec, preferred_element_type=F32)


def _dot_tn(a, b, prec=None):
    return lax.dot_general(a, b, (((0,), (0,)), ((), ())), precision=prec, preferred_element_type=F32)


def _bf(x):
    return x.astype(BF16)


def _mm(a, b, *, name, ta=False, tb=False, out_dtype=F32, a_fn=None, epi=None, extras=(),
        b_koff=0, tm=512, tn=512, tk=512):
    m = a.shape[1] if ta else a.shape[0]
    kdim = a.shape[0] if ta else a.shape[1]
    n = b.shape[0] if tb else b.shape[1]
    tm, tn, tk = min(tm, m), min(tn, n), min(tk, kdim)
    nk = kdim // tk
    grid = (m // tm, n // tn, nk)
    koff = b_koff // tk
    a_spec = pl.BlockSpec((tk, tm), lambda i, j, k: (k, i)) if ta else pl.BlockSpec((tm, tk), lambda i, j, k: (i, k))
    if tb:
        b_spec = pl.BlockSpec((tn, tk), lambda i, j, k: (j, k + koff))
    else:
        b_spec = pl.BlockSpec((tk, tn), lambda i, j, k: (k + koff, j))
    e_specs = [pl.BlockSpec((tm, tn), lambda i, j, k: (i, j)) for _ in extras]
    n_e = len(extras)
    dims = (((0 if ta else 1,), (1 if tb else 0,)), ((), ()))

    def body(a_ref, b_ref, *rest):
        e_refs, o_ref, acc = rest[:n_e], rest[n_e], rest[n_e + 1]
        k = pl.program_id(2)

        @pl.when(k == 0)
        def _():
            acc[...] = jnp.zeros_like(acc)

        av = a_ref[...]
        if a_fn is not None:
            av = a_fn(av)
        acc[...] += lax.dot_general(_bf(av), _bf(b_ref[...]), dims, preferred_element_type=F32)

        @pl.when(k == nk - 1)
        def _():
            r = acc[...]
            if epi is not None:
                r = epi(r, *[e[...] for e in e_refs])
            o_ref[...] = r.astype(out_dtype)

    return pl.pallas_call(
        body, name=name, grid=grid,
        in_specs=[a_spec, b_spec] + e_specs,
        out_specs=pl.BlockSpec((tm, tn), lambda i, j, k: (i, j)),
        out_shape=jax.ShapeDtypeStruct((m, n), out_dtype),
        scratch_shapes=[pltpu.VMEM((tm, tn), F32)],
        compiler_params=_cparams(("parallel", "parallel", "arbitrary")),
    )(a, b, *extras)


def _relu2(x):
    r = jnp.maximum(x, 0.0)
    return r * r


ROWS = 512


def _rms_fwd(x, g, name):
    n, d = x.shape

    def body(x_ref, g_ref, u_ref):
        xv = x_ref[...]
        r = lax.rsqrt(jnp.mean(xv * xv, axis=1, keepdims=True) + EPS)
        u_ref[...] = _bf(xv * r * g_ref[...])

    return pl.pallas_call(
        body, name=name, grid=(n // ROWS,),
        in_specs=[pl.BlockSpec((ROWS, d), lambda i: (i, 0)), pl.BlockSpec((1, d), lambda i: (0, 0))],
        out_specs=pl.BlockSpec((ROWS, d), lambda i: (i, 0)),
        out_shape=jax.ShapeDtypeStruct((n, d), BF16),
        compiler_params=_cparams(("parallel",)),
    )(x, g)


def _rms_bwd(dy, x, g, dres, name):
    n, d = x.shape

    def body(dy_ref, x_ref, g_ref, dres_ref, dx_ref, dg_ref):
        i = pl.program_id(0)
        xv, dyv = x_ref[...], dy_ref[...]
        r = lax.rsqrt(jnp.mean(xv * xv, axis=1, keepdims=True) + EPS)
        gy = dyv * g_ref[...]
        s = jnp.sum(gy * xv, axis=1, keepdims=True)
        dx_ref[...] = dres_ref[...] + r * gy - xv * (r * r * r * (1.0 / d)) * s

        @pl.when(i == 0)
        def _():
            dg_ref[...] = jnp.zeros_like(dg_ref)

        dg_ref[...] += jnp.sum(dyv * xv * r, axis=0, keepdims=True)

    row = pl.BlockSpec((ROWS, d), lambda i: (i, 0))
    vec = pl.BlockSpec((1, d), lambda i: (0, 0))
    return pl.pallas_call(
        body, name=name, grid=(n // ROWS,),
        in_specs=[row, row, vec, row], out_specs=[row, vec],
        out_shape=[jax.ShapeDtypeStruct((n, d), F32), jax.ShapeDtypeStruct((1, d), F32)],
        compiler_params=_cparams(("arbitrary",)),
    )(dy, x, g, dres)


def _merge_fwd(ya, yb, gate):
    n, d = ya.shape

    def body(ya_ref, yb_ref, ga_ref, gb_ref, o_ref):
        o_ref[...] = _bf(_sigmoid(ga_ref[...]) * ya_ref[...] + _sigmoid(gb_ref[...]) * yb_ref[...])

    row = pl.BlockSpec((ROWS, d), lambda i: (i, 0))
    return pl.pallas_call(
        body, name="merge_fwd", grid=(n // ROWS,),
        in_specs=[row, row, row, pl.BlockSpec((ROWS, d), lambda i: (i, 1))], out_specs=row,
        out_shape=jax.ShapeDtypeStruct((n, d), BF16),
        compiler_params=_cparams(("parallel",)),
    )(ya, yb, gate, gate)


def _merge_bwd(dm, ya, yb, gate):
    n, d = ya.shape

    def body(dm_ref, ya_ref, yb_ref, ga_ref, gb_ref, dya_ref, dyb_ref, dga_ref, dgb_ref):
        dmv = dm_ref[...]
        sa, sb = _sigmoid(ga_ref[...]), _sigmoid(gb_ref[...])
        dya_ref[...] = _bf(dmv * sa)
        dyb_ref[...] = _bf(dmv * sb)
        dga_ref[...] = _bf(dmv * ya_ref[...] * sa * (1.0 - sa))
        dgb_ref[...] = _bf(dmv * yb_ref[...] * sb * (1.0 - sb))

    row = pl.BlockSpec((ROWS, d), lambda i: (i, 0))
    o = jax.ShapeDtypeStruct((n, d), BF16)
    return pl.pallas_call(
        body, name="merge_bwd", grid=(n // ROWS,),
        in_specs=[row, row, row, row, pl.BlockSpec((ROWS, d), lambda i: (i, 1))], out_specs=[row] * 4,
        out_shape=[o] * 4,
        compiler_params=_cparams(("parallel",)),
    )(dm, ya, yb, gate, gate)


def _loss_bwd(out, target):
    n, d = out.shape

    def body(o_ref, t_ref, d_ref, l_ref):
        i = pl.program_id(0)
        err = o_ref[...] - t_ref[...]
        d_ref[...] = err * (1.0 / d)

        @pl.when(i == 0)
        def _():
            l_ref[...] = jnp.zeros_like(l_ref)

        l_ref[...] += 0.5 * jnp.sum(jnp.mean(err * err, axis=1, keepdims=True), axis=0, keepdims=True)

    row = pl.BlockSpec((ROWS, d), lambda i: (i, 0))
    return pl.pallas_call(
        body, name="loss_bwd", grid=(n // ROWS,),
        in_specs=[row, row], out_specs=[row, pl.BlockSpec((8, LANES), lambda i: (0, 0))],
        out_shape=[jax.ShapeDtypeStruct((n, d), F32), jax.ShapeDtypeStruct((8, LANES), F32)],
        compiler_params=_cparams(("arbitrary",)),
    )(out, target)


def _shift_down(x, s):
    row = lax.broadcasted_iota(jnp.int32, x.shape, 0)
    return jnp.where(row >= s, pltpu.roll(x, s, 0), 0.0)


def _shift_up(x, s):
    t = x.shape[0]
    row = lax.broadcasted_iota(jnp.int32, x.shape, 0)
    return jnp.where(row < t - s, pltpu.roll(x, t - s, 0), 0.0)


def _conv(x, w_ref):
    y = _shift_down(x, 3) * w_ref[0:1, :]
    y = y + _shift_down(x, 2) * w_ref[1:2, :]
    y = y + _shift_down(x, 1) * w_ref[2:3, :]
    return y + x * w_ref[3:4, :]


def _chunk_consts():
    r = lax.broadcasted_iota(jnp.int32, (CHUNK, CHUNK), 0)
    c = lax.broadcasted_iota(jnp.int32, (CHUNK, CHUNK), 1)
    incl, strict = r >= c, r > c
    return dict(incl=incl, strict=strict, trilf=incl.astype(F32), triuf=(r <= c).astype(F32),
                eye=(r == c).astype(F32))


def _inv_unit_lower(a, eye):
    p = -a
    r = eye + p
    for _ in range(5):
        p = _dot(p, p, HI)
        r = r + _dot(r, p, HI)
    return r


def _gdn_chunk_common(q, k, v, g128, g64, b128, b64, s, cs):
    big_g = _dot(cs["trilf"], g128, HI)
    gc = _dot(cs["trilf"], g64, HI)
    gr = _dot_tn(g64, cs["triuf"], HI)
    decay = jnp.where(cs["incl"], jnp.exp(jnp.where(cs["incl"], gc - gr, 0.0)), 0.0)
    kb, qb = _bf(k), _bf(q)
    kk = _dot_nt(kb, kb)
    qk = _dot_nt(qb, kb)
    a = jnp.where(cs["strict"], b64 * kk * decay, 0.0)
    tm = _inv_unit_lower(a, cs["eye"])
    e_g = jnp.exp(big_g)
    vb = v * b128
    kbeta = k * (b128 * e_g)
    w = _dot(tm, vb, HI)
    u = _dot(tm, kbeta, HI)
    p = qk * decay
    q_dec = q * e_g
    g_last = jnp.sum(g128, axis=0, keepdims=True)
    ekg = jnp.exp(g_last - big_g)
    k_dec = k * ekg
    dec = jnp.exp(g_last)
    sb = _bf(s)
    v_new = w - _dot(_bf(u), sb)
    return dict(decay=decay, kb=kb, qb=qb, kk=kk, qk=qk, tm=tm, e_g=e_g, vb=vb, kbeta=kbeta, u=u, p=p,
                q_dec=q_dec, ekg=ekg, k_dec=k_dec, dec=dec, sb=sb, v_new=v_new)


def _gdn_chunk_fwd(q, k, v, g128, g64, b128, b64, s, cs):
    f = _gdn_chunk_common(q, k, v, g128, g64, b128, b64, s, cs)
    vnb = _bf(f["v_new"])
    o = _dot(_bf(f["q_dec"]), f["sb"]) + _dot(_bf(f["p"]), vnb)
    s_new = s * f["dec"] + _dot_tn(_bf(f["k_dec"]), vnb)
    return o, s_new


def _gdn_chunk_bwd(q, k, v, g128, g64, b128, b64, s, do, ds_next, cs):
    f = _gdn_chunk_common(q, k, v, g128, g64, b128, b64, s, cs)
    decay, kb, qb, kk, qk, tm = f["decay"], f["kb"], f["qb"], f["kk"], f["qk"], f["tm"]
    e_g, vb, kbeta, u, sb = f["e_g"], f["vb"], f["kbeta"], f["u"], f["sb"]
    dob, dsb, vnb = _bf(do), _bf(ds_next), _bf(f["v_new"])
    dv_new = _dot_tn(_bf(f["p"]), dob) + _dot(_bf(f["k_dec"]), dsb)
    dvnb = _bf(dv_new)
    dp = jnp.where(cs["incl"], _dot_nt(dob, vnb), 0.0)
    dq_dec = _dot_nt(dob, sb)
    du = -_dot_nt(dvnb, sb)
    ds = _dot_tn(_bf(f["q_dec"]), dob) + f["dec"] * ds_next - _dot_tn(_bf(u), dvnb)
    ddec = jnp.sum(jnp.sum(s * ds_next, axis=1, keepdims=True), axis=0, keepdims=True)
    dk_dec = _dot_nt(vnb, dsb)
    dt = _dot_nt(dv_new, vb, HI) + _dot_nt(du, kbeta, HI)
    dvb = _dot_tn(tm, dv_new, HI)
    dkbeta = _dot_tn(tm, du, HI)
    da = jnp.where(cs["strict"], -_dot_tn(tm, _dot_nt(dt, tm, HI), HI), 0.0)
    dkk = _bf(da * b64 * decay)
    dqk = _bf(dp * decay)
    ddd = (da * b64 * kk + dp * qk) * decay
    dq = _dot(dqk, kb) + dq_dec * e_g
    dk = (_dot_tn(dqk, qb) + _dot(dkk, kb) + _dot_tn(dkk, kb) + dk_dec * f["ekg"] + dkbeta * (b128 * e_g))
    dv = dvb * b128
    dbeta = (jnp.sum(da * kk * decay, axis=1, keepdims=True) + jnp.sum(dvb * v, axis=1, keepdims=True)
             + jnp.sum(dkbeta * k * e_g, axis=1, keepdims=True))
    s_k = jnp.sum(dk_dec * f["k_dec"], axis=1, keepdims=True)
    dg_col = (jnp.sum(ddd, axis=1, keepdims=True) + jnp.sum(dq_dec * f["q_dec"], axis=1, keepdims=True) - s_k
              + jnp.sum(dkbeta * kbeta, axis=1, keepdims=True))
    colsum = _dot_tn(ddd, jnp.ones((CHUNK, LANES), F32), HI)
    dg_last = jnp.sum(s_k, axis=0, keepdims=True) + ddec * f["dec"]
    dg = _dot(cs["triuf"], dg_col - colsum, HI) + dg_last
    return dq, dk, dv, dg, dbeta, ds


def _stack_rows(vecs, nrows):
    row = lax.broadcasted_iota(jnp.int32, (nrows, LANES), 0)
    out = jnp.zeros((nrows, LANES), F32)
    for i, v in enumerate(vecs):
        out = out + jnp.where(row == i, jnp.broadcast_to(v, (nrows, LANES)), 0.0)
    return out


def _head_lane(x, lane_idx):
    lane = lax.broadcasted_iota(jnp.int32, x.shape, 1)
    return jnp.sum(jnp.where(lane == lane_idx, x, 0.0), axis=1, keepdims=True)


def _gdn_gates(ps, h, alog_ref, dtb_ref):
    ga = _head_lane(ps, LANE_GA + h)
    gb = _head_lane(ps, LANE_GB + h)
    a = jnp.exp(jnp.full((1, 1), alog_ref[0, h], F32))
    sp_in = ga + dtb_ref[0, h]
    g = -a * _softplus(sp_in)
    return g, _sigmoid(gb), a, sp_in


def _gdn_specs(b_loc, t):
    def col(off):
        return pl.BlockSpec((t, DH), lambda b, h: (b, off + h))

    ps_spec = pl.BlockSpec((t, LANES), lambda b, h: (b, 0))

    def wcol(off):
        return pl.BlockSpec((CONV_K, DH), lambda b, h: (0, off + h))

    smem = pl.BlockSpec(memory_space=pltpu.SMEM)
    vec = pl.BlockSpec((1, DH), lambda b, h: (0, 0))
    return col, ps_spec, wcol, smem, vec


def _gdn_fwd(pg, ps, convw, a_log, dt_bias, gnorm, b_loc, t):
    n = b_loc * t
    nc = t // CHUNK
    col, ps_spec, wcol, smem, vec = _gdn_specs(b_loc, t)

    def body(q_ref, k_ref, v_ref, z_ref, ps_ref, wq_ref, wk_ref, wv_ref, alog_ref, dtb_ref, gn_ref,
             oa_ref, oraw_ref, s_ref, qn, kn, vv, g128, g64, b128, b64):
        h = pl.program_id(1)
        g, beta, _, _ = _gdn_gates(ps_ref[...], h, alog_ref, dtb_ref)
        g128[...] = jnp.broadcast_to(g, (t, LANES))
        g64[...] = jnp.broadcast_to(g, (t, CHUNK))
        b128[...] = jnp.broadcast_to(beta, (t, LANES))
        b64[...] = jnp.broadcast_to(beta, (t, CHUNK))
        pq = _conv(q_ref[...], wq_ref)
        yq = pq * _sigmoid(pq)
        qn[...] = yq * (lax.rsqrt(jnp.sum(yq * yq, axis=1, keepdims=True) + EPS) * (DH ** -0.5))
        pk = _conv(k_ref[...], wk_ref)
        yk = pk * _sigmoid(pk)
        kn[...] = yk * lax.rsqrt(jnp.sum(yk * yk, axis=1, keepdims=True) + EPS)
        pv = _conv(v_ref[...], wv_ref)
        vv[...] = pv * _sigmoid(pv)
        cs = _chunk_consts()

        def chunk(i, s):
            r = pl.ds(pl.multiple_of(i * CHUNK, CHUNK), CHUNK)
            o, s_new = _gdn_chunk_fwd(qn[r, :], kn[r, :], vv[r, :], g128[r, :], g64[r, :], b128[r, :], b64[r, :], s, cs)
            s_ref[0, 0, i] = s
            oraw_ref[r, :] = o
            return s_new

        lax.fori_loop(0, nc, chunk, jnp.zeros((DH, DH), F32))
        o = oraw_ref[...]
        rr = lax.rsqrt(jnp.mean(o * o, axis=1, keepdims=True) + EPS)
        z = z_ref[...]
        oa_ref[...] = (o * rr * gn_ref[...]) * (z * _sigmoid(z))

    return pl.pallas_call(
        body, name="gdn_fwd", grid=(b_loc, HEADS),
        in_specs=[col(0), col(HEADS), col(2 * HEADS), col(3 * HEADS), ps_spec, wcol(0), wcol(HEADS), wcol(2 * HEADS),
                  smem, smem, vec],
        out_specs=[pl.BlockSpec((t, DH), lambda b, h: (b, h)), pl.BlockSpec((t, DH), lambda b, h: (b, h)),
                   pl.BlockSpec((1, 1, nc, DH, DH), lambda b, h: (b, h, 0, 0, 0))],
        out_shape=[jax.ShapeDtypeStruct((n, HEADS * DH), F32), jax.ShapeDtypeStruct((n, HEADS * DH), F32),
                   jax.ShapeDtypeStruct((b_loc, HEADS, nc, DH, DH), F32)],
        scratch_shapes=[pltpu.VMEM((t, DH), F32)] * 3 + [pltpu.VMEM((t, LANES), F32), pltpu.VMEM((t, CHUNK), F32)] * 2,
        compiler_params=_cparams(("arbitrary", "arbitrary")),
    )(pg, pg, pg, pg, ps, convw, convw, convw, a_log, dt_bias, gnorm)


def _gdn_bwd(pg, ps, convw, a_log, dt_bias, gnorm, d_oa, o_raw, s_all, b_loc, t):
    n = b_loc * t
    nc = t // CHUNK
    col, ps_spec, wcol, smem, vec = _gdn_specs(b_loc, t)

    def body(q_ref, k_ref, v_ref, z_ref, ps_ref, wq_ref, wk_ref, wv_ref, alog_ref, dtb_ref, gn_ref,
             doa_ref, oraw_ref, s_ref,
             dq_ref, dk_ref, dv_ref, dz_ref, dps_ref, dcw_ref, dsm_ref,
             qn, kn, vv, g128, g64, b128, b64, do_s, dqh, dkh, dvh, dg_s, db_s):
        b, h = pl.program_id(0), pl.program_id(1)
        g, beta, _, _ = _gdn_gates(ps_ref[...], h, alog_ref, dtb_ref)
        g128[...] = jnp.broadcast_to(g, (t, LANES))
        g64[...] = jnp.broadcast_to(g, (t, CHUNK))
        b128[...] = jnp.broadcast_to(beta, (t, LANES))
        b64[...] = jnp.broadcast_to(beta, (t, CHUNK))

        def prep(x_ref, w_ref):
            p = _conv(x_ref[...], w_ref)
            sg = _sigmoid(p)
            return p, sg, p * sg

        _, _, yq = prep(q_ref, wq_ref)
        qn[...] = yq * (lax.rsqrt(jnp.sum(yq * yq, axis=1, keepdims=True) + EPS) * (DH ** -0.5))
        _, _, yk = prep(k_ref, wk_ref)
        kn[...] = yk * lax.rsqrt(jnp.sum(yk * yk, axis=1, keepdims=True) + EPS)
        _, _, yv = prep(v_ref, wv_ref)
        vv[...] = yv

        o = oraw_ref[...]
        z = z_ref[...]
        doa = doa_ref[...]
        gn = gn_ref[...]
        ro = lax.rsqrt(jnp.mean(o * o, axis=1, keepdims=True) + EPS)
        sz = _sigmoid(z)
        dz_ref[...] = _bf(doa * (o * ro * gn) * (sz * (1.0 + z * (1.0 - sz))))
        dn = doa * (z * sz)
        dgn = jnp.sum(dn * o * ro, axis=0, keepdims=True)
        gy = dn * gn
        do_s[...] = ro * gy - o * (ro * ro * ro * (1.0 / DH)) * jnp.sum(gy * o, axis=1, keepdims=True)

        cs = _chunk_consts()

        def chunk(j, ds):
            i = nc - 1 - j
            r = pl.ds(pl.multiple_of(i * CHUNK, CHUNK), CHUNK)
            dq, dk, dv, dg, dbeta, ds_new = _gdn_chunk_bwd(
                qn[r, :], kn[r, :], vv[r, :], g128[r, :], g64[r, :], b128[r, :], b64[r, :], s_ref[0, 0, i],
                do_s[r, :], ds, cs)
            dqh[r, :] = dq
            dkh[r, :] = dk
            dvh[r, :] = dv
            dg_s[r, :] = dg
            db_s[r, :] = jnp.broadcast_to(dbeta, (CHUNK, LANES))
            return ds_new

        lax.fori_loop(0, nc, chunk, jnp.zeros((DH, DH), F32))

        g, beta, a, sp_in = _gdn_gates(ps_ref[...], h, alog_ref, dtb_ref)
        dg = dg_s[...]
        d_ga = dg * (-a) * _sigmoid(sp_in)
        d_alog = jnp.sum(dg * g, axis=0, keepdims=True)
        d_dtb = jnp.sum(d_ga, axis=0, keepdims=True)
        d_gb = db_s[...] * (beta * (1.0 - beta))
        lane = lax.broadcasted_iota(jnp.int32, (t, LANES), 1)
        contrib = jnp.where(lane == LANE_GA + h, d_ga, 0.0) + jnp.where(lane == LANE_GB + h, d_gb, 0.0)

        @pl.when(h == 0)
        def _():
            dps_ref[...] = jnp.zeros_like(dps_ref)

        dps_ref[...] += contrib

        lane1 = lax.broadcasted_iota(jnp.int32, (1, LANES), 1)
        small = _stack_rows([jnp.where(lane1 == h, d_alog, 0.0), jnp.where(lane1 == h, d_dtb, 0.0), dgn], 8)

        @pl.when((b == 0) & (h == 0))
        def _():
            dsm_ref[...] = jnp.zeros_like(dsm_ref)
            dcw_ref[...] = jnp.zeros_like(dcw_ref)

        dsm_ref[...] += small

        def conv_bwd(dp, x, w_ref, slot):
            dx = _shift_up(dp, 3) * w_ref[0:1, :]
            dx = dx + _shift_up(dp, 2) * w_ref[1:2, :]
            dx = dx + _shift_up(dp, 1) * w_ref[2:3, :]
            dx = dx + dp * w_ref[3:4, :]
            dw = _stack_rows([jnp.sum(dp * _shift_down(x, 3), axis=0, keepdims=True),
                              jnp.sum(dp * _shift_down(x, 2), axis=0, keepdims=True),
                              jnp.sum(dp * _shift_down(x, 1), axis=0, keepdims=True),
                              jnp.sum(dp * x, axis=0, keepdims=True)], CONV_K)
            dcw_ref[slot] += dw
            return dx

        def l2_bwd(dqn, y, c):
            r = lax.rsqrt(jnp.sum(y * y, axis=1, keepdims=True) + EPS)
            s1 = jnp.sum(dqn * y, axis=1, keepdims=True)
            return c * r * dqn - (c * r * r * r) * s1 * y

        def silu_bwd(p, sg):
            return sg * (1.0 + p * (1.0 - sg))

        pq, sq, yq = prep(q_ref, wq_ref)
        dq_ref[...] = _bf(conv_bwd(l2_bwd(dqh[...], yq, DH ** -0.5) * silu_bwd(pq, sq), q_ref[...], wq_ref, h))
        pk, sk, yk = prep(k_ref, wk_ref)
        dk_ref[...] = _bf(conv_bwd(l2_bwd(dkh[...], yk, 1.0) * silu_bwd(pk, sk), k_ref[...], wk_ref, HEADS + h))
        pv, sv, _ = prep(v_ref, wv_ref)
        dv_ref[...] = _bf(conv_bwd(dvh[...] * silu_bwd(pv, sv), v_ref[...], wv_ref, 2 * HEADS + h))

    blk = pl.BlockSpec((t, DH), lambda b, h: (b, h))
    ob = jax.ShapeDtypeStruct((n, HEADS * DH), BF16)
    return pl.pallas_call(
        body, name="gdn_bwd", grid=(b_loc, HEADS),
        in_specs=[col(0), col(HEADS), col(2 * HEADS), col(3 * HEADS), ps_spec, wcol(0), wcol(HEADS), wcol(2 * HEADS),
                  smem, smem, vec, blk, blk, pl.BlockSpec((1, 1, nc, DH, DH), lambda b, h: (b, h, 0, 0, 0))],
        out_specs=[blk, blk, blk, blk, ps_spec,
                   pl.BlockSpec((3 * HEADS, CONV_K, DH), lambda b, h: (0, 0, 0)),
                   pl.BlockSpec((8, LANES), lambda b, h: (0, 0))],
        out_shape=[ob, ob, ob, ob, jax.ShapeDtypeStruct((n, LANES), F32),
                   jax.ShapeDtypeStruct((3 * HEADS, CONV_K, DH), F32), jax.ShapeDtypeStruct((8, LANES), F32)],
        scratch_shapes=([pltpu.VMEM((t, DH), F32)] * 3 + [pltpu.VMEM((t, LANES), F32), pltpu.VMEM((t, CHUNK), F32)] * 2
                        + [pltpu.VMEM((t, DH), F32)] * 6),
        compiler_params=_cparams(("arbitrary", "arbitrary")),
    )(pg, pg, pg, pg, ps, convw, convw, convw, a_log, dt_bias, gnorm, d_oa, o_raw, s_all)


def _fox_prologue(q_ref, k_ref, v_ref, ps_ref, fb_ref, gq_ref, gk_ref, h, t, qs, ks, vs, ccol, crow):
    nb = t // FOX_BLOCK
    q, k = q_ref[...], k_ref[...]
    rq = lax.rsqrt(jnp.mean(q * q, axis=1, keepdims=True) + EPS)
    rk = lax.rsqrt(jnp.mean(k * k, axis=1, keepdims=True) + EPS)
    qs[...] = _bf(q * rq * gq_ref[...])
    ks[...] = _bf(k * rk * gk_ref[...])
    vs[...] = _bf(v_ref[...])
    f_in = _head_lane(ps_ref[...], LANE_FF + h) + fb_ref[0, h]
    ccol[...] = jnp.broadcast_to(-_softplus(-f_in), (t, LANES))
    r = lax.broadcasted_iota(jnp.int32, (FOX_BLOCK, FOX_BLOCK), 0)
    c = lax.broadcasted_iota(jnp.int32, (FOX_BLOCK, FOX_BLOCK), 1)
    trilf, triuf = (r >= c).astype(F32), (r <= c).astype(F32)

    def blk(j, carry):
        rows = pl.ds(pl.multiple_of(j * FOX_BLOCK, FOX_BLOCK), FOX_BLOCK)
        lf = ccol[rows, :]
        ccol[rows, :] = _dot(trilf, lf, HI) + carry
        crow[rows, :] = _dot_tn(lf, triuf, HI) + carry
        return carry + jnp.sum(lf, axis=0, keepdims=True)

    lax.fori_loop(0, nb, blk, jnp.zeros((1, LANES), F32))
    return rq, rk, f_in


def _fox_scores(qi, kj, cc, cr, i, j):
    s = _dot_nt(qi, kj) * (DH ** -0.5) + cc - cr
    r = lax.broadcasted_iota(jnp.int32, (FOX_BLOCK, FOX_BLOCK), 0)
    c = lax.broadcasted_iota(jnp.int32, (FOX_BLOCK, FOX_BLOCK), 1)
    return jnp.where(i * FOX_BLOCK + r >= j * FOX_BLOCK + c, s, NEG)


def _fox_specs(t):
    def col(off):
        return pl.BlockSpec((t, DH), lambda b, h: (b, off + h))

    ps_spec = pl.BlockSpec((t, LANES), lambda b, h: (b, 0))
    smem = pl.BlockSpec(memory_space=pltpu.SMEM)
    vec = pl.BlockSpec((1, DH), lambda b, h: (0, 0))
    blk = pl.BlockSpec((t, DH), lambda b, h: (b, h))
    return col, ps_spec, smem, vec, blk


def _fox_fwd(pf, ps, f_bias, gq, gk, b_loc, t):
    n = b_loc * t
    nb = t // FOX_BLOCK
    col, ps_spec, smem, vec, blk = _fox_specs(t)

    def body(q_ref, k_ref, v_ref, ps_ref, fb_ref, gq_ref, gk_ref, o_ref, lse_ref, qs, ks, vs, ccol, crow):
        h = pl.program_id(1)
        _fox_prologue(q_ref, k_ref, v_ref, ps_ref, fb_ref, gq_ref, gk_ref, h, t, qs, ks, vs, ccol, crow)

        def qblock(i, _):
            ri = pl.ds(pl.multiple_of(i * FOX_BLOCK, FOX_BLOCK), FOX_BLOCK)
            qi, cc = qs[ri, :], ccol[ri, :]

            def kblock(j, carry):
                m, l, acc = carry
                rj = pl.ds(pl.multiple_of(j * FOX_BLOCK, FOX_BLOCK), FOX_BLOCK)
                s = _fox_scores(qi, ks[rj, :], cc, crow[rj, :], i, j)
                m_new = jnp.maximum(m, jnp.max(s, axis=1, keepdims=True))
                p = jnp.exp(s - m_new)
                alpha = jnp.exp(m - m_new)
                l = alpha * l + jnp.sum(p, axis=1, keepdims=True)
                acc = alpha * acc + _dot(_bf(p), vs[rj, :])
                return m_new, l, acc

            m, l, acc = lax.fori_loop(0, i + 1, kblock, (jnp.full((FOX_BLOCK, 1), NEG, F32),
                                                         jnp.zeros((FOX_BLOCK, 1), F32),
                                                         jnp.zeros((FOX_BLOCK, DH), F32)))
            o_ref[ri, :] = acc / l
            lse_ref[ri, :] = jnp.broadcast_to(m + jnp.log(l), (FOX_BLOCK, LANES))
            return 0

        lax.fori_loop(0, nb, qblock, 0)

    o = jax.ShapeDtypeStruct((n, HEADS * DH), F32)
    return pl.pallas_call(
        body, name="fox_fwd", grid=(b_loc, HEADS),
        in_specs=[col(0), col(HEADS), col(2 * HEADS), ps_spec, smem, vec, vec],
        out_specs=[blk, blk], out_shape=[o, o],
        scratch_shapes=[pltpu.VMEM((t, DH), BF16)] * 3 + [pltpu.VMEM((t, LANES), F32)] * 2,
        compiler_params=_cparams(("arbitrary", "arbitrary")),
    )(pf, pf, pf, ps, f_bias, gq, gk)


def _fox_bwd(pf, ps, f_bias, gq, gk, d_ob, ob, lse, dps_in, b_loc, t):
    n = b_loc * t
    nb = t // FOX_BLOCK
    scale = DH ** -0.5
    col, ps_spec, smem, vec, blk = _fox_specs(t)

    def body(q_ref, k_ref, v_ref, ps_ref, fb_ref, gq_ref, gk_ref, do_ref, o_ref, lse_ref, dpsi_ref,
             dq_ref, dk_ref, dv_ref, dps_ref, dsm_ref, qs, ks, vs, ccol, crow, dos, dl, dqa, dcr, dcq):
        b, h = pl.program_id(0), pl.program_id(1)
        rq, _, f_in = _fox_prologue(q_ref, k_ref, v_ref, ps_ref, fb_ref, gq_ref, gk_ref, h, t, qs, ks, vs, ccol, crow)
        dov = do_ref[...]
        dos[...] = _bf(dov)
        dl[...] = jnp.broadcast_to(jnp.sum(dov * o_ref[...], axis=1, keepdims=True), (t, LANES))
        dqa[...] = jnp.zeros_like(dqa)
        dcq[...] = jnp.zeros_like(dcq)
        gkv = gk_ref[...]

        def kblock(j, dgk):
            rj = pl.ds(pl.multiple_of(j * FOX_BLOCK, FOX_BLOCK), FOX_BLOCK)
            kj, vj, cr = ks[rj, :], vs[rj, :], crow[rj, :]

            def qblock(i, carry):
                dk_acc, dv_acc, dc = carry
                ri = pl.ds(pl.multiple_of(i * FOX_BLOCK, FOX_BLOCK), FOX_BLOCK)
                qi, doi = qs[ri, :], dos[ri, :]
                s = _fox_scores(qi, kj, ccol[ri, :], cr, i, j)
                p = jnp.exp(s - lse_ref[ri, :])
                ds = p * (_dot_nt(doi, vj) - dl[ri, :])
                dsb = _bf(ds)
                dqa[ri, :] += _dot(dsb, kj)
                dcq[ri, :] += jnp.broadcast_to(jnp.sum(ds, axis=1, keepdims=True), (FOX_BLOCK, LANES))
                return (dk_acc + _dot_tn(dsb, qi), dv_acc + _dot_tn(_bf(p), doi),
                        dc - jnp.sum(ds, axis=0, keepdims=True))

            z = jnp.zeros((FOX_BLOCK, DH), F32)
            dk_acc, dv_acc, dc = lax.fori_loop(j, nb, qblock, (z, z, jnp.zeros((1, LANES), F32)))
            dv_ref[rj, :] = _bf(dv_acc)
            dcr[pl.ds(pl.multiple_of(j * 8, 8), 8), :] = jnp.broadcast_to(dc, (8, LANES))
            kraw = k_ref[rj, :]
            rk = lax.rsqrt(jnp.mean(kraw * kraw, axis=1, keepdims=True) + EPS)
            dkn = dk_acc * scale
            gy = dkn * gkv
            dk_ref[rj, :] = _bf(rk * gy - kraw * (rk * rk * rk * (1.0 / DH)) * jnp.sum(gy * kraw, axis=1, keepdims=True))
            return dgk + jnp.sum(dkn * kraw * rk, axis=0, keepdims=True)

        dgk = lax.fori_loop(0, nb, kblock, jnp.zeros((1, DH), F32))

        q = q_ref[...]
        dqn = dqa[...] * scale
        gy = dqn * gq_ref[...]
        dq_ref[...] = _bf(rq * gy - q * (rq * rq * rq * (1.0 / DH)) * jnp.sum(gy * q, axis=1, keepdims=True))
        dgq = jnp.sum(dqn * q * rq, axis=0, keepdims=True)

        r = lax.broadcasted_iota(jnp.int32, (FOX_BLOCK, FOX_BLOCK), 0)
        c = lax.broadcasted_iota(jnp.int32, (FOX_BLOCK, FOX_BLOCK), 1)
        triuf = (r <= c).astype(F32)

        def rev(jj, carry):
            j = nb - 1 - jj
            rows = pl.ds(pl.multiple_of(j * FOX_BLOCK, FOX_BLOCK), FOX_BLOCK)
            rowv = dcr[pl.ds(pl.multiple_of(j * 8, 8), 1), :]
            colv = jnp.sum(jnp.where(c >= r, jnp.broadcast_to(rowv, (FOX_BLOCK, LANES)), 0.0), axis=1, keepdims=True)
            qcol = dcq[rows, :]
            dl[rows, :] = colv + _dot(triuf, qcol, HI) + carry
            return carry + jnp.sum(rowv, axis=1, keepdims=True) + jnp.sum(qcol, axis=0, keepdims=True)

        lax.fori_loop(0, nb, rev, jnp.zeros((1, LANES), F32))
        d_ff = dl[...] * _sigmoid(-f_in)
        lane = lax.broadcasted_iota(jnp.int32, (t, LANES), 1)

        @pl.when(h == 0)
        def _():
            dps_ref[...] = dpsi_ref[...]

        dps_ref[...] += jnp.where(lane == LANE_FF + h, d_ff, 0.0)

        lane1 = lax.broadcasted_iota(jnp.int32, (1, LANES), 1)
        d_fb = jnp.sum(d_ff, axis=0, keepdims=True)
        small = _stack_rows([dgq, dgk, jnp.where(lane1 == h, d_fb, 0.0)], 8)

        @pl.when((b == 0) & (h == 0))
        def _():
            dsm_ref[...] = jnp.zeros_like(dsm_ref)

        dsm_ref[...] += small

    ob_ = jax.ShapeDtypeStruct((n, HEADS * DH), BF16)
    return pl.pallas_call(
        body, name="fox_bwd", grid=(b_loc, HEADS),
        in_specs=[col(0), col(HEADS), col(2 * HEADS), ps_spec, smem, vec, vec, blk, blk, blk, ps_spec],
        out_specs=[blk, blk, blk, ps_spec, pl.BlockSpec((8, LANES), lambda b, h: (0, 0))],
        out_shape=[ob_, ob_, ob_, jax.ShapeDtypeStruct((n, LANES), F32), jax.ShapeDtypeStruct((8, LANES), F32)],
        scratch_shapes=([pltpu.VMEM((t, DH), BF16)] * 3 + [pltpu.VMEM((t, LANES), F32)] * 2
                        + [pltpu.VMEM((t, DH), BF16), pltpu.VMEM((t, LANES), F32), pltpu.VMEM((t, DH), F32),
                           pltpu.VMEM((8 * nb, LANES), F32), pltpu.VMEM((t, LANES), F32)]),
        compiler_params=_cparams(("arbitrary", "arbitrary")),
    )(pf, pf, pf, ps, f_bias, gq, gk, d_ob, ob, lse, dps_in)


def _local_step(x, target, w, b_loc, t):
    xf = x
    u = _rms_fwd(xf, w["norm_mix_g"], "rms_mix")
    pg = _mm(u, w["w_gdn"], name="proj_gdn", tk=1024)
    pf = _mm(u, w["w_fox"], name="proj_fox", tk=1024)
    pgate = _mm(u, w["w_gate"], name="proj_gate", tk=1024)
    ps = _mm(u, w["w_small"], name="proj_small", tk=1024)
    oa, o_raw, s_all = _gdn_fwd(pg, ps, w["conv_w"], w["a_log"], w["dt_bias"], w["gdn_norm_g"], b_loc, t)
    ob, lse = _fox_fwd(pf, ps, w["f_bias"], w["fox_q_norm_g"], w["fox_k_norm_g"], b_loc, t)
    ya = _mm(oa, w["w_proj_gdn"], name="proj_a", tk=1024)
    yb = _mm(ob, w["w_proj_fox"], name="proj_b", tk=1024)
    merged = _merge_fwd(ya, yb, pgate)
    h = _mm(merged, w["w_out"], name="proj_out", tk=1024, epi=lambda acc, xr: acc + xr, extras=(xf,))
    hn = _rms_fwd(h, w["norm_mlp_g"], "rms_mlp")
    up = _mm(hn, w["w_up"], name="mlp_up", tk=1024)
    out = _mm(up, w["w_down"], name="mlp_down", a_fn=_relu2, tk=1024, epi=lambda acc, hr: acc + hr, extras=(h,))
    d_out, loss_blk = _loss_bwd(out, target)

    g = {}
    g["w_down"] = _mm(up, d_out, name="dw_down", ta=True, a_fn=_relu2, out_dtype=BF16)
    d_up = _mm(d_out, w["w_down"], name="d_up", tb=True, tk=1024, out_dtype=BF16,
               epi=lambda acc, upr: acc * (2.0 * jnp.maximum(upr, 0.0)), extras=(up,))
    g["w_up"] = _mm(hn, d_up, name="dw_up", ta=True, out_dtype=BF16)
    d_hn = _mm(d_up, w["w_up"], name="d_hn", tb=True, tk=1024)
    dh, g["norm_mlp_g"] = _rms_bwd(d_hn, h, w["norm_mlp_g"], d_out, "rms_mlp_bwd")
    g["w_out"] = _mm(merged, dh, name="dw_out", ta=True, out_dtype=BF16)
    dm = _mm(dh, w["w_out"], name="d_merged", tb=True, tk=1024)
    dya, dyb, dgate_a, dgate_b = _merge_bwd(dm, ya, yb, pgate)
    g["w_proj_gdn"] = _mm(oa, dya, name="dw_proj_a", ta=True, out_dtype=BF16)
    g["w_proj_fox"] = _mm(ob, dyb, name="dw_proj_b", ta=True, out_dtype=BF16)
    d_oa = _mm(dya, w["w_proj_gdn"], name="d_oa", tb=True, tk=1024)
    d_ob = _mm(dyb, w["w_proj_fox"], name="d_ob", tb=True, tk=1024)
    dgq, dgk, dgv, dgz, dps, dcw, gdn_small = _gdn_bwd(pg, ps, w["conv_w"], w["a_log"], w["dt_bias"], w["gdn_norm_g"],
                                                       d_oa, o_raw, s_all, b_loc, t)
    dfq, dfk, dfv, dps, fox_small = _fox_bwd(pf, ps, w["f_bias"], w["fox_q_norm_g"], w["fox_k_norm_g"],
                                             d_ob, ob, lse, dps, b_loc, t)
    segs = [(dgq, "w_gdn", 0), (dgk, "w_gdn", 1024), (dgv, "w_gdn", 2048), (dgz, "w_gdn", 3072),
            (dfq, "w_fox", 0), (dfk, "w_fox", 1024), (dfv, "w_fox", 2048),
            (dgate_a, "w_gate", 0), (dgate_b, "w_gate", 1024)]
    du = _mm(dps, w["w_small"], name="du_small", tb=True, tk=LANES)
    dws = [_mm(u, dps, name="dw_small", ta=True, out_dtype=BF16)]
    for idx, (dseg, wname, off) in enumerate(segs):
        du = _mm(dseg, w[wname], name=f"du_{idx}", tb=True, tk=1024, b_koff=off,
                 epi=lambda acc, prev: acc + prev, extras=(du,))
        dws.append(_mm(u, dseg, name=f"dw_in_{idx}", ta=True, out_dtype=BF16))
    g["w_in_parts"] = dws
    grad_x, g["norm_mix_g"] = _rms_bwd(du, xf, w["norm_mix_g"], dh, "rms_mix_bwd")
    g["conv"] = dcw
    g["gdn_small"] = gdn_small
    g["fox_small"] = fox_small
    return loss_blk, grad_x, g


def _position():
    x, y, c = lax.axis_index("x"), lax.axis_index("y"), lax.axis_index("c")
    return x, y, c


def _to_bf16(arrs, name):
    n = len(arrs)

    def body(*refs):
        for i in range(n):
            refs[n + i][...] = _bf(refs[i][...])

    return pl.pallas_call(
        body, name=name,
        out_shape=[jax.ShapeDtypeStruct(a.shape, BF16) for a in arrs],
        compiler_params=_cparams(),
    )(*arrs)


def _all_gather(arrs, name):
    n = len(arrs)
    hbm = pl.BlockSpec(memory_space=pl.ANY)

    def body(*refs):
        ins, outs = refs[:n], refs[n:2 * n]
        send, recv, loc = refs[2 * n:]
        x, y, c = _position()
        me = 4 * x + 2 * y + c
        sibling = (x, y, 1 - c)
        chips = [(1 - x, y), (x, 1 - y), (1 - x, 1 - y)]

        def idx(px, py, pc):
            return 4 * px + 2 * py + pc

        def cp(a, k, block, to, src=None):
            return pltpu.make_async_remote_copy(
                src_ref=outs[a].at[block] if src is None else src, dst_ref=outs[a].at[block],
                send_sem=send.at[a, k], recv_sem=recv.at[a, k], device_id=to, device_id_type=MESH)

        mine = [pltpu.make_async_copy(ins[a], outs[a].at[me], loc.at[a]) for a in range(n)]
        for m in mine:
            m.start()
        first = []
        for a in range(n):
            first.append(cp(a, 0, me, sibling, src=ins[a]))
            first += [cp(a, 1 + j, me, (*chip, c), src=ins[a]) for j, chip in enumerate(chips)]
        for f in first:
            f.start()
        passed = []
        for j, chip in enumerate(chips):
            for a in range(n):
                cp(a, 1 + j, idx(*chip, c), (x, y, c)).wait_recv()
                p = cp(a, 4 + j, idx(*chip, c), sibling)
                p.start()
                passed.append(p)
        for a in range(n):
            cp(a, 0, idx(x, y, 1 - c), (x, y, c)).wait_recv()
            for j, chip in enumerate(chips):
                cp(a, 4 + j, idx(*chip, 1 - c), (x, y, c)).wait_recv()
        for f in first + passed:
            f.wait_send()
        for m in mine:
            m.wait()

    return pl.pallas_call(
        body, name=name,
        in_specs=[hbm] * n, out_specs=[hbm] * n,
        out_shape=[jax.ShapeDtypeStruct((N_DEV,) + a.shape, a.dtype) for a in arrs],
        scratch_shapes=[pltpu.SemaphoreType.DMA((n, 7)), pltpu.SemaphoreType.DMA((n, 7)), pltpu.SemaphoreType.DMA((n,))],
        compiler_params=pltpu.CompilerParams(has_side_effects=True),
    )(*arrs)


def _peer(x, y, c, rel):
    return ((1 - x) if rel & 4 else x, (1 - y) if rel & 2 else y, (1 - c) if rel & 1 else c)


def _exchange(arrs, name):
    n = len(arrs)
    hbm = pl.BlockSpec(memory_space=pl.ANY)

    def body(*refs):
        ins, outs = refs[:n], refs[n:2 * n]
        send, recv, loc = refs[2 * n:]
        x, y, c = _position()
        me = 4 * x + 2 * y + c
        mine = [pltpu.make_async_copy(ins[a].at[me], outs[a].at[me], loc.at[a]) for a in range(n)]
        for m in mine:
            m.start()
        copies = []
        for rel in range(1, N_DEV):
            px, py, pc = _peer(x, y, c, rel)
            for a in range(n):
                copies.append(pltpu.make_async_remote_copy(
                    src_ref=ins[a].at[4 * px + 2 * py + pc], dst_ref=outs[a].at[me],
                    send_sem=send.at[a, rel - 1], recv_sem=recv.at[a, rel - 1],
                    device_id=(px, py, pc), device_id_type=MESH))
        for cpy in copies:
            cpy.start()
        for cpy in copies:
            cpy.wait()
        for m in mine:
            m.wait()

    return pl.pallas_call(
        body, name=name,
        in_specs=[hbm] * n, out_specs=[hbm] * n,
        out_shape=[jax.ShapeDtypeStruct(a.shape, a.dtype) for a in arrs],
        scratch_shapes=[pltpu.SemaphoreType.DMA((n, 7)), pltpu.SemaphoreType.DMA((n, 7)), pltpu.SemaphoreType.DMA((n,))],
        compiler_params=pltpu.CompilerParams(has_side_effects=True),
    )(*arrs)


def _all_reduce_small(buf, name):
    rows = buf.shape[0]

    def body(in_ref, out_ref, slots, send, recv):
        x, y, c = _position()
        me = 4 * x + 2 * y + c
        slots[me] = in_ref[...]
        copies = []
        for rel in range(1, N_DEV):
            copies.append(pltpu.make_async_remote_copy(
                src_ref=in_ref, dst_ref=slots.at[me], send_sem=send.at[rel - 1], recv_sem=recv.at[rel - 1],
                device_id=_peer(x, y, c, rel), device_id_type=MESH))
        for cpy in copies:
            cpy.start()
        for cpy in copies:
            cpy.wait()
        tot = slots[0]
        for d in range(1, N_DEV):
            tot = tot + slots[d]
        out_ref[...] = tot

    return pl.pallas_call(
        body, name=name,
        out_shape=jax.ShapeDtypeStruct((rows, LANES), F32),
        in_specs=[pl.BlockSpec(memory_space=pltpu.VMEM)], out_specs=pl.BlockSpec(memory_space=pltpu.VMEM),
        scratch_shapes=[pltpu.VMEM((N_DEV, rows, LANES), F32), pltpu.SemaphoreType.DMA((7,)),
                        pltpu.SemaphoreType.DMA((7,))],
        compiler_params=pltpu.CompilerParams(has_side_effects=True),
    )(buf)


def _adam_math(g, w, m, v):
    m = ADAM_B1 * m + (1.0 - ADAM_B1) * g
    v = ADAM_B2 * v + (1.0 - ADAM_B2) * (g * g)
    m_hat = m / (1.0 - ADAM_B1 ** ADAM_STEP)
    v_hat = v / (1.0 - ADAM_B2 ** ADAM_STEP)
    delta = -ADAM_LR * (m_hat / (jnp.sqrt(v_hat) + ADAM_EPS) + ADAM_WD * w)
    return delta, m, v


def _adam_shard(parts, w, m, v, name):
    r, c = w.shape
    tr = min(r, 128)

    def body(p_ref, w_ref, m_ref, v_ref, g_ref, d_ref, nm_ref, nv_ref):
        g = p_ref[0].astype(F32)
        for s in range(1, N_DEV):
            g = g + p_ref[s].astype(F32)
        d, nm, nv = _adam_math(g, w_ref[...], m_ref[...], v_ref[...])
        g_ref[...] = g
        d_ref[...] = d
        nm_ref[...] = nm
        nv_ref[...] = nv

    row = pl.BlockSpec((tr, c), lambda i: (i, 0))
    o = jax.ShapeDtypeStruct((r, c), F32)
    return pl.pallas_call(
        body, name=name, grid=(r // tr,),
        in_specs=[pl.BlockSpec((N_DEV, tr, c), lambda i: (0, i, 0)), row, row, row],
        out_specs=[row] * 4, out_shape=[o] * 4,
        compiler_params=_cparams(("parallel",)),
    )(parts, w, m, v)


def _adam_small(g, w, m, v):
    def body(g_ref, w_ref, m_ref, v_ref, d_ref, nm_ref, nv_ref):
        d, nm, nv = _adam_math(g_ref[...], w_ref[...], m_ref[...], v_ref[...])
        d_ref[...] = d
        nm_ref[...] = nm
        nv_ref[...] = nv

    o = jax.ShapeDtypeStruct(g.shape, F32)
    return pl.pallas_call(body, name="adam_small", out_shape=[o] * 3, compiler_params=_cparams())(g, w, m, v)


def _split_w_in(w_full):
    o = IN_OFF
    w_gdn = w_full[:, o["gq"]:o["ga"]]
    w_fox = w_full[:, o["fq"]:o["ff"]]
    w_gate = w_full[:, o["gate_a"]:o["end"]]
    w_small = jnp.concatenate([w_full[:, o["ga"]:o["fq"]], w_full[:, o["ff"]:o["gate_a"]],
                               jnp.zeros((w_full.shape[0], LANES - 24), w_full.dtype)], axis=1)
    return w_gdn, w_fox, w_gate, w_small


def _join_w_in(parts):
    small = parts[0]
    return jnp.concatenate(parts[1:5] + [small[:, 0:16]] + parts[5:8] + [small[:, 16:24]] + parts[8:10], axis=1)


def _rows128(a, rows):
    flat = a.reshape(-1)
    flat = jnp.concatenate([flat, jnp.zeros((rows * LANES - flat.shape[0],), flat.dtype)])
    return flat.reshape(rows, LANES)


def kernel(x, norm_mix_g, w_in, gdn_conv_w, gdn_a_log, gdn_dt_bias, gdn_norm_g, fox_q_norm_g, fox_k_norm_g, fox_f_bias, w_proj_gdn, w_proj_fox, w_out, norm_mlp_g, w_up, w_down, loss_target, m_norm_mix_g, m_w_in, m_gdn_conv_w, m_gdn_a_log, m_gdn_dt_bias, m_gdn_norm_g, m_fox_q_norm_g, m_fox_k_norm_g, m_fox_f_bias, m_w_proj_gdn, m_w_proj_fox, m_w_out, m_norm_mlp_g, m_w_up, m_w_down, v_norm_mix_g, v_w_in, v_gdn_conv_w, v_gdn_a_log, v_gdn_dt_bias, v_gdn_norm_g, v_fox_q_norm_g, v_fox_k_norm_g, v_fox_f_bias, v_w_proj_gdn, v_w_proj_fox, v_w_out, v_norm_mlp_g, v_w_up, v_w_down):
    b_loc, t, d = x.shape
    n = b_loc * t
    me = 4 * lax.axis_index("x") + 2 * lax.axis_index("y") + lax.axis_index("c")

    big = [w_in[0], w_proj_gdn[0], w_proj_fox[0], w_out[0], w_up[0], w_down[0]]
    big16 = _to_bf16(big, "weights_to_bf16")
    gathered = _all_gather(list(big16) + [gdn_conv_w[0]], "gather_weights")
    g_in, g_pa, g_pb, g_out, g_up, g_down, g_conv = gathered
    w_full = g_in.transpose(1, 0, 2).reshape(d, N_DEV * w_in.shape[2])
    w_gdn, w_fox, w_gate, w_small = _split_w_in(w_full)
    weights = {
        "w_gdn": w_gdn, "w_fox": w_fox, "w_gate": w_gate, "w_small": w_small,
        "w_proj_gdn": g_pa.reshape(d, d), "w_proj_fox": g_pb.reshape(d, d), "w_out": g_out.reshape(d, d),
        "w_up": g_up.transpose(1, 0, 2).reshape(d, D_FF), "w_down": g_down.reshape(D_FF, d),
        "conv_w": g_conv.transpose(1, 0, 2).reshape(CONV_K, 3 * d),
        "norm_mix_g": norm_mix_g, "norm_mlp_g": norm_mlp_g, "a_log": gdn_a_log, "dt_bias": gdn_dt_bias,
        "gdn_norm_g": gdn_norm_g, "fox_q_norm_g": fox_q_norm_g, "fox_k_norm_g": fox_k_norm_g, "f_bias": fox_f_bias,
    }

    loss_blk, grad_x, g = _local_step(x.reshape(n, d), loss_target.reshape(n, d), weights, b_loc, t)

    c_in = w_in.shape[2]
    gw_in = _join_w_in(g["w_in_parts"]).reshape(d, N_DEV, c_in).transpose(1, 0, 2)
    c_up = w_up.shape[2]
    send = [gw_in, g["w_proj_gdn"].reshape(N_DEV, d // N_DEV, d), g["w_proj_fox"].reshape(N_DEV, d // N_DEV, d),
            g["w_out"].reshape(N_DEV, d // N_DEV, d), g["w_up"].reshape(d, N_DEV, c_up).transpose(1, 0, 2),
            g["w_down"].reshape(N_DEV, D_FF // N_DEV, d)]
    recv = _exchange(send, "exchange_grads")
    shards = [(w_in, m_w_in, v_w_in), (w_proj_gdn, m_w_proj_gdn, v_w_proj_gdn), (w_proj_fox, m_w_proj_fox, v_w_proj_fox),
              (w_out, m_w_out, v_w_out), (w_up, m_w_up, v_w_up), (w_down, m_w_down, v_w_down)]
    big_out = []
    for i, (parts, (wi, mi, vi)) in enumerate(zip(recv, shards)):
        res = _adam_shard(parts, wi[0], mi[0], vi[0], f"adam_{i}")
        big_out.append([r[None] for r in res])

    conv_rows = CONV_K * 3 * d // LANES
    conv_g = g["conv"].transpose(1, 0, 2).reshape(conv_rows, LANES)
    buf = jnp.concatenate([conv_g, g["norm_mix_g"].reshape(8, LANES), g["norm_mlp_g"].reshape(8, LANES),
                           g["gdn_small"], g["fox_small"], loss_blk], axis=0)
    tot = _all_reduce_small(buf, "all_reduce_small")
    o = conv_rows
    conv_full = tot[0:o].reshape(CONV_K, 3 * d)
    c_conv = gdn_conv_w.shape[2]
    g_conv_shard = lax.dynamic_slice(conv_full, (0, me * c_conv), (CONV_K, c_conv))
    g_mix = tot[o:o + 8].reshape(1, d)
    g_mlp = tot[o + 8:o + 16].reshape(1, d)
    gs, fs = tot[o + 16:o + 24], tot[o + 24:o + 32]
    loss = tot[o + 32, 0]
    small_g = [g_mix, g_conv_shard[None], gs[0:1, 0:HEADS], gs[1:2, 0:HEADS], gs[2:3], fs[0:1], fs[1:2], fs[2:3, 0:HEADS],
               g_mlp]
    small_w = [norm_mix_g, gdn_conv_w, gdn_a_log, gdn_dt_bias, gdn_norm_g, fox_q_norm_g, fox_k_norm_g, fox_f_bias,
               norm_mlp_g]
    small_m = [m_norm_mix_g, m_gdn_conv_w, m_gdn_a_log, m_gdn_dt_bias, m_gdn_norm_g, m_fox_q_norm_g, m_fox_k_norm_g,
               m_fox_f_bias, m_norm_mlp_g]
    small_v = [v_norm_mix_g, v_gdn_conv_w, v_gdn_a_log, v_gdn_dt_bias, v_gdn_norm_g, v_fox_q_norm_g, v_fox_k_norm_g,
               v_fox_f_bias, v_norm_mlp_g]
    row_counts = [-(-a.size // (8 * LANES)) * 8 for a in small_w]

    def pack(arrs):
        return jnp.concatenate([_rows128(a, rc) for a, rc in zip(arrs, row_counts)], axis=0)

    sd, sm, sv = _adam_small(pack(small_g), pack(small_w), pack(small_m), pack(small_v))

    def unpack(p):
        outs, r0 = [], 0
        for a, rc in zip(small_w, row_counts):
            outs.append(p[r0:r0 + rc].reshape(-1)[:a.size].reshape(a.shape))
            r0 += rc
        return outs

    small_out = [small_g_i.reshape(w_i.shape) for small_g_i, w_i in zip(small_g, small_w)], unpack(sd), unpack(sm), unpack(sv)

    def ordered(kind):
        s = small_out[kind]
        bo = [b[kind] for b in big_out]
        return [s[0], bo[0], s[1], s[2], s[3], s[4], s[5], s[6], s[7], bo[1], bo[2], bo[3], s[8], bo[4], bo[5]]

    return (loss, grad_x.reshape(b_loc, t, d), *ordered(0), *ordered(1), *ordered(2), *ordered(3))
```

```python
import functools

import jax
import jax.numpy as jnp
from jax import lax
from jax.experimental import pallas as pl
from jax.experimental.pallas import tpu as pltpu

F32 = jnp.float32
BF16 = jnp.bfloat16
HI = lax.Precision.HIGHEST
MESH = pl.DeviceIdType.MESH

N_DEV = 8
D_MODEL = 1024
HEADS = 8
DH = 128
CONV_K = 4
CHUNK = 64
GDN_GROUP = 4
FOX_BLOCK = 128
FOX_TILE = 512
D_FF = 4 * D_MODEL
EPS = 1e-6
LANES = 128
NEG = -1e30
IN_OFF = {"gq": 0, "gk": 1024, "gv": 2048, "gz": 3072, "ga": 4096, "gb": 4104, "fq": 4112, "fk": 5136,
          "fv": 6160, "ff": 7184, "gate_a": 7192, "gate_b": 8216, "end": 9240}
LANE_GA, LANE_GB, LANE_FF = 0, 8, 16

ADAM_LR = 0.001
ADAM_B1 = 0.9
ADAM_B2 = 0.999
ADAM_EPS = 1e-08
ADAM_WD = 0.01
ADAM_STEP = 10

VMEM_LIMIT = 56 * 1024 * 1024


def _cparams(sem=None):
    return pltpu.CompilerParams(dimension_semantics=sem, vmem_limit_bytes=VMEM_LIMIT)


def _sigmoid(x):
    return 1.0 / (1.0 + jnp.exp(-x))


def _softplus(x):
    return jnp.maximum(x, 0.0) + jnp.log(1.0 + jnp.exp(-jnp.abs(x)))


def _dot(a, b, prec=None):
    return lax.dot_general(a, b, (((1,), (0,)), ((), ())), precision=prec, preferred_element_type=F32)


def _dot_nt(a, b, prec=None):
    return lax.dot_general(a, b, (((1,), (1,)), ((), ())), precision=prec, preferred_element_type=F32)


def _dot_tn(a, b, prec=None):
    return lax.dot_general(a, b, (((0,), (0,)), ((), ())), precision=prec, preferred_element_type=F32)


def _bf(x):
    return x.astype(BF16)


def _mm(a, b, *, name, ta=False, tb=False, out_dtype=F32, a_fn=None, epi=None, extras=(),
        b_koff=0, tm=512, tn=512, tk=512):
    m = a.shape[1] if ta else a.shape[0]
    kdim = a.shape[0] if ta else a.shape[1]
    n = b.shape[0] if tb else b.shape[1]
    tm, tn, tk = min(tm, m), min(tn, n), min(tk, kdim)
    nk = kdim // tk
    grid = (m // tm, n // tn, nk)
    koff = b_koff // tk
    a_spec = pl.BlockSpec((tk, tm), lambda i, j, k: (k, i)) if ta else pl.BlockSpec((tm, tk), lambda i, j, k: (i, k))
    if tb:
        b_spec = pl.BlockSpec((tn, tk), lambda i, j, k: (j, k + koff))
    else:
        b_spec = pl.BlockSpec((tk, tn), lambda i, j, k: (k + koff, j))
    e_specs = [pl.BlockSpec((tm, tn), lambda i, j, k: (i, j)) for _ in extras]
    n_e = len(extras)
    dims = (((0 if ta else 1,), (1 if tb else 0,)), ((), ()))

    def body(a_ref, b_ref, *rest):
        e_refs, o_ref, acc = rest[:n_e], rest[n_e], rest[n_e + 1]
        k = pl.program_id(2)

        @pl.when(k == 0)
        def _():
            acc[...] = jnp.zeros_like(acc)

        av = a_ref[...]
        if a_fn is not None:
            av = a_fn(av)
        acc[...] += lax.dot_general(_bf(av), _bf(b_ref[...]), dims, preferred_element_type=F32)

        @pl.when(k == nk - 1)
        def _():
            r = acc[...]
            if epi is not None:
                r = epi(r, *[e[...] for e in e_refs])
            o_ref[...] = r.astype(out_dtype)

    return pl.pallas_call(
        body, name=name, grid=grid,
        in_specs=[a_spec, b_spec] + e_specs,
        out_specs=pl.BlockSpec((tm, tn), lambda i, j, k: (i, j)),
        out_shape=jax.ShapeDtypeStruct((m, n), out_dtype),
        scratch_shapes=[pltpu.VMEM((tm, tn), F32)],
        compiler_params=_cparams(("parallel", "parallel", "arbitrary")),
    )(a, b, *extras)


def _relu2(x):
    r = jnp.maximum(x, 0.0)
    return r * r


ROWS = 512


def _rms_fwd(x, g, name):
    n, d = x.shape

    def body(x_ref, g_ref, u_ref):
        xv = x_ref[...]
        r = lax.rsqrt(jnp.mean(xv * xv, axis=1, keepdims=True) + EPS)
        u_ref[...] = _bf(xv * r * g_ref[...])

    return pl.pallas_call(
        body, name=name, grid=(n // ROWS,),
        in_specs=[pl.BlockSpec((ROWS, d), lambda i: (i, 0)), pl.BlockSpec((1, d), lambda i: (0, 0))],
        out_specs=pl.BlockSpec((ROWS, d), lambda i: (i, 0)),
        out_shape=jax.ShapeDtypeStruct((n, d), BF16),
        compiler_params=_cparams(("parallel",)),
    )(x, g)


def _rms_bwd(dy, x, g, dres, name):
    n, d = x.shape

    def body(dy_ref, x_ref, g_ref, dres_ref, dx_ref, dg_ref):
        i = pl.program_id(0)
        xv, dyv = x_ref[...], dy_ref[...]
        r = lax.rsqrt(jnp.mean(xv * xv, axis=1, keepdims=True) + EPS)
        gy = dyv * g_ref[...]
        s = jnp.sum(gy * xv, axis=1, keepdims=True)
        dx_ref[...] = dres_ref[...] + r * gy - xv * (r * r * r * (1.0 / d)) * s

        @pl.when(i == 0)
        def _():
            dg_ref[...] = jnp.zeros_like(dg_ref)

        dg_ref[...] += jnp.sum(dyv * xv * r, axis=0, keepdims=True)

    row = pl.BlockSpec((ROWS, d), lambda i: (i, 0))
    vec = pl.BlockSpec((1, d), lambda i: (0, 0))
    return pl.pallas_call(
        body, name=name, grid=(n // ROWS,),
        in_specs=[row, row, vec, row], out_specs=[row, vec],
        out_shape=[jax.ShapeDtypeStruct((n, d), F32), jax.ShapeDtypeStruct((1, d), F32)],
        compiler_params=_cparams(("arbitrary",)),
    )(dy, x, g, dres)


def _merge_fwd(ya, yb, gate):
    n, d = ya.shape

    def body(ya_ref, yb_ref, ga_ref, gb_ref, o_ref):
        o_ref[...] = _bf(_sigmoid(ga_ref[...]) * ya_ref[...] + _sigmoid(gb_ref[...]) * yb_ref[...])

    row = pl.BlockSpec((ROWS, d), lambda i: (i, 0))
    return pl.pallas_call(
        body, name="merge_fwd", grid=(n // ROWS,),
        in_specs=[row, row, row, pl.BlockSpec((ROWS, d), lambda i: (i, 1))], out_specs=row,
        out_shape=jax.ShapeDtypeStruct((n, d), BF16),
        compiler_params=_cparams(("parallel",)),
    )(ya, yb, gate, gate)


def _merge_bwd(dm, ya, yb, gate):
    n, d = ya.shape

    def body(dm_ref, ya_ref, yb_ref, ga_ref, gb_ref, dya_ref, dyb_ref, dga_ref, dgb_ref):
        dmv = dm_ref[...]
        sa, sb = _sigmoid(ga_ref[...]), _sigmoid(gb_ref[...])
        dya_ref[...] = _bf(dmv * sa)
        dyb_ref[...] = _bf(dmv * sb)
        dga_ref[...] = _bf(dmv * ya_ref[...] * sa * (1.0 - sa))
        dgb_ref[...] = _bf(dmv * yb_ref[...] * sb * (1.0 - sb))

    row = pl.BlockSpec((ROWS, d), lambda i: (i, 0))
    o = jax.ShapeDtypeStruct((n, d), BF16)
    return pl.pallas_call(
        body, name="merge_bwd", grid=(n // ROWS,),
        in_specs=[row, row, row, row, pl.BlockSpec((ROWS, d), lambda i: (i, 1))], out_specs=[row] * 4,
        out_shape=[o] * 4,
        compiler_params=_cparams(("parallel",)),
    )(dm, ya, yb, gate, gate)


def _loss_bwd(out, target):
    n, d = out.shape

    def body(o_ref, t_ref, d_ref, l_ref):
        i = pl.program_id(0)
        err = o_ref[...] - t_ref[...]
        d_ref[...] = err * (1.0 / d)

        @pl.when(i == 0)
        def _():
            l_ref[...] = jnp.zeros_like(l_ref)

        l_ref[...] += 0.5 * jnp.sum(jnp.mean(err * err, axis=1, keepdims=True), axis=0, keepdims=True)

    row = pl.BlockSpec((ROWS, d), lambda i: (i, 0))
    return pl.pallas_call(
        body, name="loss_bwd", grid=(n // ROWS,),
        in_specs=[row, row], out_specs=[row, pl.BlockSpec((8, LANES), lambda i: (0, 0))],
        out_shape=[jax.ShapeDtypeStruct((n, d), F32), jax.ShapeDtypeStruct((8, LANES), F32)],
        compiler_params=_cparams(("arbitrary",)),
    )(out, target)


PAD = 8


def _pad_zero(pad_ref):
    t = pad_ref.shape[0] - 2 * PAD
    pad_ref[0:PAD, :] = jnp.zeros((PAD, LANES), F32)
    pad_ref[PAD + t:2 * PAD + t, :] = jnp.zeros((PAD, LANES), F32)


def _shifted(pad_ref, s):
    t = pad_ref.shape[0] - 2 * PAD
    return pad_ref[PAD - s:PAD - s + t, :]


def _conv(x, w_ref, pad_ref):
    t = x.shape[0]
    pad_ref[PAD:PAD + t, :] = x
    y = _shifted(pad_ref, 3) * w_ref[0:1, :]
    y = y + _shifted(pad_ref, 2) * w_ref[1:2, :]
    y = y + _shifted(pad_ref, 1) * w_ref[2:3, :]
    return y + x * w_ref[3:4, :]


def _chunk_consts():
    r = lax.broadcasted_iota(jnp.int32, (CHUNK, CHUNK), 0)
    c = lax.broadcasted_iota(jnp.int32, (CHUNK, CHUNK), 1)
    incl, strict = r >= c, r > c
    return dict(incl=incl, strict=strict, trilf=incl.astype(F32), triuf=(r <= c).astype(F32),
                eye=(r == c).astype(F32))


class _V:
    def __init__(self, xs):
        self.xs = list(xs)

    def __add__(self, o):
        return _ap(lambda x, y: x + y, self, o)

    def __radd__(self, o):
        return _ap(lambda x, y: y + x, self, o)

    def __sub__(self, o):
        return _ap(lambda x, y: x - y, self, o)

    def __rsub__(self, o):
        return _ap(lambda x, y: y - x, self, o)

    def __mul__(self, o):
        return _ap(lambda x, y: x * y, self, o)

    def __rmul__(self, o):
        return _ap(lambda x, y: y * x, self, o)

    def __neg__(self):
        return _ap(lambda x: -x, self)

    def __getitem__(self, idx):
        return _ap(lambda x: x[idx], self)


def _ap(fn, *args):
    n = [len(a.xs) for a in args if isinstance(a, _V)]
    if not n:
        return fn(*args)
    return _V([fn(*[a.xs[i] if isinstance(a, _V) else a for a in args]) for i in range(n[0])])


def _vbf(x):
    return _ap(_bf, x)


def _vdot(a, b):
    return _ap(_dot, a, b)


def _vdot_nt(a, b):
    return _ap(_dot_nt, a, b)


def _vdot_tn(a, b):
    return _ap(_dot_tn, a, b)


def _vexp(x):
    return _ap(jnp.exp, x)


def _vsum(x, axis):
    return _ap(lambda v: jnp.sum(v, axis=axis, keepdims=True), x)


def _vcat(a, b, axis):
    return _ap(lambda x, y: jnp.concatenate([x, y], axis=axis), a, b)


def _vmask(mask, x):
    return _ap(lambda v: jnp.where(mask, v, 0.0), x)


def _split2(x):
    h = _vbf(x)
    return h, _vbf(x - _ap(lambda v: v.astype(F32), h))


def _dot3(a, b, kind=_vdot):
    ah, al = _split2(a)
    bh, bl = _split2(b)
    return kind(ah, bh) + (kind(ah, bl) + kind(al, bh))


def _split(x, terms):
    out = []
    for _ in range(terms):
        h = _vbf(x)
        out.append(h)
        x = x - _ap(lambda v: v.astype(F32), h)
    return out


def _dot_exact_l(m01, x, kind=_vdot, terms=2):
    mb = _bf(m01)
    parts = [kind(mb, xp) for xp in _split(x, terms)]
    return functools.reduce(lambda a, b: a + b, reversed(parts))


def _dot_exact_r(x, m01, kind=_vdot, terms=2):
    mb = _bf(m01)
    parts = [kind(xp, mb) for xp in _split(x, terms)]
    return functools.reduce(lambda a, b: a + b, reversed(parts))


def _inv_unit_lower(a, eye):
    p = -a
    r = p + eye
    p = _dot3(p, p)
    for j in range(1, 6):
        if j < 5:
            y = _dot3(p, _vcat(p, r, 1))
            p, r = y[:, 0:CHUNK], r + y[:, CHUNK:2 * CHUNK]
        else:
            r = r + _dot3(p, r)
    return r


def _gdn_chunk_pre(q, k, v, g128, g64, b128, b64, cs):
    incl = cs["incl"]
    big_g = _dot_exact_l(cs["trilf"], g128)
    gc = big_g[:, 0:CHUNK]
    gr = _dot_exact_r(g64, cs["triuf"], _vdot_tn)
    decay = _ap(lambda d: jnp.where(incl, jnp.exp(jnp.where(incl, d, 0.0)), 0.0), gc - gr)
    kb, qb = _vbf(k), _vbf(q)
    qkk = _vdot_nt(_vcat(qb, kb, 0), kb)
    qk, kk = qkk[0:CHUNK], qkk[CHUNK:2 * CHUNK]
    tm = _inv_unit_lower(_vmask(cs["strict"], b64 * kk * decay), cs["eye"])
    e_g = _vexp(big_g)
    wu = _dot3(tm, _vcat(v * b128, k * (b128 * e_g), 1))
    w, u = wu[:, 0:DH], wu[:, DH:2 * DH]
    g_last = _vsum(g128, 0)
    return dict(big_g=big_g, decay=decay, kk=kk, qk=qk, tm=tm, w=w, u=u, p=qk * decay, q_dec=q * e_g,
                k_dec=k * _vexp(g_last - big_g), dec=_vexp(g_last))


def _gdn_chunk_post(q, k, v, g128, b128, b64, s, ds_next, do, dv_new, big_g, decay, kk, qk, tm, u, v_new, cs):
    e_g = _vexp(big_g)
    vb = v * b128
    kbeta = k * (b128 * e_g)
    q_dec = q * e_g
    g_last = _vsum(g128, 0)
    ekg = _vexp(g_last - big_g)
    k_dec = k * ekg
    dec = _vexp(g_last)
    kb, qb, sb = _vbf(k), _vbf(q), _vbf(s)
    dob, dsb, vnb, dvnb = _vbf(do), _vbf(ds_next), _vbf(v_new), _vbf(dv_new)
    dp = _vmask(cs["incl"], _vdot_nt(dob, vnb))
    dq_dec = _vdot_nt(dob, sb)
    du = -_vdot_nt(dvnb, sb)
    ddec = _vsum(_vsum(s * ds_next, 1), 0)
    dk_dec = _vdot_nt(vnb, dsb)
    dwu = _vcat(dv_new, du, 1)
    dt = _dot3(dwu, _vcat(vb, kbeta, 1), _vdot_nt)
    dvk = _dot3(tm, dwu, _vdot_tn)
    dvb, dkbeta = dvk[:, 0:DH], dvk[:, DH:2 * DH]
    da = _vmask(cs["strict"], -_dot3(tm, _dot3(dt, tm, _vdot_nt), _vdot_tn))
    dkk = _vbf(da * b64 * decay)
    dqk = _vbf(dp * decay)
    ddd = (da * b64 * kk + dp * qk) * decay
    dq = _vdot(dqk, kb) + dq_dec * e_g
    dk = _vdot_tn(dqk, qb) + _vdot(dkk, kb) + _vdot_tn(dkk, kb) + dk_dec * ekg + dkbeta * (b128 * e_g)
    dv = dvb * b128
    dbeta = _vsum(da * kk * decay, 1) + _vsum(dvb * v, 1) + _vsum(dkbeta * k * e_g, 1)
    s_k = _vsum(dk_dec * k_dec, 1)
    dg_col = _vsum(ddd, 1) + _vsum(dq_dec * q_dec, 1) - s_k + _vsum(dkbeta * kbeta, 1)
    colsum = _dot_exact_r(ddd, jnp.ones((CHUNK, LANES), F32), _vdot_tn)
    dg_last = _vsum(s_k, 0) + ddec * dec
    dg = _dot_exact_l(cs["triuf"], dg_col - colsum) + dg_last
    return dq, dk, dv, dg, dbeta


def _stack_rows(vecs, nrows):
    row = lax.broadcasted_iota(jnp.int32, (nrows, LANES), 0)
    out = jnp.zeros((nrows, LANES), F32)
    for i, v in enumerate(vecs):
        out = out + jnp.where(row == i, jnp.broadcast_to(v, (nrows, LANES)), 0.0)
    return out


def _head_lane(x, lane_idx):
    lane = lax.broadcasted_iota(jnp.int32, x.shape, 1)
    return jnp.sum(jnp.where(lane == lane_idx, x, 0.0), axis=1, keepdims=True)


def _gdn_gates(ps, h, alog_ref, dtb_ref):
    ga = _head_lane(ps, LANE_GA + h)
    gb = _head_lane(ps, LANE_GB + h)
    a = jnp.exp(jnp.full((1, 1), alog_ref[0, h], F32))
    sp_in = ga + dtb_ref[0, h]
    g = -a * _softplus(sp_in)
    return g, _sigmoid(gb), a, sp_in


def _gdn_specs(b_loc, t):
    def col(off):
        return pl.BlockSpec((t, DH), lambda b, h: (b, off + h))

    ps_spec = pl.BlockSpec((t, LANES), lambda b, h: (b, 0))

    def wcol(off):
        return pl.BlockSpec((CONV_K, DH), lambda b, h: (0, off + h))

    smem = pl.BlockSpec(memory_space=pltpu.SMEM)
    vec = pl.BlockSpec((1, DH), lambda b, h: (0, 0))
    return col, ps_spec, wcol, smem, vec


def _gdn_fwd(pg, ps, convw, a_log, dt_bias, gnorm, b_loc, t):
    n = b_loc * t
    nc = t // CHUNK
    col, ps_spec, wcol, smem, vec = _gdn_specs(b_loc, t)

    def body(q_ref, k_ref, v_ref, z_ref, ps_ref, wq_ref, wk_ref, wv_ref, alog_ref, dtb_ref, gn_ref,
             oa_ref, oraw_ref, s_ref, qn, kn, vv, g128, g64, b128, b64, uq_s, p_s, kd_s, dec_s, pad_s):
        h = pl.program_id(1)
        g, beta, _, _ = _gdn_gates(ps_ref[...], h, alog_ref, dtb_ref)
        g128[...] = jnp.broadcast_to(g, (t, LANES))
        g64[...] = jnp.broadcast_to(g, (t, CHUNK))
        b128[...] = jnp.broadcast_to(beta, (t, LANES))
        b64[...] = jnp.broadcast_to(beta, (t, CHUNK))
        _pad_zero(pad_s)
        pq = _conv(q_ref[...], wq_ref, pad_s)
        yq = pq * _sigmoid(pq)
        qn[...] = yq * (lax.rsqrt(jnp.sum(yq * yq, axis=1, keepdims=True) + EPS) * (DH ** -0.5))
        pk = _conv(k_ref[...], wk_ref, pad_s)
        yk = pk * _sigmoid(pk)
        kn[...] = yk * lax.rsqrt(jnp.sum(yk * yk, axis=1, keepdims=True) + EPS)
        pv = _conv(v_ref[...], wv_ref, pad_s)
        vv[...] = pv * _sigmoid(pv)
        cs = _chunk_consts()

        def pre_group(gi, _):
            idx = [gi * GDN_GROUP + c for c in range(GDN_GROUP)]
            rows = [pl.ds(pl.multiple_of(i * CHUNK, CHUNK), CHUNK) for i in idx]
            ins = [_V([ref[r, :] for r in rows]) for ref in (qn, kn, vv, g128, g64, b128, b64)]
            f = _gdn_chunk_pre(*ins, cs)
            for c, (i, r) in enumerate(zip(idx, rows)):
                vv[r, :] = f["w"].xs[c]
                uq_s[i, 0:CHUNK, :] = _bf(f["u"].xs[c])
                uq_s[i, CHUNK:2 * CHUNK, :] = _bf(f["q_dec"].xs[c])
                p_s[r, :] = _bf(f["p"].xs[c])
                kd_s[r, :] = _bf(f["k_dec"].xs[c])
                dec_s[pl.ds(pl.multiple_of(i * 8, 8), 8), :] = jnp.broadcast_to(f["dec"].xs[c], (8, LANES))
            return 0

        lax.fori_loop(0, nc // GDN_GROUP, pre_group, 0)

        def chunk(i, s):
            r = pl.ds(pl.multiple_of(i * CHUNK, CHUNK), CHUNK)
            us = _dot(uq_s[i], _bf(s))
            vnb = _bf(vv[r, :] - us[0:CHUNK])
            oraw_ref[r, :] = us[CHUNK:2 * CHUNK] + _dot(p_s[r, :], vnb)
            s_ref[0, 0, i] = s
            return s * dec_s[pl.ds(pl.multiple_of(i * 8, 8), 1), :] + _dot_tn(kd_s[r, :], vnb)

        lax.fori_loop(0, nc, chunk, jnp.zeros((DH, DH), F32))
        o = oraw_ref[...]
        rr = lax.rsqrt(jnp.mean(o * o, axis=1, keepdims=True) + EPS)
        z = z_ref[...]
        oa_ref[...] = (o * rr * gn_ref[...]) * (z * _sigmoid(z))

    return pl.pallas_call(
        body, name="gdn_fwd", grid=(b_loc, HEADS),
        in_specs=[col(0), col(HEADS), col(2 * HEADS), col(3 * HEADS), ps_spec, wcol(0), wcol(HEADS), wcol(2 * HEADS),
                  smem, smem, vec],
        out_specs=[pl.BlockSpec((t, DH), lambda b, h: (b, h)), pl.BlockSpec((t, DH), lambda b, h: (b, h)),
                   pl.BlockSpec((1, 1, nc, DH, DH), lambda b, h: (b, h, 0, 0, 0))],
        out_shape=[jax.ShapeDtypeStruct((n, HEADS * DH), F32), jax.ShapeDtypeStruct((n, HEADS * DH), F32),
                   jax.ShapeDtypeStruct((b_loc, HEADS, nc, DH, DH), F32)],
        scratch_shapes=([pltpu.VMEM((t, DH), F32)] * 3 + [pltpu.VMEM((t, LANES), F32), pltpu.VMEM((t, CHUNK), F32)] * 2
                        + [pltpu.VMEM((nc, 2 * CHUNK, DH), BF16), pltpu.VMEM((t, CHUNK), BF16), pltpu.VMEM((t, DH), BF16),
                           pltpu.VMEM((8 * nc, LANES), F32), pltpu.VMEM((t + 2 * PAD, LANES), F32)]),
        compiler_params=_cparams(("arbitrary", "arbitrary")),
    )(pg, pg, pg, pg, ps, convw, convw, convw, a_log, dt_bias, gnorm)


def _gdn_bwd(pg, ps, convw, a_log, dt_bias, gnorm, d_oa, o_raw, s_all, b_loc, t):
    n = b_loc * t
    nc = t // CHUNK
    col, ps_spec, wcol, smem, vec = _gdn_specs(b_loc, t)

    def body(q_ref, k_ref, v_ref, z_ref, ps_ref, wq_ref, wk_ref, wv_ref, alog_ref, dtb_ref, gn_ref,
             doa_ref, oraw_ref, s_ref,
             dq_ref, dk_ref, dv_ref, dz_ref, dps_ref, dcw_ref, dsm_ref,
             qn, kn, vv, g128, g64, b128, b64, do_s, bg_s, u_s, vn_s, dvn_s, dcy_s, kk_s, qk_s, tm_s, dsn_s, pad_s):
        b, h = pl.program_id(0), pl.program_id(1)
        g, beta, _, _ = _gdn_gates(ps_ref[...], h, alog_ref, dtb_ref)
        g128[...] = jnp.broadcast_to(g, (t, LANES))
        g64[...] = jnp.broadcast_to(g, (t, CHUNK))
        b128[...] = jnp.broadcast_to(beta, (t, LANES))
        b64[...] = jnp.broadcast_to(beta, (t, CHUNK))
        _pad_zero(pad_s)

        def prep(x_ref, w_ref):
            p = _conv(x_ref[...], w_ref, pad_s)
            sg = _sigmoid(p)
            return p, sg, p * sg

        _, _, yq = prep(q_ref, wq_ref)
        qn[...] = yq * (lax.rsqrt(jnp.sum(yq * yq, axis=1, keepdims=True) + EPS) * (DH ** -0.5))
        _, _, yk = prep(k_ref, wk_ref)
        kn[...] = yk * lax.rsqrt(jnp.sum(yk * yk, axis=1, keepdims=True) + EPS)
        _, _, yv = prep(v_ref, wv_ref)
        vv[...] = yv

        o = oraw_ref[...]
        z = z_ref[...]
        doa = doa_ref[...]
        gn = gn_ref[...]
        ro = lax.rsqrt(jnp.mean(o * o, axis=1, keepdims=True) + EPS)
        sz = _sigmoid(z)
        dz_ref[...] = _bf(doa * (o * ro * gn) * (sz * (1.0 + z * (1.0 - sz))))
        dn = doa * (z * sz)
        dgn = jnp.sum(dn * o * ro, axis=0, keepdims=True)
        gy = dn * gn
        do_s[...] = ro * gy - o * (ro * ro * ro * (1.0 / DH)) * jnp.sum(gy * o, axis=1, keepdims=True)

        cs = _chunk_consts()

        def pre_group(gi, _):
            idx = [gi * GDN_GROUP + c for c in range(GDN_GROUP)]
            rows = [pl.ds(pl.multiple_of(i * CHUNK, CHUNK), CHUNK) for i in idx]
            ins = [_V([ref[r, :] for r in rows]) for ref in (qn, kn, vv, g128, g64, b128, b64)]
            states = _V([_bf(s_ref[0, 0, i]) for i in idx])
            f = _gdn_chunk_pre(*ins, cs)
            v_new = f["w"] - _vdot(_vbf(f["u"]), states)
            for c, r in enumerate(rows):
                bg_s[r, :] = f["big_g"].xs[c]
                u_s[r, :] = f["u"].xs[c]
                vn_s[r, :] = v_new.xs[c]
                dcy_s[r, :] = f["decay"].xs[c]
                kk_s[r, :] = f["kk"].xs[c]
                qk_s[r, :] = f["qk"].xs[c]
                tm_s[r, :] = f["tm"].xs[c]
            return 0

        lax.fori_loop(0, nc // GDN_GROUP, pre_group, 0)

        def chunk(j, ds):
            i = nc - 1 - j
            r = pl.ds(pl.multiple_of(i * CHUNK, CHUNK), CHUNK)
            big_g = bg_s[r, :]
            g_last = jnp.sum(g128[r, :], axis=0, keepdims=True)
            dob = _bf(do_s[r, :])
            dv_new = (_dot_tn(_bf(qk_s[r, :] * dcy_s[r, :]), dob)
                      + _dot(_bf(kn[r, :] * jnp.exp(g_last - big_g)), _bf(ds)))
            dvn_s[r, :] = dv_new
            dsn_s[i] = ds
            return (_dot_tn(_bf(qn[r, :] * jnp.exp(big_g)), dob) + jnp.exp(g_last) * ds
                    - _dot_tn(_bf(u_s[r, :]), _bf(dv_new)))

        lax.fori_loop(0, nc, chunk, jnp.zeros((DH, DH), F32))

        def post_group(gi, _):
            idx = [gi * GDN_GROUP + c for c in range(GDN_GROUP)]
            rows = [pl.ds(pl.multiple_of(i * CHUNK, CHUNK), CHUNK) for i in idx]
            def rows_of(ref):
                return _V([ref[r, :] for r in rows])

            dq, dk, dv, dg, dbeta = _gdn_chunk_post(
                rows_of(qn), rows_of(kn), rows_of(vv), rows_of(g128), rows_of(b128), rows_of(b64),
                _V([s_ref[0, 0, i] for i in idx]), _V([dsn_s[i] for i in idx]), rows_of(do_s), rows_of(dvn_s),
                rows_of(bg_s), rows_of(dcy_s), rows_of(kk_s), rows_of(qk_s), rows_of(tm_s), rows_of(u_s), rows_of(vn_s),
                cs)
            for c, r in enumerate(rows):
                qn[r, :] = dq.xs[c]
                kn[r, :] = dk.xs[c]
                vv[r, :] = dv.xs[c]
                g128[r, :] = dg.xs[c]
                b128[r, :] = jnp.broadcast_to(dbeta.xs[c], (CHUNK, LANES))
            return 0

        lax.fori_loop(0, nc // GDN_GROUP, post_group, 0)
        dqh, dkh, dvh = qn, kn, vv

        g, beta, a, sp_in = _gdn_gates(ps_ref[...], h, alog_ref, dtb_ref)
        dg = g128[...]
        d_ga = dg * (-a) * _sigmoid(sp_in)
        d_alog = jnp.sum(dg * g, axis=0, keepdims=True)
        d_dtb = jnp.sum(d_ga, axis=0, keepdims=True)
        d_gb = b128[...] * (beta * (1.0 - beta))
        lane = lax.broadcasted_iota(jnp.int32, (t, LANES), 1)
        contrib = jnp.where(lane == LANE_GA + h, d_ga, 0.0) + jnp.where(lane == LANE_GB + h, d_gb, 0.0)

        @pl.when(h == 0)
        def _():
            dps_ref[...] = jnp.zeros_like(dps_ref)

        dps_ref[...] += contrib

        lane1 = lax.broadcasted_iota(jnp.int32, (1, LANES), 1)
        small = _stack_rows([jnp.where(lane1 == h, d_alog, 0.0), jnp.where(lane1 == h, d_dtb, 0.0), dgn], 8)

        @pl.when((b == 0) & (h == 0))
        def _():
            dsm_ref[...] = jnp.zeros_like(dsm_ref)
            dcw_ref[...] = jnp.zeros_like(dcw_ref)

        dsm_ref[...] += small

        def conv_bwd(dp, x, w_ref, slot):
            dw = _stack_rows([jnp.sum(dp * _shifted(pad_s, 3), axis=0, keepdims=True),
                              jnp.sum(dp * _shifted(pad_s, 2), axis=0, keepdims=True),
                              jnp.sum(dp * _shifted(pad_s, 1), axis=0, keepdims=True),
                              jnp.sum(dp * x, axis=0, keepdims=True)], CONV_K)
            dcw_ref[slot] += dw
            pad_s[PAD:PAD + t, :] = dp
            dx = _shifted(pad_s, -3) * w_ref[0:1, :]
            dx = dx + _shifted(pad_s, -2) * w_ref[1:2, :]
            dx = dx + _shifted(pad_s, -1) * w_ref[2:3, :]
            return dx + dp * w_ref[3:4, :]

        def l2_bwd(dqn, y, c):
            r = lax.rsqrt(jnp.sum(y * y, axis=1, keepdims=True) + EPS)
            s1 = jnp.sum(dqn * y, axis=1, keepdims=True)
            return c * r * dqn - (c * r * r * r) * s1 * y

        def silu_bwd(p, sg):
            return sg * (1.0 + p * (1.0 - sg))

        pq, sq, yq = prep(q_ref, wq_ref)
        dq_ref[...] = _bf(conv_bwd(l2_bwd(dqh[...], yq, DH ** -0.5) * silu_bwd(pq, sq), q_ref[...], wq_ref, h))
        pk, sk, yk = prep(k_ref, wk_ref)
        dk_ref[...] = _bf(conv_bwd(l2_bwd(dkh[...], yk, 1.0) * silu_bwd(pk, sk), k_ref[...], wk_ref, HEADS + h))
        pv, sv, _ = prep(v_ref, wv_ref)
        dv_ref[...] = _bf(conv_bwd(dvh[...] * silu_bwd(pv, sv), v_ref[...], wv_ref, 2 * HEADS + h))

    blk = pl.BlockSpec((t, DH), lambda b, h: (b, h))
    ob = jax.ShapeDtypeStruct((n, HEADS * DH), BF16)
    return pl.pallas_call(
        body, name="gdn_bwd", grid=(b_loc, HEADS),
        in_specs=[col(0), col(HEADS), col(2 * HEADS), col(3 * HEADS), ps_spec, wcol(0), wcol(HEADS), wcol(2 * HEADS),
                  smem, smem, vec, blk, blk, pl.BlockSpec((1, 1, nc, DH, DH), lambda b, h: (b, h, 0, 0, 0))],
        out_specs=[blk, blk, blk, blk, ps_spec,
                   pl.BlockSpec((3 * HEADS, CONV_K, DH), lambda b, h: (0, 0, 0)),
                   pl.BlockSpec((8, LANES), lambda b, h: (0, 0))],
        out_shape=[ob, ob, ob, ob, jax.ShapeDtypeStruct((n, LANES), F32),
                   jax.ShapeDtypeStruct((3 * HEADS, CONV_K, DH), F32), jax.ShapeDtypeStruct((8, LANES), F32)],
        scratch_shapes=([pltpu.VMEM((t, DH), F32)] * 3 + [pltpu.VMEM((t, LANES), F32), pltpu.VMEM((t, CHUNK), F32)] * 2
                        + [pltpu.VMEM((t, DH), F32)] * 5 + [pltpu.VMEM((t, CHUNK), F32)] * 4
                        + [pltpu.VMEM((nc, DH, DH), F32), pltpu.VMEM((t + 2 * PAD, LANES), F32)]),
        compiler_params=_cparams(("arbitrary", "arbitrary")),
    )(pg, pg, pg, pg, ps, convw, convw, convw, a_log, dt_bias, gnorm, d_oa, o_raw, s_all)


def _fox_prologue(q_ref, k_ref, v_ref, ps_ref, fb_ref, gq_ref, gk_ref, h, t, qs, ks, vs, ccol, crow):
    nb = t // FOX_BLOCK
    q, k = q_ref[...], k_ref[...]
    rq = lax.rsqrt(jnp.mean(q * q, axis=1, keepdims=True) + EPS)
    rk = lax.rsqrt(jnp.mean(k * k, axis=1, keepdims=True) + EPS)
    qs[...] = _bf(q * rq * gq_ref[...])
    ks[...] = _bf(k * rk * gk_ref[...])
    vs[...] = _bf(v_ref[...])
    f_in = _head_lane(ps_ref[...], LANE_FF + h) + fb_ref[0, h]
    ccol[...] = jnp.broadcast_to(-_softplus(-f_in), (t, LANES))
    r = lax.broadcasted_iota(jnp.int32, (FOX_BLOCK, FOX_BLOCK), 0)
    c = lax.broadcasted_iota(jnp.int32, (FOX_BLOCK, FOX_BLOCK), 1)
    trilf, triuf = (r >= c).astype(F32), (r <= c).astype(F32)
    blocks = [pl.ds(j * FOX_BLOCK, FOX_BLOCK) for j in range(nb)]
    lfs = _V([ccol[rb, :] for rb in blocks])
    cc = _dot_exact_l(trilf, lfs, terms=3)
    cr = _dot_exact_r(lfs, triuf, _vdot_tn, terms=3)
    sums = _vsum(lfs, 0)
    carry = jnp.zeros((1, LANES), F32)
    for j, rb in enumerate(blocks):
        ccol[rb, :] = cc.xs[j] + carry
        crow[j] = (cr.xs[j] + carry)[0:8]
        carry = carry + sums.xs[j]
    return rq, rk, f_in


def _fox_scores(q_rows, k_rows, cc, cr, row0, col0):
    s = _dot_nt(q_rows, k_rows) * (DH ** -0.5) + cc - cr
    r = lax.broadcasted_iota(jnp.int32, s.shape, 0)
    c = lax.broadcasted_iota(jnp.int32, s.shape, 1)
    return jnp.where(row0 + r >= col0 + c, s, NEG)


def _fox_specs(t):
    def col(off):
        return pl.BlockSpec((t, DH), lambda b, h: (b, off + h))

    ps_spec = pl.BlockSpec((t, LANES), lambda b, h: (b, 0))
    smem = pl.BlockSpec(memory_space=pltpu.SMEM)
    vec = pl.BlockSpec((1, DH), lambda b, h: (0, 0))
    blk = pl.BlockSpec((t, DH), lambda b, h: (b, h))
    return col, ps_spec, smem, vec, blk


def _fox_fwd(pf, ps, f_bias, gq, gk, b_loc, t):
    n = b_loc * t
    nb = t // FOX_BLOCK
    kt = min(FOX_TILE, t)
    nsub = kt // FOX_BLOCK
    col, ps_spec, smem, vec, blk = _fox_specs(t)

    def body(q_ref, k_ref, v_ref, ps_ref, fb_ref, gq_ref, gk_ref, o_ref, lse_ref, qs, ks, vs, ccol, crow):
        h = pl.program_id(1)
        _fox_prologue(q_ref, k_ref, v_ref, ps_ref, fb_ref, gq_ref, gk_ref, h, t, qs, ks, vs, ccol, crow)

        def qblock(i, _):
            ri = pl.ds(pl.multiple_of(i * FOX_BLOCK, FOX_BLOCK), FOX_BLOCK)
            qi = qs[ri, :]
            cc = jnp.concatenate([ccol[ri, :]] * nsub, axis=1)

            def ktile(j, carry):
                m, l, acc = carry
                rj = pl.ds(pl.multiple_of(j * kt, kt), kt)
                cr = jnp.concatenate([crow[j * nsub + u, 0:1, :] for u in range(nsub)], axis=1)
                s = _fox_scores(qi, ks[rj, :], cc, cr, i * FOX_BLOCK, j * kt)
                m_new = jnp.maximum(m, jnp.max(s, axis=1, keepdims=True))
                p = jnp.exp(s - m_new)
                alpha = jnp.exp(m - m_new)
                l = alpha * l + jnp.sum(p, axis=1, keepdims=True)
                acc = alpha * acc + _dot(_bf(p), vs[rj, :])
                return m_new, l, acc

            m, l, acc = lax.fori_loop(0, (i * FOX_BLOCK) // kt + 1, ktile, (jnp.full((FOX_BLOCK, 1), NEG, F32),
                                                                            jnp.zeros((FOX_BLOCK, 1), F32),
                                                                            jnp.zeros((FOX_BLOCK, DH), F32)))
            o_ref[ri, :] = acc / l
            lse_ref[ri, :] = jnp.broadcast_to(m + jnp.log(l), (FOX_BLOCK, LANES))
            return 0

        lax.fori_loop(0, nb, qblock, 0)

    o = jax.ShapeDtypeStruct((n, HEADS * DH), F32)
    return pl.pallas_call(
        body, name="fox_fwd", grid=(b_loc, HEADS),
        in_specs=[col(0), col(HEADS), col(2 * HEADS), ps_spec, smem, vec, vec],
        out_specs=[blk, blk], out_shape=[o, o],
        scratch_shapes=[pltpu.VMEM((t, DH), BF16)] * 3 + [pltpu.VMEM((t, LANES), F32), pltpu.VMEM((nb, 8, LANES), F32)],
        compiler_params=_cparams(("arbitrary", "arbitrary")),
    )(pf, pf, pf, ps, f_bias, gq, gk)


def _fox_bwd(pf, ps, f_bias, gq, gk, d_ob, ob, lse, dps_in, b_loc, t):
    n = b_loc * t
    nb = t // FOX_BLOCK
    qt = min(FOX_TILE, t)
    scale = DH ** -0.5
    col, ps_spec, smem, vec, blk = _fox_specs(t)

    def body(q_ref, k_ref, v_ref, ps_ref, fb_ref, gq_ref, gk_ref, do_ref, o_ref, lse_ref, dpsi_ref,
             dq_ref, dk_ref, dv_ref, dps_ref, dsm_ref, qs, ks, vs, ccol, crow, dos, dl, dqa, dcr, dcq):
        b, h = pl.program_id(0), pl.program_id(1)
        rq, _, f_in = _fox_prologue(q_ref, k_ref, v_ref, ps_ref, fb_ref, gq_ref, gk_ref, h, t, qs, ks, vs, ccol, crow)
        dov = do_ref[...]
        dos[...] = _bf(dov)
        dl[...] = jnp.broadcast_to(jnp.sum(dov * o_ref[...], axis=1, keepdims=True), (t, LANES))
        dqa[...] = jnp.zeros_like(dqa)
        dcq[...] = jnp.zeros_like(dcq)
        gkv = gk_ref[...]

        def kblock(j, dgk):
            rj = pl.ds(pl.multiple_of(j * FOX_BLOCK, FOX_BLOCK), FOX_BLOCK)
            kj, vj, cr = ks[rj, :], vs[rj, :], crow[j, 0:1, :]

            def qtile(i, carry):
                dk_acc, dv_acc, dc = carry
                ri = pl.ds(pl.multiple_of(i * qt, qt), qt)
                qi, doi = qs[ri, :], dos[ri, :]
                s = _fox_scores(qi, kj, ccol[ri, :], cr, i * qt, j * FOX_BLOCK)
                p = jnp.exp(s - lse_ref[ri, :])
                ds = p * (_dot_nt(doi, vj) - dl[ri, :])
                dsb = _bf(ds)
                dqa[ri, :] += _dot(dsb, kj)
                dcq[ri, :] += jnp.broadcast_to(jnp.sum(ds, axis=1, keepdims=True), (qt, LANES))
                return (dk_acc + _dot_tn(dsb, qi), dv_acc + _dot_tn(_bf(p), doi),
                        dc - jnp.sum(ds, axis=0, keepdims=True))

            z = jnp.zeros((FOX_BLOCK, DH), F32)
            dk_acc, dv_acc, dc = lax.fori_loop((j * FOX_BLOCK) // qt, t // qt, qtile,
                                               (z, z, jnp.zeros((1, LANES), F32)))
            dv_ref[rj, :] = _bf(dv_acc)
            dcr[pl.ds(pl.multiple_of(j * 8, 8), 8), :] = jnp.broadcast_to(dc, (8, LANES))
            kraw = k_ref[rj, :]
            rk = lax.rsqrt(jnp.mean(kraw * kraw, axis=1, keepdims=True) + EPS)
            dkn = dk_acc * scale
            gy = dkn * gkv
            dk_ref[rj, :] = _bf(rk * gy - kraw * (rk * rk * rk * (1.0 / DH)) * jnp.sum(gy * kraw, axis=1, keepdims=True))
            return dgk + jnp.sum(dkn * kraw * rk, axis=0, keepdims=True)

        dgk = lax.fori_loop(0, nb, kblock, jnp.zeros((1, DH), F32))

        q = q_ref[...]
        dqn = dqa[...] * scale
        gy = dqn * gq_ref[...]
        dq_ref[...] = _bf(rq * gy - q * (rq * rq * rq * (1.0 / DH)) * jnp.sum(gy * q, axis=1, keepdims=True))
        dgq = jnp.sum(dqn * q * rq, axis=0, keepdims=True)

        r = lax.broadcasted_iota(jnp.int32, (FOX_BLOCK, FOX_BLOCK), 0)
        c = lax.broadcasted_iota(jnp.int32, (FOX_BLOCK, FOX_BLOCK), 1)
        triuf = (r <= c).astype(F32)

        def rev(jj, carry):
            j = nb - 1 - jj
            rows = pl.ds(pl.multiple_of(j * FOX_BLOCK, FOX_BLOCK), FOX_BLOCK)
            rowv = dcr[pl.ds(pl.multiple_of(j * 8, 8), 1), :]
            colv = jnp.sum(jnp.where(c >= r, jnp.broadcast_to(rowv, (FOX_BLOCK, LANES)), 0.0), axis=1, keepdims=True)
            qcol = dcq[rows, :]
            dl[rows, :] = colv + _dot_exact_l(triuf, qcol, terms=3) + carry
            return carry + jnp.sum(rowv, axis=1, keepdims=True) + jnp.sum(qcol, axis=0, keepdims=True)

        lax.fori_loop(0, nb, rev, jnp.zeros((1, LANES), F32))
        d_ff = dl[...] * _sigmoid(-f_in)
        lane = lax.broadcasted_iota(jnp.int32, (t, LANES), 1)

        @pl.when(h == 0)
        def _():
            dps_ref[...] = dpsi_ref[...]

        dps_ref[...] += jnp.where(lane == LANE_FF + h, d_ff, 0.0)

        lane1 = lax.broadcasted_iota(jnp.int32, (1, LANES), 1)
        d_fb = jnp.sum(d_ff, axis=0, keepdims=True)
        small = _stack_rows([dgq, dgk, jnp.where(lane1 == h, d_fb, 0.0)], 8)

        @pl.when((b == 0) & (h == 0))
        def _():
            dsm_ref[...] = jnp.zeros_like(dsm_ref)

        dsm_ref[...] += small

    ob_ = jax.ShapeDtypeStruct((n, HEADS * DH), BF16)
    return pl.pallas_call(
        body, name="fox_bwd", grid=(b_loc, HEADS),
        in_specs=[col(0), col(HEADS), col(2 * HEADS), ps_spec, smem, vec, vec, blk, blk, blk, ps_spec],
        out_specs=[blk, blk, blk, ps_spec, pl.BlockSpec((8, LANES), lambda b, h: (0, 0))],
        out_shape=[ob_, ob_, ob_, jax.ShapeDtypeStruct((n, LANES), F32), jax.ShapeDtypeStruct((8, LANES), F32)],
        scratch_shapes=([pltpu.VMEM((t, DH), BF16)] * 3 + [pltpu.VMEM((t, LANES), F32), pltpu.VMEM((nb, 8, LANES), F32)]
                        + [pltpu.VMEM((t, DH), BF16), pltpu.VMEM((t, LANES), F32), pltpu.VMEM((t, DH), F32),
                           pltpu.VMEM((8 * nb, LANES), F32), pltpu.VMEM((t, LANES), F32)]),
        compiler_params=_cparams(("arbitrary", "arbitrary")),
    )(pf, pf, pf, ps, f_bias, gq, gk, d_ob, ob, lse, dps_in)


def _local_step(x, target, w, b_loc, t):
    xf = x
    u = _rms_fwd(xf, w["norm_mix_g"], "rms_mix")
    pg = _mm(u, w["w_gdn"], name="proj_gdn", tk=1024)
    pf = _mm(u, w["w_fox"], name="proj_fox", tk=1024)
    pgate = _mm(u, w["w_gate"], name="proj_gate", tk=1024)
    ps = _mm(u, w["w_small"], name="proj_small", tk=1024)
    oa, o_raw, s_all = _gdn_fwd(pg, ps, w["conv_w"], w["a_log"], w["dt_bias"], w["gdn_norm_g"], b_loc, t)
    ob, lse = _fox_fwd(pf, ps, w["f_bias"], w["fox_q_norm_g"], w["fox_k_norm_g"], b_loc, t)
    ya = _mm(oa, w["w_proj_gdn"], name="proj_a", tk=1024)
    yb = _mm(ob, w["w_proj_fox"], name="proj_b", tk=1024)
    merged = _merge_fwd(ya, yb, pgate)
    h = _mm(merged, w["w_out"], name="proj_out", tk=1024, epi=lambda acc, xr: acc + xr, extras=(xf,))
    hn = _rms_fwd(h, w["norm_mlp_g"], "rms_mlp")
    up = _mm(hn, w["w_up"], name="mlp_up", tk=1024)
    out = _mm(up, w["w_down"], name="mlp_down", a_fn=_relu2, tk=1024, epi=lambda acc, hr: acc + hr, extras=(h,))
    d_out, loss_blk = _loss_bwd(out, target)

    g = {}
    g["w_down"] = _mm(up, d_out, name="dw_down", ta=True, a_fn=_relu2, out_dtype=BF16)
    d_up = _mm(d_out, w["w_down"], name="d_up", tb=True, tk=1024, out_dtype=BF16,
               epi=lambda acc, upr: acc * (2.0 * jnp.maximum(upr, 0.0)), extras=(up,))
    g["w_up"] = _mm(hn, d_up, name="dw_up", ta=True, out_dtype=BF16)
    d_hn = _mm(d_up, w["w_up"], name="d_hn", tb=True, tk=1024)
    dh, g["norm_mlp_g"] = _rms_bwd(d_hn, h, w["norm_mlp_g"], d_out, "rms_mlp_bwd")
    g["w_out"] = _mm(merged, dh, name="dw_out", ta=True, out_dtype=BF16)
    dm = _mm(dh, w["w_out"], name="d_merged", tb=True, tk=1024)
    dya, dyb, dgate_a, dgate_b = _merge_bwd(dm, ya, yb, pgate)
    g["w_proj_gdn"] = _mm(oa, dya, name="dw_proj_a", ta=True, out_dtype=BF16)
    g["w_proj_fox"] = _mm(ob, dyb, name="dw_proj_b", ta=True, out_dtype=BF16)
    d_oa = _mm(dya, w["w_proj_gdn"], name="d_oa", tb=True, tk=1024)
    d_ob = _mm(dyb, w["w_proj_fox"], name="d_ob", tb=True, tk=1024)
    dgq, dgk, dgv, dgz, dps, dcw, gdn_small = _gdn_bwd(pg, ps, w["conv_w"], w["a_log"], w["dt_bias"], w["gdn_norm_g"],
                                                       d_oa, o_raw, s_all, b_loc, t)
    dfq, dfk, dfv, dps, fox_small = _fox_bwd(pf, ps, w["f_bias"], w["fox_q_norm_g"], w["fox_k_norm_g"],
                                             d_ob, ob, lse, dps, b_loc, t)
    segs = [(dgq, "w_gdn", 0), (dgk, "w_gdn", 1024), (dgv, "w_gdn", 2048), (dgz, "w_gdn", 3072),
            (dfq, "w_fox", 0), (dfk, "w_fox", 1024), (dfv, "w_fox", 2048),
            (dgate_a, "w_gate", 0), (dgate_b, "w_gate", 1024)]
    du = _mm(dps, w["w_small"], name="du_small", tb=True, tk=LANES)
    dws = [_mm(u, dps, name="dw_small", ta=True, out_dtype=BF16)]
    for idx, (dseg, wname, off) in enumerate(segs):
        du = _mm(dseg, w[wname], name=f"du_{idx}", tb=True, tk=1024, b_koff=off,
                 epi=lambda acc, prev: acc + prev, extras=(du,))
        dws.append(_mm(u, dseg, name=f"dw_in_{idx}", ta=True, out_dtype=BF16))
    g["w_in_parts"] = dws
    grad_x, g["norm_mix_g"] = _rms_bwd(du, xf, w["norm_mix_g"], dh, "rms_mix_bwd")
    g["conv"] = dcw
    g["gdn_small"] = gdn_small
    g["fox_small"] = fox_small
    return loss_blk, grad_x, g


def _position():
    x, y, c = lax.axis_index("x"), lax.axis_index("y"), lax.axis_index("c")
    return x, y, c


def _to_bf16(arrs, name):
    n = len(arrs)

    def body(*refs):
        for i in range(n):
            refs[n + i][...] = _bf(refs[i][...])

    return pl.pallas_call(
        body, name=name,
        out_shape=[jax.ShapeDtypeStruct(a.shape, BF16) for a in arrs],
        compiler_params=_cparams(),
    )(*arrs)


def _all_gather(arrs, name):
    n = len(arrs)
    hbm = pl.BlockSpec(memory_space=pl.ANY)

    def body(*refs):
        ins, outs = refs[:n], refs[n:2 * n]
        send, recv, loc = refs[2 * n:]
        x, y, c = _position()
        me = 4 * x + 2 * y + c
        sibling = (x, y, 1 - c)
        chips = [(1 - x, y), (x, 1 - y), (1 - x, 1 - y)]

        def idx(px, py, pc):
            return 4 * px + 2 * py + pc

        def cp(a, k, block, to, src=None):
            return pltpu.make_async_remote_copy(
                src_ref=outs[a].at[block] if src is None else src, dst_ref=outs[a].at[block],
                send_sem=send.at[a, k], recv_sem=recv.at[a, k], device_id=to, device_id_type=MESH)

        mine = [pltpu.make_async_copy(ins[a], outs[a].at[me], loc.at[a]) for a in range(n)]
        for m in mine:
            m.start()
        first = []
        for a in range(n):
            first.append(cp(a, 0, me, sibling, src=ins[a]))
            first += [cp(a, 1 + j, me, (*chip, c), src=ins[a]) for j, chip in enumerate(chips)]
        for f in first:
            f.start()
        passed = []
        for j, chip in enumerate(chips):
            for a in range(n):
                cp(a, 1 + j, idx(*chip, c), (x, y, c)).wait_recv()
                p = cp(a, 4 + j, idx(*chip, c), sibling)
                p.start()
                passed.append(p)
        for a in range(n):
            cp(a, 0, idx(x, y, 1 - c), (x, y, c)).wait_recv()
            for j, chip in enumerate(chips):
                cp(a, 4 + j, idx(*chip, 1 - c), (x, y, c)).wait_recv()
        for f in first + passed:
            f.wait_send()
        for m in mine:
            m.wait()

    return pl.pallas_call(
        body, name=name,
        in_specs=[hbm] * n, out_specs=[hbm] * n,
        out_shape=[jax.ShapeDtypeStruct((N_DEV,) + a.shape, a.dtype) for a in arrs],
        scratch_shapes=[pltpu.SemaphoreType.DMA((n, 7)), pltpu.SemaphoreType.DMA((n, 7)), pltpu.SemaphoreType.DMA((n,))],
        compiler_params=pltpu.CompilerParams(has_side_effects=True),
    )(*arrs)


def _peer(x, y, c, rel):
    return ((1 - x) if rel & 4 else x, (1 - y) if rel & 2 else y, (1 - c) if rel & 1 else c)


def _exchange(arrs, name):
    n = len(arrs)
    hbm = pl.BlockSpec(memory_space=pl.ANY)

    def body(*refs):
        ins, outs = refs[:n], refs[n:2 * n]
        send, recv, loc = refs[2 * n:]
        x, y, c = _position()
        me = 4 * x + 2 * y + c
        mine = [pltpu.make_async_copy(ins[a].at[me], outs[a].at[me], loc.at[a]) for a in range(n)]
        for m in mine:
            m.start()
        copies = []
        for rel in range(1, N_DEV):
            px, py, pc = _peer(x, y, c, rel)
            for a in range(n):
                copies.append(pltpu.make_async_remote_copy(
                    src_ref=ins[a].at[4 * px + 2 * py + pc], dst_ref=outs[a].at[me],
                    send_sem=send.at[a, rel - 1], recv_sem=recv.at[a, rel - 1],
                    device_id=(px, py, pc), device_id_type=MESH))
        for cpy in copies:
            cpy.start()
        for cpy in copies:
            cpy.wait()
        for m in mine:
            m.wait()

    return pl.pallas_call(
        body, name=name,
        in_specs=[hbm] * n, out_specs=[hbm] * n,
        out_shape=[jax.ShapeDtypeStruct(a.shape, a.dtype) for a in arrs],
        scratch_shapes=[pltpu.SemaphoreType.DMA((n, 7)), pltpu.SemaphoreType.DMA((n, 7)), pltpu.SemaphoreType.DMA((n,))],
        compiler_params=pltpu.CompilerParams(has_side_effects=True),
    )(*arrs)


def _all_reduce_small(buf, name):
    rows = buf.shape[0]

    def body(in_ref, out_ref, slots, send, recv):
        x, y, c = _position()
        me = 4 * x + 2 * y + c
        slots[me] = in_ref[...]
        copies = []
        for rel in range(1, N_DEV):
            copies.append(pltpu.make_async_remote_copy(
                src_ref=in_ref, dst_ref=slots.at[me], send_sem=send.at[rel - 1], recv_sem=recv.at[rel - 1],
                device_id=_peer(x, y, c, rel), device_id_type=MESH))
        for cpy in copies:
            cpy.start()
        for cpy in copies:
            cpy.wait()
        tot = slots[0]
        for d in range(1, N_DEV):
            tot = tot + slots[d]
        out_ref[...] = tot

    return pl.pallas_call(
        body, name=name,
        out_shape=jax.ShapeDtypeStruct((rows, LANES), F32),
        in_specs=[pl.BlockSpec(memory_space=pltpu.VMEM)], out_specs=pl.BlockSpec(memory_space=pltpu.VMEM),
        scratch_shapes=[pltpu.VMEM((N_DEV, rows, LANES), F32), pltpu.SemaphoreType.DMA((7,)),
                        pltpu.SemaphoreType.DMA((7,))],
        compiler_params=pltpu.CompilerParams(has_side_effects=True),
    )(buf)


def _adam_math(g, w, m, v):
    m = ADAM_B1 * m + (1.0 - ADAM_B1) * g
    v = ADAM_B2 * v + (1.0 - ADAM_B2) * (g * g)
    m_hat = m / (1.0 - ADAM_B1 ** ADAM_STEP)
    v_hat = v / (1.0 - ADAM_B2 ** ADAM_STEP)
    delta = -ADAM_LR * (m_hat / (jnp.sqrt(v_hat) + ADAM_EPS) + ADAM_WD * w)
    return delta, m, v


def _adam_shard(parts, w, m, v, name):
    r, c = w.shape
    tr = min(r, 128)

    def body(p_ref, w_ref, m_ref, v_ref, g_ref, d_ref, nm_ref, nv_ref):
        g = p_ref[0].astype(F32)
        for s in range(1, N_DEV):
            g = g + p_ref[s].astype(F32)
        d, nm, nv = _adam_math(g, w_ref[...], m_ref[...], v_ref[...])
        g_ref[...] = g
        d_ref[...] = d
        nm_ref[...] = nm
        nv_ref[...] = nv

    row = pl.BlockSpec((tr, c), lambda i: (i, 0))
    o = jax.ShapeDtypeStruct((r, c), F32)
    return pl.pallas_call(
        body, name=name, grid=(r // tr,),
        in_specs=[pl.BlockSpec((N_DEV, tr, c), lambda i: (0, i, 0)), row, row, row],
        out_specs=[row] * 4, out_shape=[o] * 4,
        compiler_params=_cparams(("parallel",)),
    )(parts, w, m, v)


def _adam_small(g, w, m, v):
    def body(g_ref, w_ref, m_ref, v_ref, d_ref, nm_ref, nv_ref):
        d, nm, nv = _adam_math(g_ref[...], w_ref[...], m_ref[...], v_ref[...])
        d_ref[...] = d
        nm_ref[...] = nm
        nv_ref[...] = nv

    o = jax.ShapeDtypeStruct(g.shape, F32)
    return pl.pallas_call(body, name="adam_small", out_shape=[o] * 3, compiler_params=_cparams())(g, w, m, v)


def _split_w_in(w_full):
    o = IN_OFF
    w_gdn = w_full[:, o["gq"]:o["ga"]]
    w_fox = w_full[:, o["fq"]:o["ff"]]
    w_gate = w_full[:, o["gate_a"]:o["end"]]
    w_small = jnp.concatenate([w_full[:, o["ga"]:o["fq"]], w_full[:, o["ff"]:o["gate_a"]],
                               jnp.zeros((w_full.shape[0], LANES - 24), w_full.dtype)], axis=1)
    return w_gdn, w_fox, w_gate, w_small


def _join_w_in(parts):
    small = parts[0]
    return jnp.concatenate(parts[1:5] + [small[:, 0:16]] + parts[5:8] + [small[:, 16:24]] + parts[8:10], axis=1)


def _rows128(a, rows):
    flat = a.reshape(-1)
    flat = jnp.concatenate([flat, jnp.zeros((rows * LANES - flat.shape[0],), flat.dtype)])
    return flat.reshape(rows, LANES)


def kernel(x, norm_mix_g, w_in, gdn_conv_w, gdn_a_log, gdn_dt_bias, gdn_norm_g, fox_q_norm_g, fox_k_norm_g, fox_f_bias, w_proj_gdn, w_proj_fox, w_out, norm_mlp_g, w_up, w_down, loss_target, m_norm_mix_g, m_w_in, m_gdn_conv_w, m_gdn_a_log, m_gdn_dt_bias, m_gdn_norm_g, m_fox_q_norm_g, m_fox_k_norm_g, m_fox_f_bias, m_w_proj_gdn, m_w_proj_fox, m_w_out, m_norm_mlp_g, m_w_up, m_w_down, v_norm_mix_g, v_w_in, v_gdn_conv_w, v_gdn_a_log, v_gdn_dt_bias, v_gdn_norm_g, v_fox_q_norm_g, v_fox_k_norm_g, v_fox_f_bias, v_w_proj_gdn, v_w_proj_fox, v_w_out, v_norm_mlp_g, v_w_up, v_w_down):
    b_loc, t, d = x.shape
    n = b_loc * t
    me = 4 * lax.axis_index("x") + 2 * lax.axis_index("y") + lax.axis_index("c")

    big = [w_in[0], w_proj_gdn[0], w_proj_fox[0], w_out[0], w_up[0], w_down[0]]
    big16 = _to_bf16(big, "weights_to_bf16")
    gathered = _all_gather(list(big16) + [gdn_conv_w[0]], "gather_weights")
    g_in, g_pa, g_pb, g_out, g_up, g_down, g_conv = gathered
    w_full = g_in.transpose(1, 0, 2).reshape(d, N_DEV * w_in.shape[2])
    w_gdn, w_fox, w_gate, w_small = _split_w_in(w_full)
    weights = {
        "w_gdn": w_gdn, "w_fox": w_fox, "w_gate": w_gate, "w_small": w_small,
        "w_proj_gdn": g_pa.reshape(d, d), "w_proj_fox": g_pb.reshape(d, d), "w_out": g_out.reshape(d, d),
        "w_up": g_up.transpose(1, 0, 2).reshape(d, D_FF), "w_down": g_down.reshape(D_FF, d),
        "conv_w": g_conv.transpose(1, 0, 2).reshape(CONV_K, 3 * d),
        "norm_mix_g": norm_mix_g, "norm_mlp_g": norm_mlp_g, "a_log": gdn_a_log, "dt_bias": gdn_dt_bias,
        "gdn_norm_g": gdn_norm_g, "fox_q_norm_g": fox_q_norm_g, "fox_k_norm_g": fox_k_norm_g, "f_bias": fox_f_bias,
    }

    loss_blk, grad_x, g = _local_step(x.reshape(n, d), loss_target.reshape(n, d), weights, b_loc, t)

    c_in = w_in.shape[2]
    gw_in = _join_w_in(g["w_in_parts"]).reshape(d, N_DEV, c_in).transpose(1, 0, 2)
    c_up = w_up.shape[2]
    send = [gw_in, g["w_proj_gdn"].reshape(N_DEV, d // N_DEV, d), g["w_proj_fox"].reshape(N_DEV, d // N_DEV, d),
            g["w_out"].reshape(N_DEV, d // N_DEV, d), g["w_up"].reshape(d, N_DEV, c_up).transpose(1, 0, 2),
            g["w_down"].reshape(N_DEV, D_FF // N_DEV, d)]
    recv = _exchange(send, "exchange_grads")
    shards = [(w_in, m_w_in, v_w_in), (w_proj_gdn, m_w_proj_gdn, v_w_proj_gdn), (w_proj_fox, m_w_proj_fox, v_w_proj_fox),
              (w_out, m_w_out, v_w_out), (w_up, m_w_up, v_w_up), (w_down, m_w_down, v_w_down)]
    big_out = []
    for i, (parts, (wi, mi, vi)) in enumerate(zip(recv, shards)):
        res = _adam_shard(parts, wi[0], mi[0], vi[0], f"adam_{i}")
        big_out.append([r[None] for r in res])

    conv_rows = CONV_K * 3 * d // LANES
    conv_g = g["conv"].transpose(1, 0, 2).reshape(conv_rows, LANES)
    buf = jnp.concatenate([conv_g, g["norm_mix_g"].reshape(8, LANES), g["norm_mlp_g"].reshape(8, LANES),
                           g["gdn_small"], g["fox_small"], loss_blk], axis=0)
    tot = _all_reduce_small(buf, "all_reduce_small")
    o = conv_rows
    conv_full = tot[0:o].reshape(CONV_K, 3 * d)
    c_conv = gdn_conv_w.shape[2]
    g_conv_shard = lax.dynamic_slice(conv_full, (0, me * c_conv), (CONV_K, c_conv))
    g_mix = tot[o:o + 8].reshape(1, d)
    g_mlp = tot[o + 8:o + 16].reshape(1, d)
    gs, fs = tot[o + 16:o + 24], tot[o + 24:o + 32]
    loss = tot[o + 32, 0]
    small_g = [g_mix, g_conv_shard[None], gs[0:1, 0:HEADS], gs[1:2, 0:HEADS], gs[2:3], fs[0:1], fs[1:2], fs[2:3, 0:HEADS],
               g_mlp]
    small_w = [norm_mix_g, gdn_conv_w, gdn_a_log, gdn_dt_bias, gdn_norm_g, fox_q_norm_g, fox_k_norm_g, fox_f_bias,
               norm_mlp_g]
    small_m = [m_norm_mix_g, m_gdn_conv_w, m_gdn_a_log, m_gdn_dt_bias, m_gdn_norm_g, m_fox_q_norm_g, m_fox_k_norm_g,
               m_fox_f_bias, m_norm_mlp_g]
    small_v = [v_norm_mix_g, v_gdn_conv_w, v_gdn_a_log, v_gdn_dt_bias, v_gdn_norm_g, v_fox_q_norm_g, v_fox_k_norm_g,
               v_fox_f_bias, v_norm_mlp_g]
    row_counts = [-(-a.size // (8 * LANES)) * 8 for a in small_w]

    def pack(arrs):
        return jnp.concatenate([_rows128(a, rc) for a, rc in zip(arrs, row_counts)], axis=0)

    sd, sm, sv = _adam_small(pack(small_g), pack(small_w), pack(small_m), pack(small_v))

    def unpack(p):
        outs, r0 = [], 0
        for a, rc in zip(small_w, row_counts):
            outs.append(p[r0:r0 + rc].reshape(-1)[:a.size].reshape(a.shape))
            r0 += rc
        return outs

    small_out = [small_g_i.reshape(w_i.shape) for small_g_i, w_i in zip(small_g, small_w)], unpack(sd), unpack(sm), unpack(sv)

    def ordered(kind):
        s = small_out[kind]
        bo = [b[kind] for b in big_out]
        return [s[0], bo[0], s[1], s[2], s[3], s[4], s[5], s[6], s[7], bo[1], bo[2], bo[3], s[8], bo[4], bo[5]]

    return (loss, grad_x.reshape(b_loc, t, d), *ordered(0), *ordered(1), *ordered(2), *ordered(3))
```

```python
import functools

import jax
import jax.numpy as jnp
from jax import lax
from jax.experimental import pallas as pl
from jax.experimental.pallas import tpu as pltpu

F32 = jnp.float32
BF16 = jnp.bfloat16
HI = lax.Precision.HIGHEST
MESH = pl.DeviceIdType.MESH

N_DEV = 8
D_MODEL = 1024
HEADS = 8
DH = 128
CONV_K = 4
CHUNK = 64
GDN_GROUP = 4
FOX_BLOCK = 128
FOX_TILE = 512
D_FF = 4 * D_MODEL
EPS = 1e-6
LANES = 128
NEG = -1e30
IN_OFF = {"gq": 0, "gk": 1024, "gv": 2048, "gz": 3072, "ga": 4096, "gb": 4104, "fq": 4112, "fk": 5136,
          "fv": 6160, "ff": 7184, "gate_a": 7192, "gate_b": 8216, "end": 9240}
LANE_GA, LANE_GB, LANE_FF = 0, 8, 16

ADAM_LR = 0.001
ADAM_B1 = 0.9
ADAM_B2 = 0.999
ADAM_EPS = 1e-08
ADAM_WD = 0.01
ADAM_STEP = 10

VMEM_LIMIT = 56 * 1024 * 1024


def _cparams(sem=None):
    return pltpu.CompilerParams(dimension_semantics=sem, vmem_limit_bytes=VMEM_LIMIT)


def _sigmoid(x):
    return 1.0 / (1.0 + jnp.exp(-x))


def _softplus(x):
    return jnp.maximum(x, 0.0) + jnp.log(1.0 + jnp.exp(-jnp.abs(x)))


def _dot(a, b, prec=None):
    return lax.dot_general(a, b, (((1,), (0,)), ((), ())), precision=prec, preferred_element_type=F32)


def _dot_nt(a, b, prec=None):
    return lax.dot_general(a, b, (((1,), (1,)), ((), ())), precision=prec, preferred_element_type=F32)


def _dot_tn(a, b, prec=None):
    return lax.dot_general(a, b, (((0,), (0,)), ((), ())), precision=prec, preferred_element_type=F32)


def _bf(x):
    return x.astype(BF16)


MM_TILE = 1024


def _mm(a, b, *, name, ta=False, tb=False, out_dtype=F32, epi=None, extras=(), out2=None,
        b_koff=0, tm=MM_TILE, tn=MM_TILE, tk=MM_TILE):
    m = a.shape[1] if ta else a.shape[0]
    kdim = a.shape[0] if ta else a.shape[1]
    n = b.shape[0] if tb else b.shape[1]
    tm, tn, tk = min(tm, m), min(tn, n), min(tk, kdim)
    nk = kdim // tk
    grid = (m // tm, n // tn, nk)
    koff = b_koff // tk
    a_spec = pl.BlockSpec((tk, tm), lambda i, j, k: (k, i)) if ta else pl.BlockSpec((tm, tk), lambda i, j, k: (i, k))
    if tb:
        b_spec = pl.BlockSpec((tn, tk), lambda i, j, k: (j, k + koff))
    else:
        b_spec = pl.BlockSpec((tk, tn), lambda i, j, k: (k + koff, j))
    o_spec = pl.BlockSpec((tm, tn), lambda i, j, k: (i, j))
    n_e = len(extras)
    n_o = 1 if out2 is None else 2
    dims = (((0 if ta else 1,), (1 if tb else 0,)), ((), ()))

    def body(a_ref, b_ref, *rest):
        e_refs, o_refs = rest[:n_e], rest[n_e:n_e + n_o]
        prod = lax.dot_general(_bf(a_ref[...]), _bf(b_ref[...]), dims, preferred_element_type=F32)

        def finish(r):
            if out2 is not None:
                o_refs[1][...] = out2[0](r).astype(out2[1])
            if epi is not None:
                r = epi(r, *[e[...] for e in e_refs])
            o_refs[0][...] = r.astype(out_dtype)

        if nk == 1:
            finish(prod)
        else:
            acc = rest[n_e + n_o]
            k = pl.program_id(2)

            @pl.when(k == 0)
            def _():
                acc[...] = prod

            @pl.when(k > 0)
            def _():
                acc[...] += prod

            @pl.when(k == nk - 1)
            def _():
                finish(acc[...])

    shapes = [jax.ShapeDtypeStruct((m, n), out_dtype)]
    if out2 is not None:
        shapes.append(jax.ShapeDtypeStruct((m, n), out2[1]))
    res = pl.pallas_call(
        body, name=name, grid=grid,
        in_specs=[a_spec, b_spec] + [o_spec] * n_e,
        out_specs=[o_spec] * n_o, out_shape=shapes,
        scratch_shapes=[] if nk == 1 else [pltpu.VMEM((tm, tn), F32)],
        compiler_params=_cparams(("parallel", "parallel", "arbitrary")),
    )(a, b, *extras)
    return res[0] if out2 is None else res


def _relu2(x):
    r = jnp.maximum(x, 0.0)
    return r * r


ROWS = 512


def _rms_fwd(x, g, name):
    n, d = x.shape

    def body(x_ref, g_ref, u_ref):
        xv = x_ref[...]
        r = lax.rsqrt(jnp.mean(xv * xv, axis=1, keepdims=True) + EPS)
        u_ref[...] = _bf(xv * r * g_ref[...])

    return pl.pallas_call(
        body, name=name, grid=(n // ROWS,),
        in_specs=[pl.BlockSpec((ROWS, d), lambda i: (i, 0)), pl.BlockSpec((1, d), lambda i: (0, 0))],
        out_specs=pl.BlockSpec((ROWS, d), lambda i: (i, 0)),
        out_shape=jax.ShapeDtypeStruct((n, d), BF16),
        compiler_params=_cparams(("parallel",)),
    )(x, g)


def _rms_bwd(dy, x, g, dres, name):
    n, d = x.shape

    def body(dy_ref, x_ref, g_ref, dres_ref, dx_ref, dx16_ref, dg_ref):
        i = pl.program_id(0)
        xv, dyv = x_ref[...], dy_ref[...]
        r = lax.rsqrt(jnp.mean(xv * xv, axis=1, keepdims=True) + EPS)
        gy = dyv * g_ref[...]
        s = jnp.sum(gy * xv, axis=1, keepdims=True)
        dx = dres_ref[...] + r * gy - xv * (r * r * r * (1.0 / d)) * s
        dx_ref[...] = dx
        dx16_ref[...] = _bf(dx)

        @pl.when(i == 0)
        def _():
            dg_ref[...] = jnp.zeros_like(dg_ref)

        dg_ref[...] += jnp.sum(dyv * xv * r, axis=0, keepdims=True)

    row = pl.BlockSpec((ROWS, d), lambda i: (i, 0))
    vec = pl.BlockSpec((1, d), lambda i: (0, 0))
    return pl.pallas_call(
        body, name=name, grid=(n // ROWS,),
        in_specs=[row, row, vec, row], out_specs=[row, row, vec],
        out_shape=[jax.ShapeDtypeStruct((n, d), F32), jax.ShapeDtypeStruct((n, d), BF16),
                   jax.ShapeDtypeStruct((1, d), F32)],
        compiler_params=_cparams(("arbitrary",)),
    )(dy, x, g, dres)


def _merge_fwd(ya, yb, gate):
    n, d = ya.shape

    def body(ya_ref, yb_ref, ga_ref, gb_ref, o_ref):
        o_ref[...] = _bf(_sigmoid(ga_ref[...]) * ya_ref[...] + _sigmoid(gb_ref[...]) * yb_ref[...])

    row = pl.BlockSpec((ROWS, d), lambda i: (i, 0))
    return pl.pallas_call(
        body, name="merge_fwd", grid=(n // ROWS,),
        in_specs=[row, row, row, pl.BlockSpec((ROWS, d), lambda i: (i, 1))], out_specs=row,
        out_shape=jax.ShapeDtypeStruct((n, d), BF16),
        compiler_params=_cparams(("parallel",)),
    )(ya, yb, gate, gate)


def _merge_bwd(dm, ya, yb, gate):
    n, d = ya.shape

    def body(dm_ref, ya_ref, yb_ref, ga_ref, gb_ref, dya_ref, dyb_ref, dga_ref, dgb_ref):
        dmv = dm_ref[...]
        sa, sb = _sigmoid(ga_ref[...]), _sigmoid(gb_ref[...])
        dya_ref[...] = _bf(dmv * sa)
        dyb_ref[...] = _bf(dmv * sb)
        dga_ref[...] = _bf(dmv * ya_ref[...] * sa * (1.0 - sa))
        dgb_ref[...] = _bf(dmv * yb_ref[...] * sb * (1.0 - sb))

    row = pl.BlockSpec((ROWS, d), lambda i: (i, 0))
    o = jax.ShapeDtypeStruct((n, d), BF16)
    return pl.pallas_call(
        body, name="merge_bwd", grid=(n // ROWS,),
        in_specs=[row, row, row, row, pl.BlockSpec((ROWS, d), lambda i: (i, 1))], out_specs=[row] * 4,
        out_shape=[o] * 4,
        compiler_params=_cparams(("parallel",)),
    )(dm, ya, yb, gate, gate)


def _loss_bwd(out, target):
    n, d = out.shape

    def body(o_ref, t_ref, d_ref, d16_ref, l_ref):
        i = pl.program_id(0)
        err = o_ref[...] - t_ref[...]
        d_ref[...] = err * (1.0 / d)
        d16_ref[...] = _bf(err * (1.0 / d))

        @pl.when(i == 0)
        def _():
            l_ref[...] = jnp.zeros_like(l_ref)

        l_ref[...] += 0.5 * jnp.sum(jnp.mean(err * err, axis=1, keepdims=True), axis=0, keepdims=True)

    row = pl.BlockSpec((ROWS, d), lambda i: (i, 0))
    return pl.pallas_call(
        body, name="loss_bwd", grid=(n // ROWS,),
        in_specs=[row, row], out_specs=[row, row, pl.BlockSpec((8, LANES), lambda i: (0, 0))],
        out_shape=[jax.ShapeDtypeStruct((n, d), F32), jax.ShapeDtypeStruct((n, d), BF16),
                   jax.ShapeDtypeStruct((8, LANES), F32)],
        compiler_params=_cparams(("arbitrary",)),
    )(out, target)


PAD = 8


def _pad_zero(pad_ref):
    t = pad_ref.shape[0] - 2 * PAD
    pad_ref[0:PAD, :] = jnp.zeros((PAD, LANES), F32)
    pad_ref[PAD + t:2 * PAD + t, :] = jnp.zeros((PAD, LANES), F32)


def _shifted(pad_ref, s):
    t = pad_ref.shape[0] - 2 * PAD
    return pad_ref[PAD - s:PAD - s + t, :]


def _conv(x, w_ref, pad_ref):
    t = x.shape[0]
    pad_ref[PAD:PAD + t, :] = x
    y = _shifted(pad_ref, 3) * w_ref[0:1, :]
    y = y + _shifted(pad_ref, 2) * w_ref[1:2, :]
    y = y + _shifted(pad_ref, 1) * w_ref[2:3, :]
    return y + x * w_ref[3:4, :]


def _chunk_consts():
    r = lax.broadcasted_iota(jnp.int32, (CHUNK, CHUNK), 0)
    c = lax.broadcasted_iota(jnp.int32, (CHUNK, CHUNK), 1)
    incl, strict = r >= c, r > c
    return dict(incl=incl, strict=strict, trilf=incl.astype(F32), triuf=(r <= c).astype(F32),
                eye=(r == c).astype(F32))


class _V:
    def __init__(self, xs):
        self.xs = list(xs)

    def __add__(self, o):
        return _ap(lambda x, y: x + y, self, o)

    def __radd__(self, o):
        return _ap(lambda x, y: y + x, self, o)

    def __sub__(self, o):
        return _ap(lambda x, y: x - y, self, o)

    def __rsub__(self, o):
        return _ap(lambda x, y: y - x, self, o)

    def __mul__(self, o):
        return _ap(lambda x, y: x * y, self, o)

    def __rmul__(self, o):
        return _ap(lambda x, y: y * x, self, o)

    def __neg__(self):
        return _ap(lambda x: -x, self)

    def __getitem__(self, idx):
        return _ap(lambda x: x[idx], self)


def _ap(fn, *args):
    n = [len(a.xs) for a in args if isinstance(a, _V)]
    if not n:
        return fn(*args)
    return _V([fn(*[a.xs[i] if isinstance(a, _V) else a for a in args]) for i in range(n[0])])


def _vbf(x):
    return _ap(_bf, x)


def _vdot(a, b):
    return _ap(_dot, a, b)


def _vdot_nt(a, b):
    return _ap(_dot_nt, a, b)


def _vdot_tn(a, b):
    return _ap(_dot_tn, a, b)


def _vexp(x):
    return _ap(jnp.exp, x)


def _vsum(x, axis):
    return _ap(lambda v: jnp.sum(v, axis=axis, keepdims=True), x)


def _vcat(a, b, axis):
    return _ap(lambda x, y: jnp.concatenate([x, y], axis=axis), a, b)


def _vmask(mask, x):
    return _ap(lambda v: jnp.where(mask, v, 0.0), x)


def _split2(x):
    h = _vbf(x)
    return h, _vbf(x - _ap(lambda v: v.astype(F32), h))


def _dot3(a, b, kind=_vdot):
    ah, al = _split2(a)
    bh, bl = _split2(b)
    return kind(ah, bh) + (kind(ah, bl) + kind(al, bh))


def _split(x, terms):
    out = []
    for _ in range(terms):
        h = _vbf(x)
        out.append(h)
        x = x - _ap(lambda v: v.astype(F32), h)
    return out


def _dot_exact_l(m01, x, kind=_vdot, terms=2):
    mb = _bf(m01)
    parts = [kind(mb, xp) for xp in _split(x, terms)]
    return functools.reduce(lambda a, b: a + b, reversed(parts))


def _dot_exact_r(x, m01, kind=_vdot, terms=2):
    mb = _bf(m01)
    parts = [kind(xp, mb) for xp in _split(x, terms)]
    return functools.reduce(lambda a, b: a + b, reversed(parts))


def _inv_unit_lower(a, eye):
    p = -a
    r = p + eye
    p = _dot3(p, p)
    for j in range(1, 6):
        if j < 5:
            y = _dot3(p, _vcat(p, r, 1))
            p, r = y[:, 0:CHUNK], r + y[:, CHUNK:2 * CHUNK]
        else:
            r = r + _dot3(p, r)
    return r


def _gdn_chunk_pre(q, k, v, g128, g64, b128, b64, cs):
    incl = cs["incl"]
    big_g = _dot_exact_l(cs["trilf"], g128)
    gc = big_g[:, 0:CHUNK]
    gr = _dot_exact_r(g64, cs["triuf"], _vdot_tn)
    decay = _ap(lambda d: jnp.where(incl, jnp.exp(jnp.where(incl, d, 0.0)), 0.0), gc - gr)
    kb, qb = _vbf(k), _vbf(q)
    qkk = _vdot_nt(_vcat(qb, kb, 0), kb)
    qk, kk = qkk[0:CHUNK], qkk[CHUNK:2 * CHUNK]
    tm = _inv_unit_lower(_vmask(cs["strict"], b64 * kk * decay), cs["eye"])
    e_g = _vexp(big_g)
    wu = _dot3(tm, _vcat(v * b128, k * (b128 * e_g), 1))
    w, u = wu[:, 0:DH], wu[:, DH:2 * DH]
    g_last = _vsum(g128, 0)
    return dict(big_g=big_g, decay=decay, kk=kk, qk=qk, tm=tm, w=w, u=u, p=qk * decay, q_dec=q * e_g,
                k_dec=k * _vexp(g_last - big_g), dec=_vexp(g_last))


def _gdn_chunk_post(q, k, v, g128, b128, b64, s, ds_next, do, dv_new, big_g, decay, kk, qk, tm, u, v_new, cs):
    e_g = _vexp(big_g)
    vb = v * b128
    kbeta = k * (b128 * e_g)
    q_dec = q * e_g
    g_last = _vsum(g128, 0)
    ekg = _vexp(g_last - big_g)
    k_dec = k * ekg
    dec = _vexp(g_last)
    kb, qb, sb = _vbf(k), _vbf(q), _vbf(s)
    dob, dsb, vnb, dvnb = _vbf(do), _vbf(ds_next), _vbf(v_new), _vbf(dv_new)
    dp = _vmask(cs["incl"], _vdot_nt(dob, vnb))
    dq_dec = _vdot_nt(dob, sb)
    du = -_vdot_nt(dvnb, sb)
    ddec = _vsum(_vsum(s * ds_next, 1), 0)
    dk_dec = _vdot_nt(vnb, dsb)
    dwu = _vcat(dv_new, du, 1)
    dt = _dot3(dwu, _vcat(vb, kbeta, 1), _vdot_nt)
    dvk = _dot3(tm, dwu, _vdot_tn)
    dvb, dkbeta = dvk[:, 0:DH], dvk[:, DH:2 * DH]
    da = _vmask(cs["strict"], -_dot3(tm, _dot3(dt, tm, _vdot_nt), _vdot_tn))
    dkk = _vbf(da * b64 * decay)
    dqk = _vbf(dp * decay)
    ddd = (da * b64 * kk + dp * qk) * decay
    dq = _vdot(dqk, kb) + dq_dec * e_g
    dk = _vdot_tn(dqk, qb) + _vdot(dkk, kb) + _vdot_tn(dkk, kb) + dk_dec * ekg + dkbeta * (b128 * e_g)
    dv = dvb * b128
    dbeta = _vsum(da * kk * decay, 1) + _vsum(dvb * v, 1) + _vsum(dkbeta * k * e_g, 1)
    s_k = _vsum(dk_dec * k_dec, 1)
    dg_col = _vsum(ddd, 1) + _vsum(dq_dec * q_dec, 1) - s_k + _vsum(dkbeta * kbeta, 1)
    colsum = _dot_exact_r(ddd, jnp.ones((CHUNK, LANES), F32), _vdot_tn)
    dg_last = _vsum(s_k, 0) + ddec * dec
    dg = _dot_exact_l(cs["triuf"], dg_col - colsum) + dg_last
    return dq, dk, dv, dg, dbeta


def _stack_rows(vecs, nrows):
    row = lax.broadcasted_iota(jnp.int32, (nrows, LANES), 0)
    out = jnp.zeros((nrows, LANES), F32)
    for i, v in enumerate(vecs):
        out = out + jnp.where(row == i, jnp.broadcast_to(v, (nrows, LANES)), 0.0)
    return out


def _head_lane(x, lane_idx):
    lane = lax.broadcasted_iota(jnp.int32, x.shape, 1)
    return jnp.sum(jnp.where(lane == lane_idx, x, 0.0), axis=1, keepdims=True)


def _gdn_gates(ps, h, alog_ref, dtb_ref):
    ga = _head_lane(ps, LANE_GA + h)
    gb = _head_lane(ps, LANE_GB + h)
    a = jnp.exp(jnp.full((1, 1), alog_ref[0, h], F32))
    sp_in = ga + dtb_ref[0, h]
    g = -a * _softplus(sp_in)
    return g, _sigmoid(gb), a, sp_in


def _gdn_specs(b_loc, t):
    def col(off):
        return pl.BlockSpec((t, DH), lambda b, h: (b, off + h))

    ps_spec = pl.BlockSpec((t, LANES), lambda b, h: (b, 0))

    def wcol(off):
        return pl.BlockSpec((CONV_K, DH), lambda b, h: (0, off + h))

    smem = pl.BlockSpec(memory_space=pltpu.SMEM)
    vec = pl.BlockSpec((1, DH), lambda b, h: (0, 0))
    return col, ps_spec, wcol, smem, vec


def _gdn_fwd(pg, ps, convw, a_log, dt_bias, gnorm, b_loc, t):
    n = b_loc * t
    nc = t // CHUNK
    col, ps_spec, wcol, smem, vec = _gdn_specs(b_loc, t)

    def body(q_ref, k_ref, v_ref, z_ref, ps_ref, wq_ref, wk_ref, wv_ref, alog_ref, dtb_ref, gn_ref,
             oa_ref, oraw_ref, s_ref, qn, kn, vv, g128, g64, b128, b64, uq_s, p_s, kd_s, dec_s, pad_s):
        h = pl.program_id(1)
        g, beta, _, _ = _gdn_gates(ps_ref[...], h, alog_ref, dtb_ref)
        g128[...] = jnp.broadcast_to(g, (t, LANES))
        g64[...] = jnp.broadcast_to(g, (t, CHUNK))
        b128[...] = jnp.broadcast_to(beta, (t, LANES))
        b64[...] = jnp.broadcast_to(beta, (t, CHUNK))
        _pad_zero(pad_s)
        pq = _conv(q_ref[...], wq_ref, pad_s)
        yq = pq * _sigmoid(pq)
        qn[...] = yq * (lax.rsqrt(jnp.sum(yq * yq, axis=1, keepdims=True) + EPS) * (DH ** -0.5))
        pk = _conv(k_ref[...], wk_ref, pad_s)
        yk = pk * _sigmoid(pk)
        kn[...] = yk * lax.rsqrt(jnp.sum(yk * yk, axis=1, keepdims=True) + EPS)
        pv = _conv(v_ref[...], wv_ref, pad_s)
        vv[...] = pv * _sigmoid(pv)
        cs = _chunk_consts()

        def pre_group(gi, _):
            idx = [gi * GDN_GROUP + c for c in range(GDN_GROUP)]
            rows = [pl.ds(pl.multiple_of(i * CHUNK, CHUNK), CHUNK) for i in idx]
            ins = [_V([ref[r, :] for r in rows]) for ref in (qn, kn, vv, g128, g64, b128, b64)]
            f = _gdn_chunk_pre(*ins, cs)
            for c, (i, r) in enumerate(zip(idx, rows)):
                vv[r, :] = f["w"].xs[c]
                uq_s[i, 0:CHUNK, :] = _bf(f["u"].xs[c])
                uq_s[i, CHUNK:2 * CHUNK, :] = _bf(f["q_dec"].xs[c])
                p_s[r, :] = _bf(f["p"].xs[c])
                kd_s[r, :] = _bf(f["k_dec"].xs[c])
                dec_s[pl.ds(pl.multiple_of(i * 8, 8), 8), :] = jnp.broadcast_to(f["dec"].xs[c], (8, LANES))
            return 0

        lax.fori_loop(0, nc // GDN_GROUP, pre_group, 0)

        def chunk(i, s):
            r = pl.ds(pl.multiple_of(i * CHUNK, CHUNK), CHUNK)
            us = _dot(uq_s[i], _bf(s))
            vnb = _bf(vv[r, :] - us[0:CHUNK])
            oraw_ref[r, :] = us[CHUNK:2 * CHUNK] + _dot(p_s[r, :], vnb)
            s_ref[0, 0, i] = s
            return s * dec_s[pl.ds(pl.multiple_of(i * 8, 8), 1), :] + _dot_tn(kd_s[r, :], vnb)

        lax.fori_loop(0, nc, chunk, jnp.zeros((DH, DH), F32))
        o = oraw_ref[...]
        rr = lax.rsqrt(jnp.mean(o * o, axis=1, keepdims=True) + EPS)
        z = z_ref[...]
        oa_ref[...] = _bf((o * rr * gn_ref[...]) * (z * _sigmoid(z)))

    return pl.pallas_call(
        body, name="gdn_fwd", grid=(b_loc, HEADS),
        in_specs=[col(0), col(HEADS), col(2 * HEADS), col(3 * HEADS), ps_spec, wcol(0), wcol(HEADS), wcol(2 * HEADS),
                  smem, smem, vec],
        out_specs=[pl.BlockSpec((t, DH), lambda b, h: (b, h)), pl.BlockSpec((t, DH), lambda b, h: (b, h)),
                   pl.BlockSpec((1, 1, nc, DH, DH), lambda b, h: (b, h, 0, 0, 0))],
        out_shape=[jax.ShapeDtypeStruct((n, HEADS * DH), BF16), jax.ShapeDtypeStruct((n, HEADS * DH), F32),
                   jax.ShapeDtypeStruct((b_loc, HEADS, nc, DH, DH), F32)],
        scratch_shapes=([pltpu.VMEM((t, DH), F32)] * 3 + [pltpu.VMEM((t, LANES), F32), pltpu.VMEM((t, CHUNK), F32)] * 2
                        + [pltpu.VMEM((nc, 2 * CHUNK, DH), BF16), pltpu.VMEM((t, CHUNK), BF16), pltpu.VMEM((t, DH), BF16),
                           pltpu.VMEM((8 * nc, LANES), F32), pltpu.VMEM((t + 2 * PAD, LANES), F32)]),
        compiler_params=_cparams(("arbitrary", "arbitrary")),
    )(pg, pg, pg, pg, ps, convw, convw, convw, a_log, dt_bias, gnorm)


def _gdn_bwd(pg, ps, convw, a_log, dt_bias, gnorm, d_oa, o_raw, s_all, b_loc, t):
    n = b_loc * t
    nc = t // CHUNK
    col, ps_spec, wcol, smem, vec = _gdn_specs(b_loc, t)

    def body(q_ref, k_ref, v_ref, z_ref, ps_ref, wq_ref, wk_ref, wv_ref, alog_ref, dtb_ref, gn_ref,
             doa_ref, oraw_ref, s_ref,
             dq_ref, dk_ref, dv_ref, dz_ref, dps_ref, dcw_ref, dsm_ref,
             qn, kn, vv, g128, g64, b128, b64, do_s, bg_s, u_s, vn_s, dvn_s, dcy_s, kk_s, qk_s, tm_s, dsn_s, pad_s):
        b, h = pl.program_id(0), pl.program_id(1)
        g, beta, _, _ = _gdn_gates(ps_ref[...], h, alog_ref, dtb_ref)
        g128[...] = jnp.broadcast_to(g, (t, LANES))
        g64[...] = jnp.broadcast_to(g, (t, CHUNK))
        b128[...] = jnp.broadcast_to(beta, (t, LANES))
        b64[...] = jnp.broadcast_to(beta, (t, CHUNK))
        _pad_zero(pad_s)

        def prep(x_ref, w_ref):
            p = _conv(x_ref[...], w_ref, pad_s)
            sg = _sigmoid(p)
            return p, sg, p * sg

        _, _, yq = prep(q_ref, wq_ref)
        qn[...] = yq * (lax.rsqrt(jnp.sum(yq * yq, axis=1, keepdims=True) + EPS) * (DH ** -0.5))
        _, _, yk = prep(k_ref, wk_ref)
        kn[...] = yk * lax.rsqrt(jnp.sum(yk * yk, axis=1, keepdims=True) + EPS)
        _, _, yv = prep(v_ref, wv_ref)
        vv[...] = yv

        o = oraw_ref[...]
        z = z_ref[...]
        doa = doa_ref[...]
        gn = gn_ref[...]
        ro = lax.rsqrt(jnp.mean(o * o, axis=1, keepdims=True) + EPS)
        sz = _sigmoid(z)
        dz_ref[...] = _bf(doa * (o * ro * gn) * (sz * (1.0 + z * (1.0 - sz))))
        dn = doa * (z * sz)
        dgn = jnp.sum(dn * o * ro, axis=0, keepdims=True)
        gy = dn * gn
        do_s[...] = ro * gy - o * (ro * ro * ro * (1.0 / DH)) * jnp.sum(gy * o, axis=1, keepdims=True)

        cs = _chunk_consts()

        def pre_group(gi, _):
            idx = [gi * GDN_GROUP + c for c in range(GDN_GROUP)]
            rows = [pl.ds(pl.multiple_of(i * CHUNK, CHUNK), CHUNK) for i in idx]
            ins = [_V([ref[r, :] for r in rows]) for ref in (qn, kn, vv, g128, g64, b128, b64)]
            states = _V([_bf(s_ref[0, 0, i]) for i in idx])
            f = _gdn_chunk_pre(*ins, cs)
            v_new = f["w"] - _vdot(_vbf(f["u"]), states)
            for c, r in enumerate(rows):
                bg_s[r, :] = f["big_g"].xs[c]
                u_s[r, :] = f["u"].xs[c]
                vn_s[r, :] = v_new.xs[c]
                dcy_s[r, :] = f["decay"].xs[c]
                kk_s[r, :] = f["kk"].xs[c]
                qk_s[r, :] = f["qk"].xs[c]
                tm_s[r, :] = f["tm"].xs[c]
            return 0

        lax.fori_loop(0, nc // GDN_GROUP, pre_group, 0)

        def chunk(j, ds):
            i = nc - 1 - j
            r = pl.ds(pl.multiple_of(i * CHUNK, CHUNK), CHUNK)
            big_g = bg_s[r, :]
            g_last = jnp.sum(g128[r, :], axis=0, keepdims=True)
            dob = _bf(do_s[r, :])
            dv_new = (_dot_tn(_bf(qk_s[r, :] * dcy_s[r, :]), dob)
                      + _dot(_bf(kn[r, :] * jnp.exp(g_last - big_g)), _bf(ds)))
            dvn_s[r, :] = dv_new
            dsn_s[i] = ds
            return (_dot_tn(_bf(qn[r, :] * jnp.exp(big_g)), dob) + jnp.exp(g_last) * ds
                    - _dot_tn(_bf(u_s[r, :]), _bf(dv_new)))

        lax.fori_loop(0, nc, chunk, jnp.zeros((DH, DH), F32))

        def post_group(gi, _):
            idx = [gi * GDN_GROUP + c for c in range(GDN_GROUP)]
            rows = [pl.ds(pl.multiple_of(i * CHUNK, CHUNK), CHUNK) for i in idx]
            def rows_of(ref):
                return _V([ref[r, :] for r in rows])

            dq, dk, dv, dg, dbeta = _gdn_chunk_post(
                rows_of(qn), rows_of(kn), rows_of(vv), rows_of(g128), rows_of(b128), rows_of(b64),
                _V([s_ref[0, 0, i] for i in idx]), _V([dsn_s[i] for i in idx]), rows_of(do_s), rows_of(dvn_s),
                rows_of(bg_s), rows_of(dcy_s), rows_of(kk_s), rows_of(qk_s), rows_of(tm_s), rows_of(u_s), rows_of(vn_s),
                cs)
            for c, r in enumerate(rows):
                qn[r, :] = dq.xs[c]
                kn[r, :] = dk.xs[c]
                vv[r, :] = dv.xs[c]
                g128[r, :] = dg.xs[c]
                b128[r, :] = jnp.broadcast_to(dbeta.xs[c], (CHUNK, LANES))
            return 0

        lax.fori_loop(0, nc // GDN_GROUP, post_group, 0)
        dqh, dkh, dvh = qn, kn, vv

        g, beta, a, sp_in = _gdn_gates(ps_ref[...], h, alog_ref, dtb_ref)
        dg = g128[...]
        d_ga = dg * (-a) * _sigmoid(sp_in)
        d_alog = jnp.sum(dg * g, axis=0, keepdims=True)
        d_dtb = jnp.sum(d_ga, axis=0, keepdims=True)
        d_gb = b128[...] * (beta * (1.0 - beta))
        lane = lax.broadcasted_iota(jnp.int32, (t, LANES), 1)
        contrib = jnp.where(lane == LANE_GA + h, d_ga, 0.0) + jnp.where(lane == LANE_GB + h, d_gb, 0.0)

        @pl.when(h == 0)
        def _():
            dps_ref[...] = jnp.zeros_like(dps_ref)

        dps_ref[...] += contrib

        lane1 = lax.broadcasted_iota(jnp.int32, (1, LANES), 1)
        small = _stack_rows([jnp.where(lane1 == h, d_alog, 0.0), jnp.where(lane1 == h, d_dtb, 0.0), dgn], 8)

        @pl.when((b == 0) & (h == 0))
        def _():
            dsm_ref[...] = jnp.zeros_like(dsm_ref)
            dcw_ref[...] = jnp.zeros_like(dcw_ref)

        dsm_ref[...] += small

        def conv_bwd(dp, x, w_ref, slot):
            dw = _stack_rows([jnp.sum(dp * _shifted(pad_s, 3), axis=0, keepdims=True),
                              jnp.sum(dp * _shifted(pad_s, 2), axis=0, keepdims=True),
                              jnp.sum(dp * _shifted(pad_s, 1), axis=0, keepdims=True),
                              jnp.sum(dp * x, axis=0, keepdims=True)], CONV_K)
            dcw_ref[slot] += dw
            pad_s[PAD:PAD + t, :] = dp
            dx = _shifted(pad_s, -3) * w_ref[0:1, :]
            dx = dx + _shifted(pad_s, -2) * w_ref[1:2, :]
            dx = dx + _shifted(pad_s, -1) * w_ref[2:3, :]
            return dx + dp * w_ref[3:4, :]

        def l2_bwd(dqn, y, c):
            r = lax.rsqrt(jnp.sum(y * y, axis=1, keepdims=True) + EPS)
            s1 = jnp.sum(dqn * y, axis=1, keepdims=True)
            return c * r * dqn - (c * r * r * r) * s1 * y

        def silu_bwd(p, sg):
            return sg * (1.0 + p * (1.0 - sg))

        pq, sq, yq = prep(q_ref, wq_ref)
        dq_ref[...] = _bf(conv_bwd(l2_bwd(dqh[...], yq, DH ** -0.5) * silu_bwd(pq, sq), q_ref[...], wq_ref, h))
        pk, sk, yk = prep(k_ref, wk_ref)
        dk_ref[...] = _bf(conv_bwd(l2_bwd(dkh[...], yk, 1.0) * silu_bwd(pk, sk), k_ref[...], wk_ref, HEADS + h))
        pv, sv, _ = prep(v_ref, wv_ref)
        dv_ref[...] = _bf(conv_bwd(dvh[...] * silu_bwd(pv, sv), v_ref[...], wv_ref, 2 * HEADS + h))

    blk = pl.BlockSpec((t, DH), lambda b, h: (b, h))
    ob = jax.ShapeDtypeStruct((n, HEADS * DH), BF16)
    return pl.pallas_call(
        body, name="gdn_bwd", grid=(b_loc, HEADS),
        in_specs=[col(0), col(HEADS), col(2 * HEADS), col(3 * HEADS), ps_spec, wcol(0), wcol(HEADS), wcol(2 * HEADS),
                  smem, smem, vec, blk, blk, pl.BlockSpec((1, 1, nc, DH, DH), lambda b, h: (b, h, 0, 0, 0))],
        out_specs=[blk, blk, blk, blk, ps_spec,
                   pl.BlockSpec((3 * HEADS, CONV_K, DH), lambda b, h: (0, 0, 0)),
                   pl.BlockSpec((8, LANES), lambda b, h: (0, 0))],
        out_shape=[ob, ob, ob, ob, jax.ShapeDtypeStruct((n, LANES), F32),
                   jax.ShapeDtypeStruct((3 * HEADS, CONV_K, DH), F32), jax.ShapeDtypeStruct((8, LANES), F32)],
        scratch_shapes=([pltpu.VMEM((t, DH), F32)] * 3 + [pltpu.VMEM((t, LANES), F32), pltpu.VMEM((t, CHUNK), F32)] * 2
                        + [pltpu.VMEM((t, DH), F32)] * 5 + [pltpu.VMEM((t, CHUNK), F32)] * 4
                        + [pltpu.VMEM((nc, DH, DH), F32), pltpu.VMEM((t + 2 * PAD, LANES), F32)]),
        compiler_params=_cparams(("arbitrary", "arbitrary")),
    )(pg, pg, pg, pg, ps, convw, convw, convw, a_log, dt_bias, gnorm, d_oa, o_raw, s_all)


def _fox_prologue(q_ref, k_ref, v_ref, ps_ref, fb_ref, gq_ref, gk_ref, h, t, qs, ks, vs, ccol, crow):
    nb = t // FOX_BLOCK
    q, k = q_ref[...], k_ref[...]
    rq = lax.rsqrt(jnp.mean(q * q, axis=1, keepdims=True) + EPS)
    rk = lax.rsqrt(jnp.mean(k * k, axis=1, keepdims=True) + EPS)
    qs[...] = _bf(q * rq * gq_ref[...])
    ks[...] = _bf(k * rk * gk_ref[...])
    vs[...] = _bf(v_ref[...])
    f_in = _head_lane(ps_ref[...], LANE_FF + h) + fb_ref[0, h]
    ccol[...] = jnp.broadcast_to(-_softplus(-f_in), (t, LANES))
    r = lax.broadcasted_iota(jnp.int32, (FOX_BLOCK, FOX_BLOCK), 0)
    c = lax.broadcasted_iota(jnp.int32, (FOX_BLOCK, FOX_BLOCK), 1)
    trilf, triuf = (r >= c).astype(F32), (r <= c).astype(F32)
    blocks = [pl.ds(j * FOX_BLOCK, FOX_BLOCK) for j in range(nb)]
    lfs = _V([ccol[rb, :] for rb in blocks])
    cc = _dot_exact_l(trilf, lfs, terms=3)
    cr = _dot_exact_r(lfs, triuf, _vdot_tn, terms=3)
    sums = _vsum(lfs, 0)
    carry = jnp.zeros((1, LANES), F32)
    for j, rb in enumerate(blocks):
        ccol[rb, :] = cc.xs[j] + carry
        crow[j] = (cr.xs[j] + carry)[0:8]
        carry = carry + sums.xs[j]
    return rq, rk, f_in


def _fox_scores(q_rows, k_rows, cc, cr, row0, col0):
    s = _dot_nt(q_rows, k_rows) * (DH ** -0.5) + cc - cr
    r = lax.broadcasted_iota(jnp.int32, s.shape, 0)
    c = lax.broadcasted_iota(jnp.int32, s.shape, 1)
    return jnp.where(row0 + r >= col0 + c, s, NEG)


def _fox_specs(t):
    def col(off):
        return pl.BlockSpec((t, DH), lambda b, h: (b, off + h))

    ps_spec = pl.BlockSpec((t, LANES), lambda b, h: (b, 0))
    smem = pl.BlockSpec(memory_space=pltpu.SMEM)
    vec = pl.BlockSpec((1, DH), lambda b, h: (0, 0))
    blk = pl.BlockSpec((t, DH), lambda b, h: (b, h))
    return col, ps_spec, smem, vec, blk


def _fox_fwd(pf, ps, f_bias, gq, gk, b_loc, t):
    n = b_loc * t
    nb = t // FOX_BLOCK
    kt = min(FOX_TILE, t)
    nsub = kt // FOX_BLOCK
    col, ps_spec, smem, vec, blk = _fox_specs(t)

    def body(q_ref, k_ref, v_ref, ps_ref, fb_ref, gq_ref, gk_ref, o_ref, lse_ref, qs, ks, vs, ccol, crow):
        h = pl.program_id(1)
        _fox_prologue(q_ref, k_ref, v_ref, ps_ref, fb_ref, gq_ref, gk_ref, h, t, qs, ks, vs, ccol, crow)

        def qblock(i, _):
            ri = pl.ds(pl.multiple_of(i * FOX_BLOCK, FOX_BLOCK), FOX_BLOCK)
            qi = qs[ri, :]
            cc = jnp.concatenate([ccol[ri, :]] * nsub, axis=1)

            def ktile(j, carry):
                m, l, acc = carry
                rj = pl.ds(pl.multiple_of(j * kt, kt), kt)
                cr = jnp.concatenate([crow[j * nsub + u, 0:1, :] for u in range(nsub)], axis=1)
                s = _fox_scores(qi, ks[rj, :], cc, cr, i * FOX_BLOCK, j * kt)
                m_new = jnp.maximum(m, jnp.max(s, axis=1, keepdims=True))
                p = jnp.exp(s - m_new)
                alpha = jnp.exp(m - m_new)
                l = alpha * l + jnp.sum(p, axis=1, keepdims=True)
                acc = alpha * acc + _dot(_bf(p), vs[rj, :])
                return m_new, l, acc

            m, l, acc = lax.fori_loop(0, (i * FOX_BLOCK) // kt + 1, ktile, (jnp.full((FOX_BLOCK, 1), NEG, F32),
                                                                            jnp.zeros((FOX_BLOCK, 1), F32),
                                                                            jnp.zeros((FOX_BLOCK, DH), F32)))
            o_ref[ri, :] = acc / l
            lse_ref[ri, :] = jnp.broadcast_to(m + jnp.log(l), (FOX_BLOCK, LANES))
            return 0

        lax.fori_loop(0, nb, qblock, 0)

    o = jax.ShapeDtypeStruct((n, HEADS * DH), F32)
    return pl.pallas_call(
        body, name="fox_fwd", grid=(b_loc, HEADS),
        in_specs=[col(0), col(HEADS), col(2 * HEADS), ps_spec, smem, vec, vec],
        out_specs=[blk, blk], out_shape=[o, o],
        scratch_shapes=[pltpu.VMEM((t, DH), BF16)] * 3 + [pltpu.VMEM((t, LANES), F32), pltpu.VMEM((nb, 8, LANES), F32)],
        compiler_params=_cparams(("arbitrary", "arbitrary")),
    )(pf, pf, pf, ps, f_bias, gq, gk)


def _fox_bwd(pf, ps, f_bias, gq, gk, d_ob, ob, lse, dps_in, b_loc, t):
    n = b_loc * t
    nb = t // FOX_BLOCK
    qt = min(FOX_TILE, t)
    scale = DH ** -0.5
    col, ps_spec, smem, vec, blk = _fox_specs(t)

    def body(q_ref, k_ref, v_ref, ps_ref, fb_ref, gq_ref, gk_ref, do_ref, o_ref, lse_ref, dpsi_ref,
             dq_ref, dk_ref, dv_ref, dps_ref, dsm_ref, qs, ks, vs, ccol, crow, dos, dl, dqa, dcr, dcq):
        b, h = pl.program_id(0), pl.program_id(1)
        rq, _, f_in = _fox_prologue(q_ref, k_ref, v_ref, ps_ref, fb_ref, gq_ref, gk_ref, h, t, qs, ks, vs, ccol, crow)
        dov = do_ref[...]
        dos[...] = _bf(dov)
        dl[...] = jnp.broadcast_to(jnp.sum(dov * o_ref[...], axis=1, keepdims=True), (t, LANES))
        dqa[...] = jnp.zeros_like(dqa)
        dcq[...] = jnp.zeros_like(dcq)
        gkv = gk_ref[...]

        def kblock(j, dgk):
            rj = pl.ds(pl.multiple_of(j * FOX_BLOCK, FOX_BLOCK), FOX_BLOCK)
            kj, vj, cr = ks[rj, :], vs[rj, :], crow[j, 0:1, :]

            def qtile(i, carry):
                dk_acc, dv_acc, dc = carry
                ri = pl.ds(pl.multiple_of(i * qt, qt), qt)
                qi, doi = qs[ri, :], dos[ri, :]
                s = _fox_scores(qi, kj, ccol[ri, :], cr, i * qt, j * FOX_BLOCK)
                p = jnp.exp(s - lse_ref[ri, :])
                ds = p * (_dot_nt(doi, vj) - dl[ri, :])
                dsb = _bf(ds)
                dqa[ri, :] += _dot(dsb, kj)
                dcq[ri, :] += jnp.broadcast_to(jnp.sum(ds, axis=1, keepdims=True), (qt, LANES))
                return (dk_acc + _dot_tn(dsb, qi), dv_acc + _dot_tn(_bf(p), doi),
                        dc - jnp.sum(ds, axis=0, keepdims=True))

            z = jnp.zeros((FOX_BLOCK, DH), F32)
            dk_acc, dv_acc, dc = lax.fori_loop((j * FOX_BLOCK) // qt, t // qt, qtile,
                                               (z, z, jnp.zeros((1, LANES), F32)))
            dv_ref[rj, :] = _bf(dv_acc)
            dcr[pl.ds(pl.multiple_of(j * 8, 8), 8), :] = jnp.broadcast_to(dc, (8, LANES))
            kraw = k_ref[rj, :]
            rk = lax.rsqrt(jnp.mean(kraw * kraw, axis=1, keepdims=True) + EPS)
            dkn = dk_acc * scale
            gy = dkn * gkv
            dk_ref[rj, :] = _bf(rk * gy - kraw * (rk * rk * rk * (1.0 / DH)) * jnp.sum(gy * kraw, axis=1, keepdims=True))
            return dgk + jnp.sum(dkn * kraw * rk, axis=0, keepdims=True)

        dgk = lax.fori_loop(0, nb, kblock, jnp.zeros((1, DH), F32))

        q = q_ref[...]
        dqn = dqa[...] * scale
        gy = dqn * gq_ref[...]
        dq_ref[...] = _bf(rq * gy - q * (rq * rq * rq * (1.0 / DH)) * jnp.sum(gy * q, axis=1, keepdims=True))
        dgq = jnp.sum(dqn * q * rq, axis=0, keepdims=True)

        r = lax.broadcasted_iota(jnp.int32, (FOX_BLOCK, FOX_BLOCK), 0)
        c = lax.broadcasted_iota(jnp.int32, (FOX_BLOCK, FOX_BLOCK), 1)
        triuf = (r <= c).astype(F32)

        def rev(jj, carry):
            j = nb - 1 - jj
            rows = pl.ds(pl.multiple_of(j * FOX_BLOCK, FOX_BLOCK), FOX_BLOCK)
            rowv = dcr[pl.ds(pl.multiple_of(j * 8, 8), 1), :]
            colv = jnp.sum(jnp.where(c >= r, jnp.broadcast_to(rowv, (FOX_BLOCK, LANES)), 0.0), axis=1, keepdims=True)
            qcol = dcq[rows, :]
            dl[rows, :] = colv + _dot_exact_l(triuf, qcol, terms=3) + carry
            return carry + jnp.sum(rowv, axis=1, keepdims=True) + jnp.sum(qcol, axis=0, keepdims=True)

        lax.fori_loop(0, nb, rev, jnp.zeros((1, LANES), F32))
        d_ff = dl[...] * _sigmoid(-f_in)
        lane = lax.broadcasted_iota(jnp.int32, (t, LANES), 1)

        @pl.when(h == 0)
        def _():
            dps_ref[...] = dpsi_ref[...]

        dps_ref[...] += jnp.where(lane == LANE_FF + h, d_ff, 0.0)

        lane1 = lax.broadcasted_iota(jnp.int32, (1, LANES), 1)
        d_fb = jnp.sum(d_ff, axis=0, keepdims=True)
        small = _stack_rows([dgq, dgk, jnp.where(lane1 == h, d_fb, 0.0)], 8)

        @pl.when((b == 0) & (h == 0))
        def _():
            dsm_ref[...] = jnp.zeros_like(dsm_ref)

        dsm_ref[...] += small

    ob_ = jax.ShapeDtypeStruct((n, HEADS * DH), BF16)
    return pl.pallas_call(
        body, name="fox_bwd", grid=(b_loc, HEADS),
        in_specs=[col(0), col(HEADS), col(2 * HEADS), ps_spec, smem, vec, vec, blk, blk, blk, ps_spec],
        out_specs=[blk, blk, blk, ps_spec, pl.BlockSpec((8, LANES), lambda b, h: (0, 0))],
        out_shape=[ob_, ob_, ob_, jax.ShapeDtypeStruct((n, LANES), F32), jax.ShapeDtypeStruct((8, LANES), F32)],
        scratch_shapes=([pltpu.VMEM((t, DH), BF16)] * 3 + [pltpu.VMEM((t, LANES), F32), pltpu.VMEM((nb, 8, LANES), F32)]
                        + [pltpu.VMEM((t, DH), BF16), pltpu.VMEM((t, LANES), F32), pltpu.VMEM((t, DH), F32),
                           pltpu.VMEM((8 * nb, LANES), F32), pltpu.VMEM((t, LANES), F32)]),
        compiler_params=_cparams(("arbitrary", "arbitrary")),
    )(pf, pf, pf, ps, f_bias, gq, gk, d_ob, ob, lse, dps_in)


def _local_step(x, target, w, b_loc, t):
    xf = x
    u = _rms_fwd(xf, w["norm_mix_g"], "rms_mix")
    pg = _mm(u, w["w_gdn"], name="proj_gdn")
    pf = _mm(u, w["w_fox"], name="proj_fox")
    pgate = _mm(u, w["w_gate"], name="proj_gate")
    ps = _mm(u, w["w_small"], name="proj_small")
    oa, o_raw, s_all = _gdn_fwd(pg, ps, w["conv_w"], w["a_log"], w["dt_bias"], w["gdn_norm_g"], b_loc, t)
    ob, lse = _fox_fwd(pf, ps, w["f_bias"], w["fox_q_norm_g"], w["fox_k_norm_g"], b_loc, t)
    ya = _mm(oa, w["w_proj_gdn"], name="proj_a")
    yb = _mm(ob, w["w_proj_fox"], name="proj_b")
    merged = _merge_fwd(ya, yb, pgate)
    h = _mm(merged, w["w_out"], name="proj_out", epi=lambda acc, xr: acc + xr, extras=(xf,))
    hn = _rms_fwd(h, w["norm_mlp_g"], "rms_mlp")
    up, act = _mm(hn, w["w_up"], name="mlp_up", out_dtype=BF16, out2=(_relu2, BF16))
    out = _mm(act, w["w_down"], name="mlp_down", epi=lambda acc, hr: acc + hr, extras=(h,))
    d_out, d_out16, loss_blk = _loss_bwd(out, target)

    g = {}
    g["w_down"] = _mm(act, d_out16, name="dw_down", ta=True, out_dtype=BF16)
    d_up = _mm(d_out16, w["w_down"], name="d_up", tb=True, out_dtype=BF16,
               epi=lambda acc, upr: acc * (2.0 * jnp.maximum(upr.astype(F32), 0.0)), extras=(up,))
    g["w_up"] = _mm(hn, d_up, name="dw_up", ta=True, out_dtype=BF16)
    d_hn = _mm(d_up, w["w_up"], name="d_hn", tb=True)
    dh, dh16, g["norm_mlp_g"] = _rms_bwd(d_hn, h, w["norm_mlp_g"], d_out, "rms_mlp_bwd")
    g["w_out"] = _mm(merged, dh16, name="dw_out", ta=True, out_dtype=BF16)
    dm = _mm(dh16, w["w_out"], name="d_merged", tb=True)
    dya, dyb, dgate_a, dgate_b = _merge_bwd(dm, ya, yb, pgate)
    g["w_proj_gdn"] = _mm(oa, dya, name="dw_proj_a", ta=True, out_dtype=BF16)
    g["w_proj_fox"] = _mm(ob, dyb, name="dw_proj_b", ta=True, out_dtype=BF16)
    d_oa = _mm(dya, w["w_proj_gdn"], name="d_oa", tb=True)
    d_ob = _mm(dyb, w["w_proj_fox"], name="d_ob", tb=True)
    dgq, dgk, dgv, dgz, dps, dcw, gdn_small = _gdn_bwd(pg, ps, w["conv_w"], w["a_log"], w["dt_bias"], w["gdn_norm_g"],
                                                       d_oa, o_raw, s_all, b_loc, t)
    dfq, dfk, dfv, dps, fox_small = _fox_bwd(pf, ps, w["f_bias"], w["fox_q_norm_g"], w["fox_k_norm_g"],
                                             d_ob, ob, lse, dps, b_loc, t)
    segs = [(dgq, "w_gdn", 0), (dgk, "w_gdn", 1024), (dgv, "w_gdn", 2048), (dgz, "w_gdn", 3072),
            (dfq, "w_fox", 0), (dfk, "w_fox", 1024), (dfv, "w_fox", 2048),
            (dgate_a, "w_gate", 0), (dgate_b, "w_gate", 1024)]
    du = _mm(dps, w["w_small"], name="du_small", tb=True)
    dws = [_mm(u, dps, name="dw_small", ta=True, out_dtype=BF16)]
    for idx, (dseg, wname, off) in enumerate(segs):
        du = _mm(dseg, w[wname], name=f"du_{idx}", tb=True, b_koff=off,
                 epi=lambda acc, prev: acc + prev, extras=(du,))
        dws.append(_mm(u, dseg, name=f"dw_in_{idx}", ta=True, out_dtype=BF16))
    g["w_in_parts"] = dws
    grad_x, _, g["norm_mix_g"] = _rms_bwd(du, xf, w["norm_mix_g"], dh, "rms_mix_bwd")
    g["conv"] = dcw
    g["gdn_small"] = gdn_small
    g["fox_small"] = fox_small
    return loss_blk, grad_x, g


def _position():
    x, y, c = lax.axis_index("x"), lax.axis_index("y"), lax.axis_index("c")
    return x, y, c


def _to_bf16(arrs, name):
    n = len(arrs)

    def body(*refs):
        for i in range(n):
            refs[n + i][...] = _bf(refs[i][...])

    return pl.pallas_call(
        body, name=name,
        out_shape=[jax.ShapeDtypeStruct(a.shape, BF16) for a in arrs],
        compiler_params=_cparams(),
    )(*arrs)


def _all_gather(arrs, name):
    n = len(arrs)
    hbm = pl.BlockSpec(memory_space=pl.ANY)

    def body(*refs):
        ins, outs = refs[:n], refs[n:2 * n]
        send, recv, loc = refs[2 * n:]
        x, y, c = _position()
        me = 4 * x + 2 * y + c
        sibling = (x, y, 1 - c)
        chips = [(1 - x, y), (x, 1 - y), (1 - x, 1 - y)]

        def idx(px, py, pc):
            return 4 * px + 2 * py + pc

        def cp(a, k, block, to, src=None):
            return pltpu.make_async_remote_copy(
                src_ref=outs[a].at[block] if src is None else src, dst_ref=outs[a].at[block],
                send_sem=send.at[a, k], recv_sem=recv.at[a, k], device_id=to, device_id_type=MESH)

        mine = [pltpu.make_async_copy(ins[a], outs[a].at[me], loc.at[a]) for a in range(n)]
        for m in mine:
            m.start()
        first = []
        for a in range(n):
            first.append(cp(a, 0, me, sibling, src=ins[a]))
            first += [cp(a, 1 + j, me, (*chip, c), src=ins[a]) for j, chip in enumerate(chips)]
        for f in first:
            f.start()
        passed = []
        for j, chip in enumerate(chips):
            for a in range(n):
                cp(a, 1 + j, idx(*chip, c), (x, y, c)).wait_recv()
                p = cp(a, 4 + j, idx(*chip, c), sibling)
                p.start()
                passed.append(p)
        for a in range(n):
            cp(a, 0, idx(x, y, 1 - c), (x, y, c)).wait_recv()
            for j, chip in enumerate(chips):
                cp(a, 4 + j, idx(*chip, 1 - c), (x, y, c)).wait_recv()
        for f in first + passed:
            f.wait_send()
        for m in mine:
            m.wait()

    return pl.pallas_call(
        body, name=name,
        in_specs=[hbm] * n, out_specs=[hbm] * n,
        out_shape=[jax.ShapeDtypeStruct((N_DEV,) + a.shape, a.dtype) for a in arrs],
        scratch_shapes=[pltpu.SemaphoreType.DMA((n, 7)), pltpu.SemaphoreType.DMA((n, 7)), pltpu.SemaphoreType.DMA((n,))],
        compiler_params=pltpu.CompilerParams(has_side_effects=True),
    )(*arrs)


def _peer(x, y, c, rel):
    return ((1 - x) if rel & 4 else x, (1 - y) if rel & 2 else y, (1 - c) if rel & 1 else c)


def _exchange(arrs, name):
    n = len(arrs)
    hbm = pl.BlockSpec(memory_space=pl.ANY)

    def body(*refs):
        ins, outs = refs[:n], refs[n:2 * n]
        send, recv, loc = refs[2 * n:]
        x, y, c = _position()
        me = 4 * x + 2 * y + c
        mine = [pltpu.make_async_copy(ins[a].at[me], outs[a].at[me], loc.at[a]) for a in range(n)]
        for m in mine:
            m.start()
        copies = []
        for rel in range(1, N_DEV):
            px, py, pc = _peer(x, y, c, rel)
            for a in range(n):
                copies.append(pltpu.make_async_remote_copy(
                    src_ref=ins[a].at[4 * px + 2 * py + pc], dst_ref=outs[a].at[me],
                    send_sem=send.at[a, rel - 1], recv_sem=recv.at[a, rel - 1],
                    device_id=(px, py, pc), device_id_type=MESH))
        for cpy in copies:
            cpy.start()
        for cpy in copies:
            cpy.wait()
        for m in mine:
            m.wait()

    return pl.pallas_call(
        body, name=name,
        in_specs=[hbm] * n, out_specs=[hbm] * n,
        out_shape=[jax.ShapeDtypeStruct(a.shape, a.dtype) for a in arrs],
        scratch_shapes=[pltpu.SemaphoreType.DMA((n, 7)), pltpu.SemaphoreType.DMA((n, 7)), pltpu.SemaphoreType.DMA((n,))],
        compiler_params=pltpu.CompilerParams(has_side_effects=True),
    )(*arrs)


def _all_reduce_small(buf, name):
    rows = buf.shape[0]

    def body(in_ref, out_ref, slots, send, recv):
        x, y, c = _position()
        me = 4 * x + 2 * y + c
        slots[me] = in_ref[...]
        copies = []
        for rel in range(1, N_DEV):
            copies.append(pltpu.make_async_remote_copy(
                src_ref=in_ref, dst_ref=slots.at[me], send_sem=send.at[rel - 1], recv_sem=recv.at[rel - 1],
                device_id=_peer(x, y, c, rel), device_id_type=MESH))
        for cpy in copies:
            cpy.start()
        for cpy in copies:
            cpy.wait()
        tot = slots[0]
        for d in range(1, N_DEV):
            tot = tot + slots[d]
        out_ref[...] = tot

    return pl.pallas_call(
        body, name=name,
        out_shape=jax.ShapeDtypeStruct((rows, LANES), F32),
        in_specs=[pl.BlockSpec(memory_space=pltpu.VMEM)], out_specs=pl.BlockSpec(memory_space=pltpu.VMEM),
        scratch_shapes=[pltpu.VMEM((N_DEV, rows, LANES), F32), pltpu.SemaphoreType.DMA((7,)),
                        pltpu.SemaphoreType.DMA((7,))],
        compiler_params=pltpu.CompilerParams(has_side_effects=True),
    )(buf)


def _adam_math(g, w, m, v):
    m = ADAM_B1 * m + (1.0 - ADAM_B1) * g
    v = ADAM_B2 * v + (1.0 - ADAM_B2) * (g * g)
    m_hat = m / (1.0 - ADAM_B1 ** ADAM_STEP)
    v_hat = v / (1.0 - ADAM_B2 ** ADAM_STEP)
    delta = -ADAM_LR * (m_hat / (jnp.sqrt(v_hat) + ADAM_EPS) + ADAM_WD * w)
    return delta, m, v


def _adam_shard(parts, w, m, v, name):
    r, c = w.shape
    tr = min(r, 128)

    def body(p_ref, w_ref, m_ref, v_ref, g_ref, d_ref, nm_ref, nv_ref):
        g = p_ref[0].astype(F32)
        for s in range(1, N_DEV):
            g = g + p_ref[s].astype(F32)
        d, nm, nv = _adam_math(g, w_ref[...], m_ref[...], v_ref[...])
        g_ref[...] = g
        d_ref[...] = d
        nm_ref[...] = nm
        nv_ref[...] = nv

    row = pl.BlockSpec((tr, c), lambda i: (i, 0))
    o = jax.ShapeDtypeStruct((r, c), F32)
    return pl.pallas_call(
        body, name=name, grid=(r // tr,),
        in_specs=[pl.BlockSpec((N_DEV, tr, c), lambda i: (0, i, 0)), row, row, row],
        out_specs=[row] * 4, out_shape=[o] * 4,
        compiler_params=_cparams(("parallel",)),
    )(parts, w, m, v)


def _adam_small(g, w, m, v):
    def body(g_ref, w_ref, m_ref, v_ref, d_ref, nm_ref, nv_ref):
        d, nm, nv = _adam_math(g_ref[...], w_ref[...], m_ref[...], v_ref[...])
        d_ref[...] = d
        nm_ref[...] = nm
        nv_ref[...] = nv

    o = jax.ShapeDtypeStruct(g.shape, F32)
    return pl.pallas_call(body, name="adam_small", out_shape=[o] * 3, compiler_params=_cparams())(g, w, m, v)


def _split_w_in(w_full):
    o = IN_OFF
    w_gdn = w_full[:, o["gq"]:o["ga"]]
    w_fox = w_full[:, o["fq"]:o["ff"]]
    w_gate = w_full[:, o["gate_a"]:o["end"]]
    w_small = jnp.concatenate([w_full[:, o["ga"]:o["fq"]], w_full[:, o["ff"]:o["gate_a"]],
                               jnp.zeros((w_full.shape[0], LANES - 24), w_full.dtype)], axis=1)
    return w_gdn, w_fox, w_gate, w_small


def _join_w_in(parts):
    small = parts[0]
    return jnp.concatenate(parts[1:5] + [small[:, 0:16]] + parts[5:8] + [small[:, 16:24]] + parts[8:10], axis=1)


def _rows128(a, rows):
    flat = a.reshape(-1)
    flat = jnp.concatenate([flat, jnp.zeros((rows * LANES - flat.shape[0],), flat.dtype)])
    return flat.reshape(rows, LANES)


def kernel(x, norm_mix_g, w_in, gdn_conv_w, gdn_a_log, gdn_dt_bias, gdn_norm_g, fox_q_norm_g, fox_k_norm_g, fox_f_bias, w_proj_gdn, w_proj_fox, w_out, norm_mlp_g, w_up, w_down, loss_target, m_norm_mix_g, m_w_in, m_gdn_conv_w, m_gdn_a_log, m_gdn_dt_bias, m_gdn_norm_g, m_fox_q_norm_g, m_fox_k_norm_g, m_fox_f_bias, m_w_proj_gdn, m_w_proj_fox, m_w_out, m_norm_mlp_g, m_w_up, m_w_down, v_norm_mix_g, v_w_in, v_gdn_conv_w, v_gdn_a_log, v_gdn_dt_bias, v_gdn_norm_g, v_fox_q_norm_g, v_fox_k_norm_g, v_fox_f_bias, v_w_proj_gdn, v_w_proj_fox, v_w_out, v_norm_mlp_g, v_w_up, v_w_down):
    b_loc, t, d = x.shape
    n = b_loc * t
    me = 4 * lax.axis_index("x") + 2 * lax.axis_index("y") + lax.axis_index("c")

    big = [w_in[0], w_proj_gdn[0], w_proj_fox[0], w_out[0], w_up[0], w_down[0]]
    big16 = _to_bf16(big, "weights_to_bf16")
    gathered = _all_gather(list(big16) + [gdn_conv_w[0]], "gather_weights")
    g_in, g_pa, g_pb, g_out, g_up, g_down, g_conv = gathered
    w_full = g_in.transpose(1, 0, 2).reshape(d, N_DEV * w_in.shape[2])
    w_gdn, w_fox, w_gate, w_small = _split_w_in(w_full)
    weights = {
        "w_gdn": w_gdn, "w_fox": w_fox, "w_gate": w_gate, "w_small": w_small,
        "w_proj_gdn": g_pa.reshape(d, d), "w_proj_fox": g_pb.reshape(d, d), "w_out": g_out.reshape(d, d),
        "w_up": g_up.transpose(1, 0, 2).reshape(d, D_FF), "w_down": g_down.reshape(D_FF, d),
        "conv_w": g_conv.transpose(1, 0, 2).reshape(CONV_K, 3 * d),
        "norm_mix_g": norm_mix_g, "norm_mlp_g": norm_mlp_g, "a_log": gdn_a_log, "dt_bias": gdn_dt_bias,
        "gdn_norm_g": gdn_norm_g, "fox_q_norm_g": fox_q_norm_g, "fox_k_norm_g": fox_k_norm_g, "f_bias": fox_f_bias,
    }

    loss_blk, grad_x, g = _local_step(x.reshape(n, d), loss_target.reshape(n, d), weights, b_loc, t)

    c_in = w_in.shape[2]
    gw_in = _join_w_in(g["w_in_parts"]).reshape(d, N_DEV, c_in).transpose(1, 0, 2)
    c_up = w_up.shape[2]
    send = [gw_in, g["w_proj_gdn"].reshape(N_DEV, d // N_DEV, d), g["w_proj_fox"].reshape(N_DEV, d // N_DEV, d),
            g["w_out"].reshape(N_DEV, d // N_DEV, d), g["w_up"].reshape(d, N_DEV, c_up).transpose(1, 0, 2),
            g["w_down"].reshape(N_DEV, D_FF // N_DEV, d)]
    recv = _exchange(send, "exchange_grads")
    shards = [(w_in, m_w_in, v_w_in), (w_proj_gdn, m_w_proj_gdn, v_w_proj_gdn), (w_proj_fox, m_w_proj_fox, v_w_proj_fox),
              (w_out, m_w_out, v_w_out), (w_up, m_w_up, v_w_up), (w_down, m_w_down, v_w_down)]
    big_out = []
    for i, (parts, (wi, mi, vi)) in enumerate(zip(recv, shards)):
        res = _adam_shard(parts, wi[0], mi[0], vi[0], f"adam_{i}")
        big_out.append([r[None] for r in res])

    conv_rows = CONV_K * 3 * d // LANES
    conv_g = g["conv"].transpose(1, 0, 2).reshape(conv_rows, LANES)
    buf = jnp.concatenate([conv_g, g["norm_mix_g"].reshape(8, LANES), g["norm_mlp_g"].reshape(8, LANES),
                           g["gdn_small"], g["fox_small"], loss_blk], axis=0)
    tot = _all_reduce_small(buf, "all_reduce_small")
    o = conv_rows
    conv_full = tot[0:o].reshape(CONV_K, 3 * d)
    c_conv = gdn_conv_w.shape[2]
    g_conv_shard = lax.dynamic_slice(conv_full, (0, me * c_conv), (CONV_K, c_conv))
    g_mix = tot[o:o + 8].reshape(1, d)
    g_mlp = tot[o + 8:o + 16].reshape(1, d)
    gs, fs = tot[o + 16:o + 24], tot[o + 24:o + 32]
    loss = tot[o + 32, 0]
    small_g = [g_mix, g_conv_shard[None], gs[0:1, 0:HEADS], gs[1:2, 0:HEADS], gs[2:3], fs[0:1], fs[1:2], fs[2:3, 0:HEADS],
               g_mlp]
    small_w = [norm_mix_g, gdn_conv_w, gdn_a_log, gdn_dt_bias, gdn_norm_g, fox_q_norm_g, fox_k_norm_g, fox_f_bias,
               norm_mlp_g]
    small_m = [m_norm_mix_g, m_gdn_conv_w, m_gdn_a_log, m_gdn_dt_bias, m_gdn_norm_g, m_fox_q_norm_g, m_fox_k_norm_g,
               m_fox_f_bias, m_norm_mlp_g]
    small_v = [v_norm_mix_g, v_gdn_conv_w, v_gdn_a_log, v_gdn_dt_bias, v_gdn_norm_g, v_fox_q_norm_g, v_fox_k_norm_g,
               v_fox_f_bias, v_norm_mlp_g]
    row_counts = [-(-a.size // (8 * LANES)) * 8 for a in small_w]

    def pack(arrs):
        return jnp.concatenate([_rows128(a, rc) for a, rc in zip(arrs, row_counts)], axis=0)

    sd, sm, sv = _adam_small(pack(small_g), pack(small_w), pack(small_m), pack(small_v))

    def unpack(p):
        outs, r0 = [], 0
        for a, rc in zip(small_w, row_counts):
            outs.append(p[r0:r0 + rc].reshape(-1)[:a.size].reshape(a.shape))
            r0 += rc
        return outs

    small_out = [small_g_i.reshape(w_i.shape) for small_g_i, w_i in zip(small_g, small_w)], unpack(sd), unpack(sm), unpack(sv)

    def ordered(kind):
        s = small_out[kind]
        bo = [b[kind] for b in big_out]
        return [s[0], bo[0], s[1], s[2], s[3], s[4], s[5], s[6], s[7], bo[1], bo[2], bo[3], s[8], bo[4], bo[5]]

    return (loss, grad_x.reshape(b_loc, t, d), *ordered(0), *ordered(1), *ordered(2), *ordered(3))
```

```python
import functools

import jax
import jax.numpy as jnp
from jax import lax
from jax.experimental import pallas as pl
from jax.experimental.pallas import tpu as pltpu

F32 = jnp.float32
BF16 = jnp.bfloat16
HI = lax.Precision.HIGHEST
MESH = pl.DeviceIdType.MESH

N_DEV = 8
D_MODEL = 1024
HEADS = 8
DH = 128
CONV_K = 4
CHUNK = 64
GDN_GROUP = 4
FOX_BLOCK = 128
FOX_TILE = 512
D_FF = 4 * D_MODEL
EPS = 1e-6
LANES = 128
NEG = -1e30
IN_OFF = {"gq": 0, "gk": 1024, "gv": 2048, "gz": 3072, "ga": 4096, "gb": 4104, "fq": 4112, "fk": 5136,
          "fv": 6160, "ff": 7184, "gate_a": 7192, "gate_b": 8216, "end": 9240}
LANE_GA, LANE_GB, LANE_FF = 0, 8, 16

ADAM_LR = 0.001
ADAM_B1 = 0.9
ADAM_B2 = 0.999
ADAM_EPS = 1e-08
ADAM_WD = 0.01
ADAM_STEP = 10

VMEM_LIMIT = 56 * 1024 * 1024


def _cparams(sem=None):
    return pltpu.CompilerParams(dimension_semantics=sem, vmem_limit_bytes=VMEM_LIMIT)


def _sigmoid(x):
    return 1.0 / (1.0 + jnp.exp(-x))


def _softplus(x):
    return jnp.maximum(x, 0.0) + jnp.log(1.0 + jnp.exp(-jnp.abs(x)))


def _dot(a, b, prec=None):
    return lax.dot_general(a, b, (((1,), (0,)), ((), ())), precision=prec, preferred_element_type=F32)


def _dot_nt(a, b, prec=None):
    return lax.dot_general(a, b, (((1,), (1,)), ((), ())), precision=prec, preferred_element_type=F32)


def _dot_tn(a, b, prec=None):
    return lax.dot_general(a, b, (((0,), (0,)), ((), ())), precision=prec, preferred_element_type=F32)


def _bf(x):
    return x.astype(BF16)


MM_TILE = 1024


def _mm(a, b, *, name, ta=False, tb=False, out_dtype=F32, epi=None, extras=(), out2=None,
        b_koff=0, tm=MM_TILE, tn=MM_TILE, tk=MM_TILE):
    m = a.shape[1] if ta else a.shape[0]
    kdim = a.shape[0] if ta else a.shape[1]
    n = b.shape[0] if tb else b.shape[1]
    tm, tn, tk = min(tm, m), min(tn, n), min(tk, kdim)
    nk = kdim // tk
    grid = (m // tm, n // tn, nk)
    koff = b_koff // tk
    a_spec = pl.BlockSpec((tk, tm), lambda i, j, k: (k, i)) if ta else pl.BlockSpec((tm, tk), lambda i, j, k: (i, k))
    if tb:
        b_spec = pl.BlockSpec((tn, tk), lambda i, j, k: (j, k + koff))
    else:
        b_spec = pl.BlockSpec((tk, tn), lambda i, j, k: (k + koff, j))
    o_spec = pl.BlockSpec((tm, tn), lambda i, j, k: (i, j))
    n_e = len(extras)
    n_o = 1 if out2 is None else 2
    dims = (((0 if ta else 1,), (1 if tb else 0,)), ((), ()))

    def body(a_ref, b_ref, *rest):
        e_refs, o_refs = rest[:n_e], rest[n_e:n_e + n_o]
        prod = lax.dot_general(_bf(a_ref[...]), _bf(b_ref[...]), dims, preferred_element_type=F32)

        def finish(r):
            if out2 is not None:
                o_refs[1][...] = out2[0](r).astype(out2[1])
            if epi is not None:
                r = epi(r, *[e[...] for e in e_refs])
            o_refs[0][...] = r.astype(out_dtype)

        if nk == 1:
            finish(prod)
        else:
            acc = rest[n_e + n_o]
            k = pl.program_id(2)

            @pl.when(k == 0)
            def _():
                acc[...] = prod

            @pl.when(k > 0)
            def _():
                acc[...] += prod

            @pl.when(k == nk - 1)
            def _():
                finish(acc[...])

    shapes = [jax.ShapeDtypeStruct((m, n), out_dtype)]
    if out2 is not None:
        shapes.append(jax.ShapeDtypeStruct((m, n), out2[1]))
    res = pl.pallas_call(
        body, name=name, grid=grid,
        in_specs=[a_spec, b_spec] + [o_spec] * n_e,
        out_specs=[o_spec] * n_o, out_shape=shapes,
        scratch_shapes=[] if nk == 1 else [pltpu.VMEM((tm, tn), F32)],
        compiler_params=_cparams(("parallel", "parallel", "arbitrary")),
    )(a, b, *extras)
    return res[0] if out2 is None else res


def _relu2(x):
    r = jnp.maximum(x, 0.0)
    return r * r


ROWS = 512


def _rms_fwd(x, g, name):
    n, d = x.shape

    def body(x_ref, g_ref, u_ref):
        xv = x_ref[...]
        r = lax.rsqrt(jnp.mean(xv * xv, axis=1, keepdims=True) + EPS)
        u_ref[...] = _bf(xv * r * g_ref[...])

    return pl.pallas_call(
        body, name=name, grid=(n // ROWS,),
        in_specs=[pl.BlockSpec((ROWS, d), lambda i: (i, 0)), pl.BlockSpec((1, d), lambda i: (0, 0))],
        out_specs=pl.BlockSpec((ROWS, d), lambda i: (i, 0)),
        out_shape=jax.ShapeDtypeStruct((n, d), BF16),
        compiler_params=_cparams(("parallel",)),
    )(x, g)


def _rms_bwd(dy, x, g, dres, name):
    n, d = x.shape

    def body(dy_ref, x_ref, g_ref, dres_ref, dx_ref, dx16_ref, dg_ref):
        i = pl.program_id(0)
        xv, dyv = x_ref[...], dy_ref[...]
        r = lax.rsqrt(jnp.mean(xv * xv, axis=1, keepdims=True) + EPS)
        gy = dyv * g_ref[...]
        s = jnp.sum(gy * xv, axis=1, keepdims=True)
        dx = dres_ref[...] + r * gy - xv * (r * r * r * (1.0 / d)) * s
        dx_ref[...] = dx
        dx16_ref[...] = _bf(dx)

        @pl.when(i == 0)
        def _():
            dg_ref[...] = jnp.zeros_like(dg_ref)

        dg_ref[...] += jnp.sum(dyv * xv * r, axis=0, keepdims=True)

    row = pl.BlockSpec((ROWS, d), lambda i: (i, 0))
    vec = pl.BlockSpec((1, d), lambda i: (0, 0))
    return pl.pallas_call(
        body, name=name, grid=(n // ROWS,),
        in_specs=[row, row, vec, row], out_specs=[row, row, vec],
        out_shape=[jax.ShapeDtypeStruct((n, d), F32), jax.ShapeDtypeStruct((n, d), BF16),
                   jax.ShapeDtypeStruct((1, d), F32)],
        compiler_params=_cparams(("arbitrary",)),
    )(dy, x, g, dres)


def _merge_fwd(ya, yb, gate):
    n, d = ya.shape

    def body(ya_ref, yb_ref, ga_ref, gb_ref, o_ref):
        o_ref[...] = _bf(_sigmoid(ga_ref[...]) * ya_ref[...] + _sigmoid(gb_ref[...]) * yb_ref[...])

    row = pl.BlockSpec((ROWS, d), lambda i: (i, 0))
    return pl.pallas_call(
        body, name="merge_fwd", grid=(n // ROWS,),
        in_specs=[row, row, row, pl.BlockSpec((ROWS, d), lambda i: (i, 1))], out_specs=row,
        out_shape=jax.ShapeDtypeStruct((n, d), BF16),
        compiler_params=_cparams(("parallel",)),
    )(ya, yb, gate, gate)


def _merge_bwd(dm, ya, yb, gate):
    n, d = ya.shape

    def body(dm_ref, ya_ref, yb_ref, ga_ref, gb_ref, dya_ref, dyb_ref, dga_ref, dgb_ref):
        dmv = dm_ref[...]
        sa, sb = _sigmoid(ga_ref[...]), _sigmoid(gb_ref[...])
        dya_ref[...] = _bf(dmv * sa)
        dyb_ref[...] = _bf(dmv * sb)
        dga_ref[...] = _bf(dmv * ya_ref[...] * sa * (1.0 - sa))
        dgb_ref[...] = _bf(dmv * yb_ref[...] * sb * (1.0 - sb))

    row = pl.BlockSpec((ROWS, d), lambda i: (i, 0))
    o = jax.ShapeDtypeStruct((n, d), BF16)
    return pl.pallas_call(
        body, name="merge_bwd", grid=(n // ROWS,),
        in_specs=[row, row, row, row, pl.BlockSpec((ROWS, d), lambda i: (i, 1))], out_specs=[row] * 4,
        out_shape=[o] * 4,
        compiler_params=_cparams(("parallel",)),
    )(dm, ya, yb, gate, gate)


def _loss_bwd(out, target):
    n, d = out.shape

    def body(o_ref, t_ref, d_ref, d16_ref, l_ref):
        i = pl.program_id(0)
        err = o_ref[...] - t_ref[...]
        d_ref[...] = err * (1.0 / d)
        d16_ref[...] = _bf(err * (1.0 / d))

        @pl.when(i == 0)
        def _():
            l_ref[...] = jnp.zeros_like(l_ref)

        l_ref[...] += 0.5 * jnp.sum(jnp.mean(err * err, axis=1, keepdims=True), axis=0, keepdims=True)

    row = pl.BlockSpec((ROWS, d), lambda i: (i, 0))
    return pl.pallas_call(
        body, name="loss_bwd", grid=(n // ROWS,),
        in_specs=[row, row], out_specs=[row, row, pl.BlockSpec((8, LANES), lambda i: (0, 0))],
        out_shape=[jax.ShapeDtypeStruct((n, d), F32), jax.ShapeDtypeStruct((n, d), BF16),
                   jax.ShapeDtypeStruct((8, LANES), F32)],
        compiler_params=_cparams(("arbitrary",)),
    )(out, target)


PAD = 8


def _pad_zero(pad_ref):
    t = pad_ref.shape[0] - 2 * PAD
    pad_ref[0:PAD, :] = jnp.zeros((PAD, LANES), F32)
    pad_ref[PAD + t:2 * PAD + t, :] = jnp.zeros((PAD, LANES), F32)


def _shifted(pad_ref, s):
    t = pad_ref.shape[0] - 2 * PAD
    return pad_ref[PAD - s:PAD - s + t, :]


def _conv(x, w_ref, pad_ref):
    t = x.shape[0]
    pad_ref[PAD:PAD + t, :] = x
    y = _shifted(pad_ref, 3) * w_ref[0:1, :]
    y = y + _shifted(pad_ref, 2) * w_ref[1:2, :]
    y = y + _shifted(pad_ref, 1) * w_ref[2:3, :]
    return y + x * w_ref[3:4, :]


def _chunk_consts():
    r = lax.broadcasted_iota(jnp.int32, (CHUNK, CHUNK), 0)
    c = lax.broadcasted_iota(jnp.int32, (CHUNK, CHUNK), 1)
    incl, strict = r >= c, r > c
    return dict(incl=incl, strict=strict, trilf=incl.astype(F32), triuf=(r <= c).astype(F32),
                eye=(r == c).astype(F32))


class _V:
    def __init__(self, xs):
        self.xs = list(xs)

    def __add__(self, o):
        return _ap(lambda x, y: x + y, self, o)

    def __radd__(self, o):
        return _ap(lambda x, y: y + x, self, o)

    def __sub__(self, o):
        return _ap(lambda x, y: x - y, self, o)

    def __rsub__(self, o):
        return _ap(lambda x, y: y - x, self, o)

    def __mul__(self, o):
        return _ap(lambda x, y: x * y, self, o)

    def __rmul__(self, o):
        return _ap(lambda x, y: y * x, self, o)

    def __neg__(self):
        return _ap(lambda x: -x, self)

    def __getitem__(self, idx):
        return _ap(lambda x: x[idx], self)


def _ap(fn, *args):
    n = [len(a.xs) for a in args if isinstance(a, _V)]
    if not n:
        return fn(*args)
    return _V([fn(*[a.xs[i] if isinstance(a, _V) else a for a in args]) for i in range(n[0])])


def _vbf(x):
    return _ap(_bf, x)


def _vdot(a, b):
    return _ap(_dot, a, b)


def _vdot_nt(a, b):
    return _ap(_dot_nt, a, b)


def _vdot_tn(a, b):
    return _ap(_dot_tn, a, b)


def _vexp(x):
    return _ap(jnp.exp, x)


def _vsum(x, axis):
    return _ap(lambda v: jnp.sum(v, axis=axis, keepdims=True), x)


def _vcat(a, b, axis):
    return _ap(lambda x, y: jnp.concatenate([x, y], axis=axis), a, b)


def _vmask(mask, x):
    return _ap(lambda v: jnp.where(mask, v, 0.0), x)


def _split2(x):
    h = _vbf(x)
    return h, _vbf(x - _ap(lambda v: v.astype(F32), h))


def _dot3(a, b, kind=_vdot):
    ah, al = _split2(a)
    bh, bl = _split2(b)
    return kind(ah, bh) + (kind(ah, bl) + kind(al, bh))


def _split(x, terms):
    out = []
    for _ in range(terms):
        h = _vbf(x)
        out.append(h)
        x = x - _ap(lambda v: v.astype(F32), h)
    return out


def _dot_exact_l(m01, x, kind=_vdot, terms=2):
    mb = _bf(m01)
    parts = [kind(mb, xp) for xp in _split(x, terms)]
    return functools.reduce(lambda a, b: a + b, reversed(parts))


def _dot_exact_r(x, m01, kind=_vdot, terms=2):
    mb = _bf(m01)
    parts = [kind(xp, mb) for xp in _split(x, terms)]
    return functools.reduce(lambda a, b: a + b, reversed(parts))


def _inv_unit_lower(a, eye):
    p = -a
    r = p + eye
    p = _dot3(p, p)
    for j in range(1, 6):
        if j < 5:
            y = _dot3(p, _vcat(p, r, 1))
            p, r = y[:, 0:CHUNK], r + y[:, CHUNK:2 * CHUNK]
        else:
            r = r + _dot3(p, r)
    return r


def _gdn_chunk_pre(q, k, v, g128, g64, b128, b64, cs):
    incl = cs["incl"]
    big_g = _dot_exact_l(cs["trilf"], g128)
    gc = big_g[:, 0:CHUNK]
    gr = _dot_exact_r(g64, cs["triuf"], _vdot_tn)
    decay = _ap(lambda d: jnp.where(incl, jnp.exp(jnp.where(incl, d, 0.0)), 0.0), gc - gr)
    kb, qb = _vbf(k), _vbf(q)
    qkk = _vdot_nt(_vcat(qb, kb, 0), kb)
    qk, kk = qkk[0:CHUNK], qkk[CHUNK:2 * CHUNK]
    tm = _inv_unit_lower(_vmask(cs["strict"], b64 * kk * decay), cs["eye"])
    e_g = _vexp(big_g)
    wu = _dot3(tm, _vcat(v * b128, k * (b128 * e_g), 1))
    w, u = wu[:, 0:DH], wu[:, DH:2 * DH]
    g_last = _vsum(g128, 0)
    return dict(big_g=big_g, decay=decay, kk=kk, qk=qk, tm=tm, w=w, u=u, p=qk * decay, q_dec=q * e_g,
                k_dec=k * _vexp(g_last - big_g), dec=_vexp(g_last))


def _gdn_chunk_post(q, k, v, g128, b128, b64, s, ds_next, do, dv_new, big_g, decay, kk, qk, tm, u, v_new, cs):
    e_g = _vexp(big_g)
    vb = v * b128
    kbeta = k * (b128 * e_g)
    q_dec = q * e_g
    g_last = _vsum(g128, 0)
    ekg = _vexp(g_last - big_g)
    k_dec = k * ekg
    dec = _vexp(g_last)
    kb, qb, sb = _vbf(k), _vbf(q), _vbf(s)
    dob, dsb, vnb, dvnb = _vbf(do), _vbf(ds_next), _vbf(v_new), _vbf(dv_new)
    dp = _vmask(cs["incl"], _vdot_nt(dob, vnb))
    dq_dec = _vdot_nt(dob, sb)
    du = -_vdot_nt(dvnb, sb)
    ddec = _vsum(_vsum(s * ds_next, 1), 0)
    dk_dec = _vdot_nt(vnb, dsb)
    dwu = _vcat(dv_new, du, 1)
    dt = _dot3(dwu, _vcat(vb, kbeta, 1), _vdot_nt)
    dvk = _dot3(tm, dwu, _vdot_tn)
    dvb, dkbeta = dvk[:, 0:DH], dvk[:, DH:2 * DH]
    da = _vmask(cs["strict"], -_dot3(tm, _dot3(dt, tm, _vdot_nt), _vdot_tn))
    dkk = _vbf(da * b64 * decay)
    dqk = _vbf(dp * decay)
    ddd = (da * b64 * kk + dp * qk) * decay
    dq = _vdot(dqk, kb) + dq_dec * e_g
    dk = _vdot_tn(dqk, qb) + _vdot(dkk, kb) + _vdot_tn(dkk, kb) + dk_dec * ekg + dkbeta * (b128 * e_g)
    dv = dvb * b128
    dbeta = _vsum(da * kk * decay, 1) + _vsum(dvb * v, 1) + _vsum(dkbeta * k * e_g, 1)
    s_k = _vsum(dk_dec * k_dec, 1)
    dg_col = _vsum(ddd, 1) + _vsum(dq_dec * q_dec, 1) - s_k + _vsum(dkbeta * kbeta, 1)
    colsum = _dot_exact_r(ddd, jnp.ones((CHUNK, LANES), F32), _vdot_tn)
    dg_last = _vsum(s_k, 0) + ddec * dec
    dg = _dot_exact_l(cs["triuf"], dg_col - colsum) + dg_last
    return dq, dk, dv, dg, dbeta


def _stack_rows(vecs, nrows):
    row = lax.broadcasted_iota(jnp.int32, (nrows, LANES), 0)
    out = jnp.zeros((nrows, LANES), F32)
    for i, v in enumerate(vecs):
        out = out + jnp.where(row == i, jnp.broadcast_to(v, (nrows, LANES)), 0.0)
    return out


def _head_lane(x, lane_idx):
    lane = lax.broadcasted_iota(jnp.int32, x.shape, 1)
    return jnp.sum(jnp.where(lane == lane_idx, x, 0.0), axis=1, keepdims=True)


def _gdn_gates(ps, h, alog_ref, dtb_ref):
    ga = _head_lane(ps, LANE_GA + h)
    gb = _head_lane(ps, LANE_GB + h)
    a = jnp.exp(jnp.full((1, 1), alog_ref[0, h], F32))
    sp_in = ga + dtb_ref[0, h]
    g = -a * _softplus(sp_in)
    return g, _sigmoid(gb), a, sp_in


def _gdn_specs(b_loc, t):
    def col(off):
        return pl.BlockSpec((t, DH), lambda b, h: (b, off + h))

    ps_spec = pl.BlockSpec((t, LANES), lambda b, h: (b, 0))

    def wcol(off):
        return pl.BlockSpec((CONV_K, DH), lambda b, h: (0, off + h))

    smem = pl.BlockSpec(memory_space=pltpu.SMEM)
    vec = pl.BlockSpec((1, DH), lambda b, h: (0, 0))
    return col, ps_spec, wcol, smem, vec


def _gdn_fwd(pg, ps, convw, a_log, dt_bias, gnorm, b_loc, t):
    n = b_loc * t
    nc = t // CHUNK
    col, ps_spec, wcol, smem, vec = _gdn_specs(b_loc, t)

    def body(q_ref, k_ref, v_ref, z_ref, ps_ref, wq_ref, wk_ref, wv_ref, alog_ref, dtb_ref, gn_ref,
             oa_ref, oraw_ref, s_ref, qn, kn, vv, g128, g64, b128, b64, uq_s, p_s, kd_s, dec_s, pad_s):
        h = pl.program_id(1)
        g, beta, _, _ = _gdn_gates(ps_ref[...], h, alog_ref, dtb_ref)
        g128[...] = jnp.broadcast_to(g, (t, LANES))
        g64[...] = jnp.broadcast_to(g, (t, CHUNK))
        b128[...] = jnp.broadcast_to(beta, (t, LANES))
        b64[...] = jnp.broadcast_to(beta, (t, CHUNK))
        _pad_zero(pad_s)
        pq = _conv(q_ref[...], wq_ref, pad_s)
        yq = pq * _sigmoid(pq)
        qn[...] = yq * (lax.rsqrt(jnp.sum(yq * yq, axis=1, keepdims=True) + EPS) * (DH ** -0.5))
        pk = _conv(k_ref[...], wk_ref, pad_s)
        yk = pk * _sigmoid(pk)
        kn[...] = yk * lax.rsqrt(jnp.sum(yk * yk, axis=1, keepdims=True) + EPS)
        pv = _conv(v_ref[...], wv_ref, pad_s)
        vv[...] = pv * _sigmoid(pv)
        cs = _chunk_consts()

        def pre_group(gi, _):
            idx = [gi * GDN_GROUP + c for c in range(GDN_GROUP)]
            rows = [pl.ds(pl.multiple_of(i * CHUNK, CHUNK), CHUNK) for i in idx]
            ins = [_V([ref[r, :] for r in rows]) for ref in (qn, kn, vv, g128, g64, b128, b64)]
            f = _gdn_chunk_pre(*ins, cs)
            for c, (i, r) in enumerate(zip(idx, rows)):
                vv[r, :] = f["w"].xs[c]
                uq_s[i, 0:CHUNK, :] = _bf(f["u"].xs[c])
                uq_s[i, CHUNK:2 * CHUNK, :] = _bf(f["q_dec"].xs[c])
                p_s[r, :] = _bf(f["p"].xs[c])
                kd_s[r, :] = _bf(f["k_dec"].xs[c])
                dec_s[pl.ds(pl.multiple_of(i * 8, 8), 8), :] = jnp.broadcast_to(f["dec"].xs[c], (8, LANES))
            return 0

        lax.fori_loop(0, nc // GDN_GROUP, pre_group, 0)

        def chunk(i, s):
            r = pl.ds(pl.multiple_of(i * CHUNK, CHUNK), CHUNK)
            us = _dot(uq_s[i], _bf(s))
            vnb = _bf(vv[r, :] - us[0:CHUNK])
            oraw_ref[r, :] = us[CHUNK:2 * CHUNK] + _dot(p_s[r, :], vnb)
            s_ref[0, 0, i] = s
            return s * dec_s[pl.ds(pl.multiple_of(i * 8, 8), 1), :] + _dot_tn(kd_s[r, :], vnb)

        lax.fori_loop(0, nc, chunk, jnp.zeros((DH, DH), F32))
        o = oraw_ref[...]
        rr = lax.rsqrt(jnp.mean(o * o, axis=1, keepdims=True) + EPS)
        z = z_ref[...]
        oa_ref[...] = _bf((o * rr * gn_ref[...]) * (z * _sigmoid(z)))

    return pl.pallas_call(
        body, name="gdn_fwd", grid=(b_loc, HEADS),
        in_specs=[col(0), col(HEADS), col(2 * HEADS), col(3 * HEADS), ps_spec, wcol(0), wcol(HEADS), wcol(2 * HEADS),
                  smem, smem, vec],
        out_specs=[pl.BlockSpec((t, DH), lambda b, h: (b, h)), pl.BlockSpec((t, DH), lambda b, h: (b, h)),
                   pl.BlockSpec((1, 1, nc, DH, DH), lambda b, h: (b, h, 0, 0, 0))],
        out_shape=[jax.ShapeDtypeStruct((n, HEADS * DH), BF16), jax.ShapeDtypeStruct((n, HEADS * DH), F32),
                   jax.ShapeDtypeStruct((b_loc, HEADS, nc, DH, DH), F32)],
        scratch_shapes=([pltpu.VMEM((t, DH), F32)] * 3 + [pltpu.VMEM((t, LANES), F32), pltpu.VMEM((t, CHUNK), F32)] * 2
                        + [pltpu.VMEM((nc, 2 * CHUNK, DH), BF16), pltpu.VMEM((t, CHUNK), BF16), pltpu.VMEM((t, DH), BF16),
                           pltpu.VMEM((8 * nc, LANES), F32), pltpu.VMEM((t + 2 * PAD, LANES), F32)]),
        compiler_params=_cparams(("arbitrary", "arbitrary")),
    )(pg, pg, pg, pg, ps, convw, convw, convw, a_log, dt_bias, gnorm)


def _gdn_bwd(pg, ps, convw, a_log, dt_bias, gnorm, d_oa, o_raw, s_all, b_loc, t):
    n = b_loc * t
    nc = t // CHUNK
    col, ps_spec, wcol, smem, vec = _gdn_specs(b_loc, t)

    def body(q_ref, k_ref, v_ref, z_ref, ps_ref, wq_ref, wk_ref, wv_ref, alog_ref, dtb_ref, gn_ref,
             doa_ref, oraw_ref, s_ref,
             dq_ref, dk_ref, dv_ref, dz_ref, dps_ref, dcw_ref, dsm_ref,
             qn, kn, vv, g128, g64, b128, b64, do_s, bg_s, u_s, vn_s, dvn_s, dcy_s, kk_s, qk_s, tm_s, dsn_s, pad_s):
        b, h = pl.program_id(0), pl.program_id(1)
        g, beta, _, _ = _gdn_gates(ps_ref[...], h, alog_ref, dtb_ref)
        g128[...] = jnp.broadcast_to(g, (t, LANES))
        g64[...] = jnp.broadcast_to(g, (t, CHUNK))
        b128[...] = jnp.broadcast_to(beta, (t, LANES))
        b64[...] = jnp.broadcast_to(beta, (t, CHUNK))
        _pad_zero(pad_s)

        def prep(x_ref, w_ref):
            p = _conv(x_ref[...], w_ref, pad_s)
            sg = _sigmoid(p)
            return p, sg, p * sg

        _, _, yq = prep(q_ref, wq_ref)
        qn[...] = yq * (lax.rsqrt(jnp.sum(yq * yq, axis=1, keepdims=True) + EPS) * (DH ** -0.5))
        _, _, yk = prep(k_ref, wk_ref)
        kn[...] = yk * lax.rsqrt(jnp.sum(yk * yk, axis=1, keepdims=True) + EPS)
        _, _, yv = prep(v_ref, wv_ref)
        vv[...] = yv

        o = oraw_ref[...]
        z = z_ref[...]
        doa = doa_ref[...]
        gn = gn_ref[...]
        ro = lax.rsqrt(jnp.mean(o * o, axis=1, keepdims=True) + EPS)
        sz = _sigmoid(z)
        dz_ref[...] = _bf(doa * (o * ro * gn) * (sz * (1.0 + z * (1.0 - sz))))
        dn = doa * (z * sz)
        dgn = jnp.sum(dn * o * ro, axis=0, keepdims=True)
        gy = dn * gn
        do_s[...] = ro * gy - o * (ro * ro * ro * (1.0 / DH)) * jnp.sum(gy * o, axis=1, keepdims=True)

        cs = _chunk_consts()

        def pre_group(gi, _):
            idx = [gi * GDN_GROUP + c for c in range(GDN_GROUP)]
            rows = [pl.ds(pl.multiple_of(i * CHUNK, CHUNK), CHUNK) for i in idx]
            ins = [_V([ref[r, :] for r in rows]) for ref in (qn, kn, vv, g128, g64, b128, b64)]
            states = _V([_bf(s_ref[0, 0, i]) for i in idx])
            f = _gdn_chunk_pre(*ins, cs)
            v_new = f["w"] - _vdot(_vbf(f["u"]), states)
            for c, r in enumerate(rows):
                bg_s[r, :] = f["big_g"].xs[c]
                u_s[r, :] = f["u"].xs[c]
                vn_s[r, :] = v_new.xs[c]
                dcy_s[r, :] = f["decay"].xs[c]
                kk_s[r, :] = f["kk"].xs[c]
                qk_s[r, :] = f["qk"].xs[c]
                tm_s[r, :] = f["tm"].xs[c]
            return 0

        lax.fori_loop(0, nc // GDN_GROUP, pre_group, 0)

        def chunk(j, ds):
            i = nc - 1 - j
            r = pl.ds(pl.multiple_of(i * CHUNK, CHUNK), CHUNK)
            big_g = bg_s[r, :]
            g_last = jnp.sum(g128[r, :], axis=0, keepdims=True)
            dob = _bf(do_s[r, :])
            dv_new = (_dot_tn(_bf(qk_s[r, :] * dcy_s[r, :]), dob)
                      + _dot(_bf(kn[r, :] * jnp.exp(g_last - big_g)), _bf(ds)))
            dvn_s[r, :] = dv_new
            dsn_s[i] = ds
            return (_dot_tn(_bf(qn[r, :] * jnp.exp(big_g)), dob) + jnp.exp(g_last) * ds
                    - _dot_tn(_bf(u_s[r, :]), _bf(dv_new)))

        lax.fori_loop(0, nc, chunk, jnp.zeros((DH, DH), F32))

        def post_group(gi, _):
            idx = [gi * GDN_GROUP + c for c in range(GDN_GROUP)]
            rows = [pl.ds(pl.multiple_of(i * CHUNK, CHUNK), CHUNK) for i in idx]
            def rows_of(ref):
                return _V([ref[r, :] for r in rows])

            dq, dk, dv, dg, dbeta = _gdn_chunk_post(
                rows_of(qn), rows_of(kn), rows_of(vv), rows_of(g128), rows_of(b128), rows_of(b64),
                _V([s_ref[0, 0, i] for i in idx]), _V([dsn_s[i] for i in idx]), rows_of(do_s), rows_of(dvn_s),
                rows_of(bg_s), rows_of(dcy_s), rows_of(kk_s), rows_of(qk_s), rows_of(tm_s), rows_of(u_s), rows_of(vn_s),
                cs)
            for c, r in enumerate(rows):
                qn[r, :] = dq.xs[c]
                kn[r, :] = dk.xs[c]
                vv[r, :] = dv.xs[c]
                g128[r, :] = dg.xs[c]
                b128[r, :] = jnp.broadcast_to(dbeta.xs[c], (CHUNK, LANES))
            return 0

        lax.fori_loop(0, nc // GDN_GROUP, post_group, 0)
        dqh, dkh, dvh = qn, kn, vv

        g, beta, a, sp_in = _gdn_gates(ps_ref[...], h, alog_ref, dtb_ref)
        dg = g128[...]
        d_ga = dg * (-a) * _sigmoid(sp_in)
        d_alog = jnp.sum(dg * g, axis=0, keepdims=True)
        d_dtb = jnp.sum(d_ga, axis=0, keepdims=True)
        d_gb = b128[...] * (beta * (1.0 - beta))
        lane = lax.broadcasted_iota(jnp.int32, (t, LANES), 1)
        contrib = jnp.where(lane == LANE_GA + h, d_ga, 0.0) + jnp.where(lane == LANE_GB + h, d_gb, 0.0)

        @pl.when(h == 0)
        def _():
            dps_ref[...] = jnp.zeros_like(dps_ref)

        dps_ref[...] += contrib

        lane1 = lax.broadcasted_iota(jnp.int32, (1, LANES), 1)
        small = _stack_rows([jnp.where(lane1 == h, d_alog, 0.0), jnp.where(lane1 == h, d_dtb, 0.0), dgn], 8)

        @pl.when((b == 0) & (h == 0))
        def _():
            dsm_ref[...] = jnp.zeros_like(dsm_ref)
            dcw_ref[...] = jnp.zeros_like(dcw_ref)

        dsm_ref[...] += small

        def conv_bwd(dp, x, w_ref, slot):
            dw = _stack_rows([jnp.sum(dp * _shifted(pad_s, 3), axis=0, keepdims=True),
                              jnp.sum(dp * _shifted(pad_s, 2), axis=0, keepdims=True),
                              jnp.sum(dp * _shifted(pad_s, 1), axis=0, keepdims=True),
                              jnp.sum(dp * x, axis=0, keepdims=True)], CONV_K)
            dcw_ref[slot] += dw
            pad_s[PAD:PAD + t, :] = dp
            dx = _shifted(pad_s, -3) * w_ref[0:1, :]
            dx = dx + _shifted(pad_s, -2) * w_ref[1:2, :]
            dx = dx + _shifted(pad_s, -1) * w_ref[2:3, :]
            return dx + dp * w_ref[3:4, :]

        def l2_bwd(dqn, y, c):
            r = lax.rsqrt(jnp.sum(y * y, axis=1, keepdims=True) + EPS)
            s1 = jnp.sum(dqn * y, axis=1, keepdims=True)
            return c * r * dqn - (c * r * r * r) * s1 * y

        def silu_bwd(p, sg):
            return sg * (1.0 + p * (1.0 - sg))

        pq, sq, yq = prep(q_ref, wq_ref)
        dq_ref[...] = _bf(conv_bwd(l2_bwd(dqh[...], yq, DH ** -0.5) * silu_bwd(pq, sq), q_ref[...], wq_ref, h))
        pk, sk, yk = prep(k_ref, wk_ref)
        dk_ref[...] = _bf(conv_bwd(l2_bwd(dkh[...], yk, 1.0) * silu_bwd(pk, sk), k_ref[...], wk_ref, HEADS + h))
        pv, sv, _ = prep(v_ref, wv_ref)
        dv_ref[...] = _bf(conv_bwd(dvh[...] * silu_bwd(pv, sv), v_ref[...], wv_ref, 2 * HEADS + h))

    blk = pl.BlockSpec((t, DH), lambda b, h: (b, h))
    ob = jax.ShapeDtypeStruct((n, HEADS * DH), BF16)
    return pl.pallas_call(
        body, name="gdn_bwd", grid=(b_loc, HEADS),
        in_specs=[col(0), col(HEADS), col(2 * HEADS), col(3 * HEADS), ps_spec, wcol(0), wcol(HEADS), wcol(2 * HEADS),
                  smem, smem, vec, blk, blk, pl.BlockSpec((1, 1, nc, DH, DH), lambda b, h: (b, h, 0, 0, 0))],
        out_specs=[blk, blk, blk, blk, ps_spec,
                   pl.BlockSpec((3 * HEADS, CONV_K, DH), lambda b, h: (0, 0, 0)),
                   pl.BlockSpec((8, LANES), lambda b, h: (0, 0))],
        out_shape=[ob, ob, ob, ob, jax.ShapeDtypeStruct((n, LANES), F32),
                   jax.ShapeDtypeStruct((3 * HEADS, CONV_K, DH), F32), jax.ShapeDtypeStruct((8, LANES), F32)],
        scratch_shapes=([pltpu.VMEM((t, DH), F32)] * 3 + [pltpu.VMEM((t, LANES), F32), pltpu.VMEM((t, CHUNK), F32)] * 2
                        + [pltpu.VMEM((t, DH), F32)] * 5 + [pltpu.VMEM((t, CHUNK), F32)] * 4
                        + [pltpu.VMEM((nc, DH, DH), F32), pltpu.VMEM((t + 2 * PAD, LANES), F32)]),
        compiler_params=_cparams(("arbitrary", "arbitrary")),
    )(pg, pg, pg, pg, ps, convw, convw, convw, a_log, dt_bias, gnorm, d_oa, o_raw, s_all)


def _fox_prologue(q_ref, k_ref, v_ref, ps_ref, fb_ref, gq_ref, gk_ref, h, t, qs, ks, vs, ccol, crow):
    nb = t // FOX_BLOCK
    q, k = q_ref[...], k_ref[...]
    rq = lax.rsqrt(jnp.mean(q * q, axis=1, keepdims=True) + EPS)
    rk = lax.rsqrt(jnp.mean(k * k, axis=1, keepdims=True) + EPS)
    qs[...] = _bf(q * rq * gq_ref[...])
    ks[...] = _bf(k * rk * gk_ref[...])
    vs[...] = _bf(v_ref[...])
    f_in = _head_lane(ps_ref[...], LANE_FF + h) + fb_ref[0, h]
    ccol[...] = jnp.broadcast_to(-_softplus(-f_in), (t, LANES))
    r = lax.broadcasted_iota(jnp.int32, (FOX_BLOCK, FOX_BLOCK), 0)
    c = lax.broadcasted_iota(jnp.int32, (FOX_BLOCK, FOX_BLOCK), 1)
    trilf, triuf = (r >= c).astype(F32), (r <= c).astype(F32)
    blocks = [pl.ds(j * FOX_BLOCK, FOX_BLOCK) for j in range(nb)]
    lfs = _V([ccol[rb, :] for rb in blocks])
    cc = _dot_exact_l(trilf, lfs, terms=3)
    cr = _dot_exact_r(lfs, triuf, _vdot_tn, terms=3)
    sums = _vsum(lfs, 0)
    carry = jnp.zeros((1, LANES), F32)
    for j, rb in enumerate(blocks):
        ccol[rb, :] = cc.xs[j] + carry
        crow[j] = (cr.xs[j] + carry)[0:8]
        carry = carry + sums.xs[j]
    return rq, rk, f_in


def _fox_scores(q_rows, k_rows, cc, cr, row0, col0):
    s = _dot_nt(q_rows, k_rows) * (DH ** -0.5) + cc - cr
    r = lax.broadcasted_iota(jnp.int32, s.shape, 0)
    c = lax.broadcasted_iota(jnp.int32, s.shape, 1)
    return jnp.where(row0 + r >= col0 + c, s, NEG)


def _fox_specs(t):
    def col(off):
        return pl.BlockSpec((t, DH), lambda b, h: (b, off + h))

    ps_spec = pl.BlockSpec((t, LANES), lambda b, h: (b, 0))
    smem = pl.BlockSpec(memory_space=pltpu.SMEM)
    vec = pl.BlockSpec((1, DH), lambda b, h: (0, 0))
    blk = pl.BlockSpec((t, DH), lambda b, h: (b, h))
    return col, ps_spec, smem, vec, blk


def _fox_fwd(pf, ps, f_bias, gq, gk, b_loc, t):
    n = b_loc * t
    nb = t // FOX_BLOCK
    kt = min(FOX_TILE, t)
    nsub = kt // FOX_BLOCK
    col, ps_spec, smem, vec, blk = _fox_specs(t)

    def body(q_ref, k_ref, v_ref, ps_ref, fb_ref, gq_ref, gk_ref, o_ref, lse_ref, qs, ks, vs, ccol, crow):
        h = pl.program_id(1)
        _fox_prologue(q_ref, k_ref, v_ref, ps_ref, fb_ref, gq_ref, gk_ref, h, t, qs, ks, vs, ccol, crow)

        def qblock(i, _):
            ri = pl.ds(pl.multiple_of(i * FOX_BLOCK, FOX_BLOCK), FOX_BLOCK)
            qi = qs[ri, :]
            cc = jnp.concatenate([ccol[ri, :]] * nsub, axis=1)

            def ktile(j, carry):
                m, l, acc = carry
                rj = pl.ds(pl.multiple_of(j * kt, kt), kt)
                cr = jnp.concatenate([crow[j * nsub + u, 0:1, :] for u in range(nsub)], axis=1)
                s = _fox_scores(qi, ks[rj, :], cc, cr, i * FOX_BLOCK, j * kt)
                m_new = jnp.maximum(m, jnp.max(s, axis=1, keepdims=True))
                p = jnp.exp(s - m_new)
                alpha = jnp.exp(m - m_new)
                l = alpha * l + jnp.sum(p, axis=1, keepdims=True)
                acc = alpha * acc + _dot(_bf(p), vs[rj, :])
                return m_new, l, acc

            m, l, acc = lax.fori_loop(0, (i * FOX_BLOCK) // kt + 1, ktile, (jnp.full((FOX_BLOCK, 1), NEG, F32),
                                                                            jnp.zeros((FOX_BLOCK, 1), F32),
                                                                            jnp.zeros((FOX_BLOCK, DH), F32)))
            o_ref[ri, :] = acc / l
            lse_ref[ri, :] = jnp.broadcast_to(m + jnp.log(l), (FOX_BLOCK, LANES))
            return 0

        lax.fori_loop(0, nb, qblock, 0)

    o = jax.ShapeDtypeStruct((n, HEADS * DH), F32)
    return pl.pallas_call(
        body, name="fox_fwd", grid=(b_loc, HEADS),
        in_specs=[col(0), col(HEADS), col(2 * HEADS), ps_spec, smem, vec, vec],
        out_specs=[blk, blk], out_shape=[o, o],
        scratch_shapes=[pltpu.VMEM((t, DH), BF16)] * 3 + [pltpu.VMEM((t, LANES), F32), pltpu.VMEM((nb, 8, LANES), F32)],
        compiler_params=_cparams(("arbitrary", "arbitrary")),
    )(pf, pf, pf, ps, f_bias, gq, gk)


def _fox_bwd(pf, ps, f_bias, gq, gk, d_ob, ob, lse, dps_in, b_loc, t):
    n = b_loc * t
    nb = t // FOX_BLOCK
    qt = min(FOX_TILE, t)
    scale = DH ** -0.5
    col, ps_spec, smem, vec, blk = _fox_specs(t)

    def body(q_ref, k_ref, v_ref, ps_ref, fb_ref, gq_ref, gk_ref, do_ref, o_ref, lse_ref, dpsi_ref,
             dq_ref, dk_ref, dv_ref, dps_ref, dsm_ref, qs, ks, vs, ccol, crow, dos, dl, dqa, dcr, dcq):
        b, h = pl.program_id(0), pl.program_id(1)
        rq, _, f_in = _fox_prologue(q_ref, k_ref, v_ref, ps_ref, fb_ref, gq_ref, gk_ref, h, t, qs, ks, vs, ccol, crow)
        dov = do_ref[...]
        dos[...] = _bf(dov)
        dl[...] = jnp.broadcast_to(jnp.sum(dov * o_ref[...], axis=1, keepdims=True), (t, LANES))
        dqa[...] = jnp.zeros_like(dqa)
        dcq[...] = jnp.zeros_like(dcq)
        gkv = gk_ref[...]

        def kblock(j, dgk):
            rj = pl.ds(pl.multiple_of(j * FOX_BLOCK, FOX_BLOCK), FOX_BLOCK)
            kj, vj, cr = ks[rj, :], vs[rj, :], crow[j, 0:1, :]

            def qtile(i, carry):
                dk_acc, dv_acc, dc = carry
                ri = pl.ds(pl.multiple_of(i * qt, qt), qt)
                qi, doi = qs[ri, :], dos[ri, :]
                s = _fox_scores(qi, kj, ccol[ri, :], cr, i * qt, j * FOX_BLOCK)
                p = jnp.exp(s - lse_ref[ri, :])
                ds = p * (_dot_nt(doi, vj) - dl[ri, :])
                dsb = _bf(ds)
                dqa[ri, :] += _dot(dsb, kj)
                dcq[ri, :] += jnp.broadcast_to(jnp.sum(ds, axis=1, keepdims=True), (qt, LANES))
                return (dk_acc + _dot_tn(dsb, qi), dv_acc + _dot_tn(_bf(p), doi),
                        dc - jnp.sum(ds, axis=0, keepdims=True))

            z = jnp.zeros((FOX_BLOCK, DH), F32)
            dk_acc, dv_acc, dc = lax.fori_loop((j * FOX_BLOCK) // qt, t // qt, qtile,
                                               (z, z, jnp.zeros((1, LANES), F32)))
            dv_ref[rj, :] = _bf(dv_acc)
            dcr[pl.ds(pl.multiple_of(j * 8, 8), 8), :] = jnp.broadcast_to(dc, (8, LANES))
            kraw = k_ref[rj, :]
            rk = lax.rsqrt(jnp.mean(kraw * kraw, axis=1, keepdims=True) + EPS)
            dkn = dk_acc * scale
            gy = dkn * gkv
            dk_ref[rj, :] = _bf(rk * gy - kraw * (rk * rk * rk * (1.0 / DH)) * jnp.sum(gy * kraw, axis=1, keepdims=True))
            return dgk + jnp.sum(dkn * kraw * rk, axis=0, keepdims=True)

        dgk = lax.fori_loop(0, nb, kblock, jnp.zeros((1, DH), F32))

        q = q_ref[...]
        dqn = dqa[...] * scale
        gy = dqn * gq_ref[...]
        dq_ref[...] = _bf(rq * gy - q * (rq * rq * rq * (1.0 / DH)) * jnp.sum(gy * q, axis=1, keepdims=True))
        dgq = jnp.sum(dqn * q * rq, axis=0, keepdims=True)

        r = lax.broadcasted_iota(jnp.int32, (FOX_BLOCK, FOX_BLOCK), 0)
        c = lax.broadcasted_iota(jnp.int32, (FOX_BLOCK, FOX_BLOCK), 1)
        triuf = (r <= c).astype(F32)

        def rev(jj, carry):
            j = nb - 1 - jj
            rows = pl.ds(pl.multiple_of(j * FOX_BLOCK, FOX_BLOCK), FOX_BLOCK)
            rowv = dcr[pl.ds(pl.multiple_of(j * 8, 8), 1), :]
            colv = jnp.sum(jnp.where(c >= r, jnp.broadcast_to(rowv, (FOX_BLOCK, LANES)), 0.0), axis=1, keepdims=True)
            qcol = dcq[rows, :]
            dl[rows, :] = colv + _dot_exact_l(triuf, qcol, terms=3) + carry
            return carry + jnp.sum(rowv, axis=1, keepdims=True) + jnp.sum(qcol, axis=0, keepdims=True)

        lax.fori_loop(0, nb, rev, jnp.zeros((1, LANES), F32))
        d_ff = dl[...] * _sigmoid(-f_in)
        lane = lax.broadcasted_iota(jnp.int32, (t, LANES), 1)

        @pl.when(h == 0)
        def _():
            dps_ref[...] = dpsi_ref[...]

        dps_ref[...] += jnp.where(lane == LANE_FF + h, d_ff, 0.0)

        lane1 = lax.broadcasted_iota(jnp.int32, (1, LANES), 1)
        d_fb = jnp.sum(d_ff, axis=0, keepdims=True)
        small = _stack_rows([dgq, dgk, jnp.where(lane1 == h, d_fb, 0.0)], 8)

        @pl.when((b == 0) & (h == 0))
        def _():
            dsm_ref[...] = jnp.zeros_like(dsm_ref)

        dsm_ref[...] += small

    ob_ = jax.ShapeDtypeStruct((n, HEADS * DH), BF16)
    return pl.pallas_call(
        body, name="fox_bwd", grid=(b_loc, HEADS),
        in_specs=[col(0), col(HEADS), col(2 * HEADS), ps_spec, smem, vec, vec, blk, blk, blk, ps_spec],
        out_specs=[blk, blk, blk, ps_spec, pl.BlockSpec((8, LANES), lambda b, h: (0, 0))],
        out_shape=[ob_, ob_, ob_, jax.ShapeDtypeStruct((n, LANES), F32), jax.ShapeDtypeStruct((8, LANES), F32)],
        scratch_shapes=([pltpu.VMEM((t, DH), BF16)] * 3 + [pltpu.VMEM((t, LANES), F32), pltpu.VMEM((nb, 8, LANES), F32)]
                        + [pltpu.VMEM((t, DH), BF16), pltpu.VMEM((t, LANES), F32), pltpu.VMEM((t, DH), F32),
                           pltpu.VMEM((8 * nb, LANES), F32), pltpu.VMEM((t, LANES), F32)]),
        compiler_params=_cparams(("arbitrary", "arbitrary")),
    )(pf, pf, pf, ps, f_bias, gq, gk, d_ob, ob, lse, dps_in)


class _NoExchange:
    def late_weights(self, after):
        return {}

    def grads_ready(self, grads, tie):
        return tie


def _local_step(x, target, w, b_loc, t, comm=None):
    comm = comm or _NoExchange()
    w = dict(w)
    xf = x
    u = _rms_fwd(xf, w["norm_mix_g"], "rms_mix")
    pg = _mm(u, w["w_gdn"], name="proj_gdn")
    pf = _mm(u, w["w_fox"], name="proj_fox")
    pgate = _mm(u, w["w_gate"], name="proj_gate")
    ps = _mm(u, w["w_small"], name="proj_small")
    oa, o_raw, s_all = _gdn_fwd(pg, ps, w["conv_w"], w["a_log"], w["dt_bias"], w["gdn_norm_g"], b_loc, t)
    ob, lse = _fox_fwd(pf, ps, w["f_bias"], w["fox_q_norm_g"], w["fox_k_norm_g"], b_loc, t)
    w.update(comm.late_weights(ob))
    ya = _mm(oa, w["w_proj_gdn"], name="proj_a")
    yb = _mm(ob, w["w_proj_fox"], name="proj_b")
    merged = _merge_fwd(ya, yb, pgate)
    h = _mm(merged, w["w_out"], name="proj_out", epi=lambda acc, xr: acc + xr, extras=(xf,))
    hn = _rms_fwd(h, w["norm_mlp_g"], "rms_mlp")
    up, act = _mm(hn, w["w_up"], name="mlp_up", out_dtype=BF16, out2=(_relu2, BF16))
    out = _mm(act, w["w_down"], name="mlp_down", epi=lambda acc, hr: acc + hr, extras=(h,))
    d_out, d_out16, loss_blk = _loss_bwd(out, target)

    g = {}
    g["w_down"] = _mm(act, d_out16, name="dw_down", ta=True, out_dtype=BF16)
    d_up = _mm(d_out16, w["w_down"], name="d_up", tb=True, out_dtype=BF16,
               epi=lambda acc, upr: acc * (2.0 * jnp.maximum(upr.astype(F32), 0.0)), extras=(up,))
    g["w_up"] = _mm(hn, d_up, name="dw_up", ta=True, out_dtype=BF16)
    mlp_gain = comm.grads_ready({"w_down": g["w_down"], "w_up": g["w_up"]}, w["norm_mlp_g"])
    d_hn = _mm(d_up, w["w_up"], name="d_hn", tb=True)
    dh, dh16, g["norm_mlp_g"] = _rms_bwd(d_hn, h, mlp_gain, d_out, "rms_mlp_bwd")
    g["w_out"] = _mm(merged, dh16, name="dw_out", ta=True, out_dtype=BF16)
    dm = _mm(dh16, w["w_out"], name="d_merged", tb=True)
    dya, dyb, dgate_a, dgate_b = _merge_bwd(dm, ya, yb, pgate)
    g["w_proj_gdn"] = _mm(oa, dya, name="dw_proj_a", ta=True, out_dtype=BF16)
    g["w_proj_fox"] = _mm(ob, dyb, name="dw_proj_b", ta=True, out_dtype=BF16)
    gdn_gain = comm.grads_ready({"w_out": g["w_out"], "w_proj_gdn": g["w_proj_gdn"], "w_proj_fox": g["w_proj_fox"]},
                                w["gdn_norm_g"])
    d_oa = _mm(dya, w["w_proj_gdn"], name="d_oa", tb=True)
    d_ob = _mm(dyb, w["w_proj_fox"], name="d_ob", tb=True)
    dgq, dgk, dgv, dgz, dps, dcw, gdn_small = _gdn_bwd(pg, ps, w["conv_w"], w["a_log"], w["dt_bias"], gdn_gain,
                                                       d_oa, o_raw, s_all, b_loc, t)
    dfq, dfk, dfv, dps, fox_small = _fox_bwd(pf, ps, w["f_bias"], w["fox_q_norm_g"], w["fox_k_norm_g"],
                                             d_ob, ob, lse, dps, b_loc, t)
    segs = [(dgq, "w_gdn", 0), (dgk, "w_gdn", 1024), (dgv, "w_gdn", 2048), (dgz, "w_gdn", 3072),
            (dfq, "w_fox", 0), (dfk, "w_fox", 1024), (dfv, "w_fox", 2048),
            (dgate_a, "w_gate", 0), (dgate_b, "w_gate", 1024)]
    dws = [_mm(u, dps, name="dw_small", ta=True, out_dtype=BF16)]
    dws += [_mm(u, dseg, name=f"dw_in_{idx}", ta=True, out_dtype=BF16) for idx, (dseg, _, _) in enumerate(segs)]
    g["w_in_parts"] = dws
    mix_gain = comm.grads_ready({"w_in_parts": dws}, w["norm_mix_g"])
    du = _mm(dps, w["w_small"], name="du_small", tb=True)
    for idx, (dseg, wname, off) in enumerate(segs):
        du = _mm(dseg, w[wname], name=f"du_{idx}", tb=True, b_koff=off,
                 epi=lambda acc, prev: acc + prev, extras=(du,))
    grad_x, _, g["norm_mix_g"] = _rms_bwd(du, xf, mix_gain, dh, "rms_mix_bwd")
    g["conv"] = dcw
    g["gdn_small"] = gdn_small
    g["fox_small"] = fox_small
    return loss_blk, grad_x, g


def _position():
    x, y, c = lax.axis_index("x"), lax.axis_index("y"), lax.axis_index("c")
    return x, y, c


def _to_bf16(arrs, name):
    n = len(arrs)

    def body(*refs):
        for i in range(n):
            refs[n + i][...] = _bf(refs[i][...])

    return pl.pallas_call(
        body, name=name,
        out_shape=[jax.ShapeDtypeStruct(a.shape, BF16) for a in arrs],
        compiler_params=_cparams(),
    )(*arrs)


def _all_gather(arrs, name):
    n = len(arrs)
    hbm = pl.BlockSpec(memory_space=pl.ANY)

    def body(*refs):
        ins, outs = refs[:n], refs[n:2 * n]
        send, recv, loc = refs[2 * n:]
        x, y, c = _position()
        me = 4 * x + 2 * y + c
        sibling = (x, y, 1 - c)
        chips = [(1 - x, y), (x, 1 - y), (1 - x, 1 - y)]

        def idx(px, py, pc):
            return 4 * px + 2 * py + pc

        def cp(a, k, block, to, src=None):
            return pltpu.make_async_remote_copy(
                src_ref=outs[a].at[block] if src is None else src, dst_ref=outs[a].at[block],
                send_sem=send.at[a, k], recv_sem=recv.at[a, k], device_id=to, device_id_type=MESH)

        mine = [pltpu.make_async_copy(ins[a], outs[a].at[me], loc.at[a]) for a in range(n)]
        for m in mine:
            m.start()
        first = []
        for a in range(n):
            first.append(cp(a, 0, me, sibling, src=ins[a]))
            first += [cp(a, 1 + j, me, (*chip, c), src=ins[a]) for j, chip in enumerate(chips)]
        for f in first:
            f.start()
        passed = []
        for j, chip in enumerate(chips):
            for a in range(n):
                cp(a, 1 + j, idx(*chip, c), (x, y, c)).wait_recv()
                p = cp(a, 4 + j, idx(*chip, c), sibling)
                p.start()
                passed.append(p)
        for a in range(n):
            cp(a, 0, idx(x, y, 1 - c), (x, y, c)).wait_recv()
            for j, chip in enumerate(chips):
                cp(a, 4 + j, idx(*chip, 1 - c), (x, y, c)).wait_recv()
        for f in first + passed:
            f.wait_send()
        for m in mine:
            m.wait()

    return pl.pallas_call(
        body, name=name,
        in_specs=[hbm] * n, out_specs=[hbm] * n,
        out_shape=[jax.ShapeDtypeStruct((N_DEV,) + a.shape, a.dtype) for a in arrs],
        scratch_shapes=[pltpu.SemaphoreType.DMA((n, 7)), pltpu.SemaphoreType.DMA((n, 7)), pltpu.SemaphoreType.DMA((n,))],
        compiler_params=pltpu.CompilerParams(has_side_effects=True),
    )(*arrs)


def _peer(x, y, c, rel):
    return ((1 - x) if rel & 4 else x, (1 - y) if rel & 2 else y, (1 - c) if rel & 1 else c)


def _exchange(arrs, name):
    n = len(arrs)
    hbm = pl.BlockSpec(memory_space=pl.ANY)

    def body(*refs):
        ins, outs = refs[:n], refs[n:2 * n]
        send, recv, loc = refs[2 * n:]
        x, y, c = _position()
        me = 4 * x + 2 * y + c
        mine = [pltpu.make_async_copy(ins[a].at[me], outs[a].at[me], loc.at[a]) for a in range(n)]
        for m in mine:
            m.start()
        copies = []
        for rel in range(1, N_DEV):
            px, py, pc = _peer(x, y, c, rel)
            for a in range(n):
                copies.append(pltpu.make_async_remote_copy(
                    src_ref=ins[a].at[4 * px + 2 * py + pc], dst_ref=outs[a].at[me],
                    send_sem=send.at[a, rel - 1], recv_sem=recv.at[a, rel - 1],
                    device_id=(px, py, pc), device_id_type=MESH))
        for cpy in copies:
            cpy.start()
        for cpy in copies:
            cpy.wait()
        for m in mine:
            m.wait()

    return pl.pallas_call(
        body, name=name,
        in_specs=[hbm] * n, out_specs=[hbm] * n,
        out_shape=[jax.ShapeDtypeStruct(a.shape, a.dtype) for a in arrs],
        scratch_shapes=[pltpu.SemaphoreType.DMA((n, 7)), pltpu.SemaphoreType.DMA((n, 7)), pltpu.SemaphoreType.DMA((n,))],
        compiler_params=pltpu.CompilerParams(has_side_effects=True),
    )(*arrs)


HBM_SPEC = pl.BlockSpec(memory_space=pltpu.HBM)
SEM_SPEC = pl.BlockSpec(memory_space=pltpu.SEMAPHORE)
DATAFLOW = pltpu.SideEffectType.DATAFLOW_SIDE_EFFECTING


def _push_start(arrs, slots, name):
    n = len(arrs)
    land_shapes = [a.shape if slots else (N_DEV,) + a.shape for a in arrs]

    def body(*refs):
        ins, lands, sends, recvs, token = refs[:n], refs[n:2 * n], refs[2 * n:3 * n], refs[3 * n:4 * n], refs[-1]
        x, y, c = _position()
        me = 4 * x + 2 * y + c
        for rel in range(1, N_DEV):
            px, py, pc = _peer(x, y, c, rel)
            for a in range(n):
                pltpu.make_async_remote_copy(
                    src_ref=ins[a].at[4 * px + 2 * py + pc] if slots else ins[a], dst_ref=lands[a].at[me],
                    send_sem=sends[a], recv_sem=recvs[a], device_id=(px, py, pc), device_id_type=MESH).start()
        token[...] = jnp.zeros_like(token)

    sem = pltpu.SemaphoreType.DMA(())
    outs = pl.pallas_call(
        body, name=name,
        in_specs=[HBM_SPEC] * (2 * n),
        out_shape=(*[sem] * (2 * n), *[pltpu.HBM(a.shape, a.dtype) for a in arrs],
                   *[pltpu.HBM(s, a.dtype) for s, a in zip(land_shapes, arrs)], jax.ShapeDtypeStruct((8, LANES), F32)),
        out_specs=(*[SEM_SPEC] * (2 * n), *[HBM_SPEC] * (2 * n), pl.BlockSpec(memory_space=pltpu.VMEM)),
        input_output_aliases={i: 2 * n + i for i in range(2 * n)},
        compiler_params=pltpu.CompilerParams(has_side_effects=DATAFLOW),
    )(*[pltpu.with_memory_space_constraint(a, pltpu.HBM) for a in arrs],
      *[pltpu.with_memory_space_constraint(lax.empty(s, a.dtype), pltpu.HBM) for s, a in zip(land_shapes, arrs)])
    return dict(sends=list(outs[:n]), recvs=list(outs[n:2 * n]), ins=list(outs[2 * n:3 * n]),
                lands=list(outs[3 * n:4 * n]), token=outs[-1])


def _push_wait(started, after, name):
    n = len(started["ins"])

    def body(*refs):
        lands, sends, recvs = refs[n:2 * n], refs[2 * n:3 * n], refs[3 * n:4 * n]
        x, y, c = _position()
        for a in range(n):
            seven = lands[a].at[pl.ds(0, N_DEV - 1)]
            drain = pltpu.make_async_remote_copy(src_ref=seven, dst_ref=seven, send_sem=sends[a], recv_sem=recvs[a],
                                                 device_id=(x, y, c), device_id_type=MESH)
            drain.wait_send()
            drain.wait_recv()

    both = started["ins"] + started["lands"]
    outs = pl.pallas_call(
        body, name=name,
        in_specs=[HBM_SPEC] * (2 * n) + [SEM_SPEC] * (2 * n) + [pl.BlockSpec(memory_space=pl.ANY)],
        out_shape=tuple(pltpu.HBM(a.shape, a.dtype) for a in both), out_specs=tuple([HBM_SPEC] * (2 * n)),
        input_output_aliases={i: i for i in range(2 * n)},
        compiler_params=pltpu.CompilerParams(has_side_effects=DATAFLOW),
    )(*both, *started["sends"], *started["recvs"], after)
    return list(outs[:n]), list(outs[n:])


def _all_reduce_small(buf, name):
    rows = buf.shape[0]

    def body(in_ref, out_ref, slots, send, recv):
        x, y, c = _position()
        me = 4 * x + 2 * y + c
        slots[me] = in_ref[...]
        copies = []
        for rel in range(1, N_DEV):
            copies.append(pltpu.make_async_remote_copy(
                src_ref=in_ref, dst_ref=slots.at[me], send_sem=send.at[rel - 1], recv_sem=recv.at[rel - 1],
                device_id=_peer(x, y, c, rel), device_id_type=MESH))
        for cpy in copies:
            cpy.start()
        for cpy in copies:
            cpy.wait()
        tot = slots[0]
        for d in range(1, N_DEV):
            tot = tot + slots[d]
        out_ref[...] = tot

    return pl.pallas_call(
        body, name=name,
        out_shape=jax.ShapeDtypeStruct((rows, LANES), F32),
        in_specs=[pl.BlockSpec(memory_space=pltpu.VMEM)], out_specs=pl.BlockSpec(memory_space=pltpu.VMEM),
        scratch_shapes=[pltpu.VMEM((N_DEV, rows, LANES), F32), pltpu.SemaphoreType.DMA((7,)),
                        pltpu.SemaphoreType.DMA((7,))],
        compiler_params=pltpu.CompilerParams(has_side_effects=True),
    )(buf)


def _adam_math(g, w, m, v):
    m = ADAM_B1 * m + (1.0 - ADAM_B1) * g
    v = ADAM_B2 * v + (1.0 - ADAM_B2) * (g * g)
    m_hat = m / (1.0 - ADAM_B1 ** ADAM_STEP)
    v_hat = v / (1.0 - ADAM_B2 ** ADAM_STEP)
    delta = -ADAM_LR * (m_hat / (jnp.sqrt(v_hat) + ADAM_EPS) + ADAM_WD * w)
    return delta, m, v


def _adam_shard(me, parts, mine, w, m, v, name):
    r, c = w.shape
    tr = min(r, 128)

    def body(me_ref, p_ref, own_ref, w_ref, m_ref, v_ref, g_ref, d_ref, nm_ref, nv_ref):
        own = own_ref[0].astype(F32)
        g = None
        for s in range(N_DEV):
            term = jnp.where(me_ref[0] == s, own, p_ref[s].astype(F32))
            g = term if g is None else g + term
        d, nm, nv = _adam_math(g, w_ref[...], m_ref[...], v_ref[...])
        g_ref[...] = g
        d_ref[...] = d
        nm_ref[...] = nm
        nv_ref[...] = nv

    row = pl.BlockSpec((tr, c), lambda i, me_ref: (i, 0))
    o = jax.ShapeDtypeStruct((r, c), F32)
    return pl.pallas_call(
        body, name=name,
        grid_spec=pltpu.PrefetchScalarGridSpec(
            num_scalar_prefetch=1, grid=(r // tr,),
            in_specs=[pl.BlockSpec((N_DEV, tr, c), lambda i, me_ref: (0, i, 0)),
                      pl.BlockSpec((1, tr, c), lambda i, me_ref: (me_ref[0], i, 0)), row, row, row],
            out_specs=[row] * 4),
        out_shape=[o] * 4,
        compiler_params=_cparams(("parallel",)),
    )(me, parts, mine, w, m, v)


def _adam_small(g, w, m, v):
    def body(g_ref, w_ref, m_ref, v_ref, d_ref, nm_ref, nv_ref):
        d, nm, nv = _adam_math(g_ref[...], w_ref[...], m_ref[...], v_ref[...])
        d_ref[...] = d
        nm_ref[...] = nm
        nv_ref[...] = nv

    o = jax.ShapeDtypeStruct(g.shape, F32)
    return pl.pallas_call(body, name="adam_small", out_shape=[o] * 3, compiler_params=_cparams())(g, w, m, v)


def _split_w_in(w_full):
    o = IN_OFF
    w_gdn = w_full[:, o["gq"]:o["ga"]]
    w_fox = w_full[:, o["fq"]:o["ff"]]
    w_gate = w_full[:, o["gate_a"]:o["end"]]
    w_small = jnp.concatenate([w_full[:, o["ga"]:o["fq"]], w_full[:, o["ff"]:o["gate_a"]],
                               jnp.zeros((w_full.shape[0], LANES - 24), w_full.dtype)], axis=1)
    return w_gdn, w_fox, w_gate, w_small


def _join_w_in(parts):
    small = parts[0]
    return jnp.concatenate(parts[1:5] + [small[:, 0:16]] + parts[5:8] + [small[:, 16:24]] + parts[8:10], axis=1)


def _rows128(a, rows):
    flat = a.reshape(-1)
    flat = jnp.concatenate([flat, jnp.zeros((rows * LANES - flat.shape[0],), flat.dtype)])
    return flat.reshape(rows, LANES)


def kernel(x, norm_mix_g, w_in, gdn_conv_w, gdn_a_log, gdn_dt_bias, gdn_norm_g, fox_q_norm_g, fox_k_norm_g, fox_f_bias, w_proj_gdn, w_proj_fox, w_out, norm_mlp_g, w_up, w_down, loss_target, m_norm_mix_g, m_w_in, m_gdn_conv_w, m_gdn_a_log, m_gdn_dt_bias, m_gdn_norm_g, m_fox_q_norm_g, m_fox_k_norm_g, m_fox_f_bias, m_w_proj_gdn, m_w_proj_fox, m_w_out, m_norm_mlp_g, m_w_up, m_w_down, v_norm_mix_g, v_w_in, v_gdn_conv_w, v_gdn_a_log, v_gdn_dt_bias, v_gdn_norm_g, v_fox_q_norm_g, v_fox_k_norm_g, v_fox_f_bias, v_w_proj_gdn, v_w_proj_fox, v_w_out, v_norm_mlp_g, v_w_up, v_w_down):
    b_loc, t, d = x.shape
    n = b_loc * t
    me = 4 * lax.axis_index("x") + 2 * lax.axis_index("y") + lax.axis_index("c")

    late_names = ["w_proj_gdn", "w_proj_fox", "w_out", "w_up", "w_down"]
    big16 = _to_bf16([w_in[0], w_proj_gdn[0], w_proj_fox[0], w_out[0], w_up[0], w_down[0]], "weights_to_bf16")
    g_in, g_conv = _all_gather([big16[0], gdn_conv_w[0]], "gather_w_in")
    late = _push_start(list(big16[1:]), False, "gather_late_start")
    w_full = g_in.transpose(1, 0, 2).reshape(d, N_DEV * w_in.shape[2])
    w_gdn, w_fox, w_gate, w_small = _split_w_in(w_full)
    weights = {
        "w_gdn": w_gdn, "w_fox": w_fox, "w_gate": w_gate, "w_small": w_small,
        "conv_w": g_conv.transpose(1, 0, 2).reshape(CONV_K, 3 * d),
        "norm_mix_g": norm_mix_g + late["token"][0:1, 0:1], "norm_mlp_g": norm_mlp_g, "a_log": gdn_a_log,
        "dt_bias": gdn_dt_bias, "gdn_norm_g": gdn_norm_g, "fox_q_norm_g": fox_q_norm_g, "fox_k_norm_g": fox_k_norm_g,
        "f_bias": fox_f_bias,
    }
    c_in, c_up = w_in.shape[2], w_up.shape[2]

    class _Exchange:
        def __init__(self):
            self.started = []

        def late_weights(self, after):
            shards, lands = _push_wait(late, after, "gather_late_wait")
            full = [lax.dynamic_update_index_in_dim(land, shard, me, 0) for land, shard in zip(lands, shards)]
            g_pa, g_pb, g_out, g_up, g_down = full
            return {"w_proj_gdn": g_pa.reshape(d, d), "w_proj_fox": g_pb.reshape(d, d), "w_out": g_out.reshape(d, d),
                    "w_up": g_up.transpose(1, 0, 2).reshape(d, D_FF), "w_down": g_down.reshape(D_FF, d)}

        def grads_ready(self, grads, tie):
            layout = {
                "w_in_parts": lambda p: _join_w_in(p).reshape(d, N_DEV, c_in).transpose(1, 0, 2),
                "w_up": lambda a: a.reshape(d, N_DEV, c_up).transpose(1, 0, 2),
                "w_down": lambda a: a.reshape(N_DEV, D_FF // N_DEV, d),
            }
            names = list(grads)
            arrs = [layout.get(k, lambda a: a.reshape(N_DEV, d // N_DEV, d))(grads[k]) for k in names]
            st = _push_start(arrs, True, "grads_start_" + names[0])
            self.started.append((names, st))
            return tie + st["token"][0:1, 0:1]

    comm = _Exchange()
    loss_blk, grad_x, g = _local_step(x.reshape(n, d), loss_target.reshape(n, d), weights, b_loc, t, comm)

    me1 = jnp.reshape(me, (1,)).astype(jnp.int32)
    shards = {"w_in_parts": (w_in, m_w_in, v_w_in), "w_proj_gdn": (w_proj_gdn, m_w_proj_gdn, v_w_proj_gdn),
              "w_proj_fox": (w_proj_fox, m_w_proj_fox, v_w_proj_fox), "w_out": (w_out, m_w_out, v_w_out),
              "w_up": (w_up, m_w_up, v_w_up), "w_down": (w_down, m_w_down, v_w_down)}
    adam = {}
    for names, st in comm.started:
        mine, parts = _push_wait(st, grad_x, "grads_wait_" + names[0])
        for k, own, part in zip(names, mine, parts):
            wi, mi, vi = shards[k]
            adam[k] = [r[None] for r in _adam_shard(me1, part, own, wi[0], mi[0], vi[0], "adam_" + k)]
    big_out = [adam[k] for k in ["w_in_parts"] + late_names]

    conv_rows = CONV_K * 3 * d // LANES
    conv_g = g["conv"].transpose(1, 0, 2).reshape(conv_rows, LANES)
    buf = jnp.concatenate([conv_g, g["norm_mix_g"].reshape(8, LANES), g["norm_mlp_g"].reshape(8, LANES),
                           g["gdn_small"], g["fox_small"], loss_blk], axis=0)
    tot = _all_reduce_small(buf, "all_reduce_small")
    o = conv_rows
    conv_full = tot[0:o].reshape(CONV_K, 3 * d)
    c_conv = gdn_conv_w.shape[2]
    g_conv_shard = lax.dynamic_slice(conv_full, (0, me * c_conv), (CONV_K, c_conv))
    g_mix = tot[o:o + 8].reshape(1, d)
    g_mlp = tot[o + 8:o + 16].reshape(1, d)
    gs, fs = tot[o + 16:o + 24], tot[o + 24:o + 32]
    loss = tot[o + 32, 0]
    small_g = [g_mix, g_conv_shard[None], gs[0:1, 0:HEADS], gs[1:2, 0:HEADS], gs[2:3], fs[0:1], fs[1:2], fs[2:3, 0:HEADS],
               g_mlp]
    small_w = [norm_mix_g, gdn_conv_w, gdn_a_log, gdn_dt_bias, gdn_norm_g, fox_q_norm_g, fox_k_norm_g, fox_f_bias,
               norm_mlp_g]
    small_m = [m_norm_mix_g, m_gdn_conv_w, m_gdn_a_log, m_gdn_dt_bias, m_gdn_norm_g, m_fox_q_norm_g, m_fox_k_norm_g,
               m_fox_f_bias, m_norm_mlp_g]
    small_v = [v_norm_mix_g, v_gdn_conv_w, v_gdn_a_log, v_gdn_dt_bias, v_gdn_norm_g, v_fox_q_norm_g, v_fox_k_norm_g,
               v_fox_f_bias, v_norm_mlp_g]
    row_counts = [-(-a.size // (8 * LANES)) * 8 for a in small_w]

    def pack(arrs):
        return jnp.concatenate([_rows128(a, rc) for a, rc in zip(arrs, row_counts)], axis=0)

    sd, sm, sv = _adam_small(pack(small_g), pack(small_w), pack(small_m), pack(small_v))

    def unpack(p):
        outs, r0 = [], 0
        for a, rc in zip(small_w, row_counts):
            outs.append(p[r0:r0 + rc].reshape(-1)[:a.size].reshape(a.shape))
            r0 += rc
        return outs

    small_out = [small_g_i.reshape(w_i.shape) for small_g_i, w_i in zip(small_g, small_w)], unpack(sd), unpack(sm), unpack(sv)

    def ordered(kind):
        s = small_out[kind]
        bo = [b[kind] for b in big_out]
        return [s[0], bo[0], s[1], s[2], s[3], s[4], s[5], s[6], s[7], bo[1], bo[2], bo[3], s[8], bo[4], bo[5]]

    return (loss, grad_x.reshape(b_loc, t, d), *ordered(0), *ordered(1), *ordered(2), *ordered(3))
```

```python
import functools

import jax
import jax.numpy as jnp
from jax import lax
from jax.experimental import pallas as pl
from jax.experimental.pallas import tpu as pltpu

F32 = jnp.float32
BF16 = jnp.bfloat16
HI = lax.Precision.HIGHEST
MESH = pl.DeviceIdType.MESH

N_DEV = 8
D_MODEL = 1024
HEADS = 8
DH = 128
CONV_K = 4
CHUNK = 64
GDN_GROUP = 8
FOX_BLOCK = 128
FOX_TILE = 512
D_FF = 4 * D_MODEL
EPS = 1e-6
LANES = 128
NEG = -1e30
IN_OFF = {"gq": 0, "gk": 1024, "gv": 2048, "gz": 3072, "ga": 4096, "gb": 4104, "fq": 4112, "fk": 5136,
          "fv": 6160, "ff": 7184, "gate_a": 7192, "gate_b": 8216, "end": 9240}
LANE_GA, LANE_GB, LANE_FF = 0, 8, 16

ADAM_LR = 0.001
ADAM_B1 = 0.9
ADAM_B2 = 0.999
ADAM_EPS = 1e-08
ADAM_WD = 0.01
ADAM_STEP = 10

VMEM_LIMIT = 56 * 1024 * 1024


def _cparams(sem=None):
    return pltpu.CompilerParams(dimension_semantics=sem, vmem_limit_bytes=VMEM_LIMIT)


def _sigmoid(x):
    return 1.0 / (1.0 + jnp.exp(-x))


def _softplus(x):
    return jnp.maximum(x, 0.0) + jnp.log(1.0 + jnp.exp(-jnp.abs(x)))


def _dot(a, b, prec=None):
    return lax.dot_general(a, b, (((1,), (0,)), ((), ())), precision=prec, preferred_element_type=F32)


def _dot_nt(a, b, prec=None):
    return lax.dot_general(a, b, (((1,), (1,)), ((), ())), precision=prec, preferred_element_type=F32)


def _dot_tn(a, b, prec=None):
    return lax.dot_general(a, b, (((0,), (0,)), ((), ())), precision=prec, preferred_element_type=F32)


def _bf(x):
    return x.astype(BF16)


MM_TILE = 1024


def _mm(a, b, *, name, ta=False, tb=False, out_dtype=F32, epi=None, extras=(), out2=None,
        b_koff=0, tm=MM_TILE, tn=MM_TILE, tk=MM_TILE):
    m = a.shape[1] if ta else a.shape[0]
    kdim = a.shape[0] if ta else a.shape[1]
    n = b.shape[0] if tb else b.shape[1]
    tm, tn, tk = min(tm, m), min(tn, n), min(tk, kdim)
    nk = kdim // tk
    grid = (m // tm, n // tn, nk)
    koff = b_koff // tk
    a_spec = pl.BlockSpec((tk, tm), lambda i, j, k: (k, i)) if ta else pl.BlockSpec((tm, tk), lambda i, j, k: (i, k))
    if tb:
        b_spec = pl.BlockSpec((tn, tk), lambda i, j, k: (j, k + koff))
    else:
        b_spec = pl.BlockSpec((tk, tn), lambda i, j, k: (k + koff, j))
    o_spec = pl.BlockSpec((tm, tn), lambda i, j, k: (i, j))
    n_e = len(extras)
    n_o = 1 if out2 is None else 2
    dims = (((0 if ta else 1,), (1 if tb else 0,)), ((), ()))

    def body(a_ref, b_ref, *rest):
        e_refs, o_refs = rest[:n_e], rest[n_e:n_e + n_o]
        prod = lax.dot_general(_bf(a_ref[...]), _bf(b_ref[...]), dims, preferred_element_type=F32)

        def finish(r):
            if out2 is not None:
                o_refs[1][...] = out2[0](r).astype(out2[1])
            if epi is not None:
                r = epi(r, *[e[...] for e in e_refs])
            o_refs[0][...] = r.astype(out_dtype)

        if nk == 1:
            finish(prod)
        else:
            acc = rest[n_e + n_o]
            k = pl.program_id(2)

            @pl.when(k == 0)
            def _():
                acc[...] = prod

            @pl.when(k > 0)
            def _():
                acc[...] += prod

            @pl.when(k == nk - 1)
            def _():
                finish(acc[...])

    shapes = [jax.ShapeDtypeStruct((m, n), out_dtype)]
    if out2 is not None:
        shapes.append(jax.ShapeDtypeStruct((m, n), out2[1]))
    res = pl.pallas_call(
        body, name=name, grid=grid,
        in_specs=[a_spec, b_spec] + [o_spec] * n_e,
        out_specs=[o_spec] * n_o, out_shape=shapes,
        scratch_shapes=[] if nk == 1 else [pltpu.VMEM((tm, tn), F32)],
        compiler_params=_cparams(("parallel", "parallel", "arbitrary")),
    )(a, b, *extras)
    return res[0] if out2 is None else res


def _relu2(x):
    r = jnp.maximum(x, 0.0)
    return r * r


ROWS = 512


def _rms_fwd(x, g, name):
    n, d = x.shape

    def body(x_ref, g_ref, u_ref):
        xv = x_ref[...]
        r = lax.rsqrt(jnp.mean(xv * xv, axis=1, keepdims=True) + EPS)
        u_ref[...] = _bf(xv * r * g_ref[...])

    return pl.pallas_call(
        body, name=name, grid=(n // ROWS,),
        in_specs=[pl.BlockSpec((ROWS, d), lambda i: (i, 0)), pl.BlockSpec((1, d), lambda i: (0, 0))],
        out_specs=pl.BlockSpec((ROWS, d), lambda i: (i, 0)),
        out_shape=jax.ShapeDtypeStruct((n, d), BF16),
        compiler_params=_cparams(("parallel",)),
    )(x, g)


def _rms_bwd(dy, x, g, dres, name):
    n, d = x.shape

    def body(dy_ref, x_ref, g_ref, dres_ref, dx_ref, dx16_ref, dg_ref):
        i = pl.program_id(0)
        xv, dyv = x_ref[...], dy_ref[...]
        r = lax.rsqrt(jnp.mean(xv * xv, axis=1, keepdims=True) + EPS)
        gy = dyv * g_ref[...]
        s = jnp.sum(gy * xv, axis=1, keepdims=True)
        dx = dres_ref[...] + r * gy - xv * (r * r * r * (1.0 / d)) * s
        dx_ref[...] = dx
        dx16_ref[...] = _bf(dx)

        @pl.when(i == 0)
        def _():
            dg_ref[...] = jnp.zeros_like(dg_ref)

        dg_ref[...] += jnp.sum(dyv * xv * r, axis=0, keepdims=True)

    row = pl.BlockSpec((ROWS, d), lambda i: (i, 0))
    vec = pl.BlockSpec((1, d), lambda i: (0, 0))
    return pl.pallas_call(
        body, name=name, grid=(n // ROWS,),
        in_specs=[row, row, vec, row], out_specs=[row, row, vec],
        out_shape=[jax.ShapeDtypeStruct((n, d), F32), jax.ShapeDtypeStruct((n, d), BF16),
                   jax.ShapeDtypeStruct((1, d), F32)],
        compiler_params=_cparams(("arbitrary",)),
    )(dy, x, g, dres)


def _merge_fwd(ya, yb, gate):
    n, d = ya.shape

    def body(ya_ref, yb_ref, ga_ref, gb_ref, o_ref):
        o_ref[...] = _bf(_sigmoid(ga_ref[...]) * ya_ref[...] + _sigmoid(gb_ref[...]) * yb_ref[...])

    row = pl.BlockSpec((ROWS, d), lambda i: (i, 0))
    return pl.pallas_call(
        body, name="merge_fwd", grid=(n // ROWS,),
        in_specs=[row, row, row, pl.BlockSpec((ROWS, d), lambda i: (i, 1))], out_specs=row,
        out_shape=jax.ShapeDtypeStruct((n, d), BF16),
        compiler_params=_cparams(("parallel",)),
    )(ya, yb, gate, gate)


def _merge_bwd(dm, ya, yb, gate):
    n, d = ya.shape

    def body(dm_ref, ya_ref, yb_ref, ga_ref, gb_ref, dya_ref, dyb_ref, dga_ref, dgb_ref):
        dmv = dm_ref[...]
        sa, sb = _sigmoid(ga_ref[...]), _sigmoid(gb_ref[...])
        dya_ref[...] = _bf(dmv * sa)
        dyb_ref[...] = _bf(dmv * sb)
        dga_ref[...] = _bf(dmv * ya_ref[...] * sa * (1.0 - sa))
        dgb_ref[...] = _bf(dmv * yb_ref[...] * sb * (1.0 - sb))

    row = pl.BlockSpec((ROWS, d), lambda i: (i, 0))
    o = jax.ShapeDtypeStruct((n, d), BF16)
    return pl.pallas_call(
        body, name="merge_bwd", grid=(n // ROWS,),
        in_specs=[row, row, row, row, pl.BlockSpec((ROWS, d), lambda i: (i, 1))], out_specs=[row] * 4,
        out_shape=[o] * 4,
        compiler_params=_cparams(("parallel",)),
    )(dm, ya, yb, gate, gate)


def _loss_bwd(out, target):
    n, d = out.shape

    def body(o_ref, t_ref, d_ref, d16_ref, l_ref):
        i = pl.program_id(0)
        err = o_ref[...] - t_ref[...]
        d_ref[...] = err * (1.0 / d)
        d16_ref[...] = _bf(err * (1.0 / d))

        @pl.when(i == 0)
        def _():
            l_ref[...] = jnp.zeros_like(l_ref)

        l_ref[...] += 0.5 * jnp.sum(jnp.mean(err * err, axis=1, keepdims=True), axis=0, keepdims=True)

    row = pl.BlockSpec((ROWS, d), lambda i: (i, 0))
    return pl.pallas_call(
        body, name="loss_bwd", grid=(n // ROWS,),
        in_specs=[row, row], out_specs=[row, row, pl.BlockSpec((8, LANES), lambda i: (0, 0))],
        out_shape=[jax.ShapeDtypeStruct((n, d), F32), jax.ShapeDtypeStruct((n, d), BF16),
                   jax.ShapeDtypeStruct((8, LANES), F32)],
        compiler_params=_cparams(("arbitrary",)),
    )(out, target)


PAD = 8


def _pad_zero(pad_ref):
    t = pad_ref.shape[0] - 2 * PAD
    pad_ref[0:PAD, :] = jnp.zeros((PAD, LANES), F32)
    pad_ref[PAD + t:2 * PAD + t, :] = jnp.zeros((PAD, LANES), F32)


def _shifted(pad_ref, s):
    t = pad_ref.shape[0] - 2 * PAD
    return pad_ref[PAD - s:PAD - s + t, :]


def _conv(x, w_ref, pad_ref):
    t = x.shape[0]
    pad_ref[PAD:PAD + t, :] = x
    y = _shifted(pad_ref, 3) * w_ref[0:1, :]
    y = y + _shifted(pad_ref, 2) * w_ref[1:2, :]
    y = y + _shifted(pad_ref, 1) * w_ref[2:3, :]
    return y + x * w_ref[3:4, :]


def _chunk_consts():
    r = lax.broadcasted_iota(jnp.int32, (CHUNK, CHUNK), 0)
    c = lax.broadcasted_iota(jnp.int32, (CHUNK, CHUNK), 1)
    incl, strict = r >= c, r > c
    return dict(incl=incl, strict=strict, trilf=incl.astype(F32), triuf=(r <= c).astype(F32),
                eye=(r == c).astype(F32))


class _V:
    def __init__(self, xs):
        self.xs = list(xs)

    def __add__(self, o):
        return _ap(lambda x, y: x + y, self, o)

    def __radd__(self, o):
        return _ap(lambda x, y: y + x, self, o)

    def __sub__(self, o):
        return _ap(lambda x, y: x - y, self, o)

    def __rsub__(self, o):
        return _ap(lambda x, y: y - x, self, o)

    def __mul__(self, o):
        return _ap(lambda x, y: x * y, self, o)

    def __rmul__(self, o):
        return _ap(lambda x, y: y * x, self, o)

    def __neg__(self):
        return _ap(lambda x: -x, self)

    def __getitem__(self, idx):
        return _ap(lambda x: x[idx], self)


def _ap(fn, *args):
    n = [len(a.xs) for a in args if isinstance(a, _V)]
    if not n:
        return fn(*args)
    return _V([fn(*[a.xs[i] if isinstance(a, _V) else a for a in args]) for i in range(n[0])])


def _vbf(x):
    return _ap(_bf, x)


def _vdot(a, b):
    return _ap(_dot, a, b)


def _vdot_nt(a, b):
    return _ap(_dot_nt, a, b)


def _vdot_tn(a, b):
    return _ap(_dot_tn, a, b)


def _vexp(x):
    return _ap(jnp.exp, x)


def _vsum(x, axis):
    return _ap(lambda v: jnp.sum(v, axis=axis, keepdims=True), x)


def _vcat(a, b, axis):
    return _ap(lambda x, y: jnp.concatenate([x, y], axis=axis), a, b)


def _vmask(mask, x):
    return _ap(lambda v: jnp.where(mask, v, 0.0), x)


def _split2(x):
    h = _vbf(x)
    return h, _vbf(x - _ap(lambda v: v.astype(F32), h))


def _dot3(a, b, kind=_vdot):
    ah, al = _split2(a)
    bh, bl = _split2(b)
    return kind(ah, bh) + (kind(ah, bl) + kind(al, bh))


def _split(x, terms):
    out = []
    for _ in range(terms):
        h = _vbf(x)
        out.append(h)
        x = x - _ap(lambda v: v.astype(F32), h)
    return out


def _dot_exact_l(m01, x, kind=_vdot, terms=2):
    mb = _bf(m01)
    parts = [kind(mb, xp) for xp in _split(x, terms)]
    return functools.reduce(lambda a, b: a + b, reversed(parts))


def _dot_exact_r(x, m01, kind=_vdot, terms=2):
    mb = _bf(m01)
    parts = [kind(xp, mb) for xp in _split(x, terms)]
    return functools.reduce(lambda a, b: a + b, reversed(parts))


def _inv_unit_lower(a, eye):
    p = -a
    r = p + eye
    p = _dot3(p, p)
    for j in range(1, 6):
        if j < 5:
            y = _dot3(p, _vcat(p, r, 1))
            p, r = y[:, 0:CHUNK], r + y[:, CHUNK:2 * CHUNK]
        else:
            r = r + _dot3(p, r)
    return r


def _gdn_chunk_pre(q, k, v, g128, g64, b128, b64, cs):
    incl = cs["incl"]
    big_g = _dot_exact_l(cs["trilf"], g128)
    gc = big_g[:, 0:CHUNK]
    gr = _dot_exact_r(g64, cs["triuf"], _vdot_tn)
    decay = _ap(lambda d: jnp.where(incl, jnp.exp(jnp.where(incl, d, 0.0)), 0.0), gc - gr)
    kb, qb = _vbf(k), _vbf(q)
    qkk = _vdot_nt(_vcat(qb, kb, 0), kb)
    qk, kk = qkk[0:CHUNK], qkk[CHUNK:2 * CHUNK]
    tm = _inv_unit_lower(_vmask(cs["strict"], b64 * kk * decay), cs["eye"])
    e_g = _vexp(big_g)
    wu = _dot3(tm, _vcat(v * b128, k * (b128 * e_g), 1))
    w, u = wu[:, 0:DH], wu[:, DH:2 * DH]
    g_last = _vsum(g128, 0)
    return dict(big_g=big_g, decay=decay, kk=kk, qk=qk, tm=tm, w=w, u=u, p=qk * decay, q_dec=q * e_g,
                k_dec=k * _vexp(g_last - big_g), dec=_vexp(g_last))


def _gdn_chunk_post(q, k, v, g128, b128, b64, s, ds_next, do, dv_new, big_g, decay, kk, qk, tm, u, v_new, cs):
    e_g = _vexp(big_g)
    vb = v * b128
    kbeta = k * (b128 * e_g)
    q_dec = q * e_g
    g_last = _vsum(g128, 0)
    ekg = _vexp(g_last - big_g)
    k_dec = k * ekg
    dec = _vexp(g_last)
    kb, qb, sb = _vbf(k), _vbf(q), _vbf(s)
    dob, dsb, vnb, dvnb = _vbf(do), _vbf(ds_next), _vbf(v_new), _vbf(dv_new)
    dp = _vmask(cs["incl"], _vdot_nt(dob, vnb))
    dq_dec = _vdot_nt(dob, sb)
    du = -_vdot_nt(dvnb, sb)
    ddec = _vsum(_vsum(s * ds_next, 1), 0)
    dk_dec = _vdot_nt(vnb, dsb)
    dwu = _vcat(dv_new, du, 1)
    dt = _dot3(dwu, _vcat(vb, kbeta, 1), _vdot_nt)
    dvk = _dot3(tm, dwu, _vdot_tn)
    dvb, dkbeta = dvk[:, 0:DH], dvk[:, DH:2 * DH]
    da = _vmask(cs["strict"], -_dot3(tm, _dot3(dt, tm, _vdot_nt), _vdot_tn))
    dkk = _vbf(da * b64 * decay)
    dqk = _vbf(dp * decay)
    ddd = (da * b64 * kk + dp * qk) * decay
    dq = _vdot(dqk, kb) + dq_dec * e_g
    dk = _vdot_tn(dqk, qb) + _vdot(dkk, kb) + _vdot_tn(dkk, kb) + dk_dec * ekg + dkbeta * (b128 * e_g)
    dv = dvb * b128
    dbeta = _vsum(da * kk * decay, 1) + _vsum(dvb * v, 1) + _vsum(dkbeta * k * e_g, 1)
    s_k = _vsum(dk_dec * k_dec, 1)
    dg_col = _vsum(ddd, 1) + _vsum(dq_dec * q_dec, 1) - s_k + _vsum(dkbeta * kbeta, 1)
    colsum = _dot_exact_r(ddd, jnp.ones((CHUNK, LANES), F32), _vdot_tn)
    dg_last = _vsum(s_k, 0) + ddec * dec
    dg = _dot_exact_l(cs["triuf"], dg_col - colsum) + dg_last
    return dq, dk, dv, dg, dbeta


def _stack_rows(vecs, nrows):
    row = lax.broadcasted_iota(jnp.int32, (nrows, LANES), 0)
    out = jnp.zeros((nrows, LANES), F32)
    for i, v in enumerate(vecs):
        out = out + jnp.where(row == i, jnp.broadcast_to(v, (nrows, LANES)), 0.0)
    return out


def _head_lane(x, lane_idx):
    lane = lax.broadcasted_iota(jnp.int32, x.shape, 1)
    return jnp.sum(jnp.where(lane == lane_idx, x, 0.0), axis=1, keepdims=True)


def _gdn_gates(ps, h, alog_ref, dtb_ref):
    ga = _head_lane(ps, LANE_GA + h)
    gb = _head_lane(ps, LANE_GB + h)
    a = jnp.exp(jnp.full((1, 1), alog_ref[0, h], F32))
    sp_in = ga + dtb_ref[0, h]
    g = -a * _softplus(sp_in)
    return g, _sigmoid(gb), a, sp_in


def _gdn_specs(b_loc, t):
    def col(off):
        return pl.BlockSpec((t, DH), lambda b, h: (b, off + h))

    ps_spec = pl.BlockSpec((t, LANES), lambda b, h: (b, 0))

    def wcol(off):
        return pl.BlockSpec((CONV_K, DH), lambda b, h: (0, off + h))

    smem = pl.BlockSpec(memory_space=pltpu.SMEM)
    vec = pl.BlockSpec((1, DH), lambda b, h: (0, 0))
    return col, ps_spec, wcol, smem, vec


def _gdn_fwd(pg, ps, convw, a_log, dt_bias, gnorm, b_loc, t):
    n = b_loc * t
    nc = t // CHUNK
    col, ps_spec, wcol, smem, vec = _gdn_specs(b_loc, t)

    def body(q_ref, k_ref, v_ref, z_ref, ps_ref, wq_ref, wk_ref, wv_ref, alog_ref, dtb_ref, gn_ref,
             oa_ref, oraw_ref, s_ref, qn, kn, vv, g128, g64, b128, b64, uq_s, p_s, kd_s, dec_s, pad_s):
        h = pl.program_id(1)
        g, beta, _, _ = _gdn_gates(ps_ref[...], h, alog_ref, dtb_ref)
        g128[...] = jnp.broadcast_to(g, (t, LANES))
        g64[...] = jnp.broadcast_to(g, (t, CHUNK))
        b128[...] = jnp.broadcast_to(beta, (t, LANES))
        b64[...] = jnp.broadcast_to(beta, (t, CHUNK))
        _pad_zero(pad_s)
        pq = _conv(q_ref[...], wq_ref, pad_s)
        yq = pq * _sigmoid(pq)
        qn[...] = yq * (lax.rsqrt(jnp.sum(yq * yq, axis=1, keepdims=True) + EPS) * (DH ** -0.5))
        pk = _conv(k_ref[...], wk_ref, pad_s)
        yk = pk * _sigmoid(pk)
        kn[...] = yk * lax.rsqrt(jnp.sum(yk * yk, axis=1, keepdims=True) + EPS)
        pv = _conv(v_ref[...], wv_ref, pad_s)
        vv[...] = pv * _sigmoid(pv)
        cs = _chunk_consts()

        def pre_group(gi, _):
            idx = [gi * GDN_GROUP + c for c in range(GDN_GROUP)]
            rows = [pl.ds(pl.multiple_of(i * CHUNK, CHUNK), CHUNK) for i in idx]
            ins = [_V([ref[r, :] for r in rows]) for ref in (qn, kn, vv, g128, g64, b128, b64)]
            f = _gdn_chunk_pre(*ins, cs)
            for c, (i, r) in enumerate(zip(idx, rows)):
                vv[r, :] = f["w"].xs[c]
                uq_s[i, 0:CHUNK, :] = _bf(f["u"].xs[c])
                uq_s[i, CHUNK:2 * CHUNK, :] = _bf(f["q_dec"].xs[c])
                p_s[r, :] = _bf(f["p"].xs[c])
                kd_s[r, :] = _bf(f["k_dec"].xs[c])
                dec_s[pl.ds(pl.multiple_of(i * 8, 8), 8), :] = jnp.broadcast_to(f["dec"].xs[c], (8, LANES))
            return 0

        lax.fori_loop(0, nc // GDN_GROUP, pre_group, 0)

        def chunk(i, s):
            r = pl.ds(pl.multiple_of(i * CHUNK, CHUNK), CHUNK)
            us = _dot(uq_s[i], _bf(s))
            vnb = _bf(vv[r, :] - us[0:CHUNK])
            oraw_ref[r, :] = us[CHUNK:2 * CHUNK] + _dot(p_s[r, :], vnb)
            s_ref[0, 0, i] = s
            return s * dec_s[pl.ds(pl.multiple_of(i * 8, 8), 1), :] + _dot_tn(kd_s[r, :], vnb)

        lax.fori_loop(0, nc, chunk, jnp.zeros((DH, DH), F32))
        o = oraw_ref[...]
        rr = lax.rsqrt(jnp.mean(o * o, axis=1, keepdims=True) + EPS)
        z = z_ref[...]
        oa_ref[...] = _bf((o * rr * gn_ref[...]) * (z * _sigmoid(z)))

    return pl.pallas_call(
        body, name="gdn_fwd", grid=(b_loc, HEADS),
        in_specs=[col(0), col(HEADS), col(2 * HEADS), col(3 * HEADS), ps_spec, wcol(0), wcol(HEADS), wcol(2 * HEADS),
                  smem, smem, vec],
        out_specs=[pl.BlockSpec((t, DH), lambda b, h: (b, h)), pl.BlockSpec((t, DH), lambda b, h: (b, h)),
                   pl.BlockSpec((1, 1, nc, DH, DH), lambda b, h: (b, h, 0, 0, 0))],
        out_shape=[jax.ShapeDtypeStruct((n, HEADS * DH), BF16), jax.ShapeDtypeStruct((n, HEADS * DH), F32),
                   jax.ShapeDtypeStruct((b_loc, HEADS, nc, DH, DH), F32)],
        scratch_shapes=([pltpu.VMEM((t, DH), F32)] * 3 + [pltpu.VMEM((t, LANES), F32), pltpu.VMEM((t, CHUNK), F32)] * 2
                        + [pltpu.VMEM((nc, 2 * CHUNK, DH), BF16), pltpu.VMEM((t, CHUNK), BF16), pltpu.VMEM((t, DH), BF16),
                           pltpu.VMEM((8 * nc, LANES), F32), pltpu.VMEM((t + 2 * PAD, LANES), F32)]),
        compiler_params=_cparams(("arbitrary", "arbitrary")),
    )(pg, pg, pg, pg, ps, convw, convw, convw, a_log, dt_bias, gnorm)


def _gdn_bwd(pg, ps, convw, a_log, dt_bias, gnorm, d_oa, o_raw, s_all, b_loc, t):
    n = b_loc * t
    nc = t // CHUNK
    col, ps_spec, wcol, smem, vec = _gdn_specs(b_loc, t)

    def body(q_ref, k_ref, v_ref, z_ref, ps_ref, wq_ref, wk_ref, wv_ref, alog_ref, dtb_ref, gn_ref,
             doa_ref, oraw_ref, s_ref,
             dq_ref, dk_ref, dv_ref, dz_ref, dps_ref, dcw_ref, dsm_ref,
             qn, kn, vv, g128, g64, b128, b64, do_s, bg_s, u_s, vn_s, dvn_s, dcy_s, kk_s, qk_s, tm_s, dsn_s, pad_s):
        b, h = pl.program_id(0), pl.program_id(1)
        g, beta, _, _ = _gdn_gates(ps_ref[...], h, alog_ref, dtb_ref)
        g128[...] = jnp.broadcast_to(g, (t, LANES))
        g64[...] = jnp.broadcast_to(g, (t, CHUNK))
        b128[...] = jnp.broadcast_to(beta, (t, LANES))
        b64[...] = jnp.broadcast_to(beta, (t, CHUNK))
        _pad_zero(pad_s)

        def prep(x_ref, w_ref):
            p = _conv(x_ref[...], w_ref, pad_s)
            sg = _sigmoid(p)
            return p, sg, p * sg

        _, _, yq = prep(q_ref, wq_ref)
        qn[...] = yq * (lax.rsqrt(jnp.sum(yq * yq, axis=1, keepdims=True) + EPS) * (DH ** -0.5))
        _, _, yk = prep(k_ref, wk_ref)
        kn[...] = yk * lax.rsqrt(jnp.sum(yk * yk, axis=1, keepdims=True) + EPS)
        _, _, yv = prep(v_ref, wv_ref)
        vv[...] = yv

        o = oraw_ref[...]
        z = z_ref[...]
        doa = doa_ref[...]
        gn = gn_ref[...]
        ro = lax.rsqrt(jnp.mean(o * o, axis=1, keepdims=True) + EPS)
        sz = _sigmoid(z)
        dz_ref[...] = _bf(doa * (o * ro * gn) * (sz * (1.0 + z * (1.0 - sz))))
        dn = doa * (z * sz)
        dgn = jnp.sum(dn * o * ro, axis=0, keepdims=True)
        gy = dn * gn
        do_s[...] = ro * gy - o * (ro * ro * ro * (1.0 / DH)) * jnp.sum(gy * o, axis=1, keepdims=True)

        cs = _chunk_consts()

        def pre_group(gi, _):
            idx = [gi * GDN_GROUP + c for c in range(GDN_GROUP)]
            rows = [pl.ds(pl.multiple_of(i * CHUNK, CHUNK), CHUNK) for i in idx]
            ins = [_V([ref[r, :] for r in rows]) for ref in (qn, kn, vv, g128, g64, b128, b64)]
            states = _V([_bf(s_ref[0, 0, i]) for i in idx])
            f = _gdn_chunk_pre(*ins, cs)
            v_new = f["w"] - _vdot(_vbf(f["u"]), states)
            for c, r in enumerate(rows):
                bg_s[r, :] = f["big_g"].xs[c]
                u_s[r, :] = f["u"].xs[c]
                vn_s[r, :] = v_new.xs[c]
                dcy_s[r, :] = f["decay"].xs[c]
                kk_s[r, :] = f["kk"].xs[c]
                qk_s[r, :] = f["qk"].xs[c]
                tm_s[r, :] = f["tm"].xs[c]
            return 0

        lax.fori_loop(0, nc // GDN_GROUP, pre_group, 0)

        def chunk(j, ds):
            i = nc - 1 - j
            r = pl.ds(pl.multiple_of(i * CHUNK, CHUNK), CHUNK)
            big_g = bg_s[r, :]
            g_last = jnp.sum(g128[r, :], axis=0, keepdims=True)
            dob = _bf(do_s[r, :])
            dv_new = (_dot_tn(_bf(qk_s[r, :] * dcy_s[r, :]), dob)
                      + _dot(_bf(kn[r, :] * jnp.exp(g_last - big_g)), _bf(ds)))
            dvn_s[r, :] = dv_new
            dsn_s[i] = ds
            return (_dot_tn(_bf(qn[r, :] * jnp.exp(big_g)), dob) + jnp.exp(g_last) * ds
                    - _dot_tn(_bf(u_s[r, :]), _bf(dv_new)))

        lax.fori_loop(0, nc, chunk, jnp.zeros((DH, DH), F32))

        def post_group(gi, _):
            idx = [gi * GDN_GROUP + c for c in range(GDN_GROUP)]
            rows = [pl.ds(pl.multiple_of(i * CHUNK, CHUNK), CHUNK) for i in idx]
            def rows_of(ref):
                return _V([ref[r, :] for r in rows])

            dq, dk, dv, dg, dbeta = _gdn_chunk_post(
                rows_of(qn), rows_of(kn), rows_of(vv), rows_of(g128), rows_of(b128), rows_of(b64),
                _V([s_ref[0, 0, i] for i in idx]), _V([dsn_s[i] for i in idx]), rows_of(do_s), rows_of(dvn_s),
                rows_of(bg_s), rows_of(dcy_s), rows_of(kk_s), rows_of(qk_s), rows_of(tm_s), rows_of(u_s), rows_of(vn_s),
                cs)
            for c, r in enumerate(rows):
                qn[r, :] = dq.xs[c]
                kn[r, :] = dk.xs[c]
                vv[r, :] = dv.xs[c]
                g128[r, :] = dg.xs[c]
                b128[r, :] = jnp.broadcast_to(dbeta.xs[c], (CHUNK, LANES))
            return 0

        lax.fori_loop(0, nc // GDN_GROUP, post_group, 0)
        dqh, dkh, dvh = qn, kn, vv

        g, beta, a, sp_in = _gdn_gates(ps_ref[...], h, alog_ref, dtb_ref)
        dg = g128[...]
        d_ga = dg * (-a) * _sigmoid(sp_in)
        d_alog = jnp.sum(dg * g, axis=0, keepdims=True)
        d_dtb = jnp.sum(d_ga, axis=0, keepdims=True)
        d_gb = b128[...] * (beta * (1.0 - beta))
        lane = lax.broadcasted_iota(jnp.int32, (t, LANES), 1)
        contrib = jnp.where(lane == LANE_GA + h, d_ga, 0.0) + jnp.where(lane == LANE_GB + h, d_gb, 0.0)

        @pl.when(h == 0)
        def _():
            dps_ref[...] = jnp.zeros_like(dps_ref)

        dps_ref[...] += contrib

        lane1 = lax.broadcasted_iota(jnp.int32, (1, LANES), 1)
        small = _stack_rows([jnp.where(lane1 == h, d_alog, 0.0), jnp.where(lane1 == h, d_dtb, 0.0), dgn], 8)

        @pl.when((b == 0) & (h == 0))
        def _():
            dsm_ref[...] = jnp.zeros_like(dsm_ref)
            dcw_ref[...] = jnp.zeros_like(dcw_ref)

        dsm_ref[...] += small

        def conv_bwd(dp, x, w_ref, slot):
            dw = _stack_rows([jnp.sum(dp * _shifted(pad_s, 3), axis=0, keepdims=True),
                              jnp.sum(dp * _shifted(pad_s, 2), axis=0, keepdims=True),
                              jnp.sum(dp * _shifted(pad_s, 1), axis=0, keepdims=True),
                              jnp.sum(dp * x, axis=0, keepdims=True)], CONV_K)
            dcw_ref[slot] += dw
            pad_s[PAD:PAD + t, :] = dp
            dx = _shifted(pad_s, -3) * w_ref[0:1, :]
            dx = dx + _shifted(pad_s, -2) * w_ref[1:2, :]
            dx = dx + _shifted(pad_s, -1) * w_ref[2:3, :]
            return dx + dp * w_ref[3:4, :]

        def l2_bwd(dqn, y, c):
            r = lax.rsqrt(jnp.sum(y * y, axis=1, keepdims=True) + EPS)
            s1 = jnp.sum(dqn * y, axis=1, keepdims=True)
            return c * r * dqn - (c * r * r * r) * s1 * y

        def silu_bwd(p, sg):
            return sg * (1.0 + p * (1.0 - sg))

        pq, sq, yq = prep(q_ref, wq_ref)
        dq_ref[...] = _bf(conv_bwd(l2_bwd(dqh[...], yq, DH ** -0.5) * silu_bwd(pq, sq), q_ref[...], wq_ref, h))
        pk, sk, yk = prep(k_ref, wk_ref)
        dk_ref[...] = _bf(conv_bwd(l2_bwd(dkh[...], yk, 1.0) * silu_bwd(pk, sk), k_ref[...], wk_ref, HEADS + h))
        pv, sv, _ = prep(v_ref, wv_ref)
        dv_ref[...] = _bf(conv_bwd(dvh[...] * silu_bwd(pv, sv), v_ref[...], wv_ref, 2 * HEADS + h))

    blk = pl.BlockSpec((t, DH), lambda b, h: (b, h))
    ob = jax.ShapeDtypeStruct((n, HEADS * DH), BF16)
    return pl.pallas_call(
        body, name="gdn_bwd", grid=(b_loc, HEADS),
        in_specs=[col(0), col(HEADS), col(2 * HEADS), col(3 * HEADS), ps_spec, wcol(0), wcol(HEADS), wcol(2 * HEADS),
                  smem, smem, vec, blk, blk, pl.BlockSpec((1, 1, nc, DH, DH), lambda b, h: (b, h, 0, 0, 0))],
        out_specs=[blk, blk, blk, blk, ps_spec,
                   pl.BlockSpec((3 * HEADS, CONV_K, DH), lambda b, h: (0, 0, 0)),
                   pl.BlockSpec((8, LANES), lambda b, h: (0, 0))],
        out_shape=[ob, ob, ob, ob, jax.ShapeDtypeStruct((n, LANES), F32),
                   jax.ShapeDtypeStruct((3 * HEADS, CONV_K, DH), F32), jax.ShapeDtypeStruct((8, LANES), F32)],
        scratch_shapes=([pltpu.VMEM((t, DH), F32)] * 3 + [pltpu.VMEM((t, LANES), F32), pltpu.VMEM((t, CHUNK), F32)] * 2
                        + [pltpu.VMEM((t, DH), F32)] * 5 + [pltpu.VMEM((t, CHUNK), F32)] * 4
                        + [pltpu.VMEM((nc, DH, DH), F32), pltpu.VMEM((t + 2 * PAD, LANES), F32)]),
        compiler_params=_cparams(("arbitrary", "arbitrary")),
    )(pg, pg, pg, pg, ps, convw, convw, convw, a_log, dt_bias, gnorm, d_oa, o_raw, s_all)


def _fox_prologue(q_ref, k_ref, v_ref, ps_ref, fb_ref, gq_ref, gk_ref, h, t, qs, ks, vs, ccol, crow):
    nb = t // FOX_BLOCK
    q, k = q_ref[...], k_ref[...]
    rq = lax.rsqrt(jnp.mean(q * q, axis=1, keepdims=True) + EPS)
    rk = lax.rsqrt(jnp.mean(k * k, axis=1, keepdims=True) + EPS)
    qs[...] = _bf(q * rq * gq_ref[...])
    ks[...] = _bf(k * rk * gk_ref[...])
    vs[...] = _bf(v_ref[...])
    f_in = _head_lane(ps_ref[...], LANE_FF + h) + fb_ref[0, h]
    ccol[...] = jnp.broadcast_to(-_softplus(-f_in), (t, LANES))
    r = lax.broadcasted_iota(jnp.int32, (FOX_BLOCK, FOX_BLOCK), 0)
    c = lax.broadcasted_iota(jnp.int32, (FOX_BLOCK, FOX_BLOCK), 1)
    trilf, triuf = (r >= c).astype(F32), (r <= c).astype(F32)
    blocks = [pl.ds(j * FOX_BLOCK, FOX_BLOCK) for j in range(nb)]
    lfs = _V([ccol[rb, :] for rb in blocks])
    cc = _dot_exact_l(trilf, lfs, terms=3)
    cr = _dot_exact_r(lfs, triuf, _vdot_tn, terms=3)
    sums = _vsum(lfs, 0)
    carry = jnp.zeros((1, LANES), F32)
    for j, rb in enumerate(blocks):
        ccol[rb, :] = cc.xs[j] + carry
        crow[j] = (cr.xs[j] + carry)[0:8]
        carry = carry + sums.xs[j]
    return rq, rk, f_in


def _fox_scores(q_rows, k_rows, cc, cr, row0, col0):
    s = _dot_nt(q_rows, k_rows) * (DH ** -0.5) + cc - cr
    r = lax.broadcasted_iota(jnp.int32, s.shape, 0)
    c = lax.broadcasted_iota(jnp.int32, s.shape, 1)
    return jnp.where(row0 + r >= col0 + c, s, NEG)


def _fox_specs(t):
    def col(off):
        return pl.BlockSpec((t, DH), lambda b, h: (b, off + h))

    ps_spec = pl.BlockSpec((t, LANES), lambda b, h: (b, 0))
    smem = pl.BlockSpec(memory_space=pltpu.SMEM)
    vec = pl.BlockSpec((1, DH), lambda b, h: (0, 0))
    blk = pl.BlockSpec((t, DH), lambda b, h: (b, h))
    return col, ps_spec, smem, vec, blk


def _fox_fwd(pf, ps, f_bias, gq, gk, b_loc, t):
    n = b_loc * t
    nb = t // FOX_BLOCK
    kt = min(FOX_TILE, t)
    nsub = kt // FOX_BLOCK
    col, ps_spec, smem, vec, blk = _fox_specs(t)

    def body(q_ref, k_ref, v_ref, ps_ref, fb_ref, gq_ref, gk_ref, o_ref, lse_ref, qs, ks, vs, ccol, crow):
        h = pl.program_id(1)
        _fox_prologue(q_ref, k_ref, v_ref, ps_ref, fb_ref, gq_ref, gk_ref, h, t, qs, ks, vs, ccol, crow)

        def qblock(i, _):
            ri = pl.ds(pl.multiple_of(i * FOX_BLOCK, FOX_BLOCK), FOX_BLOCK)
            qi = qs[ri, :]
            cc = jnp.concatenate([ccol[ri, :]] * nsub, axis=1)

            def ktile(j, carry):
                m, l, acc = carry
                rj = pl.ds(pl.multiple_of(j * kt, kt), kt)
                cr = jnp.concatenate([crow[j * nsub + u, 0:1, :] for u in range(nsub)], axis=1)
                s = _fox_scores(qi, ks[rj, :], cc, cr, i * FOX_BLOCK, j * kt)
                m_new = jnp.maximum(m, jnp.max(s, axis=1, keepdims=True))
                p = jnp.exp(s - m_new)
                alpha = jnp.exp(m - m_new)
                l = alpha * l + jnp.sum(p, axis=1, keepdims=True)
                acc = alpha * acc + _dot(_bf(p), vs[rj, :])
                return m_new, l, acc

            m, l, acc = lax.fori_loop(0, (i * FOX_BLOCK) // kt + 1, ktile, (jnp.full((FOX_BLOCK, 1), NEG, F32),
                                                                            jnp.zeros((FOX_BLOCK, 1), F32),
                                                                            jnp.zeros((FOX_BLOCK, DH), F32)))
            o_ref[ri, :] = acc / l
            lse_ref[ri, :] = jnp.broadcast_to(m + jnp.log(l), (FOX_BLOCK, LANES))
            return 0

        lax.fori_loop(0, nb, qblock, 0)

    o = jax.ShapeDtypeStruct((n, HEADS * DH), F32)
    return pl.pallas_call(
        body, name="fox_fwd", grid=(b_loc, HEADS),
        in_specs=[col(0), col(HEADS), col(2 * HEADS), ps_spec, smem, vec, vec],
        out_specs=[blk, blk], out_shape=[o, o],
        scratch_shapes=[pltpu.VMEM((t, DH), BF16)] * 3 + [pltpu.VMEM((t, LANES), F32), pltpu.VMEM((nb, 8, LANES), F32)],
        compiler_params=_cparams(("arbitrary", "arbitrary")),
    )(pf, pf, pf, ps, f_bias, gq, gk)


def _fox_bwd(pf, ps, f_bias, gq, gk, d_ob, ob, lse, dps_in, b_loc, t):
    n = b_loc * t
    nb = t // FOX_BLOCK
    qt = min(FOX_TILE, t)
    scale = DH ** -0.5
    col, ps_spec, smem, vec, blk = _fox_specs(t)

    def body(q_ref, k_ref, v_ref, ps_ref, fb_ref, gq_ref, gk_ref, do_ref, o_ref, lse_ref, dpsi_ref,
             dq_ref, dk_ref, dv_ref, dps_ref, dsm_ref, qs, ks, vs, ccol, crow, dos, dl, dqa, dcr, dcq):
        b, h = pl.program_id(0), pl.program_id(1)
        rq, _, f_in = _fox_prologue(q_ref, k_ref, v_ref, ps_ref, fb_ref, gq_ref, gk_ref, h, t, qs, ks, vs, ccol, crow)
        dov = do_ref[...]
        dos[...] = _bf(dov)
        dl[...] = jnp.broadcast_to(jnp.sum(dov * o_ref[...], axis=1, keepdims=True), (t, LANES))
        dqa[...] = jnp.zeros_like(dqa)
        dcq[...] = jnp.zeros_like(dcq)
        gkv = gk_ref[...]

        def kblock(j, dgk):
            rj = pl.ds(pl.multiple_of(j * FOX_BLOCK, FOX_BLOCK), FOX_BLOCK)
            kj, vj, cr = ks[rj, :], vs[rj, :], crow[j, 0:1, :]

            def qtile(i, carry):
                dk_acc, dv_acc, dc = carry
                ri = pl.ds(pl.multiple_of(i * qt, qt), qt)
                qi, doi = qs[ri, :], dos[ri, :]
                s = _fox_scores(qi, kj, ccol[ri, :], cr, i * qt, j * FOX_BLOCK)
                p = jnp.exp(s - lse_ref[ri, :])
                ds = p * (_dot_nt(doi, vj) - dl[ri, :])
                dsb = _bf(ds)
                dqa[ri, :] += _dot(dsb, kj)
                dcq[ri, :] += jnp.broadcast_to(jnp.sum(ds, axis=1, keepdims=True), (qt, LANES))
                return (dk_acc + _dot_tn(dsb, qi), dv_acc + _dot_tn(_bf(p), doi),
                        dc - jnp.sum(ds, axis=0, keepdims=True))

            z = jnp.zeros((FOX_BLOCK, DH), F32)
            dk_acc, dv_acc, dc = lax.fori_loop((j * FOX_BLOCK) // qt, t // qt, qtile,
                                               (z, z, jnp.zeros((1, LANES), F32)))
            dv_ref[rj, :] = _bf(dv_acc)
            dcr[pl.ds(pl.multiple_of(j * 8, 8), 8), :] = jnp.broadcast_to(dc, (8, LANES))
            kraw = k_ref[rj, :]
            rk = lax.rsqrt(jnp.mean(kraw * kraw, axis=1, keepdims=True) + EPS)
            dkn = dk_acc * scale
            gy = dkn * gkv
            dk_ref[rj, :] = _bf(rk * gy - kraw * (rk * rk * rk * (1.0 / DH)) * jnp.sum(gy * kraw, axis=1, keepdims=True))
            return dgk + jnp.sum(dkn * kraw * rk, axis=0, keepdims=True)

        dgk = lax.fori_loop(0, nb, kblock, jnp.zeros((1, DH), F32))

        q = q_ref[...]
        dqn = dqa[...] * scale
        gy = dqn * gq_ref[...]
        dq_ref[...] = _bf(rq * gy - q * (rq * rq * rq * (1.0 / DH)) * jnp.sum(gy * q, axis=1, keepdims=True))
        dgq = jnp.sum(dqn * q * rq, axis=0, keepdims=True)

        r = lax.broadcasted_iota(jnp.int32, (FOX_BLOCK, FOX_BLOCK), 0)
        c = lax.broadcasted_iota(jnp.int32, (FOX_BLOCK, FOX_BLOCK), 1)
        triuf = (r <= c).astype(F32)

        def rev(jj, carry):
            j = nb - 1 - jj
            rows = pl.ds(pl.multiple_of(j * FOX_BLOCK, FOX_BLOCK), FOX_BLOCK)
            rowv = dcr[pl.ds(pl.multiple_of(j * 8, 8), 1), :]
            colv = jnp.sum(jnp.where(c >= r, jnp.broadcast_to(rowv, (FOX_BLOCK, LANES)), 0.0), axis=1, keepdims=True)
            qcol = dcq[rows, :]
            dl[rows, :] = colv + _dot_exact_l(triuf, qcol, terms=3) + carry
            return carry + jnp.sum(rowv, axis=1, keepdims=True) + jnp.sum(qcol, axis=0, keepdims=True)

        lax.fori_loop(0, nb, rev, jnp.zeros((1, LANES), F32))
        d_ff = dl[...] * _sigmoid(-f_in)
        lane = lax.broadcasted_iota(jnp.int32, (t, LANES), 1)

        @pl.when(h == 0)
        def _():
            dps_ref[...] = dpsi_ref[...]

        dps_ref[...] += jnp.where(lane == LANE_FF + h, d_ff, 0.0)

        lane1 = lax.broadcasted_iota(jnp.int32, (1, LANES), 1)
        d_fb = jnp.sum(d_ff, axis=0, keepdims=True)
        small = _stack_rows([dgq, dgk, jnp.where(lane1 == h, d_fb, 0.0)], 8)

        @pl.when((b == 0) & (h == 0))
        def _():
            dsm_ref[...] = jnp.zeros_like(dsm_ref)

        dsm_ref[...] += small

    ob_ = jax.ShapeDtypeStruct((n, HEADS * DH), BF16)
    return pl.pallas_call(
        body, name="fox_bwd", grid=(b_loc, HEADS),
        in_specs=[col(0), col(HEADS), col(2 * HEADS), ps_spec, smem, vec, vec, blk, blk, blk, ps_spec],
        out_specs=[blk, blk, blk, ps_spec, pl.BlockSpec((8, LANES), lambda b, h: (0, 0))],
        out_shape=[ob_, ob_, ob_, jax.ShapeDtypeStruct((n, LANES), F32), jax.ShapeDtypeStruct((8, LANES), F32)],
        scratch_shapes=([pltpu.VMEM((t, DH), BF16)] * 3 + [pltpu.VMEM((t, LANES), F32), pltpu.VMEM((nb, 8, LANES), F32)]
                        + [pltpu.VMEM((t, DH), BF16), pltpu.VMEM((t, LANES), F32), pltpu.VMEM((t, DH), F32),
                           pltpu.VMEM((8 * nb, LANES), F32), pltpu.VMEM((t, LANES), F32)]),
        compiler_params=_cparams(("arbitrary", "arbitrary")),
    )(pf, pf, pf, ps, f_bias, gq, gk, d_ob, ob, lse, dps_in)


class _NoExchange:
    def late_weights(self, after):
        return {}

    def grads_ready(self, grads, tie):
        return tie


def _local_step(x, target, w, b_loc, t, comm=None):
    comm = comm or _NoExchange()
    w = dict(w)
    xf = x
    u = _rms_fwd(xf, w["norm_mix_g"], "rms_mix")
    pg = _mm(u, w["w_gdn"], name="proj_gdn")
    pf = _mm(u, w["w_fox"], name="proj_fox")
    pgate = _mm(u, w["w_gate"], name="proj_gate")
    ps = _mm(u, w["w_small"], name="proj_small")
    oa, o_raw, s_all = _gdn_fwd(pg, ps, w["conv_w"], w["a_log"], w["dt_bias"], w["gdn_norm_g"], b_loc, t)
    ob, lse = _fox_fwd(pf, ps, w["f_bias"], w["fox_q_norm_g"], w["fox_k_norm_g"], b_loc, t)
    w.update(comm.late_weights(ob))
    ya = _mm(oa, w["w_proj_gdn"], name="proj_a")
    yb = _mm(ob, w["w_proj_fox"], name="proj_b")
    merged = _merge_fwd(ya, yb, pgate)
    h = _mm(merged, w["w_out"], name="proj_out", epi=lambda acc, xr: acc + xr, extras=(xf,))
    hn = _rms_fwd(h, w["norm_mlp_g"], "rms_mlp")
    up, act = _mm(hn, w["w_up"], name="mlp_up", out_dtype=BF16, out2=(_relu2, BF16))
    out = _mm(act, w["w_down"], name="mlp_down", epi=lambda acc, hr: acc + hr, extras=(h,))
    d_out, d_out16, loss_blk = _loss_bwd(out, target)

    g = {}
    g["w_down"] = _mm(act, d_out16, name="dw_down", ta=True, out_dtype=BF16)
    d_up = _mm(d_out16, w["w_down"], name="d_up", tb=True, out_dtype=BF16,
               epi=lambda acc, upr: acc * (2.0 * jnp.maximum(upr.astype(F32), 0.0)), extras=(up,))
    g["w_up"] = _mm(hn, d_up, name="dw_up", ta=True, out_dtype=BF16)
    mlp_gain = comm.grads_ready({"w_down": g["w_down"], "w_up": g["w_up"]}, w["norm_mlp_g"])
    d_hn = _mm(d_up, w["w_up"], name="d_hn", tb=True)
    dh, dh16, g["norm_mlp_g"] = _rms_bwd(d_hn, h, mlp_gain, d_out, "rms_mlp_bwd")
    g["w_out"] = _mm(merged, dh16, name="dw_out", ta=True, out_dtype=BF16)
    dm = _mm(dh16, w["w_out"], name="d_merged", tb=True)
    dya, dyb, dgate_a, dgate_b = _merge_bwd(dm, ya, yb, pgate)
    g["w_proj_gdn"] = _mm(oa, dya, name="dw_proj_a", ta=True, out_dtype=BF16)
    g["w_proj_fox"] = _mm(ob, dyb, name="dw_proj_b", ta=True, out_dtype=BF16)
    gdn_gain = comm.grads_ready({"w_out": g["w_out"], "w_proj_gdn": g["w_proj_gdn"], "w_proj_fox": g["w_proj_fox"]},
                                w["gdn_norm_g"])
    d_oa = _mm(dya, w["w_proj_gdn"], name="d_oa", tb=True)
    d_ob = _mm(dyb, w["w_proj_fox"], name="d_ob", tb=True)
    dgq, dgk, dgv, dgz, dps, dcw, gdn_small = _gdn_bwd(pg, ps, w["conv_w"], w["a_log"], w["dt_bias"], gdn_gain,
                                                       d_oa, o_raw, s_all, b_loc, t)
    dfq, dfk, dfv, dps, fox_small = _fox_bwd(pf, ps, w["f_bias"], w["fox_q_norm_g"], w["fox_k_norm_g"],
                                             d_ob, ob, lse, dps, b_loc, t)
    segs = [(dgq, "w_gdn", 0), (dgk, "w_gdn", 1024), (dgv, "w_gdn", 2048), (dgz, "w_gdn", 3072),
            (dfq, "w_fox", 0), (dfk, "w_fox", 1024), (dfv, "w_fox", 2048),
            (dgate_a, "w_gate", 0), (dgate_b, "w_gate", 1024)]
    dws = [_mm(u, dps, name="dw_small", ta=True, out_dtype=BF16)]
    dws += [_mm(u, dseg, name=f"dw_in_{idx}", ta=True, out_dtype=BF16) for idx, (dseg, _, _) in enumerate(segs)]
    g["w_in_parts"] = dws
    mix_gain = comm.grads_ready({"w_in_parts": dws}, w["norm_mix_g"])
    du = _mm(dps, w["w_small"], name="du_small", tb=True)
    for idx, (dseg, wname, off) in enumerate(segs):
        du = _mm(dseg, w[wname], name=f"du_{idx}", tb=True, b_koff=off,
                 epi=lambda acc, prev: acc + prev, extras=(du,))
    grad_x, _, g["norm_mix_g"] = _rms_bwd(du, xf, mix_gain, dh, "rms_mix_bwd")
    g["conv"] = dcw
    g["gdn_small"] = gdn_small
    g["fox_small"] = fox_small
    return loss_blk, grad_x, g


def _position():
    x, y, c = lax.axis_index("x"), lax.axis_index("y"), lax.axis_index("c")
    return x, y, c


def _to_bf16(arrs, name):
    n = len(arrs)

    def body(*refs):
        for i in range(n):
            refs[n + i][...] = _bf(refs[i][...])

    return pl.pallas_call(
        body, name=name,
        out_shape=[jax.ShapeDtypeStruct(a.shape, BF16) for a in arrs],
        compiler_params=_cparams(),
    )(*arrs)


def _all_gather(arrs, name):
    n = len(arrs)
    hbm = pl.BlockSpec(memory_space=pl.ANY)

    def body(*refs):
        ins, outs = refs[:n], refs[n:2 * n]
        send, recv, loc = refs[2 * n:]
        x, y, c = _position()
        me = 4 * x + 2 * y + c
        sibling = (x, y, 1 - c)
        chips = [(1 - x, y), (x, 1 - y), (1 - x, 1 - y)]

        def idx(px, py, pc):
            return 4 * px + 2 * py + pc

        def cp(a, k, block, to, src=None):
            return pltpu.make_async_remote_copy(
                src_ref=outs[a].at[block] if src is None else src, dst_ref=outs[a].at[block],
                send_sem=send.at[a, k], recv_sem=recv.at[a, k], device_id=to, device_id_type=MESH)

        mine = [pltpu.make_async_copy(ins[a], outs[a].at[me], loc.at[a]) for a in range(n)]
        for m in mine:
            m.start()
        first = []
        for a in range(n):
            first.append(cp(a, 0, me, sibling, src=ins[a]))
            first += [cp(a, 1 + j, me, (*chip, c), src=ins[a]) for j, chip in enumerate(chips)]
        for f in first:
            f.start()
        passed = []
        for j, chip in enumerate(chips):
            for a in range(n):
                cp(a, 1 + j, idx(*chip, c), (x, y, c)).wait_recv()
                p = cp(a, 4 + j, idx(*chip, c), sibling)
                p.start()
                passed.append(p)
        for a in range(n):
            cp(a, 0, idx(x, y, 1 - c), (x, y, c)).wait_recv()
            for j, chip in enumerate(chips):
                cp(a, 4 + j, idx(*chip, 1 - c), (x, y, c)).wait_recv()
        for f in first + passed:
            f.wait_send()
        for m in mine:
            m.wait()

    return pl.pallas_call(
        body, name=name,
        in_specs=[hbm] * n, out_specs=[hbm] * n,
        out_shape=[jax.ShapeDtypeStruct((N_DEV,) + a.shape, a.dtype) for a in arrs],
        scratch_shapes=[pltpu.SemaphoreType.DMA((n, 7)), pltpu.SemaphoreType.DMA((n, 7)), pltpu.SemaphoreType.DMA((n,))],
        compiler_params=pltpu.CompilerParams(has_side_effects=True),
    )(*arrs)


def _peer(x, y, c, rel):
    return ((1 - x) if rel & 4 else x, (1 - y) if rel & 2 else y, (1 - c) if rel & 1 else c)


def _exchange(arrs, name):
    n = len(arrs)
    hbm = pl.BlockSpec(memory_space=pl.ANY)

    def body(*refs):
        ins, outs = refs[:n], refs[n:2 * n]
        send, recv, loc = refs[2 * n:]
        x, y, c = _position()
        me = 4 * x + 2 * y + c
        mine = [pltpu.make_async_copy(ins[a].at[me], outs[a].at[me], loc.at[a]) for a in range(n)]
        for m in mine:
            m.start()
        copies = []
        for rel in range(1, N_DEV):
            px, py, pc = _peer(x, y, c, rel)
            for a in range(n):
                copies.append(pltpu.make_async_remote_copy(
                    src_ref=ins[a].at[4 * px + 2 * py + pc], dst_ref=outs[a].at[me],
                    send_sem=send.at[a, rel - 1], recv_sem=recv.at[a, rel - 1],
                    device_id=(px, py, pc), device_id_type=MESH))
        for cpy in copies:
            cpy.start()
        for cpy in copies:
            cpy.wait()
        for m in mine:
            m.wait()

    return pl.pallas_call(
        body, name=name,
        in_specs=[hbm] * n, out_specs=[hbm] * n,
        out_shape=[jax.ShapeDtypeStruct(a.shape, a.dtype) for a in arrs],
        scratch_shapes=[pltpu.SemaphoreType.DMA((n, 7)), pltpu.SemaphoreType.DMA((n, 7)), pltpu.SemaphoreType.DMA((n,))],
        compiler_params=pltpu.CompilerParams(has_side_effects=True),
    )(*arrs)


HBM_SPEC = pl.BlockSpec(memory_space=pltpu.HBM)
SEM_SPEC = pl.BlockSpec(memory_space=pltpu.SEMAPHORE)
DATAFLOW = pltpu.SideEffectType.DATAFLOW_SIDE_EFFECTING


def _push_start(arrs, slots, name):
    n = len(arrs)
    land_shapes = [a.shape if slots else (N_DEV,) + a.shape for a in arrs]

    def body(*refs):
        ins, lands, sends, recvs, token = refs[:n], refs[n:2 * n], refs[2 * n:3 * n], refs[3 * n:4 * n], refs[-1]
        x, y, c = _position()
        me = 4 * x + 2 * y + c
        for rel in range(1, N_DEV):
            px, py, pc = _peer(x, y, c, rel)
            for a in range(n):
                pltpu.make_async_remote_copy(
                    src_ref=ins[a].at[4 * px + 2 * py + pc] if slots else ins[a], dst_ref=lands[a].at[me],
                    send_sem=sends[a], recv_sem=recvs[a], device_id=(px, py, pc), device_id_type=MESH).start()
        token[...] = jnp.zeros_like(token)

    sem = pltpu.SemaphoreType.DMA(())
    outs = pl.pallas_call(
        body, name=name,
        in_specs=[HBM_SPEC] * (2 * n),
        out_shape=(*[sem] * (2 * n), *[pltpu.HBM(a.shape, a.dtype) for a in arrs],
                   *[pltpu.HBM(s, a.dtype) for s, a in zip(land_shapes, arrs)], jax.ShapeDtypeStruct((8, LANES), F32)),
        out_specs=(*[SEM_SPEC] * (2 * n), *[HBM_SPEC] * (2 * n), pl.BlockSpec(memory_space=pltpu.VMEM)),
        input_output_aliases={i: 2 * n + i for i in range(2 * n)},
        compiler_params=pltpu.CompilerParams(has_side_effects=DATAFLOW),
    )(*[pltpu.with_memory_space_constraint(a, pltpu.HBM) for a in arrs],
      *[pltpu.with_memory_space_constraint(lax.empty(s, a.dtype), pltpu.HBM) for s, a in zip(land_shapes, arrs)])
    return dict(sends=list(outs[:n]), recvs=list(outs[n:2 * n]), ins=list(outs[2 * n:3 * n]),
                lands=list(outs[3 * n:4 * n]), token=outs[-1])


def _push_wait(started, after, name):
    n = len(started["ins"])

    def body(*refs):
        lands, sends, recvs = refs[n:2 * n], refs[2 * n:3 * n], refs[3 * n:4 * n]
        x, y, c = _position()
        for a in range(n):
            seven = lands[a].at[pl.ds(0, N_DEV - 1)]
            drain = pltpu.make_async_remote_copy(src_ref=seven, dst_ref=seven, send_sem=sends[a], recv_sem=recvs[a],
                                                 device_id=(x, y, c), device_id_type=MESH)
            drain.wait_send()
            drain.wait_recv()

    both = started["ins"] + started["lands"]
    outs = pl.pallas_call(
        body, name=name,
        in_specs=[HBM_SPEC] * (2 * n) + [SEM_SPEC] * (2 * n) + [pl.BlockSpec(memory_space=pl.ANY)],
        out_shape=tuple(pltpu.HBM(a.shape, a.dtype) for a in both), out_specs=tuple([HBM_SPEC] * (2 * n)),
        input_output_aliases={i: i for i in range(2 * n)},
        compiler_params=pltpu.CompilerParams(has_side_effects=DATAFLOW),
    )(*both, *started["sends"], *started["recvs"], after)
    return list(outs[:n]), list(outs[n:])


def _all_reduce_small(buf, name):
    rows = buf.shape[0]

    def body(in_ref, out_ref, slots, send, recv):
        x, y, c = _position()
        me = 4 * x + 2 * y + c
        slots[me] = in_ref[...]
        copies = []
        for rel in range(1, N_DEV):
            copies.append(pltpu.make_async_remote_copy(
                src_ref=in_ref, dst_ref=slots.at[me], send_sem=send.at[rel - 1], recv_sem=recv.at[rel - 1],
                device_id=_peer(x, y, c, rel), device_id_type=MESH))
        for cpy in copies:
            cpy.start()
        for cpy in copies:
            cpy.wait()
        tot = slots[0]
        for d in range(1, N_DEV):
            tot = tot + slots[d]
        out_ref[...] = tot

    return pl.pallas_call(
        body, name=name,
        out_shape=jax.ShapeDtypeStruct((rows, LANES), F32),
        in_specs=[pl.BlockSpec(memory_space=pltpu.VMEM)], out_specs=pl.BlockSpec(memory_space=pltpu.VMEM),
        scratch_shapes=[pltpu.VMEM((N_DEV, rows, LANES), F32), pltpu.SemaphoreType.DMA((7,)),
                        pltpu.SemaphoreType.DMA((7,))],
        compiler_params=pltpu.CompilerParams(has_side_effects=True),
    )(buf)


def _adam_math(g, w, m, v):
    m = ADAM_B1 * m + (1.0 - ADAM_B1) * g
    v = ADAM_B2 * v + (1.0 - ADAM_B2) * (g * g)
    m_hat = m / (1.0 - ADAM_B1 ** ADAM_STEP)
    v_hat = v / (1.0 - ADAM_B2 ** ADAM_STEP)
    delta = -ADAM_LR * (m_hat / (jnp.sqrt(v_hat) + ADAM_EPS) + ADAM_WD * w)
    return delta, m, v


def _adam_shard(me, parts, mine, w, m, v, name):
    r, c = w.shape
    tr = min(r, 128)

    def body(me_ref, p_ref, own_ref, w_ref, m_ref, v_ref, g_ref, d_ref, nm_ref, nv_ref):
        own = own_ref[0].astype(F32)
        g = None
        for s in range(N_DEV):
            term = jnp.where(me_ref[0] == s, own, p_ref[s].astype(F32))
            g = term if g is None else g + term
        d, nm, nv = _adam_math(g, w_ref[...], m_ref[...], v_ref[...])
        g_ref[...] = g
        d_ref[...] = d
        nm_ref[...] = nm
        nv_ref[...] = nv

    row = pl.BlockSpec((tr, c), lambda i, me_ref: (i, 0))
    o = jax.ShapeDtypeStruct((r, c), F32)
    return pl.pallas_call(
        body, name=name,
        grid_spec=pltpu.PrefetchScalarGridSpec(
            num_scalar_prefetch=1, grid=(r // tr,),
            in_specs=[pl.BlockSpec((N_DEV, tr, c), lambda i, me_ref: (0, i, 0)),
                      pl.BlockSpec((1, tr, c), lambda i, me_ref: (me_ref[0], i, 0)), row, row, row],
            out_specs=[row] * 4),
        out_shape=[o] * 4,
        compiler_params=_cparams(("parallel",)),
    )(me, parts, mine, w, m, v)


def _adam_small(g, w, m, v):
    def body(g_ref, w_ref, m_ref, v_ref, d_ref, nm_ref, nv_ref):
        d, nm, nv = _adam_math(g_ref[...], w_ref[...], m_ref[...], v_ref[...])
        d_ref[...] = d
        nm_ref[...] = nm
        nv_ref[...] = nv

    o = jax.ShapeDtypeStruct(g.shape, F32)
    return pl.pallas_call(body, name="adam_small", out_shape=[o] * 3, compiler_params=_cparams())(g, w, m, v)


def _split_w_in(w_full):
    o = IN_OFF
    w_gdn = w_full[:, o["gq"]:o["ga"]]
    w_fox = w_full[:, o["fq"]:o["ff"]]
    w_gate = w_full[:, o["gate_a"]:o["end"]]
    w_small = jnp.concatenate([w_full[:, o["ga"]:o["fq"]], w_full[:, o["ff"]:o["gate_a"]],
                               jnp.zeros((w_full.shape[0], LANES - 24), w_full.dtype)], axis=1)
    return w_gdn, w_fox, w_gate, w_small


def _join_w_in(parts):
    small = parts[0]
    return jnp.concatenate(parts[1:5] + [small[:, 0:16]] + parts[5:8] + [small[:, 16:24]] + parts[8:10], axis=1)


def _rows128(a, rows):
    flat = a.reshape(-1)
    flat = jnp.concatenate([flat, jnp.zeros((rows * LANES - flat.shape[0],), flat.dtype)])
    return flat.reshape(rows, LANES)


def kernel(x, norm_mix_g, w_in, gdn_conv_w, gdn_a_log, gdn_dt_bias, gdn_norm_g, fox_q_norm_g, fox_k_norm_g, fox_f_bias, w_proj_gdn, w_proj_fox, w_out, norm_mlp_g, w_up, w_down, loss_target, m_norm_mix_g, m_w_in, m_gdn_conv_w, m_gdn_a_log, m_gdn_dt_bias, m_gdn_norm_g, m_fox_q_norm_g, m_fox_k_norm_g, m_fox_f_bias, m_w_proj_gdn, m_w_proj_fox, m_w_out, m_norm_mlp_g, m_w_up, m_w_down, v_norm_mix_g, v_w_in, v_gdn_conv_w, v_gdn_a_log, v_gdn_dt_bias, v_gdn_norm_g, v_fox_q_norm_g, v_fox_k_norm_g, v_fox_f_bias, v_w_proj_gdn, v_w_proj_fox, v_w_out, v_norm_mlp_g, v_w_up, v_w_down):
    b_loc, t, d = x.shape
    n = b_loc * t
    me = 4 * lax.axis_index("x") + 2 * lax.axis_index("y") + lax.axis_index("c")

    late_names = ["w_proj_gdn", "w_proj_fox", "w_out", "w_up", "w_down"]
    big16 = _to_bf16([w_in[0], w_proj_gdn[0], w_proj_fox[0], w_out[0], w_up[0], w_down[0]], "weights_to_bf16")
    g_in, g_conv = _all_gather([big16[0], gdn_conv_w[0]], "gather_w_in")
    late = _push_start(list(big16[1:]), False, "gather_late_start")
    w_full = g_in.transpose(1, 0, 2).reshape(d, N_DEV * w_in.shape[2])
    w_gdn, w_fox, w_gate, w_small = _split_w_in(w_full)
    weights = {
        "w_gdn": w_gdn, "w_fox": w_fox, "w_gate": w_gate, "w_small": w_small,
        "conv_w": g_conv.transpose(1, 0, 2).reshape(CONV_K, 3 * d),
        "norm_mix_g": norm_mix_g + late["token"][0:1, 0:1], "norm_mlp_g": norm_mlp_g, "a_log": gdn_a_log,
        "dt_bias": gdn_dt_bias, "gdn_norm_g": gdn_norm_g, "fox_q_norm_g": fox_q_norm_g, "fox_k_norm_g": fox_k_norm_g,
        "f_bias": fox_f_bias,
    }
    c_in, c_up = w_in.shape[2], w_up.shape[2]

    class _Exchange:
        def __init__(self):
            self.started = []

        def late_weights(self, after):
            shards, lands = _push_wait(late, after, "gather_late_wait")
            full = [lax.dynamic_update_index_in_dim(land, shard, me, 0) for land, shard in zip(lands, shards)]
            g_pa, g_pb, g_out, g_up, g_down = full
            return {"w_proj_gdn": g_pa.reshape(d, d), "w_proj_fox": g_pb.reshape(d, d), "w_out": g_out.reshape(d, d),
                    "w_up": g_up.transpose(1, 0, 2).reshape(d, D_FF), "w_down": g_down.reshape(D_FF, d)}

        def grads_ready(self, grads, tie):
            layout = {
                "w_in_parts": lambda p: _join_w_in(p).reshape(d, N_DEV, c_in).transpose(1, 0, 2),
                "w_up": lambda a: a.reshape(d, N_DEV, c_up).transpose(1, 0, 2),
                "w_down": lambda a: a.reshape(N_DEV, D_FF // N_DEV, d),
            }
            names = list(grads)
            arrs = [layout.get(k, lambda a: a.reshape(N_DEV, d // N_DEV, d))(grads[k]) for k in names]
            st = _push_start(arrs, True, "grads_start_" + names[0])
            self.started.append((names, st))
            return tie + st["token"][0:1, 0:1]

    comm = _Exchange()
    loss_blk, grad_x, g = _local_step(x.reshape(n, d), loss_target.reshape(n, d), weights, b_loc, t, comm)

    me1 = jnp.reshape(me, (1,)).astype(jnp.int32)
    shards = {"w_in_parts": (w_in, m_w_in, v_w_in), "w_proj_gdn": (w_proj_gdn, m_w_proj_gdn, v_w_proj_gdn),
              "w_proj_fox": (w_proj_fox, m_w_proj_fox, v_w_proj_fox), "w_out": (w_out, m_w_out, v_w_out),
              "w_up": (w_up, m_w_up, v_w_up), "w_down": (w_down, m_w_down, v_w_down)}
    adam = {}
    for names, st in comm.started:
        mine, parts = _push_wait(st, grad_x, "grads_wait_" + names[0])
        for k, own, part in zip(names, mine, parts):
            wi, mi, vi = shards[k]
            adam[k] = [r[None] for r in _adam_shard(me1, part, own, wi[0], mi[0], vi[0], "adam_" + k)]
    big_out = [adam[k] for k in ["w_in_parts"] + late_names]

    conv_rows = CONV_K * 3 * d // LANES
    conv_g = g["conv"].transpose(1, 0, 2).reshape(conv_rows, LANES)
    buf = jnp.concatenate([conv_g, g["norm_mix_g"].reshape(8, LANES), g["norm_mlp_g"].reshape(8, LANES),
                           g["gdn_small"], g["fox_small"], loss_blk], axis=0)
    tot = _all_reduce_small(buf, "all_reduce_small")
    o = conv_rows
    conv_full = tot[0:o].reshape(CONV_K, 3 * d)
    c_conv = gdn_conv_w.shape[2]
    g_conv_shard = lax.dynamic_slice(conv_full, (0, me * c_conv), (CONV_K, c_conv))
    g_mix = tot[o:o + 8].reshape(1, d)
    g_mlp = tot[o + 8:o + 16].reshape(1, d)
    gs, fs = tot[o + 16:o + 24], tot[o + 24:o + 32]
    loss = tot[o + 32, 0]
    small_g = [g_mix, g_conv_shard[None], gs[0:1, 0:HEADS], gs[1:2, 0:HEADS], gs[2:3], fs[0:1], fs[1:2], fs[2:3, 0:HEADS],
               g_mlp]
    small_w = [norm_mix_g, gdn_conv_w, gdn_a_log, gdn_dt_bias, gdn_norm_g, fox_q_norm_g, fox_k_norm_g, fox_f_bias,
               norm_mlp_g]
    small_m = [m_norm_mix_g, m_gdn_conv_w, m_gdn_a_log, m_gdn_dt_bias, m_gdn_norm_g, m_fox_q_norm_g, m_fox_k_norm_g,
               m_fox_f_bias, m_norm_mlp_g]
    small_v = [v_norm_mix_g, v_gdn_conv_w, v_gdn_a_log, v_gdn_dt_bias, v_gdn_norm_g, v_fox_q_norm_g, v_fox_k_norm_g,
               v_fox_f_bias, v_norm_mlp_g]
    row_counts = [-(-a.size // (8 * LANES)) * 8 for a in small_w]

    def pack(arrs):
        return jnp.concatenate([_rows128(a, rc) for a, rc in zip(arrs, row_counts)], axis=0)

    sd, sm, sv = _adam_small(pack(small_g), pack(small_w), pack(small_m), pack(small_v))

    def unpack(p):
        outs, r0 = [], 0
        for a, rc in zip(small_w, row_counts):
            outs.append(p[r0:r0 + rc].reshape(-1)[:a.size].reshape(a.shape))
            r0 += rc
        return outs

    small_out = [small_g_i.reshape(w_i.shape) for small_g_i, w_i in zip(small_g, small_w)], unpack(sd), unpack(sm), unpack(sv)

    def ordered(kind):
        s = small_out[kind]
        bo = [b[kind] for b in big_out]
        return [s[0], bo[0], s[1], s[2], s[3], s[4], s[5], s[6], s[7], bo[1], bo[2], bo[3], s[8], bo[4], bo[5]]

    return (loss, grad_x.reshape(b_loc, t, d), *ordered(0), *ordered(1), *ordered(2), *ordered(3))
```

```python
import functools

import jax
import jax.numpy as jnp
from jax import lax
from jax.experimental import pallas as pl
from jax.experimental.pallas import tpu as pltpu

F32 = jnp.float32
BF16 = jnp.bfloat16
HI = lax.Precision.HIGHEST
MESH = pl.DeviceIdType.MESH

N_DEV = 8
D_MODEL = 1024
HEADS = 8
DH = 128
CONV_K = 4
CHUNK = 64
GDN_GROUP = 8
FOX_BLOCK = 128
FOX_TILE = 512
FOX_SHORT = 512
D_FF = 4 * D_MODEL
EPS = 1e-6
LANES = 128
NEG = -1e30
IN_OFF = {"gq": 0, "gk": 1024, "gv": 2048, "gz": 3072, "ga": 4096, "gb": 4104, "fq": 4112, "fk": 5136,
          "fv": 6160, "ff": 7184, "gate_a": 7192, "gate_b": 8216, "end": 9240}
LANE_GA, LANE_GB, LANE_FF = 0, 8, 16

ADAM_LR = 0.001
ADAM_B1 = 0.9
ADAM_B2 = 0.999
ADAM_EPS = 1e-08
ADAM_WD = 0.01
ADAM_STEP = 10

VMEM_LIMIT = 56 * 1024 * 1024


def _cparams(sem=None):
    return pltpu.CompilerParams(dimension_semantics=sem, vmem_limit_bytes=VMEM_LIMIT)


def _sigmoid(x):
    return 1.0 / (1.0 + jnp.exp(-x))


def _softplus(x):
    return jnp.maximum(x, 0.0) + jnp.log(1.0 + jnp.exp(-jnp.abs(x)))


def _dot(a, b, prec=None):
    return lax.dot_general(a, b, (((1,), (0,)), ((), ())), precision=prec, preferred_element_type=F32)


def _dot_nt(a, b, prec=None):
    return lax.dot_general(a, b, (((1,), (1,)), ((), ())), precision=prec, preferred_element_type=F32)


def _dot_tn(a, b, prec=None):
    return lax.dot_general(a, b, (((0,), (0,)), ((), ())), precision=prec, preferred_element_type=F32)


def _bf(x):
    return x.astype(BF16)


MM_TILE = 1024


def _mm(a, b, *, name, ta=False, tb=False, out_dtype=F32, epi=None, extras=(), out2=None,
        b_koff=0, tm=MM_TILE, tn=MM_TILE, tk=MM_TILE):
    m = a.shape[1] if ta else a.shape[0]
    kdim = a.shape[0] if ta else a.shape[1]
    n = b.shape[0] if tb else b.shape[1]
    tm, tn, tk = min(tm, m), min(tn, n), min(tk, kdim)
    nk = kdim // tk
    grid = (m // tm, n // tn, nk)
    koff = b_koff // tk
    a_spec = pl.BlockSpec((tk, tm), lambda i, j, k: (k, i)) if ta else pl.BlockSpec((tm, tk), lambda i, j, k: (i, k))
    if tb:
        b_spec = pl.BlockSpec((tn, tk), lambda i, j, k: (j, k + koff))
    else:
        b_spec = pl.BlockSpec((tk, tn), lambda i, j, k: (k + koff, j))
    o_spec = pl.BlockSpec((tm, tn), lambda i, j, k: (i, j))
    n_e = len(extras)
    n_o = 1 if out2 is None else 2
    dims = (((0 if ta else 1,), (1 if tb else 0,)), ((), ()))

    def body(a_ref, b_ref, *rest):
        e_refs, o_refs = rest[:n_e], rest[n_e:n_e + n_o]
        prod = lax.dot_general(_bf(a_ref[...]), _bf(b_ref[...]), dims, preferred_element_type=F32)

        def finish(r):
            if out2 is not None:
                o_refs[1][...] = out2[0](r).astype(out2[1])
            if epi is not None:
                r = epi(r, *[e[...] for e in e_refs])
            o_refs[0][...] = r.astype(out_dtype)

        if nk == 1:
            finish(prod)
        else:
            acc = rest[n_e + n_o]
            k = pl.program_id(2)

            @pl.when(k == 0)
            def _():
                acc[...] = prod

            @pl.when(k > 0)
            def _():
                acc[...] += prod

            @pl.when(k == nk - 1)
            def _():
                finish(acc[...])

    shapes = [jax.ShapeDtypeStruct((m, n), out_dtype)]
    if out2 is not None:
        shapes.append(jax.ShapeDtypeStruct((m, n), out2[1]))
    res = pl.pallas_call(
        body, name=name, grid=grid,
        in_specs=[a_spec, b_spec] + [o_spec] * n_e,
        out_specs=[o_spec] * n_o, out_shape=shapes,
        scratch_shapes=[] if nk == 1 else [pltpu.VMEM((tm, tn), F32)],
        compiler_params=_cparams(("parallel", "parallel", "arbitrary")),
    )(a, b, *extras)
    return res[0] if out2 is None else res


def _relu2(x):
    r = jnp.maximum(x, 0.0)
    return r * r


ROWS = 512


def _rms_fwd(x, g, name):
    n, d = x.shape

    def body(x_ref, g_ref, u_ref):
        xv = x_ref[...]
        r = lax.rsqrt(jnp.mean(xv * xv, axis=1, keepdims=True) + EPS)
        u_ref[...] = _bf(xv * r * g_ref[...])

    return pl.pallas_call(
        body, name=name, grid=(n // ROWS,),
        in_specs=[pl.BlockSpec((ROWS, d), lambda i: (i, 0)), pl.BlockSpec((1, d), lambda i: (0, 0))],
        out_specs=pl.BlockSpec((ROWS, d), lambda i: (i, 0)),
        out_shape=jax.ShapeDtypeStruct((n, d), BF16),
        compiler_params=_cparams(("parallel",)),
    )(x, g)


def _rms_bwd(dy, x, g, dres, name):
    n, d = x.shape

    def body(dy_ref, x_ref, g_ref, dres_ref, dx_ref, dx16_ref, dg_ref):
        i = pl.program_id(0)
        xv, dyv = x_ref[...], dy_ref[...]
        r = lax.rsqrt(jnp.mean(xv * xv, axis=1, keepdims=True) + EPS)
        gy = dyv * g_ref[...]
        s = jnp.sum(gy * xv, axis=1, keepdims=True)
        dx = dres_ref[...] + r * gy - xv * (r * r * r * (1.0 / d)) * s
        dx_ref[...] = dx
        dx16_ref[...] = _bf(dx)

        @pl.when(i == 0)
        def _():
            dg_ref[...] = jnp.zeros_like(dg_ref)

        dg_ref[...] += jnp.sum(dyv * xv * r, axis=0, keepdims=True)

    row = pl.BlockSpec((ROWS, d), lambda i: (i, 0))
    vec = pl.BlockSpec((1, d), lambda i: (0, 0))
    return pl.pallas_call(
        body, name=name, grid=(n // ROWS,),
        in_specs=[row, row, vec, row], out_specs=[row, row, vec],
        out_shape=[jax.ShapeDtypeStruct((n, d), F32), jax.ShapeDtypeStruct((n, d), BF16),
                   jax.ShapeDtypeStruct((1, d), F32)],
        compiler_params=_cparams(("arbitrary",)),
    )(dy, x, g, dres)


def _merge_fwd(ya, yb, gate):
    n, d = ya.shape

    def body(ya_ref, yb_ref, ga_ref, gb_ref, o_ref):
        o_ref[...] = _bf(_sigmoid(ga_ref[...]) * ya_ref[...] + _sigmoid(gb_ref[...]) * yb_ref[...])

    row = pl.BlockSpec((ROWS, d), lambda i: (i, 0))
    return pl.pallas_call(
        body, name="merge_fwd", grid=(n // ROWS,),
        in_specs=[row, row, row, pl.BlockSpec((ROWS, d), lambda i: (i, 1))], out_specs=row,
        out_shape=jax.ShapeDtypeStruct((n, d), BF16),
        compiler_params=_cparams(("parallel",)),
    )(ya, yb, gate, gate)


def _merge_bwd(dm, ya, yb, gate):
    n, d = ya.shape

    def body(dm_ref, ya_ref, yb_ref, ga_ref, gb_ref, dya_ref, dyb_ref, dga_ref, dgb_ref):
        dmv = dm_ref[...]
        sa, sb = _sigmoid(ga_ref[...]), _sigmoid(gb_ref[...])
        dya_ref[...] = _bf(dmv * sa)
        dyb_ref[...] = _bf(dmv * sb)
        dga_ref[...] = _bf(dmv * ya_ref[...] * sa * (1.0 - sa))
        dgb_ref[...] = _bf(dmv * yb_ref[...] * sb * (1.0 - sb))

    row = pl.BlockSpec((ROWS, d), lambda i: (i, 0))
    o = jax.ShapeDtypeStruct((n, d), BF16)
    return pl.pallas_call(
        body, name="merge_bwd", grid=(n // ROWS,),
        in_specs=[row, row, row, row, pl.BlockSpec((ROWS, d), lambda i: (i, 1))], out_specs=[row] * 4,
        out_shape=[o] * 4,
        compiler_params=_cparams(("parallel",)),
    )(dm, ya, yb, gate, gate)


def _loss_bwd(out, target):
    n, d = out.shape

    def body(o_ref, t_ref, d_ref, d16_ref, l_ref):
        i = pl.program_id(0)
        err = o_ref[...] - t_ref[...]
        d_ref[...] = err * (1.0 / d)
        d16_ref[...] = _bf(err * (1.0 / d))

        @pl.when(i == 0)
        def _():
            l_ref[...] = jnp.zeros_like(l_ref)

        l_ref[...] += 0.5 * jnp.sum(jnp.mean(err * err, axis=1, keepdims=True), axis=0, keepdims=True)

    row = pl.BlockSpec((ROWS, d), lambda i: (i, 0))
    return pl.pallas_call(
        body, name="loss_bwd", grid=(n // ROWS,),
        in_specs=[row, row], out_specs=[row, row, pl.BlockSpec((8, LANES), lambda i: (0, 0))],
        out_shape=[jax.ShapeDtypeStruct((n, d), F32), jax.ShapeDtypeStruct((n, d), BF16),
                   jax.ShapeDtypeStruct((8, LANES), F32)],
        compiler_params=_cparams(("arbitrary",)),
    )(out, target)


PAD = 8


def _pad_zero(pad_ref):
    t = pad_ref.shape[0] - 2 * PAD
    pad_ref[0:PAD, :] = jnp.zeros((PAD, LANES), F32)
    pad_ref[PAD + t:2 * PAD + t, :] = jnp.zeros((PAD, LANES), F32)


def _shifted(pad_ref, s):
    t = pad_ref.shape[0] - 2 * PAD
    return pad_ref[PAD - s:PAD - s + t, :]


def _conv(x, w_ref, pad_ref):
    t = x.shape[0]
    pad_ref[PAD:PAD + t, :] = x
    y = _shifted(pad_ref, 3) * w_ref[0:1, :]
    y = y + _shifted(pad_ref, 2) * w_ref[1:2, :]
    y = y + _shifted(pad_ref, 1) * w_ref[2:3, :]
    return y + x * w_ref[3:4, :]


def _chunk_consts():
    r = lax.broadcasted_iota(jnp.int32, (CHUNK, CHUNK), 0)
    c = lax.broadcasted_iota(jnp.int32, (CHUNK, CHUNK), 1)
    incl, strict = r >= c, r > c
    return dict(incl=incl, strict=strict, trilf=incl.astype(F32), triuf=(r <= c).astype(F32),
                eye=(r == c).astype(F32))


class _V:
    def __init__(self, xs):
        self.xs = list(xs)

    def __add__(self, o):
        return _ap(lambda x, y: x + y, self, o)

    def __radd__(self, o):
        return _ap(lambda x, y: y + x, self, o)

    def __sub__(self, o):
        return _ap(lambda x, y: x - y, self, o)

    def __rsub__(self, o):
        return _ap(lambda x, y: y - x, self, o)

    def __mul__(self, o):
        return _ap(lambda x, y: x * y, self, o)

    def __rmul__(self, o):
        return _ap(lambda x, y: y * x, self, o)

    def __neg__(self):
        return _ap(lambda x: -x, self)

    def __getitem__(self, idx):
        return _ap(lambda x: x[idx], self)


def _ap(fn, *args):
    n = [len(a.xs) for a in args if isinstance(a, _V)]
    if not n:
        return fn(*args)
    return _V([fn(*[a.xs[i] if isinstance(a, _V) else a for a in args]) for i in range(n[0])])


def _vbf(x):
    return _ap(_bf, x)


def _vdot(a, b):
    return _ap(_dot, a, b)


def _vdot_nt(a, b):
    return _ap(_dot_nt, a, b)


def _vdot_tn(a, b):
    return _ap(_dot_tn, a, b)


def _vexp(x):
    return _ap(jnp.exp, x)


def _vsum(x, axis):
    return _ap(lambda v: jnp.sum(v, axis=axis, keepdims=True), x)


def _vcat(a, b, axis):
    return _ap(lambda x, y: jnp.concatenate([x, y], axis=axis), a, b)


def _vmask(mask, x):
    return _ap(lambda v: jnp.where(mask, v, 0.0), x)


def _split2(x):
    h = _vbf(x)
    return h, _vbf(x - _ap(lambda v: v.astype(F32), h))


def _dot3(a, b, kind=_vdot):
    ah, al = _split2(a)
    bh, bl = _split2(b)
    return kind(ah, bh) + (kind(ah, bl) + kind(al, bh))


def _split(x, terms):
    out = []
    for _ in range(terms):
        h = _vbf(x)
        out.append(h)
        x = x - _ap(lambda v: v.astype(F32), h)
    return out


def _dot_exact_l(m01, x, kind=_vdot, terms=2):
    mb = _bf(m01)
    parts = [kind(mb, xp) for xp in _split(x, terms)]
    return functools.reduce(lambda a, b: a + b, reversed(parts))


def _dot_exact_r(x, m01, kind=_vdot, terms=2):
    mb = _bf(m01)
    parts = [kind(xp, mb) for xp in _split(x, terms)]
    return functools.reduce(lambda a, b: a + b, reversed(parts))


def _inv_unit_lower(a, eye):
    p = -a
    r = p + eye
    p = _dot3(p, p)
    for j in range(1, 6):
        if j < 5:
            y = _dot3(p, _vcat(p, r, 1))
            p, r = y[:, 0:CHUNK], r + y[:, CHUNK:2 * CHUNK]
        else:
            r = r + _dot3(p, r)
    return r


def _gdn_chunk_pre(q, k, v, g128, g64, b128, b64, cs):
    incl = cs["incl"]
    big_g = _dot_exact_l(cs["trilf"], g128)
    gc = big_g[:, 0:CHUNK]
    gr = _dot_exact_r(g64, cs["triuf"], _vdot_tn)
    decay = _ap(lambda d: jnp.where(incl, jnp.exp(jnp.where(incl, d, 0.0)), 0.0), gc - gr)
    kb, qb = _vbf(k), _vbf(q)
    qkk = _vdot_nt(_vcat(qb, kb, 0), kb)
    qk, kk = qkk[0:CHUNK], qkk[CHUNK:2 * CHUNK]
    tm = _inv_unit_lower(_vmask(cs["strict"], b64 * kk * decay), cs["eye"])
    e_g = _vexp(big_g)
    wu = _dot3(tm, _vcat(v * b128, k * (b128 * e_g), 1))
    w, u = wu[:, 0:DH], wu[:, DH:2 * DH]
    g_last = _vsum(g128, 0)
    return dict(big_g=big_g, decay=decay, kk=kk, qk=qk, tm=tm, w=w, u=u, p=qk * decay, q_dec=q * e_g,
                k_dec=k * _vexp(g_last - big_g), dec=_vexp(g_last))


def _gdn_chunk_post(q, k, v, g128, b128, b64, s, ds_next, do, dv_new, big_g, decay, kk, qk, tm, u, v_new, cs):
    e_g = _vexp(big_g)
    vb = v * b128
    kbeta = k * (b128 * e_g)
    q_dec = q * e_g
    g_last = _vsum(g128, 0)
    ekg = _vexp(g_last - big_g)
    k_dec = k * ekg
    dec = _vexp(g_last)
    kb, qb, sb = _vbf(k), _vbf(q), _vbf(s)
    dob, dsb, vnb, dvnb = _vbf(do), _vbf(ds_next), _vbf(v_new), _vbf(dv_new)
    dp = _vmask(cs["incl"], _vdot_nt(dob, vnb))
    dq_dec = _vdot_nt(dob, sb)
    du = -_vdot_nt(dvnb, sb)
    ddec = _vsum(_vsum(s * ds_next, 1), 0)
    dk_dec = _vdot_nt(vnb, dsb)
    dwu = _vcat(dv_new, du, 1)
    dt = _dot3(dwu, _vcat(vb, kbeta, 1), _vdot_nt)
    dvk = _dot3(tm, dwu, _vdot_tn)
    dvb, dkbeta = dvk[:, 0:DH], dvk[:, DH:2 * DH]
    da = _vmask(cs["strict"], -_dot3(tm, _dot3(dt, tm, _vdot_nt), _vdot_tn))
    dkk = _vbf(da * b64 * decay)
    dqk = _vbf(dp * decay)
    ddd = (da * b64 * kk + dp * qk) * decay
    dq = _vdot(dqk, kb) + dq_dec * e_g
    dk = _vdot_tn(dqk, qb) + _vdot(dkk, kb) + _vdot_tn(dkk, kb) + dk_dec * ekg + dkbeta * (b128 * e_g)
    dv = dvb * b128
    dbeta = _vsum(da * kk * decay, 1) + _vsum(dvb * v, 1) + _vsum(dkbeta * k * e_g, 1)
    s_k = _vsum(dk_dec * k_dec, 1)
    dg_col = _vsum(ddd, 1) + _vsum(dq_dec * q_dec, 1) - s_k + _vsum(dkbeta * kbeta, 1)
    colsum = _dot_exact_r(ddd, jnp.ones((CHUNK, LANES), F32), _vdot_tn)
    dg_last = _vsum(s_k, 0) + ddec * dec
    dg = _dot_exact_l(cs["triuf"], dg_col - colsum) + dg_last
    return dq, dk, dv, dg, dbeta


def _stack_rows(vecs, nrows):
    row = lax.broadcasted_iota(jnp.int32, (nrows, LANES), 0)
    out = jnp.zeros((nrows, LANES), F32)
    for i, v in enumerate(vecs):
        out = out + jnp.where(row == i, jnp.broadcast_to(v, (nrows, LANES)), 0.0)
    return out


def _head_lane(x, lane_idx):
    lane = lax.broadcasted_iota(jnp.int32, x.shape, 1)
    return jnp.sum(jnp.where(lane == lane_idx, x, 0.0), axis=1, keepdims=True)


def _gdn_gates(ps, h, alog_ref, dtb_ref):
    ga = _head_lane(ps, LANE_GA + h)
    gb = _head_lane(ps, LANE_GB + h)
    a = jnp.exp(jnp.full((1, 1), alog_ref[0, h], F32))
    sp_in = ga + dtb_ref[0, h]
    g = -a * _softplus(sp_in)
    return g, _sigmoid(gb), a, sp_in


def _gdn_specs(b_loc, t):
    def col(off):
        return pl.BlockSpec((t, DH), lambda b, h: (b, off + h))

    ps_spec = pl.BlockSpec((t, LANES), lambda b, h: (b, 0))

    def wcol(off):
        return pl.BlockSpec((CONV_K, DH), lambda b, h: (0, off + h))

    smem = pl.BlockSpec(memory_space=pltpu.SMEM)
    vec = pl.BlockSpec((1, DH), lambda b, h: (0, 0))
    return col, ps_spec, wcol, smem, vec


def _gdn_fwd(pg, ps, convw, a_log, dt_bias, gnorm, b_loc, t):
    n = b_loc * t
    nc = t // CHUNK
    col, ps_spec, wcol, smem, vec = _gdn_specs(b_loc, t)

    def body(q_ref, k_ref, v_ref, z_ref, ps_ref, wq_ref, wk_ref, wv_ref, alog_ref, dtb_ref, gn_ref,
             oa_ref, oraw_ref, s_ref, qn, kn, vv, g128, g64, b128, b64, uq_s, p_s, kd_s, dec_s, pad_s):
        h = pl.program_id(1)
        g, beta, _, _ = _gdn_gates(ps_ref[...], h, alog_ref, dtb_ref)
        g128[...] = jnp.broadcast_to(g, (t, LANES))
        g64[...] = jnp.broadcast_to(g, (t, CHUNK))
        b128[...] = jnp.broadcast_to(beta, (t, LANES))
        b64[...] = jnp.broadcast_to(beta, (t, CHUNK))
        _pad_zero(pad_s)
        pq = _conv(q_ref[...], wq_ref, pad_s)
        yq = pq * _sigmoid(pq)
        qn[...] = yq * (lax.rsqrt(jnp.sum(yq * yq, axis=1, keepdims=True) + EPS) * (DH ** -0.5))
        pk = _conv(k_ref[...], wk_ref, pad_s)
        yk = pk * _sigmoid(pk)
        kn[...] = yk * lax.rsqrt(jnp.sum(yk * yk, axis=1, keepdims=True) + EPS)
        pv = _conv(v_ref[...], wv_ref, pad_s)
        vv[...] = pv * _sigmoid(pv)
        cs = _chunk_consts()

        def pre_group(gi, _):
            idx = [gi * GDN_GROUP + c for c in range(GDN_GROUP)]
            rows = [pl.ds(pl.multiple_of(i * CHUNK, CHUNK), CHUNK) for i in idx]
            ins = [_V([ref[r, :] for r in rows]) for ref in (qn, kn, vv, g128, g64, b128, b64)]
            f = _gdn_chunk_pre(*ins, cs)
            for c, (i, r) in enumerate(zip(idx, rows)):
                vv[r, :] = f["w"].xs[c]
                uq_s[i, 0:CHUNK, :] = _bf(f["u"].xs[c])
                uq_s[i, CHUNK:2 * CHUNK, :] = _bf(f["q_dec"].xs[c])
                p_s[r, :] = _bf(f["p"].xs[c])
                kd_s[r, :] = _bf(f["k_dec"].xs[c])
                dec_s[pl.ds(pl.multiple_of(i * 8, 8), 8), :] = jnp.broadcast_to(f["dec"].xs[c], (8, LANES))
            return 0

        lax.fori_loop(0, nc // GDN_GROUP, pre_group, 0)

        def chunk(i, s):
            r = pl.ds(pl.multiple_of(i * CHUNK, CHUNK), CHUNK)
            us = _dot(uq_s[i], _bf(s))
            vnb = _bf(vv[r, :] - us[0:CHUNK])
            oraw_ref[r, :] = us[CHUNK:2 * CHUNK] + _dot(p_s[r, :], vnb)
            s_ref[0, 0, i] = s
            return s * dec_s[pl.ds(pl.multiple_of(i * 8, 8), 1), :] + _dot_tn(kd_s[r, :], vnb)

        lax.fori_loop(0, nc, chunk, jnp.zeros((DH, DH), F32))
        o = oraw_ref[...]
        rr = lax.rsqrt(jnp.mean(o * o, axis=1, keepdims=True) + EPS)
        z = z_ref[...]
        oa_ref[...] = _bf((o * rr * gn_ref[...]) * (z * _sigmoid(z)))

    return pl.pallas_call(
        body, name="gdn_fwd", grid=(b_loc, HEADS),
        in_specs=[col(0), col(HEADS), col(2 * HEADS), col(3 * HEADS), ps_spec, wcol(0), wcol(HEADS), wcol(2 * HEADS),
                  smem, smem, vec],
        out_specs=[pl.BlockSpec((t, DH), lambda b, h: (b, h)), pl.BlockSpec((t, DH), lambda b, h: (b, h)),
                   pl.BlockSpec((1, 1, nc, DH, DH), lambda b, h: (b, h, 0, 0, 0))],
        out_shape=[jax.ShapeDtypeStruct((n, HEADS * DH), BF16), jax.ShapeDtypeStruct((n, HEADS * DH), F32),
                   jax.ShapeDtypeStruct((b_loc, HEADS, nc, DH, DH), F32)],
        scratch_shapes=([pltpu.VMEM((t, DH), F32)] * 3 + [pltpu.VMEM((t, LANES), F32), pltpu.VMEM((t, CHUNK), F32)] * 2
                        + [pltpu.VMEM((nc, 2 * CHUNK, DH), BF16), pltpu.VMEM((t, CHUNK), BF16), pltpu.VMEM((t, DH), BF16),
                           pltpu.VMEM((8 * nc, LANES), F32), pltpu.VMEM((t + 2 * PAD, LANES), F32)]),
        compiler_params=_cparams(("arbitrary", "arbitrary")),
    )(pg, pg, pg, pg, ps, convw, convw, convw, a_log, dt_bias, gnorm)


def _gdn_bwd(pg, ps, convw, a_log, dt_bias, gnorm, d_oa, o_raw, s_all, b_loc, t):
    n = b_loc * t
    nc = t // CHUNK
    col, ps_spec, wcol, smem, vec = _gdn_specs(b_loc, t)

    def body(q_ref, k_ref, v_ref, z_ref, ps_ref, wq_ref, wk_ref, wv_ref, alog_ref, dtb_ref, gn_ref,
             doa_ref, oraw_ref, s_ref,
             dq_ref, dk_ref, dv_ref, dz_ref, dps_ref, dcw_ref, dsm_ref,
             qn, kn, vv, g128, g64, b128, b64, do_s, bg_s, u_s, vn_s, dvn_s, dcy_s, kk_s, qk_s, tm_s, dsn_s, pad_s):
        b, h = pl.program_id(0), pl.program_id(1)
        g, beta, _, _ = _gdn_gates(ps_ref[...], h, alog_ref, dtb_ref)
        g128[...] = jnp.broadcast_to(g, (t, LANES))
        g64[...] = jnp.broadcast_to(g, (t, CHUNK))
        b128[...] = jnp.broadcast_to(beta, (t, LANES))
        b64[...] = jnp.broadcast_to(beta, (t, CHUNK))
        _pad_zero(pad_s)

        def prep(x_ref, w_ref):
            p = _conv(x_ref[...], w_ref, pad_s)
            sg = _sigmoid(p)
            return p, sg, p * sg

        _, _, yq = prep(q_ref, wq_ref)
        qn[...] = yq * (lax.rsqrt(jnp.sum(yq * yq, axis=1, keepdims=True) + EPS) * (DH ** -0.5))
        _, _, yk = prep(k_ref, wk_ref)
        kn[...] = yk * lax.rsqrt(jnp.sum(yk * yk, axis=1, keepdims=True) + EPS)
        _, _, yv = prep(v_ref, wv_ref)
        vv[...] = yv

        o = oraw_ref[...]
        z = z_ref[...]
        doa = doa_ref[...]
        gn = gn_ref[...]
        ro = lax.rsqrt(jnp.mean(o * o, axis=1, keepdims=True) + EPS)
        sz = _sigmoid(z)
        dz_ref[...] = _bf(doa * (o * ro * gn) * (sz * (1.0 + z * (1.0 - sz))))
        dn = doa * (z * sz)
        dgn = jnp.sum(dn * o * ro, axis=0, keepdims=True)
        gy = dn * gn
        do_s[...] = ro * gy - o * (ro * ro * ro * (1.0 / DH)) * jnp.sum(gy * o, axis=1, keepdims=True)

        cs = _chunk_consts()

        def pre_group(gi, _):
            idx = [gi * GDN_GROUP + c for c in range(GDN_GROUP)]
            rows = [pl.ds(pl.multiple_of(i * CHUNK, CHUNK), CHUNK) for i in idx]
            ins = [_V([ref[r, :] for r in rows]) for ref in (qn, kn, vv, g128, g64, b128, b64)]
            states = _V([_bf(s_ref[0, 0, i]) for i in idx])
            f = _gdn_chunk_pre(*ins, cs)
            v_new = f["w"] - _vdot(_vbf(f["u"]), states)
            for c, r in enumerate(rows):
                bg_s[r, :] = f["big_g"].xs[c]
                u_s[r, :] = f["u"].xs[c]
                vn_s[r, :] = v_new.xs[c]
                dcy_s[r, :] = f["decay"].xs[c]
                kk_s[r, :] = f["kk"].xs[c]
                qk_s[r, :] = f["qk"].xs[c]
                tm_s[r, :] = f["tm"].xs[c]
            return 0

        lax.fori_loop(0, nc // GDN_GROUP, pre_group, 0)

        def chunk(j, ds):
            i = nc - 1 - j
            r = pl.ds(pl.multiple_of(i * CHUNK, CHUNK), CHUNK)
            big_g = bg_s[r, :]
            g_last = jnp.sum(g128[r, :], axis=0, keepdims=True)
            dob = _bf(do_s[r, :])
            dv_new = (_dot_tn(_bf(qk_s[r, :] * dcy_s[r, :]), dob)
                      + _dot(_bf(kn[r, :] * jnp.exp(g_last - big_g)), _bf(ds)))
            dvn_s[r, :] = dv_new
            dsn_s[i] = ds
            return (_dot_tn(_bf(qn[r, :] * jnp.exp(big_g)), dob) + jnp.exp(g_last) * ds
                    - _dot_tn(_bf(u_s[r, :]), _bf(dv_new)))

        lax.fori_loop(0, nc, chunk, jnp.zeros((DH, DH), F32))

        def post_group(gi, _):
            idx = [gi * GDN_GROUP + c for c in range(GDN_GROUP)]
            rows = [pl.ds(pl.multiple_of(i * CHUNK, CHUNK), CHUNK) for i in idx]
            def rows_of(ref):
                return _V([ref[r, :] for r in rows])

            dq, dk, dv, dg, dbeta = _gdn_chunk_post(
                rows_of(qn), rows_of(kn), rows_of(vv), rows_of(g128), rows_of(b128), rows_of(b64),
                _V([s_ref[0, 0, i] for i in idx]), _V([dsn_s[i] for i in idx]), rows_of(do_s), rows_of(dvn_s),
                rows_of(bg_s), rows_of(dcy_s), rows_of(kk_s), rows_of(qk_s), rows_of(tm_s), rows_of(u_s), rows_of(vn_s),
                cs)
            for c, r in enumerate(rows):
                qn[r, :] = dq.xs[c]
                kn[r, :] = dk.xs[c]
                vv[r, :] = dv.xs[c]
                g128[r, :] = dg.xs[c]
                b128[r, :] = jnp.broadcast_to(dbeta.xs[c], (CHUNK, LANES))
            return 0

        lax.fori_loop(0, nc // GDN_GROUP, post_group, 0)
        dqh, dkh, dvh = qn, kn, vv

        g, beta, a, sp_in = _gdn_gates(ps_ref[...], h, alog_ref, dtb_ref)
        dg = g128[...]
        d_ga = dg * (-a) * _sigmoid(sp_in)
        d_alog = jnp.sum(dg * g, axis=0, keepdims=True)
        d_dtb = jnp.sum(d_ga, axis=0, keepdims=True)
        d_gb = b128[...] * (beta * (1.0 - beta))
        lane = lax.broadcasted_iota(jnp.int32, (t, LANES), 1)
        contrib = jnp.where(lane == LANE_GA + h, d_ga, 0.0) + jnp.where(lane == LANE_GB + h, d_gb, 0.0)

        @pl.when(h == 0)
        def _():
            dps_ref[...] = jnp.zeros_like(dps_ref)

        dps_ref[...] += contrib

        lane1 = lax.broadcasted_iota(jnp.int32, (1, LANES), 1)
        small = _stack_rows([jnp.where(lane1 == h, d_alog, 0.0), jnp.where(lane1 == h, d_dtb, 0.0), dgn], 8)

        @pl.when((b == 0) & (h == 0))
        def _():
            dsm_ref[...] = jnp.zeros_like(dsm_ref)
            dcw_ref[...] = jnp.zeros_like(dcw_ref)

        dsm_ref[...] += small

        def conv_bwd(dp, x, w_ref, slot):
            dw = _stack_rows([jnp.sum(dp * _shifted(pad_s, 3), axis=0, keepdims=True),
                              jnp.sum(dp * _shifted(pad_s, 2), axis=0, keepdims=True),
                              jnp.sum(dp * _shifted(pad_s, 1), axis=0, keepdims=True),
                              jnp.sum(dp * x, axis=0, keepdims=True)], CONV_K)
            dcw_ref[slot] += dw
            pad_s[PAD:PAD + t, :] = dp
            dx = _shifted(pad_s, -3) * w_ref[0:1, :]
            dx = dx + _shifted(pad_s, -2) * w_ref[1:2, :]
            dx = dx + _shifted(pad_s, -1) * w_ref[2:3, :]
            return dx + dp * w_ref[3:4, :]

        def l2_bwd(dqn, y, c):
            r = lax.rsqrt(jnp.sum(y * y, axis=1, keepdims=True) + EPS)
            s1 = jnp.sum(dqn * y, axis=1, keepdims=True)
            return c * r * dqn - (c * r * r * r) * s1 * y

        def silu_bwd(p, sg):
            return sg * (1.0 + p * (1.0 - sg))

        pq, sq, yq = prep(q_ref, wq_ref)
        dq_ref[...] = _bf(conv_bwd(l2_bwd(dqh[...], yq, DH ** -0.5) * silu_bwd(pq, sq), q_ref[...], wq_ref, h))
        pk, sk, yk = prep(k_ref, wk_ref)
        dk_ref[...] = _bf(conv_bwd(l2_bwd(dkh[...], yk, 1.0) * silu_bwd(pk, sk), k_ref[...], wk_ref, HEADS + h))
        pv, sv, _ = prep(v_ref, wv_ref)
        dv_ref[...] = _bf(conv_bwd(dvh[...] * silu_bwd(pv, sv), v_ref[...], wv_ref, 2 * HEADS + h))

    blk = pl.BlockSpec((t, DH), lambda b, h: (b, h))
    ob = jax.ShapeDtypeStruct((n, HEADS * DH), BF16)
    return pl.pallas_call(
        body, name="gdn_bwd", grid=(b_loc, HEADS),
        in_specs=[col(0), col(HEADS), col(2 * HEADS), col(3 * HEADS), ps_spec, wcol(0), wcol(HEADS), wcol(2 * HEADS),
                  smem, smem, vec, blk, blk, pl.BlockSpec((1, 1, nc, DH, DH), lambda b, h: (b, h, 0, 0, 0))],
        out_specs=[blk, blk, blk, blk, ps_spec,
                   pl.BlockSpec((3 * HEADS, CONV_K, DH), lambda b, h: (0, 0, 0)),
                   pl.BlockSpec((8, LANES), lambda b, h: (0, 0))],
        out_shape=[ob, ob, ob, ob, jax.ShapeDtypeStruct((n, LANES), F32),
                   jax.ShapeDtypeStruct((3 * HEADS, CONV_K, DH), F32), jax.ShapeDtypeStruct((8, LANES), F32)],
        scratch_shapes=([pltpu.VMEM((t, DH), F32)] * 3 + [pltpu.VMEM((t, LANES), F32), pltpu.VMEM((t, CHUNK), F32)] * 2
                        + [pltpu.VMEM((t, DH), F32)] * 5 + [pltpu.VMEM((t, CHUNK), F32)] * 4
                        + [pltpu.VMEM((nc, DH, DH), F32), pltpu.VMEM((t + 2 * PAD, LANES), F32)]),
        compiler_params=_cparams(("arbitrary", "arbitrary")),
    )(pg, pg, pg, pg, ps, convw, convw, convw, a_log, dt_bias, gnorm, d_oa, o_raw, s_all)


def _fox_prologue(q_ref, k_ref, v_ref, ps_ref, fb_ref, gq_ref, gk_ref, h, t, qs, ks, vs, ccol, crow):
    nb = t // FOX_BLOCK
    q, k = q_ref[...], k_ref[...]
    rq = lax.rsqrt(jnp.mean(q * q, axis=1, keepdims=True) + EPS)
    rk = lax.rsqrt(jnp.mean(k * k, axis=1, keepdims=True) + EPS)
    qs[...] = _bf(q * rq * gq_ref[...])
    ks[...] = _bf(k * rk * gk_ref[...])
    vs[...] = _bf(v_ref[...])
    f_in = _head_lane(ps_ref[...], LANE_FF + h) + fb_ref[0, h]
    ccol[...] = jnp.broadcast_to(-_softplus(-f_in), (t, LANES))
    r = lax.broadcasted_iota(jnp.int32, (FOX_BLOCK, FOX_BLOCK), 0)
    c = lax.broadcasted_iota(jnp.int32, (FOX_BLOCK, FOX_BLOCK), 1)
    trilf, triuf = (r >= c).astype(F32), (r <= c).astype(F32)
    blocks = [pl.ds(j * FOX_BLOCK, FOX_BLOCK) for j in range(nb)]
    lfs = _V([ccol[rb, :] for rb in blocks])
    cc = _dot_exact_l(trilf, lfs, terms=3)
    cr = _dot_exact_r(lfs, triuf, _vdot_tn, terms=3)
    sums = _vsum(lfs, 0)
    carry = jnp.zeros((1, LANES), F32)
    for j, rb in enumerate(blocks):
        ccol[rb, :] = cc.xs[j] + carry
        crow[j] = (cr.xs[j] + carry)[0:8]
        carry = carry + sums.xs[j]
    return rq, rk, f_in


def _fox_scores(q_rows, k_rows, cc, cr, row0, col0):
    s = _dot_nt(q_rows, k_rows) * (DH ** -0.5) + cc - cr
    r = lax.broadcasted_iota(jnp.int32, s.shape, 0)
    c = lax.broadcasted_iota(jnp.int32, s.shape, 1)
    return jnp.where(row0 + r >= col0 + c, s, NEG)


def _fox_specs(t):
    def col(off):
        return pl.BlockSpec((t, DH), lambda b, h: (b, off + h))

    ps_spec = pl.BlockSpec((t, LANES), lambda b, h: (b, 0))
    smem = pl.BlockSpec(memory_space=pltpu.SMEM)
    vec = pl.BlockSpec((1, DH), lambda b, h: (0, 0))
    blk = pl.BlockSpec((t, DH), lambda b, h: (b, h))
    return col, ps_spec, smem, vec, blk


def _fox_fwd(pf, ps, f_bias, gq, gk, b_loc, t):
    n = b_loc * t
    nb = t // FOX_BLOCK
    kt = min(FOX_TILE, t)
    nsub = kt // FOX_BLOCK
    col, ps_spec, smem, vec, blk = _fox_specs(t)

    def body(q_ref, k_ref, v_ref, ps_ref, fb_ref, gq_ref, gk_ref, o_ref, lse_ref, qs, ks, vs, ccol, crow):
        h = pl.program_id(1)
        _fox_prologue(q_ref, k_ref, v_ref, ps_ref, fb_ref, gq_ref, gk_ref, h, t, qs, ks, vs, ccol, crow)

        def qblock(i, _):
            ri = pl.ds(pl.multiple_of(i * FOX_SHORT, FOX_SHORT), FOX_SHORT)
            qi = qs[ri, :]
            cc = jnp.concatenate([ccol[ri, :]] * nsub, axis=1)

            def ktile(j, carry):
                m, l, acc = carry
                rj = pl.ds(pl.multiple_of(j * kt, kt), kt)
                cr = jnp.concatenate([crow[j * nsub + u, 0:1, :] for u in range(nsub)], axis=1)
                s = _fox_scores(qi, ks[rj, :], cc, cr, i * FOX_SHORT, j * kt)
                m_new = jnp.maximum(m, jnp.max(s, axis=1, keepdims=True))
                p = jnp.exp(s - m_new)
                alpha = jnp.exp(m - m_new)
                l = alpha * l + jnp.sum(p, axis=1, keepdims=True)
                acc = alpha * acc + _dot(_bf(p), vs[rj, :])
                return m_new, l, acc

            m, l, acc = lax.fori_loop(0, (i * FOX_SHORT) // kt + 1, ktile, (jnp.full((FOX_SHORT, 1), NEG, F32),
                                                                            jnp.zeros((FOX_SHORT, 1), F32),
                                                                            jnp.zeros((FOX_SHORT, DH), F32)))
            o_ref[ri, :] = acc / l
            lse_ref[ri, :] = jnp.broadcast_to(m + jnp.log(l), (FOX_SHORT, LANES))
            return 0

        lax.fori_loop(0, t // FOX_SHORT, qblock, 0)

    o = jax.ShapeDtypeStruct((n, HEADS * DH), F32)
    return pl.pallas_call(
        body, name="fox_fwd", grid=(b_loc, HEADS),
        in_specs=[col(0), col(HEADS), col(2 * HEADS), ps_spec, smem, vec, vec],
        out_specs=[blk, blk], out_shape=[o, o],
        scratch_shapes=[pltpu.VMEM((t, DH), BF16)] * 3 + [pltpu.VMEM((t, LANES), F32), pltpu.VMEM((nb, 8, LANES), F32)],
        compiler_params=_cparams(("arbitrary", "arbitrary")),
    )(pf, pf, pf, ps, f_bias, gq, gk)


def _fox_bwd(pf, ps, f_bias, gq, gk, d_ob, ob, lse, dps_in, b_loc, t):
    n = b_loc * t
    nb = t // FOX_BLOCK
    qt = min(FOX_TILE, t)
    scale = DH ** -0.5
    col, ps_spec, smem, vec, blk = _fox_specs(t)

    def body(q_ref, k_ref, v_ref, ps_ref, fb_ref, gq_ref, gk_ref, do_ref, o_ref, lse_ref, dpsi_ref,
             dq_ref, dk_ref, dv_ref, dps_ref, dsm_ref, qs, ks, vs, ccol, crow, dos, dl, dqa, dcr, dcq):
        b, h = pl.program_id(0), pl.program_id(1)
        rq, _, f_in = _fox_prologue(q_ref, k_ref, v_ref, ps_ref, fb_ref, gq_ref, gk_ref, h, t, qs, ks, vs, ccol, crow)
        dov = do_ref[...]
        dos[...] = _bf(dov)
        dl[...] = jnp.broadcast_to(jnp.sum(dov * o_ref[...], axis=1, keepdims=True), (t, LANES))
        dqa[...] = jnp.zeros_like(dqa)
        dcq[...] = jnp.zeros_like(dcq)
        gkv = gk_ref[...]

        ksub = FOX_SHORT // FOX_BLOCK

        def kblock(j, dgk):
            rj = pl.ds(pl.multiple_of(j * FOX_SHORT, FOX_SHORT), FOX_SHORT)
            kj, vj = ks[rj, :], vs[rj, :]
            cr = jnp.concatenate([crow[j * ksub + u, 0:1, :] for u in range(ksub)], axis=1)

            def wide(x):
                return jnp.concatenate([x] * ksub, axis=1)

            def qtile(i, carry):
                dk_acc, dv_acc, dc = carry
                ri = pl.ds(pl.multiple_of(i * qt, qt), qt)
                qi, doi = qs[ri, :], dos[ri, :]
                s = _fox_scores(qi, kj, wide(ccol[ri, :]), cr, i * qt, j * FOX_SHORT)
                p = jnp.exp(s - wide(lse_ref[ri, :]))
                ds = p * (_dot_nt(doi, vj) - wide(dl[ri, :]))
                dsb = _bf(ds)
                dqa[ri, :] += _dot(dsb, kj)
                dcq[ri, :] += jnp.broadcast_to(jnp.sum(ds, axis=1, keepdims=True), (qt, LANES))
                return (dk_acc + _dot_tn(dsb, qi), dv_acc + _dot_tn(_bf(p), doi),
                        dc - jnp.sum(ds, axis=0, keepdims=True))

            z = jnp.zeros((FOX_SHORT, DH), F32)
            dk_acc, dv_acc, dc = lax.fori_loop((j * FOX_SHORT) // qt, t // qt, qtile,
                                               (z, z, jnp.zeros((1, FOX_SHORT), F32)))
            dv_ref[rj, :] = _bf(dv_acc)
            for u in range(ksub):
                dcr[pl.ds(pl.multiple_of((j * ksub + u) * 8, 8), 8), :] = jnp.broadcast_to(
                    dc[:, u * FOX_BLOCK:(u + 1) * FOX_BLOCK], (8, LANES))
            kraw = k_ref[rj, :]
            rk = lax.rsqrt(jnp.mean(kraw * kraw, axis=1, keepdims=True) + EPS)
            dkn = dk_acc * scale
            gy = dkn * gkv
            dk_ref[rj, :] = _bf(rk * gy - kraw * (rk * rk * rk * (1.0 / DH)) * jnp.sum(gy * kraw, axis=1, keepdims=True))
            return dgk + jnp.sum(dkn * kraw * rk, axis=0, keepdims=True)

        dgk = lax.fori_loop(0, t // FOX_SHORT, kblock, jnp.zeros((1, DH), F32))

        q = q_ref[...]
        dqn = dqa[...] * scale
        gy = dqn * gq_ref[...]
        dq_ref[...] = _bf(rq * gy - q * (rq * rq * rq * (1.0 / DH)) * jnp.sum(gy * q, axis=1, keepdims=True))
        dgq = jnp.sum(dqn * q * rq, axis=0, keepdims=True)

        r = lax.broadcasted_iota(jnp.int32, (FOX_BLOCK, FOX_BLOCK), 0)
        c = lax.broadcasted_iota(jnp.int32, (FOX_BLOCK, FOX_BLOCK), 1)
        triuf = (r <= c).astype(F32)

        def rev(jj, carry):
            j = nb - 1 - jj
            rows = pl.ds(pl.multiple_of(j * FOX_BLOCK, FOX_BLOCK), FOX_BLOCK)
            rowv = dcr[pl.ds(pl.multiple_of(j * 8, 8), 1), :]
            colv = jnp.sum(jnp.where(c >= r, jnp.broadcast_to(rowv, (FOX_BLOCK, LANES)), 0.0), axis=1, keepdims=True)
            qcol = dcq[rows, :]
            dl[rows, :] = colv + _dot_exact_l(triuf, qcol, terms=3) + carry
            return carry + jnp.sum(rowv, axis=1, keepdims=True) + jnp.sum(qcol, axis=0, keepdims=True)

        lax.fori_loop(0, nb, rev, jnp.zeros((1, LANES), F32))
        d_ff = dl[...] * _sigmoid(-f_in)
        lane = lax.broadcasted_iota(jnp.int32, (t, LANES), 1)

        @pl.when(h == 0)
        def _():
            dps_ref[...] = dpsi_ref[...]

        dps_ref[...] += jnp.where(lane == LANE_FF + h, d_ff, 0.0)

        lane1 = lax.broadcasted_iota(jnp.int32, (1, LANES), 1)
        d_fb = jnp.sum(d_ff, axis=0, keepdims=True)
        small = _stack_rows([dgq, dgk, jnp.where(lane1 == h, d_fb, 0.0)], 8)

        @pl.when((b == 0) & (h == 0))
        def _():
            dsm_ref[...] = jnp.zeros_like(dsm_ref)

        dsm_ref[...] += small

    ob_ = jax.ShapeDtypeStruct((n, HEADS * DH), BF16)
    return pl.pallas_call(
        body, name="fox_bwd", grid=(b_loc, HEADS),
        in_specs=[col(0), col(HEADS), col(2 * HEADS), ps_spec, smem, vec, vec, blk, blk, blk, ps_spec],
        out_specs=[blk, blk, blk, ps_spec, pl.BlockSpec((8, LANES), lambda b, h: (0, 0))],
        out_shape=[ob_, ob_, ob_, jax.ShapeDtypeStruct((n, LANES), F32), jax.ShapeDtypeStruct((8, LANES), F32)],
        scratch_shapes=([pltpu.VMEM((t, DH), BF16)] * 3 + [pltpu.VMEM((t, LANES), F32), pltpu.VMEM((nb, 8, LANES), F32)]
                        + [pltpu.VMEM((t, DH), BF16), pltpu.VMEM((t, LANES), F32), pltpu.VMEM((t, DH), F32),
                           pltpu.VMEM((8 * nb, LANES), F32), pltpu.VMEM((t, LANES), F32)]),
        compiler_params=_cparams(("arbitrary", "arbitrary")),
    )(pf, pf, pf, ps, f_bias, gq, gk, d_ob, ob, lse, dps_in)


class _NoExchange:
    def late_weights(self, after):
        return {}

    def grads_ready(self, grads, tie):
        return tie


def _local_step(x, target, w, b_loc, t, comm=None):
    comm = comm or _NoExchange()
    w = dict(w)
    xf = x
    u = _rms_fwd(xf, w["norm_mix_g"], "rms_mix")
    pg = _mm(u, w["w_gdn"], name="proj_gdn")
    pf = _mm(u, w["w_fox"], name="proj_fox")
    pgate = _mm(u, w["w_gate"], name="proj_gate")
    ps = _mm(u, w["w_small"], name="proj_small")
    oa, o_raw, s_all = _gdn_fwd(pg, ps, w["conv_w"], w["a_log"], w["dt_bias"], w["gdn_norm_g"], b_loc, t)
    ob, lse = _fox_fwd(pf, ps, w["f_bias"], w["fox_q_norm_g"], w["fox_k_norm_g"], b_loc, t)
    w.update(comm.late_weights(ob))
    ya = _mm(oa, w["w_proj_gdn"], name="proj_a")
    yb = _mm(ob, w["w_proj_fox"], name="proj_b")
    merged = _merge_fwd(ya, yb, pgate)
    h = _mm(merged, w["w_out"], name="proj_out", epi=lambda acc, xr: acc + xr, extras=(xf,))
    hn = _rms_fwd(h, w["norm_mlp_g"], "rms_mlp")
    up, act = _mm(hn, w["w_up"], name="mlp_up", out_dtype=BF16, out2=(_relu2, BF16))
    out = _mm(act, w["w_down"], name="mlp_down", epi=lambda acc, hr: acc + hr, extras=(h,))
    d_out, d_out16, loss_blk = _loss_bwd(out, target)

    g = {}
    g["w_down"] = _mm(act, d_out16, name="dw_down", ta=True, out_dtype=BF16)
    d_up = _mm(d_out16, w["w_down"], name="d_up", tb=True, out_dtype=BF16,
               epi=lambda acc, upr: acc * (2.0 * jnp.maximum(upr.astype(F32), 0.0)), extras=(up,))
    g["w_up"] = _mm(hn, d_up, name="dw_up", ta=True, out_dtype=BF16)
    mlp_gain = comm.grads_ready({"w_down": g["w_down"], "w_up": g["w_up"]}, w["norm_mlp_g"])
    d_hn = _mm(d_up, w["w_up"], name="d_hn", tb=True)
    dh, dh16, g["norm_mlp_g"] = _rms_bwd(d_hn, h, mlp_gain, d_out, "rms_mlp_bwd")
    g["w_out"] = _mm(merged, dh16, name="dw_out", ta=True, out_dtype=BF16)
    dm = _mm(dh16, w["w_out"], name="d_merged", tb=True)
    dya, dyb, dgate_a, dgate_b = _merge_bwd(dm, ya, yb, pgate)
    g["w_proj_gdn"] = _mm(oa, dya, name="dw_proj_a", ta=True, out_dtype=BF16)
    g["w_proj_fox"] = _mm(ob, dyb, name="dw_proj_b", ta=True, out_dtype=BF16)
    gdn_gain = comm.grads_ready({"w_out": g["w_out"], "w_proj_gdn": g["w_proj_gdn"], "w_proj_fox": g["w_proj_fox"]},
                                w["gdn_norm_g"])
    d_oa = _mm(dya, w["w_proj_gdn"], name="d_oa", tb=True)
    d_ob = _mm(dyb, w["w_proj_fox"], name="d_ob", tb=True)
    dgq, dgk, dgv, dgz, dps, dcw, gdn_small = _gdn_bwd(pg, ps, w["conv_w"], w["a_log"], w["dt_bias"], gdn_gain,
                                                       d_oa, o_raw, s_all, b_loc, t)
    dfq, dfk, dfv, dps, fox_small = _fox_bwd(pf, ps, w["f_bias"], w["fox_q_norm_g"], w["fox_k_norm_g"],
                                             d_ob, ob, lse, dps, b_loc, t)
    segs = [(dgq, "w_gdn", 0), (dgk, "w_gdn", 1024), (dgv, "w_gdn", 2048), (dgz, "w_gdn", 3072),
            (dfq, "w_fox", 0), (dfk, "w_fox", 1024), (dfv, "w_fox", 2048),
            (dgate_a, "w_gate", 0), (dgate_b, "w_gate", 1024)]
    dws = [_mm(u, dps, name="dw_small", ta=True, out_dtype=BF16)]
    dws += [_mm(u, dseg, name=f"dw_in_{idx}", ta=True, out_dtype=BF16) for idx, (dseg, _, _) in enumerate(segs)]
    g["w_in_parts"] = dws
    mix_gain = comm.grads_ready({"w_in_parts": dws}, w["norm_mix_g"])
    du = _mm(dps, w["w_small"], name="du_small", tb=True)
    for idx, (dseg, wname, off) in enumerate(segs):
        du = _mm(dseg, w[wname], name=f"du_{idx}", tb=True, b_koff=off,
                 epi=lambda acc, prev: acc + prev, extras=(du,))
    grad_x, _, g["norm_mix_g"] = _rms_bwd(du, xf, mix_gain, dh, "rms_mix_bwd")
    g["conv"] = dcw
    g["gdn_small"] = gdn_small
    g["fox_small"] = fox_small
    return loss_blk, grad_x, g


def _position():
    x, y, c = lax.axis_index("x"), lax.axis_index("y"), lax.axis_index("c")
    return x, y, c


def _to_bf16(arrs, name):
    n = len(arrs)

    def body(*refs):
        for i in range(n):
            refs[n + i][...] = _bf(refs[i][...])

    return pl.pallas_call(
        body, name=name,
        out_shape=[jax.ShapeDtypeStruct(a.shape, BF16) for a in arrs],
        compiler_params=_cparams(),
    )(*arrs)


def _all_gather(arrs, name):
    n = len(arrs)
    hbm = pl.BlockSpec(memory_space=pl.ANY)

    def body(*refs):
        ins, outs = refs[:n], refs[n:2 * n]
        send, recv, loc = refs[2 * n:]
        x, y, c = _position()
        me = 4 * x + 2 * y + c
        sibling = (x, y, 1 - c)
        chips = [(1 - x, y), (x, 1 - y), (1 - x, 1 - y)]

        def idx(px, py, pc):
            return 4 * px + 2 * py + pc

        def cp(a, k, block, to, src=None):
            return pltpu.make_async_remote_copy(
                src_ref=outs[a].at[block] if src is None else src, dst_ref=outs[a].at[block],
                send_sem=send.at[a, k], recv_sem=recv.at[a, k], device_id=to, device_id_type=MESH)

        mine = [pltpu.make_async_copy(ins[a], outs[a].at[me], loc.at[a]) for a in range(n)]
        for m in mine:
            m.start()
        first = []
        for a in range(n):
            first.append(cp(a, 0, me, sibling, src=ins[a]))
            first += [cp(a, 1 + j, me, (*chip, c), src=ins[a]) for j, chip in enumerate(chips)]
        for f in first:
            f.start()
        passed = []
        for j, chip in enumerate(chips):
            for a in range(n):
                cp(a, 1 + j, idx(*chip, c), (x, y, c)).wait_recv()
                p = cp(a, 4 + j, idx(*chip, c), sibling)
                p.start()
                passed.append(p)
        for a in range(n):
            cp(a, 0, idx(x, y, 1 - c), (x, y, c)).wait_recv()
            for j, chip in enumerate(chips):
                cp(a, 4 + j, idx(*chip, 1 - c), (x, y, c)).wait_recv()
        for f in first + passed:
            f.wait_send()
        for m in mine:
            m.wait()

    return pl.pallas_call(
        body, name=name,
        in_specs=[hbm] * n, out_specs=[hbm] * n,
        out_shape=[jax.ShapeDtypeStruct((N_DEV,) + a.shape, a.dtype) for a in arrs],
        scratch_shapes=[pltpu.SemaphoreType.DMA((n, 7)), pltpu.SemaphoreType.DMA((n, 7)), pltpu.SemaphoreType.DMA((n,))],
        compiler_params=pltpu.CompilerParams(has_side_effects=True),
    )(*arrs)


def _peer(x, y, c, rel):
    return ((1 - x) if rel & 4 else x, (1 - y) if rel & 2 else y, (1 - c) if rel & 1 else c)


def _exchange(arrs, name):
    n = len(arrs)
    hbm = pl.BlockSpec(memory_space=pl.ANY)

    def body(*refs):
        ins, outs = refs[:n], refs[n:2 * n]
        send, recv, loc = refs[2 * n:]
        x, y, c = _position()
        me = 4 * x + 2 * y + c
        mine = [pltpu.make_async_copy(ins[a].at[me], outs[a].at[me], loc.at[a]) for a in range(n)]
        for m in mine:
            m.start()
        copies = []
        for rel in range(1, N_DEV):
            px, py, pc = _peer(x, y, c, rel)
            for a in range(n):
                copies.append(pltpu.make_async_remote_copy(
                    src_ref=ins[a].at[4 * px + 2 * py + pc], dst_ref=outs[a].at[me],
                    send_sem=send.at[a, rel - 1], recv_sem=recv.at[a, rel - 1],
                    device_id=(px, py, pc), device_id_type=MESH))
        for cpy in copies:
            cpy.start()
        for cpy in copies:
            cpy.wait()
        for m in mine:
            m.wait()

    return pl.pallas_call(
        body, name=name,
        in_specs=[hbm] * n, out_specs=[hbm] * n,
        out_shape=[jax.ShapeDtypeStruct(a.shape, a.dtype) for a in arrs],
        scratch_shapes=[pltpu.SemaphoreType.DMA((n, 7)), pltpu.SemaphoreType.DMA((n, 7)), pltpu.SemaphoreType.DMA((n,))],
        compiler_params=pltpu.CompilerParams(has_side_effects=True),
    )(*arrs)


HBM_SPEC = pl.BlockSpec(memory_space=pltpu.HBM)
SEM_SPEC = pl.BlockSpec(memory_space=pltpu.SEMAPHORE)
DATAFLOW = pltpu.SideEffectType.DATAFLOW_SIDE_EFFECTING


def _push_start(arrs, slots, name):
    n = len(arrs)
    land_shapes = [a.shape if slots else (N_DEV,) + a.shape for a in arrs]

    def body(*refs):
        ins, lands, sends, recvs, token = refs[:n], refs[n:2 * n], refs[2 * n:3 * n], refs[3 * n:4 * n], refs[-1]
        x, y, c = _position()
        me = 4 * x + 2 * y + c
        for rel in range(1, N_DEV):
            px, py, pc = _peer(x, y, c, rel)
            for a in range(n):
                pltpu.make_async_remote_copy(
                    src_ref=ins[a].at[4 * px + 2 * py + pc] if slots else ins[a], dst_ref=lands[a].at[me],
                    send_sem=sends[a], recv_sem=recvs[a], device_id=(px, py, pc), device_id_type=MESH).start()
        token[...] = jnp.zeros_like(token)

    sem = pltpu.SemaphoreType.DMA(())
    outs = pl.pallas_call(
        body, name=name,
        in_specs=[HBM_SPEC] * (2 * n),
        out_shape=(*[sem] * (2 * n), *[pltpu.HBM(a.shape, a.dtype) for a in arrs],
                   *[pltpu.HBM(s, a.dtype) for s, a in zip(land_shapes, arrs)], jax.ShapeDtypeStruct((8, LANES), F32)),
        out_specs=(*[SEM_SPEC] * (2 * n), *[HBM_SPEC] * (2 * n), pl.BlockSpec(memory_space=pltpu.VMEM)),
        input_output_aliases={i: 2 * n + i for i in range(2 * n)},
        compiler_params=pltpu.CompilerParams(has_side_effects=DATAFLOW),
    )(*[pltpu.with_memory_space_constraint(a, pltpu.HBM) for a in arrs],
      *[pltpu.with_memory_space_constraint(lax.empty(s, a.dtype), pltpu.HBM) for s, a in zip(land_shapes, arrs)])
    return dict(sends=list(outs[:n]), recvs=list(outs[n:2 * n]), ins=list(outs[2 * n:3 * n]),
                lands=list(outs[3 * n:4 * n]), token=outs[-1])


def _push_wait(started, after, name):
    n = len(started["ins"])

    def body(*refs):
        lands, sends, recvs = refs[n:2 * n], refs[2 * n:3 * n], refs[3 * n:4 * n]
        x, y, c = _position()
        for a in range(n):
            seven = lands[a].at[pl.ds(0, N_DEV - 1)]
            drain = pltpu.make_async_remote_copy(src_ref=seven, dst_ref=seven, send_sem=sends[a], recv_sem=recvs[a],
                                                 device_id=(x, y, c), device_id_type=MESH)
            drain.wait_send()
            drain.wait_recv()

    both = started["ins"] + started["lands"]
    outs = pl.pallas_call(
        body, name=name,
        in_specs=[HBM_SPEC] * (2 * n) + [SEM_SPEC] * (2 * n) + [pl.BlockSpec(memory_space=pl.ANY)],
        out_shape=tuple(pltpu.HBM(a.shape, a.dtype) for a in both), out_specs=tuple([HBM_SPEC] * (2 * n)),
        input_output_aliases={i: i for i in range(2 * n)},
        compiler_params=pltpu.CompilerParams(has_side_effects=DATAFLOW),
    )(*both, *started["sends"], *started["recvs"], after)
    return list(outs[:n]), list(outs[n:])


def _all_reduce_small(buf, name):
    rows = buf.shape[0]

    def body(in_ref, out_ref, slots, send, recv):
        x, y, c = _position()
        me = 4 * x + 2 * y + c
        slots[me] = in_ref[...]
        copies = []
        for rel in range(1, N_DEV):
            copies.append(pltpu.make_async_remote_copy(
                src_ref=in_ref, dst_ref=slots.at[me], send_sem=send.at[rel - 1], recv_sem=recv.at[rel - 1],
                device_id=_peer(x, y, c, rel), device_id_type=MESH))
        for cpy in copies:
            cpy.start()
        for cpy in copies:
            cpy.wait()
        tot = slots[0]
        for d in range(1, N_DEV):
            tot = tot + slots[d]
        out_ref[...] = tot

    return pl.pallas_call(
        body, name=name,
        out_shape=jax.ShapeDtypeStruct((rows, LANES), F32),
        in_specs=[pl.BlockSpec(memory_space=pltpu.VMEM)], out_specs=pl.BlockSpec(memory_space=pltpu.VMEM),
        scratch_shapes=[pltpu.VMEM((N_DEV, rows, LANES), F32), pltpu.SemaphoreType.DMA((7,)),
                        pltpu.SemaphoreType.DMA((7,))],
        compiler_params=pltpu.CompilerParams(has_side_effects=True),
    )(buf)


def _adam_math(g, w, m, v):
    m = ADAM_B1 * m + (1.0 - ADAM_B1) * g
    v = ADAM_B2 * v + (1.0 - ADAM_B2) * (g * g)
    m_hat = m / (1.0 - ADAM_B1 ** ADAM_STEP)
    v_hat = v / (1.0 - ADAM_B2 ** ADAM_STEP)
    delta = -ADAM_LR * (m_hat / (jnp.sqrt(v_hat) + ADAM_EPS) + ADAM_WD * w)
    return delta, m, v


def _adam_shard(me, parts, mine, w, m, v, name):
    r, c = w.shape
    tr = min(r, 128)

    def body(me_ref, p_ref, own_ref, w_ref, m_ref, v_ref, g_ref, d_ref, nm_ref, nv_ref):
        own = own_ref[0].astype(F32)
        g = None
        for s in range(N_DEV):
            term = jnp.where(me_ref[0] == s, own, p_ref[s].astype(F32))
            g = term if g is None else g + term
        d, nm, nv = _adam_math(g, w_ref[...], m_ref[...], v_ref[...])
        g_ref[...] = g
        d_ref[...] = d
        nm_ref[...] = nm
        nv_ref[...] = nv

    row = pl.BlockSpec((tr, c), lambda i, me_ref: (i, 0))
    o = jax.ShapeDtypeStruct((r, c), F32)
    return pl.pallas_call(
        body, name=name,
        grid_spec=pltpu.PrefetchScalarGridSpec(
            num_scalar_prefetch=1, grid=(r // tr,),
            in_specs=[pl.BlockSpec((N_DEV, tr, c), lambda i, me_ref: (0, i, 0)),
                      pl.BlockSpec((1, tr, c), lambda i, me_ref: (me_ref[0], i, 0)), row, row, row],
            out_specs=[row] * 4),
        out_shape=[o] * 4,
        compiler_params=_cparams(("parallel",)),
    )(me, parts, mine, w, m, v)


def _adam_small(g, w, m, v):
    def body(g_ref, w_ref, m_ref, v_ref, d_ref, nm_ref, nv_ref):
        d, nm, nv = _adam_math(g_ref[...], w_ref[...], m_ref[...], v_ref[...])
        d_ref[...] = d
        nm_ref[...] = nm
        nv_ref[...] = nv

    o = jax.ShapeDtypeStruct(g.shape, F32)
    return pl.pallas_call(body, name="adam_small", out_shape=[o] * 3, compiler_params=_cparams())(g, w, m, v)


def _split_w_in(w_full):
    o = IN_OFF
    w_gdn = w_full[:, o["gq"]:o["ga"]]
    w_fox = w_full[:, o["fq"]:o["ff"]]
    w_gate = w_full[:, o["gate_a"]:o["end"]]
    w_small = jnp.concatenate([w_full[:, o["ga"]:o["fq"]], w_full[:, o["ff"]:o["gate_a"]],
                               jnp.zeros((w_full.shape[0], LANES - 24), w_full.dtype)], axis=1)
    return w_gdn, w_fox, w_gate, w_small


def _join_w_in(parts):
    small = parts[0]
    return jnp.concatenate(parts[1:5] + [small[:, 0:16]] + parts[5:8] + [small[:, 16:24]] + parts[8:10], axis=1)


def _rows128(a, rows):
    flat = a.reshape(-1)
    flat = jnp.concatenate([flat, jnp.zeros((rows * LANES - flat.shape[0],), flat.dtype)])
    return flat.reshape(rows, LANES)


def kernel(x, norm_mix_g, w_in, gdn_conv_w, gdn_a_log, gdn_dt_bias, gdn_norm_g, fox_q_norm_g, fox_k_norm_g, fox_f_bias, w_proj_gdn, w_proj_fox, w_out, norm_mlp_g, w_up, w_down, loss_target, m_norm_mix_g, m_w_in, m_gdn_conv_w, m_gdn_a_log, m_gdn_dt_bias, m_gdn_norm_g, m_fox_q_norm_g, m_fox_k_norm_g, m_fox_f_bias, m_w_proj_gdn, m_w_proj_fox, m_w_out, m_norm_mlp_g, m_w_up, m_w_down, v_norm_mix_g, v_w_in, v_gdn_conv_w, v_gdn_a_log, v_gdn_dt_bias, v_gdn_norm_g, v_fox_q_norm_g, v_fox_k_norm_g, v_fox_f_bias, v_w_proj_gdn, v_w_proj_fox, v_w_out, v_norm_mlp_g, v_w_up, v_w_down):
    b_loc, t, d = x.shape
    n = b_loc * t
    me = 4 * lax.axis_index("x") + 2 * lax.axis_index("y") + lax.axis_index("c")

    late_names = ["w_proj_gdn", "w_proj_fox", "w_out", "w_up", "w_down"]
    big16 = _to_bf16([w_in[0], w_proj_gdn[0], w_proj_fox[0], w_out[0], w_up[0], w_down[0]], "weights_to_bf16")
    g_in, g_conv = _all_gather([big16[0], gdn_conv_w[0]], "gather_w_in")
    late = _push_start(list(big16[1:]), False, "gather_late_start")
    w_full = g_in.transpose(1, 0, 2).reshape(d, N_DEV * w_in.shape[2])
    w_gdn, w_fox, w_gate, w_small = _split_w_in(w_full)
    weights = {
        "w_gdn": w_gdn, "w_fox": w_fox, "w_gate": w_gate, "w_small": w_small,
        "conv_w": g_conv.transpose(1, 0, 2).reshape(CONV_K, 3 * d),
        "norm_mix_g": norm_mix_g + late["token"][0:1, 0:1], "norm_mlp_g": norm_mlp_g, "a_log": gdn_a_log,
        "dt_bias": gdn_dt_bias, "gdn_norm_g": gdn_norm_g, "fox_q_norm_g": fox_q_norm_g, "fox_k_norm_g": fox_k_norm_g,
        "f_bias": fox_f_bias,
    }
    c_in, c_up = w_in.shape[2], w_up.shape[2]

    class _Exchange:
        def __init__(self):
            self.started = []

        def late_weights(self, after):
            shards, lands = _push_wait(late, after, "gather_late_wait")
            full = [lax.dynamic_update_index_in_dim(land, shard, me, 0) for land, shard in zip(lands, shards)]
            g_pa, g_pb, g_out, g_up, g_down = full
            return {"w_proj_gdn": g_pa.reshape(d, d), "w_proj_fox": g_pb.reshape(d, d), "w_out": g_out.reshape(d, d),
                    "w_up": g_up.transpose(1, 0, 2).reshape(d, D_FF), "w_down": g_down.reshape(D_FF, d)}

        def grads_ready(self, grads, tie):
            layout = {
                "w_in_parts": lambda p: _join_w_in(p).reshape(d, N_DEV, c_in).transpose(1, 0, 2),
                "w_up": lambda a: a.reshape(d, N_DEV, c_up).transpose(1, 0, 2),
                "w_down": lambda a: a.reshape(N_DEV, D_FF // N_DEV, d),
            }
            names = list(grads)
            arrs = [layout.get(k, lambda a: a.reshape(N_DEV, d // N_DEV, d))(grads[k]) for k in names]
            st = _push_start(arrs, True, "grads_start_" + names[0])
            self.started.append((names, st))
            return tie + st["token"][0:1, 0:1]

    comm = _Exchange()
    loss_blk, grad_x, g = _local_step(x.reshape(n, d), loss_target.reshape(n, d), weights, b_loc, t, comm)

    me1 = jnp.reshape(me, (1,)).astype(jnp.int32)
    shards = {"w_in_parts": (w_in, m_w_in, v_w_in), "w_proj_gdn": (w_proj_gdn, m_w_proj_gdn, v_w_proj_gdn),
              "w_proj_fox": (w_proj_fox, m_w_proj_fox, v_w_proj_fox), "w_out": (w_out, m_w_out, v_w_out),
              "w_up": (w_up, m_w_up, v_w_up), "w_down": (w_down, m_w_down, v_w_down)}
    adam = {}

    def finish(names, st, after):
        mine, parts = _push_wait(st, after, "grads_wait_" + names[0])
        for k, own, part in zip(names, mine, parts):
            wi, mi, vi = shards[k]
            adam[k] = [r[None] for r in _adam_shard(me1, part, own, wi[0], mi[0], vi[0], "adam_" + k)]

    for names, st in comm.started[:-1]:
        finish(names, st, grad_x)

    conv_rows = CONV_K * 3 * d // LANES
    conv_g = g["conv"].transpose(1, 0, 2).reshape(conv_rows, LANES)
    buf = jnp.concatenate([conv_g, g["norm_mix_g"].reshape(8, LANES), g["norm_mlp_g"].reshape(8, LANES),
                           g["gdn_small"], g["fox_small"], loss_blk], axis=0)
    anchor = sum(adam[k][1][0, 0:1, 0:LANES] for names, _ in comm.started[:-1] for k in names) * 0.0
    tot = _all_reduce_small(buf + anchor, "all_reduce_small")
    finish(*comm.started[-1], tot)
    big_out = [adam[k] for k in ["w_in_parts"] + late_names]
    o = conv_rows
    conv_full = tot[0:o].reshape(CONV_K, 3 * d)
    c_conv = gdn_conv_w.shape[2]
    g_conv_shard = lax.dynamic_slice(conv_full, (0, me * c_conv), (CONV_K, c_conv))
    g_mix = tot[o:o + 8].reshape(1, d)
    g_mlp = tot[o + 8:o + 16].reshape(1, d)
    gs, fs = tot[o + 16:o + 24], tot[o + 24:o + 32]
    loss = tot[o + 32, 0]
    small_g = [g_mix, g_conv_shard[None], gs[0:1, 0:HEADS], gs[1:2, 0:HEADS], gs[2:3], fs[0:1], fs[1:2], fs[2:3, 0:HEADS],
               g_mlp]
    small_w = [norm_mix_g, gdn_conv_w, gdn_a_log, gdn_dt_bias, gdn_norm_g, fox_q_norm_g, fox_k_norm_g, fox_f_bias,
               norm_mlp_g]
    small_m = [m_norm_mix_g, m_gdn_conv_w, m_gdn_a_log, m_gdn_dt_bias, m_gdn_norm_g, m_fox_q_norm_g, m_fox_k_norm_g,
               m_fox_f_bias, m_norm_mlp_g]
    small_v = [v_norm_mix_g, v_gdn_conv_w, v_gdn_a_log, v_gdn_dt_bias, v_gdn_norm_g, v_fox_q_norm_g, v_fox_k_norm_g,
               v_fox_f_bias, v_norm_mlp_g]
    row_counts = [-(-a.size // (8 * LANES)) * 8 for a in small_w]

    def pack(arrs):
        return jnp.concatenate([_rows128(a, rc) for a, rc in zip(arrs, row_counts)], axis=0)

    sd, sm, sv = _adam_small(pack(small_g), pack(small_w), pack(small_m), pack(small_v))

    def unpack(p):
        outs, r0 = [], 0
        for a, rc in zip(small_w, row_counts):
            outs.append(p[r0:r0 + rc].reshape(-1)[:a.size].reshape(a.shape))
            r0 += rc
        return outs

    small_out = [small_g_i.reshape(w_i.shape) for small_g_i, w_i in zip(small_g, small_w)], unpack(sd), unpack(sm), unpack(sv)

    def ordered(kind):
        s = small_out[kind]
        bo = [b[kind] for b in big_out]
        return [s[0], bo[0], s[1], s[2], s[3], s[4], s[5], s[6], s[7], bo[1], bo[2], bo[3], s[8], bo[4], bo[5]]

    return (loss, grad_x.reshape(b_loc, t, d), *ordered(0), *ordered(1), *ordered(2), *ordered(3))
```

```python
import functools

import jax
import jax.numpy as jnp
from jax import lax
from jax.experimental import pallas as pl
from jax.experimental.pallas import tpu as pltpu

F32 = jnp.float32
BF16 = jnp.bfloat16
HI = lax.Precision.HIGHEST
MESH = pl.DeviceIdType.MESH

N_DEV = 8
D_MODEL = 1024
HEADS = 8
DH = 128
CONV_K = 4
CHUNK = 64
GDN_GROUP = 8
FOX_BLOCK = 128
FOX_TILE = 512
FOX_SHORT = 512
D_FF = 4 * D_MODEL
EPS = 1e-6
LANES = 128
NEG = -1e30
IN_OFF = {"gq": 0, "gk": 1024, "gv": 2048, "gz": 3072, "ga": 4096, "gb": 4104, "fq": 4112, "fk": 5136,
          "fv": 6160, "ff": 7184, "gate_a": 7192, "gate_b": 8216, "end": 9240}
LANE_GA, LANE_GB, LANE_FF = 0, 8, 16

ADAM_LR = 0.001
ADAM_B1 = 0.9
ADAM_B2 = 0.999
ADAM_EPS = 1e-08
ADAM_WD = 0.01
ADAM_STEP = 10

VMEM_LIMIT = 56 * 1024 * 1024


def _cparams(sem=None):
    return pltpu.CompilerParams(dimension_semantics=sem, vmem_limit_bytes=VMEM_LIMIT)


def _sigmoid(x):
    return 1.0 / (1.0 + jnp.exp(-x))


def _softplus(x):
    return jnp.maximum(x, 0.0) + jnp.log(1.0 + jnp.exp(-jnp.abs(x)))


def _dot(a, b, prec=None):
    return lax.dot_general(a, b, (((1,), (0,)), ((), ())), precision=prec, preferred_element_type=F32)


def _dot_nt(a, b, prec=None):
    return lax.dot_general(a, b, (((1,), (1,)), ((), ())), precision=prec, preferred_element_type=F32)


def _dot_tn(a, b, prec=None):
    return lax.dot_general(a, b, (((0,), (0,)), ((), ())), precision=prec, preferred_element_type=F32)


def _bf(x):
    return x.astype(BF16)


MM_TILE = 1024


def _mm(a, b, *, name, ta=False, tb=False, out_dtype=F32, epi=None, extras=(), out2=None,
        b_koff=0, tm=MM_TILE, tn=MM_TILE, tk=MM_TILE):
    m = a.shape[1] if ta else a.shape[0]
    kdim = a.shape[0] if ta else a.shape[1]
    n = b.shape[0] if tb else b.shape[1]
    tm, tn, tk = min(tm, m), min(tn, n), min(tk, kdim)
    nk = kdim // tk
    grid = (m // tm, n // tn, nk)
    koff = b_koff // tk
    a_spec = pl.BlockSpec((tk, tm), lambda i, j, k: (k, i)) if ta else pl.BlockSpec((tm, tk), lambda i, j, k: (i, k))
    if tb:
        b_spec = pl.BlockSpec((tn, tk), lambda i, j, k: (j, k + koff))
    else:
        b_spec = pl.BlockSpec((tk, tn), lambda i, j, k: (k + koff, j))
    o_spec = pl.BlockSpec((tm, tn), lambda i, j, k: (i, j))
    n_e = len(extras)
    n_o = 1 if out2 is None else 2
    dims = (((0 if ta else 1,), (1 if tb else 0,)), ((), ()))

    def body(a_ref, b_ref, *rest):
        e_refs, o_refs = rest[:n_e], rest[n_e:n_e + n_o]
        prod = lax.dot_general(_bf(a_ref[...]), _bf(b_ref[...]), dims, preferred_element_type=F32)

        def finish(r):
            if out2 is not None:
                o_refs[1][...] = out2[0](r).astype(out2[1])
            if epi is not None:
                r = epi(r, *[e[...] for e in e_refs])
            o_refs[0][...] = r.astype(out_dtype)

        if nk == 1:
            finish(prod)
        else:
            acc = rest[n_e + n_o]
            k = pl.program_id(2)

            @pl.when(k == 0)
            def _():
                acc[...] = prod

            @pl.when(k > 0)
            def _():
                acc[...] += prod

            @pl.when(k == nk - 1)
            def _():
                finish(acc[...])

    shapes = [jax.ShapeDtypeStruct((m, n), out_dtype)]
    if out2 is not None:
        shapes.append(jax.ShapeDtypeStruct((m, n), out2[1]))
    res = pl.pallas_call(
        body, name=name, grid=grid,
        in_specs=[a_spec, b_spec] + [o_spec] * n_e,
        out_specs=[o_spec] * n_o, out_shape=shapes,
        scratch_shapes=[] if nk == 1 else [pltpu.VMEM((tm, tn), F32)],
        compiler_params=_cparams(("parallel", "parallel", "arbitrary")),
    )(a, b, *extras)
    return res[0] if out2 is None else res


def _relu2(x):
    r = jnp.maximum(x, 0.0)
    return r * r


ROWS = 512


def _rms_fwd(x, g, name):
    n, d = x.shape

    def body(x_ref, g_ref, u_ref):
        xv = x_ref[...]
        r = lax.rsqrt(jnp.mean(xv * xv, axis=1, keepdims=True) + EPS)
        u_ref[...] = _bf(xv * r * g_ref[...])

    return pl.pallas_call(
        body, name=name, grid=(n // ROWS,),
        in_specs=[pl.BlockSpec((ROWS, d), lambda i: (i, 0)), pl.BlockSpec((1, d), lambda i: (0, 0))],
        out_specs=pl.BlockSpec((ROWS, d), lambda i: (i, 0)),
        out_shape=jax.ShapeDtypeStruct((n, d), BF16),
        compiler_params=_cparams(("parallel",)),
    )(x, g)


def _rms_bwd(dy, x, g, dres, name):
    n, d = x.shape

    def body(dy_ref, x_ref, g_ref, dres_ref, dx_ref, dx16_ref, dg_ref):
        i = pl.program_id(0)
        xv, dyv = x_ref[...], dy_ref[...]
        r = lax.rsqrt(jnp.mean(xv * xv, axis=1, keepdims=True) + EPS)
        gy = dyv * g_ref[...]
        s = jnp.sum(gy * xv, axis=1, keepdims=True)
        dx = dres_ref[...] + r * gy - xv * (r * r * r * (1.0 / d)) * s
        dx_ref[...] = dx
        dx16_ref[...] = _bf(dx)

        @pl.when(i == 0)
        def _():
            dg_ref[...] = jnp.zeros_like(dg_ref)

        dg_ref[...] += jnp.sum(dyv * xv * r, axis=0, keepdims=True)

    row = pl.BlockSpec((ROWS, d), lambda i: (i, 0))
    vec = pl.BlockSpec((1, d), lambda i: (0, 0))
    return pl.pallas_call(
        body, name=name, grid=(n // ROWS,),
        in_specs=[row, row, vec, row], out_specs=[row, row, vec],
        out_shape=[jax.ShapeDtypeStruct((n, d), F32), jax.ShapeDtypeStruct((n, d), BF16),
                   jax.ShapeDtypeStruct((1, d), F32)],
        compiler_params=_cparams(("arbitrary",)),
    )(dy, x, g, dres)


def _merge_fwd(ya, yb, gate):
    n, d = ya.shape

    def body(ya_ref, yb_ref, ga_ref, gb_ref, o_ref):
        o_ref[...] = _bf(_sigmoid(ga_ref[...]) * ya_ref[...] + _sigmoid(gb_ref[...]) * yb_ref[...])

    row = pl.BlockSpec((ROWS, d), lambda i: (i, 0))
    return pl.pallas_call(
        body, name="merge_fwd", grid=(n // ROWS,),
        in_specs=[row, row, row, pl.BlockSpec((ROWS, d), lambda i: (i, 1))], out_specs=row,
        out_shape=jax.ShapeDtypeStruct((n, d), BF16),
        compiler_params=_cparams(("parallel",)),
    )(ya, yb, gate, gate)


def _merge_bwd(dm, ya, yb, gate):
    n, d = ya.shape

    def body(dm_ref, ya_ref, yb_ref, ga_ref, gb_ref, dya_ref, dyb_ref, dga_ref, dgb_ref):
        dmv = dm_ref[...]
        sa, sb = _sigmoid(ga_ref[...]), _sigmoid(gb_ref[...])
        dya_ref[...] = _bf(dmv * sa)
        dyb_ref[...] = _bf(dmv * sb)
        dga_ref[...] = _bf(dmv * ya_ref[...] * sa * (1.0 - sa))
        dgb_ref[...] = _bf(dmv * yb_ref[...] * sb * (1.0 - sb))

    row = pl.BlockSpec((ROWS, d), lambda i: (i, 0))
    o = jax.ShapeDtypeStruct((n, d), BF16)
    return pl.pallas_call(
        body, name="merge_bwd", grid=(n // ROWS,),
        in_specs=[row, row, row, row, pl.BlockSpec((ROWS, d), lambda i: (i, 1))], out_specs=[row] * 4,
        out_shape=[o] * 4,
        compiler_params=_cparams(("parallel",)),
    )(dm, ya, yb, gate, gate)


def _loss_bwd(out, target):
    n, d = out.shape

    def body(o_ref, t_ref, d_ref, d16_ref, l_ref):
        i = pl.program_id(0)
        err = o_ref[...] - t_ref[...]
        d_ref[...] = err * (1.0 / d)
        d16_ref[...] = _bf(err * (1.0 / d))

        @pl.when(i == 0)
        def _():
            l_ref[...] = jnp.zeros_like(l_ref)

        l_ref[...] += 0.5 * jnp.sum(jnp.mean(err * err, axis=1, keepdims=True), axis=0, keepdims=True)

    row = pl.BlockSpec((ROWS, d), lambda i: (i, 0))
    return pl.pallas_call(
        body, name="loss_bwd", grid=(n // ROWS,),
        in_specs=[row, row], out_specs=[row, row, pl.BlockSpec((8, LANES), lambda i: (0, 0))],
        out_shape=[jax.ShapeDtypeStruct((n, d), F32), jax.ShapeDtypeStruct((n, d), BF16),
                   jax.ShapeDtypeStruct((8, LANES), F32)],
        compiler_params=_cparams(("arbitrary",)),
    )(out, target)


PAD = 8


def _pad_zero(pad_ref):
    t = pad_ref.shape[0] - 2 * PAD
    pad_ref[0:PAD, :] = jnp.zeros((PAD, LANES), F32)
    pad_ref[PAD + t:2 * PAD + t, :] = jnp.zeros((PAD, LANES), F32)


def _shifted(pad_ref, s):
    t = pad_ref.shape[0] - 2 * PAD
    return pad_ref[PAD - s:PAD - s + t, :]


def _conv(x, w_ref, pad_ref):
    t = x.shape[0]
    pad_ref[PAD:PAD + t, :] = x
    y = _shifted(pad_ref, 3) * w_ref[0:1, :]
    y = y + _shifted(pad_ref, 2) * w_ref[1:2, :]
    y = y + _shifted(pad_ref, 1) * w_ref[2:3, :]
    return y + x * w_ref[3:4, :]


def _chunk_consts():
    r = lax.broadcasted_iota(jnp.int32, (CHUNK, CHUNK), 0)
    c = lax.broadcasted_iota(jnp.int32, (CHUNK, CHUNK), 1)
    incl, strict = r >= c, r > c
    return dict(incl=incl, strict=strict, trilf=incl.astype(F32), triuf=(r <= c).astype(F32),
                eye=(r == c).astype(F32))


class _V:
    def __init__(self, xs):
        self.xs = list(xs)

    def __add__(self, o):
        return _ap(lambda x, y: x + y, self, o)

    def __radd__(self, o):
        return _ap(lambda x, y: y + x, self, o)

    def __sub__(self, o):
        return _ap(lambda x, y: x - y, self, o)

    def __rsub__(self, o):
        return _ap(lambda x, y: y - x, self, o)

    def __mul__(self, o):
        return _ap(lambda x, y: x * y, self, o)

    def __rmul__(self, o):
        return _ap(lambda x, y: y * x, self, o)

    def __neg__(self):
        return _ap(lambda x: -x, self)

    def __getitem__(self, idx):
        return _ap(lambda x: x[idx], self)


def _ap(fn, *args):
    n = [len(a.xs) for a in args if isinstance(a, _V)]
    if not n:
        return fn(*args)
    return _V([fn(*[a.xs[i] if isinstance(a, _V) else a for a in args]) for i in range(n[0])])


def _vbf(x):
    return _ap(_bf, x)


def _vdot(a, b):
    return _ap(_dot, a, b)


def _vdot_nt(a, b):
    return _ap(_dot_nt, a, b)


def _vdot_tn(a, b):
    return _ap(_dot_tn, a, b)


def _vexp(x):
    return _ap(jnp.exp, x)


def _vsum(x, axis):
    return _ap(lambda v: jnp.sum(v, axis=axis, keepdims=True), x)


def _vcat(a, b, axis):
    return _ap(lambda x, y: jnp.concatenate([x, y], axis=axis), a, b)


def _vmask(mask, x):
    return _ap(lambda v: jnp.where(mask, v, 0.0), x)


def _split2(x):
    h = _vbf(x)
    return h, _vbf(x - _ap(lambda v: v.astype(F32), h))


def _dot3(a, b, kind=_vdot):
    ah, al = _split2(a)
    bh, bl = _split2(b)
    return kind(ah, bh) + (kind(ah, bl) + kind(al, bh))


def _split(x, terms):
    out = []
    for _ in range(terms):
        h = _vbf(x)
        out.append(h)
        x = x - _ap(lambda v: v.astype(F32), h)
    return out


def _dot_exact_l(m01, x, kind=_vdot, terms=2):
    mb = _bf(m01)
    parts = [kind(mb, xp) for xp in _split(x, terms)]
    return functools.reduce(lambda a, b: a + b, reversed(parts))


def _dot_exact_r(x, m01, kind=_vdot, terms=2):
    mb = _bf(m01)
    parts = [kind(xp, mb) for xp in _split(x, terms)]
    return functools.reduce(lambda a, b: a + b, reversed(parts))


def _inv_unit_lower(a, eye):
    p = -a
    r = p + eye
    p = _dot3(p, p)
    for j in range(1, 6):
        if j < 5:
            y = _dot3(p, _vcat(p, r, 1))
            p, r = y[:, 0:CHUNK], r + y[:, CHUNK:2 * CHUNK]
        else:
            r = r + _dot3(p, r)
    return r


def _gdn_chunk_pre(q, k, v, g128, g64, b128, b64, cs):
    incl = cs["incl"]
    big_g = _dot_exact_l(cs["trilf"], g128)
    gc = big_g[:, 0:CHUNK]
    gr = _dot_exact_r(g64, cs["triuf"], _vdot_tn)
    decay = _ap(lambda d: jnp.where(incl, jnp.exp(jnp.where(incl, d, 0.0)), 0.0), gc - gr)
    kb, qb = _vbf(k), _vbf(q)
    qkk = _vdot_nt(_vcat(qb, kb, 0), kb)
    qk, kk = qkk[0:CHUNK], qkk[CHUNK:2 * CHUNK]
    tm = _inv_unit_lower(_vmask(cs["strict"], b64 * kk * decay), cs["eye"])
    e_g = _vexp(big_g)
    wu = _dot3(tm, _vcat(v * b128, k * (b128 * e_g), 1))
    w, u = wu[:, 0:DH], wu[:, DH:2 * DH]
    g_last = _vsum(g128, 0)
    return dict(big_g=big_g, decay=decay, kk=kk, qk=qk, tm=tm, w=w, u=u, p=qk * decay, q_dec=q * e_g,
                k_dec=k * _vexp(g_last - big_g), dec=_vexp(g_last))


def _gdn_chunk_post(q, k, v, g128, b128, b64, s, ds_next, do, dv_new, big_g, decay, kk, qk, tm, u, v_new, cs):
    e_g = _vexp(big_g)
    vb = v * b128
    kbeta = k * (b128 * e_g)
    q_dec = q * e_g
    g_last = _vsum(g128, 0)
    ekg = _vexp(g_last - big_g)
    k_dec = k * ekg
    dec = _vexp(g_last)
    kb, qb, sb = _vbf(k), _vbf(q), _vbf(s)
    dob, dsb, vnb, dvnb = _vbf(do), _vbf(ds_next), _vbf(v_new), _vbf(dv_new)
    dp = _vmask(cs["incl"], _vdot_nt(dob, vnb))
    dq_dec = _vdot_nt(dob, sb)
    du = -_vdot_nt(dvnb, sb)
    ddec = _vsum(_vsum(s * ds_next, 1), 0)
    dk_dec = _vdot_nt(vnb, dsb)
    dwu = _vcat(dv_new, du, 1)
    dt = _dot3(dwu, _vcat(vb, kbeta, 1), _vdot_nt)
    dvk = _dot3(tm, dwu, _vdot_tn)
    dvb, dkbeta = dvk[:, 0:DH], dvk[:, DH:2 * DH]
    da = _vmask(cs["strict"], -_dot3(tm, _dot3(dt, tm, _vdot_nt), _vdot_tn))
    dkk = _vbf(da * b64 * decay)
    dqk = _vbf(dp * decay)
    ddd = (da * b64 * kk + dp * qk) * decay
    dq = _vdot(dqk, kb) + dq_dec * e_g
    dk = _vdot_tn(dqk, qb) + _vdot(dkk, kb) + _vdot_tn(dkk, kb) + dk_dec * ekg + dkbeta * (b128 * e_g)
    dv = dvb * b128
    dbeta = _vsum(da * kk * decay, 1) + _vsum(dvb * v, 1) + _vsum(dkbeta * k * e_g, 1)
    s_k = _vsum(dk_dec * k_dec, 1)
    dg_col = _vsum(ddd, 1) + _vsum(dq_dec * q_dec, 1) - s_k + _vsum(dkbeta * kbeta, 1)
    colsum = _dot_exact_r(ddd, jnp.ones((CHUNK, LANES), F32), _vdot_tn)
    dg_last = _vsum(s_k, 0) + ddec * dec
    dg = _dot_exact_l(cs["triuf"], dg_col - colsum) + dg_last
    return dq, dk, dv, dg, dbeta


def _stack_rows(vecs, nrows):
    row = lax.broadcasted_iota(jnp.int32, (nrows, LANES), 0)
    out = jnp.zeros((nrows, LANES), F32)
    for i, v in enumerate(vecs):
        out = out + jnp.where(row == i, jnp.broadcast_to(v, (nrows, LANES)), 0.0)
    return out


def _head_lane(x, lane_idx):
    lane = lax.broadcasted_iota(jnp.int32, x.shape, 1)
    return jnp.sum(jnp.where(lane == lane_idx, x, 0.0), axis=1, keepdims=True)


def _gdn_gates(ps, h, alog_ref, dtb_ref):
    ga = _head_lane(ps, LANE_GA + h)
    gb = _head_lane(ps, LANE_GB + h)
    a = jnp.exp(jnp.full((1, 1), alog_ref[0, h], F32))
    sp_in = ga + dtb_ref[0, h]
    g = -a * _softplus(sp_in)
    return g, _sigmoid(gb), a, sp_in


def _gdn_specs(b_loc, t):
    def col(off):
        return pl.BlockSpec((t, DH), lambda b, h: (b, off + h))

    ps_spec = pl.BlockSpec((t, LANES), lambda b, h: (b, 0))

    def wcol(off):
        return pl.BlockSpec((CONV_K, DH), lambda b, h: (0, off + h))

    smem = pl.BlockSpec(memory_space=pltpu.SMEM)
    vec = pl.BlockSpec((1, DH), lambda b, h: (0, 0))
    return col, ps_spec, wcol, smem, vec


def _gdn_fwd(pg, ps, convw, a_log, dt_bias, gnorm, b_loc, t):
    n = b_loc * t
    nc = t // CHUNK
    col, ps_spec, wcol, smem, vec = _gdn_specs(b_loc, t)

    def body(q_ref, k_ref, v_ref, z_ref, ps_ref, wq_ref, wk_ref, wv_ref, alog_ref, dtb_ref, gn_ref,
             oa_ref, oraw_ref, s_ref, qn, kn, vv, g128, g64, b128, b64, uq_s, p_s, kd_s, dec_s, pad_s):
        h = pl.program_id(1)
        g, beta, _, _ = _gdn_gates(ps_ref[...], h, alog_ref, dtb_ref)
        g128[...] = jnp.broadcast_to(g, (t, LANES))
        g64[...] = jnp.broadcast_to(g, (t, CHUNK))
        b128[...] = jnp.broadcast_to(beta, (t, LANES))
        b64[...] = jnp.broadcast_to(beta, (t, CHUNK))
        _pad_zero(pad_s)
        pq = _conv(q_ref[...], wq_ref, pad_s)
        yq = pq * _sigmoid(pq)
        qn[...] = yq * (lax.rsqrt(jnp.sum(yq * yq, axis=1, keepdims=True) + EPS) * (DH ** -0.5))
        pk = _conv(k_ref[...], wk_ref, pad_s)
        yk = pk * _sigmoid(pk)
        kn[...] = yk * lax.rsqrt(jnp.sum(yk * yk, axis=1, keepdims=True) + EPS)
        pv = _conv(v_ref[...], wv_ref, pad_s)
        vv[...] = pv * _sigmoid(pv)
        cs = _chunk_consts()

        def pre_group(gi, _):
            idx = [gi * GDN_GROUP + c for c in range(GDN_GROUP)]
            rows = [pl.ds(pl.multiple_of(i * CHUNK, CHUNK), CHUNK) for i in idx]
            ins = [_V([ref[r, :] for r in rows]) for ref in (qn, kn, vv, g128, g64, b128, b64)]
            f = _gdn_chunk_pre(*ins, cs)
            for c, (i, r) in enumerate(zip(idx, rows)):
                vv[r, :] = f["w"].xs[c]
                uq_s[i, 0:CHUNK, :] = _bf(f["u"].xs[c])
                uq_s[i, CHUNK:2 * CHUNK, :] = _bf(f["q_dec"].xs[c])
                p_s[r, :] = _bf(f["p"].xs[c])
                kd_s[r, :] = _bf(f["k_dec"].xs[c])
                dec_s[pl.ds(pl.multiple_of(i * 8, 8), 8), :] = jnp.broadcast_to(f["dec"].xs[c], (8, LANES))
            return 0

        lax.fori_loop(0, nc // GDN_GROUP, pre_group, 0)

        def chunk(i, s):
            r = pl.ds(pl.multiple_of(i * CHUNK, CHUNK), CHUNK)
            us = _dot(uq_s[i], _bf(s))
            vnb = _bf(vv[r, :] - us[0:CHUNK])
            oraw_ref[r, :] = us[CHUNK:2 * CHUNK] + _dot(p_s[r, :], vnb)
            s_ref[0, 0, i] = s
            return s * dec_s[pl.ds(pl.multiple_of(i * 8, 8), 1), :] + _dot_tn(kd_s[r, :], vnb)

        lax.fori_loop(0, nc, chunk, jnp.zeros((DH, DH), F32))
        o = oraw_ref[...]
        rr = lax.rsqrt(jnp.mean(o * o, axis=1, keepdims=True) + EPS)
        z = z_ref[...]
        oa_ref[...] = _bf((o * rr * gn_ref[...]) * (z * _sigmoid(z)))

    return pl.pallas_call(
        body, name="gdn_fwd", grid=(b_loc, HEADS),
        in_specs=[col(0), col(HEADS), col(2 * HEADS), col(3 * HEADS), ps_spec, wcol(0), wcol(HEADS), wcol(2 * HEADS),
                  smem, smem, vec],
        out_specs=[pl.BlockSpec((t, DH), lambda b, h: (b, h)), pl.BlockSpec((t, DH), lambda b, h: (b, h)),
                   pl.BlockSpec((1, 1, nc, DH, DH), lambda b, h: (b, h, 0, 0, 0))],
        out_shape=[jax.ShapeDtypeStruct((n, HEADS * DH), BF16), jax.ShapeDtypeStruct((n, HEADS * DH), F32),
                   jax.ShapeDtypeStruct((b_loc, HEADS, nc, DH, DH), F32)],
        scratch_shapes=([pltpu.VMEM((t, DH), F32)] * 3 + [pltpu.VMEM((t, LANES), F32), pltpu.VMEM((t, CHUNK), F32)] * 2
                        + [pltpu.VMEM((nc, 2 * CHUNK, DH), BF16), pltpu.VMEM((t, CHUNK), BF16), pltpu.VMEM((t, DH), BF16),
                           pltpu.VMEM((8 * nc, LANES), F32), pltpu.VMEM((t + 2 * PAD, LANES), F32)]),
        compiler_params=_cparams(("arbitrary", "arbitrary")),
    )(pg, pg, pg, pg, ps, convw, convw, convw, a_log, dt_bias, gnorm)


def _gdn_bwd(pg, ps, convw, a_log, dt_bias, gnorm, d_oa, o_raw, s_all, b_loc, t):
    n = b_loc * t
    nc = t // CHUNK
    col, ps_spec, wcol, smem, vec = _gdn_specs(b_loc, t)

    def body(q_ref, k_ref, v_ref, z_ref, ps_ref, wq_ref, wk_ref, wv_ref, alog_ref, dtb_ref, gn_ref,
             doa_ref, oraw_ref, s_ref,
             dq_ref, dk_ref, dv_ref, dz_ref, dps_ref, dcw_ref, dsm_ref,
             qn, kn, vv, g128, g64, b128, b64, do_s, bg_s, u_s, vn_s, dvn_s, dcy_s, kk_s, qk_s, tm_s, dsn_s, pad_s):
        b, h = pl.program_id(0), pl.program_id(1)
        g, beta, _, _ = _gdn_gates(ps_ref[...], h, alog_ref, dtb_ref)
        g128[...] = jnp.broadcast_to(g, (t, LANES))
        g64[...] = jnp.broadcast_to(g, (t, CHUNK))
        b128[...] = jnp.broadcast_to(beta, (t, LANES))
        b64[...] = jnp.broadcast_to(beta, (t, CHUNK))
        _pad_zero(pad_s)

        def prep(x_ref, w_ref):
            p = _conv(x_ref[...], w_ref, pad_s)
            sg = _sigmoid(p)
            return p, sg, p * sg

        _, _, yq = prep(q_ref, wq_ref)
        qn[...] = yq * (lax.rsqrt(jnp.sum(yq * yq, axis=1, keepdims=True) + EPS) * (DH ** -0.5))
        _, _, yk = prep(k_ref, wk_ref)
        kn[...] = yk * lax.rsqrt(jnp.sum(yk * yk, axis=1, keepdims=True) + EPS)
        _, _, yv = prep(v_ref, wv_ref)
        vv[...] = yv

        o = oraw_ref[...]
        z = z_ref[...]
        doa = doa_ref[...]
        gn = gn_ref[...]
        ro = lax.rsqrt(jnp.mean(o * o, axis=1, keepdims=True) + EPS)
        sz = _sigmoid(z)
        dz_ref[...] = _bf(doa * (o * ro * gn) * (sz * (1.0 + z * (1.0 - sz))))
        dn = doa * (z * sz)
        dgn = jnp.sum(dn * o * ro, axis=0, keepdims=True)
        gy = dn * gn
        do_s[...] = ro * gy - o * (ro * ro * ro * (1.0 / DH)) * jnp.sum(gy * o, axis=1, keepdims=True)

        cs = _chunk_consts()

        def pre_group(gi, _):
            idx = [gi * GDN_GROUP + c for c in range(GDN_GROUP)]
            rows = [pl.ds(pl.multiple_of(i * CHUNK, CHUNK), CHUNK) for i in idx]
            ins = [_V([ref[r, :] for r in rows]) for ref in (qn, kn, vv, g128, g64, b128, b64)]
            states = _V([_bf(s_ref[0, 0, i]) for i in idx])
            f = _gdn_chunk_pre(*ins, cs)
            v_new = f["w"] - _vdot(_vbf(f["u"]), states)
            for c, r in enumerate(rows):
                bg_s[r, :] = f["big_g"].xs[c]
                u_s[r, :] = f["u"].xs[c]
                vn_s[r, :] = v_new.xs[c]
                dcy_s[r, :] = f["decay"].xs[c]
                kk_s[r, :] = f["kk"].xs[c]
                qk_s[r, :] = f["qk"].xs[c]
                tm_s[r, :] = f["tm"].xs[c]
            return 0

        lax.fori_loop(0, nc // GDN_GROUP, pre_group, 0)

        def chunk(j, ds):
            i = nc - 1 - j
            r = pl.ds(pl.multiple_of(i * CHUNK, CHUNK), CHUNK)
            big_g = bg_s[r, :]
            g_last = jnp.sum(g128[r, :], axis=0, keepdims=True)
            dob = _bf(do_s[r, :])
            dv_new = (_dot_tn(_bf(qk_s[r, :] * dcy_s[r, :]), dob)
                      + _dot(_bf(kn[r, :] * jnp.exp(g_last - big_g)), _bf(ds)))
            dvn_s[r, :] = dv_new
            dsn_s[i] = ds
            return (_dot_tn(_bf(qn[r, :] * jnp.exp(big_g)), dob) + jnp.exp(g_last) * ds
                    - _dot_tn(_bf(u_s[r, :]), _bf(dv_new)))

        lax.fori_loop(0, nc, chunk, jnp.zeros((DH, DH), F32))

        def post_group(gi, _):
            idx = [gi * GDN_GROUP + c for c in range(GDN_GROUP)]
            rows = [pl.ds(pl.multiple_of(i * CHUNK, CHUNK), CHUNK) for i in idx]
            def rows_of(ref):
                return _V([ref[r, :] for r in rows])

            dq, dk, dv, dg, dbeta = _gdn_chunk_post(
                rows_of(qn), rows_of(kn), rows_of(vv), rows_of(g128), rows_of(b128), rows_of(b64),
                _V([s_ref[0, 0, i] for i in idx]), _V([dsn_s[i] for i in idx]), rows_of(do_s), rows_of(dvn_s),
                rows_of(bg_s), rows_of(dcy_s), rows_of(kk_s), rows_of(qk_s), rows_of(tm_s), rows_of(u_s), rows_of(vn_s),
                cs)
            for c, r in enumerate(rows):
                qn[r, :] = dq.xs[c]
                kn[r, :] = dk.xs[c]
                vv[r, :] = dv.xs[c]
                g128[r, :] = dg.xs[c]
                b128[r, :] = jnp.broadcast_to(dbeta.xs[c], (CHUNK, LANES))
            return 0

        lax.fori_loop(0, nc // GDN_GROUP, post_group, 0)
        dqh, dkh, dvh = qn, kn, vv

        g, beta, a, sp_in = _gdn_gates(ps_ref[...], h, alog_ref, dtb_ref)
        dg = g128[...]
        d_ga = dg * (-a) * _sigmoid(sp_in)
        d_alog = jnp.sum(dg * g, axis=0, keepdims=True)
        d_dtb = jnp.sum(d_ga, axis=0, keepdims=True)
        d_gb = b128[...] * (beta * (1.0 - beta))
        lane = lax.broadcasted_iota(jnp.int32, (t, LANES), 1)
        contrib = jnp.where(lane == LANE_GA + h, d_ga, 0.0) + jnp.where(lane == LANE_GB + h, d_gb, 0.0)

        @pl.when(h == 0)
        def _():
            dps_ref[...] = jnp.zeros_like(dps_ref)

        dps_ref[...] += contrib

        lane1 = lax.broadcasted_iota(jnp.int32, (1, LANES), 1)
        small = _stack_rows([jnp.where(lane1 == h, d_alog, 0.0), jnp.where(lane1 == h, d_dtb, 0.0), dgn], 8)

        @pl.when((b == 0) & (h == 0))
        def _():
            dsm_ref[...] = jnp.zeros_like(dsm_ref)
            dcw_ref[...] = jnp.zeros_like(dcw_ref)

        dsm_ref[...] += small

        def conv_bwd(dp, x, w_ref, slot):
            dw = _stack_rows([jnp.sum(dp * _shifted(pad_s, 3), axis=0, keepdims=True),
                              jnp.sum(dp * _shifted(pad_s, 2), axis=0, keepdims=True),
                              jnp.sum(dp * _shifted(pad_s, 1), axis=0, keepdims=True),
                              jnp.sum(dp * x, axis=0, keepdims=True)], CONV_K)
            dcw_ref[slot] += dw
            pad_s[PAD:PAD + t, :] = dp
            dx = _shifted(pad_s, -3) * w_ref[0:1, :]
            dx = dx + _shifted(pad_s, -2) * w_ref[1:2, :]
            dx = dx + _shifted(pad_s, -1) * w_ref[2:3, :]
            return dx + dp * w_ref[3:4, :]

        def l2_bwd(dqn, y, c):
            r = lax.rsqrt(jnp.sum(y * y, axis=1, keepdims=True) + EPS)
            s1 = jnp.sum(dqn * y, axis=1, keepdims=True)
            return c * r * dqn - (c * r * r * r) * s1 * y

        def silu_bwd(p, sg):
            return sg * (1.0 + p * (1.0 - sg))

        pq, sq, yq = prep(q_ref, wq_ref)
        dq_ref[...] = _bf(conv_bwd(l2_bwd(dqh[...], yq, DH ** -0.5) * silu_bwd(pq, sq), q_ref[...], wq_ref, h))
        pk, sk, yk = prep(k_ref, wk_ref)
        dk_ref[...] = _bf(conv_bwd(l2_bwd(dkh[...], yk, 1.0) * silu_bwd(pk, sk), k_ref[...], wk_ref, HEADS + h))
        pv, sv, _ = prep(v_ref, wv_ref)
        dv_ref[...] = _bf(conv_bwd(dvh[...] * silu_bwd(pv, sv), v_ref[...], wv_ref, 2 * HEADS + h))

    blk = pl.BlockSpec((t, DH), lambda b, h: (b, h))
    ob = jax.ShapeDtypeStruct((n, HEADS * DH), BF16)
    return pl.pallas_call(
        body, name="gdn_bwd", grid=(b_loc, HEADS),
        in_specs=[col(0), col(HEADS), col(2 * HEADS), col(3 * HEADS), ps_spec, wcol(0), wcol(HEADS), wcol(2 * HEADS),
                  smem, smem, vec, blk, blk, pl.BlockSpec((1, 1, nc, DH, DH), lambda b, h: (b, h, 0, 0, 0))],
        out_specs=[blk, blk, blk, blk, ps_spec,
                   pl.BlockSpec((3 * HEADS, CONV_K, DH), lambda b, h: (0, 0, 0)),
                   pl.BlockSpec((8, LANES), lambda b, h: (0, 0))],
        out_shape=[ob, ob, ob, ob, jax.ShapeDtypeStruct((n, LANES), F32),
                   jax.ShapeDtypeStruct((3 * HEADS, CONV_K, DH), F32), jax.ShapeDtypeStruct((8, LANES), F32)],
        scratch_shapes=([pltpu.VMEM((t, DH), F32)] * 3 + [pltpu.VMEM((t, LANES), F32), pltpu.VMEM((t, CHUNK), F32)] * 2
                        + [pltpu.VMEM((t, DH), F32)] * 5 + [pltpu.VMEM((t, CHUNK), F32)] * 4
                        + [pltpu.VMEM((nc, DH, DH), F32), pltpu.VMEM((t + 2 * PAD, LANES), F32)]),
        compiler_params=_cparams(("arbitrary", "arbitrary")),
    )(pg, pg, pg, pg, ps, convw, convw, convw, a_log, dt_bias, gnorm, d_oa, o_raw, s_all)


def _fox_prologue(q_ref, k_ref, v_ref, ps_ref, fb_ref, gq_ref, gk_ref, h, t, qs, ks, vs, ccol, crow):
    nb = t // FOX_BLOCK
    q, k = q_ref[...], k_ref[...]
    rq = lax.rsqrt(jnp.mean(q * q, axis=1, keepdims=True) + EPS)
    rk = lax.rsqrt(jnp.mean(k * k, axis=1, keepdims=True) + EPS)
    qs[...] = _bf(q * rq * gq_ref[...])
    ks[...] = _bf(k * rk * gk_ref[...])
    vs[...] = _bf(v_ref[...])
    f_in = _head_lane(ps_ref[...], LANE_FF + h) + fb_ref[0, h]
    ccol[...] = jnp.broadcast_to(-_softplus(-f_in), (t, LANES))
    r = lax.broadcasted_iota(jnp.int32, (FOX_BLOCK, FOX_BLOCK), 0)
    c = lax.broadcasted_iota(jnp.int32, (FOX_BLOCK, FOX_BLOCK), 1)
    trilf, triuf = (r >= c).astype(F32), (r <= c).astype(F32)
    blocks = [pl.ds(j * FOX_BLOCK, FOX_BLOCK) for j in range(nb)]
    lfs = _V([ccol[rb, :] for rb in blocks])
    cc = _dot_exact_l(trilf, lfs, terms=3)
    cr = _dot_exact_r(lfs, triuf, _vdot_tn, terms=3)
    sums = _vsum(lfs, 0)
    carry = jnp.zeros((1, LANES), F32)
    for j, rb in enumerate(blocks):
        ccol[rb, :] = cc.xs[j] + carry
        crow[j] = (cr.xs[j] + carry)[0:8]
        carry = carry + sums.xs[j]
    return rq, rk, f_in


def _fox_scores(q_rows, k_rows, cc, cr, row0, col0):
    s = _dot_nt(q_rows, k_rows) * (DH ** -0.5) + cc - cr
    r = lax.broadcasted_iota(jnp.int32, s.shape, 0)
    c = lax.broadcasted_iota(jnp.int32, s.shape, 1)
    return jnp.where(row0 + r >= col0 + c, s, NEG)


def _fox_specs(t):
    def col(off):
        return pl.BlockSpec((t, DH), lambda b, h: (b, off + h))

    ps_spec = pl.BlockSpec((t, LANES), lambda b, h: (b, 0))
    smem = pl.BlockSpec(memory_space=pltpu.SMEM)
    vec = pl.BlockSpec((1, DH), lambda b, h: (0, 0))
    blk = pl.BlockSpec((t, DH), lambda b, h: (b, h))
    return col, ps_spec, smem, vec, blk


def _fox_fwd(pf, ps, f_bias, gq, gk, b_loc, t):
    n = b_loc * t
    nb = t // FOX_BLOCK
    kt = min(FOX_TILE, t)
    nsub = kt // FOX_BLOCK
    col, ps_spec, smem, vec, blk = _fox_specs(t)

    def body(q_ref, k_ref, v_ref, ps_ref, fb_ref, gq_ref, gk_ref, o_ref, lse_ref, qs, ks, vs, ccol, crow):
        h = pl.program_id(1)
        _fox_prologue(q_ref, k_ref, v_ref, ps_ref, fb_ref, gq_ref, gk_ref, h, t, qs, ks, vs, ccol, crow)

        def qblock(i, _):
            ri = pl.ds(pl.multiple_of(i * FOX_SHORT, FOX_SHORT), FOX_SHORT)
            qi = qs[ri, :]
            cc = jnp.concatenate([ccol[ri, :]] * nsub, axis=1)

            def ktile(j, carry):
                m, l, acc = carry
                rj = pl.ds(pl.multiple_of(j * kt, kt), kt)
                cr = jnp.concatenate([crow[j * nsub + u, 0:1, :] for u in range(nsub)], axis=1)
                s = _fox_scores(qi, ks[rj, :], cc, cr, i * FOX_SHORT, j * kt)
                m_new = jnp.maximum(m, jnp.max(s, axis=1, keepdims=True))
                p = jnp.exp(s - m_new)
                alpha = jnp.exp(m - m_new)
                l = alpha * l + jnp.sum(p, axis=1, keepdims=True)
                acc = alpha * acc + _dot(_bf(p), vs[rj, :])
                return m_new, l, acc

            m, l, acc = lax.fori_loop(0, (i * FOX_SHORT) // kt + 1, ktile, (jnp.full((FOX_SHORT, 1), NEG, F32),
                                                                            jnp.zeros((FOX_SHORT, 1), F32),
                                                                            jnp.zeros((FOX_SHORT, DH), F32)))
            o_ref[ri, :] = acc / l
            lse_ref[ri, :] = jnp.broadcast_to(m + jnp.log(l), (FOX_SHORT, LANES))
            return 0

        lax.fori_loop(0, t // FOX_SHORT, qblock, 0)

    o = jax.ShapeDtypeStruct((n, HEADS * DH), F32)
    return pl.pallas_call(
        body, name="fox_fwd", grid=(b_loc, HEADS),
        in_specs=[col(0), col(HEADS), col(2 * HEADS), ps_spec, smem, vec, vec],
        out_specs=[blk, blk], out_shape=[o, o],
        scratch_shapes=[pltpu.VMEM((t, DH), BF16)] * 3 + [pltpu.VMEM((t, LANES), F32), pltpu.VMEM((nb, 8, LANES), F32)],
        compiler_params=_cparams(("arbitrary", "arbitrary")),
    )(pf, pf, pf, ps, f_bias, gq, gk)


def _fox_bwd(pf, ps, f_bias, gq, gk, d_ob, ob, lse, dps_in, b_loc, t):
    n = b_loc * t
    nb = t // FOX_BLOCK
    qt = min(FOX_TILE, t)
    scale = DH ** -0.5
    col, ps_spec, smem, vec, blk = _fox_specs(t)

    def body(q_ref, k_ref, v_ref, ps_ref, fb_ref, gq_ref, gk_ref, do_ref, o_ref, lse_ref, dpsi_ref,
             dq_ref, dk_ref, dv_ref, dps_ref, dsm_ref, qs, ks, vs, ccol, crow, dos, dl, dqa, dcr, dcq):
        b, h = pl.program_id(0), pl.program_id(1)
        rq, _, f_in = _fox_prologue(q_ref, k_ref, v_ref, ps_ref, fb_ref, gq_ref, gk_ref, h, t, qs, ks, vs, ccol, crow)
        dov = do_ref[...]
        dos[...] = _bf(dov)
        dl[...] = jnp.broadcast_to(jnp.sum(dov * o_ref[...], axis=1, keepdims=True), (t, LANES))
        dqa[...] = jnp.zeros_like(dqa)
        dcq[...] = jnp.zeros_like(dcq)
        gkv = gk_ref[...]

        ksub = FOX_SHORT // FOX_BLOCK

        def kblock(j, dgk):
            rj = pl.ds(pl.multiple_of(j * FOX_SHORT, FOX_SHORT), FOX_SHORT)
            kj, vj = ks[rj, :], vs[rj, :]
            cr = jnp.concatenate([crow[j * ksub + u, 0:1, :] for u in range(ksub)], axis=1)

            def wide(x):
                return jnp.concatenate([x] * ksub, axis=1)

            def qtile(i, carry):
                dk_acc, dv_acc, dc = carry
                ri = pl.ds(pl.multiple_of(i * qt, qt), qt)
                qi, doi = qs[ri, :], dos[ri, :]
                s = _fox_scores(qi, kj, wide(ccol[ri, :]), cr, i * qt, j * FOX_SHORT)
                p = jnp.exp(s - wide(lse_ref[ri, :]))
                ds = p * (_dot_nt(doi, vj) - wide(dl[ri, :]))
                dsb = _bf(ds)
                dqa[ri, :] += _dot(dsb, kj)
                dcq[ri, :] += jnp.broadcast_to(jnp.sum(ds, axis=1, keepdims=True), (qt, LANES))
                return (dk_acc + _dot_tn(dsb, qi), dv_acc + _dot_tn(_bf(p), doi),
                        dc - jnp.sum(ds, axis=0, keepdims=True))

            z = jnp.zeros((FOX_SHORT, DH), F32)
            dk_acc, dv_acc, dc = lax.fori_loop((j * FOX_SHORT) // qt, t // qt, qtile,
                                               (z, z, jnp.zeros((1, FOX_SHORT), F32)))
            dv_ref[rj, :] = _bf(dv_acc)
            for u in range(ksub):
                dcr[pl.ds(pl.multiple_of((j * ksub + u) * 8, 8), 8), :] = jnp.broadcast_to(
                    dc[:, u * FOX_BLOCK:(u + 1) * FOX_BLOCK], (8, LANES))
            kraw = k_ref[rj, :]
            rk = lax.rsqrt(jnp.mean(kraw * kraw, axis=1, keepdims=True) + EPS)
            dkn = dk_acc * scale
            gy = dkn * gkv
            dk_ref[rj, :] = _bf(rk * gy - kraw * (rk * rk * rk * (1.0 / DH)) * jnp.sum(gy * kraw, axis=1, keepdims=True))
            return dgk + jnp.sum(dkn * kraw * rk, axis=0, keepdims=True)

        dgk = lax.fori_loop(0, t // FOX_SHORT, kblock, jnp.zeros((1, DH), F32))

        q = q_ref[...]
        dqn = dqa[...] * scale
        gy = dqn * gq_ref[...]
        dq_ref[...] = _bf(rq * gy - q * (rq * rq * rq * (1.0 / DH)) * jnp.sum(gy * q, axis=1, keepdims=True))
        dgq = jnp.sum(dqn * q * rq, axis=0, keepdims=True)

        r = lax.broadcasted_iota(jnp.int32, (FOX_BLOCK, FOX_BLOCK), 0)
        c = lax.broadcasted_iota(jnp.int32, (FOX_BLOCK, FOX_BLOCK), 1)
        triuf = (r <= c).astype(F32)

        def rev(jj, carry):
            j = nb - 1 - jj
            rows = pl.ds(pl.multiple_of(j * FOX_BLOCK, FOX_BLOCK), FOX_BLOCK)
            rowv = dcr[pl.ds(pl.multiple_of(j * 8, 8), 1), :]
            colv = jnp.sum(jnp.where(c >= r, jnp.broadcast_to(rowv, (FOX_BLOCK, LANES)), 0.0), axis=1, keepdims=True)
            qcol = dcq[rows, :]
            dl[rows, :] = colv + _dot_exact_l(triuf, qcol, terms=3) + carry
            return carry + jnp.sum(rowv, axis=1, keepdims=True) + jnp.sum(qcol, axis=0, keepdims=True)

        lax.fori_loop(0, nb, rev, jnp.zeros((1, LANES), F32))
        d_ff = dl[...] * _sigmoid(-f_in)
        lane = lax.broadcasted_iota(jnp.int32, (t, LANES), 1)

        @pl.when(h == 0)
        def _():
            dps_ref[...] = dpsi_ref[...]

        dps_ref[...] += jnp.where(lane == LANE_FF + h, d_ff, 0.0)

        lane1 = lax.broadcasted_iota(jnp.int32, (1, LANES), 1)
        d_fb = jnp.sum(d_ff, axis=0, keepdims=True)
        small = _stack_rows([dgq, dgk, jnp.where(lane1 == h, d_fb, 0.0)], 8)

        @pl.when((b == 0) & (h == 0))
        def _():
            dsm_ref[...] = jnp.zeros_like(dsm_ref)

        dsm_ref[...] += small

    ob_ = jax.ShapeDtypeStruct((n, HEADS * DH), BF16)
    return pl.pallas_call(
        body, name="fox_bwd", grid=(b_loc, HEADS),
        in_specs=[col(0), col(HEADS), col(2 * HEADS), ps_spec, smem, vec, vec, blk, blk, blk, ps_spec],
        out_specs=[blk, blk, blk, ps_spec, pl.BlockSpec((8, LANES), lambda b, h: (0, 0))],
        out_shape=[ob_, ob_, ob_, jax.ShapeDtypeStruct((n, LANES), F32), jax.ShapeDtypeStruct((8, LANES), F32)],
        scratch_shapes=([pltpu.VMEM((t, DH), BF16)] * 3 + [pltpu.VMEM((t, LANES), F32), pltpu.VMEM((nb, 8, LANES), F32)]
                        + [pltpu.VMEM((t, DH), BF16), pltpu.VMEM((t, LANES), F32), pltpu.VMEM((t, DH), F32),
                           pltpu.VMEM((8 * nb, LANES), F32), pltpu.VMEM((t, LANES), F32)]),
        compiler_params=_cparams(("arbitrary", "arbitrary")),
    )(pf, pf, pf, ps, f_bias, gq, gk, d_ob, ob, lse, dps_in)


class _NoExchange:
    def late_weights(self, after):
        return {}

    def grads_ready(self, grads, tie):
        return tie


def _local_step(x, target, w, b_loc, t, comm=None):
    comm = comm or _NoExchange()
    w = dict(w)
    xf = x
    u = _rms_fwd(xf, w["norm_mix_g"], "rms_mix")
    pg = _mm(u, w["w_gdn"], name="proj_gdn")
    pf = _mm(u, w["w_fox"], name="proj_fox")
    pgate = _mm(u, w["w_gate"], name="proj_gate")
    ps = _mm(u, w["w_small"], name="proj_small")
    oa, o_raw, s_all = _gdn_fwd(pg, ps, w["conv_w"], w["a_log"], w["dt_bias"], w["gdn_norm_g"], b_loc, t)
    ob, lse = _fox_fwd(pf, ps, w["f_bias"], w["fox_q_norm_g"], w["fox_k_norm_g"], b_loc, t)
    w.update(comm.late_weights(ob))
    ya = _mm(oa, w["w_proj_gdn"], name="proj_a")
    yb = _mm(ob, w["w_proj_fox"], name="proj_b")
    merged = _merge_fwd(ya, yb, pgate)
    h = _mm(merged, w["w_out"], name="proj_out", epi=lambda acc, xr: acc + xr, extras=(xf,))
    hn = _rms_fwd(h, w["norm_mlp_g"], "rms_mlp")
    up, act = _mm(hn, w["w_up"], name="mlp_up", out_dtype=BF16, out2=(_relu2, BF16))
    out = _mm(act, w["w_down"], name="mlp_down", epi=lambda acc, hr: acc + hr, extras=(h,))
    d_out, d_out16, loss_blk = _loss_bwd(out, target)

    g = {}
    g["w_down"] = _mm(act, d_out16, name="dw_down", ta=True, out_dtype=BF16)
    d_up = _mm(d_out16, w["w_down"], name="d_up", tb=True, out_dtype=BF16,
               epi=lambda acc, upr: acc * (2.0 * jnp.maximum(upr.astype(F32), 0.0)), extras=(up,))
    g["w_up"] = _mm(hn, d_up, name="dw_up", ta=True, out_dtype=BF16)
    mlp_gain = comm.grads_ready({"w_down": g["w_down"], "w_up": g["w_up"]}, w["norm_mlp_g"])
    d_hn = _mm(d_up, w["w_up"], name="d_hn", tb=True)
    dh, dh16, g["norm_mlp_g"] = _rms_bwd(d_hn, h, mlp_gain, d_out, "rms_mlp_bwd")
    g["w_out"] = _mm(merged, dh16, name="dw_out", ta=True, out_dtype=BF16)
    dm = _mm(dh16, w["w_out"], name="d_merged", tb=True)
    dya, dyb, dgate_a, dgate_b = _merge_bwd(dm, ya, yb, pgate)
    g["w_proj_gdn"] = _mm(oa, dya, name="dw_proj_a", ta=True, out_dtype=BF16)
    g["w_proj_fox"] = _mm(ob, dyb, name="dw_proj_b", ta=True, out_dtype=BF16)
    gdn_gain = comm.grads_ready({"w_out": g["w_out"], "w_proj_gdn": g["w_proj_gdn"], "w_proj_fox": g["w_proj_fox"]},
                                w["gdn_norm_g"])
    d_oa = _mm(dya, w["w_proj_gdn"], name="d_oa", tb=True)
    d_ob = _mm(dyb, w["w_proj_fox"], name="d_ob", tb=True)
    dgq, dgk, dgv, dgz, dps, dcw, gdn_small = _gdn_bwd(pg, ps, w["conv_w"], w["a_log"], w["dt_bias"], gdn_gain,
                                                       d_oa, o_raw, s_all, b_loc, t)
    dfq, dfk, dfv, dps, fox_small = _fox_bwd(pf, ps, w["f_bias"], w["fox_q_norm_g"], w["fox_k_norm_g"],
                                             d_ob, ob, lse, dps, b_loc, t)
    segs = [(dgq, "w_gdn", 0), (dgk, "w_gdn", 1024), (dgv, "w_gdn", 2048), (dgz, "w_gdn", 3072),
            (dfq, "w_fox", 0), (dfk, "w_fox", 1024), (dfv, "w_fox", 2048),
            (dgate_a, "w_gate", 0), (dgate_b, "w_gate", 1024)]
    dws = [_mm(u, dps, name="dw_small", ta=True, out_dtype=BF16)]
    dws += [_mm(u, dseg, name=f"dw_in_{idx}", ta=True, out_dtype=BF16) for idx, (dseg, _, _) in enumerate(segs)]
    g["w_in_parts"] = dws
    mix_gain = comm.grads_ready({"w_in_parts": dws}, w["norm_mix_g"])
    du = _mm(dps, w["w_small"], name="du_small", tb=True)
    for idx, (dseg, wname, off) in enumerate(segs):
        du = _mm(dseg, w[wname], name=f"du_{idx}", tb=True, b_koff=off,
                 epi=lambda acc, prev: acc + prev, extras=(du,))
    grad_x, _, g["norm_mix_g"] = _rms_bwd(du, xf, mix_gain, dh, "rms_mix_bwd")
    g["conv"] = dcw
    g["gdn_small"] = gdn_small
    g["fox_small"] = fox_small
    return loss_blk, grad_x, g


def _position():
    x, y, c = lax.axis_index("x"), lax.axis_index("y"), lax.axis_index("c")
    return x, y, c


def _to_bf16(arrs, name):
    n = len(arrs)

    def body(*refs):
        for i in range(n):
            refs[n + i][...] = _bf(refs[i][...])

    return pl.pallas_call(
        body, name=name,
        out_shape=[jax.ShapeDtypeStruct(a.shape, BF16) for a in arrs],
        compiler_params=_cparams(),
    )(*arrs)


def _all_gather(arrs, name):
    n = len(arrs)
    hbm = pl.BlockSpec(memory_space=pl.ANY)

    def body(*refs):
        ins, outs = refs[:n], refs[n:2 * n]
        send, recv, loc = refs[2 * n:]
        x, y, c = _position()
        me = 4 * x + 2 * y + c
        sibling = (x, y, 1 - c)
        chips = [(1 - x, y), (x, 1 - y), (1 - x, 1 - y)]

        def idx(px, py, pc):
            return 4 * px + 2 * py + pc

        def cp(a, k, block, to, src=None):
            return pltpu.make_async_remote_copy(
                src_ref=outs[a].at[block] if src is None else src, dst_ref=outs[a].at[block],
                send_sem=send.at[a, k], recv_sem=recv.at[a, k], device_id=to, device_id_type=MESH)

        mine = [pltpu.make_async_copy(ins[a], outs[a].at[me], loc.at[a]) for a in range(n)]
        for m in mine:
            m.start()
        first = []
        for a in range(n):
            first.append(cp(a, 0, me, sibling, src=ins[a]))
            first += [cp(a, 1 + j, me, (*chip, c), src=ins[a]) for j, chip in enumerate(chips)]
        for f in first:
            f.start()
        passed = []
        for j, chip in enumerate(chips):
            for a in range(n):
                cp(a, 1 + j, idx(*chip, c), (x, y, c)).wait_recv()
                p = cp(a, 4 + j, idx(*chip, c), sibling)
                p.start()
                passed.append(p)
        for a in range(n):
            cp(a, 0, idx(x, y, 1 - c), (x, y, c)).wait_recv()
            for j, chip in enumerate(chips):
                cp(a, 4 + j, idx(*chip, 1 - c), (x, y, c)).wait_recv()
        for f in first + passed:
            f.wait_send()
        for m in mine:
            m.wait()

    return pl.pallas_call(
        body, name=name,
        in_specs=[hbm] * n, out_specs=[hbm] * n,
        out_shape=[jax.ShapeDtypeStruct((N_DEV,) + a.shape, a.dtype) for a in arrs],
        scratch_shapes=[pltpu.SemaphoreType.DMA((n, 7)), pltpu.SemaphoreType.DMA((n, 7)), pltpu.SemaphoreType.DMA((n,))],
        compiler_params=pltpu.CompilerParams(has_side_effects=True),
    )(*arrs)


def _peer(x, y, c, rel):
    return ((1 - x) if rel & 4 else x, (1 - y) if rel & 2 else y, (1 - c) if rel & 1 else c)


def _exchange(arrs, name):
    n = len(arrs)
    hbm = pl.BlockSpec(memory_space=pl.ANY)

    def body(*refs):
        ins, outs = refs[:n], refs[n:2 * n]
        send, recv, loc = refs[2 * n:]
        x, y, c = _position()
        me = 4 * x + 2 * y + c
        mine = [pltpu.make_async_copy(ins[a].at[me], outs[a].at[me], loc.at[a]) for a in range(n)]
        for m in mine:
            m.start()
        copies = []
        for rel in range(1, N_DEV):
            px, py, pc = _peer(x, y, c, rel)
            for a in range(n):
                copies.append(pltpu.make_async_remote_copy(
                    src_ref=ins[a].at[4 * px + 2 * py + pc], dst_ref=outs[a].at[me],
                    send_sem=send.at[a, rel - 1], recv_sem=recv.at[a, rel - 1],
                    device_id=(px, py, pc), device_id_type=MESH))
        for cpy in copies:
            cpy.start()
        for cpy in copies:
            cpy.wait()
        for m in mine:
            m.wait()

    return pl.pallas_call(
        body, name=name,
        in_specs=[hbm] * n, out_specs=[hbm] * n,
        out_shape=[jax.ShapeDtypeStruct(a.shape, a.dtype) for a in arrs],
        scratch_shapes=[pltpu.SemaphoreType.DMA((n, 7)), pltpu.SemaphoreType.DMA((n, 7)), pltpu.SemaphoreType.DMA((n,))],
        compiler_params=pltpu.CompilerParams(has_side_effects=True),
    )(*arrs)


HBM_SPEC = pl.BlockSpec(memory_space=pltpu.HBM)
SEM_SPEC = pl.BlockSpec(memory_space=pltpu.SEMAPHORE)
DATAFLOW = pltpu.SideEffectType.DATAFLOW_SIDE_EFFECTING


CHIP_RELS = (2, 4, 6)


def _push_start(arrs, slots, name, chips=False):
    n = len(arrs)
    n_slots = 4 if chips else N_DEV
    rels = CHIP_RELS if chips else tuple(range(1, N_DEV))
    land_shapes = [a.shape if slots else (n_slots,) + a.shape for a in arrs]

    def body(*refs):
        ins, lands, sends, recvs, token = refs[:n], refs[n:2 * n], refs[2 * n:3 * n], refs[3 * n:4 * n], refs[-1]
        x, y, c = _position()
        for rel in rels:
            px, py, pc = _peer(x, y, c, rel)
            mine, theirs = (2 * x + y, 2 * px + py) if chips else (4 * x + 2 * y + c, 4 * px + 2 * py + pc)
            for a in range(n):
                pltpu.make_async_remote_copy(
                    src_ref=ins[a].at[theirs] if slots else ins[a], dst_ref=lands[a].at[mine],
                    send_sem=sends[a], recv_sem=recvs[a], device_id=(px, py, pc), device_id_type=MESH).start()
        token[...] = jnp.zeros_like(token)

    sem = pltpu.SemaphoreType.DMA(())
    outs = pl.pallas_call(
        body, name=name,
        in_specs=[HBM_SPEC] * (2 * n),
        out_shape=(*[sem] * (2 * n), *[pltpu.HBM(a.shape, a.dtype) for a in arrs],
                   *[pltpu.HBM(s, a.dtype) for s, a in zip(land_shapes, arrs)], jax.ShapeDtypeStruct((8, LANES), F32)),
        out_specs=(*[SEM_SPEC] * (2 * n), *[HBM_SPEC] * (2 * n), pl.BlockSpec(memory_space=pltpu.VMEM)),
        input_output_aliases={i: 2 * n + i for i in range(2 * n)},
        compiler_params=pltpu.CompilerParams(has_side_effects=DATAFLOW),
    )(*[pltpu.with_memory_space_constraint(a, pltpu.HBM) for a in arrs],
      *[pltpu.with_memory_space_constraint(lax.empty(s, a.dtype), pltpu.HBM) for s, a in zip(land_shapes, arrs)])
    return dict(sends=list(outs[:n]), recvs=list(outs[n:2 * n]), ins=list(outs[2 * n:3 * n]),
                lands=list(outs[3 * n:4 * n]), token=outs[-1], copies=len(rels))


def _push_wait(started, after, name):
    n = len(started["ins"])
    copies = started["copies"]

    def body(*refs):
        lands, sends, recvs = refs[n:2 * n], refs[2 * n:3 * n], refs[3 * n:4 * n]
        x, y, c = _position()
        for a in range(n):
            every = lands[a].at[pl.ds(0, copies)]
            drain = pltpu.make_async_remote_copy(src_ref=every, dst_ref=every, send_sem=sends[a], recv_sem=recvs[a],
                                                 device_id=(x, y, c), device_id_type=MESH)
            drain.wait_send()
            drain.wait_recv()

    both = started["ins"] + started["lands"]
    outs = pl.pallas_call(
        body, name=name,
        in_specs=[HBM_SPEC] * (2 * n) + [SEM_SPEC] * (2 * n) + [pl.BlockSpec(memory_space=pl.ANY)],
        out_shape=tuple(pltpu.HBM(a.shape, a.dtype) for a in both), out_specs=tuple([HBM_SPEC] * (2 * n)),
        input_output_aliases={i: i for i in range(2 * n)},
        compiler_params=pltpu.CompilerParams(has_side_effects=DATAFLOW),
    )(*both, *started["sends"], *started["recvs"], after)
    return list(outs[:n]), list(outs[n:])


def _sibling_swap(arr, name):
    def body(in_ref, out_ref, send, recv):
        x, y, c = _position()
        cpy = pltpu.make_async_remote_copy(src_ref=in_ref.at[1 - c], dst_ref=out_ref, send_sem=send, recv_sem=recv,
                                           device_id=(x, y, 1 - c), device_id_type=MESH)
        cpy.start()
        cpy.wait()

    hbm = pl.BlockSpec(memory_space=pl.ANY)
    return pl.pallas_call(
        body, name=name, in_specs=[hbm], out_specs=hbm,
        out_shape=jax.ShapeDtypeStruct(arr.shape[1:], arr.dtype),
        scratch_shapes=[pltpu.SemaphoreType.DMA, pltpu.SemaphoreType.DMA],
        compiler_params=pltpu.CompilerParams(has_side_effects=True),
    )(arr)


def _add_halves(core, arr, other, name):
    _, ns, r, c = arr.shape
    tr = min(r, 256)

    def body(core_ref, a_ref, o_ref, out_ref):
        out_ref[...] = _bf(a_ref[0].astype(F32) + o_ref[...].astype(F32))

    blk = pl.BlockSpec((1, tr, c), lambda s, i, core_ref: (s, i, 0))
    return pl.pallas_call(
        body, name=name,
        grid_spec=pltpu.PrefetchScalarGridSpec(
            num_scalar_prefetch=1, grid=(ns, r // tr),
            in_specs=[pl.BlockSpec((1, 1, tr, c), lambda s, i, core_ref: (core_ref[0], s, i, 0)), blk],
            out_specs=blk),
        out_shape=jax.ShapeDtypeStruct((ns, r, c), BF16),
        compiler_params=_cparams(("parallel", "parallel")),
    )(core, arr, other)


def _all_reduce_small(buf, name):
    rows = buf.shape[0]

    def body(in_ref, out_ref, slots, send, recv):
        x, y, c = _position()
        me = 4 * x + 2 * y + c
        slots[me] = in_ref[...]
        copies = []
        for rel in range(1, N_DEV):
            copies.append(pltpu.make_async_remote_copy(
                src_ref=in_ref, dst_ref=slots.at[me], send_sem=send.at[rel - 1], recv_sem=recv.at[rel - 1],
                device_id=_peer(x, y, c, rel), device_id_type=MESH))
        for cpy in copies:
            cpy.start()
        for cpy in copies:
            cpy.wait()
        tot = slots[0]
        for d in range(1, N_DEV):
            tot = tot + slots[d]
        out_ref[...] = tot

    return pl.pallas_call(
        body, name=name,
        out_shape=jax.ShapeDtypeStruct((rows, LANES), F32),
        in_specs=[pl.BlockSpec(memory_space=pltpu.VMEM)], out_specs=pl.BlockSpec(memory_space=pltpu.VMEM),
        scratch_shapes=[pltpu.VMEM((N_DEV, rows, LANES), F32), pltpu.SemaphoreType.DMA((7,)),
                        pltpu.SemaphoreType.DMA((7,))],
        compiler_params=pltpu.CompilerParams(has_side_effects=True),
    )(buf)


def _adam_math(g, w, m, v):
    m = ADAM_B1 * m + (1.0 - ADAM_B1) * g
    v = ADAM_B2 * v + (1.0 - ADAM_B2) * (g * g)
    m_hat = m / (1.0 - ADAM_B1 ** ADAM_STEP)
    v_hat = v / (1.0 - ADAM_B2 ** ADAM_STEP)
    delta = -ADAM_LR * (m_hat / (jnp.sqrt(v_hat) + ADAM_EPS) + ADAM_WD * w)
    return delta, m, v


def _adam_shard(me, parts, mine, w, m, v, name):
    r, c = w.shape
    tr = min(r, 128)
    n_slots = parts.shape[0]

    def body(me_ref, p_ref, own_ref, w_ref, m_ref, v_ref, g_ref, d_ref, nm_ref, nv_ref):
        own = own_ref[0].astype(F32)
        g = None
        for s in range(n_slots):
            term = jnp.where(me_ref[0] == s, own, p_ref[s].astype(F32))
            g = term if g is None else g + term
        d, nm, nv = _adam_math(g, w_ref[...], m_ref[...], v_ref[...])
        g_ref[...] = g
        d_ref[...] = d
        nm_ref[...] = nm
        nv_ref[...] = nv

    row = pl.BlockSpec((tr, c), lambda i, me_ref: (i, 0))
    o = jax.ShapeDtypeStruct((r, c), F32)
    return pl.pallas_call(
        body, name=name,
        grid_spec=pltpu.PrefetchScalarGridSpec(
            num_scalar_prefetch=1, grid=(r // tr,),
            in_specs=[pl.BlockSpec((n_slots, tr, c), lambda i, me_ref: (0, i, 0)),
                      pl.BlockSpec((1, tr, c), lambda i, me_ref: (me_ref[0], i, 0)), row, row, row],
            out_specs=[row] * 4),
        out_shape=[o] * 4,
        compiler_params=_cparams(("parallel",)),
    )(me, parts, mine, w, m, v)


def _adam_small(g, w, m, v):
    def body(g_ref, w_ref, m_ref, v_ref, d_ref, nm_ref, nv_ref):
        d, nm, nv = _adam_math(g_ref[...], w_ref[...], m_ref[...], v_ref[...])
        d_ref[...] = d
        nm_ref[...] = nm
        nv_ref[...] = nv

    o = jax.ShapeDtypeStruct(g.shape, F32)
    return pl.pallas_call(body, name="adam_small", out_shape=[o] * 3, compiler_params=_cparams())(g, w, m, v)


def _split_w_in(w_full):
    o = IN_OFF
    w_gdn = w_full[:, o["gq"]:o["ga"]]
    w_fox = w_full[:, o["fq"]:o["ff"]]
    w_gate = w_full[:, o["gate_a"]:o["end"]]
    w_small = jnp.concatenate([w_full[:, o["ga"]:o["fq"]], w_full[:, o["ff"]:o["gate_a"]],
                               jnp.zeros((w_full.shape[0], LANES - 24), w_full.dtype)], axis=1)
    return w_gdn, w_fox, w_gate, w_small


def _join_w_in(parts):
    small = parts[0]
    return jnp.concatenate(parts[1:5] + [small[:, 0:16]] + parts[5:8] + [small[:, 16:24]] + parts[8:10], axis=1)


def _rows128(a, rows):
    flat = a.reshape(-1)
    flat = jnp.concatenate([flat, jnp.zeros((rows * LANES - flat.shape[0],), flat.dtype)])
    return flat.reshape(rows, LANES)


def kernel(x, norm_mix_g, w_in, gdn_conv_w, gdn_a_log, gdn_dt_bias, gdn_norm_g, fox_q_norm_g, fox_k_norm_g, fox_f_bias, w_proj_gdn, w_proj_fox, w_out, norm_mlp_g, w_up, w_down, loss_target, m_norm_mix_g, m_w_in, m_gdn_conv_w, m_gdn_a_log, m_gdn_dt_bias, m_gdn_norm_g, m_fox_q_norm_g, m_fox_k_norm_g, m_fox_f_bias, m_w_proj_gdn, m_w_proj_fox, m_w_out, m_norm_mlp_g, m_w_up, m_w_down, v_norm_mix_g, v_w_in, v_gdn_conv_w, v_gdn_a_log, v_gdn_dt_bias, v_gdn_norm_g, v_fox_q_norm_g, v_fox_k_norm_g, v_fox_f_bias, v_w_proj_gdn, v_w_proj_fox, v_w_out, v_norm_mlp_g, v_w_up, v_w_down):
    b_loc, t, d = x.shape
    n = b_loc * t
    me = 4 * lax.axis_index("x") + 2 * lax.axis_index("y") + lax.axis_index("c")

    late_names = ["w_proj_gdn", "w_proj_fox", "w_out", "w_up", "w_down"]
    big16 = _to_bf16([w_in[0], w_proj_gdn[0], w_proj_fox[0], w_out[0], w_up[0], w_down[0]], "weights_to_bf16")
    g_in, g_conv = _all_gather([big16[0], gdn_conv_w[0]], "gather_w_in")
    late = _push_start(list(big16[1:]), False, "gather_late_start")
    w_full = g_in.transpose(1, 0, 2).reshape(d, N_DEV * w_in.shape[2])
    w_gdn, w_fox, w_gate, w_small = _split_w_in(w_full)
    weights = {
        "w_gdn": w_gdn, "w_fox": w_fox, "w_gate": w_gate, "w_small": w_small,
        "conv_w": g_conv.transpose(1, 0, 2).reshape(CONV_K, 3 * d),
        "norm_mix_g": norm_mix_g + late["token"][0:1, 0:1], "norm_mlp_g": norm_mlp_g, "a_log": gdn_a_log,
        "dt_bias": gdn_dt_bias, "gdn_norm_g": gdn_norm_g, "fox_q_norm_g": fox_q_norm_g, "fox_k_norm_g": fox_k_norm_g,
        "f_bias": fox_f_bias,
    }
    c_in, c_up = w_in.shape[2], w_up.shape[2]
    me1 = jnp.reshape(me, (1,)).astype(jnp.int32)
    chip1 = jnp.reshape(2 * lax.axis_index("x") + lax.axis_index("y"), (1,)).astype(jnp.int32)
    core1 = jnp.reshape(lax.axis_index("c"), (1,)).astype(jnp.int32)

    class _Exchange:
        def __init__(self):
            self.started = []

        def late_weights(self, after):
            shards, lands = _push_wait(late, after, "gather_late_wait")
            full = [lax.dynamic_update_index_in_dim(land, shard, me, 0) for land, shard in zip(lands, shards)]
            g_pa, g_pb, g_out, g_up, g_down = full
            return {"w_proj_gdn": g_pa.reshape(d, d), "w_proj_fox": g_pb.reshape(d, d), "w_out": g_out.reshape(d, d),
                    "w_up": g_up.transpose(1, 0, 2).reshape(d, D_FF), "w_down": g_down.reshape(D_FF, d)}

        def grads_ready(self, grads, tie):
            names = list(grads)
            if names == ["w_in_parts"]:
                halves = _join_w_in(grads["w_in_parts"]).reshape(d, 4, 2, c_in).transpose(2, 1, 0, 3)
                other = _sibling_swap(halves, "grads_w_in_sibling")
                pair = _add_halves(core1, halves, other, "grads_w_in_pair")
                st = _push_start([pair], True, "grads_start_w_in_parts", chips=True)
            else:
                layout = {"w_up": lambda a: a.reshape(d, N_DEV, c_up).transpose(1, 0, 2),
                          "w_down": lambda a: a.reshape(N_DEV, D_FF // N_DEV, d)}
                arrs = [layout.get(k, lambda a: a.reshape(N_DEV, d // N_DEV, d))(grads[k]) for k in names]
                st = _push_start(arrs, True, "grads_start_" + names[0])
            self.started.append((names, st))
            return tie + st["token"][0:1, 0:1]

    comm = _Exchange()
    loss_blk, grad_x, g = _local_step(x.reshape(n, d), loss_target.reshape(n, d), weights, b_loc, t, comm)

    shards = {"w_in_parts": (w_in, m_w_in, v_w_in), "w_proj_gdn": (w_proj_gdn, m_w_proj_gdn, v_w_proj_gdn),
              "w_proj_fox": (w_proj_fox, m_w_proj_fox, v_w_proj_fox), "w_out": (w_out, m_w_out, v_w_out),
              "w_up": (w_up, m_w_up, v_w_up), "w_down": (w_down, m_w_down, v_w_down)}
    adam = {}

    def finish(names, st, after):
        mine, parts = _push_wait(st, after, "grads_wait_" + names[0])
        slot = chip1 if st["copies"] == len(CHIP_RELS) else me1
        for k, own, part in zip(names, mine, parts):
            wi, mi, vi = shards[k]
            adam[k] = [r[None] for r in _adam_shard(slot, part, own, wi[0], mi[0], vi[0], "adam_" + k)]

    for names, st in comm.started[:-1]:
        finish(names, st, grad_x)

    conv_rows = CONV_K * 3 * d // LANES
    conv_g = g["conv"].transpose(1, 0, 2).reshape(conv_rows, LANES)
    buf = jnp.concatenate([conv_g, g["norm_mix_g"].reshape(8, LANES), g["norm_mlp_g"].reshape(8, LANES),
                           g["gdn_small"], g["fox_small"], loss_blk], axis=0)
    anchor = sum(adam[k][1][0, 0:1, 0:LANES] for names, _ in comm.started[:-1] for k in names) * 0.0
    tot = _all_reduce_small(buf + anchor, "all_reduce_small")
    finish(*comm.started[-1], tot)
    big_out = [adam[k] for k in ["w_in_parts"] + late_names]
    o = conv_rows
    conv_full = tot[0:o].reshape(CONV_K, 3 * d)
    c_conv = gdn_conv_w.shape[2]
    g_conv_shard = lax.dynamic_slice(conv_full, (0, me * c_conv), (CONV_K, c_conv))
    g_mix = tot[o:o + 8].reshape(1, d)
    g_mlp = tot[o + 8:o + 16].reshape(1, d)
    gs, fs = tot[o + 16:o + 24], tot[o + 24:o + 32]
    loss = tot[o + 32, 0]
    small_g = [g_mix, g_conv_shard[None], gs[0:1, 0:HEADS], gs[1:2, 0:HEADS], gs[2:3], fs[0:1], fs[1:2], fs[2:3, 0:HEADS],
               g_mlp]
    small_w = [norm_mix_g, gdn_conv_w, gdn_a_log, gdn_dt_bias, gdn_norm_g, fox_q_norm_g, fox_k_norm_g, fox_f_bias,
               norm_mlp_g]
    small_m = [m_norm_mix_g, m_gdn_conv_w, m_gdn_a_log, m_gdn_dt_bias, m_gdn_norm_g, m_fox_q_norm_g, m_fox_k_norm_g,
               m_fox_f_bias, m_norm_mlp_g]
    small_v = [v_norm_mix_g, v_gdn_conv_w, v_gdn_a_log, v_gdn_dt_bias, v_gdn_norm_g, v_fox_q_norm_g, v_fox_k_norm_g,
               v_fox_f_bias, v_norm_mlp_g]
    row_counts = [-(-a.size // (8 * LANES)) * 8 for a in small_w]

    def pack(arrs):
        return jnp.concatenate([_rows128(a, rc) for a, rc in zip(arrs, row_counts)], axis=0)

    sd, sm, sv = _adam_small(pack(small_g), pack(small_w), pack(small_m), pack(small_v))

    def unpack(p):
        outs, r0 = [], 0
        for a, rc in zip(small_w, row_counts):
            outs.append(p[r0:r0 + rc].reshape(-1)[:a.size].reshape(a.shape))
            r0 += rc
        return outs

    small_out = [small_g_i.reshape(w_i.shape) for small_g_i, w_i in zip(small_g, small_w)], unpack(sd), unpack(sm), unpack(sv)

    def ordered(kind):
        s = small_out[kind]
        bo = [b[kind] for b in big_out]
        return [s[0], bo[0], s[1], s[2], s[3], s[4], s[5], s[6], s[7], bo[1], bo[2], bo[3], s[8], bo[4], bo[5]]

    return (loss, grad_x.reshape(b_loc, t, d), *ordered(0), *ordered(1), *ordered(2), *ordered(3))
```

```python
import functools

import jax
import jax.numpy as jnp
from jax import lax
from jax.experimental import pallas as pl
from jax.experimental.pallas import tpu as pltpu

F32 = jnp.float32
BF16 = jnp.bfloat16
HI = lax.Precision.HIGHEST
MESH = pl.DeviceIdType.MESH

N_DEV = 8
D_MODEL = 1024
HEADS = 8
DH = 128
CONV_K = 4
CHUNK = 64
GDN_GROUP = 8
FOX_BLOCK = 128
FOX_TILE = 512
FOX_SHORT = 512
D_FF = 4 * D_MODEL
EPS = 1e-6
LANES = 128
NEG = -1e30
IN_OFF = {"gq": 0, "gk": 1024, "gv": 2048, "gz": 3072, "ga": 4096, "gb": 4104, "fq": 4112, "fk": 5136,
          "fv": 6160, "ff": 7184, "gate_a": 7192, "gate_b": 8216, "end": 9240}
LANE_GA, LANE_GB, LANE_FF = 0, 8, 16

ADAM_LR = 0.001
ADAM_B1 = 0.9
ADAM_B2 = 0.999
ADAM_EPS = 1e-08
ADAM_WD = 0.01
ADAM_STEP = 10

VMEM_LIMIT = 56 * 1024 * 1024


def _cparams(sem=None):
    return pltpu.CompilerParams(dimension_semantics=sem, vmem_limit_bytes=VMEM_LIMIT)


def _sigmoid(x):
    return 1.0 / (1.0 + jnp.exp(-x))


def _softplus(x):
    return jnp.maximum(x, 0.0) + jnp.log(1.0 + jnp.exp(-jnp.abs(x)))


def _dot(a, b, prec=None):
    return lax.dot_general(a, b, (((1,), (0,)), ((), ())), precision=prec, preferred_element_type=F32)


def _dot_nt(a, b, prec=None):
    return lax.dot_general(a, b, (((1,), (1,)), ((), ())), precision=prec, preferred_element_type=F32)


def _dot_tn(a, b, prec=None):
    return lax.dot_general(a, b, (((0,), (0,)), ((), ())), precision=prec, preferred_element_type=F32)


def _bf(x):
    return x.astype(BF16)


MM_TILE = 1024


def _mm(a, b, *, name, ta=False, tb=False, out_dtype=F32, epi=None, extras=(), out2=None,
        b_koff=0, tm=MM_TILE, tn=MM_TILE, tk=MM_TILE):
    m = a.shape[1] if ta else a.shape[0]
    kdim = a.shape[0] if ta else a.shape[1]
    n = b.shape[0] if tb else b.shape[1]
    tm, tn, tk = min(tm, m), min(tn, n), min(tk, kdim)
    nk = kdim // tk
    grid = (m // tm, n // tn, nk)
    koff = b_koff // tk
    a_spec = pl.BlockSpec((tk, tm), lambda i, j, k: (k, i)) if ta else pl.BlockSpec((tm, tk), lambda i, j, k: (i, k))
    if tb:
        b_spec = pl.BlockSpec((tn, tk), lambda i, j, k: (j, k + koff))
    else:
        b_spec = pl.BlockSpec((tk, tn), lambda i, j, k: (k + koff, j))
    o_spec = pl.BlockSpec((tm, tn), lambda i, j, k: (i, j))
    n_e = len(extras)
    n_o = 1 if out2 is None else 2
    dims = (((0 if ta else 1,), (1 if tb else 0,)), ((), ()))

    def body(a_ref, b_ref, *rest):
        e_refs, o_refs = rest[:n_e], rest[n_e:n_e + n_o]
        prod = lax.dot_general(_bf(a_ref[...]), _bf(b_ref[...]), dims, preferred_element_type=F32)

        def finish(r):
            if out2 is not None:
                o_refs[1][...] = out2[0](r).astype(out2[1])
            if epi is not None:
                r = epi(r, *[e[...] for e in e_refs])
            o_refs[0][...] = r.astype(out_dtype)

        if nk == 1:
            finish(prod)
        else:
            acc = rest[n_e + n_o]
            k = pl.program_id(2)

            @pl.when(k == 0)
            def _():
                acc[...] = prod

            @pl.when(k > 0)
            def _():
                acc[...] += prod

            @pl.when(k == nk - 1)
            def _():
                finish(acc[...])

    shapes = [jax.ShapeDtypeStruct((m, n), out_dtype)]
    if out2 is not None:
        shapes.append(jax.ShapeDtypeStruct((m, n), out2[1]))
    res = pl.pallas_call(
        body, name=name, grid=grid,
        in_specs=[a_spec, b_spec] + [o_spec] * n_e,
        out_specs=[o_spec] * n_o, out_shape=shapes,
        scratch_shapes=[] if nk == 1 else [pltpu.VMEM((tm, tn), F32)],
        compiler_params=_cparams(("parallel", "parallel", "arbitrary")),
    )(a, b, *extras)
    return res[0] if out2 is None else res


def _relu2(x):
    r = jnp.maximum(x, 0.0)
    return r * r


ROWS = 512


def _rms_fwd(x, g, name):
    n, d = x.shape

    def body(x_ref, g_ref, u_ref):
        xv = x_ref[...]
        r = lax.rsqrt(jnp.mean(xv * xv, axis=1, keepdims=True) + EPS)
        u_ref[...] = _bf(xv * r * g_ref[...])

    return pl.pallas_call(
        body, name=name, grid=(n // ROWS,),
        in_specs=[pl.BlockSpec((ROWS, d), lambda i: (i, 0)), pl.BlockSpec((1, d), lambda i: (0, 0))],
        out_specs=pl.BlockSpec((ROWS, d), lambda i: (i, 0)),
        out_shape=jax.ShapeDtypeStruct((n, d), BF16),
        compiler_params=_cparams(("parallel",)),
    )(x, g)


def _rms_bwd(dy, x, g, dres, name):
    n, d = x.shape

    def body(dy_ref, x_ref, g_ref, dres_ref, dx_ref, dx16_ref, dg_ref):
        i = pl.program_id(0)
        xv, dyv = x_ref[...], dy_ref[...]
        r = lax.rsqrt(jnp.mean(xv * xv, axis=1, keepdims=True) + EPS)
        gy = dyv * g_ref[...]
        s = jnp.sum(gy * xv, axis=1, keepdims=True)
        dx = dres_ref[...] + r * gy - xv * (r * r * r * (1.0 / d)) * s
        dx_ref[...] = dx
        dx16_ref[...] = _bf(dx)

        @pl.when(i == 0)
        def _():
            dg_ref[...] = jnp.zeros_like(dg_ref)

        dg_ref[...] += jnp.sum(dyv * xv * r, axis=0, keepdims=True)

    row = pl.BlockSpec((ROWS, d), lambda i: (i, 0))
    vec = pl.BlockSpec((1, d), lambda i: (0, 0))
    return pl.pallas_call(
        body, name=name, grid=(n // ROWS,),
        in_specs=[row, row, vec, row], out_specs=[row, row, vec],
        out_shape=[jax.ShapeDtypeStruct((n, d), F32), jax.ShapeDtypeStruct((n, d), BF16),
                   jax.ShapeDtypeStruct((1, d), F32)],
        compiler_params=_cparams(("arbitrary",)),
    )(dy, x, g, dres)


def _merge_fwd(ya, yb, gate):
    n, d = ya.shape

    def body(ya_ref, yb_ref, ga_ref, gb_ref, o_ref):
        o_ref[...] = _bf(_sigmoid(ga_ref[...]) * ya_ref[...] + _sigmoid(gb_ref[...]) * yb_ref[...])

    row = pl.BlockSpec((ROWS, d), lambda i: (i, 0))
    return pl.pallas_call(
        body, name="merge_fwd", grid=(n // ROWS,),
        in_specs=[row, row, row, pl.BlockSpec((ROWS, d), lambda i: (i, 1))], out_specs=row,
        out_shape=jax.ShapeDtypeStruct((n, d), BF16),
        compiler_params=_cparams(("parallel",)),
    )(ya, yb, gate, gate)


def _merge_bwd(dm, ya, yb, gate):
    n, d = ya.shape

    def body(dm_ref, ya_ref, yb_ref, ga_ref, gb_ref, dya_ref, dyb_ref, dga_ref, dgb_ref):
        dmv = dm_ref[...]
        sa, sb = _sigmoid(ga_ref[...]), _sigmoid(gb_ref[...])
        dya_ref[...] = _bf(dmv * sa)
        dyb_ref[...] = _bf(dmv * sb)
        dga_ref[...] = _bf(dmv * ya_ref[...] * sa * (1.0 - sa))
        dgb_ref[...] = _bf(dmv * yb_ref[...] * sb * (1.0 - sb))

    row = pl.BlockSpec((ROWS, d), lambda i: (i, 0))
    o = jax.ShapeDtypeStruct((n, d), BF16)
    return pl.pallas_call(
        body, name="merge_bwd", grid=(n // ROWS,),
        in_specs=[row, row, row, row, pl.BlockSpec((ROWS, d), lambda i: (i, 1))], out_specs=[row] * 4,
        out_shape=[o] * 4,
        compiler_params=_cparams(("parallel",)),
    )(dm, ya, yb, gate, gate)


def _loss_bwd(out, target):
    n, d = out.shape

    def body(o_ref, t_ref, d_ref, d16_ref, l_ref):
        i = pl.program_id(0)
        err = o_ref[...] - t_ref[...]
        d_ref[...] = err * (1.0 / d)
        d16_ref[...] = _bf(err * (1.0 / d))

        @pl.when(i == 0)
        def _():
            l_ref[...] = jnp.zeros_like(l_ref)

        l_ref[...] += 0.5 * jnp.sum(jnp.mean(err * err, axis=1, keepdims=True), axis=0, keepdims=True)

    row = pl.BlockSpec((ROWS, d), lambda i: (i, 0))
    return pl.pallas_call(
        body, name="loss_bwd", grid=(n // ROWS,),
        in_specs=[row, row], out_specs=[row, row, pl.BlockSpec((8, LANES), lambda i: (0, 0))],
        out_shape=[jax.ShapeDtypeStruct((n, d), F32), jax.ShapeDtypeStruct((n, d), BF16),
                   jax.ShapeDtypeStruct((8, LANES), F32)],
        compiler_params=_cparams(("arbitrary",)),
    )(out, target)


PAD = 8


def _pad_zero(pad_ref):
    t = pad_ref.shape[0] - 2 * PAD
    pad_ref[0:PAD, :] = jnp.zeros((PAD, LANES), F32)
    pad_ref[PAD + t:2 * PAD + t, :] = jnp.zeros((PAD, LANES), F32)


def _shifted(pad_ref, s):
    t = pad_ref.shape[0] - 2 * PAD
    return pad_ref[PAD - s:PAD - s + t, :]


def _conv(x, w_ref, pad_ref):
    t = x.shape[0]
    pad_ref[PAD:PAD + t, :] = x
    y = _shifted(pad_ref, 3) * w_ref[0:1, :]
    y = y + _shifted(pad_ref, 2) * w_ref[1:2, :]
    y = y + _shifted(pad_ref, 1) * w_ref[2:3, :]
    return y + x * w_ref[3:4, :]


def _chunk_consts():
    r = lax.broadcasted_iota(jnp.int32, (CHUNK, CHUNK), 0)
    c = lax.broadcasted_iota(jnp.int32, (CHUNK, CHUNK), 1)
    incl, strict = r >= c, r > c
    return dict(incl=incl, strict=strict, trilf=incl.astype(F32), triuf=(r <= c).astype(F32),
                eye=(r == c).astype(F32))


class _V:
    def __init__(self, xs):
        self.xs = list(xs)

    def __add__(self, o):
        return _ap(lambda x, y: x + y, self, o)

    def __radd__(self, o):
        return _ap(lambda x, y: y + x, self, o)

    def __sub__(self, o):
        return _ap(lambda x, y: x - y, self, o)

    def __rsub__(self, o):
        return _ap(lambda x, y: y - x, self, o)

    def __mul__(self, o):
        return _ap(lambda x, y: x * y, self, o)

    def __rmul__(self, o):
        return _ap(lambda x, y: y * x, self, o)

    def __neg__(self):
        return _ap(lambda x: -x, self)

    def __getitem__(self, idx):
        return _ap(lambda x: x[idx], self)


def _ap(fn, *args):
    n = [len(a.xs) for a in args if isinstance(a, _V)]
    if not n:
        return fn(*args)
    return _V([fn(*[a.xs[i] if isinstance(a, _V) else a for a in args]) for i in range(n[0])])


def _vbf(x):
    return _ap(_bf, x)


def _vdot(a, b):
    return _ap(_dot, a, b)


def _vdot_nt(a, b):
    return _ap(_dot_nt, a, b)


def _vdot_tn(a, b):
    return _ap(_dot_tn, a, b)


def _vexp(x):
    return _ap(jnp.exp, x)


def _vsum(x, axis):
    return _ap(lambda v: jnp.sum(v, axis=axis, keepdims=True), x)


def _vcat(a, b, axis):
    return _ap(lambda x, y: jnp.concatenate([x, y], axis=axis), a, b)


def _vmask(mask, x):
    return _ap(lambda v: jnp.where(mask, v, 0.0), x)


def _split2(x):
    h = _vbf(x)
    return h, _vbf(x - _ap(lambda v: v.astype(F32), h))


def _dot3(a, b, kind=_vdot):
    ah, al = _split2(a)
    bh, bl = _split2(b)
    return kind(ah, bh) + (kind(ah, bl) + kind(al, bh))


def _split(x, terms):
    out = []
    for _ in range(terms):
        h = _vbf(x)
        out.append(h)
        x = x - _ap(lambda v: v.astype(F32), h)
    return out


def _dot_exact_l(m01, x, kind=_vdot, terms=2):
    mb = _bf(m01)
    parts = [kind(mb, xp) for xp in _split(x, terms)]
    return functools.reduce(lambda a, b: a + b, reversed(parts))


def _dot_exact_r(x, m01, kind=_vdot, terms=2):
    mb = _bf(m01)
    parts = [kind(xp, mb) for xp in _split(x, terms)]
    return functools.reduce(lambda a, b: a + b, reversed(parts))


def _inv_unit_lower(a, eye):
    p = -a
    r = p + eye
    p = _dot3(p, p)
    for j in range(1, 6):
        if j < 5:
            y = _dot3(p, _vcat(p, r, 1))
            p, r = y[:, 0:CHUNK], r + y[:, CHUNK:2 * CHUNK]
        else:
            r = r + _dot3(p, r)
    return r


def _gdn_chunk_pre(q, k, v, g128, g64, b128, b64, cs):
    incl = cs["incl"]
    big_g = _dot_exact_l(cs["trilf"], g128)
    gc = big_g[:, 0:CHUNK]
    gr = _dot_exact_r(g64, cs["triuf"], _vdot_tn)
    decay = _ap(lambda d: jnp.where(incl, jnp.exp(jnp.where(incl, d, 0.0)), 0.0), gc - gr)
    kb, qb = _vbf(k), _vbf(q)
    qkk = _vdot_nt(_vcat(qb, kb, 0), kb)
    qk, kk = qkk[0:CHUNK], qkk[CHUNK:2 * CHUNK]
    tm = _inv_unit_lower(_vmask(cs["strict"], b64 * kk * decay), cs["eye"])
    e_g = _vexp(big_g)
    wu = _dot3(tm, _vcat(v * b128, k * (b128 * e_g), 1))
    w, u = wu[:, 0:DH], wu[:, DH:2 * DH]
    g_last = _vsum(g128, 0)
    return dict(big_g=big_g, decay=decay, kk=kk, qk=qk, tm=tm, w=w, u=u, p=qk * decay, q_dec=q * e_g,
                k_dec=k * _vexp(g_last - big_g), dec=_vexp(g_last))


def _gdn_chunk_post(q, k, v, g128, b128, b64, s, ds_next, do, dv_new, big_g, decay, kk, qk, tm, u, v_new, cs):
    e_g = _vexp(big_g)
    vb = v * b128
    kbeta = k * (b128 * e_g)
    q_dec = q * e_g
    g_last = _vsum(g128, 0)
    ekg = _vexp(g_last - big_g)
    k_dec = k * ekg
    dec = _vexp(g_last)
    kb, qb, sb = _vbf(k), _vbf(q), _vbf(s)
    dob, dsb, vnb, dvnb = _vbf(do), _vbf(ds_next), _vbf(v_new), _vbf(dv_new)
    dp = _vmask(cs["incl"], _vdot_nt(dob, vnb))
    dq_dec = _vdot_nt(dob, sb)
    du = -_vdot_nt(dvnb, sb)
    ddec = _vsum(_vsum(s * ds_next, 1), 0)
    dk_dec = _vdot_nt(vnb, dsb)
    dwu = _vcat(dv_new, du, 1)
    dt = _dot3(dwu, _vcat(vb, kbeta, 1), _vdot_nt)
    dvk = _dot3(tm, dwu, _vdot_tn)
    dvb, dkbeta = dvk[:, 0:DH], dvk[:, DH:2 * DH]
    da = _vmask(cs["strict"], -_dot3(tm, _dot3(dt, tm, _vdot_nt), _vdot_tn))
    dkk = _vbf(da * b64 * decay)
    dqk = _vbf(dp * decay)
    ddd = (da * b64 * kk + dp * qk) * decay
    dq = _vdot(dqk, kb) + dq_dec * e_g
    dk = _vdot_tn(dqk, qb) + _vdot(dkk, kb) + _vdot_tn(dkk, kb) + dk_dec * ekg + dkbeta * (b128 * e_g)
    dv = dvb * b128
    dbeta = _vsum(da * kk * decay, 1) + _vsum(dvb * v, 1) + _vsum(dkbeta * k * e_g, 1)
    s_k = _vsum(dk_dec * k_dec, 1)
    dg_col = _vsum(ddd, 1) + _vsum(dq_dec * q_dec, 1) - s_k + _vsum(dkbeta * kbeta, 1)
    colsum = _dot_exact_r(ddd, jnp.ones((CHUNK, LANES), F32), _vdot_tn)
    dg_last = _vsum(s_k, 0) + ddec * dec
    dg = _dot_exact_l(cs["triuf"], dg_col - colsum) + dg_last
    return dq, dk, dv, dg, dbeta


def _stack_rows(vecs, nrows):
    row = lax.broadcasted_iota(jnp.int32, (nrows, LANES), 0)
    out = jnp.zeros((nrows, LANES), F32)
    for i, v in enumerate(vecs):
        out = out + jnp.where(row == i, jnp.broadcast_to(v, (nrows, LANES)), 0.0)
    return out


def _head_lane(x, lane_idx):
    lane = lax.broadcasted_iota(jnp.int32, x.shape, 1)
    return jnp.sum(jnp.where(lane == lane_idx, x, 0.0), axis=1, keepdims=True)


def _gdn_gates(ps, h, alog_ref, dtb_ref):
    ga = _head_lane(ps, LANE_GA + h)
    gb = _head_lane(ps, LANE_GB + h)
    a = jnp.exp(jnp.full((1, 1), alog_ref[0, h], F32))
    sp_in = ga + dtb_ref[0, h]
    g = -a * _softplus(sp_in)
    return g, _sigmoid(gb), a, sp_in


def _gdn_specs(b_loc, t):
    def col(off):
        return pl.BlockSpec((t, DH), lambda b, h: (b, off + h))

    ps_spec = pl.BlockSpec((t, LANES), lambda b, h: (b, 0))

    def wcol(off):
        return pl.BlockSpec((CONV_K, DH), lambda b, h: (0, off + h))

    smem = pl.BlockSpec(memory_space=pltpu.SMEM)
    vec = pl.BlockSpec((1, DH), lambda b, h: (0, 0))
    return col, ps_spec, wcol, smem, vec


def _gdn_fwd(pg, ps, convw, a_log, dt_bias, gnorm, b_loc, t):
    n = b_loc * t
    nc = t // CHUNK
    col, ps_spec, wcol, smem, vec = _gdn_specs(b_loc, t)

    def body(q_ref, k_ref, v_ref, z_ref, ps_ref, wq_ref, wk_ref, wv_ref, alog_ref, dtb_ref, gn_ref,
             oa_ref, oraw_ref, s_ref, qn, kn, vv, g128, g64, b128, b64, uq_s, p_s, kd_s, dec_s, pad_s):
        h = pl.program_id(1)
        g, beta, _, _ = _gdn_gates(ps_ref[...], h, alog_ref, dtb_ref)
        g128[...] = jnp.broadcast_to(g, (t, LANES))
        g64[...] = jnp.broadcast_to(g, (t, CHUNK))
        b128[...] = jnp.broadcast_to(beta, (t, LANES))
        b64[...] = jnp.broadcast_to(beta, (t, CHUNK))
        _pad_zero(pad_s)
        pq = _conv(q_ref[...], wq_ref, pad_s)
        yq = pq * _sigmoid(pq)
        qn[...] = yq * (lax.rsqrt(jnp.sum(yq * yq, axis=1, keepdims=True) + EPS) * (DH ** -0.5))
        pk = _conv(k_ref[...], wk_ref, pad_s)
        yk = pk * _sigmoid(pk)
        kn[...] = yk * lax.rsqrt(jnp.sum(yk * yk, axis=1, keepdims=True) + EPS)
        pv = _conv(v_ref[...], wv_ref, pad_s)
        vv[...] = pv * _sigmoid(pv)
        cs = _chunk_consts()

        def pre_group(gi, _):
            idx = [gi * GDN_GROUP + c for c in range(GDN_GROUP)]
            rows = [pl.ds(pl.multiple_of(i * CHUNK, CHUNK), CHUNK) for i in idx]
            ins = [_V([ref[r, :] for r in rows]) for ref in (qn, kn, vv, g128, g64, b128, b64)]
            f = _gdn_chunk_pre(*ins, cs)
            for c, (i, r) in enumerate(zip(idx, rows)):
                vv[r, :] = f["w"].xs[c]
                uq_s[i, 0:CHUNK, :] = _bf(f["u"].xs[c])
                uq_s[i, CHUNK:2 * CHUNK, :] = _bf(f["q_dec"].xs[c])
                p_s[r, :] = _bf(f["p"].xs[c])
                kd_s[r, :] = _bf(f["k_dec"].xs[c])
                dec_s[pl.ds(pl.multiple_of(i * 8, 8), 8), :] = jnp.broadcast_to(f["dec"].xs[c], (8, LANES))
            return 0

        lax.fori_loop(0, nc // GDN_GROUP, pre_group, 0)

        def chunk(i, s):
            r = pl.ds(pl.multiple_of(i * CHUNK, CHUNK), CHUNK)
            us = _dot(uq_s[i], _bf(s))
            vnb = _bf(vv[r, :] - us[0:CHUNK])
            oraw_ref[r, :] = us[CHUNK:2 * CHUNK] + _dot(p_s[r, :], vnb)
            s_ref[0, 0, i] = s
            return s * dec_s[pl.ds(pl.multiple_of(i * 8, 8), 1), :] + _dot_tn(kd_s[r, :], vnb)

        lax.fori_loop(0, nc, chunk, jnp.zeros((DH, DH), F32))
        o = oraw_ref[...]
        rr = lax.rsqrt(jnp.mean(o * o, axis=1, keepdims=True) + EPS)
        z = z_ref[...]
        oa_ref[...] = _bf((o * rr * gn_ref[...]) * (z * _sigmoid(z)))

    return pl.pallas_call(
        body, name="gdn_fwd", grid=(b_loc, HEADS),
        in_specs=[col(0), col(HEADS), col(2 * HEADS), col(3 * HEADS), ps_spec, wcol(0), wcol(HEADS), wcol(2 * HEADS),
                  smem, smem, vec],
        out_specs=[pl.BlockSpec((t, DH), lambda b, h: (b, h)), pl.BlockSpec((t, DH), lambda b, h: (b, h)),
                   pl.BlockSpec((1, 1, nc, DH, DH), lambda b, h: (b, h, 0, 0, 0))],
        out_shape=[jax.ShapeDtypeStruct((n, HEADS * DH), BF16), jax.ShapeDtypeStruct((n, HEADS * DH), F32),
                   jax.ShapeDtypeStruct((b_loc, HEADS, nc, DH, DH), F32)],
        scratch_shapes=([pltpu.VMEM((t, DH), F32)] * 3 + [pltpu.VMEM((t, LANES), F32), pltpu.VMEM((t, CHUNK), F32)] * 2
                        + [pltpu.VMEM((nc, 2 * CHUNK, DH), BF16), pltpu.VMEM((t, CHUNK), BF16), pltpu.VMEM((t, DH), BF16),
                           pltpu.VMEM((8 * nc, LANES), F32), pltpu.VMEM((t + 2 * PAD, LANES), F32)]),
        compiler_params=_cparams(("arbitrary", "arbitrary")),
    )(pg, pg, pg, pg, ps, convw, convw, convw, a_log, dt_bias, gnorm)


def _gdn_bwd(pg, ps, convw, a_log, dt_bias, gnorm, d_oa, o_raw, s_all, b_loc, t):
    n = b_loc * t
    nc = t // CHUNK
    col, ps_spec, wcol, smem, vec = _gdn_specs(b_loc, t)

    def body(q_ref, k_ref, v_ref, z_ref, ps_ref, wq_ref, wk_ref, wv_ref, alog_ref, dtb_ref, gn_ref,
             doa_ref, oraw_ref, s_ref,
             dq_ref, dk_ref, dv_ref, dz_ref, dps_ref, dcw_ref, dsm_ref,
             qn, kn, vv, g128, g64, b128, b64, do_s, bg_s, u_s, vn_s, dvn_s, dcy_s, kk_s, qk_s, tm_s, dsn_s, pad_s):
        b, h = pl.program_id(0), pl.program_id(1)
        g, beta, _, _ = _gdn_gates(ps_ref[...], h, alog_ref, dtb_ref)
        g128[...] = jnp.broadcast_to(g, (t, LANES))
        g64[...] = jnp.broadcast_to(g, (t, CHUNK))
        b128[...] = jnp.broadcast_to(beta, (t, LANES))
        b64[...] = jnp.broadcast_to(beta, (t, CHUNK))
        _pad_zero(pad_s)

        def prep(x_ref, w_ref):
            p = _conv(x_ref[...], w_ref, pad_s)
            sg = _sigmoid(p)
            return p, sg, p * sg

        _, _, yq = prep(q_ref, wq_ref)
        qn[...] = yq * (lax.rsqrt(jnp.sum(yq * yq, axis=1, keepdims=True) + EPS) * (DH ** -0.5))
        _, _, yk = prep(k_ref, wk_ref)
        kn[...] = yk * lax.rsqrt(jnp.sum(yk * yk, axis=1, keepdims=True) + EPS)
        _, _, yv = prep(v_ref, wv_ref)
        vv[...] = yv

        o = oraw_ref[...]
        z = z_ref[...]
        doa = doa_ref[...]
        gn = gn_ref[...]
        ro = lax.rsqrt(jnp.mean(o * o, axis=1, keepdims=True) + EPS)
        sz = _sigmoid(z)
        dz_ref[...] = _bf(doa * (o * ro * gn) * (sz * (1.0 + z * (1.0 - sz))))
        dn = doa * (z * sz)
        dgn = jnp.sum(dn * o * ro, axis=0, keepdims=True)
        gy = dn * gn
        do_s[...] = ro * gy - o * (ro * ro * ro * (1.0 / DH)) * jnp.sum(gy * o, axis=1, keepdims=True)

        cs = _chunk_consts()

        def pre_group(gi, _):
            idx = [gi * GDN_GROUP + c for c in range(GDN_GROUP)]
            rows = [pl.ds(pl.multiple_of(i * CHUNK, CHUNK), CHUNK) for i in idx]
            ins = [_V([ref[r, :] for r in rows]) for ref in (qn, kn, vv, g128, g64, b128, b64)]
            states = _V([_bf(s_ref[0, 0, i]) for i in idx])
            f = _gdn_chunk_pre(*ins, cs)
            v_new = f["w"] - _vdot(_vbf(f["u"]), states)
            for c, r in enumerate(rows):
                bg_s[r, :] = f["big_g"].xs[c]
                u_s[r, :] = f["u"].xs[c]
                vn_s[r, :] = v_new.xs[c]
                dcy_s[r, :] = f["decay"].xs[c]
                kk_s[r, :] = f["kk"].xs[c]
                qk_s[r, :] = f["qk"].xs[c]
                tm_s[r, :] = f["tm"].xs[c]
            return 0

        lax.fori_loop(0, nc // GDN_GROUP, pre_group, 0)

        def chunk(j, ds):
            i = nc - 1 - j
            r = pl.ds(pl.multiple_of(i * CHUNK, CHUNK), CHUNK)
            big_g = bg_s[r, :]
            g_last = jnp.sum(g128[r, :], axis=0, keepdims=True)
            dob = _bf(do_s[r, :])
            dv_new = (_dot_tn(_bf(qk_s[r, :] * dcy_s[r, :]), dob)
                      + _dot(_bf(kn[r, :] * jnp.exp(g_last - big_g)), _bf(ds)))
            dvn_s[r, :] = dv_new
            dsn_s[i] = ds
            return (_dot_tn(_bf(qn[r, :] * jnp.exp(big_g)), dob) + jnp.exp(g_last) * ds
                    - _dot_tn(_bf(u_s[r, :]), _bf(dv_new)))

        lax.fori_loop(0, nc, chunk, jnp.zeros((DH, DH), F32))

        def post_group(gi, _):
            idx = [gi * GDN_GROUP + c for c in range(GDN_GROUP)]
            rows = [pl.ds(pl.multiple_of(i * CHUNK, CHUNK), CHUNK) for i in idx]
            def rows_of(ref):
                return _V([ref[r, :] for r in rows])

            dq, dk, dv, dg, dbeta = _gdn_chunk_post(
                rows_of(qn), rows_of(kn), rows_of(vv), rows_of(g128), rows_of(b128), rows_of(b64),
                _V([s_ref[0, 0, i] for i in idx]), _V([dsn_s[i] for i in idx]), rows_of(do_s), rows_of(dvn_s),
                rows_of(bg_s), rows_of(dcy_s), rows_of(kk_s), rows_of(qk_s), rows_of(tm_s), rows_of(u_s), rows_of(vn_s),
                cs)
            for c, r in enumerate(rows):
                qn[r, :] = dq.xs[c]
                kn[r, :] = dk.xs[c]
                vv[r, :] = dv.xs[c]
                g128[r, :] = dg.xs[c]
                b128[r, :] = jnp.broadcast_to(dbeta.xs[c], (CHUNK, LANES))
            return 0

        lax.fori_loop(0, nc // GDN_GROUP, post_group, 0)
        dqh, dkh, dvh = qn, kn, vv

        g, beta, a, sp_in = _gdn_gates(ps_ref[...], h, alog_ref, dtb_ref)
        dg = g128[...]
        d_ga = dg * (-a) * _sigmoid(sp_in)
        d_alog = jnp.sum(dg * g, axis=0, keepdims=True)
        d_dtb = jnp.sum(d_ga, axis=0, keepdims=True)
        d_gb = b128[...] * (beta * (1.0 - beta))
        lane = lax.broadcasted_iota(jnp.int32, (t, LANES), 1)
        contrib = jnp.where(lane == LANE_GA + h, d_ga, 0.0) + jnp.where(lane == LANE_GB + h, d_gb, 0.0)

        @pl.when(h == 0)
        def _():
            dps_ref[...] = jnp.zeros_like(dps_ref)

        dps_ref[...] += contrib

        lane1 = lax.broadcasted_iota(jnp.int32, (1, LANES), 1)
        small = _stack_rows([jnp.where(lane1 == h, d_alog, 0.0), jnp.where(lane1 == h, d_dtb, 0.0), dgn], 8)

        @pl.when((b == 0) & (h == 0))
        def _():
            dsm_ref[...] = jnp.zeros_like(dsm_ref)
            dcw_ref[...] = jnp.zeros_like(dcw_ref)

        dsm_ref[...] += small

        def conv_bwd(dp, x, w_ref, slot):
            dw = _stack_rows([jnp.sum(dp * _shifted(pad_s, 3), axis=0, keepdims=True),
                              jnp.sum(dp * _shifted(pad_s, 2), axis=0, keepdims=True),
                              jnp.sum(dp * _shifted(pad_s, 1), axis=0, keepdims=True),
                              jnp.sum(dp * x, axis=0, keepdims=True)], CONV_K)
            dcw_ref[slot] += dw
            pad_s[PAD:PAD + t, :] = dp
            dx = _shifted(pad_s, -3) * w_ref[0:1, :]
            dx = dx + _shifted(pad_s, -2) * w_ref[1:2, :]
            dx = dx + _shifted(pad_s, -1) * w_ref[2:3, :]
            return dx + dp * w_ref[3:4, :]

        def l2_bwd(dqn, y, c):
            r = lax.rsqrt(jnp.sum(y * y, axis=1, keepdims=True) + EPS)
            s1 = jnp.sum(dqn * y, axis=1, keepdims=True)
            return c * r * dqn - (c * r * r * r) * s1 * y

        def silu_bwd(p, sg):
            return sg * (1.0 + p * (1.0 - sg))

        pq, sq, yq = prep(q_ref, wq_ref)
        dq_ref[...] = _bf(conv_bwd(l2_bwd(dqh[...], yq, DH ** -0.5) * silu_bwd(pq, sq), q_ref[...], wq_ref, h))
        pk, sk, yk = prep(k_ref, wk_ref)
        dk_ref[...] = _bf(conv_bwd(l2_bwd(dkh[...], yk, 1.0) * silu_bwd(pk, sk), k_ref[...], wk_ref, HEADS + h))
        pv, sv, _ = prep(v_ref, wv_ref)
        dv_ref[...] = _bf(conv_bwd(dvh[...] * silu_bwd(pv, sv), v_ref[...], wv_ref, 2 * HEADS + h))

    blk = pl.BlockSpec((t, DH), lambda b, h: (b, h))
    ob = jax.ShapeDtypeStruct((n, HEADS * DH), BF16)
    return pl.pallas_call(
        body, name="gdn_bwd", grid=(b_loc, HEADS),
        in_specs=[col(0), col(HEADS), col(2 * HEADS), col(3 * HEADS), ps_spec, wcol(0), wcol(HEADS), wcol(2 * HEADS),
                  smem, smem, vec, blk, blk, pl.BlockSpec((1, 1, nc, DH, DH), lambda b, h: (b, h, 0, 0, 0))],
        out_specs=[blk, blk, blk, blk, ps_spec,
                   pl.BlockSpec((3 * HEADS, CONV_K, DH), lambda b, h: (0, 0, 0)),
                   pl.BlockSpec((8, LANES), lambda b, h: (0, 0))],
        out_shape=[ob, ob, ob, ob, jax.ShapeDtypeStruct((n, LANES), F32),
                   jax.ShapeDtypeStruct((3 * HEADS, CONV_K, DH), F32), jax.ShapeDtypeStruct((8, LANES), F32)],
        scratch_shapes=([pltpu.VMEM((t, DH), F32)] * 3 + [pltpu.VMEM((t, LANES), F32), pltpu.VMEM((t, CHUNK), F32)] * 2
                        + [pltpu.VMEM((t, DH), F32)] * 5 + [pltpu.VMEM((t, CHUNK), F32)] * 4
                        + [pltpu.VMEM((nc, DH, DH), F32), pltpu.VMEM((t + 2 * PAD, LANES), F32)]),
        compiler_params=_cparams(("arbitrary", "arbitrary")),
    )(pg, pg, pg, pg, ps, convw, convw, convw, a_log, dt_bias, gnorm, d_oa, o_raw, s_all)


def _fox_prologue(q_ref, k_ref, v_ref, ps_ref, fb_ref, gq_ref, gk_ref, h, t, qs, ks, vs, ccol, crow):
    nb = t // FOX_BLOCK
    q, k = q_ref[...], k_ref[...]
    rq = lax.rsqrt(jnp.mean(q * q, axis=1, keepdims=True) + EPS)
    rk = lax.rsqrt(jnp.mean(k * k, axis=1, keepdims=True) + EPS)
    qs[...] = _bf(q * rq * gq_ref[...])
    ks[...] = _bf(k * rk * gk_ref[...])
    vs[...] = _bf(v_ref[...])
    f_in = _head_lane(ps_ref[...], LANE_FF + h) + fb_ref[0, h]
    ccol[...] = jnp.broadcast_to(-_softplus(-f_in), (t, LANES))
    r = lax.broadcasted_iota(jnp.int32, (FOX_BLOCK, FOX_BLOCK), 0)
    c = lax.broadcasted_iota(jnp.int32, (FOX_BLOCK, FOX_BLOCK), 1)
    trilf, triuf = (r >= c).astype(F32), (r <= c).astype(F32)
    blocks = [pl.ds(j * FOX_BLOCK, FOX_BLOCK) for j in range(nb)]
    lfs = _V([ccol[rb, :] for rb in blocks])
    cc = _dot_exact_l(trilf, lfs, terms=3)
    cr = _dot_exact_r(lfs, triuf, _vdot_tn, terms=3)
    sums = _vsum(lfs, 0)
    carry = jnp.zeros((1, LANES), F32)
    for j, rb in enumerate(blocks):
        ccol[rb, :] = cc.xs[j] + carry
        crow[j] = (cr.xs[j] + carry)[0:8]
        carry = carry + sums.xs[j]
    return rq, rk, f_in


def _fox_scores(q_rows, k_rows, cc, cr, row0, col0):
    s = _dot_nt(q_rows, k_rows) * (DH ** -0.5) + cc - cr
    r = lax.broadcasted_iota(jnp.int32, s.shape, 0)
    c = lax.broadcasted_iota(jnp.int32, s.shape, 1)
    return jnp.where(row0 + r >= col0 + c, s, NEG)


def _fox_specs(t):
    def col(off):
        return pl.BlockSpec((t, DH), lambda b, h: (b, off + h))

    ps_spec = pl.BlockSpec((t, LANES), lambda b, h: (b, 0))
    smem = pl.BlockSpec(memory_space=pltpu.SMEM)
    vec = pl.BlockSpec((1, DH), lambda b, h: (0, 0))
    blk = pl.BlockSpec((t, DH), lambda b, h: (b, h))
    return col, ps_spec, smem, vec, blk


def _fox_fwd(pf, ps, f_bias, gq, gk, b_loc, t):
    n = b_loc * t
    nb = t // FOX_BLOCK
    kt = min(FOX_TILE, t)
    nsub = kt // FOX_BLOCK
    col, ps_spec, smem, vec, blk = _fox_specs(t)

    def body(q_ref, k_ref, v_ref, ps_ref, fb_ref, gq_ref, gk_ref, o_ref, lse_ref, qs, ks, vs, ccol, crow):
        h = pl.program_id(1)
        _fox_prologue(q_ref, k_ref, v_ref, ps_ref, fb_ref, gq_ref, gk_ref, h, t, qs, ks, vs, ccol, crow)

        def qblock(i, _):
            ri = pl.ds(pl.multiple_of(i * FOX_SHORT, FOX_SHORT), FOX_SHORT)
            qi = qs[ri, :]
            cc = jnp.concatenate([ccol[ri, :]] * nsub, axis=1)

            def ktile(j, carry):
                m, l, acc = carry
                rj = pl.ds(pl.multiple_of(j * kt, kt), kt)
                cr = jnp.concatenate([crow[j * nsub + u, 0:1, :] for u in range(nsub)], axis=1)
                s = _fox_scores(qi, ks[rj, :], cc, cr, i * FOX_SHORT, j * kt)
                m_new = jnp.maximum(m, jnp.max(s, axis=1, keepdims=True))
                p = jnp.exp(s - m_new)
                alpha = jnp.exp(m - m_new)
                l = alpha * l + jnp.sum(p, axis=1, keepdims=True)
                acc = alpha * acc + _dot(_bf(p), vs[rj, :])
                return m_new, l, acc

            m, l, acc = lax.fori_loop(0, (i * FOX_SHORT) // kt + 1, ktile, (jnp.full((FOX_SHORT, 1), NEG, F32),
                                                                            jnp.zeros((FOX_SHORT, 1), F32),
                                                                            jnp.zeros((FOX_SHORT, DH), F32)))
            o_ref[ri, :] = acc / l
            lse_ref[ri, :] = jnp.broadcast_to(m + jnp.log(l), (FOX_SHORT, LANES))
            return 0

        lax.fori_loop(0, t // FOX_SHORT, qblock, 0)

    o = jax.ShapeDtypeStruct((n, HEADS * DH), F32)
    return pl.pallas_call(
        body, name="fox_fwd", grid=(b_loc, HEADS),
        in_specs=[col(0), col(HEADS), col(2 * HEADS), ps_spec, smem, vec, vec],
        out_specs=[blk, blk], out_shape=[o, o],
        scratch_shapes=[pltpu.VMEM((t, DH), BF16)] * 3 + [pltpu.VMEM((t, LANES), F32), pltpu.VMEM((nb, 8, LANES), F32)],
        compiler_params=_cparams(("arbitrary", "arbitrary")),
    )(pf, pf, pf, ps, f_bias, gq, gk)


def _fox_bwd(pf, ps, f_bias, gq, gk, d_ob, ob, lse, dps_in, b_loc, t):
    n = b_loc * t
    nb = t // FOX_BLOCK
    qt = min(FOX_TILE, t)
    scale = DH ** -0.5
    col, ps_spec, smem, vec, blk = _fox_specs(t)

    def body(q_ref, k_ref, v_ref, ps_ref, fb_ref, gq_ref, gk_ref, do_ref, o_ref, lse_ref, dpsi_ref,
             dq_ref, dk_ref, dv_ref, dps_ref, dsm_ref, qs, ks, vs, ccol, crow, dos, dl, dqa, dcr, dcq):
        b, h = pl.program_id(0), pl.program_id(1)
        rq, _, f_in = _fox_prologue(q_ref, k_ref, v_ref, ps_ref, fb_ref, gq_ref, gk_ref, h, t, qs, ks, vs, ccol, crow)
        dov = do_ref[...]
        dos[...] = _bf(dov)
        dl[...] = jnp.broadcast_to(jnp.sum(dov * o_ref[...], axis=1, keepdims=True), (t, LANES))
        dqa[...] = jnp.zeros_like(dqa)
        dcq[...] = jnp.zeros_like(dcq)
        gkv = gk_ref[...]

        ksub = FOX_SHORT // FOX_BLOCK

        def kblock(j, dgk):
            rj = pl.ds(pl.multiple_of(j * FOX_SHORT, FOX_SHORT), FOX_SHORT)
            kj, vj = ks[rj, :], vs[rj, :]
            cr = jnp.concatenate([crow[j * ksub + u, 0:1, :] for u in range(ksub)], axis=1)

            def wide(x):
                return jnp.concatenate([x] * ksub, axis=1)

            def qtile(i, carry):
                dk_acc, dv_acc, dc = carry
                ri = pl.ds(pl.multiple_of(i * qt, qt), qt)
                qi, doi = qs[ri, :], dos[ri, :]
                s = _fox_scores(qi, kj, wide(ccol[ri, :]), cr, i * qt, j * FOX_SHORT)
                p = jnp.exp(s - wide(lse_ref[ri, :]))
                ds = p * (_dot_nt(doi, vj) - wide(dl[ri, :]))
                dsb = _bf(ds)
                dqa[ri, :] += _dot(dsb, kj)
                dcq[ri, :] += jnp.broadcast_to(jnp.sum(ds, axis=1, keepdims=True), (qt, LANES))
                return (dk_acc + _dot_tn(dsb, qi), dv_acc + _dot_tn(_bf(p), doi),
                        dc - jnp.sum(ds, axis=0, keepdims=True))

            z = jnp.zeros((FOX_SHORT, DH), F32)
            dk_acc, dv_acc, dc = lax.fori_loop((j * FOX_SHORT) // qt, t // qt, qtile,
                                               (z, z, jnp.zeros((1, FOX_SHORT), F32)))
            dv_ref[rj, :] = _bf(dv_acc)
            for u in range(ksub):
                dcr[pl.ds(pl.multiple_of((j * ksub + u) * 8, 8), 8), :] = jnp.broadcast_to(
                    dc[:, u * FOX_BLOCK:(u + 1) * FOX_BLOCK], (8, LANES))
            kraw = k_ref[rj, :]
            rk = lax.rsqrt(jnp.mean(kraw * kraw, axis=1, keepdims=True) + EPS)
            dkn = dk_acc * scale
            gy = dkn * gkv
            dk_ref[rj, :] = _bf(rk * gy - kraw * (rk * rk * rk * (1.0 / DH)) * jnp.sum(gy * kraw, axis=1, keepdims=True))
            return dgk + jnp.sum(dkn * kraw * rk, axis=0, keepdims=True)

        dgk = lax.fori_loop(0, t // FOX_SHORT, kblock, jnp.zeros((1, DH), F32))

        q = q_ref[...]
        dqn = dqa[...] * scale
        gy = dqn * gq_ref[...]
        dq_ref[...] = _bf(rq * gy - q * (rq * rq * rq * (1.0 / DH)) * jnp.sum(gy * q, axis=1, keepdims=True))
        dgq = jnp.sum(dqn * q * rq, axis=0, keepdims=True)

        r = lax.broadcasted_iota(jnp.int32, (FOX_BLOCK, FOX_BLOCK), 0)
        c = lax.broadcasted_iota(jnp.int32, (FOX_BLOCK, FOX_BLOCK), 1)
        triuf = (r <= c).astype(F32)

        def rev(jj, carry):
            j = nb - 1 - jj
            rows = pl.ds(pl.multiple_of(j * FOX_BLOCK, FOX_BLOCK), FOX_BLOCK)
            rowv = dcr[pl.ds(pl.multiple_of(j * 8, 8), 1), :]
            colv = jnp.sum(jnp.where(c >= r, jnp.broadcast_to(rowv, (FOX_BLOCK, LANES)), 0.0), axis=1, keepdims=True)
            qcol = dcq[rows, :]
            dl[rows, :] = colv + _dot_exact_l(triuf, qcol, terms=3) + carry
            return carry + jnp.sum(rowv, axis=1, keepdims=True) + jnp.sum(qcol, axis=0, keepdims=True)

        lax.fori_loop(0, nb, rev, jnp.zeros((1, LANES), F32))
        d_ff = dl[...] * _sigmoid(-f_in)
        lane = lax.broadcasted_iota(jnp.int32, (t, LANES), 1)

        @pl.when(h == 0)
        def _():
            dps_ref[...] = dpsi_ref[...]

        dps_ref[...] += jnp.where(lane == LANE_FF + h, d_ff, 0.0)

        lane1 = lax.broadcasted_iota(jnp.int32, (1, LANES), 1)
        d_fb = jnp.sum(d_ff, axis=0, keepdims=True)
        small = _stack_rows([dgq, dgk, jnp.where(lane1 == h, d_fb, 0.0)], 8)

        @pl.when((b == 0) & (h == 0))
        def _():
            dsm_ref[...] = jnp.zeros_like(dsm_ref)

        dsm_ref[...] += small

    ob_ = jax.ShapeDtypeStruct((n, HEADS * DH), BF16)
    return pl.pallas_call(
        body, name="fox_bwd", grid=(b_loc, HEADS),
        in_specs=[col(0), col(HEADS), col(2 * HEADS), ps_spec, smem, vec, vec, blk, blk, blk, ps_spec],
        out_specs=[blk, blk, blk, ps_spec, pl.BlockSpec((8, LANES), lambda b, h: (0, 0))],
        out_shape=[ob_, ob_, ob_, jax.ShapeDtypeStruct((n, LANES), F32), jax.ShapeDtypeStruct((8, LANES), F32)],
        scratch_shapes=([pltpu.VMEM((t, DH), BF16)] * 3 + [pltpu.VMEM((t, LANES), F32), pltpu.VMEM((nb, 8, LANES), F32)]
                        + [pltpu.VMEM((t, DH), BF16), pltpu.VMEM((t, LANES), F32), pltpu.VMEM((t, DH), F32),
                           pltpu.VMEM((8 * nb, LANES), F32), pltpu.VMEM((t, LANES), F32)]),
        compiler_params=_cparams(("arbitrary", "arbitrary")),
    )(pf, pf, pf, ps, f_bias, gq, gk, d_ob, ob, lse, dps_in)


class _NoExchange:
    def late_weights(self, after):
        return {}

    def grads_ready(self, grads, tie):
        return tie


def _local_step(x, target, w, b_loc, t, comm=None):
    comm = comm or _NoExchange()
    w = dict(w)
    xf = x
    u = _rms_fwd(xf, w["norm_mix_g"], "rms_mix")
    pg = _mm(u, w["w_gdn"], name="proj_gdn")
    pf = _mm(u, w["w_fox"], name="proj_fox")
    pgate = _mm(u, w["w_gate"], name="proj_gate")
    ps = _mm(u, w["w_small"], name="proj_small")
    oa, o_raw, s_all = _gdn_fwd(pg, ps, w["conv_w"], w["a_log"], w["dt_bias"], w["gdn_norm_g"], b_loc, t)
    ob, lse = _fox_fwd(pf, ps, w["f_bias"], w["fox_q_norm_g"], w["fox_k_norm_g"], b_loc, t)
    w.update(comm.late_weights(ob))
    ya = _mm(oa, w["w_proj_gdn"], name="proj_a")
    yb = _mm(ob, w["w_proj_fox"], name="proj_b")
    merged = _merge_fwd(ya, yb, pgate)
    h = _mm(merged, w["w_out"], name="proj_out", epi=lambda acc, xr: acc + xr, extras=(xf,))
    hn = _rms_fwd(h, w["norm_mlp_g"], "rms_mlp")
    up, act = _mm(hn, w["w_up"], name="mlp_up", out_dtype=BF16, out2=(_relu2, BF16))
    out = _mm(act, w["w_down"], name="mlp_down", epi=lambda acc, hr: acc + hr, extras=(h,))
    d_out, d_out16, loss_blk = _loss_bwd(out, target)

    g = {}
    g["w_down"] = _mm(act, d_out16, name="dw_down", ta=True, out_dtype=BF16)
    d_up = _mm(d_out16, w["w_down"], name="d_up", tb=True, out_dtype=BF16,
               epi=lambda acc, upr: acc * (2.0 * jnp.maximum(upr.astype(F32), 0.0)), extras=(up,))
    g["w_up"] = _mm(hn, d_up, name="dw_up", ta=True, out_dtype=BF16)
    mlp_gain = comm.grads_ready({"w_down": g["w_down"], "w_up": g["w_up"]}, w["norm_mlp_g"])
    d_hn = _mm(d_up, w["w_up"], name="d_hn", tb=True)
    dh, dh16, g["norm_mlp_g"] = _rms_bwd(d_hn, h, mlp_gain, d_out, "rms_mlp_bwd")
    g["w_out"] = _mm(merged, dh16, name="dw_out", ta=True, out_dtype=BF16)
    dm = _mm(dh16, w["w_out"], name="d_merged", tb=True)
    dya, dyb, dgate_a, dgate_b = _merge_bwd(dm, ya, yb, pgate)
    g["w_proj_gdn"] = _mm(oa, dya, name="dw_proj_a", ta=True, out_dtype=BF16)
    g["w_proj_fox"] = _mm(ob, dyb, name="dw_proj_b", ta=True, out_dtype=BF16)
    gdn_gain = comm.grads_ready({"w_out": g["w_out"], "w_proj_gdn": g["w_proj_gdn"], "w_proj_fox": g["w_proj_fox"]},
                                w["gdn_norm_g"])
    d_oa = _mm(dya, w["w_proj_gdn"], name="d_oa", tb=True)
    d_ob = _mm(dyb, w["w_proj_fox"], name="d_ob", tb=True)
    dgq, dgk, dgv, dgz, dps, dcw, gdn_small = _gdn_bwd(pg, ps, w["conv_w"], w["a_log"], w["dt_bias"], gdn_gain,
                                                       d_oa, o_raw, s_all, b_loc, t)
    dfq, dfk, dfv, dps, fox_small = _fox_bwd(pf, ps, w["f_bias"], w["fox_q_norm_g"], w["fox_k_norm_g"],
                                             d_ob, ob, lse, dps, b_loc, t)
    segs = [(dgq, "w_gdn", 0), (dgk, "w_gdn", 1024), (dgv, "w_gdn", 2048), (dgz, "w_gdn", 3072),
            (dfq, "w_fox", 0), (dfk, "w_fox", 1024), (dfv, "w_fox", 2048),
            (dgate_a, "w_gate", 0), (dgate_b, "w_gate", 1024)]
    dws = [_mm(u, dps, name="dw_small", ta=True, out_dtype=BF16)]
    dws += [_mm(u, dseg, name=f"dw_in_{idx}", ta=True, out_dtype=BF16) for idx, (dseg, _, _) in enumerate(segs)]
    g["w_in_parts"] = dws
    mix_gain = comm.grads_ready({"w_in_parts": dws}, w["norm_mix_g"])
    du = _mm(dps, w["w_small"], name="du_small", tb=True)
    for idx, (dseg, wname, off) in enumerate(segs):
        du = _mm(dseg, w[wname], name=f"du_{idx}", tb=True, b_koff=off,
                 epi=lambda acc, prev: acc + prev, extras=(du,))
    grad_x, _, g["norm_mix_g"] = _rms_bwd(du, xf, mix_gain, dh, "rms_mix_bwd")
    g["conv"] = dcw
    g["gdn_small"] = gdn_small
    g["fox_small"] = fox_small
    return loss_blk, grad_x, g


def _position():
    x, y, c = lax.axis_index("x"), lax.axis_index("y"), lax.axis_index("c")
    return x, y, c


def _to_bf16(arrs, name):
    n = len(arrs)

    def body(*refs):
        for i in range(n):
            refs[n + i][...] = _bf(refs[i][...])

    return pl.pallas_call(
        body, name=name,
        out_shape=[jax.ShapeDtypeStruct(a.shape, BF16) for a in arrs],
        compiler_params=_cparams(),
    )(*arrs)


def _all_gather(arrs, name):
    n = len(arrs)
    hbm = pl.BlockSpec(memory_space=pl.ANY)

    def body(*refs):
        ins, outs = refs[:n], refs[n:2 * n]
        send, recv, loc = refs[2 * n:]
        x, y, c = _position()
        me = 4 * x + 2 * y + c
        sibling = (x, y, 1 - c)
        chips = [(1 - x, y), (x, 1 - y), (1 - x, 1 - y)]

        def idx(px, py, pc):
            return 4 * px + 2 * py + pc

        def cp(a, k, block, to, src=None):
            return pltpu.make_async_remote_copy(
                src_ref=outs[a].at[block] if src is None else src, dst_ref=outs[a].at[block],
                send_sem=send.at[a, k], recv_sem=recv.at[a, k], device_id=to, device_id_type=MESH)

        mine = [pltpu.make_async_copy(ins[a], outs[a].at[me], loc.at[a]) for a in range(n)]
        for m in mine:
            m.start()
        first = []
        for a in range(n):
            first.append(cp(a, 0, me, sibling, src=ins[a]))
            first += [cp(a, 1 + j, me, (*chip, c), src=ins[a]) for j, chip in enumerate(chips)]
        for f in first:
            f.start()
        passed = []
        for j, chip in enumerate(chips):
            for a in range(n):
                cp(a, 1 + j, idx(*chip, c), (x, y, c)).wait_recv()
                p = cp(a, 4 + j, idx(*chip, c), sibling)
                p.start()
                passed.append(p)
        for a in range(n):
            cp(a, 0, idx(x, y, 1 - c), (x, y, c)).wait_recv()
            for j, chip in enumerate(chips):
                cp(a, 4 + j, idx(*chip, 1 - c), (x, y, c)).wait_recv()
        for f in first + passed:
            f.wait_send()
        for m in mine:
            m.wait()

    return pl.pallas_call(
        body, name=name,
        in_specs=[hbm] * n, out_specs=[hbm] * n,
        out_shape=[jax.ShapeDtypeStruct((N_DEV,) + a.shape, a.dtype) for a in arrs],
        scratch_shapes=[pltpu.SemaphoreType.DMA((n, 7)), pltpu.SemaphoreType.DMA((n, 7)), pltpu.SemaphoreType.DMA((n,))],
        compiler_params=pltpu.CompilerParams(has_side_effects=True),
    )(*arrs)


def _peer(x, y, c, rel):
    return ((1 - x) if rel & 4 else x, (1 - y) if rel & 2 else y, (1 - c) if rel & 1 else c)


def _exchange(arrs, name):
    n = len(arrs)
    hbm = pl.BlockSpec(memory_space=pl.ANY)

    def body(*refs):
        ins, outs = refs[:n], refs[n:2 * n]
        send, recv, loc = refs[2 * n:]
        x, y, c = _position()
        me = 4 * x + 2 * y + c
        mine = [pltpu.make_async_copy(ins[a].at[me], outs[a].at[me], loc.at[a]) for a in range(n)]
        for m in mine:
            m.start()
        copies = []
        for rel in range(1, N_DEV):
            px, py, pc = _peer(x, y, c, rel)
            for a in range(n):
                copies.append(pltpu.make_async_remote_copy(
                    src_ref=ins[a].at[4 * px + 2 * py + pc], dst_ref=outs[a].at[me],
                    send_sem=send.at[a, rel - 1], recv_sem=recv.at[a, rel - 1],
                    device_id=(px, py, pc), device_id_type=MESH))
        for cpy in copies:
            cpy.start()
        for cpy in copies:
            cpy.wait()
        for m in mine:
            m.wait()

    return pl.pallas_call(
        body, name=name,
        in_specs=[hbm] * n, out_specs=[hbm] * n,
        out_shape=[jax.ShapeDtypeStruct(a.shape, a.dtype) for a in arrs],
        scratch_shapes=[pltpu.SemaphoreType.DMA((n, 7)), pltpu.SemaphoreType.DMA((n, 7)), pltpu.SemaphoreType.DMA((n,))],
        compiler_params=pltpu.CompilerParams(has_side_effects=True),
    )(*arrs)


HBM_SPEC = pl.BlockSpec(memory_space=pltpu.HBM)
SEM_SPEC = pl.BlockSpec(memory_space=pltpu.SEMAPHORE)
DATAFLOW = pltpu.SideEffectType.DATAFLOW_SIDE_EFFECTING


CHIP_RELS = (2, 4, 6)


def _push_start(arrs, slots, name, chips=False):
    n = len(arrs)
    n_slots = 4 if chips else N_DEV
    rels = CHIP_RELS if chips else tuple(range(1, N_DEV))
    land_shapes = [a.shape if slots else (n_slots,) + a.shape for a in arrs]

    def body(*refs):
        ins, lands, sends, recvs, token = refs[:n], refs[n:2 * n], refs[2 * n:3 * n], refs[3 * n:4 * n], refs[-1]
        x, y, c = _position()
        for rel in rels:
            px, py, pc = _peer(x, y, c, rel)
            mine, theirs = (2 * x + y, 2 * px + py) if chips else (4 * x + 2 * y + c, 4 * px + 2 * py + pc)
            for a in range(n):
                pltpu.make_async_remote_copy(
                    src_ref=ins[a].at[theirs] if slots else ins[a], dst_ref=lands[a].at[mine],
                    send_sem=sends[a], recv_sem=recvs[a], device_id=(px, py, pc), device_id_type=MESH).start()
        token[...] = jnp.zeros_like(token)

    sem = pltpu.SemaphoreType.DMA(())
    outs = pl.pallas_call(
        body, name=name,
        in_specs=[HBM_SPEC] * (2 * n),
        out_shape=(*[sem] * (2 * n), *[pltpu.HBM(a.shape, a.dtype) for a in arrs],
                   *[pltpu.HBM(s, a.dtype) for s, a in zip(land_shapes, arrs)], jax.ShapeDtypeStruct((8, LANES), F32)),
        out_specs=(*[SEM_SPEC] * (2 * n), *[HBM_SPEC] * (2 * n), pl.BlockSpec(memory_space=pltpu.VMEM)),
        input_output_aliases={i: 2 * n + i for i in range(2 * n)},
        compiler_params=pltpu.CompilerParams(has_side_effects=DATAFLOW),
    )(*[pltpu.with_memory_space_constraint(a, pltpu.HBM) for a in arrs],
      *[pltpu.with_memory_space_constraint(lax.empty(s, a.dtype), pltpu.HBM) for s, a in zip(land_shapes, arrs)])
    return dict(sends=list(outs[:n]), recvs=list(outs[n:2 * n]), ins=list(outs[2 * n:3 * n]),
                lands=list(outs[3 * n:4 * n]), token=outs[-1], copies=len(rels))


def _push_wait(started, after, name):
    n = len(started["ins"])
    copies = started["copies"]

    def body(*refs):
        lands, sends, recvs = refs[n:2 * n], refs[2 * n:3 * n], refs[3 * n:4 * n]
        x, y, c = _position()
        for a in range(n):
            every = lands[a].at[pl.ds(0, copies)]
            drain = pltpu.make_async_remote_copy(src_ref=every, dst_ref=every, send_sem=sends[a], recv_sem=recvs[a],
                                                 device_id=(x, y, c), device_id_type=MESH)
            drain.wait_send()
            drain.wait_recv()

    both = started["ins"] + started["lands"]
    outs = pl.pallas_call(
        body, name=name,
        in_specs=[HBM_SPEC] * (2 * n) + [SEM_SPEC] * (2 * n) + [pl.BlockSpec(memory_space=pl.ANY)],
        out_shape=tuple(pltpu.HBM(a.shape, a.dtype) for a in both), out_specs=tuple([HBM_SPEC] * (2 * n)),
        input_output_aliases={i: i for i in range(2 * n)},
        compiler_params=pltpu.CompilerParams(has_side_effects=DATAFLOW),
    )(*both, *started["sends"], *started["recvs"], after)
    return list(outs[:n]), list(outs[n:])


def _sibling_swap(arr, name):
    chips = N_DEV // 2

    def body(in_ref, out_ref, send, recv):
        x, y, c = _position()
        for s in range(chips):
            pltpu.make_async_remote_copy(src_ref=in_ref.at[2 * s + 1 - c], dst_ref=out_ref.at[s], send_sem=send,
                                         recv_sem=recv, device_id=(x, y, 1 - c), device_id_type=MESH).start()
        pltpu.make_async_remote_copy(src_ref=out_ref, dst_ref=out_ref, send_sem=send, recv_sem=recv,
                                     device_id=(x, y, 1 - c), device_id_type=MESH).wait()

    hbm = pl.BlockSpec(memory_space=pl.ANY)
    return pl.pallas_call(
        body, name=name, in_specs=[hbm], out_specs=hbm,
        out_shape=jax.ShapeDtypeStruct((chips,) + arr.shape[1:], arr.dtype),
        scratch_shapes=[pltpu.SemaphoreType.DMA, pltpu.SemaphoreType.DMA],
        compiler_params=pltpu.CompilerParams(has_side_effects=True),
    )(arr)


def _add_halves(core, arr, other, name):
    ns, r, c = other.shape
    tr = min(r, 256)

    def body(core_ref, a_ref, o_ref, out_ref):
        out_ref[...] = _bf(a_ref[...].astype(F32) + o_ref[...].astype(F32))

    blk = pl.BlockSpec((1, tr, c), lambda s, i, core_ref: (s, i, 0))
    return pl.pallas_call(
        body, name=name,
        grid_spec=pltpu.PrefetchScalarGridSpec(
            num_scalar_prefetch=1, grid=(ns, r // tr),
            in_specs=[pl.BlockSpec((1, tr, c), lambda s, i, core_ref: (2 * s + core_ref[0], i, 0)), blk],
            out_specs=blk),
        out_shape=jax.ShapeDtypeStruct((ns, r, c), BF16),
        compiler_params=_cparams(("parallel", "parallel")),
    )(core, arr, other)


def _all_reduce_small(buf, name):
    rows = buf.shape[0]

    def body(in_ref, out_ref, slots, send, recv):
        x, y, c = _position()
        me = 4 * x + 2 * y + c
        slots[me] = in_ref[...]
        copies = []
        for rel in range(1, N_DEV):
            copies.append(pltpu.make_async_remote_copy(
                src_ref=in_ref, dst_ref=slots.at[me], send_sem=send.at[rel - 1], recv_sem=recv.at[rel - 1],
                device_id=_peer(x, y, c, rel), device_id_type=MESH))
        for cpy in copies:
            cpy.start()
        for cpy in copies:
            cpy.wait()
        tot = slots[0]
        for d in range(1, N_DEV):
            tot = tot + slots[d]
        out_ref[...] = tot

    return pl.pallas_call(
        body, name=name,
        out_shape=jax.ShapeDtypeStruct((rows, LANES), F32),
        in_specs=[pl.BlockSpec(memory_space=pltpu.VMEM)], out_specs=pl.BlockSpec(memory_space=pltpu.VMEM),
        scratch_shapes=[pltpu.VMEM((N_DEV, rows, LANES), F32), pltpu.SemaphoreType.DMA((7,)),
                        pltpu.SemaphoreType.DMA((7,))],
        compiler_params=pltpu.CompilerParams(has_side_effects=True),
    )(buf)


def _adam_math(g, w, m, v):
    m = ADAM_B1 * m + (1.0 - ADAM_B1) * g
    v = ADAM_B2 * v + (1.0 - ADAM_B2) * (g * g)
    m_hat = m / (1.0 - ADAM_B1 ** ADAM_STEP)
    v_hat = v / (1.0 - ADAM_B2 ** ADAM_STEP)
    delta = -ADAM_LR * (m_hat / (jnp.sqrt(v_hat) + ADAM_EPS) + ADAM_WD * w)
    return delta, m, v


def _adam_shard(me, parts, mine, w, m, v, name):
    r, c = w.shape
    tr = min(r, 128)
    n_slots = parts.shape[0]

    def body(me_ref, p_ref, own_ref, w_ref, m_ref, v_ref, g_ref, d_ref, nm_ref, nv_ref):
        own = own_ref[0].astype(F32)
        g = None
        for s in range(n_slots):
            term = jnp.where(me_ref[0] == s, own, p_ref[s].astype(F32))
            g = term if g is None else g + term
        d, nm, nv = _adam_math(g, w_ref[...], m_ref[...], v_ref[...])
        g_ref[...] = g
        d_ref[...] = d
        nm_ref[...] = nm
        nv_ref[...] = nv

    row = pl.BlockSpec((tr, c), lambda i, me_ref: (i, 0))
    o = jax.ShapeDtypeStruct((r, c), F32)
    return pl.pallas_call(
        body, name=name,
        grid_spec=pltpu.PrefetchScalarGridSpec(
            num_scalar_prefetch=1, grid=(r // tr,),
            in_specs=[pl.BlockSpec((n_slots, tr, c), lambda i, me_ref: (0, i, 0)),
                      pl.BlockSpec((1, tr, c), lambda i, me_ref: (me_ref[0], i, 0)), row, row, row],
            out_specs=[row] * 4),
        out_shape=[o] * 4,
        compiler_params=_cparams(("parallel",)),
    )(me, parts, mine, w, m, v)


def _adam_small(g, w, m, v):
    def body(g_ref, w_ref, m_ref, v_ref, d_ref, nm_ref, nv_ref):
        d, nm, nv = _adam_math(g_ref[...], w_ref[...], m_ref[...], v_ref[...])
        d_ref[...] = d
        nm_ref[...] = nm
        nv_ref[...] = nv

    o = jax.ShapeDtypeStruct(g.shape, F32)
    return pl.pallas_call(body, name="adam_small", out_shape=[o] * 3, compiler_params=_cparams())(g, w, m, v)


def _split_w_in(w_full):
    o = IN_OFF
    w_gdn = w_full[:, o["gq"]:o["ga"]]
    w_fox = w_full[:, o["fq"]:o["ff"]]
    w_gate = w_full[:, o["gate_a"]:o["end"]]
    w_small = jnp.concatenate([w_full[:, o["ga"]:o["fq"]], w_full[:, o["ff"]:o["gate_a"]],
                               jnp.zeros((w_full.shape[0], LANES - 24), w_full.dtype)], axis=1)
    return w_gdn, w_fox, w_gate, w_small


def _join_w_in(parts):
    small = parts[0]
    return jnp.concatenate(parts[1:5] + [small[:, 0:16]] + parts[5:8] + [small[:, 16:24]] + parts[8:10], axis=1)


def _rows128(a, rows):
    flat = a.reshape(-1)
    flat = jnp.concatenate([flat, jnp.zeros((rows * LANES - flat.shape[0],), flat.dtype)])
    return flat.reshape(rows, LANES)


def kernel(x, norm_mix_g, w_in, gdn_conv_w, gdn_a_log, gdn_dt_bias, gdn_norm_g, fox_q_norm_g, fox_k_norm_g, fox_f_bias, w_proj_gdn, w_proj_fox, w_out, norm_mlp_g, w_up, w_down, loss_target, m_norm_mix_g, m_w_in, m_gdn_conv_w, m_gdn_a_log, m_gdn_dt_bias, m_gdn_norm_g, m_fox_q_norm_g, m_fox_k_norm_g, m_fox_f_bias, m_w_proj_gdn, m_w_proj_fox, m_w_out, m_norm_mlp_g, m_w_up, m_w_down, v_norm_mix_g, v_w_in, v_gdn_conv_w, v_gdn_a_log, v_gdn_dt_bias, v_gdn_norm_g, v_fox_q_norm_g, v_fox_k_norm_g, v_fox_f_bias, v_w_proj_gdn, v_w_proj_fox, v_w_out, v_norm_mlp_g, v_w_up, v_w_down):
    b_loc, t, d = x.shape
    n = b_loc * t
    me = 4 * lax.axis_index("x") + 2 * lax.axis_index("y") + lax.axis_index("c")

    late_names = ["w_proj_gdn", "w_proj_fox", "w_out", "w_up", "w_down"]
    big16 = _to_bf16([w_in[0], w_proj_gdn[0], w_proj_fox[0], w_out[0], w_up[0], w_down[0]], "weights_to_bf16")
    g_in, g_conv = _all_gather([big16[0], gdn_conv_w[0]], "gather_w_in")
    late = _push_start(list(big16[1:]), False, "gather_late_start")
    w_full = g_in.transpose(1, 0, 2).reshape(d, N_DEV * w_in.shape[2])
    w_gdn, w_fox, w_gate, w_small = _split_w_in(w_full)
    weights = {
        "w_gdn": w_gdn, "w_fox": w_fox, "w_gate": w_gate, "w_small": w_small,
        "conv_w": g_conv.transpose(1, 0, 2).reshape(CONV_K, 3 * d),
        "norm_mix_g": norm_mix_g + late["token"][0:1, 0:1], "norm_mlp_g": norm_mlp_g, "a_log": gdn_a_log,
        "dt_bias": gdn_dt_bias, "gdn_norm_g": gdn_norm_g, "fox_q_norm_g": fox_q_norm_g, "fox_k_norm_g": fox_k_norm_g,
        "f_bias": fox_f_bias,
    }
    c_in, c_up = w_in.shape[2], w_up.shape[2]
    me1 = jnp.reshape(me, (1,)).astype(jnp.int32)
    chip1 = jnp.reshape(2 * lax.axis_index("x") + lax.axis_index("y"), (1,)).astype(jnp.int32)
    core1 = jnp.reshape(lax.axis_index("c"), (1,)).astype(jnp.int32)

    class _Exchange:
        def __init__(self):
            self.started = []

        def late_weights(self, after):
            shards, lands = _push_wait(late, after, "gather_late_wait")
            full = [lax.dynamic_update_index_in_dim(land, shard, me, 0) for land, shard in zip(lands, shards)]
            g_pa, g_pb, g_out, g_up, g_down = full
            return {"w_proj_gdn": g_pa.reshape(d, d), "w_proj_fox": g_pb.reshape(d, d), "w_out": g_out.reshape(d, d),
                    "w_up": g_up.transpose(1, 0, 2).reshape(d, D_FF), "w_down": g_down.reshape(D_FF, d)}

        def grads_ready(self, grads, tie):
            names = list(grads)
            if names == ["w_in_parts"]:
                halves = _join_w_in(grads["w_in_parts"]).reshape(d, N_DEV, c_in).transpose(1, 0, 2)
                other = _sibling_swap(halves, "grads_w_in_sibling")
                pair = _add_halves(core1, halves, other, "grads_w_in_pair")
                st = _push_start([pair], True, "grads_start_w_in_parts", chips=True)
            else:
                layout = {"w_up": lambda a: a.reshape(d, N_DEV, c_up).transpose(1, 0, 2),
                          "w_down": lambda a: a.reshape(N_DEV, D_FF // N_DEV, d)}
                arrs = [layout.get(k, lambda a: a.reshape(N_DEV, d // N_DEV, d))(grads[k]) for k in names]
                st = _push_start(arrs, True, "grads_start_" + names[0])
            self.started.append((names, st))
            return tie + st["token"][0:1, 0:1]

    comm = _Exchange()
    loss_blk, grad_x, g = _local_step(x.reshape(n, d), loss_target.reshape(n, d), weights, b_loc, t, comm)

    shards = {"w_in_parts": (w_in, m_w_in, v_w_in), "w_proj_gdn": (w_proj_gdn, m_w_proj_gdn, v_w_proj_gdn),
              "w_proj_fox": (w_proj_fox, m_w_proj_fox, v_w_proj_fox), "w_out": (w_out, m_w_out, v_w_out),
              "w_up": (w_up, m_w_up, v_w_up), "w_down": (w_down, m_w_down, v_w_down)}
    adam = {}

    def finish(names, st, after):
        mine, parts = _push_wait(st, after, "grads_wait_" + names[0])
        slot = chip1 if st["copies"] == len(CHIP_RELS) else me1
        for k, own, part in zip(names, mine, parts):
            wi, mi, vi = shards[k]
            adam[k] = [r[None] for r in _adam_shard(slot, part, own, wi[0], mi[0], vi[0], "adam_" + k)]

    for names, st in comm.started[:-1]:
        finish(names, st, grad_x)

    conv_rows = CONV_K * 3 * d // LANES
    conv_g = g["conv"].transpose(1, 0, 2).reshape(conv_rows, LANES)
    buf = jnp.concatenate([conv_g, g["norm_mix_g"].reshape(8, LANES), g["norm_mlp_g"].reshape(8, LANES),
                           g["gdn_small"], g["fox_small"], loss_blk], axis=0)
    anchor = sum(adam[k][1][0, 0:1, 0:LANES] for names, _ in comm.started[:-1] for k in names) * 0.0
    tot = _all_reduce_small(buf + anchor, "all_reduce_small")
    finish(*comm.started[-1], tot)
    big_out = [adam[k] for k in ["w_in_parts"] + late_names]
    o = conv_rows
    conv_full = tot[0:o].reshape(CONV_K, 3 * d)
    c_conv = gdn_conv_w.shape[2]
    g_conv_shard = lax.dynamic_slice(conv_full, (0, me * c_conv), (CONV_K, c_conv))
    g_mix = tot[o:o + 8].reshape(1, d)
    g_mlp = tot[o + 8:o + 16].reshape(1, d)
    gs, fs = tot[o + 16:o + 24], tot[o + 24:o + 32]
    loss = tot[o + 32, 0]
    small_g = [g_mix, g_conv_shard[None], gs[0:1, 0:HEADS], gs[1:2, 0:HEADS], gs[2:3], fs[0:1], fs[1:2], fs[2:3, 0:HEADS],
               g_mlp]
    small_w = [norm_mix_g, gdn_conv_w, gdn_a_log, gdn_dt_bias, gdn_norm_g, fox_q_norm_g, fox_k_norm_g, fox_f_bias,
               norm_mlp_g]
    small_m = [m_norm_mix_g, m_gdn_conv_w, m_gdn_a_log, m_gdn_dt_bias, m_gdn_norm_g, m_fox_q_norm_g, m_fox_k_norm_g,
               m_fox_f_bias, m_norm_mlp_g]
    small_v = [v_norm_mix_g, v_gdn_conv_w, v_gdn_a_log, v_gdn_dt_bias, v_gdn_norm_g, v_fox_q_norm_g, v_fox_k_norm_g,
               v_fox_f_bias, v_norm_mlp_g]
    row_counts = [-(-a.size // (8 * LANES)) * 8 for a in small_w]

    def pack(arrs):
        return jnp.concatenate([_rows128(a, rc) for a, rc in zip(arrs, row_counts)], axis=0)

    sd, sm, sv = _adam_small(pack(small_g), pack(small_w), pack(small_m), pack(small_v))

    def unpack(p):
        outs, r0 = [], 0
        for a, rc in zip(small_w, row_counts):
            outs.append(p[r0:r0 + rc].reshape(-1)[:a.size].reshape(a.shape))
            r0 += rc
        return outs

    small_out = [small_g_i.reshape(w_i.shape) for small_g_i, w_i in zip(small_g, small_w)], unpack(sd), unpack(sm), unpack(sv)

    def ordered(kind):
        s = small_out[kind]
        bo = [b[kind] for b in big_out]
        return [s[0], bo[0], s[1], s[2], s[3], s[4], s[5], s[6], s[7], bo[1], bo[2], bo[3], s[8], bo[4], bo[5]]

    return (loss, grad_x.reshape(b_loc, t, d), *ordered(0), *ordered(1), *ordered(2), *ordered(3))
```

```python
import functools

import jax
import jax.numpy as jnp
from jax import lax
from jax.experimental import pallas as pl
from jax.experimental.pallas import tpu as pltpu

F32 = jnp.float32
BF16 = jnp.bfloat16
HI = lax.Precision.HIGHEST
MESH = pl.DeviceIdType.MESH

N_DEV = 8
D_MODEL = 1024
HEADS = 8
DH = 128
CONV_K = 4
CHUNK = 128
GDN_GROUP = 8
FOX_BLOCK = 128
FOX_TILE = 512
FOX_SHORT = 512
D_FF = 4 * D_MODEL
EPS = 1e-6
LANES = 128
NEG = -1e30
IN_OFF = {"gq": 0, "gk": 1024, "gv": 2048, "gz": 3072, "ga": 4096, "gb": 4104, "fq": 4112, "fk": 5136,
          "fv": 6160, "ff": 7184, "gate_a": 7192, "gate_b": 8216, "end": 9240}
LANE_GA, LANE_GB, LANE_FF = 0, 8, 16

ADAM_LR = 0.001
ADAM_B1 = 0.9
ADAM_B2 = 0.999
ADAM_EPS = 1e-08
ADAM_WD = 0.01
ADAM_STEP = 10

VMEM_LIMIT = 56 * 1024 * 1024


def _cparams(sem=None):
    return pltpu.CompilerParams(dimension_semantics=sem, vmem_limit_bytes=VMEM_LIMIT)


def _sigmoid(x):
    return 1.0 / (1.0 + jnp.exp(-x))


def _softplus(x):
    return jnp.maximum(x, 0.0) + jnp.log(1.0 + jnp.exp(-jnp.abs(x)))


def _dot(a, b, prec=None):
    return lax.dot_general(a, b, (((1,), (0,)), ((), ())), precision=prec, preferred_element_type=F32)


def _dot_nt(a, b, prec=None):
    return lax.dot_general(a, b, (((1,), (1,)), ((), ())), precision=prec, preferred_element_type=F32)


def _dot_tn(a, b, prec=None):
    return lax.dot_general(a, b, (((0,), (0,)), ((), ())), precision=prec, preferred_element_type=F32)


def _bf(x):
    return x.astype(BF16)


MM_TILE = 1024


def _mm(a, b, *, name, ta=False, tb=False, out_dtype=F32, epi=None, extras=(), out2=None,
        b_koff=0, tm=MM_TILE, tn=MM_TILE, tk=MM_TILE):
    m = a.shape[1] if ta else a.shape[0]
    kdim = a.shape[0] if ta else a.shape[1]
    n = b.shape[0] if tb else b.shape[1]
    tm, tn, tk = min(tm, m), min(tn, n), min(tk, kdim)
    nk = kdim // tk
    grid = (m // tm, n // tn, nk)
    koff = b_koff // tk
    a_spec = pl.BlockSpec((tk, tm), lambda i, j, k: (k, i)) if ta else pl.BlockSpec((tm, tk), lambda i, j, k: (i, k))
    if tb:
        b_spec = pl.BlockSpec((tn, tk), lambda i, j, k: (j, k + koff))
    else:
        b_spec = pl.BlockSpec((tk, tn), lambda i, j, k: (k + koff, j))
    o_spec = pl.BlockSpec((tm, tn), lambda i, j, k: (i, j))
    n_e = len(extras)
    n_o = 1 if out2 is None else 2
    dims = (((0 if ta else 1,), (1 if tb else 0,)), ((), ()))

    def body(a_ref, b_ref, *rest):
        e_refs, o_refs = rest[:n_e], rest[n_e:n_e + n_o]
        prod = lax.dot_general(_bf(a_ref[...]), _bf(b_ref[...]), dims, preferred_element_type=F32)

        def finish(r):
            if out2 is not None:
                o_refs[1][...] = out2[0](r).astype(out2[1])
            if epi is not None:
                r = epi(r, *[e[...] for e in e_refs])
            o_refs[0][...] = r.astype(out_dtype)

        if nk == 1:
            finish(prod)
        else:
            acc = rest[n_e + n_o]
            k = pl.program_id(2)

            @pl.when(k == 0)
            def _():
                acc[...] = prod

            @pl.when(k > 0)
            def _():
                acc[...] += prod

            @pl.when(k == nk - 1)
            def _():
                finish(acc[...])

    shapes = [jax.ShapeDtypeStruct((m, n), out_dtype)]
    if out2 is not None:
        shapes.append(jax.ShapeDtypeStruct((m, n), out2[1]))
    res = pl.pallas_call(
        body, name=name, grid=grid,
        in_specs=[a_spec, b_spec] + [o_spec] * n_e,
        out_specs=[o_spec] * n_o, out_shape=shapes,
        scratch_shapes=[] if nk == 1 else [pltpu.VMEM((tm, tn), F32)],
        compiler_params=_cparams(("parallel", "parallel", "arbitrary")),
    )(a, b, *extras)
    return res[0] if out2 is None else res


def _du_all(dps, w_small, segs, w, tm=512):
    n, d = dps.shape[0], w_small.shape[0]
    names = []
    for _, wname, _ in segs:
        if wname not in names:
            names.append(wname)
    first = {nm: min(i for i, s in enumerate(segs) if s[1] == nm) for nm in names}
    count = {nm: sum(1 for s in segs if s[1] == nm) for nm in names}
    n_seg = len(segs)

    def w_spec(nm):
        return pl.BlockSpec((d, d), lambda i, k: (0, jnp.clip(k - first[nm], 0, count[nm] - 1)))

    def body(dps_ref, ws_ref, *rest):
        seg_refs, w_refs, o_ref, acc = rest[:n_seg], rest[n_seg:n_seg + len(names)], rest[-2], rest[-1]
        k = pl.program_id(1)

        @pl.when(k == 0)
        def _():
            acc[...] = _dot_nt(_bf(dps_ref[...]), ws_ref[...])

        for idx, (_, wname, _) in enumerate(segs):
            @pl.when(k == idx)
            def _(idx=idx, wname=wname):
                acc[...] += _dot_nt(seg_refs[idx][...], w_refs[names.index(wname)][...])

        @pl.when(k == n_seg - 1)
        def _():
            o_ref[...] = acc[...]

    row = lambda cols: pl.BlockSpec((tm, cols), lambda i, k: (i, 0))
    return pl.pallas_call(
        body, name="du_all", grid=(n // tm, n_seg),
        in_specs=[row(dps.shape[1]), pl.BlockSpec(w_small.shape, lambda i, k: (0, 0))]
                 + [row(d)] * n_seg + [w_spec(nm) for nm in names],
        out_specs=row(d), out_shape=jax.ShapeDtypeStruct((n, d), F32),
        scratch_shapes=[pltpu.VMEM((tm, d), F32)],
        compiler_params=_cparams(("parallel", "arbitrary")),
    )(dps, w_small, *[s[0] for s in segs], *[w[nm] for nm in names])


def _relu2(x):
    r = jnp.maximum(x, 0.0)
    return r * r


ROWS = 512


def _rms_fwd(x, g, name):
    n, d = x.shape

    def body(x_ref, g_ref, u_ref):
        xv = x_ref[...]
        r = lax.rsqrt(jnp.mean(xv * xv, axis=1, keepdims=True) + EPS)
        u_ref[...] = _bf(xv * r * g_ref[...])

    return pl.pallas_call(
        body, name=name, grid=(n // ROWS,),
        in_specs=[pl.BlockSpec((ROWS, d), lambda i: (i, 0)), pl.BlockSpec((1, d), lambda i: (0, 0))],
        out_specs=pl.BlockSpec((ROWS, d), lambda i: (i, 0)),
        out_shape=jax.ShapeDtypeStruct((n, d), BF16),
        compiler_params=_cparams(("parallel",)),
    )(x, g)


def _rms_bwd(dy, x, g, dres, name):
    n, d = x.shape

    def body(dy_ref, x_ref, g_ref, dres_ref, dx_ref, dx16_ref, dg_ref):
        i = pl.program_id(0)
        xv, dyv = x_ref[...], dy_ref[...]
        r = lax.rsqrt(jnp.mean(xv * xv, axis=1, keepdims=True) + EPS)
        gy = dyv * g_ref[...]
        s = jnp.sum(gy * xv, axis=1, keepdims=True)
        dx = dres_ref[...] + r * gy - xv * (r * r * r * (1.0 / d)) * s
        dx_ref[...] = dx
        dx16_ref[...] = _bf(dx)

        @pl.when(i == 0)
        def _():
            dg_ref[...] = jnp.zeros_like(dg_ref)

        dg_ref[...] += jnp.sum(dyv * xv * r, axis=0, keepdims=True)

    row = pl.BlockSpec((ROWS, d), lambda i: (i, 0))
    vec = pl.BlockSpec((1, d), lambda i: (0, 0))
    return pl.pallas_call(
        body, name=name, grid=(n // ROWS,),
        in_specs=[row, row, vec, row], out_specs=[row, row, vec],
        out_shape=[jax.ShapeDtypeStruct((n, d), F32), jax.ShapeDtypeStruct((n, d), BF16),
                   jax.ShapeDtypeStruct((1, d), F32)],
        compiler_params=_cparams(("arbitrary",)),
    )(dy, x, g, dres)


def _merge_fwd(ya, yb, gate):
    n, d = ya.shape

    def body(ya_ref, yb_ref, ga_ref, gb_ref, o_ref):
        o_ref[...] = _bf(_sigmoid(ga_ref[...]) * ya_ref[...] + _sigmoid(gb_ref[...]) * yb_ref[...])

    row = pl.BlockSpec((ROWS, d), lambda i: (i, 0))
    return pl.pallas_call(
        body, name="merge_fwd", grid=(n // ROWS,),
        in_specs=[row, row, row, pl.BlockSpec((ROWS, d), lambda i: (i, 1))], out_specs=row,
        out_shape=jax.ShapeDtypeStruct((n, d), BF16),
        compiler_params=_cparams(("parallel",)),
    )(ya, yb, gate, gate)


def _merge_bwd(dm, ya, yb, gate):
    n, d = ya.shape

    def body(dm_ref, ya_ref, yb_ref, ga_ref, gb_ref, dya_ref, dyb_ref, dga_ref, dgb_ref):
        dmv = dm_ref[...]
        sa, sb = _sigmoid(ga_ref[...]), _sigmoid(gb_ref[...])
        dya_ref[...] = _bf(dmv * sa)
        dyb_ref[...] = _bf(dmv * sb)
        dga_ref[...] = _bf(dmv * ya_ref[...] * sa * (1.0 - sa))
        dgb_ref[...] = _bf(dmv * yb_ref[...] * sb * (1.0 - sb))

    row = pl.BlockSpec((ROWS, d), lambda i: (i, 0))
    o = jax.ShapeDtypeStruct((n, d), BF16)
    return pl.pallas_call(
        body, name="merge_bwd", grid=(n // ROWS,),
        in_specs=[row, row, row, row, pl.BlockSpec((ROWS, d), lambda i: (i, 1))], out_specs=[row] * 4,
        out_shape=[o] * 4,
        compiler_params=_cparams(("parallel",)),
    )(dm, ya, yb, gate, gate)


def _loss_bwd(out, target):
    n, d = out.shape

    def body(o_ref, t_ref, d_ref, d16_ref, l_ref):
        i = pl.program_id(0)
        err = o_ref[...] - t_ref[...]
        d_ref[...] = err * (1.0 / d)
        d16_ref[...] = _bf(err * (1.0 / d))

        @pl.when(i == 0)
        def _():
            l_ref[...] = jnp.zeros_like(l_ref)

        l_ref[...] += 0.5 * jnp.sum(jnp.mean(err * err, axis=1, keepdims=True), axis=0, keepdims=True)

    row = pl.BlockSpec((ROWS, d), lambda i: (i, 0))
    return pl.pallas_call(
        body, name="loss_bwd", grid=(n // ROWS,),
        in_specs=[row, row], out_specs=[row, row, pl.BlockSpec((8, LANES), lambda i: (0, 0))],
        out_shape=[jax.ShapeDtypeStruct((n, d), F32), jax.ShapeDtypeStruct((n, d), BF16),
                   jax.ShapeDtypeStruct((8, LANES), F32)],
        compiler_params=_cparams(("arbitrary",)),
    )(out, target)


PAD = 8


def _pad_zero(pad_ref):
    t = pad_ref.shape[0] - 2 * PAD
    pad_ref[0:PAD, :] = jnp.zeros((PAD, LANES), F32)
    pad_ref[PAD + t:2 * PAD + t, :] = jnp.zeros((PAD, LANES), F32)


def _shifted(pad_ref, s):
    t = pad_ref.shape[0] - 2 * PAD
    return pad_ref[PAD - s:PAD - s + t, :]


def _conv(x, w_ref, pad_ref):
    t = x.shape[0]
    pad_ref[PAD:PAD + t, :] = x
    y = _shifted(pad_ref, 3) * w_ref[0:1, :]
    y = y + _shifted(pad_ref, 2) * w_ref[1:2, :]
    y = y + _shifted(pad_ref, 1) * w_ref[2:3, :]
    return y + x * w_ref[3:4, :]


def _chunk_consts():
    r = lax.broadcasted_iota(jnp.int32, (CHUNK, CHUNK), 0)
    c = lax.broadcasted_iota(jnp.int32, (CHUNK, CHUNK), 1)
    incl, strict = r >= c, r > c
    return dict(incl=incl, strict=strict, trilf=incl.astype(F32), triuf=(r <= c).astype(F32),
                eye=(r == c).astype(F32))


class _V:
    def __init__(self, xs):
        self.xs = list(xs)

    def __add__(self, o):
        return _ap(lambda x, y: x + y, self, o)

    def __radd__(self, o):
        return _ap(lambda x, y: y + x, self, o)

    def __sub__(self, o):
        return _ap(lambda x, y: x - y, self, o)

    def __rsub__(self, o):
        return _ap(lambda x, y: y - x, self, o)

    def __mul__(self, o):
        return _ap(lambda x, y: x * y, self, o)

    def __rmul__(self, o):
        return _ap(lambda x, y: y * x, self, o)

    def __neg__(self):
        return _ap(lambda x: -x, self)

    def __getitem__(self, idx):
        return _ap(lambda x: x[idx], self)


def _ap(fn, *args):
    n = [len(a.xs) for a in args if isinstance(a, _V)]
    if not n:
        return fn(*args)
    return _V([fn(*[a.xs[i] if isinstance(a, _V) else a for a in args]) for i in range(n[0])])


def _vbf(x):
    return _ap(_bf, x)


def _vdot(a, b):
    return _ap(_dot, a, b)


def _vdot_nt(a, b):
    return _ap(_dot_nt, a, b)


def _vdot_tn(a, b):
    return _ap(_dot_tn, a, b)


def _vexp(x):
    return _ap(jnp.exp, x)


def _vsum(x, axis):
    return _ap(lambda v: jnp.sum(v, axis=axis, keepdims=True), x)


def _vcat(a, b, axis):
    return _ap(lambda x, y: jnp.concatenate([x, y], axis=axis), a, b)


def _vmask(mask, x):
    return _ap(lambda v: jnp.where(mask, v, 0.0), x)


def _split2(x):
    h = _vbf(x)
    return h, _vbf(x - _ap(lambda v: v.astype(F32), h))


def _dot3(a, b, kind=_vdot):
    ah, al = _split2(a)
    bh, bl = _split2(b)
    return kind(ah, bh) + (kind(ah, bl) + kind(al, bh))


def _split(x, terms):
    out = []
    for _ in range(terms):
        h = _vbf(x)
        out.append(h)
        x = x - _ap(lambda v: v.astype(F32), h)
    return out


def _dot_exact_l(m01, x, kind=_vdot, terms=2):
    mb = _bf(m01)
    parts = [kind(mb, xp) for xp in _split(x, terms)]
    return functools.reduce(lambda a, b: a + b, reversed(parts))


def _dot_exact_r(x, m01, kind=_vdot, terms=2):
    mb = _bf(m01)
    parts = [kind(xp, mb) for xp in _split(x, terms)]
    return functools.reduce(lambda a, b: a + b, reversed(parts))


def _inv_unit_lower(a, eye):
    levels = CHUNK.bit_length() - 1
    p = -a
    r = p + eye
    p = _dot3(p, p)
    for j in range(1, levels):
        if j < levels - 1:
            y = _dot3(p, _vcat(p, r, 1))
            p, r = y[:, 0:CHUNK], r + y[:, CHUNK:2 * CHUNK]
        else:
            r = r + _dot3(p, r)
    return r


def _gdn_chunk_pre(q, k, v, g128, g64, b128, b64, cs):
    incl = cs["incl"]
    big_g = _dot_exact_l(cs["trilf"], g128)
    gc = big_g[:, 0:CHUNK]
    gr = _dot_exact_r(g64, cs["triuf"], _vdot_tn)
    decay = _ap(lambda d: jnp.where(incl, jnp.exp(jnp.where(incl, d, 0.0)), 0.0), gc - gr)
    kb, qb = _vbf(k), _vbf(q)
    qkk = _vdot_nt(_vcat(qb, kb, 0), kb)
    qk, kk = qkk[0:CHUNK], qkk[CHUNK:2 * CHUNK]
    tm = _inv_unit_lower(_vmask(cs["strict"], b64 * kk * decay), cs["eye"])
    e_g = _vexp(big_g)
    wu = _dot3(tm, _vcat(v * b128, k * (b128 * e_g), 1))
    w, u = wu[:, 0:DH], wu[:, DH:2 * DH]
    g_last = _vsum(g128, 0)
    return dict(big_g=big_g, decay=decay, kk=kk, qk=qk, tm=tm, w=w, u=u, p=qk * decay, q_dec=q * e_g,
                k_dec=k * _vexp(g_last - big_g), dec=_vexp(g_last))


def _gdn_chunk_post(q, k, v, g128, b128, b64, s, ds_next, do, dv_new, big_g, decay, kk, qk, tm, u, v_new, cs):
    e_g = _vexp(big_g)
    vb = v * b128
    kbeta = k * (b128 * e_g)
    q_dec = q * e_g
    g_last = _vsum(g128, 0)
    ekg = _vexp(g_last - big_g)
    k_dec = k * ekg
    dec = _vexp(g_last)
    kb, qb, sb = _vbf(k), _vbf(q), _vbf(s)
    dob, dsb, vnb, dvnb = _vbf(do), _vbf(ds_next), _vbf(v_new), _vbf(dv_new)
    dp = _vmask(cs["incl"], _vdot_nt(dob, vnb))
    dq_dec = _vdot_nt(dob, sb)
    du = -_vdot_nt(dvnb, sb)
    ddec = _vsum(_vsum(s * ds_next, 1), 0)
    dk_dec = _vdot_nt(vnb, dsb)
    dwu = _vcat(dv_new, du, 1)
    dt = _dot3(dwu, _vcat(vb, kbeta, 1), _vdot_nt)
    dvk = _dot3(tm, dwu, _vdot_tn)
    dvb, dkbeta = dvk[:, 0:DH], dvk[:, DH:2 * DH]
    da = _vmask(cs["strict"], -_dot3(tm, _dot3(dt, tm, _vdot_nt), _vdot_tn))
    dkk = _vbf(da * b64 * decay)
    dqk = _vbf(dp * decay)
    ddd = (da * b64 * kk + dp * qk) * decay
    dq = _vdot(dqk, kb) + dq_dec * e_g
    dk = _vdot_tn(dqk, qb) + _vdot(dkk, kb) + _vdot_tn(dkk, kb) + dk_dec * ekg + dkbeta * (b128 * e_g)
    dv = dvb * b128
    dbeta = _vsum(da * kk * decay, 1) + _vsum(dvb * v, 1) + _vsum(dkbeta * k * e_g, 1)
    s_k = _vsum(dk_dec * k_dec, 1)
    dg_col = _vsum(ddd, 1) + _vsum(dq_dec * q_dec, 1) - s_k + _vsum(dkbeta * kbeta, 1)
    colsum = _dot_exact_r(ddd, jnp.ones((CHUNK, LANES), F32), _vdot_tn)
    dg_last = _vsum(s_k, 0) + ddec * dec
    dg = _dot_exact_l(cs["triuf"], dg_col - colsum) + dg_last
    return dq, dk, dv, dg, dbeta


def _stack_rows(vecs, nrows):
    row = lax.broadcasted_iota(jnp.int32, (nrows, LANES), 0)
    out = jnp.zeros((nrows, LANES), F32)
    for i, v in enumerate(vecs):
        out = out + jnp.where(row == i, jnp.broadcast_to(v, (nrows, LANES)), 0.0)
    return out


def _head_lane(x, lane_idx):
    lane = lax.broadcasted_iota(jnp.int32, x.shape, 1)
    return jnp.sum(jnp.where(lane == lane_idx, x, 0.0), axis=1, keepdims=True)


def _gdn_gates(ps, h, alog_ref, dtb_ref):
    ga = _head_lane(ps, LANE_GA + h)
    gb = _head_lane(ps, LANE_GB + h)
    a = jnp.exp(jnp.full((1, 1), alog_ref[0, h], F32))
    sp_in = ga + dtb_ref[0, h]
    g = -a * _softplus(sp_in)
    return g, _sigmoid(gb), a, sp_in


def _gdn_specs(b_loc, t):
    def col(off):
        return pl.BlockSpec((t, DH), lambda b, h: (b, off + h))

    ps_spec = pl.BlockSpec((t, LANES), lambda b, h: (b, 0))

    def wcol(off):
        return pl.BlockSpec((CONV_K, DH), lambda b, h: (0, off + h))

    smem = pl.BlockSpec(memory_space=pltpu.SMEM)
    vec = pl.BlockSpec((1, DH), lambda b, h: (0, 0))
    return col, ps_spec, wcol, smem, vec


def _gdn_fwd(pg, ps, convw, a_log, dt_bias, gnorm, b_loc, t):
    n = b_loc * t
    assert t % (CHUNK * GDN_GROUP) == 0 and CHUNK == LANES, (t, CHUNK, GDN_GROUP)
    nc = t // CHUNK
    col, ps_spec, wcol, smem, vec = _gdn_specs(b_loc, t)

    def body(q_ref, k_ref, v_ref, z_ref, ps_ref, wq_ref, wk_ref, wv_ref, alog_ref, dtb_ref, gn_ref,
             oa_ref, oraw_ref, s_ref, qn, kn, vv, g128, g64, b128, b64, uq_s, p_s, kd_s, dec_s, pad_s):
        h = pl.program_id(1)
        g, beta, _, _ = _gdn_gates(ps_ref[...], h, alog_ref, dtb_ref)
        g128[...] = jnp.broadcast_to(g, (t, LANES))
        g64[...] = jnp.broadcast_to(g, (t, CHUNK))
        b128[...] = jnp.broadcast_to(beta, (t, LANES))
        b64[...] = jnp.broadcast_to(beta, (t, CHUNK))
        _pad_zero(pad_s)
        pq = _conv(q_ref[...], wq_ref, pad_s)
        yq = pq * _sigmoid(pq)
        qn[...] = yq * (lax.rsqrt(jnp.sum(yq * yq, axis=1, keepdims=True) + EPS) * (DH ** -0.5))
        pk = _conv(k_ref[...], wk_ref, pad_s)
        yk = pk * _sigmoid(pk)
        kn[...] = yk * lax.rsqrt(jnp.sum(yk * yk, axis=1, keepdims=True) + EPS)
        pv = _conv(v_ref[...], wv_ref, pad_s)
        vv[...] = pv * _sigmoid(pv)
        cs = _chunk_consts()

        def pre_group(gi, _):
            idx = [gi * GDN_GROUP + c for c in range(GDN_GROUP)]
            rows = [pl.ds(pl.multiple_of(i * CHUNK, CHUNK), CHUNK) for i in idx]
            ins = [_V([ref[r, :] for r in rows]) for ref in (qn, kn, vv, g128, g64, b128, b64)]
            f = _gdn_chunk_pre(*ins, cs)
            for c, (i, r) in enumerate(zip(idx, rows)):
                vv[r, :] = f["w"].xs[c]
                uq_s[i, 0:CHUNK, :] = _bf(f["u"].xs[c])
                uq_s[i, CHUNK:2 * CHUNK, :] = _bf(f["q_dec"].xs[c])
                p_s[r, :] = _bf(f["p"].xs[c])
                kd_s[r, :] = _bf(f["k_dec"].xs[c])
                dec_s[pl.ds(pl.multiple_of(i * 8, 8), 8), :] = jnp.broadcast_to(f["dec"].xs[c], (8, LANES))
            return 0

        lax.fori_loop(0, nc // GDN_GROUP, pre_group, 0)

        def chunk(i, s):
            r = pl.ds(pl.multiple_of(i * CHUNK, CHUNK), CHUNK)
            us = _dot(uq_s[i], _bf(s))
            vnb = _bf(vv[r, :] - us[0:CHUNK])
            oraw_ref[r, :] = us[CHUNK:2 * CHUNK] + _dot(p_s[r, :], vnb)
            s_ref[0, 0, i] = s
            return s * dec_s[pl.ds(pl.multiple_of(i * 8, 8), 1), :] + _dot_tn(kd_s[r, :], vnb)

        lax.fori_loop(0, nc, chunk, jnp.zeros((DH, DH), F32))
        o = oraw_ref[...]
        rr = lax.rsqrt(jnp.mean(o * o, axis=1, keepdims=True) + EPS)
        z = z_ref[...]
        oa_ref[...] = _bf((o * rr * gn_ref[...]) * (z * _sigmoid(z)))

    return pl.pallas_call(
        body, name="gdn_fwd", grid=(b_loc, HEADS),
        in_specs=[col(0), col(HEADS), col(2 * HEADS), col(3 * HEADS), ps_spec, wcol(0), wcol(HEADS), wcol(2 * HEADS),
                  smem, smem, vec],
        out_specs=[pl.BlockSpec((t, DH), lambda b, h: (b, h)), pl.BlockSpec((t, DH), lambda b, h: (b, h)),
                   pl.BlockSpec((1, 1, nc, DH, DH), lambda b, h: (b, h, 0, 0, 0))],
        out_shape=[jax.ShapeDtypeStruct((n, HEADS * DH), BF16), jax.ShapeDtypeStruct((n, HEADS * DH), F32),
                   jax.ShapeDtypeStruct((b_loc, HEADS, nc, DH, DH), F32)],
        scratch_shapes=([pltpu.VMEM((t, DH), F32)] * 3 + [pltpu.VMEM((t, LANES), F32), pltpu.VMEM((t, CHUNK), F32)] * 2
                        + [pltpu.VMEM((nc, 2 * CHUNK, DH), BF16), pltpu.VMEM((t, CHUNK), BF16), pltpu.VMEM((t, DH), BF16),
                           pltpu.VMEM((8 * nc, LANES), F32), pltpu.VMEM((t + 2 * PAD, LANES), F32)]),
        compiler_params=_cparams(("arbitrary", "arbitrary")),
    )(pg, pg, pg, pg, ps, convw, convw, convw, a_log, dt_bias, gnorm)


def _gdn_bwd(pg, ps, convw, a_log, dt_bias, gnorm, d_oa, o_raw, s_all, b_loc, t):
    n = b_loc * t
    assert t % (CHUNK * GDN_GROUP) == 0 and CHUNK == LANES, (t, CHUNK, GDN_GROUP)
    nc = t // CHUNK
    col, ps_spec, wcol, smem, vec = _gdn_specs(b_loc, t)

    def body(q_ref, k_ref, v_ref, z_ref, ps_ref, wq_ref, wk_ref, wv_ref, alog_ref, dtb_ref, gn_ref,
             doa_ref, oraw_ref, s_ref,
             dq_ref, dk_ref, dv_ref, dz_ref, dps_ref, dcw_ref, dsm_ref,
             qn, kn, vv, g128, g64, b128, b64, do_s, bg_s, u_s, vn_s, dvn_s, dcy_s, kk_s, qk_s, tm_s, dsn_s, pad_s):
        b, h = pl.program_id(0), pl.program_id(1)
        g, beta, _, _ = _gdn_gates(ps_ref[...], h, alog_ref, dtb_ref)
        g128[...] = jnp.broadcast_to(g, (t, LANES))
        g64[...] = jnp.broadcast_to(g, (t, CHUNK))
        b128[...] = jnp.broadcast_to(beta, (t, LANES))
        b64[...] = jnp.broadcast_to(beta, (t, CHUNK))
        _pad_zero(pad_s)

        def prep(x_ref, w_ref):
            p = _conv(x_ref[...], w_ref, pad_s)
            sg = _sigmoid(p)
            return p, sg, p * sg

        _, _, yq = prep(q_ref, wq_ref)
        qn[...] = yq * (lax.rsqrt(jnp.sum(yq * yq, axis=1, keepdims=True) + EPS) * (DH ** -0.5))
        _, _, yk = prep(k_ref, wk_ref)
        kn[...] = yk * lax.rsqrt(jnp.sum(yk * yk, axis=1, keepdims=True) + EPS)
        _, _, yv = prep(v_ref, wv_ref)
        vv[...] = yv

        o = oraw_ref[...]
        z = z_ref[...]
        doa = doa_ref[...]
        gn = gn_ref[...]
        ro = lax.rsqrt(jnp.mean(o * o, axis=1, keepdims=True) + EPS)
        sz = _sigmoid(z)
        dz_ref[...] = _bf(doa * (o * ro * gn) * (sz * (1.0 + z * (1.0 - sz))))
        dn = doa * (z * sz)
        dgn = jnp.sum(dn * o * ro, axis=0, keepdims=True)
        gy = dn * gn
        do_s[...] = ro * gy - o * (ro * ro * ro * (1.0 / DH)) * jnp.sum(gy * o, axis=1, keepdims=True)

        cs = _chunk_consts()

        def pre_group(gi, _):
            idx = [gi * GDN_GROUP + c for c in range(GDN_GROUP)]
            rows = [pl.ds(pl.multiple_of(i * CHUNK, CHUNK), CHUNK) for i in idx]
            ins = [_V([ref[r, :] for r in rows]) for ref in (qn, kn, vv, g128, g64, b128, b64)]
            states = _V([_bf(s_ref[0, 0, i]) for i in idx])
            f = _gdn_chunk_pre(*ins, cs)
            v_new = f["w"] - _vdot(_vbf(f["u"]), states)
            for c, r in enumerate(rows):
                bg_s[r, :] = f["big_g"].xs[c]
                u_s[r, :] = f["u"].xs[c]
                vn_s[r, :] = v_new.xs[c]
                dcy_s[r, :] = f["decay"].xs[c]
                kk_s[r, :] = f["kk"].xs[c]
                qk_s[r, :] = f["qk"].xs[c]
                tm_s[r, :] = f["tm"].xs[c]
            return 0

        lax.fori_loop(0, nc // GDN_GROUP, pre_group, 0)

        def chunk(j, ds):
            i = nc - 1 - j
            r = pl.ds(pl.multiple_of(i * CHUNK, CHUNK), CHUNK)
            big_g = bg_s[r, :]
            g_last = jnp.sum(g128[r, :], axis=0, keepdims=True)
            dob = _bf(do_s[r, :])
            dv_new = (_dot_tn(_bf(qk_s[r, :] * dcy_s[r, :]), dob)
                      + _dot(_bf(kn[r, :] * jnp.exp(g_last - big_g)), _bf(ds)))
            dvn_s[r, :] = dv_new
            dsn_s[i] = ds
            return (_dot_tn(_bf(qn[r, :] * jnp.exp(big_g)), dob) + jnp.exp(g_last) * ds
                    - _dot_tn(_bf(u_s[r, :]), _bf(dv_new)))

        lax.fori_loop(0, nc, chunk, jnp.zeros((DH, DH), F32))

        def post_group(gi, _):
            idx = [gi * GDN_GROUP + c for c in range(GDN_GROUP)]
            rows = [pl.ds(pl.multiple_of(i * CHUNK, CHUNK), CHUNK) for i in idx]
            def rows_of(ref):
                return _V([ref[r, :] for r in rows])

            dq, dk, dv, dg, dbeta = _gdn_chunk_post(
                rows_of(qn), rows_of(kn), rows_of(vv), rows_of(g128), rows_of(b128), rows_of(b64),
                _V([s_ref[0, 0, i] for i in idx]), _V([dsn_s[i] for i in idx]), rows_of(do_s), rows_of(dvn_s),
                rows_of(bg_s), rows_of(dcy_s), rows_of(kk_s), rows_of(qk_s), rows_of(tm_s), rows_of(u_s), rows_of(vn_s),
                cs)
            for c, r in enumerate(rows):
                qn[r, :] = dq.xs[c]
                kn[r, :] = dk.xs[c]
                vv[r, :] = dv.xs[c]
                g128[r, :] = dg.xs[c]
                b128[r, :] = jnp.broadcast_to(dbeta.xs[c], (CHUNK, LANES))
            return 0

        lax.fori_loop(0, nc // GDN_GROUP, post_group, 0)
        dqh, dkh, dvh = qn, kn, vv

        g, beta, a, sp_in = _gdn_gates(ps_ref[...], h, alog_ref, dtb_ref)
        dg = g128[...]
        d_ga = dg * (-a) * _sigmoid(sp_in)
        d_alog = jnp.sum(dg * g, axis=0, keepdims=True)
        d_dtb = jnp.sum(d_ga, axis=0, keepdims=True)
        d_gb = b128[...] * (beta * (1.0 - beta))
        lane = lax.broadcasted_iota(jnp.int32, (t, LANES), 1)
        contrib = jnp.where(lane == LANE_GA + h, d_ga, 0.0) + jnp.where(lane == LANE_GB + h, d_gb, 0.0)

        @pl.when(h == 0)
        def _():
            dps_ref[...] = jnp.zeros_like(dps_ref)

        dps_ref[...] += contrib

        lane1 = lax.broadcasted_iota(jnp.int32, (1, LANES), 1)
        small = _stack_rows([jnp.where(lane1 == h, d_alog, 0.0), jnp.where(lane1 == h, d_dtb, 0.0), dgn], 8)

        @pl.when((b == 0) & (h == 0))
        def _():
            dsm_ref[...] = jnp.zeros_like(dsm_ref)
            dcw_ref[...] = jnp.zeros_like(dcw_ref)

        dsm_ref[...] += small

        def conv_bwd(dp, x, w_ref, slot):
            dw = _stack_rows([jnp.sum(dp * _shifted(pad_s, 3), axis=0, keepdims=True),
                              jnp.sum(dp * _shifted(pad_s, 2), axis=0, keepdims=True),
                              jnp.sum(dp * _shifted(pad_s, 1), axis=0, keepdims=True),
                              jnp.sum(dp * x, axis=0, keepdims=True)], CONV_K)
            dcw_ref[slot] += dw
            pad_s[PAD:PAD + t, :] = dp
            dx = _shifted(pad_s, -3) * w_ref[0:1, :]
            dx = dx + _shifted(pad_s, -2) * w_ref[1:2, :]
            dx = dx + _shifted(pad_s, -1) * w_ref[2:3, :]
            return dx + dp * w_ref[3:4, :]

        def l2_bwd(dqn, y, c):
            r = lax.rsqrt(jnp.sum(y * y, axis=1, keepdims=True) + EPS)
            s1 = jnp.sum(dqn * y, axis=1, keepdims=True)
            return c * r * dqn - (c * r * r * r) * s1 * y

        def silu_bwd(p, sg):
            return sg * (1.0 + p * (1.0 - sg))

        pq, sq, yq = prep(q_ref, wq_ref)
        dq_ref[...] = _bf(conv_bwd(l2_bwd(dqh[...], yq, DH ** -0.5) * silu_bwd(pq, sq), q_ref[...], wq_ref, h))
        pk, sk, yk = prep(k_ref, wk_ref)
        dk_ref[...] = _bf(conv_bwd(l2_bwd(dkh[...], yk, 1.0) * silu_bwd(pk, sk), k_ref[...], wk_ref, HEADS + h))
        pv, sv, _ = prep(v_ref, wv_ref)
        dv_ref[...] = _bf(conv_bwd(dvh[...] * silu_bwd(pv, sv), v_ref[...], wv_ref, 2 * HEADS + h))

    blk = pl.BlockSpec((t, DH), lambda b, h: (b, h))
    ob = jax.ShapeDtypeStruct((n, HEADS * DH), BF16)
    return pl.pallas_call(
        body, name="gdn_bwd", grid=(b_loc, HEADS),
        in_specs=[col(0), col(HEADS), col(2 * HEADS), col(3 * HEADS), ps_spec, wcol(0), wcol(HEADS), wcol(2 * HEADS),
                  smem, smem, vec, blk, blk, pl.BlockSpec((1, 1, nc, DH, DH), lambda b, h: (b, h, 0, 0, 0))],
        out_specs=[blk, blk, blk, blk, ps_spec,
                   pl.BlockSpec((3 * HEADS, CONV_K, DH), lambda b, h: (0, 0, 0)),
                   pl.BlockSpec((8, LANES), lambda b, h: (0, 0))],
        out_shape=[ob, ob, ob, ob, jax.ShapeDtypeStruct((n, LANES), F32),
                   jax.ShapeDtypeStruct((3 * HEADS, CONV_K, DH), F32), jax.ShapeDtypeStruct((8, LANES), F32)],
        scratch_shapes=([pltpu.VMEM((t, DH), F32)] * 3 + [pltpu.VMEM((t, LANES), F32), pltpu.VMEM((t, CHUNK), F32)] * 2
                        + [pltpu.VMEM((t, DH), F32)] * 5 + [pltpu.VMEM((t, CHUNK), F32)] * 4
                        + [pltpu.VMEM((nc, DH, DH), F32), pltpu.VMEM((t + 2 * PAD, LANES), F32)]),
        compiler_params=_cparams(("arbitrary", "arbitrary")),
    )(pg, pg, pg, pg, ps, convw, convw, convw, a_log, dt_bias, gnorm, d_oa, o_raw, s_all)


def _fox_prologue(q_ref, k_ref, v_ref, ps_ref, fb_ref, gq_ref, gk_ref, h, t, qs, ks, vs, ccol, crow):
    nb = t // FOX_BLOCK
    q, k = q_ref[...], k_ref[...]
    rq = lax.rsqrt(jnp.mean(q * q, axis=1, keepdims=True) + EPS)
    rk = lax.rsqrt(jnp.mean(k * k, axis=1, keepdims=True) + EPS)
    qs[...] = _bf(q * rq * gq_ref[...])
    ks[...] = _bf(k * rk * gk_ref[...])
    vs[...] = _bf(v_ref[...])
    f_in = _head_lane(ps_ref[...], LANE_FF + h) + fb_ref[0, h]
    ccol[...] = jnp.broadcast_to(-_softplus(-f_in), (t, LANES))
    r = lax.broadcasted_iota(jnp.int32, (FOX_BLOCK, FOX_BLOCK), 0)
    c = lax.broadcasted_iota(jnp.int32, (FOX_BLOCK, FOX_BLOCK), 1)
    trilf, triuf = (r >= c).astype(F32), (r <= c).astype(F32)
    blocks = [pl.ds(j * FOX_BLOCK, FOX_BLOCK) for j in range(nb)]
    lfs = _V([ccol[rb, :] for rb in blocks])
    cc = _dot_exact_l(trilf, lfs, terms=3)
    cr = _dot_exact_r(lfs, triuf, _vdot_tn, terms=3)
    sums = _vsum(lfs, 0)
    carry = jnp.zeros((1, LANES), F32)
    for j, rb in enumerate(blocks):
        ccol[rb, :] = cc.xs[j] + carry
        crow[j] = (cr.xs[j] + carry)[0:8]
        carry = carry + sums.xs[j]
    return rq, rk, f_in


def _fox_scores(q_rows, k_rows, cc, cr, row0, col0):
    s = _dot_nt(q_rows, k_rows) * (DH ** -0.5) + cc - cr
    r = lax.broadcasted_iota(jnp.int32, s.shape, 0)
    c = lax.broadcasted_iota(jnp.int32, s.shape, 1)
    return jnp.where(row0 + r >= col0 + c, s, NEG)


def _fox_specs(t):
    def col(off):
        return pl.BlockSpec((t, DH), lambda b, h: (b, off + h))

    ps_spec = pl.BlockSpec((t, LANES), lambda b, h: (b, 0))
    smem = pl.BlockSpec(memory_space=pltpu.SMEM)
    vec = pl.BlockSpec((1, DH), lambda b, h: (0, 0))
    blk = pl.BlockSpec((t, DH), lambda b, h: (b, h))
    return col, ps_spec, smem, vec, blk


def _fox_fwd(pf, ps, f_bias, gq, gk, b_loc, t):
    n = b_loc * t
    nb = t // FOX_BLOCK
    assert t % FOX_TILE == 0 and FOX_TILE % FOX_SHORT == 0, (t, FOX_TILE, FOX_SHORT)
    kt = FOX_TILE
    nsub = kt // FOX_BLOCK
    col, ps_spec, smem, vec, blk = _fox_specs(t)

    def body(q_ref, k_ref, v_ref, ps_ref, fb_ref, gq_ref, gk_ref, o_ref, lse_ref, qs, ks, vs, ccol, crow):
        h = pl.program_id(1)
        _fox_prologue(q_ref, k_ref, v_ref, ps_ref, fb_ref, gq_ref, gk_ref, h, t, qs, ks, vs, ccol, crow)

        def qblock(i, _):
            ri = pl.ds(pl.multiple_of(i * FOX_SHORT, FOX_SHORT), FOX_SHORT)
            qi = qs[ri, :]
            cc = jnp.concatenate([ccol[ri, :]] * nsub, axis=1)

            def ktile(j, carry):
                m, l, acc = carry
                rj = pl.ds(pl.multiple_of(j * kt, kt), kt)
                cr = jnp.concatenate([crow[j * nsub + u, 0:1, :] for u in range(nsub)], axis=1)
                s = _fox_scores(qi, ks[rj, :], cc, cr, i * FOX_SHORT, j * kt)
                m_new = jnp.maximum(m, jnp.max(s, axis=1, keepdims=True))
                p = jnp.exp(s - m_new)
                alpha = jnp.exp(m - m_new)
                l = alpha * l + jnp.sum(p, axis=1, keepdims=True)
                acc = alpha * acc + _dot(_bf(p), vs[rj, :])
                return m_new, l, acc

            m, l, acc = lax.fori_loop(0, (i * FOX_SHORT) // kt + 1, ktile, (jnp.full((FOX_SHORT, 1), NEG, F32),
                                                                            jnp.zeros((FOX_SHORT, 1), F32),
                                                                            jnp.zeros((FOX_SHORT, DH), F32)))
            o_ref[ri, :] = acc / l
            lse_ref[ri, :] = jnp.broadcast_to(m + jnp.log(l), (FOX_SHORT, LANES))
            return 0

        lax.fori_loop(0, t // FOX_SHORT, qblock, 0)

    o = jax.ShapeDtypeStruct((n, HEADS * DH), F32)
    return pl.pallas_call(
        body, name="fox_fwd", grid=(b_loc, HEADS),
        in_specs=[col(0), col(HEADS), col(2 * HEADS), ps_spec, smem, vec, vec],
        out_specs=[blk, blk], out_shape=[o, o],
        scratch_shapes=[pltpu.VMEM((t, DH), BF16)] * 3 + [pltpu.VMEM((t, LANES), F32), pltpu.VMEM((nb, 8, LANES), F32)],
        compiler_params=_cparams(("arbitrary", "arbitrary")),
    )(pf, pf, pf, ps, f_bias, gq, gk)


def _fox_bwd(pf, ps, f_bias, gq, gk, d_ob, ob, lse, dps_in, b_loc, t):
    n = b_loc * t
    nb = t // FOX_BLOCK
    assert t % FOX_TILE == 0 and FOX_TILE % FOX_SHORT == 0, (t, FOX_TILE, FOX_SHORT)
    qt = FOX_TILE
    scale = DH ** -0.5
    col, ps_spec, smem, vec, blk = _fox_specs(t)

    def body(q_ref, k_ref, v_ref, ps_ref, fb_ref, gq_ref, gk_ref, do_ref, o_ref, lse_ref, dpsi_ref,
             dq_ref, dk_ref, dv_ref, dps_ref, dsm_ref, qs, ks, vs, ccol, crow, dos, dl, dqa, dcr, dcq):
        b, h = pl.program_id(0), pl.program_id(1)
        rq, _, f_in = _fox_prologue(q_ref, k_ref, v_ref, ps_ref, fb_ref, gq_ref, gk_ref, h, t, qs, ks, vs, ccol, crow)
        dov = do_ref[...]
        dos[...] = _bf(dov)
        dl[...] = jnp.broadcast_to(jnp.sum(dov * o_ref[...], axis=1, keepdims=True), (t, LANES))
        dqa[...] = jnp.zeros_like(dqa)
        dcq[...] = jnp.zeros_like(dcq)
        gkv = gk_ref[...]

        ksub = FOX_SHORT // FOX_BLOCK

        def kblock(j, dgk):
            rj = pl.ds(pl.multiple_of(j * FOX_SHORT, FOX_SHORT), FOX_SHORT)
            kj, vj = ks[rj, :], vs[rj, :]
            cr = jnp.concatenate([crow[j * ksub + u, 0:1, :] for u in range(ksub)], axis=1)

            def wide(x):
                return jnp.concatenate([x] * ksub, axis=1)

            def qtile(i, carry):
                dk_acc, dv_acc, dc = carry
                ri = pl.ds(pl.multiple_of(i * qt, qt), qt)
                qi, doi = qs[ri, :], dos[ri, :]
                s = _fox_scores(qi, kj, wide(ccol[ri, :]), cr, i * qt, j * FOX_SHORT)
                p = jnp.exp(s - wide(lse_ref[ri, :]))
                ds = p * (_dot_nt(doi, vj) - wide(dl[ri, :]))
                dsb = _bf(ds)
                dqa[ri, :] += _dot(dsb, kj)
                dcq[ri, :] += jnp.broadcast_to(jnp.sum(ds, axis=1, keepdims=True), (qt, LANES))
                return (dk_acc + _dot_tn(dsb, qi), dv_acc + _dot_tn(_bf(p), doi),
                        dc - jnp.sum(ds, axis=0, keepdims=True))

            z = jnp.zeros((FOX_SHORT, DH), F32)
            dk_acc, dv_acc, dc = lax.fori_loop((j * FOX_SHORT) // qt, t // qt, qtile,
                                               (z, z, jnp.zeros((1, FOX_SHORT), F32)))
            dv_ref[rj, :] = _bf(dv_acc)
            for u in range(ksub):
                dcr[pl.ds(pl.multiple_of((j * ksub + u) * 8, 8), 8), :] = jnp.broadcast_to(
                    dc[:, u * FOX_BLOCK:(u + 1) * FOX_BLOCK], (8, LANES))
            kraw = k_ref[rj, :]
            rk = lax.rsqrt(jnp.mean(kraw * kraw, axis=1, keepdims=True) + EPS)
            dkn = dk_acc * scale
            gy = dkn * gkv
            dk_ref[rj, :] = _bf(rk * gy - kraw * (rk * rk * rk * (1.0 / DH)) * jnp.sum(gy * kraw, axis=1, keepdims=True))
            return dgk + jnp.sum(dkn * kraw * rk, axis=0, keepdims=True)

        dgk = lax.fori_loop(0, t // FOX_SHORT, kblock, jnp.zeros((1, DH), F32))

        q = q_ref[...]
        dqn = dqa[...] * scale
        gy = dqn * gq_ref[...]
        dq_ref[...] = _bf(rq * gy - q * (rq * rq * rq * (1.0 / DH)) * jnp.sum(gy * q, axis=1, keepdims=True))
        dgq = jnp.sum(dqn * q * rq, axis=0, keepdims=True)

        r = lax.broadcasted_iota(jnp.int32, (FOX_BLOCK, FOX_BLOCK), 0)
        c = lax.broadcasted_iota(jnp.int32, (FOX_BLOCK, FOX_BLOCK), 1)
        triuf = (r <= c).astype(F32)

        def rev(jj, carry):
            j = nb - 1 - jj
            rows = pl.ds(pl.multiple_of(j * FOX_BLOCK, FOX_BLOCK), FOX_BLOCK)
            rowv = dcr[pl.ds(pl.multiple_of(j * 8, 8), 1), :]
            colv = jnp.sum(jnp.where(c >= r, jnp.broadcast_to(rowv, (FOX_BLOCK, LANES)), 0.0), axis=1, keepdims=True)
            qcol = dcq[rows, :]
            dl[rows, :] = colv + _dot_exact_l(triuf, qcol, terms=3) + carry
            return carry + jnp.sum(rowv, axis=1, keepdims=True) + jnp.sum(qcol, axis=0, keepdims=True)

        lax.fori_loop(0, nb, rev, jnp.zeros((1, LANES), F32))
        d_ff = dl[...] * _sigmoid(-f_in)
        lane = lax.broadcasted_iota(jnp.int32, (t, LANES), 1)

        @pl.when(h == 0)
        def _():
            dps_ref[...] = dpsi_ref[...]

        dps_ref[...] += jnp.where(lane == LANE_FF + h, d_ff, 0.0)

        lane1 = lax.broadcasted_iota(jnp.int32, (1, LANES), 1)
        d_fb = jnp.sum(d_ff, axis=0, keepdims=True)
        small = _stack_rows([dgq, dgk, jnp.where(lane1 == h, d_fb, 0.0)], 8)

        @pl.when((b == 0) & (h == 0))
        def _():
            dsm_ref[...] = jnp.zeros_like(dsm_ref)

        dsm_ref[...] += small

    ob_ = jax.ShapeDtypeStruct((n, HEADS * DH), BF16)
    return pl.pallas_call(
        body, name="fox_bwd", grid=(b_loc, HEADS),
        in_specs=[col(0), col(HEADS), col(2 * HEADS), ps_spec, smem, vec, vec, blk, blk, blk, ps_spec],
        out_specs=[blk, blk, blk, ps_spec, pl.BlockSpec((8, LANES), lambda b, h: (0, 0))],
        out_shape=[ob_, ob_, ob_, jax.ShapeDtypeStruct((n, LANES), F32), jax.ShapeDtypeStruct((8, LANES), F32)],
        scratch_shapes=([pltpu.VMEM((t, DH), BF16)] * 3 + [pltpu.VMEM((t, LANES), F32), pltpu.VMEM((nb, 8, LANES), F32)]
                        + [pltpu.VMEM((t, DH), BF16), pltpu.VMEM((t, LANES), F32), pltpu.VMEM((t, DH), F32),
                           pltpu.VMEM((8 * nb, LANES), F32), pltpu.VMEM((t, LANES), F32)]),
        compiler_params=_cparams(("arbitrary", "arbitrary")),
    )(pf, pf, pf, ps, f_bias, gq, gk, d_ob, ob, lse, dps_in)


class _NoExchange:
    def late_weights(self, after):
        return {}

    def grads_ready(self, grads, tie):
        return tie


def _local_step(x, target, w, b_loc, t, comm=None):
    comm = comm or _NoExchange()
    w = dict(w)
    xf = x
    u = _rms_fwd(xf, w["norm_mix_g"], "rms_mix")
    pg = _mm(u, w["w_gdn"], name="proj_gdn")
    pf = _mm(u, w["w_fox"], name="proj_fox")
    pgate = _mm(u, w["w_gate"], name="proj_gate")
    ps = _mm(u, w["w_small"], name="proj_small")
    oa, o_raw, s_all = _gdn_fwd(pg, ps, w["conv_w"], w["a_log"], w["dt_bias"], w["gdn_norm_g"], b_loc, t)
    ob, lse = _fox_fwd(pf, ps, w["f_bias"], w["fox_q_norm_g"], w["fox_k_norm_g"], b_loc, t)
    w.update(comm.late_weights(ob))
    ya = _mm(oa, w["w_proj_gdn"], name="proj_a")
    yb = _mm(ob, w["w_proj_fox"], name="proj_b")
    merged = _merge_fwd(ya, yb, pgate)
    h = _mm(merged, w["w_out"], name="proj_out", epi=lambda acc, xr: acc + xr, extras=(xf,))
    hn = _rms_fwd(h, w["norm_mlp_g"], "rms_mlp")
    up, act = _mm(hn, w["w_up"], name="mlp_up", out_dtype=BF16, out2=(_relu2, BF16))
    out = _mm(act, w["w_down"], name="mlp_down", epi=lambda acc, hr: acc + hr, extras=(h,))
    d_out, d_out16, loss_blk = _loss_bwd(out, target)

    g = {}
    g["w_down"] = _mm(act, d_out16, name="dw_down", ta=True, out_dtype=BF16)
    d_up = _mm(d_out16, w["w_down"], name="d_up", tb=True, out_dtype=BF16,
               epi=lambda acc, upr: acc * (2.0 * jnp.maximum(upr.astype(F32), 0.0)), extras=(up,))
    g["w_up"] = _mm(hn, d_up, name="dw_up", ta=True, out_dtype=BF16)
    mlp_gain = comm.grads_ready({"w_down": g["w_down"], "w_up": g["w_up"]}, w["norm_mlp_g"])
    d_hn = _mm(d_up, w["w_up"], name="d_hn", tb=True)
    dh, dh16, g["norm_mlp_g"] = _rms_bwd(d_hn, h, mlp_gain, d_out, "rms_mlp_bwd")
    g["w_out"] = _mm(merged, dh16, name="dw_out", ta=True, out_dtype=BF16)
    dm = _mm(dh16, w["w_out"], name="d_merged", tb=True)
    dya, dyb, dgate_a, dgate_b = _merge_bwd(dm, ya, yb, pgate)
    g["w_proj_gdn"] = _mm(oa, dya, name="dw_proj_a", ta=True, out_dtype=BF16)
    g["w_proj_fox"] = _mm(ob, dyb, name="dw_proj_b", ta=True, out_dtype=BF16)
    gdn_gain = comm.grads_ready({"w_out": g["w_out"], "w_proj_gdn": g["w_proj_gdn"], "w_proj_fox": g["w_proj_fox"]},
                                w["gdn_norm_g"])
    d_oa = _mm(dya, w["w_proj_gdn"], name="d_oa", tb=True)
    d_ob = _mm(dyb, w["w_proj_fox"], name="d_ob", tb=True)
    dgq, dgk, dgv, dgz, dps, dcw, gdn_small = _gdn_bwd(pg, ps, w["conv_w"], w["a_log"], w["dt_bias"], gdn_gain,
                                                       d_oa, o_raw, s_all, b_loc, t)
    dfq, dfk, dfv, dps, fox_small = _fox_bwd(pf, ps, w["f_bias"], w["fox_q_norm_g"], w["fox_k_norm_g"],
                                             d_ob, ob, lse, dps, b_loc, t)
    segs = [(dgq, "w_gdn", 0), (dgk, "w_gdn", 1024), (dgv, "w_gdn", 2048), (dgz, "w_gdn", 3072),
            (dfq, "w_fox", 0), (dfk, "w_fox", 1024), (dfv, "w_fox", 2048),
            (dgate_a, "w_gate", 0), (dgate_b, "w_gate", 1024)]
    dws = [_mm(u, dps, name="dw_small", ta=True, out_dtype=BF16)]
    dws += [_mm(u, dseg, name=f"dw_in_{idx}", ta=True, out_dtype=BF16) for idx, (dseg, _, _) in enumerate(segs)]
    g["w_in_parts"] = dws
    mix_gain = comm.grads_ready({"w_in_parts": dws}, w["norm_mix_g"])
    du = _du_all(dps, w["w_small"], segs, w)
    grad_x, _, g["norm_mix_g"] = _rms_bwd(du, xf, mix_gain, dh, "rms_mix_bwd")
    g["conv"] = dcw
    g["gdn_small"] = gdn_small
    g["fox_small"] = fox_small
    return loss_blk, grad_x, g


def _position():
    x, y, c = lax.axis_index("x"), lax.axis_index("y"), lax.axis_index("c")
    return x, y, c


def _to_bf16(arrs, name):
    n = len(arrs)

    def body(*refs):
        for i in range(n):
            refs[n + i][...] = _bf(refs[i][...])

    return pl.pallas_call(
        body, name=name,
        out_shape=[jax.ShapeDtypeStruct(a.shape, BF16) for a in arrs],
        compiler_params=_cparams(),
    )(*arrs)


def _all_gather(arrs, name):
    n = len(arrs)
    hbm = pl.BlockSpec(memory_space=pl.ANY)

    def body(*refs):
        ins, outs = refs[:n], refs[n:2 * n]
        send, recv, loc = refs[2 * n:]
        x, y, c = _position()
        me = 4 * x + 2 * y + c
        sibling = (x, y, 1 - c)
        chips = [(1 - x, y), (x, 1 - y), (1 - x, 1 - y)]

        def idx(px, py, pc):
            return 4 * px + 2 * py + pc

        def cp(a, k, block, to, src=None):
            return pltpu.make_async_remote_copy(
                src_ref=outs[a].at[block] if src is None else src, dst_ref=outs[a].at[block],
                send_sem=send.at[a, k], recv_sem=recv.at[a, k], device_id=to, device_id_type=MESH)

        mine = [pltpu.make_async_copy(ins[a], outs[a].at[me], loc.at[a]) for a in range(n)]
        for m in mine:
            m.start()
        first = []
        for a in range(n):
            first.append(cp(a, 0, me, sibling, src=ins[a]))
            first += [cp(a, 1 + j, me, (*chip, c), src=ins[a]) for j, chip in enumerate(chips)]
        for f in first:
            f.start()
        passed = []
        for j, chip in enumerate(chips):
            for a in range(n):
                cp(a, 1 + j, idx(*chip, c), (x, y, c)).wait_recv()
                p = cp(a, 4 + j, idx(*chip, c), sibling)
                p.start()
                passed.append(p)
        for a in range(n):
            cp(a, 0, idx(x, y, 1 - c), (x, y, c)).wait_recv()
            for j, chip in enumerate(chips):
                cp(a, 4 + j, idx(*chip, 1 - c), (x, y, c)).wait_recv()
        for f in first + passed:
            f.wait_send()
        for m in mine:
            m.wait()

    return pl.pallas_call(
        body, name=name,
        in_specs=[hbm] * n, out_specs=[hbm] * n,
        out_shape=[jax.ShapeDtypeStruct((N_DEV,) + a.shape, a.dtype) for a in arrs],
        scratch_shapes=[pltpu.SemaphoreType.DMA((n, 7)), pltpu.SemaphoreType.DMA((n, 7)), pltpu.SemaphoreType.DMA((n,))],
        compiler_params=pltpu.CompilerParams(has_side_effects=True),
    )(*arrs)


def _peer(x, y, c, rel):
    return ((1 - x) if rel & 4 else x, (1 - y) if rel & 2 else y, (1 - c) if rel & 1 else c)


def _exchange(arrs, name):
    n = len(arrs)
    hbm = pl.BlockSpec(memory_space=pl.ANY)

    def body(*refs):
        ins, outs = refs[:n], refs[n:2 * n]
        send, recv, loc = refs[2 * n:]
        x, y, c = _position()
        me = 4 * x + 2 * y + c
        mine = [pltpu.make_async_copy(ins[a].at[me], outs[a].at[me], loc.at[a]) for a in range(n)]
        for m in mine:
            m.start()
        copies = []
        for rel in range(1, N_DEV):
            px, py, pc = _peer(x, y, c, rel)
            for a in range(n):
                copies.append(pltpu.make_async_remote_copy(
                    src_ref=ins[a].at[4 * px + 2 * py + pc], dst_ref=outs[a].at[me],
                    send_sem=send.at[a, rel - 1], recv_sem=recv.at[a, rel - 1],
                    device_id=(px, py, pc), device_id_type=MESH))
        for cpy in copies:
            cpy.start()
        for cpy in copies:
            cpy.wait()
        for m in mine:
            m.wait()

    return pl.pallas_call(
        body, name=name,
        in_specs=[hbm] * n, out_specs=[hbm] * n,
        out_shape=[jax.ShapeDtypeStruct(a.shape, a.dtype) for a in arrs],
        scratch_shapes=[pltpu.SemaphoreType.DMA((n, 7)), pltpu.SemaphoreType.DMA((n, 7)), pltpu.SemaphoreType.DMA((n,))],
        compiler_params=pltpu.CompilerParams(has_side_effects=True),
    )(*arrs)


HBM_SPEC = pl.BlockSpec(memory_space=pltpu.HBM)
SEM_SPEC = pl.BlockSpec(memory_space=pltpu.SEMAPHORE)
DATAFLOW = pltpu.SideEffectType.DATAFLOW_SIDE_EFFECTING


CHIP_RELS = (2, 4, 6)


def _push_start(arrs, slots, name, chips=False):
    n = len(arrs)
    n_slots = 4 if chips else N_DEV
    rels = CHIP_RELS if chips else tuple(range(1, N_DEV))
    land_shapes = [a.shape if slots else (n_slots,) + a.shape for a in arrs]

    def body(*refs):
        ins, lands, sends, recvs, token = refs[:n], refs[n:2 * n], refs[2 * n:3 * n], refs[3 * n:4 * n], refs[-1]
        x, y, c = _position()
        for rel in rels:
            px, py, pc = _peer(x, y, c, rel)
            mine, theirs = (2 * x + y, 2 * px + py) if chips else (4 * x + 2 * y + c, 4 * px + 2 * py + pc)
            for a in range(n):
                pltpu.make_async_remote_copy(
                    src_ref=ins[a].at[theirs] if slots else ins[a], dst_ref=lands[a].at[mine],
                    send_sem=sends[a], recv_sem=recvs[a], device_id=(px, py, pc), device_id_type=MESH).start()
        token[...] = jnp.zeros_like(token)

    sem = pltpu.SemaphoreType.DMA(())
    outs = pl.pallas_call(
        body, name=name,
        in_specs=[HBM_SPEC] * (2 * n),
        out_shape=(*[sem] * (2 * n), *[pltpu.HBM(a.shape, a.dtype) for a in arrs],
                   *[pltpu.HBM(s, a.dtype) for s, a in zip(land_shapes, arrs)], jax.ShapeDtypeStruct((8, LANES), F32)),
        out_specs=(*[SEM_SPEC] * (2 * n), *[HBM_SPEC] * (2 * n), pl.BlockSpec(memory_space=pltpu.VMEM)),
        input_output_aliases={i: 2 * n + i for i in range(2 * n)},
        compiler_params=pltpu.CompilerParams(has_side_effects=DATAFLOW),
    )(*[pltpu.with_memory_space_constraint(a, pltpu.HBM) for a in arrs],
      *[pltpu.with_memory_space_constraint(lax.empty(s, a.dtype), pltpu.HBM) for s, a in zip(land_shapes, arrs)])
    return dict(sends=list(outs[:n]), recvs=list(outs[n:2 * n]), ins=list(outs[2 * n:3 * n]),
                lands=list(outs[3 * n:4 * n]), token=outs[-1], copies=len(rels))


def _push_wait(started, after, name):
    n = len(started["ins"])
    copies = started["copies"]

    def body(*refs):
        lands, sends, recvs = refs[n:2 * n], refs[2 * n:3 * n], refs[3 * n:4 * n]
        x, y, c = _position()
        for a in range(n):
            every = lands[a].at[pl.ds(0, copies)]
            drain = pltpu.make_async_remote_copy(src_ref=every, dst_ref=every, send_sem=sends[a], recv_sem=recvs[a],
                                                 device_id=(x, y, c), device_id_type=MESH)
            drain.wait_send()
            drain.wait_recv()

    both = started["ins"] + started["lands"]
    outs = pl.pallas_call(
        body, name=name,
        in_specs=[HBM_SPEC] * (2 * n) + [SEM_SPEC] * (2 * n) + [pl.BlockSpec(memory_space=pl.ANY)],
        out_shape=tuple(pltpu.HBM(a.shape, a.dtype) for a in both), out_specs=tuple([HBM_SPEC] * (2 * n)),
        input_output_aliases={i: i for i in range(2 * n)},
        compiler_params=pltpu.CompilerParams(has_side_effects=DATAFLOW),
    )(*both, *started["sends"], *started["recvs"], after)
    return list(outs[:n]), list(outs[n:])


def _sibling_swap(arr, name):
    chips = N_DEV // 2

    def body(in_ref, out_ref, send, recv):
        x, y, c = _position()
        for s in range(chips):
            pltpu.make_async_remote_copy(src_ref=in_ref.at[2 * s + 1 - c], dst_ref=out_ref.at[s], send_sem=send,
                                         recv_sem=recv, device_id=(x, y, 1 - c), device_id_type=MESH).start()
        pltpu.make_async_remote_copy(src_ref=out_ref, dst_ref=out_ref, send_sem=send, recv_sem=recv,
                                     device_id=(x, y, 1 - c), device_id_type=MESH).wait()

    hbm = pl.BlockSpec(memory_space=pl.ANY)
    return pl.pallas_call(
        body, name=name, in_specs=[hbm], out_specs=hbm,
        out_shape=jax.ShapeDtypeStruct((chips,) + arr.shape[1:], arr.dtype),
        scratch_shapes=[pltpu.SemaphoreType.DMA, pltpu.SemaphoreType.DMA],
        compiler_params=pltpu.CompilerParams(has_side_effects=True),
    )(arr)


def _add_halves(core, arr, other, name):
    ns, r, c = other.shape
    tr = min(r, 256)

    def body(core_ref, a_ref, o_ref, out_ref):
        out_ref[...] = _bf(a_ref[...].astype(F32) + o_ref[...].astype(F32))

    blk = pl.BlockSpec((1, tr, c), lambda s, i, core_ref: (s, i, 0))
    return pl.pallas_call(
        body, name=name,
        grid_spec=pltpu.PrefetchScalarGridSpec(
            num_scalar_prefetch=1, grid=(ns, r // tr),
            in_specs=[pl.BlockSpec((1, tr, c), lambda s, i, core_ref: (2 * s + core_ref[0], i, 0)), blk],
            out_specs=blk),
        out_shape=jax.ShapeDtypeStruct((ns, r, c), BF16),
        compiler_params=_cparams(("parallel", "parallel")),
    )(core, arr, other)


def _all_reduce_small(buf, name):
    rows = buf.shape[0]

    def body(in_ref, out_ref, slots, send, recv):
        x, y, c = _position()
        me = 4 * x + 2 * y + c
        slots[me] = in_ref[...]
        copies = []
        for rel in range(1, N_DEV):
            copies.append(pltpu.make_async_remote_copy(
                src_ref=in_ref, dst_ref=slots.at[me], send_sem=send.at[rel - 1], recv_sem=recv.at[rel - 1],
                device_id=_peer(x, y, c, rel), device_id_type=MESH))
        for cpy in copies:
            cpy.start()
        for cpy in copies:
            cpy.wait()
        tot = slots[0]
        for d in range(1, N_DEV):
            tot = tot + slots[d]
        out_ref[...] = tot

    return pl.pallas_call(
        body, name=name,
        out_shape=jax.ShapeDtypeStruct((rows, LANES), F32),
        in_specs=[pl.BlockSpec(memory_space=pltpu.VMEM)], out_specs=pl.BlockSpec(memory_space=pltpu.VMEM),
        scratch_shapes=[pltpu.VMEM((N_DEV, rows, LANES), F32), pltpu.SemaphoreType.DMA((7,)),
                        pltpu.SemaphoreType.DMA((7,))],
        compiler_params=pltpu.CompilerParams(has_side_effects=True),
    )(buf)


def _adam_math(g, w, m, v):
    m = ADAM_B1 * m + (1.0 - ADAM_B1) * g
    v = ADAM_B2 * v + (1.0 - ADAM_B2) * (g * g)
    m_hat = m / (1.0 - ADAM_B1 ** ADAM_STEP)
    v_hat = v / (1.0 - ADAM_B2 ** ADAM_STEP)
    delta = -ADAM_LR * (m_hat / (jnp.sqrt(v_hat) + ADAM_EPS) + ADAM_WD * w)
    return delta, m, v


def _adam_shard(me, parts, mine, w, m, v, name):
    r, c = w.shape
    tr = min(r, 128)
    n_slots = parts.shape[0]

    def body(me_ref, p_ref, own_ref, w_ref, m_ref, v_ref, g_ref, d_ref, nm_ref, nv_ref):
        own = own_ref[0].astype(F32)
        g = None
        for s in range(n_slots):
            term = jnp.where(me_ref[0] == s, own, p_ref[s].astype(F32))
            g = term if g is None else g + term
        d, nm, nv = _adam_math(g, w_ref[...], m_ref[...], v_ref[...])
        g_ref[...] = g
        d_ref[...] = d
        nm_ref[...] = nm
        nv_ref[...] = nv

    row = pl.BlockSpec((tr, c), lambda i, me_ref: (i, 0))
    o = jax.ShapeDtypeStruct((r, c), F32)
    return pl.pallas_call(
        body, name=name,
        grid_spec=pltpu.PrefetchScalarGridSpec(
            num_scalar_prefetch=1, grid=(r // tr,),
            in_specs=[pl.BlockSpec((n_slots, tr, c), lambda i, me_ref: (0, i, 0)),
                      pl.BlockSpec((1, tr, c), lambda i, me_ref: (me_ref[0], i, 0)), row, row, row],
            out_specs=[row] * 4),
        out_shape=[o] * 4,
        compiler_params=_cparams(("parallel",)),
    )(me, parts, mine, w, m, v)


def _adam_small(g, w, m, v):
    def body(g_ref, w_ref, m_ref, v_ref, d_ref, nm_ref, nv_ref):
        d, nm, nv = _adam_math(g_ref[...], w_ref[...], m_ref[...], v_ref[...])
        d_ref[...] = d
        nm_ref[...] = nm
        nv_ref[...] = nv

    o = jax.ShapeDtypeStruct(g.shape, F32)
    return pl.pallas_call(body, name="adam_small", out_shape=[o] * 3, compiler_params=_cparams())(g, w, m, v)


def _split_w_in(w_full):
    o = IN_OFF
    w_gdn = w_full[:, o["gq"]:o["ga"]]
    w_fox = w_full[:, o["fq"]:o["ff"]]
    w_gate = w_full[:, o["gate_a"]:o["end"]]
    w_small = jnp.concatenate([w_full[:, o["ga"]:o["fq"]], w_full[:, o["ff"]:o["gate_a"]],
                               jnp.zeros((w_full.shape[0], LANES - 24), w_full.dtype)], axis=1)
    return w_gdn, w_fox, w_gate, w_small


def _join_w_in(parts):
    small = parts[0]
    return jnp.concatenate(parts[1:5] + [small[:, 0:16]] + parts[5:8] + [small[:, 16:24]] + parts[8:10], axis=1)


def _rows128(a, rows):
    flat = a.reshape(-1)
    flat = jnp.concatenate([flat, jnp.zeros((rows * LANES - flat.shape[0],), flat.dtype)])
    return flat.reshape(rows, LANES)


def kernel(x, norm_mix_g, w_in, gdn_conv_w, gdn_a_log, gdn_dt_bias, gdn_norm_g, fox_q_norm_g, fox_k_norm_g, fox_f_bias, w_proj_gdn, w_proj_fox, w_out, norm_mlp_g, w_up, w_down, loss_target, m_norm_mix_g, m_w_in, m_gdn_conv_w, m_gdn_a_log, m_gdn_dt_bias, m_gdn_norm_g, m_fox_q_norm_g, m_fox_k_norm_g, m_fox_f_bias, m_w_proj_gdn, m_w_proj_fox, m_w_out, m_norm_mlp_g, m_w_up, m_w_down, v_norm_mix_g, v_w_in, v_gdn_conv_w, v_gdn_a_log, v_gdn_dt_bias, v_gdn_norm_g, v_fox_q_norm_g, v_fox_k_norm_g, v_fox_f_bias, v_w_proj_gdn, v_w_proj_fox, v_w_out, v_norm_mlp_g, v_w_up, v_w_down):
    b_loc, t, d = x.shape
    n = b_loc * t
    me = 4 * lax.axis_index("x") + 2 * lax.axis_index("y") + lax.axis_index("c")

    late_names = ["w_proj_gdn", "w_proj_fox", "w_out", "w_up", "w_down"]
    big16 = _to_bf16([w_in[0], w_proj_gdn[0], w_proj_fox[0], w_out[0], w_up[0], w_down[0]], "weights_to_bf16")
    g_in, g_conv = _all_gather([big16[0], gdn_conv_w[0]], "gather_w_in")
    late = _push_start(list(big16[1:]), False, "gather_late_start")
    w_full = g_in.transpose(1, 0, 2).reshape(d, N_DEV * w_in.shape[2])
    w_gdn, w_fox, w_gate, w_small = _split_w_in(w_full)
    weights = {
        "w_gdn": w_gdn, "w_fox": w_fox, "w_gate": w_gate, "w_small": w_small,
        "conv_w": g_conv.transpose(1, 0, 2).reshape(CONV_K, 3 * d),
        "norm_mix_g": norm_mix_g + late["token"][0:1, 0:1], "norm_mlp_g": norm_mlp_g, "a_log": gdn_a_log,
        "dt_bias": gdn_dt_bias, "gdn_norm_g": gdn_norm_g, "fox_q_norm_g": fox_q_norm_g, "fox_k_norm_g": fox_k_norm_g,
        "f_bias": fox_f_bias,
    }
    c_in, c_up = w_in.shape[2], w_up.shape[2]
    me1 = jnp.reshape(me, (1,)).astype(jnp.int32)
    chip1 = jnp.reshape(2 * lax.axis_index("x") + lax.axis_index("y"), (1,)).astype(jnp.int32)
    core1 = jnp.reshape(lax.axis_index("c"), (1,)).astype(jnp.int32)

    class _Exchange:
        def __init__(self):
            self.started = []

        def late_weights(self, after):
            shards, lands = _push_wait(late, after, "gather_late_wait")
            full = [lax.dynamic_update_index_in_dim(land, shard, me, 0) for land, shard in zip(lands, shards)]
            g_pa, g_pb, g_out, g_up, g_down = full
            return {"w_proj_gdn": g_pa.reshape(d, d), "w_proj_fox": g_pb.reshape(d, d), "w_out": g_out.reshape(d, d),
                    "w_up": g_up.transpose(1, 0, 2).reshape(d, D_FF), "w_down": g_down.reshape(D_FF, d)}

        def grads_ready(self, grads, tie):
            names = list(grads)
            if names == ["w_in_parts"]:
                halves = _join_w_in(grads["w_in_parts"]).reshape(d, N_DEV, c_in).transpose(1, 0, 2)
                other = _sibling_swap(halves, "grads_w_in_sibling")
                pair = _add_halves(core1, halves, other, "grads_w_in_pair")
                st = _push_start([pair], True, "grads_start_w_in_parts", chips=True)
            else:
                layout = {"w_up": lambda a: a.reshape(d, N_DEV, c_up).transpose(1, 0, 2),
                          "w_down": lambda a: a.reshape(N_DEV, D_FF // N_DEV, d)}
                arrs = [layout.get(k, lambda a: a.reshape(N_DEV, d // N_DEV, d))(grads[k]) for k in names]
                st = _push_start(arrs, True, "grads_start_" + names[0])
            self.started.append((names, st))
            return tie + st["token"][0:1, 0:1]

    comm = _Exchange()
    loss_blk, grad_x, g = _local_step(x.reshape(n, d), loss_target.reshape(n, d), weights, b_loc, t, comm)

    shards = {"w_in_parts": (w_in, m_w_in, v_w_in), "w_proj_gdn": (w_proj_gdn, m_w_proj_gdn, v_w_proj_gdn),
              "w_proj_fox": (w_proj_fox, m_w_proj_fox, v_w_proj_fox), "w_out": (w_out, m_w_out, v_w_out),
              "w_up": (w_up, m_w_up, v_w_up), "w_down": (w_down, m_w_down, v_w_down)}
    adam = {}

    def finish(names, st, after):
        mine, parts = _push_wait(st, after, "grads_wait_" + names[0])
        slot = chip1 if st["copies"] == len(CHIP_RELS) else me1
        for k, own, part in zip(names, mine, parts):
            wi, mi, vi = shards[k]
            adam[k] = [r[None] for r in _adam_shard(slot, part, own, wi[0], mi[0], vi[0], "adam_" + k)]

    for names, st in comm.started[:-1]:
        finish(names, st, grad_x)

    conv_rows = CONV_K * 3 * d // LANES
    conv_g = g["conv"].transpose(1, 0, 2).reshape(conv_rows, LANES)
    buf = jnp.concatenate([conv_g, g["norm_mix_g"].reshape(8, LANES), g["norm_mlp_g"].reshape(8, LANES),
                           g["gdn_small"], g["fox_small"], loss_blk], axis=0)
    anchor = sum(adam[k][1][0, 0:1, 0:LANES] for names, _ in comm.started[:-1] for k in names) * 0.0
    tot = _all_reduce_small(buf + anchor, "all_reduce_small")
    finish(*comm.started[-1], tot)
    big_out = [adam[k] for k in ["w_in_parts"] + late_names]
    o = conv_rows
    conv_full = tot[0:o].reshape(CONV_K, 3 * d)
    c_conv = gdn_conv_w.shape[2]
    g_conv_shard = lax.dynamic_slice(conv_full, (0, me * c_conv), (CONV_K, c_conv))
    g_mix = tot[o:o + 8].reshape(1, d)
    g_mlp = tot[o + 8:o + 16].reshape(1, d)
    gs, fs = tot[o + 16:o + 24], tot[o + 24:o + 32]
    loss = tot[o + 32, 0]
    small_g = [g_mix, g_conv_shard[None], gs[0:1, 0:HEADS], gs[1:2, 0:HEADS], gs[2:3], fs[0:1], fs[1:2], fs[2:3, 0:HEADS],
               g_mlp]
    small_w = [norm_mix_g, gdn_conv_w, gdn_a_log, gdn_dt_bias, gdn_norm_g, fox_q_norm_g, fox_k_norm_g, fox_f_bias,
               norm_mlp_g]
    small_m = [m_norm_mix_g, m_gdn_conv_w, m_gdn_a_log, m_gdn_dt_bias, m_gdn_norm_g, m_fox_q_norm_g, m_fox_k_norm_g,
               m_fox_f_bias, m_norm_mlp_g]
    small_v = [v_norm_mix_g, v_gdn_conv_w, v_gdn_a_log, v_gdn_dt_bias, v_gdn_norm_g, v_fox_q_norm_g, v_fox_k_norm_g,
               v_fox_f_bias, v_norm_mlp_g]
    row_counts = [-(-a.size // (8 * LANES)) * 8 for a in small_w]

    def pack(arrs):
        return jnp.concatenate([_rows128(a, rc) for a, rc in zip(arrs, row_counts)], axis=0)

    sd, sm, sv = _adam_small(pack(small_g), pack(small_w), pack(small_m), pack(small_v))

    def unpack(p):
        outs, r0 = [], 0
        for a, rc in zip(small_w, row_counts):
            outs.append(p[r0:r0 + rc].reshape(-1)[:a.size].reshape(a.shape))
            r0 += rc
        return outs

    small_out = [small_g_i.reshape(w_i.shape) for small_g_i, w_i in zip(small_g, small_w)], unpack(sd), unpack(sm), unpack(sv)

    def ordered(kind):
        s = small_out[kind]
        bo = [b[kind] for b in big_out]
        return [s[0], bo[0], s[1], s[2], s[3], s[4], s[5], s[6], s[7], bo[1], bo[2], bo[3], s[8], bo[4], bo[5]]

    return (loss, grad_x.reshape(b_loc, t, d), *ordered(0), *ordered(1), *ordered(2), *ordered(3))
```

```python
import functools

import jax
import jax.numpy as jnp
from jax import lax
from jax.experimental import pallas as pl
from jax.experimental.pallas import tpu as pltpu

F32 = jnp.float32
BF16 = jnp.bfloat16
MESH = pl.DeviceIdType.MESH

N_DEV = 8
D_MODEL = 1024
HEADS = 8
DH = 128
CONV_K = 4
CHUNK = 128
GDN_GROUP = 8
FOX_BLOCK = 128
FOX_TILE = 512
FOX_SHORT = 512
D_FF = 4 * D_MODEL
EPS = 1e-6
LANES = 128
NEG = -1e30
IN_OFF = {"gq": 0, "gk": 1024, "gv": 2048, "gz": 3072, "ga": 4096, "gb": 4104, "fq": 4112, "fk": 5136,
          "fv": 6160, "ff": 7184, "gate_a": 7192, "gate_b": 8216, "end": 9240}
LANE_GA, LANE_GB, LANE_FF = 0, 8, 16

ADAM_LR = 0.001
ADAM_B1 = 0.9
ADAM_B2 = 0.999
ADAM_EPS = 1e-08
ADAM_WD = 0.01
ADAM_STEP = 10

VMEM_LIMIT = 56 * 1024 * 1024


def _cparams(sem=None):
    return pltpu.CompilerParams(dimension_semantics=sem, vmem_limit_bytes=VMEM_LIMIT)


def _sigmoid(x):
    return 1.0 / (1.0 + jnp.exp(-x))


def _softplus(x):
    return jnp.maximum(x, 0.0) + jnp.log(1.0 + jnp.exp(-jnp.abs(x)))


def _dot(a, b, prec=None):
    return lax.dot_general(a, b, (((1,), (0,)), ((), ())), precision=prec, preferred_element_type=F32)


def _dot_nt(a, b, prec=None):
    return lax.dot_general(a, b, (((1,), (1,)), ((), ())), precision=prec, preferred_element_type=F32)


def _dot_tn(a, b, prec=None):
    return lax.dot_general(a, b, (((0,), (0,)), ((), ())), precision=prec, preferred_element_type=F32)


def _bf(x):
    return x.astype(BF16)


MM_TILE = 1024


def _mm(a, b, *, name, ta=False, tb=False, out_dtype=F32, epi=None, extras=(), out2=None,
        b_koff=0, tm=MM_TILE, tn=MM_TILE, tk=MM_TILE):
    m = a.shape[1] if ta else a.shape[0]
    kdim = a.shape[0] if ta else a.shape[1]
    n = b.shape[0] if tb else b.shape[1]
    tm, tn, tk = min(tm, m), min(tn, n), min(tk, kdim)
    nk = kdim // tk
    grid = (m // tm, n // tn, nk)
    koff = b_koff // tk
    a_spec = pl.BlockSpec((tk, tm), lambda i, j, k: (k, i)) if ta else pl.BlockSpec((tm, tk), lambda i, j, k: (i, k))
    if tb:
        b_spec = pl.BlockSpec((tn, tk), lambda i, j, k: (j, k + koff))
    else:
        b_spec = pl.BlockSpec((tk, tn), lambda i, j, k: (k + koff, j))
    o_spec = pl.BlockSpec((tm, tn), lambda i, j, k: (i, j))
    n_e = len(extras)
    n_o = 1 if out2 is None else 2
    dims = (((0 if ta else 1,), (1 if tb else 0,)), ((), ()))

    def body(a_ref, b_ref, *rest):
        e_refs, o_refs = rest[:n_e], rest[n_e:n_e + n_o]
        prod = lax.dot_general(_bf(a_ref[...]), _bf(b_ref[...]), dims, preferred_element_type=F32)

        def finish(r):
            if out2 is not None:
                o_refs[1][...] = out2[0](r).astype(out2[1])
            if epi is not None:
                r = epi(r, *[e[...] for e in e_refs])
            o_refs[0][...] = r.astype(out_dtype)

        if nk == 1:
            finish(prod)
        else:
            acc = rest[n_e + n_o]
            k = pl.program_id(2)

            @pl.when(k == 0)
            def _():
                acc[...] = prod

            @pl.when(k > 0)
            def _():
                acc[...] += prod

            @pl.when(k == nk - 1)
            def _():
                finish(acc[...])

    shapes = [jax.ShapeDtypeStruct((m, n), out_dtype)]
    if out2 is not None:
        shapes.append(jax.ShapeDtypeStruct((m, n), out2[1]))
    res = pl.pallas_call(
        body, name=name, grid=grid,
        in_specs=[a_spec, b_spec] + [o_spec] * n_e,
        out_specs=[o_spec] * n_o, out_shape=shapes,
        scratch_shapes=[] if nk == 1 else [pltpu.VMEM((tm, tn), F32)],
        compiler_params=_cparams(("parallel", "parallel", "arbitrary")),
    )(a, b, *extras)
    return res[0] if out2 is None else res


def _du_all(dps, w_small, segs, w, tm=512):
    n, d = dps.shape[0], w_small.shape[0]
    names = []
    for _, wname, _ in segs:
        if wname not in names:
            names.append(wname)
    first = {nm: min(i for i, s in enumerate(segs) if s[1] == nm) for nm in names}
    count = {nm: sum(1 for s in segs if s[1] == nm) for nm in names}
    n_seg = len(segs)

    def w_spec(nm):
        return pl.BlockSpec((d, d), lambda i, k: (0, jnp.clip(k - first[nm], 0, count[nm] - 1)))

    def body(dps_ref, ws_ref, *rest):
        seg_refs, w_refs, o_ref, acc = rest[:n_seg], rest[n_seg:n_seg + len(names)], rest[-2], rest[-1]
        k = pl.program_id(1)

        @pl.when(k == 0)
        def _():
            acc[...] = _dot_nt(_bf(dps_ref[...]), ws_ref[...])

        for idx, (_, wname, _) in enumerate(segs):
            @pl.when(k == idx)
            def _(idx=idx, wname=wname):
                acc[...] += _dot_nt(seg_refs[idx][...], w_refs[names.index(wname)][...])

        @pl.when(k == n_seg - 1)
        def _():
            o_ref[...] = acc[...]

    n_i = n // tm

    def row(cols, used_at):
        return pl.BlockSpec((tm, cols), lambda i, k: (jnp.minimum(i + (k > used_at).astype(jnp.int32), n_i - 1), 0))

    return pl.pallas_call(
        body, name="du_all", grid=(n_i, n_seg),
        in_specs=[row(dps.shape[1], 0), pl.BlockSpec(w_small.shape, lambda i, k: (0, 0))]
                 + [row(d, j) for j in range(n_seg)] + [w_spec(nm) for nm in names],
        out_specs=pl.BlockSpec((tm, d), lambda i, k: (i, 0)), out_shape=jax.ShapeDtypeStruct((n, d), F32),
        scratch_shapes=[pltpu.VMEM((tm, d), F32)],
        compiler_params=_cparams(("parallel", "arbitrary")),
    )(dps, w_small, *[s[0] for s in segs], *[w[nm] for nm in names])


def _relu2(x):
    r = jnp.maximum(x, 0.0)
    return r * r


ROWS = 512


def _rms_fwd(x, g, name):
    n, d = x.shape

    def body(x_ref, g_ref, u_ref):
        xv = x_ref[...]
        r = lax.rsqrt(jnp.mean(xv * xv, axis=1, keepdims=True) + EPS)
        u_ref[...] = _bf(xv * r * g_ref[...])

    return pl.pallas_call(
        body, name=name, grid=(n // ROWS,),
        in_specs=[pl.BlockSpec((ROWS, d), lambda i: (i, 0)), pl.BlockSpec((1, d), lambda i: (0, 0))],
        out_specs=pl.BlockSpec((ROWS, d), lambda i: (i, 0)),
        out_shape=jax.ShapeDtypeStruct((n, d), BF16),
        compiler_params=_cparams(("parallel",)),
    )(x, g)


def _rms_bwd(dy, x, g, dres, name):
    n, d = x.shape

    def body(dy_ref, x_ref, g_ref, dres_ref, dx_ref, dx16_ref, dg_ref):
        i = pl.program_id(0)
        xv, dyv = x_ref[...], dy_ref[...]
        r = lax.rsqrt(jnp.mean(xv * xv, axis=1, keepdims=True) + EPS)
        gy = dyv * g_ref[...]
        s = jnp.sum(gy * xv, axis=1, keepdims=True)
        dx = dres_ref[...] + r * gy - xv * (r * r * r * (1.0 / d)) * s
        dx_ref[...] = dx
        dx16_ref[...] = _bf(dx)

        @pl.when(i == 0)
        def _():
            dg_ref[...] = jnp.zeros_like(dg_ref)

        dg_ref[...] += jnp.sum(dyv * xv * r, axis=0, keepdims=True)

    row = pl.BlockSpec((ROWS, d), lambda i: (i, 0))
    vec = pl.BlockSpec((1, d), lambda i: (0, 0))
    return pl.pallas_call(
        body, name=name, grid=(n // ROWS,),
        in_specs=[row, row, vec, row], out_specs=[row, row, vec],
        out_shape=[jax.ShapeDtypeStruct((n, d), F32), jax.ShapeDtypeStruct((n, d), BF16),
                   jax.ShapeDtypeStruct((1, d), F32)],
        compiler_params=_cparams(("arbitrary",)),
    )(dy, x, g, dres)


def _merge_fwd(ya, yb, gate):
    n, d = ya.shape

    def body(ya_ref, yb_ref, ga_ref, gb_ref, o_ref):
        o_ref[...] = _bf(_sigmoid(ga_ref[...]) * ya_ref[...] + _sigmoid(gb_ref[...]) * yb_ref[...])

    row = pl.BlockSpec((ROWS, d), lambda i: (i, 0))
    return pl.pallas_call(
        body, name="merge_fwd", grid=(n // ROWS,),
        in_specs=[row, row, row, pl.BlockSpec((ROWS, d), lambda i: (i, 1))], out_specs=row,
        out_shape=jax.ShapeDtypeStruct((n, d), BF16),
        compiler_params=_cparams(("parallel",)),
    )(ya, yb, gate, gate)


def _merge_bwd(dm, ya, yb, gate):
    n, d = ya.shape

    def body(dm_ref, ya_ref, yb_ref, ga_ref, gb_ref, dya_ref, dyb_ref, dga_ref, dgb_ref):
        dmv = dm_ref[...]
        sa, sb = _sigmoid(ga_ref[...]), _sigmoid(gb_ref[...])
        dya_ref[...] = _bf(dmv * sa)
        dyb_ref[...] = _bf(dmv * sb)
        dga_ref[...] = _bf(dmv * ya_ref[...] * sa * (1.0 - sa))
        dgb_ref[...] = _bf(dmv * yb_ref[...] * sb * (1.0 - sb))

    row = pl.BlockSpec((ROWS, d), lambda i: (i, 0))
    o = jax.ShapeDtypeStruct((n, d), BF16)
    return pl.pallas_call(
        body, name="merge_bwd", grid=(n // ROWS,),
        in_specs=[row, row, row, row, pl.BlockSpec((ROWS, d), lambda i: (i, 1))], out_specs=[row] * 4,
        out_shape=[o] * 4,
        compiler_params=_cparams(("parallel",)),
    )(dm, ya, yb, gate, gate)


def _loss_bwd(out, target):
    n, d = out.shape

    def body(o_ref, t_ref, d_ref, d16_ref, l_ref):
        i = pl.program_id(0)
        err = o_ref[...] - t_ref[...]
        d_ref[...] = err * (1.0 / d)
        d16_ref[...] = _bf(err * (1.0 / d))

        @pl.when(i == 0)
        def _():
            l_ref[...] = jnp.zeros_like(l_ref)

        l_ref[...] += 0.5 * jnp.sum(jnp.mean(err * err, axis=1, keepdims=True), axis=0, keepdims=True)

    row = pl.BlockSpec((ROWS, d), lambda i: (i, 0))
    return pl.pallas_call(
        body, name="loss_bwd", grid=(n // ROWS,),
        in_specs=[row, row], out_specs=[row, row, pl.BlockSpec((8, LANES), lambda i: (0, 0))],
        out_shape=[jax.ShapeDtypeStruct((n, d), F32), jax.ShapeDtypeStruct((n, d), BF16),
                   jax.ShapeDtypeStruct((8, LANES), F32)],
        compiler_params=_cparams(("arbitrary",)),
    )(out, target)


PAD = 8


def _pad_zero(pad_ref):
    t = pad_ref.shape[0] - 2 * PAD
    pad_ref[0:PAD, :] = jnp.zeros((PAD, LANES), F32)
    pad_ref[PAD + t:2 * PAD + t, :] = jnp.zeros((PAD, LANES), F32)


def _shifted(pad_ref, s):
    t = pad_ref.shape[0] - 2 * PAD
    return pad_ref[PAD - s:PAD - s + t, :]


def _conv(x, w_ref, pad_ref):
    t = x.shape[0]
    pad_ref[PAD:PAD + t, :] = x
    y = _shifted(pad_ref, 3) * w_ref[0:1, :]
    y = y + _shifted(pad_ref, 2) * w_ref[1:2, :]
    y = y + _shifted(pad_ref, 1) * w_ref[2:3, :]
    return y + x * w_ref[3:4, :]


def _chunk_consts():
    r = lax.broadcasted_iota(jnp.int32, (CHUNK, CHUNK), 0)
    c = lax.broadcasted_iota(jnp.int32, (CHUNK, CHUNK), 1)
    incl, strict = r >= c, r > c
    return dict(incl=incl, strict=strict, trilf=incl.astype(F32), triuf=(r <= c).astype(F32),
                eye=(r == c).astype(F32))


class _V:
    def __init__(self, xs):
        self.xs = list(xs)

    def __add__(self, o):
        return _ap(lambda x, y: x + y, self, o)

    def __radd__(self, o):
        return _ap(lambda x, y: y + x, self, o)

    def __sub__(self, o):
        return _ap(lambda x, y: x - y, self, o)

    def __rsub__(self, o):
        return _ap(lambda x, y: y - x, self, o)

    def __mul__(self, o):
        return _ap(lambda x, y: x * y, self, o)

    def __rmul__(self, o):
        return _ap(lambda x, y: y * x, self, o)

    def __neg__(self):
        return _ap(lambda x: -x, self)

    def __getitem__(self, idx):
        return _ap(lambda x: x[idx], self)


def _ap(fn, *args):
    n = [len(a.xs) for a in args if isinstance(a, _V)]
    if not n:
        return fn(*args)
    return _V([fn(*[a.xs[i] if isinstance(a, _V) else a for a in args]) for i in range(n[0])])


def _vbf(x):
    return _ap(_bf, x)


def _vdot(a, b):
    return _ap(_dot, a, b)


def _vdot_nt(a, b):
    return _ap(_dot_nt, a, b)


def _vdot_tn(a, b):
    return _ap(_dot_tn, a, b)


def _vexp(x):
    return _ap(jnp.exp, x)


def _vsum(x, axis):
    return _ap(lambda v: jnp.sum(v, axis=axis, keepdims=True), x)


def _vcat(a, b, axis):
    return _ap(lambda x, y: jnp.concatenate([x, y], axis=axis), a, b)


def _vmask(mask, x):
    return _ap(lambda v: jnp.where(mask, v, 0.0), x)


def _split2(x):
    h = _vbf(x)
    return h, _vbf(x - _ap(lambda v: v.astype(F32), h))


def _dot3(a, b, kind=_vdot):
    ah, al = _split2(a)
    bh, bl = _split2(b)
    return kind(ah, bh) + (kind(ah, bl) + kind(al, bh))


def _split(x, terms):
    out = []
    for _ in range(terms):
        h = _vbf(x)
        out.append(h)
        x = x - _ap(lambda v: v.astype(F32), h)
    return out


def _dot_exact_l(m01, x, kind=_vdot, terms=2):
    mb = _bf(m01)
    parts = [kind(mb, xp) for xp in _split(x, terms)]
    return functools.reduce(lambda a, b: a + b, reversed(parts))


def _dot_exact_r(x, m01, kind=_vdot, terms=2):
    mb = _bf(m01)
    parts = [kind(xp, mb) for xp in _split(x, terms)]
    return functools.reduce(lambda a, b: a + b, reversed(parts))


def _inv_series(a, eye):
    m = eye.shape[0]
    levels = m.bit_length() - 1
    p = -a
    r = p + eye
    p = _dot3(p, p)
    for j in range(1, levels):
        if j < levels - 1:
            y = _dot3(p, _vcat(p, r, 1))
            p, r = y[:, 0:m], r + y[:, m:2 * m]
        else:
            r = r + _dot3(p, r)
    return r


def _inv_unit_lower(a, eye):
    return _inv_series(a, eye)


def _gdn_chunk_pre(q, k, v, g128, g64, b128, b64, cs):
    incl = cs["incl"]
    big_g = _dot_exact_l(cs["trilf"], g128)
    gc = big_g[:, 0:CHUNK]
    gr = _dot_exact_r(g64, cs["triuf"], _vdot_tn)
    decay = _ap(lambda d: jnp.where(incl, jnp.exp(jnp.where(incl, d, 0.0)), 0.0), gc - gr)
    kb, qb = _vbf(k), _vbf(q)
    qkk = _vdot_nt(_vcat(qb, kb, 0), kb)
    qk, kk = qkk[0:CHUNK], qkk[CHUNK:2 * CHUNK]
    tm = _inv_unit_lower(_vmask(cs["strict"], b64 * kk * decay), cs["eye"])
    e_g = _vexp(big_g)
    wu = _dot3(tm, _vcat(v * b128, k * (b128 * e_g), 1))
    w, u = wu[:, 0:DH], wu[:, DH:2 * DH]
    g_last = _vsum(g128, 0)
    return dict(big_g=big_g, decay=decay, kk=kk, qk=qk, tm=tm, w=w, u=u, p=qk * decay, q_dec=q * e_g,
                k_dec=k * _vexp(g_last - big_g), dec=_vexp(g_last))


def _gdn_chunk_post(q, k, v, g128, b128, b64, s, ds_next, do, dv_new, big_g, decay, kk, qk, tm, u, v_new, cs):
    e_g = _vexp(big_g)
    vb = v * b128
    kbeta = k * (b128 * e_g)
    q_dec = q * e_g
    g_last = _vsum(g128, 0)
    ekg = _vexp(g_last - big_g)
    k_dec = k * ekg
    dec = _vexp(g_last)
    kb, qb, sb = _vbf(k), _vbf(q), _vbf(s)
    dob, dsb, vnb, dvnb = _vbf(do), _vbf(ds_next), _vbf(v_new), _vbf(dv_new)
    dp = _vmask(cs["incl"], _vdot_nt(dob, vnb))
    dq_dec = _vdot_nt(dob, sb)
    du = -_vdot_nt(dvnb, sb)
    ddec = _vsum(_vsum(s * ds_next, 1), 0)
    dk_dec = _vdot_nt(vnb, dsb)
    dwu = _vcat(dv_new, du, 1)
    dt = _dot3(dwu, _vcat(vb, kbeta, 1), _vdot_nt)
    dvk = _dot3(tm, dwu, _vdot_tn)
    dvb, dkbeta = dvk[:, 0:DH], dvk[:, DH:2 * DH]
    da = _vmask(cs["strict"], -_dot3(tm, _dot3(dt, tm, _vdot_nt), _vdot_tn))
    dkk = _vbf(da * b64 * decay)
    dqk = _vbf(dp * decay)
    ddd = (da * b64 * kk + dp * qk) * decay
    dq = _vdot(dqk, kb) + dq_dec * e_g
    dk = _vdot_tn(dqk, qb) + _vdot(dkk, kb) + _vdot_tn(dkk, kb) + dk_dec * ekg + dkbeta * (b128 * e_g)
    dv = dvb * b128
    dbeta = _vsum(da * kk * decay, 1) + _vsum(dvb * v, 1) + _vsum(dkbeta * k * e_g, 1)
    s_k = _vsum(dk_dec * k_dec, 1)
    dg_col = _vsum(ddd, 1) + _vsum(dq_dec * q_dec, 1) - s_k + _vsum(dkbeta * kbeta, 1)
    colsum = _dot_exact_r(ddd, jnp.ones((CHUNK, LANES), F32), _vdot_tn)
    dg_last = _vsum(s_k, 0) + ddec * dec
    dg = _dot_exact_l(cs["triuf"], dg_col - colsum) + dg_last
    return dq, dk, dv, dg, dbeta


def _stack_rows(vecs, nrows):
    row = lax.broadcasted_iota(jnp.int32, (nrows, LANES), 0)
    out = jnp.zeros((nrows, LANES), F32)
    for i, v in enumerate(vecs):
        out = out + jnp.where(row == i, jnp.broadcast_to(v, (nrows, LANES)), 0.0)
    return out


def _head_lane(x, lane_idx):
    lane = lax.broadcasted_iota(jnp.int32, x.shape, 1)
    return jnp.sum(jnp.where(lane == lane_idx, x, 0.0), axis=1, keepdims=True)


def _gdn_gates(ps, h, alog_ref, dtb_ref):
    ga = _head_lane(ps, LANE_GA + h)
    gb = _head_lane(ps, LANE_GB + h)
    a = jnp.exp(jnp.full((1, 1), alog_ref[0, h], F32))
    sp_in = ga + dtb_ref[0, h]
    g = -a * _softplus(sp_in)
    return g, _sigmoid(gb), a, sp_in


def _gdn_specs(b_loc, t):
    def col(off):
        return pl.BlockSpec((t, DH), lambda b, h: (b, off + h))

    ps_spec = pl.BlockSpec((t, LANES), lambda b, h: (b, 0))

    def wcol(off):
        return pl.BlockSpec((CONV_K, DH), lambda b, h: (0, off + h))

    smem = pl.BlockSpec(memory_space=pltpu.SMEM)
    vec = pl.BlockSpec((1, DH), lambda b, h: (0, 0))
    return col, ps_spec, wcol, smem, vec


def _gdn_fwd(pg, ps, convw, a_log, dt_bias, gnorm, b_loc, t):
    n = b_loc * t
    assert t % (CHUNK * GDN_GROUP) == 0 and CHUNK == LANES, (t, CHUNK, GDN_GROUP)
    nc = t // CHUNK
    col, ps_spec, wcol, smem, vec = _gdn_specs(b_loc, t)

    def body(q_ref, k_ref, v_ref, z_ref, ps_ref, wq_ref, wk_ref, wv_ref, alog_ref, dtb_ref, gn_ref,
             oa_ref, oraw_ref, s_ref, qn, kn, vv, g128, g64, b128, b64, uq_s, p_s, kd_s, dec_s, pad_s):
        h = pl.program_id(1)
        g, beta, _, _ = _gdn_gates(ps_ref[...], h, alog_ref, dtb_ref)
        g128[...] = jnp.broadcast_to(g, (t, LANES))
        g64[...] = jnp.broadcast_to(g, (t, CHUNK))
        b128[...] = jnp.broadcast_to(beta, (t, LANES))
        b64[...] = jnp.broadcast_to(beta, (t, CHUNK))
        _pad_zero(pad_s)
        pq = _conv(q_ref[...], wq_ref, pad_s)
        yq = pq * _sigmoid(pq)
        qn[...] = yq * (lax.rsqrt(jnp.sum(yq * yq, axis=1, keepdims=True) + EPS) * (DH ** -0.5))
        pk = _conv(k_ref[...], wk_ref, pad_s)
        yk = pk * _sigmoid(pk)
        kn[...] = yk * lax.rsqrt(jnp.sum(yk * yk, axis=1, keepdims=True) + EPS)
        pv = _conv(v_ref[...], wv_ref, pad_s)
        vv[...] = pv * _sigmoid(pv)
        cs = _chunk_consts()

        def pre_group(gi, _):
            idx = [gi * GDN_GROUP + c for c in range(GDN_GROUP)]
            rows = [pl.ds(pl.multiple_of(i * CHUNK, CHUNK), CHUNK) for i in idx]
            ins = [_V([ref[r, :] for r in rows]) for ref in (qn, kn, vv, g128, g64, b128, b64)]
            f = _gdn_chunk_pre(*ins, cs)
            for c, (i, r) in enumerate(zip(idx, rows)):
                vv[r, :] = f["w"].xs[c]
                uq_s[i, 0:CHUNK, :] = _bf(f["u"].xs[c])
                uq_s[i, CHUNK:2 * CHUNK, :] = _bf(f["q_dec"].xs[c])
                p_s[r, :] = _bf(f["p"].xs[c])
                kd_s[r, :] = _bf(f["k_dec"].xs[c])
                dec_s[pl.ds(pl.multiple_of(i * 8, 8), 8), :] = jnp.broadcast_to(f["dec"].xs[c], (8, LANES))
            return 0

        lax.fori_loop(0, nc // GDN_GROUP, pre_group, 0)

        def chunk(i, s):
            r = pl.ds(pl.multiple_of(i * CHUNK, CHUNK), CHUNK)
            us = _dot(uq_s[i], _bf(s))
            vnb = _bf(vv[r, :] - us[0:CHUNK])
            oraw_ref[r, :] = us[CHUNK:2 * CHUNK] + _dot(p_s[r, :], vnb)
            s_ref[0, 0, i] = s
            return s * dec_s[pl.ds(pl.multiple_of(i * 8, 8), 1), :] + _dot_tn(kd_s[r, :], vnb)

        lax.fori_loop(0, nc, chunk, jnp.zeros((DH, DH), F32))
        o = oraw_ref[...]
        rr = lax.rsqrt(jnp.mean(o * o, axis=1, keepdims=True) + EPS)
        z = z_ref[...]
        oa_ref[...] = _bf((o * rr * gn_ref[...]) * (z * _sigmoid(z)))

    return pl.pallas_call(
        body, name="gdn_fwd", grid=(b_loc, HEADS),
        in_specs=[col(0), col(HEADS), col(2 * HEADS), col(3 * HEADS), ps_spec, wcol(0), wcol(HEADS), wcol(2 * HEADS),
                  smem, smem, vec],
        out_specs=[pl.BlockSpec((t, DH), lambda b, h: (b, h)), pl.BlockSpec((t, DH), lambda b, h: (b, h)),
                   pl.BlockSpec((1, 1, nc, DH, DH), lambda b, h: (b, h, 0, 0, 0))],
        out_shape=[jax.ShapeDtypeStruct((n, HEADS * DH), BF16), jax.ShapeDtypeStruct((n, HEADS * DH), F32),
                   jax.ShapeDtypeStruct((b_loc, HEADS, nc, DH, DH), F32)],
        scratch_shapes=([pltpu.VMEM((t, DH), F32)] * 3 + [pltpu.VMEM((t, LANES), F32), pltpu.VMEM((t, CHUNK), F32)] * 2
                        + [pltpu.VMEM((nc, 2 * CHUNK, DH), BF16), pltpu.VMEM((t, CHUNK), BF16), pltpu.VMEM((t, DH), BF16),
                           pltpu.VMEM((8 * nc, LANES), F32), pltpu.VMEM((t + 2 * PAD, LANES), F32)]),
        compiler_params=_cparams(("arbitrary", "arbitrary")),
    )(pg, pg, pg, pg, ps, convw, convw, convw, a_log, dt_bias, gnorm)


def _gdn_bwd(pg, ps, convw, a_log, dt_bias, gnorm, d_oa, o_raw, s_all, b_loc, t):
    n = b_loc * t
    assert t % (CHUNK * GDN_GROUP) == 0 and CHUNK == LANES, (t, CHUNK, GDN_GROUP)
    nc = t // CHUNK
    col, ps_spec, wcol, smem, vec = _gdn_specs(b_loc, t)

    def body(q_ref, k_ref, v_ref, z_ref, ps_ref, wq_ref, wk_ref, wv_ref, alog_ref, dtb_ref, gn_ref,
             doa_ref, oraw_ref, s_ref,
             dq_ref, dk_ref, dv_ref, dz_ref, dps_ref, dcw_ref, dsm_ref,
             qn, kn, vv, g128, g64, b128, b64, do_s, bg_s, u_s, vn_s, dvn_s, dcy_s, kk_s, qk_s, tm_s, dsn_s, pad_s):
        b, h = pl.program_id(0), pl.program_id(1)
        g, beta, _, _ = _gdn_gates(ps_ref[...], h, alog_ref, dtb_ref)
        g128[...] = jnp.broadcast_to(g, (t, LANES))
        g64[...] = jnp.broadcast_to(g, (t, CHUNK))
        b128[...] = jnp.broadcast_to(beta, (t, LANES))
        b64[...] = jnp.broadcast_to(beta, (t, CHUNK))
        _pad_zero(pad_s)

        def prep(x_ref, w_ref):
            p = _conv(x_ref[...], w_ref, pad_s)
            sg = _sigmoid(p)
            return p, sg, p * sg

        _, _, yq = prep(q_ref, wq_ref)
        qn[...] = yq * (lax.rsqrt(jnp.sum(yq * yq, axis=1, keepdims=True) + EPS) * (DH ** -0.5))
        _, _, yk = prep(k_ref, wk_ref)
        kn[...] = yk * lax.rsqrt(jnp.sum(yk * yk, axis=1, keepdims=True) + EPS)
        _, _, yv = prep(v_ref, wv_ref)
        vv[...] = yv

        o = oraw_ref[...]
        z = z_ref[...]
        doa = doa_ref[...]
        gn = gn_ref[...]
        ro = lax.rsqrt(jnp.mean(o * o, axis=1, keepdims=True) + EPS)
        sz = _sigmoid(z)
        dz_ref[...] = _bf(doa * (o * ro * gn) * (sz * (1.0 + z * (1.0 - sz))))
        dn = doa * (z * sz)
        dgn = jnp.sum(dn * o * ro, axis=0, keepdims=True)
        gy = dn * gn
        do_s[...] = ro * gy - o * (ro * ro * ro * (1.0 / DH)) * jnp.sum(gy * o, axis=1, keepdims=True)

        cs = _chunk_consts()

        def pre_group(gi, _):
            idx = [gi * GDN_GROUP + c for c in range(GDN_GROUP)]
            rows = [pl.ds(pl.multiple_of(i * CHUNK, CHUNK), CHUNK) for i in idx]
            ins = [_V([ref[r, :] for r in rows]) for ref in (qn, kn, vv, g128, g64, b128, b64)]
            states = _V([_bf(s_ref[0, 0, i]) for i in idx])
            f = _gdn_chunk_pre(*ins, cs)
            v_new = f["w"] - _vdot(_vbf(f["u"]), states)
            for c, r in enumerate(rows):
                bg_s[r, :] = f["big_g"].xs[c]
                u_s[r, :] = f["u"].xs[c]
                vn_s[r, :] = v_new.xs[c]
                dcy_s[r, :] = f["decay"].xs[c]
                kk_s[r, :] = f["kk"].xs[c]
                qk_s[r, :] = f["qk"].xs[c]
                tm_s[r, :] = f["tm"].xs[c]
            return 0

        lax.fori_loop(0, nc // GDN_GROUP, pre_group, 0)

        def chunk(j, ds):
            i = nc - 1 - j
            r = pl.ds(pl.multiple_of(i * CHUNK, CHUNK), CHUNK)
            big_g = bg_s[r, :]
            g_last = jnp.sum(g128[r, :], axis=0, keepdims=True)
            dob = _bf(do_s[r, :])
            dv_new = (_dot_tn(_bf(qk_s[r, :] * dcy_s[r, :]), dob)
                      + _dot(_bf(kn[r, :] * jnp.exp(g_last - big_g)), _bf(ds)))
            dvn_s[r, :] = dv_new
            dsn_s[i] = ds
            return (_dot_tn(_bf(qn[r, :] * jnp.exp(big_g)), dob) + jnp.exp(g_last) * ds
                    - _dot_tn(_bf(u_s[r, :]), _bf(dv_new)))

        lax.fori_loop(0, nc, chunk, jnp.zeros((DH, DH), F32))

        def post_group(gi, _):
            idx = [gi * GDN_GROUP + c for c in range(GDN_GROUP)]
            rows = [pl.ds(pl.multiple_of(i * CHUNK, CHUNK), CHUNK) for i in idx]
            def rows_of(ref):
                return _V([ref[r, :] for r in rows])

            dq, dk, dv, dg, dbeta = _gdn_chunk_post(
                rows_of(qn), rows_of(kn), rows_of(vv), rows_of(g128), rows_of(b128), rows_of(b64),
                _V([s_ref[0, 0, i] for i in idx]), _V([dsn_s[i] for i in idx]), rows_of(do_s), rows_of(dvn_s),
                rows_of(bg_s), rows_of(dcy_s), rows_of(kk_s), rows_of(qk_s), rows_of(tm_s), rows_of(u_s), rows_of(vn_s),
                cs)
            for c, r in enumerate(rows):
                qn[r, :] = dq.xs[c]
                kn[r, :] = dk.xs[c]
                vv[r, :] = dv.xs[c]
                g128[r, :] = dg.xs[c]
                b128[r, :] = jnp.broadcast_to(dbeta.xs[c], (CHUNK, LANES))
            return 0

        lax.fori_loop(0, nc // GDN_GROUP, post_group, 0)
        dqh, dkh, dvh = qn, kn, vv

        g, beta, a, sp_in = _gdn_gates(ps_ref[...], h, alog_ref, dtb_ref)
        dg = g128[...]
        d_ga = dg * (-a) * _sigmoid(sp_in)
        d_alog = jnp.sum(dg * g, axis=0, keepdims=True)
        d_dtb = jnp.sum(d_ga, axis=0, keepdims=True)
        d_gb = b128[...] * (beta * (1.0 - beta))
        lane = lax.broadcasted_iota(jnp.int32, (t, LANES), 1)
        contrib = jnp.where(lane == LANE_GA + h, d_ga, 0.0) + jnp.where(lane == LANE_GB + h, d_gb, 0.0)

        @pl.when(h == 0)
        def _():
            dps_ref[...] = jnp.zeros_like(dps_ref)

        dps_ref[...] += contrib

        lane1 = lax.broadcasted_iota(jnp.int32, (1, LANES), 1)
        small = _stack_rows([jnp.where(lane1 == h, d_alog, 0.0), jnp.where(lane1 == h, d_dtb, 0.0), dgn], 8)

        @pl.when((b == 0) & (h == 0))
        def _():
            dsm_ref[...] = jnp.zeros_like(dsm_ref)
            dcw_ref[...] = jnp.zeros_like(dcw_ref)

        dsm_ref[...] += small

        def conv_bwd(dp, x, w_ref, slot):
            dw = _stack_rows([jnp.sum(dp * _shifted(pad_s, 3), axis=0, keepdims=True),
                              jnp.sum(dp * _shifted(pad_s, 2), axis=0, keepdims=True),
                              jnp.sum(dp * _shifted(pad_s, 1), axis=0, keepdims=True),
                              jnp.sum(dp * x, axis=0, keepdims=True)], CONV_K)
            dcw_ref[slot] += dw
            pad_s[PAD:PAD + t, :] = dp
            dx = _shifted(pad_s, -3) * w_ref[0:1, :]
            dx = dx + _shifted(pad_s, -2) * w_ref[1:2, :]
            dx = dx + _shifted(pad_s, -1) * w_ref[2:3, :]
            return dx + dp * w_ref[3:4, :]

        def l2_bwd(dqn, y, c):
            r = lax.rsqrt(jnp.sum(y * y, axis=1, keepdims=True) + EPS)
            s1 = jnp.sum(dqn * y, axis=1, keepdims=True)
            return c * r * dqn - (c * r * r * r) * s1 * y

        def silu_bwd(p, sg):
            return sg * (1.0 + p * (1.0 - sg))

        pq, sq, yq = prep(q_ref, wq_ref)
        dq_ref[...] = _bf(conv_bwd(l2_bwd(dqh[...], yq, DH ** -0.5) * silu_bwd(pq, sq), q_ref[...], wq_ref, h))
        pk, sk, yk = prep(k_ref, wk_ref)
        dk_ref[...] = _bf(conv_bwd(l2_bwd(dkh[...], yk, 1.0) * silu_bwd(pk, sk), k_ref[...], wk_ref, HEADS + h))
        pv, sv, _ = prep(v_ref, wv_ref)
        dv_ref[...] = _bf(conv_bwd(dvh[...] * silu_bwd(pv, sv), v_ref[...], wv_ref, 2 * HEADS + h))

    blk = pl.BlockSpec((t, DH), lambda b, h: (b, h))
    ob = jax.ShapeDtypeStruct((n, HEADS * DH), BF16)
    return pl.pallas_call(
        body, name="gdn_bwd", grid=(b_loc, HEADS),
        in_specs=[col(0), col(HEADS), col(2 * HEADS), col(3 * HEADS), ps_spec, wcol(0), wcol(HEADS), wcol(2 * HEADS),
                  smem, smem, vec, blk, blk, pl.BlockSpec((1, 1, nc, DH, DH), lambda b, h: (b, h, 0, 0, 0))],
        out_specs=[blk, blk, blk, blk, ps_spec,
                   pl.BlockSpec((3 * HEADS, CONV_K, DH), lambda b, h: (0, 0, 0)),
                   pl.BlockSpec((8, LANES), lambda b, h: (0, 0))],
        out_shape=[ob, ob, ob, ob, jax.ShapeDtypeStruct((n, LANES), F32),
                   jax.ShapeDtypeStruct((3 * HEADS, CONV_K, DH), F32), jax.ShapeDtypeStruct((8, LANES), F32)],
        scratch_shapes=([pltpu.VMEM((t, DH), F32)] * 3 + [pltpu.VMEM((t, LANES), F32), pltpu.VMEM((t, CHUNK), F32)] * 2
                        + [pltpu.VMEM((t, DH), F32)] * 5 + [pltpu.VMEM((t, CHUNK), F32)] * 4
                        + [pltpu.VMEM((nc, DH, DH), F32), pltpu.VMEM((t + 2 * PAD, LANES), F32)]),
        compiler_params=_cparams(("arbitrary", "arbitrary")),
    )(pg, pg, pg, pg, ps, convw, convw, convw, a_log, dt_bias, gnorm, d_oa, o_raw, s_all)


def _fox_prologue(q_ref, k_ref, v_ref, ps_ref, fb_ref, gq_ref, gk_ref, h, t, qs, ks, vs, ccol, crow):
    nb = t // FOX_BLOCK
    q, k = q_ref[...], k_ref[...]
    rq = lax.rsqrt(jnp.mean(q * q, axis=1, keepdims=True) + EPS)
    rk = lax.rsqrt(jnp.mean(k * k, axis=1, keepdims=True) + EPS)
    qs[...] = _bf(q * rq * gq_ref[...])
    ks[...] = _bf(k * rk * gk_ref[...])
    vs[...] = _bf(v_ref[...])
    f_in = _head_lane(ps_ref[...], LANE_FF + h) + fb_ref[0, h]
    ccol[...] = jnp.broadcast_to(-_softplus(-f_in), (t, LANES))
    r = lax.broadcasted_iota(jnp.int32, (FOX_BLOCK, FOX_BLOCK), 0)
    c = lax.broadcasted_iota(jnp.int32, (FOX_BLOCK, FOX_BLOCK), 1)
    trilf, triuf = (r >= c).astype(F32), (r <= c).astype(F32)
    blocks = [pl.ds(j * FOX_BLOCK, FOX_BLOCK) for j in range(nb)]
    lfs = _V([ccol[rb, :] for rb in blocks])
    cc = _dot_exact_l(trilf, lfs, terms=3)
    cr = _dot_exact_r(lfs, triuf, _vdot_tn, terms=3)
    sums = _vsum(lfs, 0)
    carry = jnp.zeros((1, LANES), F32)
    for j, rb in enumerate(blocks):
        ccol[rb, :] = cc.xs[j] + carry
        crow[j] = (cr.xs[j] + carry)[0:8]
        carry = carry + sums.xs[j]
    return rq, rk, f_in


def _fox_scores(q_rows, k_rows, cc, cr, row0, col0):
    s = _dot_nt(q_rows, k_rows) * (DH ** -0.5) + cc - cr
    r = lax.broadcasted_iota(jnp.int32, s.shape, 0)
    c = lax.broadcasted_iota(jnp.int32, s.shape, 1)
    return jnp.where(row0 + r >= col0 + c, s, NEG)


def _fox_specs(t):
    def col(off):
        return pl.BlockSpec((t, DH), lambda b, h: (b, off + h))

    ps_spec = pl.BlockSpec((t, LANES), lambda b, h: (b, 0))
    smem = pl.BlockSpec(memory_space=pltpu.SMEM)
    vec = pl.BlockSpec((1, DH), lambda b, h: (0, 0))
    blk = pl.BlockSpec((t, DH), lambda b, h: (b, h))
    return col, ps_spec, smem, vec, blk


def _fox_fwd(pf, ps, f_bias, gq, gk, b_loc, t):
    n = b_loc * t
    nb = t // FOX_BLOCK
    assert t % FOX_TILE == 0 and FOX_TILE % FOX_SHORT == 0, (t, FOX_TILE, FOX_SHORT)
    kt = FOX_TILE
    nsub = kt // FOX_BLOCK
    col, ps_spec, smem, vec, blk = _fox_specs(t)

    def body(q_ref, k_ref, v_ref, ps_ref, fb_ref, gq_ref, gk_ref, o_ref, lse_ref, qs, ks, vs, ccol, crow):
        h = pl.program_id(1)
        _fox_prologue(q_ref, k_ref, v_ref, ps_ref, fb_ref, gq_ref, gk_ref, h, t, qs, ks, vs, ccol, crow)

        def qblock(i, _):
            ri = pl.ds(pl.multiple_of(i * FOX_SHORT, FOX_SHORT), FOX_SHORT)
            qi = qs[ri, :]
            cc = jnp.concatenate([ccol[ri, :]] * nsub, axis=1)

            def ktile(j, carry):
                m, l, acc = carry
                rj = pl.ds(pl.multiple_of(j * kt, kt), kt)
                cr = jnp.concatenate([crow[j * nsub + u, 0:1, :] for u in range(nsub)], axis=1)
                s = _fox_scores(qi, ks[rj, :], cc, cr, i * FOX_SHORT, j * kt)
                m_new = jnp.maximum(m, jnp.max(s, axis=1, keepdims=True))
                p = jnp.exp(s - m_new)
                alpha = jnp.exp(m - m_new)
                l = alpha * l + jnp.sum(p, axis=1, keepdims=True)
                acc = alpha * acc + _dot(_bf(p), vs[rj, :])
                return m_new, l, acc

            m, l, acc = lax.fori_loop(0, (i * FOX_SHORT) // kt + 1, ktile, (jnp.full((FOX_SHORT, 1), NEG, F32),
                                                                            jnp.zeros((FOX_SHORT, 1), F32),
                                                                            jnp.zeros((FOX_SHORT, DH), F32)))
            o_ref[ri, :] = acc / l
            lse_ref[ri, :] = jnp.broadcast_to(m + jnp.log(l), (FOX_SHORT, LANES))
            return 0

        lax.fori_loop(0, t // FOX_SHORT, qblock, 0)

    o = jax.ShapeDtypeStruct((n, HEADS * DH), F32)
    return pl.pallas_call(
        body, name="fox_fwd", grid=(b_loc, HEADS),
        in_specs=[col(0), col(HEADS), col(2 * HEADS), ps_spec, smem, vec, vec],
        out_specs=[blk, blk], out_shape=[o, o],
        scratch_shapes=[pltpu.VMEM((t, DH), BF16)] * 3 + [pltpu.VMEM((t, LANES), F32), pltpu.VMEM((nb, 8, LANES), F32)],
        compiler_params=_cparams(("arbitrary", "arbitrary")),
    )(pf, pf, pf, ps, f_bias, gq, gk)


def _fox_bwd(pf, ps, f_bias, gq, gk, d_ob, ob, lse, dps_in, b_loc, t):
    n = b_loc * t
    nb = t // FOX_BLOCK
    assert t % FOX_TILE == 0 and FOX_TILE % FOX_SHORT == 0, (t, FOX_TILE, FOX_SHORT)
    qt = FOX_TILE
    scale = DH ** -0.5
    col, ps_spec, smem, vec, blk = _fox_specs(t)

    def body(q_ref, k_ref, v_ref, ps_ref, fb_ref, gq_ref, gk_ref, do_ref, o_ref, lse_ref, dpsi_ref,
             dq_ref, dk_ref, dv_ref, dps_ref, dsm_ref, qs, ks, vs, ccol, crow, dos, dl, dqa, dcr, dcq):
        b, h = pl.program_id(0), pl.program_id(1)
        rq, _, f_in = _fox_prologue(q_ref, k_ref, v_ref, ps_ref, fb_ref, gq_ref, gk_ref, h, t, qs, ks, vs, ccol, crow)
        dov = do_ref[...]
        dos[...] = _bf(dov)
        dl[...] = jnp.broadcast_to(jnp.sum(dov * o_ref[...], axis=1, keepdims=True), (t, LANES))
        dqa[...] = jnp.zeros_like(dqa)
        dcq[...] = jnp.zeros_like(dcq)
        gkv = gk_ref[...]

        ksub = FOX_SHORT // FOX_BLOCK

        def kblock(j, dgk):
            rj = pl.ds(pl.multiple_of(j * FOX_SHORT, FOX_SHORT), FOX_SHORT)
            kj, vj = ks[rj, :], vs[rj, :]
            cr = jnp.concatenate([crow[j * ksub + u, 0:1, :] for u in range(ksub)], axis=1)

            def wide(x):
                return jnp.concatenate([x] * ksub, axis=1)

            def qtile(i, carry):
                dk_acc, dv_acc, dc = carry
                ri = pl.ds(pl.multiple_of(i * qt, qt), qt)
                qi, doi = qs[ri, :], dos[ri, :]
                s = _fox_scores(qi, kj, wide(ccol[ri, :]), cr, i * qt, j * FOX_SHORT)
                p = jnp.exp(s - wide(lse_ref[ri, :]))
                ds = p * (_dot_nt(doi, vj) - wide(dl[ri, :]))
                dsb = _bf(ds)
                dqa[ri, :] += _dot(dsb, kj)
                dcq[ri, :] += jnp.broadcast_to(jnp.sum(ds, axis=1, keepdims=True), (qt, LANES))
                return (dk_acc + _dot_tn(dsb, qi), dv_acc + _dot_tn(_bf(p), doi),
                        dc - jnp.sum(ds, axis=0, keepdims=True))

            z = jnp.zeros((FOX_SHORT, DH), F32)
            dk_acc, dv_acc, dc = lax.fori_loop((j * FOX_SHORT) // qt, t // qt, qtile,
                                               (z, z, jnp.zeros((1, FOX_SHORT), F32)))
            dv_ref[rj, :] = _bf(dv_acc)
            for u in range(ksub):
                dcr[pl.ds(pl.multiple_of((j * ksub + u) * 8, 8), 8), :] = jnp.broadcast_to(
                    dc[:, u * FOX_BLOCK:(u + 1) * FOX_BLOCK], (8, LANES))
            kraw = k_ref[rj, :]
            rk = lax.rsqrt(jnp.mean(kraw * kraw, axis=1, keepdims=True) + EPS)
            dkn = dk_acc * scale
            gy = dkn * gkv
            dk_ref[rj, :] = _bf(rk * gy - kraw * (rk * rk * rk * (1.0 / DH)) * jnp.sum(gy * kraw, axis=1, keepdims=True))
            return dgk + jnp.sum(dkn * kraw * rk, axis=0, keepdims=True)

        dgk = lax.fori_loop(0, t // FOX_SHORT, kblock, jnp.zeros((1, DH), F32))

        q = q_ref[...]
        dqn = dqa[...] * scale
        gy = dqn * gq_ref[...]
        dq_ref[...] = _bf(rq * gy - q * (rq * rq * rq * (1.0 / DH)) * jnp.sum(gy * q, axis=1, keepdims=True))
        dgq = jnp.sum(dqn * q * rq, axis=0, keepdims=True)

        r = lax.broadcasted_iota(jnp.int32, (FOX_BLOCK, FOX_BLOCK), 0)
        c = lax.broadcasted_iota(jnp.int32, (FOX_BLOCK, FOX_BLOCK), 1)
        triuf = (r <= c).astype(F32)

        def rev(jj, carry):
            j = nb - 1 - jj
            rows = pl.ds(pl.multiple_of(j * FOX_BLOCK, FOX_BLOCK), FOX_BLOCK)
            rowv = dcr[pl.ds(pl.multiple_of(j * 8, 8), 1), :]
            colv = jnp.sum(jnp.where(c >= r, jnp.broadcast_to(rowv, (FOX_BLOCK, LANES)), 0.0), axis=1, keepdims=True)
            qcol = dcq[rows, :]
            dl[rows, :] = colv + _dot_exact_l(triuf, qcol, terms=3) + carry
            return carry + jnp.sum(rowv, axis=1, keepdims=True) + jnp.sum(qcol, axis=0, keepdims=True)

        lax.fori_loop(0, nb, rev, jnp.zeros((1, LANES), F32))
        d_ff = dl[...] * _sigmoid(-f_in)
        lane = lax.broadcasted_iota(jnp.int32, (t, LANES), 1)

        @pl.when(h == 0)
        def _():
            dps_ref[...] = dpsi_ref[...]

        dps_ref[...] += jnp.where(lane == LANE_FF + h, d_ff, 0.0)

        lane1 = lax.broadcasted_iota(jnp.int32, (1, LANES), 1)
        d_fb = jnp.sum(d_ff, axis=0, keepdims=True)
        small = _stack_rows([dgq, dgk, jnp.where(lane1 == h, d_fb, 0.0)], 8)

        @pl.when((b == 0) & (h == 0))
        def _():
            dsm_ref[...] = jnp.zeros_like(dsm_ref)

        dsm_ref[...] += small

    ob_ = jax.ShapeDtypeStruct((n, HEADS * DH), BF16)
    return pl.pallas_call(
        body, name="fox_bwd", grid=(b_loc, HEADS),
        in_specs=[col(0), col(HEADS), col(2 * HEADS), ps_spec, smem, vec, vec, blk, blk, blk, ps_spec],
        out_specs=[blk, blk, blk, ps_spec, pl.BlockSpec((8, LANES), lambda b, h: (0, 0))],
        out_shape=[ob_, ob_, ob_, jax.ShapeDtypeStruct((n, LANES), F32), jax.ShapeDtypeStruct((8, LANES), F32)],
        scratch_shapes=([pltpu.VMEM((t, DH), BF16)] * 3 + [pltpu.VMEM((t, LANES), F32), pltpu.VMEM((nb, 8, LANES), F32)]
                        + [pltpu.VMEM((t, DH), BF16), pltpu.VMEM((t, LANES), F32), pltpu.VMEM((t, DH), F32),
                           pltpu.VMEM((8 * nb, LANES), F32), pltpu.VMEM((t, LANES), F32)]),
        compiler_params=_cparams(("arbitrary", "arbitrary")),
    )(pf, pf, pf, ps, f_bias, gq, gk, d_ob, ob, lse, dps_in)


class _NoExchange:
    def late_weights(self, after):
        return {}

    def grads_ready(self, grads, tie):
        return tie


def _local_step(x, target, w, b_loc, t, comm=None):
    comm = comm or _NoExchange()
    w = dict(w)
    xf = x
    u = _rms_fwd(xf, w["norm_mix_g"], "rms_mix")
    pg = _mm(u, w["w_gdn"], name="proj_gdn")
    pf = _mm(u, w["w_fox"], name="proj_fox")
    pgate = _mm(u, w["w_gate"], name="proj_gate")
    ps = _mm(u, w["w_small"], name="proj_small")
    oa, o_raw, s_all = _gdn_fwd(pg, ps, w["conv_w"], w["a_log"], w["dt_bias"], w["gdn_norm_g"], b_loc, t)
    ob, lse = _fox_fwd(pf, ps, w["f_bias"], w["fox_q_norm_g"], w["fox_k_norm_g"], b_loc, t)
    w.update(comm.late_weights(ob))
    ya = _mm(oa, w["w_proj_gdn"], name="proj_a")
    yb = _mm(ob, w["w_proj_fox"], name="proj_b")
    merged = _merge_fwd(ya, yb, pgate)
    h = _mm(merged, w["w_out"], name="proj_out", epi=lambda acc, xr: acc + xr, extras=(xf,))
    hn = _rms_fwd(h, w["norm_mlp_g"], "rms_mlp")
    up, act = _mm(hn, w["w_up"], name="mlp_up", out_dtype=BF16, out2=(_relu2, BF16))
    out = _mm(act, w["w_down"], name="mlp_down", epi=lambda acc, hr: acc + hr, extras=(h,))
    d_out, d_out16, loss_blk = _loss_bwd(out, target)

    g = {}
    g["w_down"] = _mm(act, d_out16, name="dw_down", ta=True, out_dtype=BF16)
    d_up = _mm(d_out16, w["w_down"], name="d_up", tb=True, out_dtype=BF16,
               epi=lambda acc, upr: acc * (2.0 * jnp.maximum(upr.astype(F32), 0.0)), extras=(up,))
    g["w_up"] = _mm(hn, d_up, name="dw_up", ta=True, out_dtype=BF16)
    mlp_gain = comm.grads_ready({"w_down": g["w_down"], "w_up": g["w_up"]}, w["norm_mlp_g"])
    d_hn = _mm(d_up, w["w_up"], name="d_hn", tb=True)
    dh, dh16, g["norm_mlp_g"] = _rms_bwd(d_hn, h, mlp_gain, d_out, "rms_mlp_bwd")
    g["w_out"] = _mm(merged, dh16, name="dw_out", ta=True, out_dtype=BF16)
    dm = _mm(dh16, w["w_out"], name="d_merged", tb=True)
    dya, dyb, dgate_a, dgate_b = _merge_bwd(dm, ya, yb, pgate)
    g["w_proj_gdn"] = _mm(oa, dya, name="dw_proj_a", ta=True, out_dtype=BF16)
    g["w_proj_fox"] = _mm(ob, dyb, name="dw_proj_b", ta=True, out_dtype=BF16)
    gdn_gain = comm.grads_ready({"w_out": g["w_out"], "w_proj_gdn": g["w_proj_gdn"], "w_proj_fox": g["w_proj_fox"]},
                                w["gdn_norm_g"])
    d_oa = _mm(dya, w["w_proj_gdn"], name="d_oa", tb=True)
    d_ob = _mm(dyb, w["w_proj_fox"], name="d_ob", tb=True)
    dgq, dgk, dgv, dgz, dps, dcw, gdn_small = _gdn_bwd(pg, ps, w["conv_w"], w["a_log"], w["dt_bias"], gdn_gain,
                                                       d_oa, o_raw, s_all, b_loc, t)
    dfq, dfk, dfv, dps, fox_small = _fox_bwd(pf, ps, w["f_bias"], w["fox_q_norm_g"], w["fox_k_norm_g"],
                                             d_ob, ob, lse, dps, b_loc, t)
    segs = [(dgq, "w_gdn", 0), (dgk, "w_gdn", 1024), (dgv, "w_gdn", 2048), (dgz, "w_gdn", 3072),
            (dfq, "w_fox", 0), (dfk, "w_fox", 1024), (dfv, "w_fox", 2048),
            (dgate_a, "w_gate", 0), (dgate_b, "w_gate", 1024)]
    dws = [_mm(u, dps, name="dw_small", ta=True, out_dtype=BF16)]
    dws += [_mm(u, dseg, name=f"dw_in_{idx}", ta=True, out_dtype=BF16) for idx, (dseg, _, _) in enumerate(segs)]
    g["w_in_parts"] = dws
    mix_gain = comm.grads_ready({"w_in_parts": dws}, w["norm_mix_g"])
    du = _du_all(dps, w["w_small"], segs, w)
    grad_x, _, g["norm_mix_g"] = _rms_bwd(du, xf, mix_gain, dh, "rms_mix_bwd")
    g["conv"] = dcw
    g["gdn_small"] = gdn_small
    g["fox_small"] = fox_small
    return loss_blk, grad_x, g


def _position():
    x, y, c = lax.axis_index("x"), lax.axis_index("y"), lax.axis_index("c")
    return x, y, c


def _to_bf16(arrs, name):
    n = len(arrs)

    def body(*refs):
        for i in range(n):
            refs[n + i][...] = _bf(refs[i][...])

    return pl.pallas_call(
        body, name=name,
        out_shape=[jax.ShapeDtypeStruct(a.shape, BF16) for a in arrs],
        compiler_params=_cparams(),
    )(*arrs)


def _all_gather(arrs, name):
    n = len(arrs)
    hbm = pl.BlockSpec(memory_space=pl.ANY)

    def body(*refs):
        ins, outs = refs[:n], refs[n:2 * n]
        send, recv, loc = refs[2 * n:]
        x, y, c = _position()
        me = 4 * x + 2 * y + c
        sibling = (x, y, 1 - c)
        chips = [(1 - x, y), (x, 1 - y), (1 - x, 1 - y)]

        def idx(px, py, pc):
            return 4 * px + 2 * py + pc

        def cp(a, k, block, to, src=None):
            return pltpu.make_async_remote_copy(
                src_ref=outs[a].at[block] if src is None else src, dst_ref=outs[a].at[block],
                send_sem=send.at[a, k], recv_sem=recv.at[a, k], device_id=to, device_id_type=MESH)

        mine = [pltpu.make_async_copy(ins[a], outs[a].at[me], loc.at[a]) for a in range(n)]
        for m in mine:
            m.start()
        first = []
        for a in range(n):
            first.append(cp(a, 0, me, sibling, src=ins[a]))
            first += [cp(a, 1 + j, me, (*chip, c), src=ins[a]) for j, chip in enumerate(chips)]
        for f in first:
            f.start()
        passed = []
        for j, chip in enumerate(chips):
            for a in range(n):
                cp(a, 1 + j, idx(*chip, c), (x, y, c)).wait_recv()
                p = cp(a, 4 + j, idx(*chip, c), sibling)
                p.start()
                passed.append(p)
        for a in range(n):
            cp(a, 0, idx(x, y, 1 - c), (x, y, c)).wait_recv()
            for j, chip in enumerate(chips):
                cp(a, 4 + j, idx(*chip, 1 - c), (x, y, c)).wait_recv()
        for f in first + passed:
            f.wait_send()
        for m in mine:
            m.wait()

    return pl.pallas_call(
        body, name=name,
        in_specs=[hbm] * n, out_specs=[hbm] * n,
        out_shape=[jax.ShapeDtypeStruct((N_DEV,) + a.shape, a.dtype) for a in arrs],
        scratch_shapes=[pltpu.SemaphoreType.DMA((n, 7)), pltpu.SemaphoreType.DMA((n, 7)), pltpu.SemaphoreType.DMA((n,))],
        compiler_params=pltpu.CompilerParams(has_side_effects=True),
    )(*arrs)


def _peer(x, y, c, rel):
    return ((1 - x) if rel & 4 else x, (1 - y) if rel & 2 else y, (1 - c) if rel & 1 else c)


HBM_SPEC = pl.BlockSpec(memory_space=pltpu.HBM)
SEM_SPEC = pl.BlockSpec(memory_space=pltpu.SEMAPHORE)
DATAFLOW = pltpu.SideEffectType.DATAFLOW_SIDE_EFFECTING


CHIP_RELS = (2, 4, 6)


def _push_start(arrs, slots, name, chips=False):
    n = len(arrs)
    n_slots = 4 if chips else N_DEV
    rels = CHIP_RELS if chips else tuple(range(1, N_DEV))
    land_shapes = [a.shape if slots else (n_slots,) + a.shape for a in arrs]

    def body(*refs):
        ins, lands, sends, recvs, token = refs[:n], refs[n:2 * n], refs[2 * n:3 * n], refs[3 * n:4 * n], refs[-1]
        x, y, c = _position()
        for rel in rels:
            px, py, pc = _peer(x, y, c, rel)
            mine, theirs = (2 * x + y, 2 * px + py) if chips else (4 * x + 2 * y + c, 4 * px + 2 * py + pc)
            for a in range(n):
                pltpu.make_async_remote_copy(
                    src_ref=ins[a].at[theirs] if slots else ins[a], dst_ref=lands[a].at[mine],
                    send_sem=sends[a], recv_sem=recvs[a], device_id=(px, py, pc), device_id_type=MESH).start()
        token[...] = jnp.zeros_like(token)

    sem = pltpu.SemaphoreType.DMA(())
    outs = pl.pallas_call(
        body, name=name,
        in_specs=[HBM_SPEC] * (2 * n),
        out_shape=(*[sem] * (2 * n), *[pltpu.HBM(a.shape, a.dtype) for a in arrs],
                   *[pltpu.HBM(s, a.dtype) for s, a in zip(land_shapes, arrs)], jax.ShapeDtypeStruct((8, LANES), F32)),
        out_specs=(*[SEM_SPEC] * (2 * n), *[HBM_SPEC] * (2 * n), pl.BlockSpec(memory_space=pltpu.VMEM)),
        input_output_aliases={i: 2 * n + i for i in range(2 * n)},
        compiler_params=pltpu.CompilerParams(has_side_effects=DATAFLOW),
    )(*[pltpu.with_memory_space_constraint(a, pltpu.HBM) for a in arrs],
      *[pltpu.with_memory_space_constraint(lax.empty(s, a.dtype), pltpu.HBM) for s, a in zip(land_shapes, arrs)])
    return dict(sends=list(outs[:n]), recvs=list(outs[n:2 * n]), ins=list(outs[2 * n:3 * n]),
                lands=list(outs[3 * n:4 * n]), token=outs[-1], copies=len(rels))


def _push_wait(started, after, name):
    n = len(started["ins"])
    copies = started["copies"]

    def body(*refs):
        lands, sends, recvs = refs[n:2 * n], refs[2 * n:3 * n], refs[3 * n:4 * n]
        x, y, c = _position()
        for a in range(n):
            every = lands[a].at[pl.ds(0, copies)]
            drain = pltpu.make_async_remote_copy(src_ref=every, dst_ref=every, send_sem=sends[a], recv_sem=recvs[a],
                                                 device_id=(x, y, c), device_id_type=MESH)
            drain.wait_send()
            drain.wait_recv()

    both = started["ins"] + started["lands"]
    outs = pl.pallas_call(
        body, name=name,
        in_specs=[HBM_SPEC] * (2 * n) + [SEM_SPEC] * (2 * n) + [pl.BlockSpec(memory_space=pl.ANY)],
        out_shape=tuple(pltpu.HBM(a.shape, a.dtype) for a in both), out_specs=tuple([HBM_SPEC] * (2 * n)),
        input_output_aliases={i: i for i in range(2 * n)},
        compiler_params=pltpu.CompilerParams(has_side_effects=DATAFLOW),
    )(*both, *started["sends"], *started["recvs"], after)
    return list(outs[:n]), list(outs[n:])


def _sibling_swap(arr, name):
    chips = N_DEV // 2

    def body(in_ref, out_ref, send, recv):
        x, y, c = _position()
        for s in range(chips):
            pltpu.make_async_remote_copy(src_ref=in_ref.at[2 * s + 1 - c], dst_ref=out_ref.at[s], send_sem=send,
                                         recv_sem=recv, device_id=(x, y, 1 - c), device_id_type=MESH).start()
        pltpu.make_async_remote_copy(src_ref=out_ref, dst_ref=out_ref, send_sem=send, recv_sem=recv,
                                     device_id=(x, y, 1 - c), device_id_type=MESH).wait()

    hbm = pl.BlockSpec(memory_space=pl.ANY)
    return pl.pallas_call(
        body, name=name, in_specs=[hbm], out_specs=hbm,
        out_shape=jax.ShapeDtypeStruct((chips,) + arr.shape[1:], arr.dtype),
        scratch_shapes=[pltpu.SemaphoreType.DMA, pltpu.SemaphoreType.DMA],
        compiler_params=pltpu.CompilerParams(has_side_effects=True),
    )(arr)


def _add_halves(core, arr, other, name):
    ns, r, c = other.shape
    tr = min(r, 256)

    def body(core_ref, a_ref, o_ref, out_ref):
        out_ref[...] = _bf(a_ref[...].astype(F32) + o_ref[...].astype(F32))

    blk = pl.BlockSpec((1, tr, c), lambda s, i, core_ref: (s, i, 0))
    return pl.pallas_call(
        body, name=name,
        grid_spec=pltpu.PrefetchScalarGridSpec(
            num_scalar_prefetch=1, grid=(ns, r // tr),
            in_specs=[pl.BlockSpec((1, tr, c), lambda s, i, core_ref: (2 * s + core_ref[0], i, 0)), blk],
            out_specs=blk),
        out_shape=jax.ShapeDtypeStruct((ns, r, c), BF16),
        compiler_params=_cparams(("parallel", "parallel")),
    )(core, arr, other)


def _all_reduce_small(buf, name):
    rows = buf.shape[0]

    def body(in_ref, out_ref, slots, send, recv):
        x, y, c = _position()
        me = 4 * x + 2 * y + c
        slots[me] = in_ref[...]
        copies = []
        for rel in range(1, N_DEV):
            copies.append(pltpu.make_async_remote_copy(
                src_ref=in_ref, dst_ref=slots.at[me], send_sem=send.at[rel - 1], recv_sem=recv.at[rel - 1],
                device_id=_peer(x, y, c, rel), device_id_type=MESH))
        for cpy in copies:
            cpy.start()
        for cpy in copies:
            cpy.wait()
        tot = slots[0]
        for d in range(1, N_DEV):
            tot = tot + slots[d]
        out_ref[...] = tot

    return pl.pallas_call(
        body, name=name,
        out_shape=jax.ShapeDtypeStruct((rows, LANES), F32),
        in_specs=[pl.BlockSpec(memory_space=pltpu.VMEM)], out_specs=pl.BlockSpec(memory_space=pltpu.VMEM),
        scratch_shapes=[pltpu.VMEM((N_DEV, rows, LANES), F32), pltpu.SemaphoreType.DMA((7,)),
                        pltpu.SemaphoreType.DMA((7,))],
        compiler_params=pltpu.CompilerParams(has_side_effects=True),
    )(buf)


def _adam_math(g, w, m, v):
    m = ADAM_B1 * m + (1.0 - ADAM_B1) * g
    v = ADAM_B2 * v + (1.0 - ADAM_B2) * (g * g)
    m_hat = m / (1.0 - ADAM_B1 ** ADAM_STEP)
    v_hat = v / (1.0 - ADAM_B2 ** ADAM_STEP)
    delta = -ADAM_LR * (m_hat / (jnp.sqrt(v_hat) + ADAM_EPS) + ADAM_WD * w)
    return delta, m, v


def _adam_shard(me, parts, mine, w, m, v, name):
    r, c = w.shape
    tr = min(r, 128)
    n_slots = parts.shape[0]

    def body(me_ref, p_ref, own_ref, w_ref, m_ref, v_ref, g_ref, d_ref, nm_ref, nv_ref):
        own = own_ref[0].astype(F32)
        g = None
        for s in range(n_slots):
            term = jnp.where(me_ref[0] == s, own, p_ref[s].astype(F32))
            g = term if g is None else g + term
        d, nm, nv = _adam_math(g, w_ref[...], m_ref[...], v_ref[...])
        g_ref[...] = g
        d_ref[...] = d
        nm_ref[...] = nm
        nv_ref[...] = nv

    row = pl.BlockSpec((tr, c), lambda i, me_ref: (i, 0))
    o = jax.ShapeDtypeStruct((r, c), F32)
    return pl.pallas_call(
        body, name=name,
        grid_spec=pltpu.PrefetchScalarGridSpec(
            num_scalar_prefetch=1, grid=(r // tr,),
            in_specs=[pl.BlockSpec((n_slots, tr, c), lambda i, me_ref: (0, i, 0)),
                      pl.BlockSpec((1, tr, c), lambda i, me_ref: (me_ref[0], i, 0)), row, row, row],
            out_specs=[row] * 4),
        out_shape=[o] * 4,
        compiler_params=_cparams(("parallel",)),
    )(me, parts, mine, w, m, v)


def _adam_small(g, w, m, v):
    def body(g_ref, w_ref, m_ref, v_ref, d_ref, nm_ref, nv_ref):
        d, nm, nv = _adam_math(g_ref[...], w_ref[...], m_ref[...], v_ref[...])
        d_ref[...] = d
        nm_ref[...] = nm
        nv_ref[...] = nv

    o = jax.ShapeDtypeStruct(g.shape, F32)
    return pl.pallas_call(body, name="adam_small", out_shape=[o] * 3, compiler_params=_cparams())(g, w, m, v)


def _split_w_in(w_full):
    o = IN_OFF
    w_gdn = w_full[:, o["gq"]:o["ga"]]
    w_fox = w_full[:, o["fq"]:o["ff"]]
    w_gate = w_full[:, o["gate_a"]:o["end"]]
    w_small = jnp.concatenate([w_full[:, o["ga"]:o["fq"]], w_full[:, o["ff"]:o["gate_a"]],
                               jnp.zeros((w_full.shape[0], LANES - 24), w_full.dtype)], axis=1)
    return w_gdn, w_fox, w_gate, w_small


def _join_w_in(parts):
    small = parts[0]
    return jnp.concatenate(parts[1:5] + [small[:, 0:16]] + parts[5:8] + [small[:, 16:24]] + parts[8:10], axis=1)


def _rows128(a, rows):
    flat = a.reshape(-1)
    flat = jnp.concatenate([flat, jnp.zeros((rows * LANES - flat.shape[0],), flat.dtype)])
    return flat.reshape(rows, LANES)


def kernel(x, norm_mix_g, w_in, gdn_conv_w, gdn_a_log, gdn_dt_bias, gdn_norm_g, fox_q_norm_g, fox_k_norm_g, fox_f_bias, w_proj_gdn, w_proj_fox, w_out, norm_mlp_g, w_up, w_down, loss_target, m_norm_mix_g, m_w_in, m_gdn_conv_w, m_gdn_a_log, m_gdn_dt_bias, m_gdn_norm_g, m_fox_q_norm_g, m_fox_k_norm_g, m_fox_f_bias, m_w_proj_gdn, m_w_proj_fox, m_w_out, m_norm_mlp_g, m_w_up, m_w_down, v_norm_mix_g, v_w_in, v_gdn_conv_w, v_gdn_a_log, v_gdn_dt_bias, v_gdn_norm_g, v_fox_q_norm_g, v_fox_k_norm_g, v_fox_f_bias, v_w_proj_gdn, v_w_proj_fox, v_w_out, v_norm_mlp_g, v_w_up, v_w_down):
    b_loc, t, d = x.shape
    n = b_loc * t
    me = 4 * lax.axis_index("x") + 2 * lax.axis_index("y") + lax.axis_index("c")

    late_names = ["w_proj_gdn", "w_proj_fox", "w_out", "w_up", "w_down"]
    big16 = _to_bf16([w_in[0], w_proj_gdn[0], w_proj_fox[0], w_out[0], w_up[0], w_down[0]], "weights_to_bf16")
    g_in, g_conv = _all_gather([big16[0], gdn_conv_w[0]], "gather_w_in")
    late = _push_start(list(big16[1:]), False, "gather_late_start")
    w_full = g_in.transpose(1, 0, 2).reshape(d, N_DEV * w_in.shape[2])
    w_gdn, w_fox, w_gate, w_small = _split_w_in(w_full)
    weights = {
        "w_gdn": w_gdn, "w_fox": w_fox, "w_gate": w_gate, "w_small": w_small,
        "conv_w": g_conv.transpose(1, 0, 2).reshape(CONV_K, 3 * d),
        "norm_mix_g": norm_mix_g + late["token"][0:1, 0:1], "norm_mlp_g": norm_mlp_g, "a_log": gdn_a_log,
        "dt_bias": gdn_dt_bias, "gdn_norm_g": gdn_norm_g, "fox_q_norm_g": fox_q_norm_g, "fox_k_norm_g": fox_k_norm_g,
        "f_bias": fox_f_bias,
    }
    c_in, c_up = w_in.shape[2], w_up.shape[2]
    me1 = jnp.reshape(me, (1,)).astype(jnp.int32)
    chip1 = jnp.reshape(2 * lax.axis_index("x") + lax.axis_index("y"), (1,)).astype(jnp.int32)
    core1 = jnp.reshape(lax.axis_index("c"), (1,)).astype(jnp.int32)

    class _Exchange:
        def __init__(self):
            self.started = []

        def late_weights(self, after):
            shards, lands = _push_wait(late, after, "gather_late_wait")
            full = [lax.dynamic_update_index_in_dim(land, shard, me, 0) for land, shard in zip(lands, shards)]
            g_pa, g_pb, g_out, g_up, g_down = full
            return {"w_proj_gdn": g_pa.reshape(d, d), "w_proj_fox": g_pb.reshape(d, d), "w_out": g_out.reshape(d, d),
                    "w_up": g_up.transpose(1, 0, 2).reshape(d, D_FF), "w_down": g_down.reshape(D_FF, d)}

        def grads_ready(self, grads, tie):
            names = list(grads)
            if names == ["w_in_parts"]:
                halves = _join_w_in(grads["w_in_parts"]).reshape(d, N_DEV, c_in).transpose(1, 0, 2)
                other = _sibling_swap(halves, "grads_w_in_sibling")
                pair = _add_halves(core1, halves, other, "grads_w_in_pair")
                st = _push_start([pair], True, "grads_start_w_in_parts", chips=True)
            else:
                layout = {"w_up": lambda a: a.reshape(d, N_DEV, c_up).transpose(1, 0, 2),
                          "w_down": lambda a: a.reshape(N_DEV, D_FF // N_DEV, d)}
                arrs = [layout.get(k, lambda a: a.reshape(N_DEV, d // N_DEV, d))(grads[k]) for k in names]
                st = _push_start(arrs, True, "grads_start_" + names[0])
            self.started.append((names, st))
            return tie + st["token"][0:1, 0:1]

    comm = _Exchange()
    loss_blk, grad_x, g = _local_step(x.reshape(n, d), loss_target.reshape(n, d), weights, b_loc, t, comm)

    shards = {"w_in_parts": (w_in, m_w_in, v_w_in), "w_proj_gdn": (w_proj_gdn, m_w_proj_gdn, v_w_proj_gdn),
              "w_proj_fox": (w_proj_fox, m_w_proj_fox, v_w_proj_fox), "w_out": (w_out, m_w_out, v_w_out),
              "w_up": (w_up, m_w_up, v_w_up), "w_down": (w_down, m_w_down, v_w_down)}
    adam = {}

    def finish(names, st, after):
        mine, parts = _push_wait(st, after, "grads_wait_" + names[0])
        slot = chip1 if st["copies"] == len(CHIP_RELS) else me1
        for k, own, part in zip(names, mine, parts):
            wi, mi, vi = shards[k]
            adam[k] = [r[None] for r in _adam_shard(slot, part, own, wi[0], mi[0], vi[0], "adam_" + k)]

    for names, st in comm.started[:-1]:
        finish(names, st, grad_x)

    conv_rows = CONV_K * 3 * d // LANES
    conv_g = g["conv"].transpose(1, 0, 2).reshape(conv_rows, LANES)
    buf = jnp.concatenate([conv_g, g["norm_mix_g"].reshape(8, LANES), g["norm_mlp_g"].reshape(8, LANES),
                           g["gdn_small"], g["fox_small"], loss_blk], axis=0)
    anchor = sum(adam[k][1][0, 0:1, 0:LANES] for names, _ in comm.started[:-1] for k in names) * 0.0
    tot = _all_reduce_small(buf + anchor, "all_reduce_small")
    finish(*comm.started[-1], tot)
    big_out = [adam[k] for k in ["w_in_parts"] + late_names]
    o = conv_rows
    conv_full = tot[0:o].reshape(CONV_K, 3 * d)
    c_conv = gdn_conv_w.shape[2]
    g_conv_shard = lax.dynamic_slice(conv_full, (0, me * c_conv), (CONV_K, c_conv))
    g_mix = tot[o:o + 8].reshape(1, d)
    g_mlp = tot[o + 8:o + 16].reshape(1, d)
    gs, fs = tot[o + 16:o + 24], tot[o + 24:o + 32]
    loss = tot[o + 32, 0]
    small_g = [g_mix, g_conv_shard[None], gs[0:1, 0:HEADS], gs[1:2, 0:HEADS], gs[2:3], fs[0:1], fs[1:2], fs[2:3, 0:HEADS],
               g_mlp]
    small_w = [norm_mix_g, gdn_conv_w, gdn_a_log, gdn_dt_bias, gdn_norm_g, fox_q_norm_g, fox_k_norm_g, fox_f_bias,
               norm_mlp_g]
    small_m = [m_norm_mix_g, m_gdn_conv_w, m_gdn_a_log, m_gdn_dt_bias, m_gdn_norm_g, m_fox_q_norm_g, m_fox_k_norm_g,
               m_fox_f_bias, m_norm_mlp_g]
    small_v = [v_norm_mix_g, v_gdn_conv_w, v_gdn_a_log, v_gdn_dt_bias, v_gdn_norm_g, v_fox_q_norm_g, v_fox_k_norm_g,
               v_fox_f_bias, v_norm_mlp_g]
    row_counts = [-(-a.size // (8 * LANES)) * 8 for a in small_w]

    def pack(arrs):
        return jnp.concatenate([_rows128(a, rc) for a, rc in zip(arrs, row_counts)], axis=0)

    sd, sm, sv = _adam_small(pack(small_g), pack(small_w), pack(small_m), pack(small_v))

    def unpack(p):
        outs, r0 = [], 0
        for a, rc in zip(small_w, row_counts):
            outs.append(p[r0:r0 + rc].reshape(-1)[:a.size].reshape(a.shape))
            r0 += rc
        return outs

    small_out = [small_g_i.reshape(w_i.shape) for small_g_i, w_i in zip(small_g, small_w)], unpack(sd), unpack(sm), unpack(sv)

    def ordered(kind):
        s = small_out[kind]
        bo = [b[kind] for b in big_out]
        return [s[0], bo[0], s[1], s[2], s[3], s[4], s[5], s[6], s[7], bo[1], bo[2], bo[3], s[8], bo[4], bo[5]]

    return (loss, grad_x.reshape(b_loc, t, d), *ordered(0), *ordered(1), *ordered(2), *ordered(3))
```

```python
import functools

import jax
import jax.numpy as jnp
from jax import lax
from jax.experimental import pallas as pl
from jax.experimental.pallas import tpu as pltpu

F32 = jnp.float32
BF16 = jnp.bfloat16
MESH = pl.DeviceIdType.MESH

N_DEV = 8
D_MODEL = 1024
HEADS = 8
DH = 128
CONV_K = 4
CHUNK = 128
GDN_GROUP = 8
FOX_BLOCK = 128
FOX_TILE = 512
FOX_SHORT = 512
D_FF = 4 * D_MODEL
EPS = 1e-6
LANES = 128
NEG = -1e30
IN_OFF = {"gq": 0, "gk": 1024, "gv": 2048, "gz": 3072, "ga": 4096, "gb": 4104, "fq": 4112, "fk": 5136,
          "fv": 6160, "ff": 7184, "gate_a": 7192, "gate_b": 8216, "end": 9240}
LANE_GA, LANE_GB, LANE_FF = 0, 8, 16

ADAM_LR = 0.001
ADAM_B1 = 0.9
ADAM_B2 = 0.999
ADAM_EPS = 1e-08
ADAM_WD = 0.01
ADAM_STEP = 10

VMEM_LIMIT = 56 * 1024 * 1024


def _cparams(sem=None):
    return pltpu.CompilerParams(dimension_semantics=sem, vmem_limit_bytes=VMEM_LIMIT)


def _sigmoid(x):
    return 1.0 / (1.0 + jnp.exp(-x))


def _softplus(x):
    return jnp.maximum(x, 0.0) + jnp.log(1.0 + jnp.exp(-jnp.abs(x)))


def _dot(a, b, prec=None):
    return lax.dot_general(a, b, (((1,), (0,)), ((), ())), precision=prec, preferred_element_type=F32)


def _dot_nt(a, b, prec=None):
    return lax.dot_general(a, b, (((1,), (1,)), ((), ())), precision=prec, preferred_element_type=F32)


def _dot_tn(a, b, prec=None):
    return lax.dot_general(a, b, (((0,), (0,)), ((), ())), precision=prec, preferred_element_type=F32)


def _bf(x):
    return x.astype(BF16)


MM_TILE = 1024


def _mm(a, b, *, name, ta=False, tb=False, out_dtype=F32, epi=None, extras=(), out2=None,
        b_koff=0, tm=MM_TILE, tn=MM_TILE, tk=MM_TILE):
    m = a.shape[1] if ta else a.shape[0]
    kdim = a.shape[0] if ta else a.shape[1]
    n = b.shape[0] if tb else b.shape[1]
    tm, tn, tk = min(tm, m), min(tn, n), min(tk, kdim)
    nk = kdim // tk
    grid = (m // tm, n // tn, nk)
    koff = b_koff // tk
    a_spec = pl.BlockSpec((tk, tm), lambda i, j, k: (k, i)) if ta else pl.BlockSpec((tm, tk), lambda i, j, k: (i, k))
    if tb:
        b_spec = pl.BlockSpec((tn, tk), lambda i, j, k: (j, k + koff))
    else:
        b_spec = pl.BlockSpec((tk, tn), lambda i, j, k: (k + koff, j))
    o_spec = pl.BlockSpec((tm, tn), lambda i, j, k: (i, j))
    n_e = len(extras)
    n_o = 1 if out2 is None else 2
    dims = (((0 if ta else 1,), (1 if tb else 0,)), ((), ()))

    def body(a_ref, b_ref, *rest):
        e_refs, o_refs = rest[:n_e], rest[n_e:n_e + n_o]
        prod = lax.dot_general(_bf(a_ref[...]), _bf(b_ref[...]), dims, preferred_element_type=F32)

        def finish(r):
            if out2 is not None:
                o_refs[1][...] = out2[0](r).astype(out2[1])
            if epi is not None:
                r = epi(r, *[e[...] for e in e_refs])
            o_refs[0][...] = r.astype(out_dtype)

        if nk == 1:
            finish(prod)
        else:
            acc = rest[n_e + n_o]
            k = pl.program_id(2)

            @pl.when(k == 0)
            def _():
                acc[...] = prod

            @pl.when(k > 0)
            def _():
                acc[...] += prod

            @pl.when(k == nk - 1)
            def _():
                finish(acc[...])

    shapes = [jax.ShapeDtypeStruct((m, n), out_dtype)]
    if out2 is not None:
        shapes.append(jax.ShapeDtypeStruct((m, n), out2[1]))
    res = pl.pallas_call(
        body, name=name, grid=grid,
        in_specs=[a_spec, b_spec] + [o_spec] * n_e,
        out_specs=[o_spec] * n_o, out_shape=shapes,
        scratch_shapes=[] if nk == 1 else [pltpu.VMEM((tm, tn), F32)],
        compiler_params=_cparams(("parallel", "parallel", "arbitrary")),
    )(a, b, *extras)
    return res[0] if out2 is None else res


def _du_all(dps, w_small, segs, w, tm=512):
    n, d = dps.shape[0], w_small.shape[0]
    names = []
    for _, wname, _ in segs:
        if wname not in names:
            names.append(wname)
    first = {nm: min(i for i, s in enumerate(segs) if s[1] == nm) for nm in names}
    count = {nm: sum(1 for s in segs if s[1] == nm) for nm in names}
    n_seg, n_i = len(segs), n // tm

    def w_spec(nm):
        return pl.BlockSpec((d, d), lambda k, i: (0, jnp.clip(k - first[nm], 0, count[nm] - 1)))

    def rows_spec(cols, j):
        return pl.BlockSpec((tm, cols), lambda k, i: (jnp.where(k == j, i, jnp.where(k < j, 0, n_i - 1)), 0))

    def body(dps_ref, ws_ref, *rest):
        seg_refs, w_refs, o_ref, acc = rest[:n_seg], rest[n_seg:n_seg + len(names)], rest[-2], rest[-1]
        k, i = pl.program_id(0), pl.program_id(1)
        rows = pl.ds(pl.multiple_of(i * tm, tm), tm)

        @pl.when(k == 0)
        def _():
            acc[rows, :] = _dot_nt(_bf(dps_ref[...]), ws_ref[...])

        for idx, (_, wname, _) in enumerate(segs):
            @pl.when(k == idx)
            def _(idx=idx, wname=wname):
                acc[rows, :] += _dot_nt(seg_refs[idx][...], w_refs[names.index(wname)][...])

        @pl.when(k == n_seg - 1)
        def _():
            o_ref[...] = acc[rows, :]

    return pl.pallas_call(
        body, name="du_all", grid=(n_seg, n_i),
        in_specs=[rows_spec(dps.shape[1], 0), pl.BlockSpec(w_small.shape, lambda k, i: (0, 0))]
                 + [rows_spec(d, j) for j in range(n_seg)] + [w_spec(nm) for nm in names],
        out_specs=pl.BlockSpec((tm, d), lambda k, i: (jnp.where(k == n_seg - 1, i, 0), 0)),
        out_shape=jax.ShapeDtypeStruct((n, d), F32),
        scratch_shapes=[pltpu.VMEM((n, d), F32)],
        compiler_params=_cparams(("arbitrary", "arbitrary")),
    )(dps, w_small, *[s[0] for s in segs], *[w[nm] for nm in names])


def _relu2(x):
    r = jnp.maximum(x, 0.0)
    return r * r


ROWS = 512


def _rms_fwd(x, g, name):
    n, d = x.shape

    def body(x_ref, g_ref, u_ref):
        xv = x_ref[...]
        r = lax.rsqrt(jnp.mean(xv * xv, axis=1, keepdims=True) + EPS)
        u_ref[...] = _bf(xv * r * g_ref[...])

    return pl.pallas_call(
        body, name=name, grid=(n // ROWS,),
        in_specs=[pl.BlockSpec((ROWS, d), lambda i: (i, 0)), pl.BlockSpec((1, d), lambda i: (0, 0))],
        out_specs=pl.BlockSpec((ROWS, d), lambda i: (i, 0)),
        out_shape=jax.ShapeDtypeStruct((n, d), BF16),
        compiler_params=_cparams(("parallel",)),
    )(x, g)


def _rms_bwd(dy, x, g, dres, name):
    n, d = x.shape

    def body(dy_ref, x_ref, g_ref, dres_ref, dx_ref, dx16_ref, dg_ref):
        i = pl.program_id(0)
        xv, dyv = x_ref[...], dy_ref[...]
        r = lax.rsqrt(jnp.mean(xv * xv, axis=1, keepdims=True) + EPS)
        gy = dyv * g_ref[...]
        s = jnp.sum(gy * xv, axis=1, keepdims=True)
        dx = dres_ref[...] + r * gy - xv * (r * r * r * (1.0 / d)) * s
        dx_ref[...] = dx
        dx16_ref[...] = _bf(dx)

        @pl.when(i == 0)
        def _():
            dg_ref[...] = jnp.zeros_like(dg_ref)

        dg_ref[...] += jnp.sum(dyv * xv * r, axis=0, keepdims=True)

    row = pl.BlockSpec((ROWS, d), lambda i: (i, 0))
    vec = pl.BlockSpec((1, d), lambda i: (0, 0))
    return pl.pallas_call(
        body, name=name, grid=(n // ROWS,),
        in_specs=[row, row, vec, row], out_specs=[row, row, vec],
        out_shape=[jax.ShapeDtypeStruct((n, d), F32), jax.ShapeDtypeStruct((n, d), BF16),
                   jax.ShapeDtypeStruct((1, d), F32)],
        compiler_params=_cparams(("arbitrary",)),
    )(dy, x, g, dres)


def _merge_fwd(ya, yb, gate):
    n, d = ya.shape

    def body(ya_ref, yb_ref, ga_ref, gb_ref, o_ref):
        o_ref[...] = _bf(_sigmoid(ga_ref[...]) * ya_ref[...] + _sigmoid(gb_ref[...]) * yb_ref[...])

    row = pl.BlockSpec((ROWS, d), lambda i: (i, 0))
    return pl.pallas_call(
        body, name="merge_fwd", grid=(n // ROWS,),
        in_specs=[row, row, row, pl.BlockSpec((ROWS, d), lambda i: (i, 1))], out_specs=row,
        out_shape=jax.ShapeDtypeStruct((n, d), BF16),
        compiler_params=_cparams(("parallel",)),
    )(ya, yb, gate, gate)


def _merge_bwd(dm, ya, yb, gate):
    n, d = ya.shape

    def body(dm_ref, ya_ref, yb_ref, ga_ref, gb_ref, dya_ref, dyb_ref, dga_ref, dgb_ref):
        dmv = dm_ref[...]
        sa, sb = _sigmoid(ga_ref[...]), _sigmoid(gb_ref[...])
        dya_ref[...] = _bf(dmv * sa)
        dyb_ref[...] = _bf(dmv * sb)
        dga_ref[...] = _bf(dmv * ya_ref[...] * sa * (1.0 - sa))
        dgb_ref[...] = _bf(dmv * yb_ref[...] * sb * (1.0 - sb))

    row = pl.BlockSpec((ROWS, d), lambda i: (i, 0))
    o = jax.ShapeDtypeStruct((n, d), BF16)
    return pl.pallas_call(
        body, name="merge_bwd", grid=(n // ROWS,),
        in_specs=[row, row, row, row, pl.BlockSpec((ROWS, d), lambda i: (i, 1))], out_specs=[row] * 4,
        out_shape=[o] * 4,
        compiler_params=_cparams(("parallel",)),
    )(dm, ya, yb, gate, gate)


def _loss_bwd(out, target):
    n, d = out.shape

    def body(o_ref, t_ref, d_ref, d16_ref, l_ref):
        i = pl.program_id(0)
        err = o_ref[...] - t_ref[...]
        d_ref[...] = err * (1.0 / d)
        d16_ref[...] = _bf(err * (1.0 / d))

        @pl.when(i == 0)
        def _():
            l_ref[...] = jnp.zeros_like(l_ref)

        l_ref[...] += 0.5 * jnp.sum(jnp.mean(err * err, axis=1, keepdims=True), axis=0, keepdims=True)

    row = pl.BlockSpec((ROWS, d), lambda i: (i, 0))
    return pl.pallas_call(
        body, name="loss_bwd", grid=(n // ROWS,),
        in_specs=[row, row], out_specs=[row, row, pl.BlockSpec((8, LANES), lambda i: (0, 0))],
        out_shape=[jax.ShapeDtypeStruct((n, d), F32), jax.ShapeDtypeStruct((n, d), BF16),
                   jax.ShapeDtypeStruct((8, LANES), F32)],
        compiler_params=_cparams(("arbitrary",)),
    )(out, target)


PAD = 8


def _pad_zero(pad_ref):
    t = pad_ref.shape[0] - 2 * PAD
    pad_ref[0:PAD, :] = jnp.zeros((PAD, LANES), F32)
    pad_ref[PAD + t:2 * PAD + t, :] = jnp.zeros((PAD, LANES), F32)


def _shifted(pad_ref, s):
    t = pad_ref.shape[0] - 2 * PAD
    return pad_ref[PAD - s:PAD - s + t, :]


def _conv(x, w_ref, pad_ref):
    t = x.shape[0]
    pad_ref[PAD:PAD + t, :] = x
    y = _shifted(pad_ref, 3) * w_ref[0:1, :]
    y = y + _shifted(pad_ref, 2) * w_ref[1:2, :]
    y = y + _shifted(pad_ref, 1) * w_ref[2:3, :]
    return y + x * w_ref[3:4, :]


def _chunk_consts():
    r = lax.broadcasted_iota(jnp.int32, (CHUNK, CHUNK), 0)
    c = lax.broadcasted_iota(jnp.int32, (CHUNK, CHUNK), 1)
    incl, strict = r >= c, r > c
    return dict(incl=incl, strict=strict, trilf=incl.astype(F32), triuf=(r <= c).astype(F32),
                eye=(r == c).astype(F32))


class _V:
    def __init__(self, xs):
        self.xs = list(xs)

    def __add__(self, o):
        return _ap(lambda x, y: x + y, self, o)

    def __radd__(self, o):
        return _ap(lambda x, y: y + x, self, o)

    def __sub__(self, o):
        return _ap(lambda x, y: x - y, self, o)

    def __rsub__(self, o):
        return _ap(lambda x, y: y - x, self, o)

    def __mul__(self, o):
        return _ap(lambda x, y: x * y, self, o)

    def __rmul__(self, o):
        return _ap(lambda x, y: y * x, self, o)

    def __neg__(self):
        return _ap(lambda x: -x, self)

    def __getitem__(self, idx):
        return _ap(lambda x: x[idx], self)


def _ap(fn, *args):
    n = [len(a.xs) for a in args if isinstance(a, _V)]
    if not n:
        return fn(*args)
    return _V([fn(*[a.xs[i] if isinstance(a, _V) else a for a in args]) for i in range(n[0])])


def _vbf(x):
    return _ap(_bf, x)


def _vdot(a, b):
    return _ap(_dot, a, b)


def _vdot_nt(a, b):
    return _ap(_dot_nt, a, b)


def _vdot_tn(a, b):
    return _ap(_dot_tn, a, b)


def _vexp(x):
    return _ap(jnp.exp, x)


def _vsum(x, axis):
    return _ap(lambda v: jnp.sum(v, axis=axis, keepdims=True), x)


def _vcat(a, b, axis):
    return _ap(lambda x, y: jnp.concatenate([x, y], axis=axis), a, b)


def _vmask(mask, x):
    return _ap(lambda v: jnp.where(mask, v, 0.0), x)


def _split2(x):
    h = _vbf(x)
    return h, _vbf(x - _ap(lambda v: v.astype(F32), h))


def _dot3(a, b, kind=_vdot):
    ah, al = _split2(a)
    bh, bl = _split2(b)
    return kind(ah, bh) + (kind(ah, bl) + kind(al, bh))


def _split(x, terms):
    out = []
    for _ in range(terms):
        h = _vbf(x)
        out.append(h)
        x = x - _ap(lambda v: v.astype(F32), h)
    return out


def _dot_exact_l(m01, x, kind=_vdot, terms=2):
    mb = _bf(m01)
    parts = [kind(mb, xp) for xp in _split(x, terms)]
    return functools.reduce(lambda a, b: a + b, reversed(parts))


def _dot_exact_r(x, m01, kind=_vdot, terms=2):
    mb = _bf(m01)
    parts = [kind(xp, mb) for xp in _split(x, terms)]
    return functools.reduce(lambda a, b: a + b, reversed(parts))


def _inv_series(a, eye):
    m = eye.shape[0]
    levels = m.bit_length() - 1
    p = -a
    r = p + eye
    p = _dot3(p, p)
    for j in range(1, levels):
        if j < levels - 1:
            y = _dot3(p, _vcat(p, r, 1))
            p, r = y[:, 0:m], r + y[:, m:2 * m]
        else:
            r = r + _dot3(p, r)
    return r


def _inv_unit_lower(a, eye):
    return _inv_series(a, eye)


def _gdn_chunk_pre(q, k, v, g128, b128, cs):
    incl = cs["incl"]
    b64 = b128
    big_g = _dot_exact_l(cs["trilf"], g128)
    gc = big_g[:, 0:CHUNK]
    gr = _dot_exact_r(g128, cs["triuf"], _vdot_tn)
    decay = _ap(lambda d: jnp.where(incl, jnp.exp(jnp.where(incl, d, 0.0)), 0.0), gc - gr)
    kb, qb = _vbf(k), _vbf(q)
    qkk = _vdot_nt(_vcat(qb, kb, 0), kb)
    qk, kk = qkk[0:CHUNK], qkk[CHUNK:2 * CHUNK]
    tm = _inv_unit_lower(_vmask(cs["strict"], b64 * kk * decay), cs["eye"])
    e_g = _vexp(big_g)
    wu = _dot3(tm, _vcat(v * b128, k * (b128 * e_g), 1))
    w, u = wu[:, 0:DH], wu[:, DH:2 * DH]
    g_last = _vsum(g128, 0)
    return dict(big_g=big_g, decay=decay, kk=kk, qk=qk, tm=tm, w=w, u=u, p=qk * decay, q_dec=q * e_g,
                k_dec=k * _vexp(g_last - big_g), dec=_vexp(g_last))


def _gdn_chunk_post(q, k, v, g128, b128, s, ds_next, do, dv_new, big_g, decay, kk, qk, tm, u, v_new, cs):
    b64 = b128
    e_g = _vexp(big_g)
    vb = v * b128
    kbeta = k * (b128 * e_g)
    q_dec = q * e_g
    g_last = _vsum(g128, 0)
    ekg = _vexp(g_last - big_g)
    k_dec = k * ekg
    dec = _vexp(g_last)
    kb, qb, sb = _vbf(k), _vbf(q), _vbf(s)
    dob, dsb, vnb, dvnb = _vbf(do), _vbf(ds_next), _vbf(v_new), _vbf(dv_new)
    dp = _vmask(cs["incl"], _vdot_nt(dob, vnb))
    dq_dec = _vdot_nt(dob, sb)
    du = -_vdot_nt(dvnb, sb)
    ddec = _vsum(_vsum(s * ds_next, 1), 0)
    dk_dec = _vdot_nt(vnb, dsb)
    dwu = _vcat(dv_new, du, 1)
    dt = _dot3(dwu, _vcat(vb, kbeta, 1), _vdot_nt)
    dvk = _dot3(tm, dwu, _vdot_tn)
    dvb, dkbeta = dvk[:, 0:DH], dvk[:, DH:2 * DH]
    da = _vmask(cs["strict"], -_dot3(tm, _dot3(dt, tm, _vdot_nt), _vdot_tn))
    dkk = _vbf(da * b64 * decay)
    dqk = _vbf(dp * decay)
    ddd = (da * b64 * kk + dp * qk) * decay
    dq = _vdot(dqk, kb) + dq_dec * e_g
    dk = _vdot_tn(dqk, qb) + _vdot(dkk, kb) + _vdot_tn(dkk, kb) + dk_dec * ekg + dkbeta * (b128 * e_g)
    dv = dvb * b128
    dbeta = _vsum(da * kk * decay, 1) + _vsum(dvb * v, 1) + _vsum(dkbeta * k * e_g, 1)
    s_k = _vsum(dk_dec * k_dec, 1)
    dg_col = _vsum(ddd, 1) + _vsum(dq_dec * q_dec, 1) - s_k + _vsum(dkbeta * kbeta, 1)
    colsum = _dot_exact_r(ddd, jnp.ones((CHUNK, LANES), F32), _vdot_tn)
    dg_last = _vsum(s_k, 0) + ddec * dec
    dg = _dot_exact_l(cs["triuf"], dg_col - colsum) + dg_last
    return dq, dk, dv, dg, dbeta


def _stack_rows(vecs, nrows):
    row = lax.broadcasted_iota(jnp.int32, (nrows, LANES), 0)
    out = jnp.zeros((nrows, LANES), F32)
    for i, v in enumerate(vecs):
        out = out + jnp.where(row == i, jnp.broadcast_to(v, (nrows, LANES)), 0.0)
    return out


def _head_lane(x, lane_idx):
    lane = lax.broadcasted_iota(jnp.int32, x.shape, 1)
    return jnp.sum(jnp.where(lane == lane_idx, x, 0.0), axis=1, keepdims=True)


def _gdn_gates(ps, h, alog_ref, dtb_ref):
    ga = _head_lane(ps, LANE_GA + h)
    gb = _head_lane(ps, LANE_GB + h)
    a = jnp.exp(jnp.full((1, 1), alog_ref[0, h], F32))
    sp_in = ga + dtb_ref[0, h]
    g = -a * _softplus(sp_in)
    return g, _sigmoid(gb), a, sp_in


def _gdn_specs(b_loc, t):
    def col(off):
        return pl.BlockSpec((t, DH), lambda b, h: (b, off + h))

    ps_spec = pl.BlockSpec((t, LANES), lambda b, h: (b, 0))

    def wcol(off):
        return pl.BlockSpec((CONV_K, DH), lambda b, h: (0, off + h))

    smem = pl.BlockSpec(memory_space=pltpu.SMEM)
    vec = pl.BlockSpec((1, DH), lambda b, h: (0, 0))
    return col, ps_spec, wcol, smem, vec


def _gdn_fwd(pg, ps, convw, a_log, dt_bias, gnorm, b_loc, t):
    n = b_loc * t
    assert t % (CHUNK * GDN_GROUP) == 0 and CHUNK == LANES, (t, CHUNK, GDN_GROUP)
    nc = t // CHUNK
    col, ps_spec, wcol, smem, vec = _gdn_specs(b_loc, t)

    def body(q_ref, k_ref, v_ref, z_ref, ps_ref, wq_ref, wk_ref, wv_ref, alog_ref, dtb_ref, gn_ref,
             oa_ref, oraw_ref, s_ref, qn, kn, vv, g128, b128, uq_s, p_s, kd_s, dec_s, pad_s):
        h = pl.program_id(1)
        g, beta, _, _ = _gdn_gates(ps_ref[...], h, alog_ref, dtb_ref)
        g128[...] = jnp.broadcast_to(g, (t, LANES))
        b128[...] = jnp.broadcast_to(beta, (t, LANES))
        _pad_zero(pad_s)
        pq = _conv(q_ref[...], wq_ref, pad_s)
        yq = pq * _sigmoid(pq)
        qn[...] = yq * (lax.rsqrt(jnp.sum(yq * yq, axis=1, keepdims=True) + EPS) * (DH ** -0.5))
        pk = _conv(k_ref[...], wk_ref, pad_s)
        yk = pk * _sigmoid(pk)
        kn[...] = yk * lax.rsqrt(jnp.sum(yk * yk, axis=1, keepdims=True) + EPS)
        pv = _conv(v_ref[...], wv_ref, pad_s)
        vv[...] = pv * _sigmoid(pv)
        cs = _chunk_consts()

        def pre_group(gi, _):
            idx = [gi * GDN_GROUP + c for c in range(GDN_GROUP)]
            rows = [pl.ds(pl.multiple_of(i * CHUNK, CHUNK), CHUNK) for i in idx]
            ins = [_V([ref[r, :] for r in rows]) for ref in (qn, kn, vv, g128, b128)]
            f = _gdn_chunk_pre(*ins, cs)
            for c, (i, r) in enumerate(zip(idx, rows)):
                vv[r, :] = f["w"].xs[c]
                uq_s[i, 0:CHUNK, :] = _bf(f["u"].xs[c])
                uq_s[i, CHUNK:2 * CHUNK, :] = _bf(f["q_dec"].xs[c])
                p_s[r, :] = _bf(f["p"].xs[c])
                kd_s[r, :] = _bf(f["k_dec"].xs[c])
                dec_s[pl.ds(pl.multiple_of(i * 8, 8), 8), :] = jnp.broadcast_to(f["dec"].xs[c], (8, LANES))
            return 0

        lax.fori_loop(0, nc // GDN_GROUP, pre_group, 0)

        def chunk(i, s):
            r = pl.ds(pl.multiple_of(i * CHUNK, CHUNK), CHUNK)
            us = _dot(uq_s[i], _bf(s))
            vnb = _bf(vv[r, :] - us[0:CHUNK])
            oraw_ref[r, :] = us[CHUNK:2 * CHUNK] + _dot(p_s[r, :], vnb)
            s_ref[0, 0, i] = s
            return s * dec_s[pl.ds(pl.multiple_of(i * 8, 8), 1), :] + _dot_tn(kd_s[r, :], vnb)

        lax.fori_loop(0, nc, chunk, jnp.zeros((DH, DH), F32))
        o = oraw_ref[...]
        rr = lax.rsqrt(jnp.mean(o * o, axis=1, keepdims=True) + EPS)
        z = z_ref[...]
        oa_ref[...] = _bf((o * rr * gn_ref[...]) * (z * _sigmoid(z)))

    return pl.pallas_call(
        body, name="gdn_fwd", grid=(b_loc, HEADS),
        in_specs=[col(0), col(HEADS), col(2 * HEADS), col(3 * HEADS), ps_spec, wcol(0), wcol(HEADS), wcol(2 * HEADS),
                  smem, smem, vec],
        out_specs=[pl.BlockSpec((t, DH), lambda b, h: (b, h)), pl.BlockSpec((t, DH), lambda b, h: (b, h)),
                   pl.BlockSpec((1, 1, nc, DH, DH), lambda b, h: (b, h, 0, 0, 0))],
        out_shape=[jax.ShapeDtypeStruct((n, HEADS * DH), BF16), jax.ShapeDtypeStruct((n, HEADS * DH), F32),
                   jax.ShapeDtypeStruct((b_loc, HEADS, nc, DH, DH), F32)],
        scratch_shapes=([pltpu.VMEM((t, DH), F32)] * 3 + [pltpu.VMEM((t, LANES), F32)] * 2
                        + [pltpu.VMEM((nc, 2 * CHUNK, DH), BF16), pltpu.VMEM((t, CHUNK), BF16), pltpu.VMEM((t, DH), BF16),
                           pltpu.VMEM((8 * nc, LANES), F32), pltpu.VMEM((t + 2 * PAD, LANES), F32)]),
        compiler_params=_cparams(("arbitrary", "arbitrary")),
    )(pg, pg, pg, pg, ps, convw, convw, convw, a_log, dt_bias, gnorm)


def _gdn_bwd(pg, ps, convw, a_log, dt_bias, gnorm, d_oa, o_raw, s_all, b_loc, t):
    n = b_loc * t
    assert t % (CHUNK * GDN_GROUP) == 0 and CHUNK == LANES, (t, CHUNK, GDN_GROUP)
    nc = t // CHUNK
    col, ps_spec, wcol, smem, vec = _gdn_specs(b_loc, t)

    def body(q_ref, k_ref, v_ref, z_ref, ps_ref, wq_ref, wk_ref, wv_ref, alog_ref, dtb_ref, gn_ref,
             doa_ref, oraw_ref, s_ref,
             dq_ref, dk_ref, dv_ref, dz_ref, dps_ref, dcw_ref, dsm_ref,
             qn, kn, vv, g128, b128, do_s, bg_s, u_s, vn_s, dvn_s, dcy_s, kk_s, qk_s, tm_s, dsn_s, pad_s):
        b, h = pl.program_id(0), pl.program_id(1)
        g, beta, _, _ = _gdn_gates(ps_ref[...], h, alog_ref, dtb_ref)
        g128[...] = jnp.broadcast_to(g, (t, LANES))
        b128[...] = jnp.broadcast_to(beta, (t, LANES))
        _pad_zero(pad_s)

        def prep(x_ref, w_ref):
            p = _conv(x_ref[...], w_ref, pad_s)
            sg = _sigmoid(p)
            return p, sg, p * sg

        _, _, yq = prep(q_ref, wq_ref)
        qn[...] = yq * (lax.rsqrt(jnp.sum(yq * yq, axis=1, keepdims=True) + EPS) * (DH ** -0.5))
        _, _, yk = prep(k_ref, wk_ref)
        kn[...] = yk * lax.rsqrt(jnp.sum(yk * yk, axis=1, keepdims=True) + EPS)
        _, _, yv = prep(v_ref, wv_ref)
        vv[...] = yv

        o = oraw_ref[...]
        z = z_ref[...]
        doa = doa_ref[...]
        gn = gn_ref[...]
        ro = lax.rsqrt(jnp.mean(o * o, axis=1, keepdims=True) + EPS)
        sz = _sigmoid(z)
        dz_ref[...] = _bf(doa * (o * ro * gn) * (sz * (1.0 + z * (1.0 - sz))))
        dn = doa * (z * sz)
        dgn = jnp.sum(dn * o * ro, axis=0, keepdims=True)
        gy = dn * gn
        do_s[...] = ro * gy - o * (ro * ro * ro * (1.0 / DH)) * jnp.sum(gy * o, axis=1, keepdims=True)

        cs = _chunk_consts()

        def pre_group(gi, _):
            idx = [gi * GDN_GROUP + c for c in range(GDN_GROUP)]
            rows = [pl.ds(pl.multiple_of(i * CHUNK, CHUNK), CHUNK) for i in idx]
            ins = [_V([ref[r, :] for r in rows]) for ref in (qn, kn, vv, g128, b128)]
            states = _V([_bf(s_ref[0, 0, i]) for i in idx])
            f = _gdn_chunk_pre(*ins, cs)
            v_new = f["w"] - _vdot(_vbf(f["u"]), states)
            for c, r in enumerate(rows):
                bg_s[r, :] = f["big_g"].xs[c]
                u_s[r, :] = f["u"].xs[c]
                vn_s[r, :] = v_new.xs[c]
                dcy_s[r, :] = f["decay"].xs[c]
                kk_s[r, :] = f["kk"].xs[c]
                qk_s[r, :] = f["qk"].xs[c]
                tm_s[r, :] = f["tm"].xs[c]
            return 0

        lax.fori_loop(0, nc // GDN_GROUP, pre_group, 0)

        def chunk(j, ds):
            i = nc - 1 - j
            r = pl.ds(pl.multiple_of(i * CHUNK, CHUNK), CHUNK)
            big_g = bg_s[r, :]
            g_last = jnp.sum(g128[r, :], axis=0, keepdims=True)
            dob = _bf(do_s[r, :])
            dv_new = (_dot_tn(_bf(qk_s[r, :] * dcy_s[r, :]), dob)
                      + _dot(_bf(kn[r, :] * jnp.exp(g_last - big_g)), _bf(ds)))
            dvn_s[r, :] = dv_new
            dsn_s[i] = ds
            return (_dot_tn(_bf(qn[r, :] * jnp.exp(big_g)), dob) + jnp.exp(g_last) * ds
                    - _dot_tn(_bf(u_s[r, :]), _bf(dv_new)))

        lax.fori_loop(0, nc, chunk, jnp.zeros((DH, DH), F32))

        def post_group(gi, _):
            idx = [gi * GDN_GROUP + c for c in range(GDN_GROUP)]
            rows = [pl.ds(pl.multiple_of(i * CHUNK, CHUNK), CHUNK) for i in idx]
            def rows_of(ref):
                return _V([ref[r, :] for r in rows])

            dq, dk, dv, dg, dbeta = _gdn_chunk_post(
                rows_of(qn), rows_of(kn), rows_of(vv), rows_of(g128), rows_of(b128),
                _V([s_ref[0, 0, i] for i in idx]), _V([dsn_s[i] for i in idx]), rows_of(do_s), rows_of(dvn_s),
                rows_of(bg_s), rows_of(dcy_s), rows_of(kk_s), rows_of(qk_s), rows_of(tm_s), rows_of(u_s), rows_of(vn_s),
                cs)
            for c, r in enumerate(rows):
                qn[r, :] = dq.xs[c]
                kn[r, :] = dk.xs[c]
                vv[r, :] = dv.xs[c]
                g128[r, :] = dg.xs[c]
                b128[r, :] = jnp.broadcast_to(dbeta.xs[c], (CHUNK, LANES))
            return 0

        lax.fori_loop(0, nc // GDN_GROUP, post_group, 0)
        dqh, dkh, dvh = qn, kn, vv

        g, beta, a, sp_in = _gdn_gates(ps_ref[...], h, alog_ref, dtb_ref)
        dg = g128[...]
        d_ga = dg * (-a) * _sigmoid(sp_in)
        d_alog = jnp.sum(dg * g, axis=0, keepdims=True)
        d_dtb = jnp.sum(d_ga, axis=0, keepdims=True)
        d_gb = b128[...] * (beta * (1.0 - beta))
        lane = lax.broadcasted_iota(jnp.int32, (t, LANES), 1)
        contrib = jnp.where(lane == LANE_GA + h, d_ga, 0.0) + jnp.where(lane == LANE_GB + h, d_gb, 0.0)

        @pl.when(h == 0)
        def _():
            dps_ref[...] = jnp.zeros_like(dps_ref)

        dps_ref[...] += contrib

        lane1 = lax.broadcasted_iota(jnp.int32, (1, LANES), 1)
        small = _stack_rows([jnp.where(lane1 == h, d_alog, 0.0), jnp.where(lane1 == h, d_dtb, 0.0), dgn], 8)

        @pl.when((b == 0) & (h == 0))
        def _():
            dsm_ref[...] = jnp.zeros_like(dsm_ref)
            dcw_ref[...] = jnp.zeros_like(dcw_ref)

        dsm_ref[...] += small

        def conv_bwd(dp, x, w_ref, slot):
            dw = _stack_rows([jnp.sum(dp * _shifted(pad_s, 3), axis=0, keepdims=True),
                              jnp.sum(dp * _shifted(pad_s, 2), axis=0, keepdims=True),
                              jnp.sum(dp * _shifted(pad_s, 1), axis=0, keepdims=True),
                              jnp.sum(dp * x, axis=0, keepdims=True)], CONV_K)
            dcw_ref[slot] += dw
            pad_s[PAD:PAD + t, :] = dp
            dx = _shifted(pad_s, -3) * w_ref[0:1, :]
            dx = dx + _shifted(pad_s, -2) * w_ref[1:2, :]
            dx = dx + _shifted(pad_s, -1) * w_ref[2:3, :]
            return dx + dp * w_ref[3:4, :]

        def l2_bwd(dqn, y, c):
            r = lax.rsqrt(jnp.sum(y * y, axis=1, keepdims=True) + EPS)
            s1 = jnp.sum(dqn * y, axis=1, keepdims=True)
            return c * r * dqn - (c * r * r * r) * s1 * y

        def silu_bwd(p, sg):
            return sg * (1.0 + p * (1.0 - sg))

        pq, sq, yq = prep(q_ref, wq_ref)
        dq_ref[...] = _bf(conv_bwd(l2_bwd(dqh[...], yq, DH ** -0.5) * silu_bwd(pq, sq), q_ref[...], wq_ref, h))
        pk, sk, yk = prep(k_ref, wk_ref)
        dk_ref[...] = _bf(conv_bwd(l2_bwd(dkh[...], yk, 1.0) * silu_bwd(pk, sk), k_ref[...], wk_ref, HEADS + h))
        pv, sv, _ = prep(v_ref, wv_ref)
        dv_ref[...] = _bf(conv_bwd(dvh[...] * silu_bwd(pv, sv), v_ref[...], wv_ref, 2 * HEADS + h))

    blk = pl.BlockSpec((t, DH), lambda b, h: (b, h))
    ob = jax.ShapeDtypeStruct((n, HEADS * DH), BF16)
    return pl.pallas_call(
        body, name="gdn_bwd", grid=(b_loc, HEADS),
        in_specs=[col(0), col(HEADS), col(2 * HEADS), col(3 * HEADS), ps_spec, wcol(0), wcol(HEADS), wcol(2 * HEADS),
                  smem, smem, vec, blk, blk, pl.BlockSpec((1, 1, nc, DH, DH), lambda b, h: (b, h, 0, 0, 0))],
        out_specs=[blk, blk, blk, blk, ps_spec,
                   pl.BlockSpec((3 * HEADS, CONV_K, DH), lambda b, h: (0, 0, 0)),
                   pl.BlockSpec((8, LANES), lambda b, h: (0, 0))],
        out_shape=[ob, ob, ob, ob, jax.ShapeDtypeStruct((n, LANES), F32),
                   jax.ShapeDtypeStruct((3 * HEADS, CONV_K, DH), F32), jax.ShapeDtypeStruct((8, LANES), F32)],
        scratch_shapes=([pltpu.VMEM((t, DH), F32)] * 3 + [pltpu.VMEM((t, LANES), F32)] * 2
                        + [pltpu.VMEM((t, DH), F32)] * 5 + [pltpu.VMEM((t, CHUNK), F32)] * 4
                        + [pltpu.VMEM((nc, DH, DH), F32), pltpu.VMEM((t + 2 * PAD, LANES), F32)]),
        compiler_params=_cparams(("arbitrary", "arbitrary")),
    )(pg, pg, pg, pg, ps, convw, convw, convw, a_log, dt_bias, gnorm, d_oa, o_raw, s_all)


def _fox_prologue(q_ref, k_ref, v_ref, ps_ref, fb_ref, gq_ref, gk_ref, h, t, qs, ks, vs, ccol, crow):
    nb = t // FOX_BLOCK
    q, k = q_ref[...], k_ref[...]
    rq = lax.rsqrt(jnp.mean(q * q, axis=1, keepdims=True) + EPS)
    rk = lax.rsqrt(jnp.mean(k * k, axis=1, keepdims=True) + EPS)
    qs[...] = _bf(q * rq * gq_ref[...])
    ks[...] = _bf(k * rk * gk_ref[...])
    vs[...] = _bf(v_ref[...])
    f_in = _head_lane(ps_ref[...], LANE_FF + h) + fb_ref[0, h]
    ccol[...] = jnp.broadcast_to(-_softplus(-f_in), (t, LANES))
    r = lax.broadcasted_iota(jnp.int32, (FOX_BLOCK, FOX_BLOCK), 0)
    c = lax.broadcasted_iota(jnp.int32, (FOX_BLOCK, FOX_BLOCK), 1)
    trilf, triuf = (r >= c).astype(F32), (r <= c).astype(F32)
    blocks = [pl.ds(j * FOX_BLOCK, FOX_BLOCK) for j in range(nb)]
    lfs = _V([ccol[rb, :] for rb in blocks])
    cc = _dot_exact_l(trilf, lfs, terms=3)
    cr = _dot_exact_r(lfs, triuf, _vdot_tn, terms=3)
    sums = _vsum(lfs, 0)
    carry = jnp.zeros((1, LANES), F32)
    for j, rb in enumerate(blocks):
        ccol[rb, :] = cc.xs[j] + carry
        crow[j] = (cr.xs[j] + carry)[0:8]
        carry = carry + sums.xs[j]
    return rq, rk, f_in


def _fox_scores(q_rows, k_rows, cc, cr, row0, col0):
    s = _dot_nt(q_rows, k_rows) * (DH ** -0.5) + cc - cr
    r = lax.broadcasted_iota(jnp.int32, s.shape, 0)
    c = lax.broadcasted_iota(jnp.int32, s.shape, 1)
    return jnp.where(row0 + r >= col0 + c, s, NEG)


def _fox_specs(t):
    def col(off):
        return pl.BlockSpec((t, DH), lambda b, h: (b, off + h))

    ps_spec = pl.BlockSpec((t, LANES), lambda b, h: (b, 0))
    smem = pl.BlockSpec(memory_space=pltpu.SMEM)
    vec = pl.BlockSpec((1, DH), lambda b, h: (0, 0))
    blk = pl.BlockSpec((t, DH), lambda b, h: (b, h))
    return col, ps_spec, smem, vec, blk


def _fox_fwd(pf, ps, f_bias, gq, gk, b_loc, t):
    n = b_loc * t
    nb = t // FOX_BLOCK
    assert t % FOX_TILE == 0 and FOX_TILE % FOX_SHORT == 0, (t, FOX_TILE, FOX_SHORT)
    kt = FOX_TILE
    nsub = kt // FOX_BLOCK
    col, ps_spec, smem, vec, blk = _fox_specs(t)

    def body(q_ref, k_ref, v_ref, ps_ref, fb_ref, gq_ref, gk_ref, o_ref, lse_ref, qs, ks, vs, ccol, crow):
        h = pl.program_id(1)
        _fox_prologue(q_ref, k_ref, v_ref, ps_ref, fb_ref, gq_ref, gk_ref, h, t, qs, ks, vs, ccol, crow)

        def qblock(i, _):
            ri = pl.ds(pl.multiple_of(i * FOX_SHORT, FOX_SHORT), FOX_SHORT)
            qi = qs[ri, :]
            cc = jnp.concatenate([ccol[ri, :]] * nsub, axis=1)

            def ktile(j, carry):
                m, l, acc = carry
                rj = pl.ds(pl.multiple_of(j * kt, kt), kt)
                cr = jnp.concatenate([crow[j * nsub + u, 0:1, :] for u in range(nsub)], axis=1)
                s = _fox_scores(qi, ks[rj, :], cc, cr, i * FOX_SHORT, j * kt)
                m_new = jnp.maximum(m, jnp.max(s, axis=1, keepdims=True))
                p = jnp.exp(s - m_new)
                alpha = jnp.exp(m - m_new)
                l = alpha * l + jnp.sum(p, axis=1, keepdims=True)
                acc = alpha * acc + _dot(_bf(p), vs[rj, :])
                return m_new, l, acc

            m, l, acc = lax.fori_loop(0, (i * FOX_SHORT) // kt + 1, ktile, (jnp.full((FOX_SHORT, 1), NEG, F32),
                                                                            jnp.zeros((FOX_SHORT, 1), F32),
                                                                            jnp.zeros((FOX_SHORT, DH), F32)))
            o_ref[ri, :] = acc / l
            lse_ref[ri, :] = jnp.broadcast_to(m + jnp.log(l), (FOX_SHORT, LANES))
            return 0

        lax.fori_loop(0, t // FOX_SHORT, qblock, 0)

    o = jax.ShapeDtypeStruct((n, HEADS * DH), F32)
    return pl.pallas_call(
        body, name="fox_fwd", grid=(b_loc, HEADS),
        in_specs=[col(0), col(HEADS), col(2 * HEADS), ps_spec, smem, vec, vec],
        out_specs=[blk, blk], out_shape=[o, o],
        scratch_shapes=[pltpu.VMEM((t, DH), BF16)] * 3 + [pltpu.VMEM((t, LANES), F32), pltpu.VMEM((nb, 8, LANES), F32)],
        compiler_params=_cparams(("arbitrary", "arbitrary")),
    )(pf, pf, pf, ps, f_bias, gq, gk)


def _fox_bwd(pf, ps, f_bias, gq, gk, d_ob, ob, lse, dps_in, b_loc, t):
    n = b_loc * t
    nb = t // FOX_BLOCK
    assert t % FOX_TILE == 0 and FOX_TILE % FOX_SHORT == 0, (t, FOX_TILE, FOX_SHORT)
    qt = FOX_TILE
    scale = DH ** -0.5
    col, ps_spec, smem, vec, blk = _fox_specs(t)

    def body(q_ref, k_ref, v_ref, ps_ref, fb_ref, gq_ref, gk_ref, do_ref, o_ref, lse_ref, dpsi_ref,
             dq_ref, dk_ref, dv_ref, dps_ref, dsm_ref, qs, ks, vs, ccol, crow, dos, dl, dqa, dcr, dcq):
        b, h = pl.program_id(0), pl.program_id(1)
        rq, _, f_in = _fox_prologue(q_ref, k_ref, v_ref, ps_ref, fb_ref, gq_ref, gk_ref, h, t, qs, ks, vs, ccol, crow)
        dov = do_ref[...]
        dos[...] = _bf(dov)
        dl[...] = jnp.broadcast_to(jnp.sum(dov * o_ref[...], axis=1, keepdims=True), (t, LANES))
        dqa[...] = jnp.zeros_like(dqa)
        dcq[...] = jnp.zeros_like(dcq)
        gkv = gk_ref[...]

        ksub = FOX_SHORT // FOX_BLOCK

        def kblock(j, dgk):
            rj = pl.ds(pl.multiple_of(j * FOX_SHORT, FOX_SHORT), FOX_SHORT)
            kj, vj = ks[rj, :], vs[rj, :]
            cr = jnp.concatenate([crow[j * ksub + u, 0:1, :] for u in range(ksub)], axis=1)

            def wide(x):
                return jnp.concatenate([x] * ksub, axis=1)

            def qtile(i, carry):
                dk_acc, dv_acc, dc = carry
                ri = pl.ds(pl.multiple_of(i * qt, qt), qt)
                qi, doi = qs[ri, :], dos[ri, :]
                s = _fox_scores(qi, kj, wide(ccol[ri, :]), cr, i * qt, j * FOX_SHORT)
                p = jnp.exp(s - wide(lse_ref[ri, :]))
                ds = p * (_dot_nt(doi, vj) - wide(dl[ri, :]))
                dsb = _bf(ds)
                dqa[ri, :] += _dot(dsb, kj)
                dcq[ri, :] += jnp.broadcast_to(jnp.sum(ds, axis=1, keepdims=True), (qt, LANES))
                return (dk_acc + _dot_tn(dsb, qi), dv_acc + _dot_tn(_bf(p), doi),
                        dc - jnp.sum(ds, axis=0, keepdims=True))

            z = jnp.zeros((FOX_SHORT, DH), F32)
            dk_acc, dv_acc, dc = lax.fori_loop((j * FOX_SHORT) // qt, t // qt, qtile,
                                               (z, z, jnp.zeros((1, FOX_SHORT), F32)))
            dv_ref[rj, :] = _bf(dv_acc)
            for u in range(ksub):
                dcr[pl.ds(pl.multiple_of((j * ksub + u) * 8, 8), 8), :] = jnp.broadcast_to(
                    dc[:, u * FOX_BLOCK:(u + 1) * FOX_BLOCK], (8, LANES))
            kraw = k_ref[rj, :]
            rk = lax.rsqrt(jnp.mean(kraw * kraw, axis=1, keepdims=True) + EPS)
            dkn = dk_acc * scale
            gy = dkn * gkv
            dk_ref[rj, :] = _bf(rk * gy - kraw * (rk * rk * rk * (1.0 / DH)) * jnp.sum(gy * kraw, axis=1, keepdims=True))
            return dgk + jnp.sum(dkn * kraw * rk, axis=0, keepdims=True)

        dgk = lax.fori_loop(0, t // FOX_SHORT, kblock, jnp.zeros((1, DH), F32))

        q = q_ref[...]
        dqn = dqa[...] * scale
        gy = dqn * gq_ref[...]
        dq_ref[...] = _bf(rq * gy - q * (rq * rq * rq * (1.0 / DH)) * jnp.sum(gy * q, axis=1, keepdims=True))
        dgq = jnp.sum(dqn * q * rq, axis=0, keepdims=True)

        r = lax.broadcasted_iota(jnp.int32, (FOX_BLOCK, FOX_BLOCK), 0)
        c = lax.broadcasted_iota(jnp.int32, (FOX_BLOCK, FOX_BLOCK), 1)
        triuf = (r <= c).astype(F32)

        def rev(jj, carry):
            j = nb - 1 - jj
            rows = pl.ds(pl.multiple_of(j * FOX_BLOCK, FOX_BLOCK), FOX_BLOCK)
            rowv = dcr[pl.ds(pl.multiple_of(j * 8, 8), 1), :]
            colv = jnp.sum(jnp.where(c >= r, jnp.broadcast_to(rowv, (FOX_BLOCK, LANES)), 0.0), axis=1, keepdims=True)
            qcol = dcq[rows, :]
            dl[rows, :] = colv + _dot_exact_l(triuf, qcol, terms=3) + carry
            return carry + jnp.sum(rowv, axis=1, keepdims=True) + jnp.sum(qcol, axis=0, keepdims=True)

        lax.fori_loop(0, nb, rev, jnp.zeros((1, LANES), F32))
        d_ff = dl[...] * _sigmoid(-f_in)
        lane = lax.broadcasted_iota(jnp.int32, (t, LANES), 1)

        @pl.when(h == 0)
        def _():
            dps_ref[...] = dpsi_ref[...]

        dps_ref[...] += jnp.where(lane == LANE_FF + h, d_ff, 0.0)

        lane1 = lax.broadcasted_iota(jnp.int32, (1, LANES), 1)
        d_fb = jnp.sum(d_ff, axis=0, keepdims=True)
        small = _stack_rows([dgq, dgk, jnp.where(lane1 == h, d_fb, 0.0)], 8)

        @pl.when((b == 0) & (h == 0))
        def _():
            dsm_ref[...] = jnp.zeros_like(dsm_ref)

        dsm_ref[...] += small

    ob_ = jax.ShapeDtypeStruct((n, HEADS * DH), BF16)
    return pl.pallas_call(
        body, name="fox_bwd", grid=(b_loc, HEADS),
        in_specs=[col(0), col(HEADS), col(2 * HEADS), ps_spec, smem, vec, vec, blk, blk, blk, ps_spec],
        out_specs=[blk, blk, blk, ps_spec, pl.BlockSpec((8, LANES), lambda b, h: (0, 0))],
        out_shape=[ob_, ob_, ob_, jax.ShapeDtypeStruct((n, LANES), F32), jax.ShapeDtypeStruct((8, LANES), F32)],
        scratch_shapes=([pltpu.VMEM((t, DH), BF16)] * 3 + [pltpu.VMEM((t, LANES), F32), pltpu.VMEM((nb, 8, LANES), F32)]
                        + [pltpu.VMEM((t, DH), BF16), pltpu.VMEM((t, LANES), F32), pltpu.VMEM((t, DH), F32),
                           pltpu.VMEM((8 * nb, LANES), F32), pltpu.VMEM((t, LANES), F32)]),
        compiler_params=_cparams(("arbitrary", "arbitrary")),
    )(pf, pf, pf, ps, f_bias, gq, gk, d_ob, ob, lse, dps_in)


class _NoExchange:
    def late_weights(self, after):
        return {}

    def grads_ready(self, grads, tie):
        return tie


def _local_step(x, target, w, b_loc, t, comm=None):
    comm = comm or _NoExchange()
    w = dict(w)
    xf = x
    u = _rms_fwd(xf, w["norm_mix_g"], "rms_mix")
    pg = _mm(u, w["w_gdn"], name="proj_gdn")
    pf = _mm(u, w["w_fox"], name="proj_fox")
    pgate = _mm(u, w["w_gate"], name="proj_gate")
    ps = _mm(u, w["w_small"], name="proj_small")
    oa, o_raw, s_all = _gdn_fwd(pg, ps, w["conv_w"], w["a_log"], w["dt_bias"], w["gdn_norm_g"], b_loc, t)
    ob, lse = _fox_fwd(pf, ps, w["f_bias"], w["fox_q_norm_g"], w["fox_k_norm_g"], b_loc, t)
    w.update(comm.late_weights(ob))
    ya = _mm(oa, w["w_proj_gdn"], name="proj_a")
    yb = _mm(ob, w["w_proj_fox"], name="proj_b")
    merged = _merge_fwd(ya, yb, pgate)
    h = _mm(merged, w["w_out"], name="proj_out", epi=lambda acc, xr: acc + xr, extras=(xf,))
    hn = _rms_fwd(h, w["norm_mlp_g"], "rms_mlp")
    up, act = _mm(hn, w["w_up"], name="mlp_up", out_dtype=BF16, out2=(_relu2, BF16))
    out = _mm(act, w["w_down"], name="mlp_down", epi=lambda acc, hr: acc + hr, extras=(h,))
    d_out, d_out16, loss_blk = _loss_bwd(out, target)

    g = {}
    g["w_down"] = _mm(act, d_out16, name="dw_down", ta=True, out_dtype=BF16)
    d_up = _mm(d_out16, w["w_down"], name="d_up", tb=True, out_dtype=BF16,
               epi=lambda acc, upr: acc * (2.0 * jnp.maximum(upr.astype(F32), 0.0)), extras=(up,))
    g["w_up"] = _mm(hn, d_up, name="dw_up", ta=True, out_dtype=BF16)
    mlp_gain = comm.grads_ready({"w_down": g["w_down"], "w_up": g["w_up"]}, w["norm_mlp_g"])
    d_hn = _mm(d_up, w["w_up"], name="d_hn", tb=True)
    dh, dh16, g["norm_mlp_g"] = _rms_bwd(d_hn, h, mlp_gain, d_out, "rms_mlp_bwd")
    g["w_out"] = _mm(merged, dh16, name="dw_out", ta=True, out_dtype=BF16)
    dm = _mm(dh16, w["w_out"], name="d_merged", tb=True)
    dya, dyb, dgate_a, dgate_b = _merge_bwd(dm, ya, yb, pgate)
    g["w_proj_gdn"] = _mm(oa, dya, name="dw_proj_a", ta=True, out_dtype=BF16)
    g["w_proj_fox"] = _mm(ob, dyb, name="dw_proj_b", ta=True, out_dtype=BF16)
    gdn_gain = comm.grads_ready({"w_out": g["w_out"], "w_proj_gdn": g["w_proj_gdn"], "w_proj_fox": g["w_proj_fox"]},
                                w["gdn_norm_g"])
    d_oa = _mm(dya, w["w_proj_gdn"], name="d_oa", tb=True)
    d_ob = _mm(dyb, w["w_proj_fox"], name="d_ob", tb=True)
    dgq, dgk, dgv, dgz, dps, dcw, gdn_small = _gdn_bwd(pg, ps, w["conv_w"], w["a_log"], w["dt_bias"], gdn_gain,
                                                       d_oa, o_raw, s_all, b_loc, t)
    dfq, dfk, dfv, dps, fox_small = _fox_bwd(pf, ps, w["f_bias"], w["fox_q_norm_g"], w["fox_k_norm_g"],
                                             d_ob, ob, lse, dps, b_loc, t)
    segs = [(dgq, "w_gdn", 0), (dgk, "w_gdn", 1024), (dgv, "w_gdn", 2048), (dgz, "w_gdn", 3072),
            (dfq, "w_fox", 0), (dfk, "w_fox", 1024), (dfv, "w_fox", 2048),
            (dgate_a, "w_gate", 0), (dgate_b, "w_gate", 1024)]
    dws = [_mm(u, dps, name="dw_small", ta=True, out_dtype=BF16)]
    dws += [_mm(u, dseg, name=f"dw_in_{idx}", ta=True, out_dtype=BF16) for idx, (dseg, _, _) in enumerate(segs)]
    g["w_in_parts"] = dws
    mix_gain = comm.grads_ready({"w_in_parts": dws}, w["norm_mix_g"])
    du = _du_all(dps, w["w_small"], segs, w)
    grad_x, _, g["norm_mix_g"] = _rms_bwd(du, xf, mix_gain, dh, "rms_mix_bwd")
    g["conv"] = dcw
    g["gdn_small"] = gdn_small
    g["fox_small"] = fox_small
    return loss_blk, grad_x, g


def _position():
    x, y, c = lax.axis_index("x"), lax.axis_index("y"), lax.axis_index("c")
    return x, y, c


def _to_bf16(arrs, name):
    n = len(arrs)

    def body(*refs):
        for i in range(n):
            refs[n + i][...] = _bf(refs[i][...])

    return pl.pallas_call(
        body, name=name,
        out_shape=[jax.ShapeDtypeStruct(a.shape, BF16) for a in arrs],
        compiler_params=_cparams(),
    )(*arrs)


def _all_gather(arrs, name):
    n = len(arrs)
    hbm = pl.BlockSpec(memory_space=pl.ANY)

    def body(*refs):
        ins, outs = refs[:n], refs[n:2 * n]
        send, recv, loc = refs[2 * n:]
        x, y, c = _position()
        me = 4 * x + 2 * y + c
        sibling = (x, y, 1 - c)
        chips = [(1 - x, y), (x, 1 - y), (1 - x, 1 - y)]

        def idx(px, py, pc):
            return 4 * px + 2 * py + pc

        def cp(a, k, block, to, src=None):
            return pltpu.make_async_remote_copy(
                src_ref=outs[a].at[block] if src is None else src, dst_ref=outs[a].at[block],
                send_sem=send.at[a, k], recv_sem=recv.at[a, k], device_id=to, device_id_type=MESH)

        mine = [pltpu.make_async_copy(ins[a], outs[a].at[me], loc.at[a]) for a in range(n)]
        for m in mine:
            m.start()
        first = []
        for a in range(n):
            first.append(cp(a, 0, me, sibling, src=ins[a]))
            first += [cp(a, 1 + j, me, (*chip, c), src=ins[a]) for j, chip in enumerate(chips)]
        for f in first:
            f.start()
        passed = []
        for j, chip in enumerate(chips):
            for a in range(n):
                cp(a, 1 + j, idx(*chip, c), (x, y, c)).wait_recv()
                p = cp(a, 4 + j, idx(*chip, c), sibling)
                p.start()
                passed.append(p)
        for a in range(n):
            cp(a, 0, idx(x, y, 1 - c), (x, y, c)).wait_recv()
            for j, chip in enumerate(chips):
                cp(a, 4 + j, idx(*chip, 1 - c), (x, y, c)).wait_recv()
        for f in first + passed:
            f.wait_send()
        for m in mine:
            m.wait()

    return pl.pallas_call(
        body, name=name,
        in_specs=[hbm] * n, out_specs=[hbm] * n,
        out_shape=[jax.ShapeDtypeStruct((N_DEV,) + a.shape, a.dtype) for a in arrs],
        scratch_shapes=[pltpu.SemaphoreType.DMA((n, 7)), pltpu.SemaphoreType.DMA((n, 7)), pltpu.SemaphoreType.DMA((n,))],
        compiler_params=pltpu.CompilerParams(has_side_effects=True),
    )(*arrs)


def _peer(x, y, c, rel):
    return ((1 - x) if rel & 4 else x, (1 - y) if rel & 2 else y, (1 - c) if rel & 1 else c)


HBM_SPEC = pl.BlockSpec(memory_space=pltpu.HBM)
SEM_SPEC = pl.BlockSpec(memory_space=pltpu.SEMAPHORE)
DATAFLOW = pltpu.SideEffectType.DATAFLOW_SIDE_EFFECTING


CHIP_RELS = (2, 4, 6)


def _push_start(arrs, slots, name, chips=False):
    n = len(arrs)
    n_slots = 4 if chips else N_DEV
    rels = CHIP_RELS if chips else tuple(range(1, N_DEV))
    land_shapes = [a.shape if slots else (n_slots,) + a.shape for a in arrs]

    def body(*refs):
        ins, lands, sends, recvs, token = refs[:n], refs[n:2 * n], refs[2 * n:3 * n], refs[3 * n:4 * n], refs[-1]
        x, y, c = _position()
        for rel in rels:
            px, py, pc = _peer(x, y, c, rel)
            mine, theirs = (2 * x + y, 2 * px + py) if chips else (4 * x + 2 * y + c, 4 * px + 2 * py + pc)
            for a in range(n):
                pltpu.make_async_remote_copy(
                    src_ref=ins[a].at[theirs] if slots else ins[a], dst_ref=lands[a].at[mine],
                    send_sem=sends[a], recv_sem=recvs[a], device_id=(px, py, pc), device_id_type=MESH).start()
        token[...] = jnp.zeros_like(token)

    sem = pltpu.SemaphoreType.DMA(())
    outs = pl.pallas_call(
        body, name=name,
        in_specs=[HBM_SPEC] * (2 * n),
        out_shape=(*[sem] * (2 * n), *[pltpu.HBM(a.shape, a.dtype) for a in arrs],
                   *[pltpu.HBM(s, a.dtype) for s, a in zip(land_shapes, arrs)], jax.ShapeDtypeStruct((8, LANES), F32)),
        out_specs=(*[SEM_SPEC] * (2 * n), *[HBM_SPEC] * (2 * n), pl.BlockSpec(memory_space=pltpu.VMEM)),
        input_output_aliases={i: 2 * n + i for i in range(2 * n)},
        compiler_params=pltpu.CompilerParams(has_side_effects=DATAFLOW),
    )(*[pltpu.with_memory_space_constraint(a, pltpu.HBM) for a in arrs],
      *[pltpu.with_memory_space_constraint(lax.empty(s, a.dtype), pltpu.HBM) for s, a in zip(land_shapes, arrs)])
    return dict(sends=list(outs[:n]), recvs=list(outs[n:2 * n]), ins=list(outs[2 * n:3 * n]),
                lands=list(outs[3 * n:4 * n]), token=outs[-1], copies=len(rels))


def _push_wait(started, after, name):
    n = len(started["ins"])
    copies = started["copies"]

    def body(*refs):
        lands, sends, recvs = refs[n:2 * n], refs[2 * n:3 * n], refs[3 * n:4 * n]
        x, y, c = _position()
        for a in range(n):
            every = lands[a].at[pl.ds(0, copies)]
            drain = pltpu.make_async_remote_copy(src_ref=every, dst_ref=every, send_sem=sends[a], recv_sem=recvs[a],
                                                 device_id=(x, y, c), device_id_type=MESH)
            drain.wait_send()
            drain.wait_recv()

    both = started["ins"] + started["lands"]
    outs = pl.pallas_call(
        body, name=name,
        in_specs=[HBM_SPEC] * (2 * n) + [SEM_SPEC] * (2 * n) + [pl.BlockSpec(memory_space=pl.ANY)],
        out_shape=tuple(pltpu.HBM(a.shape, a.dtype) for a in both), out_specs=tuple([HBM_SPEC] * (2 * n)),
        input_output_aliases={i: i for i in range(2 * n)},
        compiler_params=pltpu.CompilerParams(has_side_effects=DATAFLOW),
    )(*both, *started["sends"], *started["recvs"], after)
    return list(outs[:n]), list(outs[n:])


def _sibling_swap(arr, name):
    chips = N_DEV // 2

    def body(in_ref, out_ref, send, recv):
        x, y, c = _position()
        for s in range(chips):
            pltpu.make_async_remote_copy(src_ref=in_ref.at[2 * s + 1 - c], dst_ref=out_ref.at[s], send_sem=send,
                                         recv_sem=recv, device_id=(x, y, 1 - c), device_id_type=MESH).start()
        pltpu.make_async_remote_copy(src_ref=out_ref, dst_ref=out_ref, send_sem=send, recv_sem=recv,
                                     device_id=(x, y, 1 - c), device_id_type=MESH).wait()

    hbm = pl.BlockSpec(memory_space=pl.ANY)
    return pl.pallas_call(
        body, name=name, in_specs=[hbm], out_specs=hbm,
        out_shape=jax.ShapeDtypeStruct((chips,) + arr.shape[1:], arr.dtype),
        scratch_shapes=[pltpu.SemaphoreType.DMA, pltpu.SemaphoreType.DMA],
        compiler_params=pltpu.CompilerParams(has_side_effects=True),
    )(arr)


def _add_halves(core, arr, other, name):
    ns, r, c = other.shape
    tr = min(r, 256)

    def body(core_ref, a_ref, o_ref, out_ref):
        out_ref[...] = _bf(a_ref[...].astype(F32) + o_ref[...].astype(F32))

    blk = pl.BlockSpec((1, tr, c), lambda s, i, core_ref: (s, i, 0))
    return pl.pallas_call(
        body, name=name,
        grid_spec=pltpu.PrefetchScalarGridSpec(
            num_scalar_prefetch=1, grid=(ns, r // tr),
            in_specs=[pl.BlockSpec((1, tr, c), lambda s, i, core_ref: (2 * s + core_ref[0], i, 0)), blk],
            out_specs=blk),
        out_shape=jax.ShapeDtypeStruct((ns, r, c), BF16),
        compiler_params=_cparams(("parallel", "parallel")),
    )(core, arr, other)


def _all_reduce_small(buf, name):
    rows = buf.shape[0]

    def body(in_ref, out_ref, slots, send, recv):
        x, y, c = _position()
        me = 4 * x + 2 * y + c
        slots[me] = in_ref[...]
        copies = []
        for rel in range(1, N_DEV):
            copies.append(pltpu.make_async_remote_copy(
                src_ref=in_ref, dst_ref=slots.at[me], send_sem=send.at[rel - 1], recv_sem=recv.at[rel - 1],
                device_id=_peer(x, y, c, rel), device_id_type=MESH))
        for cpy in copies:
            cpy.start()
        for cpy in copies:
            cpy.wait()
        tot = slots[0]
        for d in range(1, N_DEV):
            tot = tot + slots[d]
        out_ref[...] = tot

    return pl.pallas_call(
        body, name=name,
        out_shape=jax.ShapeDtypeStruct((rows, LANES), F32),
        in_specs=[pl.BlockSpec(memory_space=pltpu.VMEM)], out_specs=pl.BlockSpec(memory_space=pltpu.VMEM),
        scratch_shapes=[pltpu.VMEM((N_DEV, rows, LANES), F32), pltpu.SemaphoreType.DMA((7,)),
                        pltpu.SemaphoreType.DMA((7,))],
        compiler_params=pltpu.CompilerParams(has_side_effects=True),
    )(buf)


def _adam_math(g, w, m, v):
    m = ADAM_B1 * m + (1.0 - ADAM_B1) * g
    v = ADAM_B2 * v + (1.0 - ADAM_B2) * (g * g)
    m_hat = m / (1.0 - ADAM_B1 ** ADAM_STEP)
    v_hat = v / (1.0 - ADAM_B2 ** ADAM_STEP)
    delta = -ADAM_LR * (m_hat / (jnp.sqrt(v_hat) + ADAM_EPS) + ADAM_WD * w)
    return delta, m, v


def _adam_shard(me, parts, mine, w, m, v, name):
    r, c = w.shape
    tr = min(r, 128)
    n_slots = parts.shape[0]

    def body(me_ref, p_ref, own_ref, w_ref, m_ref, v_ref, g_ref, d_ref, nm_ref, nv_ref):
        own = own_ref[0].astype(F32)
        g = None
        for s in range(n_slots):
            term = jnp.where(me_ref[0] == s, own, p_ref[s].astype(F32))
            g = term if g is None else g + term
        d, nm, nv = _adam_math(g, w_ref[...], m_ref[...], v_ref[...])
        g_ref[...] = g
        d_ref[...] = d
        nm_ref[...] = nm
        nv_ref[...] = nv

    row = pl.BlockSpec((tr, c), lambda i, me_ref: (i, 0))
    o = jax.ShapeDtypeStruct((r, c), F32)
    return pl.pallas_call(
        body, name=name,
        grid_spec=pltpu.PrefetchScalarGridSpec(
            num_scalar_prefetch=1, grid=(r // tr,),
            in_specs=[pl.BlockSpec((n_slots, tr, c), lambda i, me_ref: (0, i, 0)),
                      pl.BlockSpec((1, tr, c), lambda i, me_ref: (me_ref[0], i, 0)), row, row, row],
            out_specs=[row] * 4),
        out_shape=[o] * 4,
        compiler_params=_cparams(("parallel",)),
    )(me, parts, mine, w, m, v)


def _adam_small(g, w, m, v):
    def body(g_ref, w_ref, m_ref, v_ref, d_ref, nm_ref, nv_ref):
        d, nm, nv = _adam_math(g_ref[...], w_ref[...], m_ref[...], v_ref[...])
        d_ref[...] = d
        nm_ref[...] = nm
        nv_ref[...] = nv

    o = jax.ShapeDtypeStruct(g.shape, F32)
    return pl.pallas_call(body, name="adam_small", out_shape=[o] * 3, compiler_params=_cparams())(g, w, m, v)


def _split_w_in(w_full):
    o = IN_OFF
    w_gdn = w_full[:, o["gq"]:o["ga"]]
    w_fox = w_full[:, o["fq"]:o["ff"]]
    w_gate = w_full[:, o["gate_a"]:o["end"]]
    w_small = jnp.concatenate([w_full[:, o["ga"]:o["fq"]], w_full[:, o["ff"]:o["gate_a"]],
                               jnp.zeros((w_full.shape[0], LANES - 24), w_full.dtype)], axis=1)
    return w_gdn, w_fox, w_gate, w_small


def _join_w_in(parts):
    small = parts[0]
    return jnp.concatenate(parts[1:5] + [small[:, 0:16]] + parts[5:8] + [small[:, 16:24]] + parts[8:10], axis=1)


def _rows128(a, rows):
    flat = a.reshape(-1)
    flat = jnp.concatenate([flat, jnp.zeros((rows * LANES - flat.shape[0],), flat.dtype)])
    return flat.reshape(rows, LANES)


def kernel(x, norm_mix_g, w_in, gdn_conv_w, gdn_a_log, gdn_dt_bias, gdn_norm_g, fox_q_norm_g, fox_k_norm_g, fox_f_bias, w_proj_gdn, w_proj_fox, w_out, norm_mlp_g, w_up, w_down, loss_target, m_norm_mix_g, m_w_in, m_gdn_conv_w, m_gdn_a_log, m_gdn_dt_bias, m_gdn_norm_g, m_fox_q_norm_g, m_fox_k_norm_g, m_fox_f_bias, m_w_proj_gdn, m_w_proj_fox, m_w_out, m_norm_mlp_g, m_w_up, m_w_down, v_norm_mix_g, v_w_in, v_gdn_conv_w, v_gdn_a_log, v_gdn_dt_bias, v_gdn_norm_g, v_fox_q_norm_g, v_fox_k_norm_g, v_fox_f_bias, v_w_proj_gdn, v_w_proj_fox, v_w_out, v_norm_mlp_g, v_w_up, v_w_down):
    b_loc, t, d = x.shape
    n = b_loc * t
    me = 4 * lax.axis_index("x") + 2 * lax.axis_index("y") + lax.axis_index("c")

    late_names = ["w_proj_gdn", "w_proj_fox", "w_out", "w_up", "w_down"]
    big16 = _to_bf16([w_in[0], w_proj_gdn[0], w_proj_fox[0], w_out[0], w_up[0], w_down[0]], "weights_to_bf16")
    g_in, g_conv = _all_gather([big16[0], gdn_conv_w[0]], "gather_w_in")
    late = _push_start(list(big16[1:]), False, "gather_late_start")
    w_full = g_in.transpose(1, 0, 2).reshape(d, N_DEV * w_in.shape[2])
    w_gdn, w_fox, w_gate, w_small = _split_w_in(w_full)
    weights = {
        "w_gdn": w_gdn, "w_fox": w_fox, "w_gate": w_gate, "w_small": w_small,
        "conv_w": g_conv.transpose(1, 0, 2).reshape(CONV_K, 3 * d),
        "norm_mix_g": norm_mix_g + late["token"][0:1, 0:1], "norm_mlp_g": norm_mlp_g, "a_log": gdn_a_log,
        "dt_bias": gdn_dt_bias, "gdn_norm_g": gdn_norm_g, "fox_q_norm_g": fox_q_norm_g, "fox_k_norm_g": fox_k_norm_g,
        "f_bias": fox_f_bias,
    }
    c_in, c_up = w_in.shape[2], w_up.shape[2]
    me1 = jnp.reshape(me, (1,)).astype(jnp.int32)
    chip1 = jnp.reshape(2 * lax.axis_index("x") + lax.axis_index("y"), (1,)).astype(jnp.int32)
    core1 = jnp.reshape(lax.axis_index("c"), (1,)).astype(jnp.int32)

    class _Exchange:
        def __init__(self):
            self.started = []

        def late_weights(self, after):
            shards, lands = _push_wait(late, after, "gather_late_wait")
            full = [lax.dynamic_update_index_in_dim(land, shard, me, 0) for land, shard in zip(lands, shards)]
            g_pa, g_pb, g_out, g_up, g_down = full
            return {"w_proj_gdn": g_pa.reshape(d, d), "w_proj_fox": g_pb.reshape(d, d), "w_out": g_out.reshape(d, d),
                    "w_up": g_up.transpose(1, 0, 2).reshape(d, D_FF), "w_down": g_down.reshape(D_FF, d)}

        def grads_ready(self, grads, tie):
            names = list(grads)
            if names == ["w_in_parts"]:
                halves = _join_w_in(grads["w_in_parts"]).reshape(d, N_DEV, c_in).transpose(1, 0, 2)
                other = _sibling_swap(halves, "grads_w_in_sibling")
                pair = _add_halves(core1, halves, other, "grads_w_in_pair")
                st = _push_start([pair], True, "grads_start_w_in_parts", chips=True)
            else:
                layout = {"w_up": lambda a: a.reshape(d, N_DEV, c_up).transpose(1, 0, 2),
                          "w_down": lambda a: a.reshape(N_DEV, D_FF // N_DEV, d)}
                arrs = [layout.get(k, lambda a: a.reshape(N_DEV, d // N_DEV, d))(grads[k]) for k in names]
                st = _push_start(arrs, True, "grads_start_" + names[0])
            self.started.append((names, st))
            return tie + st["token"][0:1, 0:1]

    comm = _Exchange()
    loss_blk, grad_x, g = _local_step(x.reshape(n, d), loss_target.reshape(n, d), weights, b_loc, t, comm)

    shards = {"w_in_parts": (w_in, m_w_in, v_w_in), "w_proj_gdn": (w_proj_gdn, m_w_proj_gdn, v_w_proj_gdn),
              "w_proj_fox": (w_proj_fox, m_w_proj_fox, v_w_proj_fox), "w_out": (w_out, m_w_out, v_w_out),
              "w_up": (w_up, m_w_up, v_w_up), "w_down": (w_down, m_w_down, v_w_down)}
    adam = {}

    def finish(names, st, after):
        mine, parts = _push_wait(st, after, "grads_wait_" + names[0])
        slot = chip1 if st["copies"] == len(CHIP_RELS) else me1
        for k, own, part in zip(names, mine, parts):
            wi, mi, vi = shards[k]
            adam[k] = [r[None] for r in _adam_shard(slot, part, own, wi[0], mi[0], vi[0], "adam_" + k)]

    for names, st in comm.started[:-1]:
        finish(names, st, grad_x)

    conv_rows = CONV_K * 3 * d // LANES
    conv_g = g["conv"].transpose(1, 0, 2).reshape(conv_rows, LANES)
    buf = jnp.concatenate([conv_g, g["norm_mix_g"].reshape(8, LANES), g["norm_mlp_g"].reshape(8, LANES),
                           g["gdn_small"], g["fox_small"], loss_blk], axis=0)
    anchor = sum(adam[k][1][0, 0:1, 0:LANES] for names, _ in comm.started[:-1] for k in names) * 0.0
    tot = _all_reduce_small(buf + anchor, "all_reduce_small")
    finish(*comm.started[-1], tot)
    big_out = [adam[k] for k in ["w_in_parts"] + late_names]
    o = conv_rows
    conv_full = tot[0:o].reshape(CONV_K, 3 * d)
    c_conv = gdn_conv_w.shape[2]
    g_conv_shard = lax.dynamic_slice(conv_full, (0, me * c_conv), (CONV_K, c_conv))
    g_mix = tot[o:o + 8].reshape(1, d)
    g_mlp = tot[o + 8:o + 16].reshape(1, d)
    gs, fs = tot[o + 16:o + 24], tot[o + 24:o + 32]
    loss = tot[o + 32, 0]
    small_g = [g_mix, g_conv_shard[None], gs[0:1, 0:HEADS], gs[1:2, 0:HEADS], gs[2:3], fs[0:1], fs[1:2], fs[2:3, 0:HEADS],
               g_mlp]
    small_w = [norm_mix_g, gdn_conv_w, gdn_a_log, gdn_dt_bias, gdn_norm_g, fox_q_norm_g, fox_k_norm_g, fox_f_bias,
               norm_mlp_g]
    small_m = [m_norm_mix_g, m_gdn_conv_w, m_gdn_a_log, m_gdn_dt_bias, m_gdn_norm_g, m_fox_q_norm_g, m_fox_k_norm_g,
               m_fox_f_bias, m_norm_mlp_g]
    small_v = [v_norm_mix_g, v_gdn_conv_w, v_gdn_a_log, v_gdn_dt_bias, v_gdn_norm_g, v_fox_q_norm_g, v_fox_k_norm_g,
               v_fox_f_bias, v_norm_mlp_g]
    row_counts = [-(-a.size // (8 * LANES)) * 8 for a in small_w]

    def pack(arrs):
        return jnp.concatenate([_rows128(a, rc) for a, rc in zip(arrs, row_counts)], axis=0)

    sd, sm, sv = _adam_small(pack(small_g), pack(small_w), pack(small_m), pack(small_v))

    def unpack(p):
        outs, r0 = [], 0
        for a, rc in zip(small_w, row_counts):
            outs.append(p[r0:r0 + rc].reshape(-1)[:a.size].reshape(a.shape))
            r0 += rc
        return outs

    small_out = [small_g_i.reshape(w_i.shape) for small_g_i, w_i in zip(small_g, small_w)], unpack(sd), unpack(sm), unpack(sv)

    def ordered(kind):
        s = small_out[kind]
        bo = [b[kind] for b in big_out]
        return [s[0], bo[0], s[1], s[2], s[3], s[4], s[5], s[6], s[7], bo[1], bo[2], bo[3], s[8], bo[4], bo[5]]

    return (loss, grad_x.reshape(b_loc, t, d), *ordered(0), *ordered(1), *ordered(2), *ordered(3))
```

```python
import functools

import jax
import jax.numpy as jnp
from jax import lax
from jax.experimental import pallas as pl
from jax.experimental.pallas import tpu as pltpu

F32 = jnp.float32
BF16 = jnp.bfloat16
MESH = pl.DeviceIdType.MESH

N_DEV = 8
D_MODEL = 1024
HEADS = 8
DH = 128
CONV_K = 4
CHUNK = 128
GDN_GROUP = 8
FOX_BLOCK = 128
FOX_TILE = 512
FOX_SHORT = 512
D_FF = 4 * D_MODEL
EPS = 1e-6
LANES = 128
NEG = -1e30
IN_OFF = {"gq": 0, "gk": 1024, "gv": 2048, "gz": 3072, "ga": 4096, "gb": 4104, "fq": 4112, "fk": 5136,
          "fv": 6160, "ff": 7184, "gate_a": 7192, "gate_b": 8216, "end": 9240}
LANE_GA, LANE_GB, LANE_FF = 0, 8, 16

ADAM_LR = 0.001
ADAM_B1 = 0.9
ADAM_B2 = 0.999
ADAM_EPS = 1e-08
ADAM_WD = 0.01
ADAM_STEP = 10

VMEM_LIMIT = 56 * 1024 * 1024


def _cparams(sem=None):
    return pltpu.CompilerParams(dimension_semantics=sem, vmem_limit_bytes=VMEM_LIMIT)


def _sigmoid(x):
    return 1.0 / (1.0 + jnp.exp(-x))


def _softplus(x):
    return jnp.maximum(x, 0.0) + jnp.log(1.0 + jnp.exp(-jnp.abs(x)))


def _dot(a, b, prec=None):
    return lax.dot_general(a, b, (((1,), (0,)), ((), ())), precision=prec, preferred_element_type=F32)


def _dot_nt(a, b, prec=None):
    return lax.dot_general(a, b, (((1,), (1,)), ((), ())), precision=prec, preferred_element_type=F32)


def _dot_tn(a, b, prec=None):
    return lax.dot_general(a, b, (((0,), (0,)), ((), ())), precision=prec, preferred_element_type=F32)


def _bf(x):
    return x.astype(BF16)


MM_TILE = 1024


def _mm(a, b, *, name, ta=False, tb=False, out_dtype=F32, epi=None, extras=(), out2=None,
        b_koff=0, tm=MM_TILE, tn=MM_TILE, tk=MM_TILE):
    m = a.shape[1] if ta else a.shape[0]
    kdim = a.shape[0] if ta else a.shape[1]
    n = b.shape[0] if tb else b.shape[1]
    tm, tn, tk = min(tm, m), min(tn, n), min(tk, kdim)
    nk = kdim // tk
    grid = (m // tm, n // tn, nk)
    koff = b_koff // tk
    a_spec = pl.BlockSpec((tk, tm), lambda i, j, k: (k, i)) if ta else pl.BlockSpec((tm, tk), lambda i, j, k: (i, k))
    if tb:
        b_spec = pl.BlockSpec((tn, tk), lambda i, j, k: (j, k + koff))
    else:
        b_spec = pl.BlockSpec((tk, tn), lambda i, j, k: (k + koff, j))
    o_spec = pl.BlockSpec((tm, tn), lambda i, j, k: (i, j))
    n_e = len(extras)
    n_o = 1 if out2 is None else 2
    dims = (((0 if ta else 1,), (1 if tb else 0,)), ((), ()))

    def body(a_ref, b_ref, *rest):
        e_refs, o_refs = rest[:n_e], rest[n_e:n_e + n_o]
        prod = lax.dot_general(_bf(a_ref[...]), _bf(b_ref[...]), dims, preferred_element_type=F32)

        def finish(r):
            if out2 is not None:
                o_refs[1][...] = out2[0](r).astype(out2[1])
            if epi is not None:
                r = epi(r, *[e[...] for e in e_refs])
            o_refs[0][...] = r.astype(out_dtype)

        if nk == 1:
            finish(prod)
        else:
            acc = rest[n_e + n_o]
            k = pl.program_id(2)

            @pl.when(k == 0)
            def _():
                acc[...] = prod

            @pl.when(k > 0)
            def _():
                acc[...] += prod

            @pl.when(k == nk - 1)
            def _():
                finish(acc[...])

    shapes = [jax.ShapeDtypeStruct((m, n), out_dtype)]
    if out2 is not None:
        shapes.append(jax.ShapeDtypeStruct((m, n), out2[1]))
    res = pl.pallas_call(
        body, name=name, grid=grid,
        in_specs=[a_spec, b_spec] + [o_spec] * n_e,
        out_specs=[o_spec] * n_o, out_shape=shapes,
        scratch_shapes=[] if nk == 1 else [pltpu.VMEM((tm, tn), F32)],
        compiler_params=_cparams(("parallel", "parallel", "arbitrary")),
    )(a, b, *extras)
    return res[0] if out2 is None else res


def _du_all(dps, w_small, segs, w, tm=512):
    n, d = dps.shape[0], w_small.shape[0]
    names = []
    for _, wname, _ in segs:
        if wname not in names:
            names.append(wname)
    first = {nm: min(i for i, s in enumerate(segs) if s[1] == nm) for nm in names}
    count = {nm: sum(1 for s in segs if s[1] == nm) for nm in names}
    n_seg, n_i = len(segs), n // tm

    def w_spec(nm):
        return pl.BlockSpec((d, d), lambda k, i: (0, jnp.clip(k - first[nm], 0, count[nm] - 1)))

    def rows_spec(cols, j):
        return pl.BlockSpec((tm, cols), lambda k, i: (jnp.where(k == j, i, jnp.where(k < j, 0, n_i - 1)), 0))

    def body(dps_ref, ws_ref, *rest):
        seg_refs, w_refs, o_ref, acc = rest[:n_seg], rest[n_seg:n_seg + len(names)], rest[-2], rest[-1]
        k, i = pl.program_id(0), pl.program_id(1)
        rows = pl.ds(pl.multiple_of(i * tm, tm), tm)

        @pl.when(k == 0)
        def _():
            acc[rows, :] = _dot_nt(_bf(dps_ref[...]), ws_ref[...])

        for idx, (_, wname, _) in enumerate(segs):
            @pl.when(k == idx)
            def _(idx=idx, wname=wname):
                acc[rows, :] += _dot_nt(seg_refs[idx][...], w_refs[names.index(wname)][...])

        @pl.when(k == n_seg - 1)
        def _():
            o_ref[...] = acc[rows, :]

    return pl.pallas_call(
        body, name="du_all", grid=(n_seg, n_i),
        in_specs=[rows_spec(dps.shape[1], 0), pl.BlockSpec(w_small.shape, lambda k, i: (0, 0))]
                 + [rows_spec(d, j) for j in range(n_seg)] + [w_spec(nm) for nm in names],
        out_specs=pl.BlockSpec((tm, d), lambda k, i: (jnp.where(k == n_seg - 1, i, 0), 0)),
        out_shape=jax.ShapeDtypeStruct((n, d), F32),
        scratch_shapes=[pltpu.VMEM((n, d), F32)],
        compiler_params=_cparams(("arbitrary", "arbitrary")),
    )(dps, w_small, *[s[0] for s in segs], *[w[nm] for nm in names])


def _relu2(x):
    r = jnp.maximum(x, 0.0)
    return r * r


ROWS = 512


def _rms_fwd(x, g, name):
    n, d = x.shape

    def body(x_ref, g_ref, u_ref):
        xv = x_ref[...]
        r = lax.rsqrt(jnp.mean(xv * xv, axis=1, keepdims=True) + EPS)
        u_ref[...] = _bf(xv * r * g_ref[...])

    return pl.pallas_call(
        body, name=name, grid=(n // ROWS,),
        in_specs=[pl.BlockSpec((ROWS, d), lambda i: (i, 0)), pl.BlockSpec((1, d), lambda i: (0, 0))],
        out_specs=pl.BlockSpec((ROWS, d), lambda i: (i, 0)),
        out_shape=jax.ShapeDtypeStruct((n, d), BF16),
        compiler_params=_cparams(("parallel",)),
    )(x, g)


def _rms_bwd(dy, x, g, dres, name):
    n, d = x.shape

    def body(dy_ref, x_ref, g_ref, dres_ref, dx_ref, dx16_ref, dg_ref):
        i = pl.program_id(0)
        xv, dyv = x_ref[...], dy_ref[...]
        r = lax.rsqrt(jnp.mean(xv * xv, axis=1, keepdims=True) + EPS)
        gy = dyv * g_ref[...]
        s = jnp.sum(gy * xv, axis=1, keepdims=True)
        dx = dres_ref[...] + r * gy - xv * (r * r * r * (1.0 / d)) * s
        dx_ref[...] = dx
        dx16_ref[...] = _bf(dx)

        @pl.when(i == 0)
        def _():
            dg_ref[...] = jnp.zeros_like(dg_ref)

        dg_ref[...] += jnp.sum(dyv * xv * r, axis=0, keepdims=True)

    row = pl.BlockSpec((ROWS, d), lambda i: (i, 0))
    vec = pl.BlockSpec((1, d), lambda i: (0, 0))
    return pl.pallas_call(
        body, name=name, grid=(n // ROWS,),
        in_specs=[row, row, vec, row], out_specs=[row, row, vec],
        out_shape=[jax.ShapeDtypeStruct((n, d), F32), jax.ShapeDtypeStruct((n, d), BF16),
                   jax.ShapeDtypeStruct((1, d), F32)],
        compiler_params=_cparams(("arbitrary",)),
    )(dy, x, g, dres)


def _merge_fwd(ya, yb, gate):
    n, d = ya.shape

    def body(ya_ref, yb_ref, ga_ref, gb_ref, o_ref):
        o_ref[...] = _bf(_sigmoid(ga_ref[...]) * ya_ref[...] + _sigmoid(gb_ref[...]) * yb_ref[...])

    row = pl.BlockSpec((ROWS, d), lambda i: (i, 0))
    return pl.pallas_call(
        body, name="merge_fwd", grid=(n // ROWS,),
        in_specs=[row, row, row, pl.BlockSpec((ROWS, d), lambda i: (i, 1))], out_specs=row,
        out_shape=jax.ShapeDtypeStruct((n, d), BF16),
        compiler_params=_cparams(("parallel",)),
    )(ya, yb, gate, gate)


def _merge_bwd(dm, ya, yb, gate):
    n, d = ya.shape

    def body(dm_ref, ya_ref, yb_ref, ga_ref, gb_ref, dya_ref, dyb_ref, dga_ref, dgb_ref):
        dmv = dm_ref[...]
        sa, sb = _sigmoid(ga_ref[...]), _sigmoid(gb_ref[...])
        dya_ref[...] = _bf(dmv * sa)
        dyb_ref[...] = _bf(dmv * sb)
        dga_ref[...] = _bf(dmv * ya_ref[...] * sa * (1.0 - sa))
        dgb_ref[...] = _bf(dmv * yb_ref[...] * sb * (1.0 - sb))

    row = pl.BlockSpec((ROWS, d), lambda i: (i, 0))
    o = jax.ShapeDtypeStruct((n, d), BF16)
    return pl.pallas_call(
        body, name="merge_bwd", grid=(n // ROWS,),
        in_specs=[row, row, row, row, pl.BlockSpec((ROWS, d), lambda i: (i, 1))], out_specs=[row] * 4,
        out_shape=[o] * 4,
        compiler_params=_cparams(("parallel",)),
    )(dm, ya, yb, gate, gate)


def _loss_bwd(out, target):
    n, d = out.shape

    def body(o_ref, t_ref, d_ref, d16_ref, l_ref):
        i = pl.program_id(0)
        err = o_ref[...] - t_ref[...]
        d_ref[...] = err * (1.0 / d)
        d16_ref[...] = _bf(err * (1.0 / d))

        @pl.when(i == 0)
        def _():
            l_ref[...] = jnp.zeros_like(l_ref)

        l_ref[...] += 0.5 * jnp.sum(jnp.mean(err * err, axis=1, keepdims=True), axis=0, keepdims=True)

    row = pl.BlockSpec((ROWS, d), lambda i: (i, 0))
    return pl.pallas_call(
        body, name="loss_bwd", grid=(n // ROWS,),
        in_specs=[row, row], out_specs=[row, row, pl.BlockSpec((8, LANES), lambda i: (0, 0))],
        out_shape=[jax.ShapeDtypeStruct((n, d), F32), jax.ShapeDtypeStruct((n, d), BF16),
                   jax.ShapeDtypeStruct((8, LANES), F32)],
        compiler_params=_cparams(("arbitrary",)),
    )(out, target)


PAD = 8


def _pad_zero(pad_ref):
    t = pad_ref.shape[0] - 2 * PAD
    pad_ref[0:PAD, :] = jnp.zeros((PAD, LANES), F32)
    pad_ref[PAD + t:2 * PAD + t, :] = jnp.zeros((PAD, LANES), F32)


def _shifted(pad_ref, s):
    t = pad_ref.shape[0] - 2 * PAD
    return pad_ref[PAD - s:PAD - s + t, :]


def _conv(x, w_ref, pad_ref):
    t = x.shape[0]
    pad_ref[PAD:PAD + t, :] = x
    y = _shifted(pad_ref, 3) * w_ref[0:1, :]
    y = y + _shifted(pad_ref, 2) * w_ref[1:2, :]
    y = y + _shifted(pad_ref, 1) * w_ref[2:3, :]
    return y + x * w_ref[3:4, :]


def _chunk_consts():
    r = lax.broadcasted_iota(jnp.int32, (CHUNK, CHUNK), 0)
    c = lax.broadcasted_iota(jnp.int32, (CHUNK, CHUNK), 1)
    incl, strict = r >= c, r > c
    return dict(incl=incl, strict=strict, trilf=incl.astype(F32), triuf=(r <= c).astype(F32),
                eye=(r == c).astype(F32))


class _V:
    def __init__(self, xs):
        self.xs = list(xs)

    def __add__(self, o):
        return _ap(lambda x, y: x + y, self, o)

    def __radd__(self, o):
        return _ap(lambda x, y: y + x, self, o)

    def __sub__(self, o):
        return _ap(lambda x, y: x - y, self, o)

    def __rsub__(self, o):
        return _ap(lambda x, y: y - x, self, o)

    def __mul__(self, o):
        return _ap(lambda x, y: x * y, self, o)

    def __rmul__(self, o):
        return _ap(lambda x, y: y * x, self, o)

    def __neg__(self):
        return _ap(lambda x: -x, self)

    def __getitem__(self, idx):
        return _ap(lambda x: x[idx], self)


def _ap(fn, *args):
    n = [len(a.xs) for a in args if isinstance(a, _V)]
    if not n:
        return fn(*args)
    return _V([fn(*[a.xs[i] if isinstance(a, _V) else a for a in args]) for i in range(n[0])])


def _vbf(x):
    return _ap(_bf, x)


def _vdot(a, b):
    return _ap(_dot, a, b)


def _vdot_nt(a, b):
    return _ap(_dot_nt, a, b)


def _vdot_tn(a, b):
    return _ap(_dot_tn, a, b)


def _vexp(x):
    return _ap(jnp.exp, x)


def _vsum(x, axis):
    return _ap(lambda v: jnp.sum(v, axis=axis, keepdims=True), x)


def _vcat(a, b, axis):
    return _ap(lambda x, y: jnp.concatenate([x, y], axis=axis), a, b)


def _vmask(mask, x):
    return _ap(lambda v: jnp.where(mask, v, 0.0), x)


def _split2(x):
    h = _vbf(x)
    return h, _vbf(x - _ap(lambda v: v.astype(F32), h))


def _dot3(a, b, kind=_vdot):
    ah, al = _split2(a)
    bh, bl = _split2(b)
    return kind(ah, bh) + (kind(ah, bl) + kind(al, bh))


def _split(x, terms):
    out = []
    for _ in range(terms):
        h = _vbf(x)
        out.append(h)
        x = x - _ap(lambda v: v.astype(F32), h)
    return out


def _dot_exact_l(m01, x, kind=_vdot, terms=2):
    mb = _bf(m01)
    parts = [kind(mb, xp) for xp in _split(x, terms)]
    return functools.reduce(lambda a, b: a + b, reversed(parts))


def _dot_exact_r(x, m01, kind=_vdot, terms=2):
    mb = _bf(m01)
    parts = [kind(xp, mb) for xp in _split(x, terms)]
    return functools.reduce(lambda a, b: a + b, reversed(parts))


def _inv_series(a, eye):
    m = eye.shape[0]
    levels = m.bit_length() - 1
    p = -a
    r = p + eye
    p = _dot3(p, p)
    for j in range(1, levels):
        if j < levels - 1:
            y = _dot3(p, _vcat(p, r, 1))
            p, r = y[:, 0:m], r + y[:, m:2 * m]
        else:
            r = r + _dot3(p, r)
    return r


def _inv_unit_lower(a, eye):
    return _inv_series(a, eye)


def _gdn_chunk_pre(q, k, v, g128, b128, cs):
    incl = cs["incl"]
    b64 = b128
    big_g = _dot_exact_l(cs["trilf"], g128)
    gc = big_g[:, 0:CHUNK]
    gr = _dot_exact_r(g128, cs["triuf"], _vdot_tn)
    decay = _ap(lambda d: jnp.where(incl, jnp.exp(jnp.where(incl, d, 0.0)), 0.0), gc - gr)
    kb, qb = _vbf(k), _vbf(q)
    qkk = _vdot_nt(_vcat(qb, kb, 0), kb)
    qk, kk = qkk[0:CHUNK], qkk[CHUNK:2 * CHUNK]
    tm = _inv_unit_lower(_vmask(cs["strict"], b64 * kk * decay), cs["eye"])
    e_g = _vexp(big_g)
    wu = _dot3(tm, _vcat(v * b128, k * (b128 * e_g), 1))
    w, u = wu[:, 0:DH], wu[:, DH:2 * DH]
    g_last = _vsum(g128, 0)
    return dict(big_g=big_g, decay=decay, kk=kk, qk=qk, tm=tm, w=w, u=u, p=qk * decay, q_dec=q * e_g,
                k_dec=k * _vexp(g_last - big_g), dec=_vexp(g_last))


def _gdn_chunk_post(q, k, v, g128, b128, s, ds_next, do, dv_new, big_g, decay, kk, qk, tm, u, v_new, cs):
    b64 = b128
    e_g = _vexp(big_g)
    vb = v * b128
    kbeta = k * (b128 * e_g)
    q_dec = q * e_g
    g_last = _vsum(g128, 0)
    ekg = _vexp(g_last - big_g)
    k_dec = k * ekg
    dec = _vexp(g_last)
    kb, qb, sb = _vbf(k), _vbf(q), _vbf(s)
    dob, dsb, vnb, dvnb = _vbf(do), _vbf(ds_next), _vbf(v_new), _vbf(dv_new)
    dp = _vmask(cs["incl"], _vdot_nt(dob, vnb))
    dq_dec = _vdot_nt(dob, sb)
    du = -_vdot_nt(dvnb, sb)
    ddec = _vsum(_vsum(s * ds_next, 1), 0)
    dk_dec = _vdot_nt(vnb, dsb)
    dwu = _vcat(dv_new, du, 1)
    dt = _dot3(dwu, _vcat(vb, kbeta, 1), _vdot_nt)
    dvk = _dot3(tm, dwu, _vdot_tn)
    dvb, dkbeta = dvk[:, 0:DH], dvk[:, DH:2 * DH]
    da = _vmask(cs["strict"], -_dot3(tm, _dot3(dt, tm, _vdot_nt), _vdot_tn))
    dkk = _vbf(da * b64 * decay)
    dqk = _vbf(dp * decay)
    ddd = (da * b64 * kk + dp * qk) * decay
    dq = _vdot(dqk, kb) + dq_dec * e_g
    dk = _vdot_tn(dqk, qb) + _vdot(dkk, kb) + _vdot_tn(dkk, kb) + dk_dec * ekg + dkbeta * (b128 * e_g)
    dv = dvb * b128
    dbeta = _vsum(da * kk * decay, 1) + _vsum(dvb * v, 1) + _vsum(dkbeta * k * e_g, 1)
    s_k = _vsum(dk_dec * k_dec, 1)
    dg_col = _vsum(ddd, 1) + _vsum(dq_dec * q_dec, 1) - s_k + _vsum(dkbeta * kbeta, 1)
    colsum = _dot_exact_r(ddd, jnp.ones((CHUNK, LANES), F32), _vdot_tn)
    dg_last = _vsum(s_k, 0) + ddec * dec
    dg = _dot_exact_l(cs["triuf"], dg_col - colsum) + dg_last
    return dq, dk, dv, dg, dbeta


def _stack_rows(vecs, nrows):
    row = lax.broadcasted_iota(jnp.int32, (nrows, LANES), 0)
    out = jnp.zeros((nrows, LANES), F32)
    for i, v in enumerate(vecs):
        out = out + jnp.where(row == i, jnp.broadcast_to(v, (nrows, LANES)), 0.0)
    return out


def _head_lane(x, lane_idx):
    lane = lax.broadcasted_iota(jnp.int32, x.shape, 1)
    return jnp.sum(jnp.where(lane == lane_idx, x, 0.0), axis=1, keepdims=True)


def _gdn_gates(ps, h, alog_ref, dtb_ref):
    ga = _head_lane(ps, LANE_GA + h)
    gb = _head_lane(ps, LANE_GB + h)
    a = jnp.exp(jnp.full((1, 1), alog_ref[0, h], F32))
    sp_in = ga + dtb_ref[0, h]
    g = -a * _softplus(sp_in)
    return g, _sigmoid(gb), a, sp_in


def _gdn_specs(b_loc, t):
    def col(off):
        return pl.BlockSpec((t, DH), lambda b, h: (b, off + h))

    ps_spec = pl.BlockSpec((t, LANES), lambda b, h: (b, 0))

    def wcol(off):
        return pl.BlockSpec((CONV_K, DH), lambda b, h: (0, off + h))

    smem = pl.BlockSpec(memory_space=pltpu.SMEM)
    vec = pl.BlockSpec((1, DH), lambda b, h: (0, 0))
    return col, ps_spec, wcol, smem, vec


def _gdn_fwd(pg, ps, convw, a_log, dt_bias, gnorm, b_loc, t):
    n = b_loc * t
    assert t % (CHUNK * GDN_GROUP) == 0 and CHUNK == LANES, (t, CHUNK, GDN_GROUP)
    nc = t // CHUNK
    col, ps_spec, wcol, smem, vec = _gdn_specs(b_loc, t)

    def body(q_ref, k_ref, v_ref, z_ref, ps_ref, wq_ref, wk_ref, wv_ref, alog_ref, dtb_ref, gn_ref,
             oa_ref, oraw_ref, s_ref, qn, kn, vv, g128, b128, uq_s, p_s, kd_s, dec_s, pad_s):
        h = pl.program_id(1)
        g, beta, _, _ = _gdn_gates(ps_ref[...], h, alog_ref, dtb_ref)
        g128[...] = jnp.broadcast_to(g, (t, LANES))
        b128[...] = jnp.broadcast_to(beta, (t, LANES))
        _pad_zero(pad_s)
        pq = _conv(q_ref[...], wq_ref, pad_s)
        yq = pq * _sigmoid(pq)
        qn[...] = yq * (lax.rsqrt(jnp.sum(yq * yq, axis=1, keepdims=True) + EPS) * (DH ** -0.5))
        pk = _conv(k_ref[...], wk_ref, pad_s)
        yk = pk * _sigmoid(pk)
        kn[...] = yk * lax.rsqrt(jnp.sum(yk * yk, axis=1, keepdims=True) + EPS)
        pv = _conv(v_ref[...], wv_ref, pad_s)
        vv[...] = pv * _sigmoid(pv)
        cs = _chunk_consts()

        def pre_group(gi, _):
            idx = [gi * GDN_GROUP + c for c in range(GDN_GROUP)]
            rows = [pl.ds(pl.multiple_of(i * CHUNK, CHUNK), CHUNK) for i in idx]
            ins = [_V([ref[r, :] for r in rows]) for ref in (qn, kn, vv, g128, b128)]
            f = _gdn_chunk_pre(*ins, cs)
            for c, (i, r) in enumerate(zip(idx, rows)):
                vv[r, :] = f["w"].xs[c]
                uq_s[i, 0:CHUNK, :] = _bf(f["u"].xs[c])
                uq_s[i, CHUNK:2 * CHUNK, :] = _bf(f["q_dec"].xs[c])
                p_s[r, :] = _bf(f["p"].xs[c])
                kd_s[r, :] = _bf(f["k_dec"].xs[c])
                dec_s[pl.ds(pl.multiple_of(i * 8, 8), 8), :] = jnp.broadcast_to(f["dec"].xs[c], (8, LANES))
            return 0

        lax.fori_loop(0, nc // GDN_GROUP, pre_group, 0)

        def chunk(i, s):
            r = pl.ds(pl.multiple_of(i * CHUNK, CHUNK), CHUNK)
            us = _dot(uq_s[i], _bf(s))
            vnb = _bf(vv[r, :] - us[0:CHUNK])
            oraw_ref[r, :] = us[CHUNK:2 * CHUNK] + _dot(p_s[r, :], vnb)
            s_ref[0, 0, i] = s
            return s * dec_s[pl.ds(pl.multiple_of(i * 8, 8), 1), :] + _dot_tn(kd_s[r, :], vnb)

        lax.fori_loop(0, nc, chunk, jnp.zeros((DH, DH), F32))
        o = oraw_ref[...]
        rr = lax.rsqrt(jnp.mean(o * o, axis=1, keepdims=True) + EPS)
        z = z_ref[...]
        oa_ref[...] = _bf((o * rr * gn_ref[...]) * (z * _sigmoid(z)))

    return pl.pallas_call(
        body, name="gdn_fwd", grid=(b_loc, HEADS),
        in_specs=[col(0), col(HEADS), col(2 * HEADS), col(3 * HEADS), ps_spec, wcol(0), wcol(HEADS), wcol(2 * HEADS),
                  smem, smem, vec],
        out_specs=[pl.BlockSpec((t, DH), lambda b, h: (b, h)), pl.BlockSpec((t, DH), lambda b, h: (b, h)),
                   pl.BlockSpec((1, 1, nc, DH, DH), lambda b, h: (b, h, 0, 0, 0))],
        out_shape=[jax.ShapeDtypeStruct((n, HEADS * DH), BF16), jax.ShapeDtypeStruct((n, HEADS * DH), F32),
                   jax.ShapeDtypeStruct((b_loc, HEADS, nc, DH, DH), F32)],
        scratch_shapes=([pltpu.VMEM((t, DH), F32)] * 3 + [pltpu.VMEM((t, LANES), F32)] * 2
                        + [pltpu.VMEM((nc, 2 * CHUNK, DH), BF16), pltpu.VMEM((t, CHUNK), BF16), pltpu.VMEM((t, DH), BF16),
                           pltpu.VMEM((8 * nc, LANES), F32), pltpu.VMEM((t + 2 * PAD, LANES), F32)]),
        compiler_params=_cparams(("arbitrary", "arbitrary")),
    )(pg, pg, pg, pg, ps, convw, convw, convw, a_log, dt_bias, gnorm)


def _gdn_bwd(pg, ps, convw, a_log, dt_bias, gnorm, d_oa, o_raw, s_all, b_loc, t):
    n = b_loc * t
    assert t % (CHUNK * GDN_GROUP) == 0 and CHUNK == LANES, (t, CHUNK, GDN_GROUP)
    nc = t // CHUNK
    col, ps_spec, wcol, smem, vec = _gdn_specs(b_loc, t)

    def body(q_ref, k_ref, v_ref, z_ref, ps_ref, wq_ref, wk_ref, wv_ref, alog_ref, dtb_ref, gn_ref,
             doa_ref, oraw_ref, s_ref,
             dq_ref, dk_ref, dv_ref, dz_ref, dps_ref, dcw_ref, dsm_ref,
             qn, kn, vv, g128, b128, do_s, bg_s, u_s, vn_s, dvn_s, dcy_s, kk_s, qk_s, tm_s, dsn_s, pad_s):
        b, h = pl.program_id(0), pl.program_id(1)
        g, beta, _, _ = _gdn_gates(ps_ref[...], h, alog_ref, dtb_ref)
        g128[...] = jnp.broadcast_to(g, (t, LANES))
        b128[...] = jnp.broadcast_to(beta, (t, LANES))
        _pad_zero(pad_s)

        def prep(x_ref, w_ref):
            p = _conv(x_ref[...], w_ref, pad_s)
            sg = _sigmoid(p)
            return p, sg, p * sg

        _, _, yq = prep(q_ref, wq_ref)
        qn[...] = yq * (lax.rsqrt(jnp.sum(yq * yq, axis=1, keepdims=True) + EPS) * (DH ** -0.5))
        _, _, yk = prep(k_ref, wk_ref)
        kn[...] = yk * lax.rsqrt(jnp.sum(yk * yk, axis=1, keepdims=True) + EPS)
        _, _, yv = prep(v_ref, wv_ref)
        vv[...] = yv

        o = oraw_ref[...]
        z = z_ref[...]
        doa = doa_ref[...]
        gn = gn_ref[...]
        ro = lax.rsqrt(jnp.mean(o * o, axis=1, keepdims=True) + EPS)
        sz = _sigmoid(z)
        dz_ref[...] = _bf(doa * (o * ro * gn) * (sz * (1.0 + z * (1.0 - sz))))
        dn = doa * (z * sz)
        dgn = jnp.sum(dn * o * ro, axis=0, keepdims=True)
        gy = dn * gn
        do_s[...] = ro * gy - o * (ro * ro * ro * (1.0 / DH)) * jnp.sum(gy * o, axis=1, keepdims=True)

        cs = _chunk_consts()

        def pre_group(gi, _):
            idx = [gi * GDN_GROUP + c for c in range(GDN_GROUP)]
            rows = [pl.ds(pl.multiple_of(i * CHUNK, CHUNK), CHUNK) for i in idx]
            ins = [_V([ref[r, :] for r in rows]) for ref in (qn, kn, vv, g128, b128)]
            states = _V([_bf(s_ref[0, 0, i]) for i in idx])
            f = _gdn_chunk_pre(*ins, cs)
            v_new = f["w"] - _vdot(_vbf(f["u"]), states)
            for c, r in enumerate(rows):
                bg_s[r, :] = f["big_g"].xs[c]
                u_s[r, :] = f["u"].xs[c]
                vn_s[r, :] = v_new.xs[c]
                dcy_s[r, :] = f["decay"].xs[c]
                kk_s[r, :] = f["kk"].xs[c]
                qk_s[r, :] = f["qk"].xs[c]
                tm_s[r, :] = f["tm"].xs[c]
            return 0

        lax.fori_loop(0, nc // GDN_GROUP, pre_group, 0)

        def chunk(j, ds):
            i = nc - 1 - j
            r = pl.ds(pl.multiple_of(i * CHUNK, CHUNK), CHUNK)
            big_g = bg_s[r, :]
            g_last = jnp.sum(g128[r, :], axis=0, keepdims=True)
            dob = _bf(do_s[r, :])
            dv_new = (_dot_tn(_bf(qk_s[r, :] * dcy_s[r, :]), dob)
                      + _dot(_bf(kn[r, :] * jnp.exp(g_last - big_g)), _bf(ds)))
            dvn_s[r, :] = dv_new
            dsn_s[i] = ds
            return (_dot_tn(_bf(qn[r, :] * jnp.exp(big_g)), dob) + jnp.exp(g_last) * ds
                    - _dot_tn(_bf(u_s[r, :]), _bf(dv_new)))

        lax.fori_loop(0, nc, chunk, jnp.zeros((DH, DH), F32))

        def post_group(gi, _):
            idx = [gi * GDN_GROUP + c for c in range(GDN_GROUP)]
            rows = [pl.ds(pl.multiple_of(i * CHUNK, CHUNK), CHUNK) for i in idx]
            def rows_of(ref):
                return _V([ref[r, :] for r in rows])

            dq, dk, dv, dg, dbeta = _gdn_chunk_post(
                rows_of(qn), rows_of(kn), rows_of(vv), rows_of(g128), rows_of(b128),
                _V([s_ref[0, 0, i] for i in idx]), _V([dsn_s[i] for i in idx]), rows_of(do_s), rows_of(dvn_s),
                rows_of(bg_s), rows_of(dcy_s), rows_of(kk_s), rows_of(qk_s), rows_of(tm_s), rows_of(u_s), rows_of(vn_s),
                cs)
            for c, r in enumerate(rows):
                qn[r, :] = dq.xs[c]
                kn[r, :] = dk.xs[c]
                vv[r, :] = dv.xs[c]
                g128[r, :] = dg.xs[c]
                b128[r, :] = jnp.broadcast_to(dbeta.xs[c], (CHUNK, LANES))
            return 0

        lax.fori_loop(0, nc // GDN_GROUP, post_group, 0)
        dqh, dkh, dvh = qn, kn, vv

        g, beta, a, sp_in = _gdn_gates(ps_ref[...], h, alog_ref, dtb_ref)
        dg = g128[...]
        d_ga = dg * (-a) * _sigmoid(sp_in)
        d_alog = jnp.sum(dg * g, axis=0, keepdims=True)
        d_dtb = jnp.sum(d_ga, axis=0, keepdims=True)
        d_gb = b128[...] * (beta * (1.0 - beta))
        lane = lax.broadcasted_iota(jnp.int32, (t, LANES), 1)
        contrib = jnp.where(lane == LANE_GA + h, d_ga, 0.0) + jnp.where(lane == LANE_GB + h, d_gb, 0.0)

        @pl.when(h == 0)
        def _():
            dps_ref[...] = jnp.zeros_like(dps_ref)

        dps_ref[...] += contrib

        lane1 = lax.broadcasted_iota(jnp.int32, (1, LANES), 1)
        small = _stack_rows([jnp.where(lane1 == h, d_alog, 0.0), jnp.where(lane1 == h, d_dtb, 0.0), dgn], 8)

        @pl.when((b == 0) & (h == 0))
        def _():
            dsm_ref[...] = jnp.zeros_like(dsm_ref)
            dcw_ref[...] = jnp.zeros_like(dcw_ref)

        dsm_ref[...] += small

        def conv_bwd(dp, x, w_ref, slot):
            dw = _stack_rows([jnp.sum(dp * _shifted(pad_s, 3), axis=0, keepdims=True),
                              jnp.sum(dp * _shifted(pad_s, 2), axis=0, keepdims=True),
                              jnp.sum(dp * _shifted(pad_s, 1), axis=0, keepdims=True),
                              jnp.sum(dp * x, axis=0, keepdims=True)], CONV_K)
            dcw_ref[slot] += dw
            pad_s[PAD:PAD + t, :] = dp
            dx = _shifted(pad_s, -3) * w_ref[0:1, :]
            dx = dx + _shifted(pad_s, -2) * w_ref[1:2, :]
            dx = dx + _shifted(pad_s, -1) * w_ref[2:3, :]
            return dx + dp * w_ref[3:4, :]

        def l2_bwd(dqn, y, c):
            r = lax.rsqrt(jnp.sum(y * y, axis=1, keepdims=True) + EPS)
            s1 = jnp.sum(dqn * y, axis=1, keepdims=True)
            return c * r * dqn - (c * r * r * r) * s1 * y

        def silu_bwd(p, sg):
            return sg * (1.0 + p * (1.0 - sg))

        pq, sq, yq = prep(q_ref, wq_ref)
        dq_ref[...] = _bf(conv_bwd(l2_bwd(dqh[...], yq, DH ** -0.5) * silu_bwd(pq, sq), q_ref[...], wq_ref, h))
        pk, sk, yk = prep(k_ref, wk_ref)
        dk_ref[...] = _bf(conv_bwd(l2_bwd(dkh[...], yk, 1.0) * silu_bwd(pk, sk), k_ref[...], wk_ref, HEADS + h))
        pv, sv, _ = prep(v_ref, wv_ref)
        dv_ref[...] = _bf(conv_bwd(dvh[...] * silu_bwd(pv, sv), v_ref[...], wv_ref, 2 * HEADS + h))

    blk = pl.BlockSpec((t, DH), lambda b, h: (b, h))
    ob = jax.ShapeDtypeStruct((n, HEADS * DH), BF16)
    return pl.pallas_call(
        body, name="gdn_bwd", grid=(b_loc, HEADS),
        in_specs=[col(0), col(HEADS), col(2 * HEADS), col(3 * HEADS), ps_spec, wcol(0), wcol(HEADS), wcol(2 * HEADS),
                  smem, smem, vec, blk, blk, pl.BlockSpec((1, 1, nc, DH, DH), lambda b, h: (b, h, 0, 0, 0))],
        out_specs=[blk, blk, blk, blk, ps_spec,
                   pl.BlockSpec((3 * HEADS, CONV_K, DH), lambda b, h: (0, 0, 0)),
                   pl.BlockSpec((8, LANES), lambda b, h: (0, 0))],
        out_shape=[ob, ob, ob, ob, jax.ShapeDtypeStruct((n, LANES), F32),
                   jax.ShapeDtypeStruct((3 * HEADS, CONV_K, DH), F32), jax.ShapeDtypeStruct((8, LANES), F32)],
        scratch_shapes=([pltpu.VMEM((t, DH), F32)] * 3 + [pltpu.VMEM((t, LANES), F32)] * 2
                        + [pltpu.VMEM((t, DH), F32)] * 5 + [pltpu.VMEM((t, CHUNK), F32)] * 4
                        + [pltpu.VMEM((nc, DH, DH), F32), pltpu.VMEM((t + 2 * PAD, LANES), F32)]),
        compiler_params=_cparams(("arbitrary", "arbitrary")),
    )(pg, pg, pg, pg, ps, convw, convw, convw, a_log, dt_bias, gnorm, d_oa, o_raw, s_all)


def _fox_prologue(q_ref, k_ref, v_ref, ps_ref, fb_ref, gq_ref, gk_ref, h, t, qs, ks, vs, ccol, crow):
    nb = t // FOX_BLOCK
    q, k = q_ref[...], k_ref[...]
    rq = lax.rsqrt(jnp.mean(q * q, axis=1, keepdims=True) + EPS)
    rk = lax.rsqrt(jnp.mean(k * k, axis=1, keepdims=True) + EPS)
    qs[...] = _bf(q * rq * gq_ref[...])
    ks[...] = _bf(k * rk * gk_ref[...])
    vs[...] = _bf(v_ref[...])
    f_in = _head_lane(ps_ref[...], LANE_FF + h) + fb_ref[0, h]
    ccol[...] = jnp.broadcast_to(-_softplus(-f_in), (t, LANES))
    r = lax.broadcasted_iota(jnp.int32, (FOX_BLOCK, FOX_BLOCK), 0)
    c = lax.broadcasted_iota(jnp.int32, (FOX_BLOCK, FOX_BLOCK), 1)
    trilf, triuf = (r >= c).astype(F32), (r <= c).astype(F32)
    blocks = [pl.ds(j * FOX_BLOCK, FOX_BLOCK) for j in range(nb)]
    lfs = _V([ccol[rb, :] for rb in blocks])
    cc = _dot_exact_l(trilf, lfs, terms=3)
    cr = _dot_exact_r(lfs, triuf, _vdot_tn, terms=3)
    sums = _vsum(lfs, 0)
    carry = jnp.zeros((1, LANES), F32)
    for j, rb in enumerate(blocks):
        ccol[rb, :] = cc.xs[j] + carry
        crow[j] = (cr.xs[j] + carry)[0:8]
        carry = carry + sums.xs[j]
    return rq, rk, f_in


def _fox_scores(q_rows, k_rows, cc, cr, row0, col0):
    s = _dot_nt(q_rows, k_rows) * (DH ** -0.5) + cc - cr
    r = lax.broadcasted_iota(jnp.int32, s.shape, 0)
    c = lax.broadcasted_iota(jnp.int32, s.shape, 1)
    return jnp.where(row0 + r >= col0 + c, s, NEG)


def _fox_specs(t):
    def col(off):
        return pl.BlockSpec((t, DH), lambda b, h: (b, off + h))

    ps_spec = pl.BlockSpec((t, LANES), lambda b, h: (b, 0))
    smem = pl.BlockSpec(memory_space=pltpu.SMEM)
    vec = pl.BlockSpec((1, DH), lambda b, h: (0, 0))
    blk = pl.BlockSpec((t, DH), lambda b, h: (b, h))
    return col, ps_spec, smem, vec, blk


def _fox_fwd(pf, ps, f_bias, gq, gk, b_loc, t):
    n = b_loc * t
    nb = t // FOX_BLOCK
    assert t % FOX_TILE == 0 and FOX_TILE % FOX_SHORT == 0, (t, FOX_TILE, FOX_SHORT)
    kt = FOX_TILE
    nsub = kt // FOX_BLOCK
    col, ps_spec, smem, vec, blk = _fox_specs(t)

    def body(q_ref, k_ref, v_ref, ps_ref, fb_ref, gq_ref, gk_ref, o_ref, lse_ref, qs, ks, vs, ccol, crow):
        h = pl.program_id(1)
        _fox_prologue(q_ref, k_ref, v_ref, ps_ref, fb_ref, gq_ref, gk_ref, h, t, qs, ks, vs, ccol, crow)

        def qblock(i, _):
            ri = pl.ds(pl.multiple_of(i * FOX_SHORT, FOX_SHORT), FOX_SHORT)
            qi = qs[ri, :]
            cc = jnp.concatenate([ccol[ri, :]] * nsub, axis=1)

            def ktile(j, carry):
                m, l, acc = carry
                rj = pl.ds(pl.multiple_of(j * kt, kt), kt)
                cr = jnp.concatenate([crow[j * nsub + u, 0:1, :] for u in range(nsub)], axis=1)
                s = _fox_scores(qi, ks[rj, :], cc, cr, i * FOX_SHORT, j * kt)
                m_new = jnp.maximum(m, jnp.max(s, axis=1, keepdims=True))
                p = jnp.exp(s - m_new)
                alpha = jnp.exp(m - m_new)
                l = alpha * l + jnp.sum(p, axis=1, keepdims=True)
                acc = alpha * acc + _dot(_bf(p), vs[rj, :])
                return m_new, l, acc

            m, l, acc = lax.fori_loop(0, (i * FOX_SHORT) // kt + 1, ktile, (jnp.full((FOX_SHORT, 1), NEG, F32),
                                                                            jnp.zeros((FOX_SHORT, 1), F32),
                                                                            jnp.zeros((FOX_SHORT, DH), F32)))
            o_ref[ri, :] = acc / l
            lse_ref[ri, :] = jnp.broadcast_to(m + jnp.log(l), (FOX_SHORT, LANES))
            return 0

        lax.fori_loop(0, t // FOX_SHORT, qblock, 0)

    o = jax.ShapeDtypeStruct((n, HEADS * DH), F32)
    return pl.pallas_call(
        body, name="fox_fwd", grid=(b_loc, HEADS),
        in_specs=[col(0), col(HEADS), col(2 * HEADS), ps_spec, smem, vec, vec],
        out_specs=[blk, blk], out_shape=[o, o],
        scratch_shapes=[pltpu.VMEM((t, DH), BF16)] * 3 + [pltpu.VMEM((t, LANES), F32), pltpu.VMEM((nb, 8, LANES), F32)],
        compiler_params=_cparams(("arbitrary", "arbitrary")),
    )(pf, pf, pf, ps, f_bias, gq, gk)


def _fox_bwd(pf, ps, f_bias, gq, gk, d_ob, ob, lse, dps_in, b_loc, t):
    n = b_loc * t
    nb = t // FOX_BLOCK
    assert t % FOX_TILE == 0 and FOX_TILE % FOX_SHORT == 0, (t, FOX_TILE, FOX_SHORT)
    qt = FOX_TILE
    scale = DH ** -0.5
    col, ps_spec, smem, vec, blk = _fox_specs(t)

    def body(q_ref, k_ref, v_ref, ps_ref, fb_ref, gq_ref, gk_ref, do_ref, o_ref, lse_ref, dpsi_ref,
             dq_ref, dk_ref, dv_ref, dps_ref, dsm_ref, qs, ks, vs, ccol, crow, dos, dl, dqa, dcr, dcq):
        b, h = pl.program_id(0), pl.program_id(1)
        rq, _, f_in = _fox_prologue(q_ref, k_ref, v_ref, ps_ref, fb_ref, gq_ref, gk_ref, h, t, qs, ks, vs, ccol, crow)
        dov = do_ref[...]
        dos[...] = _bf(dov)
        dl[...] = jnp.broadcast_to(jnp.sum(dov * o_ref[...], axis=1, keepdims=True), (t, LANES))
        dqa[...] = jnp.zeros_like(dqa)
        dcq[...] = jnp.zeros_like(dcq)
        gkv = gk_ref[...]

        ksub = FOX_SHORT // FOX_BLOCK

        def kblock(j, dgk):
            rj = pl.ds(pl.multiple_of(j * FOX_SHORT, FOX_SHORT), FOX_SHORT)
            kj, vj = ks[rj, :], vs[rj, :]
            cr = jnp.concatenate([crow[j * ksub + u, 0:1, :] for u in range(ksub)], axis=1)

            def wide(x):
                return jnp.concatenate([x] * ksub, axis=1)

            def qtile(i, carry):
                dk_acc, dv_acc, dc = carry
                ri = pl.ds(pl.multiple_of(i * qt, qt), qt)
                qi, doi = qs[ri, :], dos[ri, :]
                s = _fox_scores(qi, kj, wide(ccol[ri, :]), cr, i * qt, j * FOX_SHORT)
                p = jnp.exp(s - wide(lse_ref[ri, :]))
                ds = p * (_dot_nt(doi, vj) - wide(dl[ri, :]))
                dsb = _bf(ds)
                dqa[ri, :] += _dot(dsb, kj)
                dcq[ri, :] += jnp.broadcast_to(jnp.sum(ds, axis=1, keepdims=True), (qt, LANES))
                return (dk_acc + _dot_tn(dsb, qi), dv_acc + _dot_tn(_bf(p), doi),
                        dc - jnp.sum(ds, axis=0, keepdims=True))

            z = jnp.zeros((FOX_SHORT, DH), F32)
            dk_acc, dv_acc, dc = lax.fori_loop((j * FOX_SHORT) // qt, t // qt, qtile,
                                               (z, z, jnp.zeros((1, FOX_SHORT), F32)))
            dv_ref[rj, :] = _bf(dv_acc)
            for u in range(ksub):
                dcr[pl.ds(pl.multiple_of((j * ksub + u) * 8, 8), 8), :] = jnp.broadcast_to(
                    dc[:, u * FOX_BLOCK:(u + 1) * FOX_BLOCK], (8, LANES))
            kraw = k_ref[rj, :]
            rk = lax.rsqrt(jnp.mean(kraw * kraw, axis=1, keepdims=True) + EPS)
            dkn = dk_acc * scale
            gy = dkn * gkv
            dk_ref[rj, :] = _bf(rk * gy - kraw * (rk * rk * rk * (1.0 / DH)) * jnp.sum(gy * kraw, axis=1, keepdims=True))
            return dgk + jnp.sum(dkn * kraw * rk, axis=0, keepdims=True)

        dgk = lax.fori_loop(0, t // FOX_SHORT, kblock, jnp.zeros((1, DH), F32))

        q = q_ref[...]
        dqn = dqa[...] * scale
        gy = dqn * gq_ref[...]
        dq_ref[...] = _bf(rq * gy - q * (rq * rq * rq * (1.0 / DH)) * jnp.sum(gy * q, axis=1, keepdims=True))
        dgq = jnp.sum(dqn * q * rq, axis=0, keepdims=True)

        r = lax.broadcasted_iota(jnp.int32, (FOX_BLOCK, FOX_BLOCK), 0)
        c = lax.broadcasted_iota(jnp.int32, (FOX_BLOCK, FOX_BLOCK), 1)
        triuf = (r <= c).astype(F32)

        def rev(jj, carry):
            j = nb - 1 - jj
            rows = pl.ds(pl.multiple_of(j * FOX_BLOCK, FOX_BLOCK), FOX_BLOCK)
            rowv = dcr[pl.ds(pl.multiple_of(j * 8, 8), 1), :]
            colv = jnp.sum(jnp.where(c >= r, jnp.broadcast_to(rowv, (FOX_BLOCK, LANES)), 0.0), axis=1, keepdims=True)
            qcol = dcq[rows, :]
            dl[rows, :] = colv + _dot_exact_l(triuf, qcol, terms=3) + carry
            return carry + jnp.sum(rowv, axis=1, keepdims=True) + jnp.sum(qcol, axis=0, keepdims=True)

        lax.fori_loop(0, nb, rev, jnp.zeros((1, LANES), F32))
        d_ff = dl[...] * _sigmoid(-f_in)
        lane = lax.broadcasted_iota(jnp.int32, (t, LANES), 1)

        @pl.when(h == 0)
        def _():
            dps_ref[...] = dpsi_ref[...]

        dps_ref[...] += jnp.where(lane == LANE_FF + h, d_ff, 0.0)

        lane1 = lax.broadcasted_iota(jnp.int32, (1, LANES), 1)
        d_fb = jnp.sum(d_ff, axis=0, keepdims=True)
        small = _stack_rows([dgq, dgk, jnp.where(lane1 == h, d_fb, 0.0)], 8)

        @pl.when((b == 0) & (h == 0))
        def _():
            dsm_ref[...] = jnp.zeros_like(dsm_ref)

        dsm_ref[...] += small

    ob_ = jax.ShapeDtypeStruct((n, HEADS * DH), BF16)
    return pl.pallas_call(
        body, name="fox_bwd", grid=(b_loc, HEADS),
        in_specs=[col(0), col(HEADS), col(2 * HEADS), ps_spec, smem, vec, vec, blk, blk, blk, ps_spec],
        out_specs=[blk, blk, blk, ps_spec, pl.BlockSpec((8, LANES), lambda b, h: (0, 0))],
        out_shape=[ob_, ob_, ob_, jax.ShapeDtypeStruct((n, LANES), F32), jax.ShapeDtypeStruct((8, LANES), F32)],
        scratch_shapes=([pltpu.VMEM((t, DH), BF16)] * 3 + [pltpu.VMEM((t, LANES), F32), pltpu.VMEM((nb, 8, LANES), F32)]
                        + [pltpu.VMEM((t, DH), BF16), pltpu.VMEM((t, LANES), F32), pltpu.VMEM((t, DH), F32),
                           pltpu.VMEM((8 * nb, LANES), F32), pltpu.VMEM((t, LANES), F32)]),
        compiler_params=_cparams(("arbitrary", "arbitrary")),
    )(pf, pf, pf, ps, f_bias, gq, gk, d_ob, ob, lse, dps_in)


class _NoExchange:
    def late_weights(self, after):
        return {}

    def grads_ready(self, grads, tie):
        return tie


def _local_step(x, target, w, b_loc, t, comm=None):
    comm = comm or _NoExchange()
    w = dict(w)
    xf = x
    u = _rms_fwd(xf, w["norm_mix_g"], "rms_mix")
    pg = _mm(u, w["w_gdn"], name="proj_gdn")
    pf = _mm(u, w["w_fox"], name="proj_fox")
    pgate = _mm(u, w["w_gate"], name="proj_gate")
    ps = _mm(u, w["w_small"], name="proj_small")
    oa, o_raw, s_all = _gdn_fwd(pg, ps, w["conv_w"], w["a_log"], w["dt_bias"], w["gdn_norm_g"], b_loc, t)
    ob, lse = _fox_fwd(pf, ps, w["f_bias"], w["fox_q_norm_g"], w["fox_k_norm_g"], b_loc, t)
    w.update(comm.late_weights(ob))
    ya = _mm(oa, w["w_proj_gdn"], name="proj_a")
    yb = _mm(ob, w["w_proj_fox"], name="proj_b")
    merged = _merge_fwd(ya, yb, pgate)
    h = _mm(merged, w["w_out"], name="proj_out", epi=lambda acc, xr: acc + xr, extras=(xf,))
    hn = _rms_fwd(h, w["norm_mlp_g"], "rms_mlp")
    up, act = _mm(hn, w["w_up"], name="mlp_up", out_dtype=BF16, out2=(_relu2, BF16))
    out = _mm(act, w["w_down"], name="mlp_down", epi=lambda acc, hr: acc + hr, extras=(h,))
    d_out, d_out16, loss_blk = _loss_bwd(out, target)

    g = {}
    g["w_down"] = _mm(act, d_out16, name="dw_down", ta=True, out_dtype=BF16)
    d_up = _mm(d_out16, w["w_down"], name="d_up", tb=True, out_dtype=BF16,
               epi=lambda acc, upr: acc * (2.0 * jnp.maximum(upr.astype(F32), 0.0)), extras=(up,))
    g["w_up"] = _mm(hn, d_up, name="dw_up", ta=True, out_dtype=BF16)
    mlp_gain = comm.grads_ready({"w_down": g["w_down"], "w_up": g["w_up"]}, w["norm_mlp_g"])
    d_hn = _mm(d_up, w["w_up"], name="d_hn", tb=True)
    dh, dh16, g["norm_mlp_g"] = _rms_bwd(d_hn, h, mlp_gain, d_out, "rms_mlp_bwd")
    g["w_out"] = _mm(merged, dh16, name="dw_out", ta=True, out_dtype=BF16)
    dm = _mm(dh16, w["w_out"], name="d_merged", tb=True)
    dya, dyb, dgate_a, dgate_b = _merge_bwd(dm, ya, yb, pgate)
    g["w_proj_gdn"] = _mm(oa, dya, name="dw_proj_a", ta=True, out_dtype=BF16)
    g["w_proj_fox"] = _mm(ob, dyb, name="dw_proj_b", ta=True, out_dtype=BF16)
    gdn_gain = comm.grads_ready({"w_out": g["w_out"], "w_proj_gdn": g["w_proj_gdn"], "w_proj_fox": g["w_proj_fox"]},
                                w["gdn_norm_g"])
    d_oa = _mm(dya, w["w_proj_gdn"], name="d_oa", tb=True)
    d_ob = _mm(dyb, w["w_proj_fox"], name="d_ob", tb=True)
    dgq, dgk, dgv, dgz, dps, dcw, gdn_small = _gdn_bwd(pg, ps, w["conv_w"], w["a_log"], w["dt_bias"], gdn_gain,
                                                       d_oa, o_raw, s_all, b_loc, t)
    dfq, dfk, dfv, dps, fox_small = _fox_bwd(pf, ps, w["f_bias"], w["fox_q_norm_g"], w["fox_k_norm_g"],
                                             d_ob, ob, lse, dps, b_loc, t)
    segs = [(dgq, "w_gdn", 0), (dgk, "w_gdn", 1024), (dgv, "w_gdn", 2048), (dgz, "w_gdn", 3072),
            (dfq, "w_fox", 0), (dfk, "w_fox", 1024), (dfv, "w_fox", 2048),
            (dgate_a, "w_gate", 0), (dgate_b, "w_gate", 1024)]
    dws = [_mm(u, dps, name="dw_small", ta=True, out_dtype=BF16)]
    dws += [_mm(u, dseg, name=f"dw_in_{idx}", ta=True, out_dtype=BF16) for idx, (dseg, _, _) in enumerate(segs)]
    g["w_in_parts"] = dws
    mix_gain = comm.grads_ready({"w_in_parts": dws}, w["norm_mix_g"])
    du = _du_all(dps, w["w_small"], segs, w)
    grad_x, _, g["norm_mix_g"] = _rms_bwd(du, xf, mix_gain, dh, "rms_mix_bwd")
    g["conv"] = dcw
    g["gdn_small"] = gdn_small
    g["fox_small"] = fox_small
    return loss_blk, grad_x, g


def _position():
    x, y, c = lax.axis_index("x"), lax.axis_index("y"), lax.axis_index("c")
    return x, y, c


def _to_bf16(arrs, name):
    n = len(arrs)

    def body(*refs):
        for i in range(n):
            refs[n + i][...] = _bf(refs[i][...])

    return pl.pallas_call(
        body, name=name,
        out_shape=[jax.ShapeDtypeStruct(a.shape, BF16) for a in arrs],
        compiler_params=_cparams(),
    )(*arrs)


def _all_gather(arrs, name):
    n = len(arrs)
    hbm = pl.BlockSpec(memory_space=pl.ANY)

    def body(*refs):
        ins, outs = refs[:n], refs[n:2 * n]
        send, recv, loc = refs[2 * n:]
        x, y, c = _position()
        me = 4 * x + 2 * y + c
        sibling = (x, y, 1 - c)
        chips = [(1 - x, y), (x, 1 - y), (1 - x, 1 - y)]

        def idx(px, py, pc):
            return 4 * px + 2 * py + pc

        def cp(a, k, block, to, src=None):
            return pltpu.make_async_remote_copy(
                src_ref=outs[a].at[block] if src is None else src, dst_ref=outs[a].at[block],
                send_sem=send.at[a, k], recv_sem=recv.at[a, k], device_id=to, device_id_type=MESH)

        mine = [pltpu.make_async_copy(ins[a], outs[a].at[me], loc.at[a]) for a in range(n)]
        for m in mine:
            m.start()
        first = []
        for a in range(n):
            first.append(cp(a, 0, me, sibling, src=ins[a]))
            first += [cp(a, 1 + j, me, (*chip, c), src=ins[a]) for j, chip in enumerate(chips)]
        for f in first:
            f.start()
        passed = []
        for j, chip in enumerate(chips):
            for a in range(n):
                cp(a, 1 + j, idx(*chip, c), (x, y, c)).wait_recv()
                p = cp(a, 4 + j, idx(*chip, c), sibling)
                p.start()
                passed.append(p)
        for a in range(n):
            cp(a, 0, idx(x, y, 1 - c), (x, y, c)).wait_recv()
            for j, chip in enumerate(chips):
                cp(a, 4 + j, idx(*chip, 1 - c), (x, y, c)).wait_recv()
        for f in first + passed:
            f.wait_send()
        for m in mine:
            m.wait()

    return pl.pallas_call(
        body, name=name,
        in_specs=[hbm] * n, out_specs=[hbm] * n,
        out_shape=[jax.ShapeDtypeStruct((N_DEV,) + a.shape, a.dtype) for a in arrs],
        scratch_shapes=[pltpu.SemaphoreType.DMA((n, 7)), pltpu.SemaphoreType.DMA((n, 7)), pltpu.SemaphoreType.DMA((n,))],
        compiler_params=pltpu.CompilerParams(has_side_effects=True),
    )(*arrs)


def _peer(x, y, c, rel):
    return ((1 - x) if rel & 4 else x, (1 - y) if rel & 2 else y, (1 - c) if rel & 1 else c)


HBM_SPEC = pl.BlockSpec(memory_space=pltpu.HBM)
SEM_SPEC = pl.BlockSpec(memory_space=pltpu.SEMAPHORE)
DATAFLOW = pltpu.SideEffectType.DATAFLOW_SIDE_EFFECTING


CHIP_RELS = (2, 4, 6)


def _push_start(arrs, slots, name, chips=False):
    n = len(arrs)
    n_slots = 4 if chips else N_DEV
    rels = CHIP_RELS if chips else tuple(range(1, N_DEV))
    land_shapes = [a.shape if slots else (n_slots,) + a.shape for a in arrs]

    def body(*refs):
        ins, lands, sends, recvs, token = refs[:n], refs[n:2 * n], refs[2 * n:3 * n], refs[3 * n:4 * n], refs[-1]
        x, y, c = _position()
        for rel in rels:
            px, py, pc = _peer(x, y, c, rel)
            mine, theirs = (2 * x + y, 2 * px + py) if chips else (4 * x + 2 * y + c, 4 * px + 2 * py + pc)
            for a in range(n):
                pltpu.make_async_remote_copy(
                    src_ref=ins[a].at[theirs] if slots else ins[a], dst_ref=lands[a].at[mine],
                    send_sem=sends[a], recv_sem=recvs[a], device_id=(px, py, pc), device_id_type=MESH).start()
        token[...] = jnp.zeros_like(token)

    sem = pltpu.SemaphoreType.DMA(())
    outs = pl.pallas_call(
        body, name=name,
        in_specs=[HBM_SPEC] * (2 * n),
        out_shape=(*[sem] * (2 * n), *[pltpu.HBM(a.shape, a.dtype) for a in arrs],
                   *[pltpu.HBM(s, a.dtype) for s, a in zip(land_shapes, arrs)], jax.ShapeDtypeStruct((8, LANES), F32)),
        out_specs=(*[SEM_SPEC] * (2 * n), *[HBM_SPEC] * (2 * n), pl.BlockSpec(memory_space=pltpu.VMEM)),
        input_output_aliases={i: 2 * n + i for i in range(2 * n)},
        compiler_params=pltpu.CompilerParams(has_side_effects=DATAFLOW),
    )(*[pltpu.with_memory_space_constraint(a, pltpu.HBM) for a in arrs],
      *[pltpu.with_memory_space_constraint(lax.empty(s, a.dtype), pltpu.HBM) for s, a in zip(land_shapes, arrs)])
    return dict(sends=list(outs[:n]), recvs=list(outs[n:2 * n]), ins=list(outs[2 * n:3 * n]),
                lands=list(outs[3 * n:4 * n]), token=outs[-1], copies=len(rels))


def _push_wait(started, after, name):
    n = len(started["ins"])
    copies = started["copies"]

    def body(*refs):
        lands, sends, recvs = refs[n:2 * n], refs[2 * n:3 * n], refs[3 * n:4 * n]
        x, y, c = _position()
        for a in range(n):
            every = lands[a].at[pl.ds(0, copies)]
            drain = pltpu.make_async_remote_copy(src_ref=every, dst_ref=every, send_sem=sends[a], recv_sem=recvs[a],
                                                 device_id=(x, y, c), device_id_type=MESH)
            drain.wait_send()
            drain.wait_recv()

    both = started["ins"] + started["lands"]
    outs = pl.pallas_call(
        body, name=name,
        in_specs=[HBM_SPEC] * (2 * n) + [SEM_SPEC] * (2 * n) + [pl.BlockSpec(memory_space=pl.ANY)],
        out_shape=tuple(pltpu.HBM(a.shape, a.dtype) for a in both), out_specs=tuple([HBM_SPEC] * (2 * n)),
        input_output_aliases={i: i for i in range(2 * n)},
        compiler_params=pltpu.CompilerParams(has_side_effects=DATAFLOW),
    )(*both, *started["sends"], *started["recvs"], after)
    return list(outs[:n]), list(outs[n:])


def _sibling_swap(arr, name):
    chips = N_DEV // 2

    def body(in_ref, out_ref, send, recv):
        x, y, c = _position()
        for s in range(chips):
            pltpu.make_async_remote_copy(src_ref=in_ref.at[2 * s + 1 - c], dst_ref=out_ref.at[s], send_sem=send,
                                         recv_sem=recv, device_id=(x, y, 1 - c), device_id_type=MESH).start()
        pltpu.make_async_remote_copy(src_ref=out_ref, dst_ref=out_ref, send_sem=send, recv_sem=recv,
                                     device_id=(x, y, 1 - c), device_id_type=MESH).wait()

    hbm = pl.BlockSpec(memory_space=pl.ANY)
    return pl.pallas_call(
        body, name=name, in_specs=[hbm], out_specs=hbm,
        out_shape=jax.ShapeDtypeStruct((chips,) + arr.shape[1:], arr.dtype),
        scratch_shapes=[pltpu.SemaphoreType.DMA, pltpu.SemaphoreType.DMA],
        compiler_params=pltpu.CompilerParams(has_side_effects=True),
    )(arr)


def _add_halves(core, arr, other, name):
    ns, r, c = other.shape
    tr = min(r, 256)

    def body(core_ref, a_ref, o_ref, out_ref):
        out_ref[...] = _bf(a_ref[...].astype(F32) + o_ref[...].astype(F32))

    blk = pl.BlockSpec((1, tr, c), lambda s, i, core_ref: (s, i, 0))
    return pl.pallas_call(
        body, name=name,
        grid_spec=pltpu.PrefetchScalarGridSpec(
            num_scalar_prefetch=1, grid=(ns, r // tr),
            in_specs=[pl.BlockSpec((1, tr, c), lambda s, i, core_ref: (2 * s + core_ref[0], i, 0)), blk],
            out_specs=blk),
        out_shape=jax.ShapeDtypeStruct((ns, r, c), BF16),
        compiler_params=_cparams(("parallel", "parallel")),
    )(core, arr, other)


def _all_reduce_small(buf, name):
    rows = buf.shape[0]

    def body(in_ref, out_ref, slots, send, recv):
        x, y, c = _position()
        me = 4 * x + 2 * y + c
        slots[me] = in_ref[...]
        copies = []
        for rel in range(1, N_DEV):
            copies.append(pltpu.make_async_remote_copy(
                src_ref=in_ref, dst_ref=slots.at[me], send_sem=send.at[rel - 1], recv_sem=recv.at[rel - 1],
                device_id=_peer(x, y, c, rel), device_id_type=MESH))
        for cpy in copies:
            cpy.start()
        for cpy in copies:
            cpy.wait()
        tot = slots[0]
        for d in range(1, N_DEV):
            tot = tot + slots[d]
        out_ref[...] = tot

    return pl.pallas_call(
        body, name=name,
        out_shape=jax.ShapeDtypeStruct((rows, LANES), F32),
        in_specs=[pl.BlockSpec(memory_space=pltpu.VMEM)], out_specs=pl.BlockSpec(memory_space=pltpu.VMEM),
        scratch_shapes=[pltpu.VMEM((N_DEV, rows, LANES), F32), pltpu.SemaphoreType.DMA((7,)),
                        pltpu.SemaphoreType.DMA((7,))],
        compiler_params=pltpu.CompilerParams(has_side_effects=True),
    )(buf)


def _adam_math(g, w, m, v):
    m = ADAM_B1 * m + (1.0 - ADAM_B1) * g
    v = ADAM_B2 * v + (1.0 - ADAM_B2) * (g * g)
    m_hat = m / (1.0 - ADAM_B1 ** ADAM_STEP)
    v_hat = v / (1.0 - ADAM_B2 ** ADAM_STEP)
    delta = -ADAM_LR * (m_hat / (jnp.sqrt(v_hat) + ADAM_EPS) + ADAM_WD * w)
    return delta, m, v


def _adam_shard(me, parts, mine, w, m, v, name):
    r, c = w.shape
    tr = min(r, 128)
    n_slots = parts.shape[0]

    def body(me_ref, p_ref, own_ref, w_ref, m_ref, v_ref, g_ref, d_ref, nm_ref, nv_ref):
        own = own_ref[0].astype(F32)
        g = None
        for s in range(n_slots):
            term = jnp.where(me_ref[0] == s, own, p_ref[s].astype(F32))
            g = term if g is None else g + term
        d, nm, nv = _adam_math(g, w_ref[...], m_ref[...], v_ref[...])
        g_ref[...] = g
        d_ref[...] = d
        nm_ref[...] = nm
        nv_ref[...] = nv

    row = pl.BlockSpec((tr, c), lambda i, me_ref: (i, 0))
    o = jax.ShapeDtypeStruct((r, c), F32)
    return pl.pallas_call(
        body, name=name,
        grid_spec=pltpu.PrefetchScalarGridSpec(
            num_scalar_prefetch=1, grid=(r // tr,),
            in_specs=[pl.BlockSpec((n_slots, tr, c), lambda i, me_ref: (0, i, 0)),
                      pl.BlockSpec((1, tr, c), lambda i, me_ref: (me_ref[0], i, 0)), row, row, row],
            out_specs=[row] * 4),
        out_shape=[o] * 4,
        compiler_params=_cparams(("parallel",)),
    )(me, parts, mine, w, m, v)


def _adam_small(g, w, m, v):
    def body(g_ref, w_ref, m_ref, v_ref, d_ref, nm_ref, nv_ref):
        d, nm, nv = _adam_math(g_ref[...], w_ref[...], m_ref[...], v_ref[...])
        d_ref[...] = d
        nm_ref[...] = nm
        nv_ref[...] = nv

    o = jax.ShapeDtypeStruct(g.shape, F32)
    return pl.pallas_call(body, name="adam_small", out_shape=[o] * 3, compiler_params=_cparams())(g, w, m, v)


def _split_w_in(w_full):
    o = IN_OFF
    w_gdn = w_full[:, o["gq"]:o["ga"]]
    w_fox = w_full[:, o["fq"]:o["ff"]]
    w_gate = w_full[:, o["gate_a"]:o["end"]]
    w_small = jnp.concatenate([w_full[:, o["ga"]:o["fq"]], w_full[:, o["ff"]:o["gate_a"]],
                               jnp.zeros((w_full.shape[0], LANES - 24), w_full.dtype)], axis=1)
    return w_gdn, w_fox, w_gate, w_small


def _w_in_pieces(g_in):
    nd, d, c = g_in.shape
    tr = 128
    widths = (IN_OFF["ga"] - IN_OFF["gq"], IN_OFF["ff"] - IN_OFF["fq"], IN_OFF["end"] - IN_OFF["gate_a"], LANES)

    def body(in_ref, gdn_ref, fox_ref, gate_ref, small_ref):
        full = jnp.concatenate([in_ref[dv] for dv in range(nd)], axis=1)
        for ref, piece in zip((gdn_ref, fox_ref, gate_ref, small_ref), _split_w_in(full)):
            ref[...] = piece

    return pl.pallas_call(
        body, name="w_in_pieces", grid=(d // tr,),
        in_specs=[pl.BlockSpec((nd, tr, c), lambda i: (0, i, 0))],
        out_specs=[pl.BlockSpec((tr, wd), lambda i: (i, 0)) for wd in widths],
        out_shape=[jax.ShapeDtypeStruct((d, wd), g_in.dtype) for wd in widths],
        compiler_params=_cparams(("parallel",)),
    )(g_in)


def _w_in_shards(parts, c):
    d = parts[0].shape[0]
    tr = 128

    def body(*refs):
        full = _join_w_in([r[...] for r in refs[:-1]])
        for dv in range(N_DEV):
            refs[-1][dv] = full[:, dv * c:(dv + 1) * c]

    return pl.pallas_call(
        body, name="w_in_shards", grid=(d // tr,),
        in_specs=[pl.BlockSpec((tr, p.shape[1]), lambda i: (i, 0)) for p in parts],
        out_specs=pl.BlockSpec((N_DEV, tr, c), lambda i: (0, i, 0)),
        out_shape=jax.ShapeDtypeStruct((N_DEV, d, c), parts[0].dtype),
        compiler_params=_cparams(("parallel",)),
    )(*parts)


def _join_w_in(parts):
    small = parts[0]
    return jnp.concatenate(parts[1:5] + [small[:, 0:16]] + parts[5:8] + [small[:, 16:24]] + parts[8:10], axis=1)


def _rows128(a, rows):
    flat = a.reshape(-1)
    flat = jnp.concatenate([flat, jnp.zeros((rows * LANES - flat.shape[0],), flat.dtype)])
    return flat.reshape(rows, LANES)


def kernel(x, norm_mix_g, w_in, gdn_conv_w, gdn_a_log, gdn_dt_bias, gdn_norm_g, fox_q_norm_g, fox_k_norm_g, fox_f_bias, w_proj_gdn, w_proj_fox, w_out, norm_mlp_g, w_up, w_down, loss_target, m_norm_mix_g, m_w_in, m_gdn_conv_w, m_gdn_a_log, m_gdn_dt_bias, m_gdn_norm_g, m_fox_q_norm_g, m_fox_k_norm_g, m_fox_f_bias, m_w_proj_gdn, m_w_proj_fox, m_w_out, m_norm_mlp_g, m_w_up, m_w_down, v_norm_mix_g, v_w_in, v_gdn_conv_w, v_gdn_a_log, v_gdn_dt_bias, v_gdn_norm_g, v_fox_q_norm_g, v_fox_k_norm_g, v_fox_f_bias, v_w_proj_gdn, v_w_proj_fox, v_w_out, v_norm_mlp_g, v_w_up, v_w_down):
    b_loc, t, d = x.shape
    n = b_loc * t
    me = 4 * lax.axis_index("x") + 2 * lax.axis_index("y") + lax.axis_index("c")

    late_names = ["w_proj_gdn", "w_proj_fox", "w_out", "w_up", "w_down"]
    big16 = _to_bf16([w_in[0], w_proj_gdn[0], w_proj_fox[0], w_out[0], w_up[0], w_down[0]], "weights_to_bf16")
    g_in, g_conv = _all_gather([big16[0], gdn_conv_w[0]], "gather_w_in")
    late = _push_start(list(big16[1:]), False, "gather_late_start")
    w_gdn, w_fox, w_gate, w_small = _w_in_pieces(g_in)
    weights = {
        "w_gdn": w_gdn, "w_fox": w_fox, "w_gate": w_gate, "w_small": w_small,
        "conv_w": g_conv.transpose(1, 0, 2).reshape(CONV_K, 3 * d),
        "norm_mix_g": norm_mix_g + late["token"][0:1, 0:1], "norm_mlp_g": norm_mlp_g, "a_log": gdn_a_log,
        "dt_bias": gdn_dt_bias, "gdn_norm_g": gdn_norm_g, "fox_q_norm_g": fox_q_norm_g, "fox_k_norm_g": fox_k_norm_g,
        "f_bias": fox_f_bias,
    }
    c_in, c_up = w_in.shape[2], w_up.shape[2]
    me1 = jnp.reshape(me, (1,)).astype(jnp.int32)
    chip1 = jnp.reshape(2 * lax.axis_index("x") + lax.axis_index("y"), (1,)).astype(jnp.int32)
    core1 = jnp.reshape(lax.axis_index("c"), (1,)).astype(jnp.int32)

    class _Exchange:
        def __init__(self):
            self.started = []

        def late_weights(self, after):
            shards, lands = _push_wait(late, after, "gather_late_wait")
            full = [lax.dynamic_update_index_in_dim(land, shard, me, 0) for land, shard in zip(lands, shards)]
            g_pa, g_pb, g_out, g_up, g_down = full
            return {"w_proj_gdn": g_pa.reshape(d, d), "w_proj_fox": g_pb.reshape(d, d), "w_out": g_out.reshape(d, d),
                    "w_up": g_up.transpose(1, 0, 2).reshape(d, D_FF), "w_down": g_down.reshape(D_FF, d)}

        def grads_ready(self, grads, tie):
            names = list(grads)
            if names == ["w_in_parts"]:
                halves = _w_in_shards(grads["w_in_parts"], c_in)
                other = _sibling_swap(halves, "grads_w_in_sibling")
                pair = _add_halves(core1, halves, other, "grads_w_in_pair")
                st = _push_start([pair], True, "grads_start_w_in_parts", chips=True)
            else:
                layout = {"w_up": lambda a: a.reshape(d, N_DEV, c_up).transpose(1, 0, 2),
                          "w_down": lambda a: a.reshape(N_DEV, D_FF // N_DEV, d)}
                arrs = [layout.get(k, lambda a: a.reshape(N_DEV, d // N_DEV, d))(grads[k]) for k in names]
                st = _push_start(arrs, True, "grads_start_" + names[0])
            self.started.append((names, st))
            return tie + st["token"][0:1, 0:1]

    comm = _Exchange()
    loss_blk, grad_x, g = _local_step(x.reshape(n, d), loss_target.reshape(n, d), weights, b_loc, t, comm)

    shards = {"w_in_parts": (w_in, m_w_in, v_w_in), "w_proj_gdn": (w_proj_gdn, m_w_proj_gdn, v_w_proj_gdn),
              "w_proj_fox": (w_proj_fox, m_w_proj_fox, v_w_proj_fox), "w_out": (w_out, m_w_out, v_w_out),
              "w_up": (w_up, m_w_up, v_w_up), "w_down": (w_down, m_w_down, v_w_down)}
    adam = {}

    def finish(names, st, after):
        mine, parts = _push_wait(st, after, "grads_wait_" + names[0])
        slot = chip1 if st["copies"] == len(CHIP_RELS) else me1
        for k, own, part in zip(names, mine, parts):
            wi, mi, vi = shards[k]
            adam[k] = [r[None] for r in _adam_shard(slot, part, own, wi[0], mi[0], vi[0], "adam_" + k)]

    for names, st in comm.started[:-1]:
        finish(names, st, grad_x)

    conv_rows = CONV_K * 3 * d // LANES
    conv_g = g["conv"].transpose(1, 0, 2).reshape(conv_rows, LANES)
    buf = jnp.concatenate([conv_g, g["norm_mix_g"].reshape(8, LANES), g["norm_mlp_g"].reshape(8, LANES),
                           g["gdn_small"], g["fox_small"], loss_blk], axis=0)
    anchor = sum(adam[k][1][0, 0:1, 0:LANES] for names, _ in comm.started[:-1] for k in names) * 0.0
    tot = _all_reduce_small(buf + anchor, "all_reduce_small")
    finish(*comm.started[-1], tot)
    big_out = [adam[k] for k in ["w_in_parts"] + late_names]
    o = conv_rows
    conv_full = tot[0:o].reshape(CONV_K, 3 * d)
    c_conv = gdn_conv_w.shape[2]
    g_conv_shard = lax.dynamic_slice(conv_full, (0, me * c_conv), (CONV_K, c_conv))
    g_mix = tot[o:o + 8].reshape(1, d)
    g_mlp = tot[o + 8:o + 16].reshape(1, d)
    gs, fs = tot[o + 16:o + 24], tot[o + 24:o + 32]
    loss = tot[o + 32, 0]
    small_g = [g_mix, g_conv_shard[None], gs[0:1, 0:HEADS], gs[1:2, 0:HEADS], gs[2:3], fs[0:1], fs[1:2], fs[2:3, 0:HEADS],
               g_mlp]
    small_w = [norm_mix_g, gdn_conv_w, gdn_a_log, gdn_dt_bias, gdn_norm_g, fox_q_norm_g, fox_k_norm_g, fox_f_bias,
               norm_mlp_g]
    small_m = [m_norm_mix_g, m_gdn_conv_w, m_gdn_a_log, m_gdn_dt_bias, m_gdn_norm_g, m_fox_q_norm_g, m_fox_k_norm_g,
               m_fox_f_bias, m_norm_mlp_g]
    small_v = [v_norm_mix_g, v_gdn_conv_w, v_gdn_a_log, v_gdn_dt_bias, v_gdn_norm_g, v_fox_q_norm_g, v_fox_k_norm_g,
               v_fox_f_bias, v_norm_mlp_g]
    row_counts = [-(-a.size // (8 * LANES)) * 8 for a in small_w]

    def pack(arrs):
        return jnp.concatenate([_rows128(a, rc) for a, rc in zip(arrs, row_counts)], axis=0)

    sd, sm, sv = _adam_small(pack(small_g), pack(small_w), pack(small_m), pack(small_v))

    def unpack(p):
        outs, r0 = [], 0
        for a, rc in zip(small_w, row_counts):
            outs.append(p[r0:r0 + rc].reshape(-1)[:a.size].reshape(a.shape))
            r0 += rc
        return outs

    small_out = [small_g_i.reshape(w_i.shape) for small_g_i, w_i in zip(small_g, small_w)], unpack(sd), unpack(sm), unpack(sv)

    def ordered(kind):
        s = small_out[kind]
        bo = [b[kind] for b in big_out]
        return [s[0], bo[0], s[1], s[2], s[3], s[4], s[5], s[6], s[7], bo[1], bo[2], bo[3], s[8], bo[4], bo[5]]

    return (loss, grad_x.reshape(b_loc, t, d), *ordered(0), *ordered(1), *ordered(2), *ordered(3))
```

```python
import functools

import jax
import jax.numpy as jnp
from jax import lax
from jax.experimental import pallas as pl
from jax.experimental.pallas import tpu as pltpu

F32 = jnp.float32
BF16 = jnp.bfloat16
MESH = pl.DeviceIdType.MESH

N_DEV = 8
D_MODEL = 1024
HEADS = 8
DH = 128
CONV_K = 4
CHUNK = 128
GDN_GROUP = 8
FOX_BLOCK = 128
FOX_TILE = 512
FOX_SHORT = 512
D_FF = 4 * D_MODEL
EPS = 1e-6
LANES = 128
NEG = -1e30
IN_OFF = {"gq": 0, "gk": 1024, "gv": 2048, "gz": 3072, "ga": 4096, "gb": 4104, "fq": 4112, "fk": 5136,
          "fv": 6160, "ff": 7184, "gate_a": 7192, "gate_b": 8216, "end": 9240}
LANE_GA, LANE_GB, LANE_FF = 0, 8, 16

ADAM_LR = 0.001
ADAM_B1 = 0.9
ADAM_B2 = 0.999
ADAM_EPS = 1e-08
ADAM_WD = 0.01
ADAM_STEP = 10

VMEM_LIMIT = 56 * 1024 * 1024


def _cparams(sem=None):
    return pltpu.CompilerParams(dimension_semantics=sem, vmem_limit_bytes=VMEM_LIMIT)


def _sigmoid(x):
    return 1.0 / (1.0 + jnp.exp(-x))


def _softplus(x):
    return jnp.maximum(x, 0.0) + jnp.log(1.0 + jnp.exp(-jnp.abs(x)))


def _dot(a, b, prec=None):
    return lax.dot_general(a, b, (((1,), (0,)), ((), ())), precision=prec, preferred_element_type=F32)


def _dot_nt(a, b, prec=None):
    return lax.dot_general(a, b, (((1,), (1,)), ((), ())), precision=prec, preferred_element_type=F32)


def _dot_tn(a, b, prec=None):
    return lax.dot_general(a, b, (((0,), (0,)), ((), ())), precision=prec, preferred_element_type=F32)


def _bf(x):
    return x.astype(BF16)


MM_TILE = 1024


def _mm(a, b, *, name, ta=False, tb=False, out_dtype=F32, epi=None, extras=(), out2=None,
        b_koff=0, tm=MM_TILE, tn=MM_TILE, tk=MM_TILE):
    m = a.shape[1] if ta else a.shape[0]
    kdim = a.shape[0] if ta else a.shape[1]
    n = b.shape[0] if tb else b.shape[1]
    tm, tn, tk = min(tm, m), min(tn, n), min(tk, kdim)
    nk = kdim // tk
    grid = (m // tm, n // tn, nk)
    koff = b_koff // tk
    a_spec = pl.BlockSpec((tk, tm), lambda i, j, k: (k, i)) if ta else pl.BlockSpec((tm, tk), lambda i, j, k: (i, k))
    if tb:
        b_spec = pl.BlockSpec((tn, tk), lambda i, j, k: (j, k + koff))
    else:
        b_spec = pl.BlockSpec((tk, tn), lambda i, j, k: (k + koff, j))
    o_spec = pl.BlockSpec((tm, tn), lambda i, j, k: (i, j))
    n_e = len(extras)
    n_o = 1 if out2 is None else 2
    dims = (((0 if ta else 1,), (1 if tb else 0,)), ((), ()))

    def body(a_ref, b_ref, *rest):
        e_refs, o_refs = rest[:n_e], rest[n_e:n_e + n_o]
        prod = lax.dot_general(_bf(a_ref[...]), _bf(b_ref[...]), dims, preferred_element_type=F32)

        def finish(r):
            if out2 is not None:
                o_refs[1][...] = out2[0](r).astype(out2[1])
            if epi is not None:
                r = epi(r, *[e[...] for e in e_refs])
            o_refs[0][...] = r.astype(out_dtype)

        if nk == 1:
            finish(prod)
        else:
            acc = rest[n_e + n_o]
            k = pl.program_id(2)

            @pl.when(k == 0)
            def _():
                acc[...] = prod

            @pl.when(k > 0)
            def _():
                acc[...] += prod

            @pl.when(k == nk - 1)
            def _():
                finish(acc[...])

    shapes = [jax.ShapeDtypeStruct((m, n), out_dtype)]
    if out2 is not None:
        shapes.append(jax.ShapeDtypeStruct((m, n), out2[1]))
    res = pl.pallas_call(
        body, name=name, grid=grid,
        in_specs=[a_spec, b_spec] + [o_spec] * n_e,
        out_specs=[o_spec] * n_o, out_shape=shapes,
        scratch_shapes=[] if nk == 1 else [pltpu.VMEM((tm, tn), F32)],
        compiler_params=_cparams(("parallel", "parallel", "arbitrary")),
    )(a, b, *extras)
    return res[0] if out2 is None else res


def _du_all(dps, w_small, segs, w, tm=512):
    n, d = dps.shape[0], w_small.shape[0]
    names = []
    for _, wname, _ in segs:
        if wname not in names:
            names.append(wname)
    first = {nm: min(i for i, s in enumerate(segs) if s[1] == nm) for nm in names}
    count = {nm: sum(1 for s in segs if s[1] == nm) for nm in names}
    n_seg, n_i = len(segs), n // tm

    def w_spec(nm):
        return pl.BlockSpec((d, d), lambda k, i: (0, jnp.clip(k - first[nm], 0, count[nm] - 1)))

    def rows_spec(cols, j):
        return pl.BlockSpec((tm, cols), lambda k, i: (jnp.where(k == j, i, jnp.where(k < j, 0, n_i - 1)), 0))

    def body(dps_ref, ws_ref, *rest):
        seg_refs, w_refs, o_ref, acc = rest[:n_seg], rest[n_seg:n_seg + len(names)], rest[-2], rest[-1]
        k, i = pl.program_id(0), pl.program_id(1)
        rows = pl.ds(pl.multiple_of(i * tm, tm), tm)

        @pl.when(k == 0)
        def _():
            acc[rows, :] = _dot_nt(_bf(dps_ref[...]), ws_ref[...])

        for idx, (_, wname, _) in enumerate(segs):
            @pl.when(k == idx)
            def _(idx=idx, wname=wname):
                acc[rows, :] += _dot_nt(seg_refs[idx][...], w_refs[names.index(wname)][...])

        @pl.when(k == n_seg - 1)
        def _():
            o_ref[...] = acc[rows, :]

    return pl.pallas_call(
        body, name="du_all", grid=(n_seg, n_i),
        in_specs=[rows_spec(dps.shape[1], 0), pl.BlockSpec(w_small.shape, lambda k, i: (0, 0))]
                 + [rows_spec(d, j) for j in range(n_seg)] + [w_spec(nm) for nm in names],
        out_specs=pl.BlockSpec((tm, d), lambda k, i: (jnp.where(k == n_seg - 1, i, 0), 0)),
        out_shape=jax.ShapeDtypeStruct((n, d), F32),
        scratch_shapes=[pltpu.VMEM((n, d), F32)],
        compiler_params=_cparams(("arbitrary", "arbitrary")),
    )(dps, w_small, *[s[0] for s in segs], *[w[nm] for nm in names])


def _relu2(x):
    r = jnp.maximum(x, 0.0)
    return r * r


ROWS = 512


def _rms_fwd(x, g, name):
    n, d = x.shape

    def body(x_ref, g_ref, u_ref):
        xv = x_ref[...]
        r = lax.rsqrt(jnp.mean(xv * xv, axis=1, keepdims=True) + EPS)
        u_ref[...] = _bf(xv * r * g_ref[...])

    return pl.pallas_call(
        body, name=name, grid=(n // ROWS,),
        in_specs=[pl.BlockSpec((ROWS, d), lambda i: (i, 0)), pl.BlockSpec((1, d), lambda i: (0, 0))],
        out_specs=pl.BlockSpec((ROWS, d), lambda i: (i, 0)),
        out_shape=jax.ShapeDtypeStruct((n, d), BF16),
        compiler_params=_cparams(("parallel",)),
    )(x, g)


def _rms_bwd(dy, x, g, dres, name):
    n, d = x.shape

    def body(dy_ref, x_ref, g_ref, dres_ref, dx_ref, dx16_ref, dg_ref):
        i = pl.program_id(0)
        xv, dyv = x_ref[...], dy_ref[...]
        r = lax.rsqrt(jnp.mean(xv * xv, axis=1, keepdims=True) + EPS)
        gy = dyv * g_ref[...]
        s = jnp.sum(gy * xv, axis=1, keepdims=True)
        dx = dres_ref[...] + r * gy - xv * (r * r * r * (1.0 / d)) * s
        dx_ref[...] = dx
        dx16_ref[...] = _bf(dx)

        @pl.when(i == 0)
        def _():
            dg_ref[...] = jnp.zeros_like(dg_ref)

        dg_ref[...] += jnp.sum(dyv * xv * r, axis=0, keepdims=True)

    row = pl.BlockSpec((ROWS, d), lambda i: (i, 0))
    vec = pl.BlockSpec((1, d), lambda i: (0, 0))
    return pl.pallas_call(
        body, name=name, grid=(n // ROWS,),
        in_specs=[row, row, vec, row], out_specs=[row, row, vec],
        out_shape=[jax.ShapeDtypeStruct((n, d), F32), jax.ShapeDtypeStruct((n, d), BF16),
                   jax.ShapeDtypeStruct((1, d), F32)],
        compiler_params=_cparams(("arbitrary",)),
    )(dy, x, g, dres)


def _merge_fwd(ya, yb, gate):
    n, d = ya.shape

    def body(ya_ref, yb_ref, ga_ref, gb_ref, o_ref):
        o_ref[...] = _bf(_sigmoid(ga_ref[...]) * ya_ref[...] + _sigmoid(gb_ref[...]) * yb_ref[...])

    row = pl.BlockSpec((ROWS, d), lambda i: (i, 0))
    return pl.pallas_call(
        body, name="merge_fwd", grid=(n // ROWS,),
        in_specs=[row, row, row, pl.BlockSpec((ROWS, d), lambda i: (i, 1))], out_specs=row,
        out_shape=jax.ShapeDtypeStruct((n, d), BF16),
        compiler_params=_cparams(("parallel",)),
    )(ya, yb, gate, gate)


def _merge_bwd(dm, ya, yb, gate):
    n, d = ya.shape

    def body(dm_ref, ya_ref, yb_ref, ga_ref, gb_ref, dya_ref, dyb_ref, dga_ref, dgb_ref):
        dmv = dm_ref[...]
        sa, sb = _sigmoid(ga_ref[...]), _sigmoid(gb_ref[...])
        dya_ref[...] = _bf(dmv * sa)
        dyb_ref[...] = _bf(dmv * sb)
        dga_ref[...] = _bf(dmv * ya_ref[...] * sa * (1.0 - sa))
        dgb_ref[...] = _bf(dmv * yb_ref[...] * sb * (1.0 - sb))

    row = pl.BlockSpec((ROWS, d), lambda i: (i, 0))
    o = jax.ShapeDtypeStruct((n, d), BF16)
    return pl.pallas_call(
        body, name="merge_bwd", grid=(n // ROWS,),
        in_specs=[row, row, row, row, pl.BlockSpec((ROWS, d), lambda i: (i, 1))], out_specs=[row] * 4,
        out_shape=[o] * 4,
        compiler_params=_cparams(("parallel",)),
    )(dm, ya, yb, gate, gate)


def _loss_bwd(out, target):
    n, d = out.shape

    def body(o_ref, t_ref, d_ref, d16_ref, l_ref):
        i = pl.program_id(0)
        err = o_ref[...] - t_ref[...]
        d_ref[...] = err * (1.0 / d)
        d16_ref[...] = _bf(err * (1.0 / d))

        @pl.when(i == 0)
        def _():
            l_ref[...] = jnp.zeros_like(l_ref)

        l_ref[...] += 0.5 * jnp.sum(jnp.mean(err * err, axis=1, keepdims=True), axis=0, keepdims=True)

    row = pl.BlockSpec((ROWS, d), lambda i: (i, 0))
    return pl.pallas_call(
        body, name="loss_bwd", grid=(n // ROWS,),
        in_specs=[row, row], out_specs=[row, row, pl.BlockSpec((8, LANES), lambda i: (0, 0))],
        out_shape=[jax.ShapeDtypeStruct((n, d), F32), jax.ShapeDtypeStruct((n, d), BF16),
                   jax.ShapeDtypeStruct((8, LANES), F32)],
        compiler_params=_cparams(("arbitrary",)),
    )(out, target)


PAD = 8


def _pad_zero(pad_ref):
    t = pad_ref.shape[0] - 2 * PAD
    pad_ref[0:PAD, :] = jnp.zeros((PAD, LANES), F32)
    pad_ref[PAD + t:2 * PAD + t, :] = jnp.zeros((PAD, LANES), F32)


def _shifted(pad_ref, s):
    t = pad_ref.shape[0] - 2 * PAD
    return pad_ref[PAD - s:PAD - s + t, :]


def _conv(x, w_ref, pad_ref):
    t = x.shape[0]
    pad_ref[PAD:PAD + t, :] = x
    y = _shifted(pad_ref, 3) * w_ref[0:1, :]
    y = y + _shifted(pad_ref, 2) * w_ref[1:2, :]
    y = y + _shifted(pad_ref, 1) * w_ref[2:3, :]
    return y + x * w_ref[3:4, :]


def _chunk_consts():
    r = lax.broadcasted_iota(jnp.int32, (CHUNK, CHUNK), 0)
    c = lax.broadcasted_iota(jnp.int32, (CHUNK, CHUNK), 1)
    incl, strict = r >= c, r > c
    return dict(incl=incl, strict=strict, trilf=incl.astype(F32), triuf=(r <= c).astype(F32),
                eye=(r == c).astype(F32))


class _V:
    def __init__(self, xs):
        self.xs = list(xs)

    def __add__(self, o):
        return _ap(lambda x, y: x + y, self, o)

    def __radd__(self, o):
        return _ap(lambda x, y: y + x, self, o)

    def __sub__(self, o):
        return _ap(lambda x, y: x - y, self, o)

    def __rsub__(self, o):
        return _ap(lambda x, y: y - x, self, o)

    def __mul__(self, o):
        return _ap(lambda x, y: x * y, self, o)

    def __rmul__(self, o):
        return _ap(lambda x, y: y * x, self, o)

    def __neg__(self):
        return _ap(lambda x: -x, self)

    def __getitem__(self, idx):
        return _ap(lambda x: x[idx], self)


def _ap(fn, *args):
    n = [len(a.xs) for a in args if isinstance(a, _V)]
    if not n:
        return fn(*args)
    return _V([fn(*[a.xs[i] if isinstance(a, _V) else a for a in args]) for i in range(n[0])])


def _vbf(x):
    return _ap(_bf, x)


def _vdot(a, b):
    return _ap(_dot, a, b)


def _vdot_nt(a, b):
    return _ap(_dot_nt, a, b)


def _vdot_tn(a, b):
    return _ap(_dot_tn, a, b)


def _vexp(x):
    return _ap(jnp.exp, x)


def _vsum(x, axis):
    return _ap(lambda v: jnp.sum(v, axis=axis, keepdims=True), x)


def _vcat(a, b, axis):
    return _ap(lambda x, y: jnp.concatenate([x, y], axis=axis), a, b)


def _vmask(mask, x):
    return _ap(lambda v: jnp.where(mask, v, 0.0), x)


def _split2(x):
    h = _vbf(x)
    return h, _vbf(x - _ap(lambda v: v.astype(F32), h))


def _dot3(a, b, kind=_vdot):
    ah, al = _split2(a)
    bh, bl = _split2(b)
    return kind(ah, bh) + (kind(ah, bl) + kind(al, bh))


def _split(x, terms):
    out = []
    for _ in range(terms):
        h = _vbf(x)
        out.append(h)
        x = x - _ap(lambda v: v.astype(F32), h)
    return out


def _dot_exact_l(m01, x, kind=_vdot, terms=2):
    mb = _bf(m01)
    parts = [kind(mb, xp) for xp in _split(x, terms)]
    return functools.reduce(lambda a, b: a + b, reversed(parts))


def _dot_exact_r(x, m01, kind=_vdot, terms=2):
    mb = _bf(m01)
    parts = [kind(xp, mb) for xp in _split(x, terms)]
    return functools.reduce(lambda a, b: a + b, reversed(parts))


def _inv_series(a, eye):
    m = eye.shape[0]
    levels = m.bit_length() - 1
    p = -a
    r = p + eye
    p = _dot3(p, p)
    for j in range(1, levels):
        if j < levels - 1:
            y = _dot3(p, _vcat(p, r, 1))
            p, r = y[:, 0:m], r + y[:, m:2 * m]
        else:
            r = r + _dot3(p, r)
    return r


def _inv_unit_lower(a, eye):
    return _inv_series(a, eye)


def _gdn_chunk_pre(q, k, v, g128, b128, cs):
    incl = cs["incl"]
    b64 = b128
    big_g = _dot_exact_l(cs["trilf"], g128)
    gc = big_g[:, 0:CHUNK]
    gr = _dot_exact_r(g128, cs["triuf"], _vdot_tn)
    decay = _ap(lambda d: jnp.where(incl, jnp.exp(jnp.where(incl, d, 0.0)), 0.0), gc - gr)
    kb, qb = _vbf(k), _vbf(q)
    qkk = _vdot_nt(_vcat(qb, kb, 0), kb)
    qk, kk = qkk[0:CHUNK], qkk[CHUNK:2 * CHUNK]
    tm = _inv_unit_lower(_vmask(cs["strict"], b64 * kk * decay), cs["eye"])
    e_g = _vexp(big_g)
    wu = _dot3(tm, _vcat(v * b128, k * (b128 * e_g), 1))
    w, u = wu[:, 0:DH], wu[:, DH:2 * DH]
    g_last = _vsum(g128, 0)
    return dict(big_g=big_g, decay=decay, kk=kk, qk=qk, tm=tm, w=w, u=u, p=qk * decay, q_dec=q * e_g,
                k_dec=k * _vexp(g_last - big_g), dec=_vexp(g_last))


def _gdn_chunk_post(q, k, v, g128, b128, s, ds_next, do, dv_new, big_g, decay, kk, qk, tm, u, v_new, cs):
    b64 = b128
    e_g = _vexp(big_g)
    vb = v * b128
    kbeta = k * (b128 * e_g)
    q_dec = q * e_g
    g_last = _vsum(g128, 0)
    ekg = _vexp(g_last - big_g)
    k_dec = k * ekg
    dec = _vexp(g_last)
    kb, qb, sb = _vbf(k), _vbf(q), _vbf(s)
    dob, dsb, vnb, dvnb = _vbf(do), _vbf(ds_next), _vbf(v_new), _vbf(dv_new)
    dp = _vmask(cs["incl"], _vdot_nt(dob, vnb))
    dq_dec = _vdot_nt(dob, sb)
    du = -_vdot_nt(dvnb, sb)
    ddec = _vsum(_vsum(s * ds_next, 1), 0)
    dk_dec = _vdot_nt(vnb, dsb)
    dwu = _vcat(dv_new, du, 1)
    dt = _dot3(dwu, _vcat(vb, kbeta, 1), _vdot_nt)
    dvk = _dot3(tm, dwu, _vdot_tn)
    dvb, dkbeta = dvk[:, 0:DH], dvk[:, DH:2 * DH]
    da = _vmask(cs["strict"], -_dot3(tm, _dot3(dt, tm, _vdot_nt), _vdot_tn))
    dkk = _vbf(da * b64 * decay)
    dqk = _vbf(dp * decay)
    ddd = (da * b64 * kk + dp * qk) * decay
    dq = _vdot(dqk, kb) + dq_dec * e_g
    dk = _vdot_tn(dqk, qb) + _vdot(dkk, kb) + _vdot_tn(dkk, kb) + dk_dec * ekg + dkbeta * (b128 * e_g)
    dv = dvb * b128
    dbeta = _vsum(da * kk * decay, 1) + _vsum(dvb * v, 1) + _vsum(dkbeta * k * e_g, 1)
    s_k = _vsum(dk_dec * k_dec, 1)
    dg_col = _vsum(ddd, 1) + _vsum(dq_dec * q_dec, 1) - s_k + _vsum(dkbeta * kbeta, 1)
    colsum = _dot_exact_r(ddd, jnp.ones((CHUNK, LANES), F32), _vdot_tn)
    dg_last = _vsum(s_k, 0) + ddec * dec
    dg = _dot_exact_l(cs["triuf"], dg_col - colsum) + dg_last
    return dq, dk, dv, dg, dbeta


def _stack_rows(vecs, nrows):
    row = lax.broadcasted_iota(jnp.int32, (nrows, LANES), 0)
    out = jnp.zeros((nrows, LANES), F32)
    for i, v in enumerate(vecs):
        out = out + jnp.where(row == i, jnp.broadcast_to(v, (nrows, LANES)), 0.0)
    return out


def _head_lane(x, lane_idx):
    lane = lax.broadcasted_iota(jnp.int32, x.shape, 1)
    return jnp.sum(jnp.where(lane == lane_idx, x, 0.0), axis=1, keepdims=True)


def _gdn_gates(ps, h, alog_ref, dtb_ref):
    ga = _head_lane(ps, LANE_GA + h)
    gb = _head_lane(ps, LANE_GB + h)
    a = jnp.exp(jnp.full((1, 1), alog_ref[0, h], F32))
    sp_in = ga + dtb_ref[0, h]
    g = -a * _softplus(sp_in)
    return g, _sigmoid(gb), a, sp_in


def _gdn_specs(b_loc, t):
    def col(off):
        return pl.BlockSpec((t, DH), lambda b, h: (b, off + h))

    ps_spec = pl.BlockSpec((t, LANES), lambda b, h: (b, 0))

    def wcol(off):
        return pl.BlockSpec((CONV_K, DH), lambda b, h: (0, off + h))

    smem = pl.BlockSpec(memory_space=pltpu.SMEM)
    vec = pl.BlockSpec((1, DH), lambda b, h: (0, 0))
    return col, ps_spec, wcol, smem, vec


def _gdn_fwd(pg, ps, convw, a_log, dt_bias, gnorm, b_loc, t):
    n = b_loc * t
    assert t % (CHUNK * GDN_GROUP) == 0 and CHUNK == LANES, (t, CHUNK, GDN_GROUP)
    nc = t // CHUNK
    col, ps_spec, wcol, smem, vec = _gdn_specs(b_loc, t)

    def body(q_ref, k_ref, v_ref, z_ref, ps_ref, wq_ref, wk_ref, wv_ref, alog_ref, dtb_ref, gn_ref,
             oa_ref, oraw_ref, s_ref, qn, kn, vv, g128, b128, uq_s, p_s, kd_s, dec_s, pad_s):
        h = pl.program_id(1)
        g, beta, _, _ = _gdn_gates(ps_ref[...], h, alog_ref, dtb_ref)
        g128[...] = jnp.broadcast_to(g, (t, LANES))
        b128[...] = jnp.broadcast_to(beta, (t, LANES))
        _pad_zero(pad_s)
        pq = _conv(q_ref[...], wq_ref, pad_s)
        yq = pq * _sigmoid(pq)
        qn[...] = yq * (lax.rsqrt(jnp.sum(yq * yq, axis=1, keepdims=True) + EPS) * (DH ** -0.5))
        pk = _conv(k_ref[...], wk_ref, pad_s)
        yk = pk * _sigmoid(pk)
        kn[...] = yk * lax.rsqrt(jnp.sum(yk * yk, axis=1, keepdims=True) + EPS)
        pv = _conv(v_ref[...], wv_ref, pad_s)
        vv[...] = pv * _sigmoid(pv)
        cs = _chunk_consts()

        def pre_group(gi, _):
            idx = [gi * GDN_GROUP + c for c in range(GDN_GROUP)]
            rows = [pl.ds(pl.multiple_of(i * CHUNK, CHUNK), CHUNK) for i in idx]
            ins = [_V([ref[r, :] for r in rows]) for ref in (qn, kn, vv, g128, b128)]
            f = _gdn_chunk_pre(*ins, cs)
            for c, (i, r) in enumerate(zip(idx, rows)):
                vv[r, :] = f["w"].xs[c]
                uq_s[i, 0:CHUNK, :] = _bf(f["u"].xs[c])
                uq_s[i, CHUNK:2 * CHUNK, :] = _bf(f["q_dec"].xs[c])
                p_s[r, :] = _bf(f["p"].xs[c])
                kd_s[r, :] = _bf(f["k_dec"].xs[c])
                dec_s[pl.ds(pl.multiple_of(i * 8, 8), 8), :] = jnp.broadcast_to(f["dec"].xs[c], (8, LANES))
            return 0

        lax.fori_loop(0, nc // GDN_GROUP, pre_group, 0)

        def chunk(i, s):
            r = pl.ds(pl.multiple_of(i * CHUNK, CHUNK), CHUNK)
            us = _dot(uq_s[i], _bf(s))
            vnb = _bf(vv[r, :] - us[0:CHUNK])
            oraw_ref[r, :] = us[CHUNK:2 * CHUNK] + _dot(p_s[r, :], vnb)
            s_ref[0, 0, i] = s
            return s * dec_s[pl.ds(pl.multiple_of(i * 8, 8), 1), :] + _dot_tn(kd_s[r, :], vnb)

        lax.fori_loop(0, nc, chunk, jnp.zeros((DH, DH), F32))
        o = oraw_ref[...]
        rr = lax.rsqrt(jnp.mean(o * o, axis=1, keepdims=True) + EPS)
        z = z_ref[...]
        oa_ref[...] = _bf((o * rr * gn_ref[...]) * (z * _sigmoid(z)))

    return pl.pallas_call(
        body, name="gdn_fwd", grid=(b_loc, HEADS),
        in_specs=[col(0), col(HEADS), col(2 * HEADS), col(3 * HEADS), ps_spec, wcol(0), wcol(HEADS), wcol(2 * HEADS),
                  smem, smem, vec],
        out_specs=[pl.BlockSpec((t, DH), lambda b, h: (b, h)), pl.BlockSpec((t, DH), lambda b, h: (b, h)),
                   pl.BlockSpec((1, 1, nc, DH, DH), lambda b, h: (b, h, 0, 0, 0))],
        out_shape=[jax.ShapeDtypeStruct((n, HEADS * DH), BF16), jax.ShapeDtypeStruct((n, HEADS * DH), F32),
                   jax.ShapeDtypeStruct((b_loc, HEADS, nc, DH, DH), F32)],
        scratch_shapes=([pltpu.VMEM((t, DH), F32)] * 3 + [pltpu.VMEM((t, LANES), F32)] * 2
                        + [pltpu.VMEM((nc, 2 * CHUNK, DH), BF16), pltpu.VMEM((t, CHUNK), BF16), pltpu.VMEM((t, DH), BF16),
                           pltpu.VMEM((8 * nc, LANES), F32), pltpu.VMEM((t + 2 * PAD, LANES), F32)]),
        compiler_params=_cparams(("arbitrary", "arbitrary")),
    )(pg, pg, pg, pg, ps, convw, convw, convw, a_log, dt_bias, gnorm)


def _gdn_bwd(pg, ps, convw, a_log, dt_bias, gnorm, d_oa, o_raw, s_all, b_loc, t):
    n = b_loc * t
    assert t % (CHUNK * GDN_GROUP) == 0 and CHUNK == LANES, (t, CHUNK, GDN_GROUP)
    nc = t // CHUNK
    col, ps_spec, wcol, smem, vec = _gdn_specs(b_loc, t)

    def body(q_ref, k_ref, v_ref, z_ref, ps_ref, wq_ref, wk_ref, wv_ref, alog_ref, dtb_ref, gn_ref,
             doa_ref, oraw_ref, s_ref,
             dq_ref, dk_ref, dv_ref, dz_ref, dps_ref, dcw_ref, dsm_ref,
             qn, kn, vv, g128, b128, do_s, bg_s, u_s, vn_s, dvn_s, dcy_s, kk_s, qk_s, tm_s, dsn_s, pad_s):
        b, h = pl.program_id(0), pl.program_id(1)
        g, beta, _, _ = _gdn_gates(ps_ref[...], h, alog_ref, dtb_ref)
        g128[...] = jnp.broadcast_to(g, (t, LANES))
        b128[...] = jnp.broadcast_to(beta, (t, LANES))
        _pad_zero(pad_s)

        def prep(x_ref, w_ref):
            p = _conv(x_ref[...], w_ref, pad_s)
            sg = _sigmoid(p)
            return p, sg, p * sg

        _, _, yq = prep(q_ref, wq_ref)
        qn[...] = yq * (lax.rsqrt(jnp.sum(yq * yq, axis=1, keepdims=True) + EPS) * (DH ** -0.5))
        _, _, yk = prep(k_ref, wk_ref)
        kn[...] = yk * lax.rsqrt(jnp.sum(yk * yk, axis=1, keepdims=True) + EPS)
        _, _, yv = prep(v_ref, wv_ref)
        vv[...] = yv

        o = oraw_ref[...]
        z = z_ref[...]
        doa = doa_ref[...]
        gn = gn_ref[...]
        ro = lax.rsqrt(jnp.mean(o * o, axis=1, keepdims=True) + EPS)
        sz = _sigmoid(z)
        dz_ref[...] = _bf(doa * (o * ro * gn) * (sz * (1.0 + z * (1.0 - sz))))
        dn = doa * (z * sz)
        dgn = jnp.sum(dn * o * ro, axis=0, keepdims=True)
        gy = dn * gn
        do_s[...] = ro * gy - o * (ro * ro * ro * (1.0 / DH)) * jnp.sum(gy * o, axis=1, keepdims=True)

        cs = _chunk_consts()

        def pre_group(gi, _):
            idx = [gi * GDN_GROUP + c for c in range(GDN_GROUP)]
            rows = [pl.ds(pl.multiple_of(i * CHUNK, CHUNK), CHUNK) for i in idx]
            ins = [_V([ref[r, :] for r in rows]) for ref in (qn, kn, vv, g128, b128)]
            states = _V([_bf(s_ref[0, 0, i]) for i in idx])
            f = _gdn_chunk_pre(*ins, cs)
            v_new = f["w"] - _vdot(_vbf(f["u"]), states)
            for c, r in enumerate(rows):
                bg_s[r, :] = f["big_g"].xs[c]
                u_s[r, :] = f["u"].xs[c]
                vn_s[r, :] = v_new.xs[c]
                dcy_s[r, :] = f["decay"].xs[c]
                kk_s[r, :] = f["kk"].xs[c]
                qk_s[r, :] = f["qk"].xs[c]
                tm_s[r, :] = f["tm"].xs[c]
            return 0

        lax.fori_loop(0, nc // GDN_GROUP, pre_group, 0)

        def chunk(j, ds):
            i = nc - 1 - j
            r = pl.ds(pl.multiple_of(i * CHUNK, CHUNK), CHUNK)
            big_g = bg_s[r, :]
            g_last = jnp.sum(g128[r, :], axis=0, keepdims=True)
            dob = _bf(do_s[r, :])
            dv_new = (_dot_tn(_bf(qk_s[r, :] * dcy_s[r, :]), dob)
                      + _dot(_bf(kn[r, :] * jnp.exp(g_last - big_g)), _bf(ds)))
            dvn_s[r, :] = dv_new
            dsn_s[i] = ds
            return (_dot_tn(_bf(qn[r, :] * jnp.exp(big_g)), dob) + jnp.exp(g_last) * ds
                    - _dot_tn(_bf(u_s[r, :]), _bf(dv_new)))

        lax.fori_loop(0, nc, chunk, jnp.zeros((DH, DH), F32))

        def post_group(gi, _):
            idx = [gi * GDN_GROUP + c for c in range(GDN_GROUP)]
            rows = [pl.ds(pl.multiple_of(i * CHUNK, CHUNK), CHUNK) for i in idx]
            def rows_of(ref):
                return _V([ref[r, :] for r in rows])

            dq, dk, dv, dg, dbeta = _gdn_chunk_post(
                rows_of(qn), rows_of(kn), rows_of(vv), rows_of(g128), rows_of(b128),
                _V([s_ref[0, 0, i] for i in idx]), _V([dsn_s[i] for i in idx]), rows_of(do_s), rows_of(dvn_s),
                rows_of(bg_s), rows_of(dcy_s), rows_of(kk_s), rows_of(qk_s), rows_of(tm_s), rows_of(u_s), rows_of(vn_s),
                cs)
            for c, r in enumerate(rows):
                qn[r, :] = dq.xs[c]
                kn[r, :] = dk.xs[c]
                vv[r, :] = dv.xs[c]
                g128[r, :] = dg.xs[c]
                b128[r, :] = jnp.broadcast_to(dbeta.xs[c], (CHUNK, LANES))
            return 0

        lax.fori_loop(0, nc // GDN_GROUP, post_group, 0)
        dqh, dkh, dvh = qn, kn, vv

        g, beta, a, sp_in = _gdn_gates(ps_ref[...], h, alog_ref, dtb_ref)
        dg = g128[...]
        d_ga = dg * (-a) * _sigmoid(sp_in)
        d_alog = jnp.sum(dg * g, axis=0, keepdims=True)
        d_dtb = jnp.sum(d_ga, axis=0, keepdims=True)
        d_gb = b128[...] * (beta * (1.0 - beta))
        lane = lax.broadcasted_iota(jnp.int32, (t, LANES), 1)
        contrib = jnp.where(lane == LANE_GA + h, d_ga, 0.0) + jnp.where(lane == LANE_GB + h, d_gb, 0.0)

        @pl.when(h == 0)
        def _():
            dps_ref[...] = jnp.zeros_like(dps_ref)

        dps_ref[...] += contrib

        lane1 = lax.broadcasted_iota(jnp.int32, (1, LANES), 1)
        small = _stack_rows([jnp.where(lane1 == h, d_alog, 0.0), jnp.where(lane1 == h, d_dtb, 0.0), dgn], 8)

        @pl.when((b == 0) & (h == 0))
        def _():
            dsm_ref[...] = jnp.zeros_like(dsm_ref)
            dcw_ref[...] = jnp.zeros_like(dcw_ref)

        dsm_ref[...] += small

        def conv_bwd(dp, x, w_ref, slot):
            dw = _stack_rows([jnp.sum(dp * _shifted(pad_s, 3), axis=0, keepdims=True),
                              jnp.sum(dp * _shifted(pad_s, 2), axis=0, keepdims=True),
                              jnp.sum(dp * _shifted(pad_s, 1), axis=0, keepdims=True),
                              jnp.sum(dp * x, axis=0, keepdims=True)], CONV_K)
            dcw_ref[slot] += dw
            pad_s[PAD:PAD + t, :] = dp
            dx = _shifted(pad_s, -3) * w_ref[0:1, :]
            dx = dx + _shifted(pad_s, -2) * w_ref[1:2, :]
            dx = dx + _shifted(pad_s, -1) * w_ref[2:3, :]
            return dx + dp * w_ref[3:4, :]

        def l2_bwd(dqn, y, c):
            r = lax.rsqrt(jnp.sum(y * y, axis=1, keepdims=True) + EPS)
            s1 = jnp.sum(dqn * y, axis=1, keepdims=True)
            return c * r * dqn - (c * r * r * r) * s1 * y

        def silu_bwd(p, sg):
            return sg * (1.0 + p * (1.0 - sg))

        pq, sq, yq = prep(q_ref, wq_ref)
        dq_ref[...] = _bf(conv_bwd(l2_bwd(dqh[...], yq, DH ** -0.5) * silu_bwd(pq, sq), q_ref[...], wq_ref, h))
        pk, sk, yk = prep(k_ref, wk_ref)
        dk_ref[...] = _bf(conv_bwd(l2_bwd(dkh[...], yk, 1.0) * silu_bwd(pk, sk), k_ref[...], wk_ref, HEADS + h))
        pv, sv, _ = prep(v_ref, wv_ref)
        dv_ref[...] = _bf(conv_bwd(dvh[...] * silu_bwd(pv, sv), v_ref[...], wv_ref, 2 * HEADS + h))

    blk = pl.BlockSpec((t, DH), lambda b, h: (b, h))
    ob = jax.ShapeDtypeStruct((n, HEADS * DH), BF16)
    return pl.pallas_call(
        body, name="gdn_bwd", grid=(b_loc, HEADS),
        in_specs=[col(0), col(HEADS), col(2 * HEADS), col(3 * HEADS), ps_spec, wcol(0), wcol(HEADS), wcol(2 * HEADS),
                  smem, smem, vec, blk, blk, pl.BlockSpec((1, 1, nc, DH, DH), lambda b, h: (b, h, 0, 0, 0))],
        out_specs=[blk, blk, blk, blk, ps_spec,
                   pl.BlockSpec((3 * HEADS, CONV_K, DH), lambda b, h: (0, 0, 0)),
                   pl.BlockSpec((8, LANES), lambda b, h: (0, 0))],
        out_shape=[ob, ob, ob, ob, jax.ShapeDtypeStruct((n, LANES), F32),
                   jax.ShapeDtypeStruct((3 * HEADS, CONV_K, DH), F32), jax.ShapeDtypeStruct((8, LANES), F32)],
        scratch_shapes=([pltpu.VMEM((t, DH), F32)] * 3 + [pltpu.VMEM((t, LANES), F32)] * 2
                        + [pltpu.VMEM((t, DH), F32)] * 5 + [pltpu.VMEM((t, CHUNK), F32)] * 4
                        + [pltpu.VMEM((nc, DH, DH), F32), pltpu.VMEM((t + 2 * PAD, LANES), F32)]),
        compiler_params=_cparams(("arbitrary", "arbitrary")),
    )(pg, pg, pg, pg, ps, convw, convw, convw, a_log, dt_bias, gnorm, d_oa, o_raw, s_all)


def _fox_prologue(q_ref, k_ref, v_ref, ps_ref, fb_ref, gq_ref, gk_ref, h, t, qs, ks, vs, ccol, crow):
    nb = t // FOX_BLOCK
    q, k = q_ref[...], k_ref[...]
    rq = lax.rsqrt(jnp.mean(q * q, axis=1, keepdims=True) + EPS)
    rk = lax.rsqrt(jnp.mean(k * k, axis=1, keepdims=True) + EPS)
    qs[...] = _bf(q * rq * gq_ref[...])
    ks[...] = _bf(k * rk * gk_ref[...])
    vs[...] = _bf(v_ref[...])
    f_in = _head_lane(ps_ref[...], LANE_FF + h) + fb_ref[0, h]
    ccol[...] = jnp.broadcast_to(-_softplus(-f_in), (t, LANES))
    r = lax.broadcasted_iota(jnp.int32, (FOX_BLOCK, FOX_BLOCK), 0)
    c = lax.broadcasted_iota(jnp.int32, (FOX_BLOCK, FOX_BLOCK), 1)
    trilf, triuf = (r >= c).astype(F32), (r <= c).astype(F32)
    blocks = [pl.ds(j * FOX_BLOCK, FOX_BLOCK) for j in range(nb)]
    lfs = _V([ccol[rb, :] for rb in blocks])
    cc = _dot_exact_l(trilf, lfs, terms=3)
    cr = _dot_exact_r(lfs, triuf, _vdot_tn, terms=3)
    sums = _vsum(lfs, 0)
    carry = jnp.zeros((1, LANES), F32)
    for j, rb in enumerate(blocks):
        ccol[rb, :] = cc.xs[j] + carry
        crow[j] = (cr.xs[j] + carry)[0:8]
        carry = carry + sums.xs[j]
    return rq, rk, f_in


def _fox_scores(q_rows, k_rows, cc, cr, row0, col0):
    s = _dot_nt(q_rows, k_rows) * (DH ** -0.5) + cc - cr
    r = lax.broadcasted_iota(jnp.int32, s.shape, 0)
    c = lax.broadcasted_iota(jnp.int32, s.shape, 1)
    return jnp.where(row0 + r >= col0 + c, s, NEG)


def _fox_specs(t):
    def col(off):
        return pl.BlockSpec((t, DH), lambda b, h: (b, off + h))

    ps_spec = pl.BlockSpec((t, LANES), lambda b, h: (b, 0))
    smem = pl.BlockSpec(memory_space=pltpu.SMEM)
    vec = pl.BlockSpec((1, DH), lambda b, h: (0, 0))
    blk = pl.BlockSpec((t, DH), lambda b, h: (b, h))
    return col, ps_spec, smem, vec, blk


def _fox_fwd(pf, ps, f_bias, gq, gk, b_loc, t):
    n = b_loc * t
    nb = t // FOX_BLOCK
    assert t % FOX_TILE == 0 and FOX_TILE % FOX_SHORT == 0, (t, FOX_TILE, FOX_SHORT)
    kt = FOX_TILE
    nsub = kt // FOX_BLOCK
    col, ps_spec, smem, vec, blk = _fox_specs(t)

    def body(q_ref, k_ref, v_ref, ps_ref, fb_ref, gq_ref, gk_ref, o_ref, lse_ref, qs, ks, vs, ccol, crow):
        h = pl.program_id(1)
        _fox_prologue(q_ref, k_ref, v_ref, ps_ref, fb_ref, gq_ref, gk_ref, h, t, qs, ks, vs, ccol, crow)

        def qblock(i, _):
            ri = pl.ds(pl.multiple_of(i * FOX_SHORT, FOX_SHORT), FOX_SHORT)
            qi = qs[ri, :]
            cc = jnp.concatenate([ccol[ri, :]] * nsub, axis=1)

            def ktile(j, carry):
                m, l, acc = carry
                rj = pl.ds(pl.multiple_of(j * kt, kt), kt)
                cr = jnp.concatenate([crow[j * nsub + u, 0:1, :] for u in range(nsub)], axis=1)
                s = _fox_scores(qi, ks[rj, :], cc, cr, i * FOX_SHORT, j * kt)
                m_new = jnp.maximum(m, jnp.max(s, axis=1, keepdims=True))
                p = jnp.exp(s - m_new)
                alpha = jnp.exp(m - m_new)
                l = alpha * l + jnp.sum(p, axis=1, keepdims=True)
                acc = alpha * acc + _dot(_bf(p), vs[rj, :])
                return m_new, l, acc

            m, l, acc = lax.fori_loop(0, (i * FOX_SHORT) // kt + 1, ktile, (jnp.full((FOX_SHORT, 1), NEG, F32),
                                                                            jnp.zeros((FOX_SHORT, 1), F32),
                                                                            jnp.zeros((FOX_SHORT, DH), F32)))
            o_ref[ri, :] = acc / l
            lse_ref[ri, :] = jnp.broadcast_to(m + jnp.log(l), (FOX_SHORT, LANES))
            return 0

        lax.fori_loop(0, t // FOX_SHORT, qblock, 0)

    o = jax.ShapeDtypeStruct((n, HEADS * DH), F32)
    return pl.pallas_call(
        body, name="fox_fwd", grid=(b_loc, HEADS),
        in_specs=[col(0), col(HEADS), col(2 * HEADS), ps_spec, smem, vec, vec],
        out_specs=[blk, blk], out_shape=[o, o],
        scratch_shapes=[pltpu.VMEM((t, DH), BF16)] * 3 + [pltpu.VMEM((t, LANES), F32), pltpu.VMEM((nb, 8, LANES), F32)],
        compiler_params=_cparams(("arbitrary", "arbitrary")),
    )(pf, pf, pf, ps, f_bias, gq, gk)


def _fox_bwd(pf, ps, f_bias, gq, gk, d_ob, ob, lse, dps_in, b_loc, t):
    n = b_loc * t
    nb = t // FOX_BLOCK
    assert t % FOX_TILE == 0 and FOX_TILE % FOX_SHORT == 0, (t, FOX_TILE, FOX_SHORT)
    qt = FOX_TILE
    scale = DH ** -0.5
    col, ps_spec, smem, vec, blk = _fox_specs(t)

    def body(q_ref, k_ref, v_ref, ps_ref, fb_ref, gq_ref, gk_ref, do_ref, o_ref, lse_ref, dpsi_ref,
             dq_ref, dk_ref, dv_ref, dps_ref, dsm_ref, qs, ks, vs, ccol, crow, dos, dl, dqa, dcr, dcq):
        b, h = pl.program_id(0), pl.program_id(1)
        rq, _, f_in = _fox_prologue(q_ref, k_ref, v_ref, ps_ref, fb_ref, gq_ref, gk_ref, h, t, qs, ks, vs, ccol, crow)
        dov = do_ref[...]
        dos[...] = _bf(dov)
        dl[...] = jnp.broadcast_to(jnp.sum(dov * o_ref[...], axis=1, keepdims=True), (t, LANES))
        dqa[...] = jnp.zeros_like(dqa)
        dcq[...] = jnp.zeros_like(dcq)
        gkv = gk_ref[...]

        ksub = FOX_SHORT // FOX_BLOCK

        def kblock(j, dgk):
            rj = pl.ds(pl.multiple_of(j * FOX_SHORT, FOX_SHORT), FOX_SHORT)
            kj, vj = ks[rj, :], vs[rj, :]
            cr = jnp.concatenate([crow[j * ksub + u, 0:1, :] for u in range(ksub)], axis=1)

            def wide(x):
                return jnp.concatenate([x] * ksub, axis=1)

            def qtile(i, carry):
                dk_acc, dv_acc, dc = carry
                ri = pl.ds(pl.multiple_of(i * qt, qt), qt)
                qi, doi = qs[ri, :], dos[ri, :]
                s = _fox_scores(qi, kj, wide(ccol[ri, :]), cr, i * qt, j * FOX_SHORT)
                p = jnp.exp(s - wide(lse_ref[ri, :]))
                ds = p * (_dot_nt(doi, vj) - wide(dl[ri, :]))
                dsb = _bf(ds)
                dqa[ri, :] += _dot(dsb, kj)
                dcq[ri, :] += jnp.broadcast_to(jnp.sum(ds, axis=1, keepdims=True), (qt, LANES))
                return (dk_acc + _dot_tn(dsb, qi), dv_acc + _dot_tn(_bf(p), doi),
                        dc - jnp.sum(ds, axis=0, keepdims=True))

            z = jnp.zeros((FOX_SHORT, DH), F32)
            dk_acc, dv_acc, dc = lax.fori_loop((j * FOX_SHORT) // qt, t // qt, qtile,
                                               (z, z, jnp.zeros((1, FOX_SHORT), F32)))
            dv_ref[rj, :] = _bf(dv_acc)
            for u in range(ksub):
                dcr[pl.ds(pl.multiple_of((j * ksub + u) * 8, 8), 8), :] = jnp.broadcast_to(
                    dc[:, u * FOX_BLOCK:(u + 1) * FOX_BLOCK], (8, LANES))
            kraw = k_ref[rj, :]
            rk = lax.rsqrt(jnp.mean(kraw * kraw, axis=1, keepdims=True) + EPS)
            dkn = dk_acc * scale
            gy = dkn * gkv
            dk_ref[rj, :] = _bf(rk * gy - kraw * (rk * rk * rk * (1.0 / DH)) * jnp.sum(gy * kraw, axis=1, keepdims=True))
            return dgk + jnp.sum(dkn * kraw * rk, axis=0, keepdims=True)

        dgk = lax.fori_loop(0, t // FOX_SHORT, kblock, jnp.zeros((1, DH), F32))

        q = q_ref[...]
        dqn = dqa[...] * scale
        gy = dqn * gq_ref[...]
        dq_ref[...] = _bf(rq * gy - q * (rq * rq * rq * (1.0 / DH)) * jnp.sum(gy * q, axis=1, keepdims=True))
        dgq = jnp.sum(dqn * q * rq, axis=0, keepdims=True)

        r = lax.broadcasted_iota(jnp.int32, (FOX_BLOCK, FOX_BLOCK), 0)
        c = lax.broadcasted_iota(jnp.int32, (FOX_BLOCK, FOX_BLOCK), 1)
        triuf = (r <= c).astype(F32)

        def rev(jj, carry):
            j = nb - 1 - jj
            rows = pl.ds(pl.multiple_of(j * FOX_BLOCK, FOX_BLOCK), FOX_BLOCK)
            rowv = dcr[pl.ds(pl.multiple_of(j * 8, 8), 1), :]
            colv = jnp.sum(jnp.where(c >= r, jnp.broadcast_to(rowv, (FOX_BLOCK, LANES)), 0.0), axis=1, keepdims=True)
            qcol = dcq[rows, :]
            dl[rows, :] = colv + _dot_exact_l(triuf, qcol, terms=3) + carry
            return carry + jnp.sum(rowv, axis=1, keepdims=True) + jnp.sum(qcol, axis=0, keepdims=True)

        lax.fori_loop(0, nb, rev, jnp.zeros((1, LANES), F32))
        d_ff = dl[...] * _sigmoid(-f_in)
        lane = lax.broadcasted_iota(jnp.int32, (t, LANES), 1)

        @pl.when(h == 0)
        def _():
            dps_ref[...] = dpsi_ref[...]

        dps_ref[...] += jnp.where(lane == LANE_FF + h, d_ff, 0.0)

        lane1 = lax.broadcasted_iota(jnp.int32, (1, LANES), 1)
        d_fb = jnp.sum(d_ff, axis=0, keepdims=True)
        small = _stack_rows([dgq, dgk, jnp.where(lane1 == h, d_fb, 0.0)], 8)

        @pl.when((b == 0) & (h == 0))
        def _():
            dsm_ref[...] = jnp.zeros_like(dsm_ref)

        dsm_ref[...] += small

    ob_ = jax.ShapeDtypeStruct((n, HEADS * DH), BF16)
    return pl.pallas_call(
        body, name="fox_bwd", grid=(b_loc, HEADS),
        in_specs=[col(0), col(HEADS), col(2 * HEADS), ps_spec, smem, vec, vec, blk, blk, blk, ps_spec],
        out_specs=[blk, blk, blk, ps_spec, pl.BlockSpec((8, LANES), lambda b, h: (0, 0))],
        out_shape=[ob_, ob_, ob_, jax.ShapeDtypeStruct((n, LANES), F32), jax.ShapeDtypeStruct((8, LANES), F32)],
        scratch_shapes=([pltpu.VMEM((t, DH), BF16)] * 3 + [pltpu.VMEM((t, LANES), F32), pltpu.VMEM((nb, 8, LANES), F32)]
                        + [pltpu.VMEM((t, DH), BF16), pltpu.VMEM((t, LANES), F32), pltpu.VMEM((t, DH), F32),
                           pltpu.VMEM((8 * nb, LANES), F32), pltpu.VMEM((t, LANES), F32)]),
        compiler_params=_cparams(("arbitrary", "arbitrary")),
    )(pf, pf, pf, ps, f_bias, gq, gk, d_ob, ob, lse, dps_in)


class _NoExchange:
    def late_weights(self, after):
        return {}

    def grads_ready(self, grads, tie):
        return tie


def _local_step(x, target, w, b_loc, t, comm=None):
    comm = comm or _NoExchange()
    w = dict(w)
    xf = x
    u = _rms_fwd(xf, w["norm_mix_g"], "rms_mix")
    pg = _mm(u, w["w_gdn"], name="proj_gdn")
    pf = _mm(u, w["w_fox"], name="proj_fox")
    pgate = _mm(u, w["w_gate"], name="proj_gate")
    ps = _mm(u, w["w_small"], name="proj_small")
    oa, o_raw, s_all = _gdn_fwd(pg, ps, w["conv_w"], w["a_log"], w["dt_bias"], w["gdn_norm_g"], b_loc, t)
    ob, lse = _fox_fwd(pf, ps, w["f_bias"], w["fox_q_norm_g"], w["fox_k_norm_g"], b_loc, t)
    w.update(comm.late_weights(ob))
    ya = _mm(oa, w["w_proj_gdn"], name="proj_a")
    yb = _mm(ob, w["w_proj_fox"], name="proj_b")
    merged = _merge_fwd(ya, yb, pgate)
    h = _mm(merged, w["w_out"], name="proj_out", epi=lambda acc, xr: acc + xr, extras=(xf,))
    hn = _rms_fwd(h, w["norm_mlp_g"], "rms_mlp")
    up, act = _mm(hn, w["w_up"], name="mlp_up", out_dtype=BF16, out2=(_relu2, BF16))
    out = _mm(act, w["w_down"], name="mlp_down", epi=lambda acc, hr: acc + hr, extras=(h,))
    d_out, d_out16, loss_blk = _loss_bwd(out, target)

    g = {}
    g["w_down"] = _mm(act, d_out16, name="dw_down", ta=True, out_dtype=BF16)
    d_up = _mm(d_out16, w["w_down"], name="d_up", tb=True, out_dtype=BF16,
               epi=lambda acc, upr: acc * (2.0 * jnp.maximum(upr.astype(F32), 0.0)), extras=(up,))
    g["w_up"] = _mm(hn, d_up, name="dw_up", ta=True, out_dtype=BF16)
    mlp_gain = comm.grads_ready({"w_down": g["w_down"], "w_up": g["w_up"]}, w["norm_mlp_g"])
    d_hn = _mm(d_up, w["w_up"], name="d_hn", tb=True)
    dh, dh16, g["norm_mlp_g"] = _rms_bwd(d_hn, h, mlp_gain, d_out, "rms_mlp_bwd")
    g["w_out"] = _mm(merged, dh16, name="dw_out", ta=True, out_dtype=BF16)
    dm = _mm(dh16, w["w_out"], name="d_merged", tb=True)
    dya, dyb, dgate_a, dgate_b = _merge_bwd(dm, ya, yb, pgate)
    g["w_proj_gdn"] = _mm(oa, dya, name="dw_proj_a", ta=True, out_dtype=BF16)
    g["w_proj_fox"] = _mm(ob, dyb, name="dw_proj_b", ta=True, out_dtype=BF16)
    gdn_gain = comm.grads_ready({"w_out": g["w_out"], "w_proj_gdn": g["w_proj_gdn"], "w_proj_fox": g["w_proj_fox"]},
                                w["gdn_norm_g"])
    d_oa = _mm(dya, w["w_proj_gdn"], name="d_oa", tb=True)
    d_ob = _mm(dyb, w["w_proj_fox"], name="d_ob", tb=True)
    dgq, dgk, dgv, dgz, dps, dcw, gdn_small = _gdn_bwd(pg, ps, w["conv_w"], w["a_log"], w["dt_bias"], gdn_gain,
                                                       d_oa, o_raw, s_all, b_loc, t)
    dfq, dfk, dfv, dps, fox_small = _fox_bwd(pf, ps, w["f_bias"], w["fox_q_norm_g"], w["fox_k_norm_g"],
                                             d_ob, ob, lse, dps, b_loc, t)
    segs = [(dgq, "w_gdn", 0), (dgk, "w_gdn", 1024), (dgv, "w_gdn", 2048), (dgz, "w_gdn", 3072),
            (dfq, "w_fox", 0), (dfk, "w_fox", 1024), (dfv, "w_fox", 2048),
            (dgate_a, "w_gate", 0), (dgate_b, "w_gate", 1024)]
    dws = [_mm(u, dps, name="dw_small", ta=True, out_dtype=BF16)]
    dws += [_mm(u, dseg, name=f"dw_in_{idx}", ta=True, out_dtype=BF16) for idx, (dseg, _, _) in enumerate(segs)]
    g["w_in_parts"] = dws
    mix_gain = comm.grads_ready({"w_in_parts": dws}, w["norm_mix_g"])
    du = _du_all(dps, w["w_small"], segs, w)
    grad_x, _, g["norm_mix_g"] = _rms_bwd(du, xf, mix_gain, dh, "rms_mix_bwd")
    g["conv"] = dcw
    g["gdn_small"] = gdn_small
    g["fox_small"] = fox_small
    return loss_blk, grad_x, g


def _position():
    x, y, c = lax.axis_index("x"), lax.axis_index("y"), lax.axis_index("c")
    return x, y, c


def _to_bf16(arrs, name):
    n = len(arrs)

    def body(*refs):
        for i in range(n):
            refs[n + i][...] = _bf(refs[i][...])

    return pl.pallas_call(
        body, name=name,
        out_shape=[jax.ShapeDtypeStruct(a.shape, BF16) for a in arrs],
        compiler_params=_cparams(),
    )(*arrs)


def _all_gather(arrs, name):
    n = len(arrs)
    hbm = pl.BlockSpec(memory_space=pl.ANY)

    def body(*refs):
        ins, outs = refs[:n], refs[n:2 * n]
        send, recv, loc = refs[2 * n:]
        x, y, c = _position()
        me = 4 * x + 2 * y + c
        sibling = (x, y, 1 - c)
        chips = [(1 - x, y), (x, 1 - y), (1 - x, 1 - y)]

        def idx(px, py, pc):
            return 4 * px + 2 * py + pc

        def cp(a, k, block, to, src=None):
            return pltpu.make_async_remote_copy(
                src_ref=outs[a].at[block] if src is None else src, dst_ref=outs[a].at[block],
                send_sem=send.at[a, k], recv_sem=recv.at[a, k], device_id=to, device_id_type=MESH)

        mine = [pltpu.make_async_copy(ins[a], outs[a].at[me], loc.at[a]) for a in range(n)]
        for m in mine:
            m.start()
        first = []
        for a in range(n):
            first.append(cp(a, 0, me, sibling, src=ins[a]))
            first += [cp(a, 1 + j, me, (*chip, c), src=ins[a]) for j, chip in enumerate(chips)]
        for f in first:
            f.start()
        passed = []
        for j, chip in enumerate(chips):
            for a in range(n):
                cp(a, 1 + j, idx(*chip, c), (x, y, c)).wait_recv()
                p = cp(a, 4 + j, idx(*chip, c), sibling)
                p.start()
                passed.append(p)
        for a in range(n):
            cp(a, 0, idx(x, y, 1 - c), (x, y, c)).wait_recv()
            for j, chip in enumerate(chips):
                cp(a, 4 + j, idx(*chip, 1 - c), (x, y, c)).wait_recv()
        for f in first + passed:
            f.wait_send()
        for m in mine:
            m.wait()

    return pl.pallas_call(
        body, name=name,
        in_specs=[hbm] * n, out_specs=[hbm] * n,
        out_shape=[jax.ShapeDtypeStruct((N_DEV,) + a.shape, a.dtype) for a in arrs],
        scratch_shapes=[pltpu.SemaphoreType.DMA((n, 7)), pltpu.SemaphoreType.DMA((n, 7)), pltpu.SemaphoreType.DMA((n,))],
        compiler_params=pltpu.CompilerParams(has_side_effects=True),
    )(*arrs)


def _peer(x, y, c, rel):
    return ((1 - x) if rel & 4 else x, (1 - y) if rel & 2 else y, (1 - c) if rel & 1 else c)


HBM_SPEC = pl.BlockSpec(memory_space=pltpu.HBM)
SEM_SPEC = pl.BlockSpec(memory_space=pltpu.SEMAPHORE)
DATAFLOW = pltpu.SideEffectType.DATAFLOW_SIDE_EFFECTING


CHIP_RELS = (2, 4, 6)


def _push_start(arrs, slots, name, chips=False):
    n = len(arrs)
    n_slots = 4 if chips else N_DEV
    rels = CHIP_RELS if chips else tuple(range(1, N_DEV))
    land_shapes = [a.shape if slots else (n_slots,) + a.shape for a in arrs]

    def body(*refs):
        ins, lands, sends, recvs, token = refs[:n], refs[n:2 * n], refs[2 * n:3 * n], refs[3 * n:4 * n], refs[-1]
        x, y, c = _position()
        for rel in rels:
            px, py, pc = _peer(x, y, c, rel)
            mine, theirs = (2 * x + y, 2 * px + py) if chips else (4 * x + 2 * y + c, 4 * px + 2 * py + pc)
            for a in range(n):
                pltpu.make_async_remote_copy(
                    src_ref=ins[a].at[theirs] if slots else ins[a], dst_ref=lands[a].at[mine],
                    send_sem=sends[a], recv_sem=recvs[a], device_id=(px, py, pc), device_id_type=MESH).start()
        token[...] = jnp.zeros_like(token)

    sem = pltpu.SemaphoreType.DMA(())
    outs = pl.pallas_call(
        body, name=name,
        in_specs=[HBM_SPEC] * (2 * n),
        out_shape=(*[sem] * (2 * n), *[pltpu.HBM(a.shape, a.dtype) for a in arrs],
                   *[pltpu.HBM(s, a.dtype) for s, a in zip(land_shapes, arrs)], jax.ShapeDtypeStruct((8, LANES), F32)),
        out_specs=(*[SEM_SPEC] * (2 * n), *[HBM_SPEC] * (2 * n), pl.BlockSpec(memory_space=pltpu.VMEM)),
        input_output_aliases={i: 2 * n + i for i in range(2 * n)},
        compiler_params=pltpu.CompilerParams(has_side_effects=DATAFLOW),
    )(*[pltpu.with_memory_space_constraint(a, pltpu.HBM) for a in arrs],
      *[pltpu.with_memory_space_constraint(lax.empty(s, a.dtype), pltpu.HBM) for s, a in zip(land_shapes, arrs)])
    return dict(sends=list(outs[:n]), recvs=list(outs[n:2 * n]), ins=list(outs[2 * n:3 * n]),
                lands=list(outs[3 * n:4 * n]), token=outs[-1], copies=len(rels))


def _push_wait(started, after, name):
    n = len(started["ins"])
    copies = started["copies"]

    def body(*refs):
        lands, sends, recvs = refs[n:2 * n], refs[2 * n:3 * n], refs[3 * n:4 * n]
        x, y, c = _position()
        for a in range(n):
            every = lands[a].at[pl.ds(0, copies)]
            drain = pltpu.make_async_remote_copy(src_ref=every, dst_ref=every, send_sem=sends[a], recv_sem=recvs[a],
                                                 device_id=(x, y, c), device_id_type=MESH)
            drain.wait_send()
            drain.wait_recv()

    both = started["ins"] + started["lands"]
    outs = pl.pallas_call(
        body, name=name,
        in_specs=[HBM_SPEC] * (2 * n) + [SEM_SPEC] * (2 * n) + [pl.BlockSpec(memory_space=pl.ANY)],
        out_shape=tuple(pltpu.HBM(a.shape, a.dtype) for a in both), out_specs=tuple([HBM_SPEC] * (2 * n)),
        input_output_aliases={i: i for i in range(2 * n)},
        compiler_params=pltpu.CompilerParams(has_side_effects=DATAFLOW),
    )(*both, *started["sends"], *started["recvs"], after)
    return list(outs[:n]), list(outs[n:])


def _sibling_swap(arr, name):
    chips = N_DEV // 2

    def body(in_ref, out_ref, send, recv):
        x, y, c = _position()
        for s in range(chips):
            pltpu.make_async_remote_copy(src_ref=in_ref.at[2 * s + 1 - c], dst_ref=out_ref.at[s], send_sem=send,
                                         recv_sem=recv, device_id=(x, y, 1 - c), device_id_type=MESH).start()
        pltpu.make_async_remote_copy(src_ref=out_ref, dst_ref=out_ref, send_sem=send, recv_sem=recv,
                                     device_id=(x, y, 1 - c), device_id_type=MESH).wait()

    hbm = pl.BlockSpec(memory_space=pl.ANY)
    return pl.pallas_call(
        body, name=name, in_specs=[hbm], out_specs=hbm,
        out_shape=jax.ShapeDtypeStruct((chips,) + arr.shape[1:], arr.dtype),
        scratch_shapes=[pltpu.SemaphoreType.DMA, pltpu.SemaphoreType.DMA],
        compiler_params=pltpu.CompilerParams(has_side_effects=True),
    )(arr)


def _add_halves(core, arr, other, name):
    ns, r, c = other.shape
    tr = min(r, 256)

    def body(core_ref, a_ref, o_ref, out_ref):
        out_ref[...] = _bf(a_ref[...].astype(F32) + o_ref[...].astype(F32))

    blk = pl.BlockSpec((1, tr, c), lambda s, i, core_ref: (s, i, 0))
    return pl.pallas_call(
        body, name=name,
        grid_spec=pltpu.PrefetchScalarGridSpec(
            num_scalar_prefetch=1, grid=(ns, r // tr),
            in_specs=[pl.BlockSpec((1, tr, c), lambda s, i, core_ref: (2 * s + core_ref[0], i, 0)), blk],
            out_specs=blk),
        out_shape=jax.ShapeDtypeStruct((ns, r, c), BF16),
        compiler_params=_cparams(("parallel", "parallel")),
    )(core, arr, other)


def _all_reduce_small(buf, name):
    rows = buf.shape[0]

    def body(in_ref, out_ref, slots, send, recv):
        x, y, c = _position()
        me = 4 * x + 2 * y + c
        slots[me] = in_ref[...]
        copies = []
        for rel in range(1, N_DEV):
            copies.append(pltpu.make_async_remote_copy(
                src_ref=in_ref, dst_ref=slots.at[me], send_sem=send.at[rel - 1], recv_sem=recv.at[rel - 1],
                device_id=_peer(x, y, c, rel), device_id_type=MESH))
        for cpy in copies:
            cpy.start()
        for cpy in copies:
            cpy.wait()
        tot = slots[0]
        for d in range(1, N_DEV):
            tot = tot + slots[d]
        out_ref[...] = tot

    return pl.pallas_call(
        body, name=name,
        out_shape=jax.ShapeDtypeStruct((rows, LANES), F32),
        in_specs=[pl.BlockSpec(memory_space=pltpu.VMEM)], out_specs=pl.BlockSpec(memory_space=pltpu.VMEM),
        scratch_shapes=[pltpu.VMEM((N_DEV, rows, LANES), F32), pltpu.SemaphoreType.DMA((7,)),
                        pltpu.SemaphoreType.DMA((7,))],
        compiler_params=pltpu.CompilerParams(has_side_effects=True),
    )(buf)


def _adam_math(g, w, m, v):
    m = ADAM_B1 * m + (1.0 - ADAM_B1) * g
    v = ADAM_B2 * v + (1.0 - ADAM_B2) * (g * g)
    m_hat = m / (1.0 - ADAM_B1 ** ADAM_STEP)
    v_hat = v / (1.0 - ADAM_B2 ** ADAM_STEP)
    delta = -ADAM_LR * (m_hat / (jnp.sqrt(v_hat) + ADAM_EPS) + ADAM_WD * w)
    return delta, m, v


def _adam_shard(me, parts, mine, w, m, v, name):
    r, c = w.shape
    tr = min(r, 128)
    n_slots = parts.shape[0]

    def body(me_ref, p_ref, own_ref, w_ref, m_ref, v_ref, g_ref, d_ref, nm_ref, nv_ref):
        own = own_ref[0].astype(F32)
        g = None
        for s in range(n_slots):
            term = jnp.where(me_ref[0] == s, own, p_ref[s].astype(F32))
            g = term if g is None else g + term
        d, nm, nv = _adam_math(g, w_ref[...], m_ref[...], v_ref[...])
        g_ref[...] = g
        d_ref[...] = d
        nm_ref[...] = nm
        nv_ref[...] = nv

    row = pl.BlockSpec((tr, c), lambda i, me_ref: (i, 0))
    o = jax.ShapeDtypeStruct((r, c), F32)
    return pl.pallas_call(
        body, name=name,
        grid_spec=pltpu.PrefetchScalarGridSpec(
            num_scalar_prefetch=1, grid=(r // tr,),
            in_specs=[pl.BlockSpec((n_slots, tr, c), lambda i, me_ref: (0, i, 0)),
                      pl.BlockSpec((1, tr, c), lambda i, me_ref: (me_ref[0], i, 0)), row, row, row],
            out_specs=[row] * 4),
        out_shape=[o] * 4,
        compiler_params=_cparams(("parallel",)),
    )(me, parts, mine, w, m, v)


def _adam_small(g, w, m, v):
    def body(g_ref, w_ref, m_ref, v_ref, d_ref, nm_ref, nv_ref):
        d, nm, nv = _adam_math(g_ref[...], w_ref[...], m_ref[...], v_ref[...])
        d_ref[...] = d
        nm_ref[...] = nm
        nv_ref[...] = nv

    o = jax.ShapeDtypeStruct(g.shape, F32)
    return pl.pallas_call(body, name="adam_small", out_shape=[o] * 3, compiler_params=_cparams())(g, w, m, v)


def _split_w_in(w_full):
    o = IN_OFF
    w_gdn = w_full[:, o["gq"]:o["ga"]]
    w_fox = w_full[:, o["fq"]:o["ff"]]
    w_gate = w_full[:, o["gate_a"]:o["end"]]
    w_small = jnp.concatenate([w_full[:, o["ga"]:o["fq"]], w_full[:, o["ff"]:o["gate_a"]],
                               jnp.zeros((w_full.shape[0], LANES - 24), w_full.dtype)], axis=1)
    return w_gdn, w_fox, w_gate, w_small


def _w_in_pieces(g_in):
    nd, d, c = g_in.shape
    tr = 128
    widths = (IN_OFF["ga"] - IN_OFF["gq"], IN_OFF["ff"] - IN_OFF["fq"], IN_OFF["end"] - IN_OFF["gate_a"], LANES)

    def body(in_ref, gdn_ref, fox_ref, gate_ref, small_ref):
        full = jnp.concatenate([in_ref[dv] for dv in range(nd)], axis=1)
        for ref, piece in zip((gdn_ref, fox_ref, gate_ref, small_ref), _split_w_in(full)):
            ref[...] = piece

    return pl.pallas_call(
        body, name="w_in_pieces", grid=(d // tr,),
        in_specs=[pl.BlockSpec((nd, tr, c), lambda i: (0, i, 0))],
        out_specs=[pl.BlockSpec((tr, wd), lambda i: (i, 0)) for wd in widths],
        out_shape=[jax.ShapeDtypeStruct((d, wd), g_in.dtype) for wd in widths],
        compiler_params=_cparams(("parallel",)),
    )(g_in)


def _w_in_shards(parts, c):
    d = parts[0].shape[0]
    tr = 128

    def body(*refs):
        full = _join_w_in([r[...] for r in refs[:-1]])
        for dv in range(N_DEV):
            refs[-1][dv] = full[:, dv * c:(dv + 1) * c]

    return pl.pallas_call(
        body, name="w_in_shards", grid=(d // tr,),
        in_specs=[pl.BlockSpec((tr, p.shape[1]), lambda i: (i, 0)) for p in parts],
        out_specs=pl.BlockSpec((N_DEV, tr, c), lambda i: (0, i, 0)),
        out_shape=jax.ShapeDtypeStruct((N_DEV, d, c), parts[0].dtype),
        compiler_params=_cparams(("parallel",)),
    )(*parts)


def _join_w_in(parts):
    small = parts[0]
    return jnp.concatenate(parts[1:5] + [small[:, 0:16]] + parts[5:8] + [small[:, 16:24]] + parts[8:10], axis=1)


def _rows128(a, rows):
    flat = a.reshape(-1)
    flat = jnp.concatenate([flat, jnp.zeros((rows * LANES - flat.shape[0],), flat.dtype)])
    return flat.reshape(rows, LANES)


def kernel(x, norm_mix_g, w_in, gdn_conv_w, gdn_a_log, gdn_dt_bias, gdn_norm_g, fox_q_norm_g, fox_k_norm_g, fox_f_bias, w_proj_gdn, w_proj_fox, w_out, norm_mlp_g, w_up, w_down, loss_target, m_norm_mix_g, m_w_in, m_gdn_conv_w, m_gdn_a_log, m_gdn_dt_bias, m_gdn_norm_g, m_fox_q_norm_g, m_fox_k_norm_g, m_fox_f_bias, m_w_proj_gdn, m_w_proj_fox, m_w_out, m_norm_mlp_g, m_w_up, m_w_down, v_norm_mix_g, v_w_in, v_gdn_conv_w, v_gdn_a_log, v_gdn_dt_bias, v_gdn_norm_g, v_fox_q_norm_g, v_fox_k_norm_g, v_fox_f_bias, v_w_proj_gdn, v_w_proj_fox, v_w_out, v_norm_mlp_g, v_w_up, v_w_down):
    b_loc, t, d = x.shape
    n = b_loc * t
    me = 4 * lax.axis_index("x") + 2 * lax.axis_index("y") + lax.axis_index("c")

    late_names = ["w_proj_gdn", "w_proj_fox", "w_out", "w_up", "w_down"]
    big16 = _to_bf16([w_in[0], w_proj_gdn[0], w_proj_fox[0], w_out[0], w_up[0], w_down[0]], "weights_to_bf16")
    g_in, g_conv = _all_gather([big16[0], gdn_conv_w[0]], "gather_w_in")
    behind = (g_conv[0:1, 0, 0:1] * 0.0).astype(BF16)
    late = _push_start([big16[1] + behind] + list(big16[2:]), False, "gather_late_start")
    w_gdn, w_fox, w_gate, w_small = _w_in_pieces(g_in)
    weights = {
        "w_gdn": w_gdn, "w_fox": w_fox, "w_gate": w_gate, "w_small": w_small,
        "conv_w": g_conv.transpose(1, 0, 2).reshape(CONV_K, 3 * d),
        "norm_mix_g": norm_mix_g + late["token"][0:1, 0:1], "norm_mlp_g": norm_mlp_g, "a_log": gdn_a_log,
        "dt_bias": gdn_dt_bias, "gdn_norm_g": gdn_norm_g, "fox_q_norm_g": fox_q_norm_g, "fox_k_norm_g": fox_k_norm_g,
        "f_bias": fox_f_bias,
    }
    c_in, c_up = w_in.shape[2], w_up.shape[2]
    me1 = jnp.reshape(me, (1,)).astype(jnp.int32)
    chip1 = jnp.reshape(2 * lax.axis_index("x") + lax.axis_index("y"), (1,)).astype(jnp.int32)
    core1 = jnp.reshape(lax.axis_index("c"), (1,)).astype(jnp.int32)

    class _Exchange:
        def __init__(self):
            self.started = []

        def late_weights(self, after):
            shards, lands = _push_wait(late, after, "gather_late_wait")
            full = [lax.dynamic_update_index_in_dim(land, shard, me, 0) for land, shard in zip(lands, shards)]
            g_pa, g_pb, g_out, g_up, g_down = full
            return {"w_proj_gdn": g_pa.reshape(d, d), "w_proj_fox": g_pb.reshape(d, d), "w_out": g_out.reshape(d, d),
                    "w_up": g_up.transpose(1, 0, 2).reshape(d, D_FF), "w_down": g_down.reshape(D_FF, d)}

        def grads_ready(self, grads, tie):
            names = list(grads)
            if names == ["w_in_parts"]:
                halves = _w_in_shards(grads["w_in_parts"], c_in)
                other = _sibling_swap(halves, "grads_w_in_sibling")
                pair = _add_halves(core1, halves, other, "grads_w_in_pair")
                st = _push_start([pair], True, "grads_start_w_in_parts", chips=True)
            else:
                layout = {"w_up": lambda a: a.reshape(d, N_DEV, c_up).transpose(1, 0, 2),
                          "w_down": lambda a: a.reshape(N_DEV, D_FF // N_DEV, d)}
                arrs = [layout.get(k, lambda a: a.reshape(N_DEV, d // N_DEV, d))(grads[k]) for k in names]
                st = _push_start(arrs, True, "grads_start_" + names[0])
            self.started.append((names, st))
            return tie + st["token"][0:1, 0:1]

    comm = _Exchange()
    loss_blk, grad_x, g = _local_step(x.reshape(n, d), loss_target.reshape(n, d), weights, b_loc, t, comm)

    shards = {"w_in_parts": (w_in, m_w_in, v_w_in), "w_proj_gdn": (w_proj_gdn, m_w_proj_gdn, v_w_proj_gdn),
              "w_proj_fox": (w_proj_fox, m_w_proj_fox, v_w_proj_fox), "w_out": (w_out, m_w_out, v_w_out),
              "w_up": (w_up, m_w_up, v_w_up), "w_down": (w_down, m_w_down, v_w_down)}
    adam = {}

    def finish(names, st, after):
        mine, parts = _push_wait(st, after, "grads_wait_" + names[0])
        slot = chip1 if st["copies"] == len(CHIP_RELS) else me1
        for k, own, part in zip(names, mine, parts):
            wi, mi, vi = shards[k]
            adam[k] = [r[None] for r in _adam_shard(slot, part, own, wi[0], mi[0], vi[0], "adam_" + k)]

    for names, st in comm.started[:-1]:
        finish(names, st, grad_x)

    conv_rows = CONV_K * 3 * d // LANES
    conv_g = g["conv"].transpose(1, 0, 2).reshape(conv_rows, LANES)
    buf = jnp.concatenate([conv_g, g["norm_mix_g"].reshape(8, LANES), g["norm_mlp_g"].reshape(8, LANES),
                           g["gdn_small"], g["fox_small"], loss_blk], axis=0)
    anchor = sum(adam[k][1][0, 0:1, 0:LANES] for names, _ in comm.started[:-1] for k in names) * 0.0
    tot = _all_reduce_small(buf + anchor, "all_reduce_small")
    finish(*comm.started[-1], tot)
    big_out = [adam[k] for k in ["w_in_parts"] + late_names]
    o = conv_rows
    conv_full = tot[0:o].reshape(CONV_K, 3 * d)
    c_conv = gdn_conv_w.shape[2]
    g_conv_shard = lax.dynamic_slice(conv_full, (0, me * c_conv), (CONV_K, c_conv))
    g_mix = tot[o:o + 8].reshape(1, d)
    g_mlp = tot[o + 8:o + 16].reshape(1, d)
    gs, fs = tot[o + 16:o + 24], tot[o + 24:o + 32]
    loss = tot[o + 32, 0]
    small_g = [g_mix, g_conv_shard[None], gs[0:1, 0:HEADS], gs[1:2, 0:HEADS], gs[2:3], fs[0:1], fs[1:2], fs[2:3, 0:HEADS],
               g_mlp]
    small_w = [norm_mix_g, gdn_conv_w, gdn_a_log, gdn_dt_bias, gdn_norm_g, fox_q_norm_g, fox_k_norm_g, fox_f_bias,
               norm_mlp_g]
    small_m = [m_norm_mix_g, m_gdn_conv_w, m_gdn_a_log, m_gdn_dt_bias, m_gdn_norm_g, m_fox_q_norm_g, m_fox_k_norm_g,
               m_fox_f_bias, m_norm_mlp_g]
    small_v = [v_norm_mix_g, v_gdn_conv_w, v_gdn_a_log, v_gdn_dt_bias, v_gdn_norm_g, v_fox_q_norm_g, v_fox_k_norm_g,
               v_fox_f_bias, v_norm_mlp_g]
    row_counts = [-(-a.size // (8 * LANES)) * 8 for a in small_w]

    def pack(arrs):
        return jnp.concatenate([_rows128(a, rc) for a, rc in zip(arrs, row_counts)], axis=0)

    sd, sm, sv = _adam_small(pack(small_g), pack(small_w), pack(small_m), pack(small_v))

    def unpack(p):
        outs, r0 = [], 0
        for a, rc in zip(small_w, row_counts):
            outs.append(p[r0:r0 + rc].reshape(-1)[:a.size].reshape(a.shape))
            r0 += rc
        return outs

    small_out = [small_g_i.reshape(w_i.shape) for small_g_i, w_i in zip(small_g, small_w)], unpack(sd), unpack(sm), unpack(sv)

    def ordered(kind):
        s = small_out[kind]
        bo = [b[kind] for b in big_out]
        return [s[0], bo[0], s[1], s[2], s[3], s[4], s[5], s[6], s[7], bo[1], bo[2], bo[3], s[8], bo[4], bo[5]]

    return (loss, grad_x.reshape(b_loc, t, d), *ordered(0), *ordered(1), *ordered(2), *ordered(3))
```

```python
import functools

import jax
import jax.numpy as jnp
from jax import lax
from jax.experimental import pallas as pl
from jax.experimental.pallas import tpu as pltpu

F32 = jnp.float32
BF16 = jnp.bfloat16
MESH = pl.DeviceIdType.MESH

N_DEV = 8
D_MODEL = 1024
HEADS = 8
DH = 128
CONV_K = 4
CHUNK = 128
GDN_GROUP = 16
FOX_BLOCK = 128
FOX_TILE = 512
FOX_SHORT = 512
D_FF = 4 * D_MODEL
EPS = 1e-6
LANES = 128
NEG = -1e30
IN_OFF = {"gq": 0, "gk": 1024, "gv": 2048, "gz": 3072, "ga": 4096, "gb": 4104, "fq": 4112, "fk": 5136,
          "fv": 6160, "ff": 7184, "gate_a": 7192, "gate_b": 8216, "end": 9240}
LANE_GA, LANE_GB, LANE_FF = 0, 8, 16

ADAM_LR = 0.001
ADAM_B1 = 0.9
ADAM_B2 = 0.999
ADAM_EPS = 1e-08
ADAM_WD = 0.01
ADAM_STEP = 10

VMEM_LIMIT = 56 * 1024 * 1024


def _cparams(sem=None):
    return pltpu.CompilerParams(dimension_semantics=sem, vmem_limit_bytes=VMEM_LIMIT)


def _sigmoid(x):
    return 1.0 / (1.0 + jnp.exp(-x))


def _softplus(x):
    return jnp.maximum(x, 0.0) + jnp.log(1.0 + jnp.exp(-jnp.abs(x)))


def _dot(a, b, prec=None):
    return lax.dot_general(a, b, (((1,), (0,)), ((), ())), precision=prec, preferred_element_type=F32)


def _dot_nt(a, b, prec=None):
    return lax.dot_general(a, b, (((1,), (1,)), ((), ())), precision=prec, preferred_element_type=F32)


def _dot_tn(a, b, prec=None):
    return lax.dot_general(a, b, (((0,), (0,)), ((), ())), precision=prec, preferred_element_type=F32)


def _bf(x):
    return x.astype(BF16)


MM_TILE = 1024


def _mm(a, b, *, name, ta=False, tb=False, out_dtype=F32, epi=None, extras=(), out2=None,
        b_koff=0, tm=MM_TILE, tn=MM_TILE, tk=MM_TILE):
    m = a.shape[1] if ta else a.shape[0]
    kdim = a.shape[0] if ta else a.shape[1]
    n = b.shape[0] if tb else b.shape[1]
    tm, tn, tk = min(tm, m), min(tn, n), min(tk, kdim)
    nk = kdim // tk
    grid = (m // tm, n // tn, nk)
    koff = b_koff // tk
    a_spec = pl.BlockSpec((tk, tm), lambda i, j, k: (k, i)) if ta else pl.BlockSpec((tm, tk), lambda i, j, k: (i, k))
    if tb:
        b_spec = pl.BlockSpec((tn, tk), lambda i, j, k: (j, k + koff))
    else:
        b_spec = pl.BlockSpec((tk, tn), lambda i, j, k: (k + koff, j))
    o_spec = pl.BlockSpec((tm, tn), lambda i, j, k: (i, j))
    n_e = len(extras)
    n_o = 1 if out2 is None else 2
    dims = (((0 if ta else 1,), (1 if tb else 0,)), ((), ()))

    def body(a_ref, b_ref, *rest):
        e_refs, o_refs = rest[:n_e], rest[n_e:n_e + n_o]
        prod = lax.dot_general(_bf(a_ref[...]), _bf(b_ref[...]), dims, preferred_element_type=F32)

        def finish(r):
            if out2 is not None:
                o_refs[1][...] = out2[0](r).astype(out2[1])
            if epi is not None:
                r = epi(r, *[e[...] for e in e_refs])
            o_refs[0][...] = r.astype(out_dtype)

        if nk == 1:
            finish(prod)
        else:
            acc = rest[n_e + n_o]
            k = pl.program_id(2)

            @pl.when(k == 0)
            def _():
                acc[...] = prod

            @pl.when(k > 0)
            def _():
                acc[...] += prod

            @pl.when(k == nk - 1)
            def _():
                finish(acc[...])

    shapes = [jax.ShapeDtypeStruct((m, n), out_dtype)]
    if out2 is not None:
        shapes.append(jax.ShapeDtypeStruct((m, n), out2[1]))
    res = pl.pallas_call(
        body, name=name, grid=grid,
        in_specs=[a_spec, b_spec] + [o_spec] * n_e,
        out_specs=[o_spec] * n_o, out_shape=shapes,
        scratch_shapes=[] if nk == 1 else [pltpu.VMEM((tm, tn), F32)],
        compiler_params=_cparams(("parallel", "parallel", "arbitrary")),
    )(a, b, *extras)
    return res[0] if out2 is None else res


def _du_all(dps, w_small, segs, w, tm=512):
    n, d = dps.shape[0], w_small.shape[0]
    names = []
    for _, wname, _ in segs:
        if wname not in names:
            names.append(wname)
    first = {nm: min(i for i, s in enumerate(segs) if s[1] == nm) for nm in names}
    count = {nm: sum(1 for s in segs if s[1] == nm) for nm in names}
    n_seg, n_i = len(segs), n // tm

    def w_spec(nm):
        return pl.BlockSpec((d, d), lambda k, i: (0, jnp.clip(k - first[nm], 0, count[nm] - 1)))

    def rows_spec(cols, j):
        return pl.BlockSpec((tm, cols), lambda k, i: (jnp.where(k == j, i, jnp.where(k < j, 0, n_i - 1)), 0))

    def body(dps_ref, ws_ref, *rest):
        seg_refs, w_refs, o_ref, acc = rest[:n_seg], rest[n_seg:n_seg + len(names)], rest[-2], rest[-1]
        k, i = pl.program_id(0), pl.program_id(1)
        rows = pl.ds(pl.multiple_of(i * tm, tm), tm)

        @pl.when(k == 0)
        def _():
            acc[rows, :] = _dot_nt(_bf(dps_ref[...]), ws_ref[...])

        for idx, (_, wname, _) in enumerate(segs):
            @pl.when(k == idx)
            def _(idx=idx, wname=wname):
                acc[rows, :] += _dot_nt(seg_refs[idx][...], w_refs[names.index(wname)][...])

        @pl.when(k == n_seg - 1)
        def _():
            o_ref[...] = acc[rows, :]

    return pl.pallas_call(
        body, name="du_all", grid=(n_seg, n_i),
        in_specs=[rows_spec(dps.shape[1], 0), pl.BlockSpec(w_small.shape, lambda k, i: (0, 0))]
                 + [rows_spec(d, j) for j in range(n_seg)] + [w_spec(nm) for nm in names],
        out_specs=pl.BlockSpec((tm, d), lambda k, i: (jnp.where(k == n_seg - 1, i, 0), 0)),
        out_shape=jax.ShapeDtypeStruct((n, d), F32),
        scratch_shapes=[pltpu.VMEM((n, d), F32)],
        compiler_params=_cparams(("arbitrary", "arbitrary")),
    )(dps, w_small, *[s[0] for s in segs], *[w[nm] for nm in names])


def _relu2(x):
    r = jnp.maximum(x, 0.0)
    return r * r


ROWS = 512


def _rms_fwd(x, g, name):
    n, d = x.shape

    def body(x_ref, g_ref, u_ref):
        xv = x_ref[...]
        r = lax.rsqrt(jnp.mean(xv * xv, axis=1, keepdims=True) + EPS)
        u_ref[...] = _bf(xv * r * g_ref[...])

    return pl.pallas_call(
        body, name=name, grid=(n // ROWS,),
        in_specs=[pl.BlockSpec((ROWS, d), lambda i: (i, 0)), pl.BlockSpec((1, d), lambda i: (0, 0))],
        out_specs=pl.BlockSpec((ROWS, d), lambda i: (i, 0)),
        out_shape=jax.ShapeDtypeStruct((n, d), BF16),
        compiler_params=_cparams(("parallel",)),
    )(x, g)


def _rms_bwd(dy, x, g, dres, name):
    n, d = x.shape

    def body(dy_ref, x_ref, g_ref, dres_ref, dx_ref, dx16_ref, dg_ref):
        i = pl.program_id(0)
        xv, dyv = x_ref[...], dy_ref[...]
        r = lax.rsqrt(jnp.mean(xv * xv, axis=1, keepdims=True) + EPS)
        gy = dyv * g_ref[...]
        s = jnp.sum(gy * xv, axis=1, keepdims=True)
        dx = dres_ref[...] + r * gy - xv * (r * r * r * (1.0 / d)) * s
        dx_ref[...] = dx
        dx16_ref[...] = _bf(dx)

        @pl.when(i == 0)
        def _():
            dg_ref[...] = jnp.zeros_like(dg_ref)

        dg_ref[...] += jnp.sum(dyv * xv * r, axis=0, keepdims=True)

    row = pl.BlockSpec((ROWS, d), lambda i: (i, 0))
    vec = pl.BlockSpec((1, d), lambda i: (0, 0))
    return pl.pallas_call(
        body, name=name, grid=(n // ROWS,),
        in_specs=[row, row, vec, row], out_specs=[row, row, vec],
        out_shape=[jax.ShapeDtypeStruct((n, d), F32), jax.ShapeDtypeStruct((n, d), BF16),
                   jax.ShapeDtypeStruct((1, d), F32)],
        compiler_params=_cparams(("arbitrary",)),
    )(dy, x, g, dres)


def _merge_fwd(ya, yb, gate):
    n, d = ya.shape

    def body(ya_ref, yb_ref, ga_ref, gb_ref, o_ref):
        o_ref[...] = _bf(_sigmoid(ga_ref[...]) * ya_ref[...] + _sigmoid(gb_ref[...]) * yb_ref[...])

    row = pl.BlockSpec((ROWS, d), lambda i: (i, 0))
    return pl.pallas_call(
        body, name="merge_fwd", grid=(n // ROWS,),
        in_specs=[row, row, row, pl.BlockSpec((ROWS, d), lambda i: (i, 1))], out_specs=row,
        out_shape=jax.ShapeDtypeStruct((n, d), BF16),
        compiler_params=_cparams(("parallel",)),
    )(ya, yb, gate, gate)


def _merge_bwd(dm, ya, yb, gate):
    n, d = ya.shape

    def body(dm_ref, ya_ref, yb_ref, ga_ref, gb_ref, dya_ref, dyb_ref, dga_ref, dgb_ref):
        dmv = dm_ref[...]
        sa, sb = _sigmoid(ga_ref[...]), _sigmoid(gb_ref[...])
        dya_ref[...] = _bf(dmv * sa)
        dyb_ref[...] = _bf(dmv * sb)
        dga_ref[...] = _bf(dmv * ya_ref[...] * sa * (1.0 - sa))
        dgb_ref[...] = _bf(dmv * yb_ref[...] * sb * (1.0 - sb))

    row = pl.BlockSpec((ROWS, d), lambda i: (i, 0))
    o = jax.ShapeDtypeStruct((n, d), BF16)
    return pl.pallas_call(
        body, name="merge_bwd", grid=(n // ROWS,),
        in_specs=[row, row, row, row, pl.BlockSpec((ROWS, d), lambda i: (i, 1))], out_specs=[row] * 4,
        out_shape=[o] * 4,
        compiler_params=_cparams(("parallel",)),
    )(dm, ya, yb, gate, gate)


def _loss_bwd(out, target):
    n, d = out.shape

    def body(o_ref, t_ref, d_ref, d16_ref, l_ref):
        i = pl.program_id(0)
        err = o_ref[...] - t_ref[...]
        d_ref[...] = err * (1.0 / d)
        d16_ref[...] = _bf(err * (1.0 / d))

        @pl.when(i == 0)
        def _():
            l_ref[...] = jnp.zeros_like(l_ref)

        l_ref[...] += 0.5 * jnp.sum(jnp.mean(err * err, axis=1, keepdims=True), axis=0, keepdims=True)

    row = pl.BlockSpec((ROWS, d), lambda i: (i, 0))
    return pl.pallas_call(
        body, name="loss_bwd", grid=(n // ROWS,),
        in_specs=[row, row], out_specs=[row, row, pl.BlockSpec((8, LANES), lambda i: (0, 0))],
        out_shape=[jax.ShapeDtypeStruct((n, d), F32), jax.ShapeDtypeStruct((n, d), BF16),
                   jax.ShapeDtypeStruct((8, LANES), F32)],
        compiler_params=_cparams(("arbitrary",)),
    )(out, target)


PAD = 8


def _pad_zero(pad_ref):
    t = pad_ref.shape[0] - 2 * PAD
    pad_ref[0:PAD, :] = jnp.zeros((PAD, LANES), F32)
    pad_ref[PAD + t:2 * PAD + t, :] = jnp.zeros((PAD, LANES), F32)


def _shifted(pad_ref, s):
    t = pad_ref.shape[0] - 2 * PAD
    return pad_ref[PAD - s:PAD - s + t, :]


def _conv(x, w_ref, pad_ref):
    t = x.shape[0]
    pad_ref[PAD:PAD + t, :] = x
    y = _shifted(pad_ref, 3) * w_ref[0:1, :]
    y = y + _shifted(pad_ref, 2) * w_ref[1:2, :]
    y = y + _shifted(pad_ref, 1) * w_ref[2:3, :]
    return y + x * w_ref[3:4, :]


def _chunk_consts():
    r = lax.broadcasted_iota(jnp.int32, (CHUNK, CHUNK), 0)
    c = lax.broadcasted_iota(jnp.int32, (CHUNK, CHUNK), 1)
    incl, strict = r >= c, r > c
    return dict(incl=incl, strict=strict, trilf=incl.astype(F32), triuf=(r <= c).astype(F32),
                eye=(r == c).astype(F32))


class _V:
    def __init__(self, xs):
        self.xs = list(xs)

    def __add__(self, o):
        return _ap(lambda x, y: x + y, self, o)

    def __radd__(self, o):
        return _ap(lambda x, y: y + x, self, o)

    def __sub__(self, o):
        return _ap(lambda x, y: x - y, self, o)

    def __rsub__(self, o):
        return _ap(lambda x, y: y - x, self, o)

    def __mul__(self, o):
        return _ap(lambda x, y: x * y, self, o)

    def __rmul__(self, o):
        return _ap(lambda x, y: y * x, self, o)

    def __neg__(self):
        return _ap(lambda x: -x, self)

    def __getitem__(self, idx):
        return _ap(lambda x: x[idx], self)


def _ap(fn, *args):
    n = [len(a.xs) for a in args if isinstance(a, _V)]
    if not n:
        return fn(*args)
    return _V([fn(*[a.xs[i] if isinstance(a, _V) else a for a in args]) for i in range(n[0])])


def _vbf(x):
    return _ap(_bf, x)


def _vdot(a, b):
    return _ap(_dot, a, b)


def _vdot_nt(a, b):
    return _ap(_dot_nt, a, b)


def _vdot_tn(a, b):
    return _ap(_dot_tn, a, b)


def _vexp(x):
    return _ap(jnp.exp, x)


def _vsum(x, axis):
    return _ap(lambda v: jnp.sum(v, axis=axis, keepdims=True), x)


def _vcat(a, b, axis):
    return _ap(lambda x, y: jnp.concatenate([x, y], axis=axis), a, b)


def _vmask(mask, x):
    return _ap(lambda v: jnp.where(mask, v, 0.0), x)


def _split2(x):
    h = _vbf(x)
    return h, _vbf(x - _ap(lambda v: v.astype(F32), h))


def _dot3(a, b, kind=_vdot):
    ah, al = _split2(a)
    bh, bl = _split2(b)
    return kind(ah, bh) + (kind(ah, bl) + kind(al, bh))


def _split(x, terms):
    out = []
    for _ in range(terms):
        h = _vbf(x)
        out.append(h)
        x = x - _ap(lambda v: v.astype(F32), h)
    return out


def _dot_exact_l(m01, x, kind=_vdot, terms=2):
    mb = _bf(m01)
    parts = [kind(mb, xp) for xp in _split(x, terms)]
    return functools.reduce(lambda a, b: a + b, reversed(parts))


def _dot_exact_r(x, m01, kind=_vdot, terms=2):
    mb = _bf(m01)
    parts = [kind(xp, mb) for xp in _split(x, terms)]
    return functools.reduce(lambda a, b: a + b, reversed(parts))


def _inv_series(a, eye):
    m = eye.shape[0]
    levels = m.bit_length() - 1
    p = -a
    r = p + eye
    p = _dot3(p, p)
    for j in range(1, levels):
        if j < levels - 1:
            y = _dot3(p, _vcat(p, r, 1))
            p, r = y[:, 0:m], r + y[:, m:2 * m]
        else:
            r = r + _dot3(p, r)
    return r


def _inv_unit_lower(a, eye):
    return _inv_series(a, eye)


def _gdn_chunk_pre(q, k, v, g128, b128, cs):
    incl = cs["incl"]
    b64 = b128
    big_g = _dot_exact_l(cs["trilf"], g128)
    gc = big_g[:, 0:CHUNK]
    gr = _dot_exact_r(g128, cs["triuf"], _vdot_tn)
    decay = _ap(lambda d: jnp.where(incl, jnp.exp(jnp.where(incl, d, 0.0)), 0.0), gc - gr)
    kb, qb = _vbf(k), _vbf(q)
    qkk = _vdot_nt(_vcat(qb, kb, 0), kb)
    qk, kk = qkk[0:CHUNK], qkk[CHUNK:2 * CHUNK]
    tm = _inv_unit_lower(_vmask(cs["strict"], b64 * kk * decay), cs["eye"])
    e_g = _vexp(big_g)
    wu = _dot3(tm, _vcat(v * b128, k * (b128 * e_g), 1))
    w, u = wu[:, 0:DH], wu[:, DH:2 * DH]
    g_last = _vsum(g128, 0)
    return dict(big_g=big_g, decay=decay, kk=kk, qk=qk, tm=tm, w=w, u=u, p=qk * decay, q_dec=q * e_g,
                k_dec=k * _vexp(g_last - big_g), dec=_vexp(g_last))


def _gdn_chunk_post(q, k, v, g128, b128, s, ds_next, do, dv_new, big_g, decay, kk, qk, tm, u, v_new, cs):
    b64 = b128
    e_g = _vexp(big_g)
    vb = v * b128
    kbeta = k * (b128 * e_g)
    q_dec = q * e_g
    g_last = _vsum(g128, 0)
    ekg = _vexp(g_last - big_g)
    k_dec = k * ekg
    dec = _vexp(g_last)
    kb, qb, sb = _vbf(k), _vbf(q), _vbf(s)
    dob, dsb, vnb, dvnb = _vbf(do), _vbf(ds_next), _vbf(v_new), _vbf(dv_new)
    dp = _vmask(cs["incl"], _vdot_nt(dob, vnb))
    dq_dec = _vdot_nt(dob, sb)
    du = -_vdot_nt(dvnb, sb)
    ddec = _vsum(_vsum(s * ds_next, 1), 0)
    dk_dec = _vdot_nt(vnb, dsb)
    dwu = _vcat(dv_new, du, 1)
    dt = _dot3(dwu, _vcat(vb, kbeta, 1), _vdot_nt)
    dvk = _dot3(tm, dwu, _vdot_tn)
    dvb, dkbeta = dvk[:, 0:DH], dvk[:, DH:2 * DH]
    da = _vmask(cs["strict"], -_dot3(tm, _dot3(dt, tm, _vdot_nt), _vdot_tn))
    dkk = _vbf(da * b64 * decay)
    dqk = _vbf(dp * decay)
    ddd = (da * b64 * kk + dp * qk) * decay
    dq = _vdot(dqk, kb) + dq_dec * e_g
    dk = _vdot_tn(dqk, qb) + _vdot(dkk, kb) + _vdot_tn(dkk, kb) + dk_dec * ekg + dkbeta * (b128 * e_g)
    dv = dvb * b128
    dbeta = _vsum(da * kk * decay, 1) + _vsum(dvb * v, 1) + _vsum(dkbeta * k * e_g, 1)
    s_k = _vsum(dk_dec * k_dec, 1)
    dg_col = _vsum(ddd, 1) + _vsum(dq_dec * q_dec, 1) - s_k + _vsum(dkbeta * kbeta, 1)
    colsum = _dot_exact_r(ddd, jnp.ones((CHUNK, LANES), F32), _vdot_tn)
    dg_last = _vsum(s_k, 0) + ddec * dec
    dg = _dot_exact_l(cs["triuf"], dg_col - colsum) + dg_last
    return dq, dk, dv, dg, dbeta


def _stack_rows(vecs, nrows):
    row = lax.broadcasted_iota(jnp.int32, (nrows, LANES), 0)
    out = jnp.zeros((nrows, LANES), F32)
    for i, v in enumerate(vecs):
        out = out + jnp.where(row == i, jnp.broadcast_to(v, (nrows, LANES)), 0.0)
    return out


def _head_lane(x, lane_idx):
    lane = lax.broadcasted_iota(jnp.int32, x.shape, 1)
    return jnp.sum(jnp.where(lane == lane_idx, x, 0.0), axis=1, keepdims=True)


def _gdn_gates(ps, h, alog_ref, dtb_ref):
    ga = _head_lane(ps, LANE_GA + h)
    gb = _head_lane(ps, LANE_GB + h)
    a = jnp.exp(jnp.full((1, 1), alog_ref[0, h], F32))
    sp_in = ga + dtb_ref[0, h]
    g = -a * _softplus(sp_in)
    return g, _sigmoid(gb), a, sp_in


def _gdn_specs(b_loc, t):
    def col(off):
        return pl.BlockSpec((t, DH), lambda b, h: (b, off + h))

    ps_spec = pl.BlockSpec((t, LANES), lambda b, h: (b, 0))

    def wcol(off):
        return pl.BlockSpec((CONV_K, DH), lambda b, h: (0, off + h))

    smem = pl.BlockSpec(memory_space=pltpu.SMEM)
    vec = pl.BlockSpec((1, DH), lambda b, h: (0, 0))
    return col, ps_spec, wcol, smem, vec


def _gdn_fwd(pg, ps, convw, a_log, dt_bias, gnorm, b_loc, t):
    n = b_loc * t
    assert t % (CHUNK * GDN_GROUP) == 0 and CHUNK == LANES, (t, CHUNK, GDN_GROUP)
    nc = t // CHUNK
    col, ps_spec, wcol, smem, vec = _gdn_specs(b_loc, t)

    def body(q_ref, k_ref, v_ref, z_ref, ps_ref, wq_ref, wk_ref, wv_ref, alog_ref, dtb_ref, gn_ref,
             oa_ref, oraw_ref, s_ref, qn, kn, vv, g128, b128, uq_s, p_s, kd_s, dec_s, pad_s):
        h = pl.program_id(1)
        g, beta, _, _ = _gdn_gates(ps_ref[...], h, alog_ref, dtb_ref)
        g128[...] = jnp.broadcast_to(g, (t, LANES))
        b128[...] = jnp.broadcast_to(beta, (t, LANES))
        _pad_zero(pad_s)
        pq = _conv(q_ref[...], wq_ref, pad_s)
        yq = pq * _sigmoid(pq)
        qn[...] = yq * (lax.rsqrt(jnp.sum(yq * yq, axis=1, keepdims=True) + EPS) * (DH ** -0.5))
        pk = _conv(k_ref[...], wk_ref, pad_s)
        yk = pk * _sigmoid(pk)
        kn[...] = yk * lax.rsqrt(jnp.sum(yk * yk, axis=1, keepdims=True) + EPS)
        pv = _conv(v_ref[...], wv_ref, pad_s)
        vv[...] = pv * _sigmoid(pv)
        cs = _chunk_consts()

        def pre_group(gi, _):
            idx = [gi * GDN_GROUP + c for c in range(GDN_GROUP)]
            rows = [pl.ds(pl.multiple_of(i * CHUNK, CHUNK), CHUNK) for i in idx]
            ins = [_V([ref[r, :] for r in rows]) for ref in (qn, kn, vv, g128, b128)]
            f = _gdn_chunk_pre(*ins, cs)
            for c, (i, r) in enumerate(zip(idx, rows)):
                vv[r, :] = f["w"].xs[c]
                uq_s[i, 0:CHUNK, :] = _bf(f["u"].xs[c])
                uq_s[i, CHUNK:2 * CHUNK, :] = _bf(f["q_dec"].xs[c])
                p_s[r, :] = _bf(f["p"].xs[c])
                kd_s[r, :] = _bf(f["k_dec"].xs[c])
                dec_s[pl.ds(pl.multiple_of(i * 8, 8), 8), :] = jnp.broadcast_to(f["dec"].xs[c], (8, LANES))
            return 0

        lax.fori_loop(0, nc // GDN_GROUP, pre_group, 0)

        def chunk(i, s):
            r = pl.ds(pl.multiple_of(i * CHUNK, CHUNK), CHUNK)
            us = _dot(uq_s[i], _bf(s))
            vnb = _bf(vv[r, :] - us[0:CHUNK])
            oraw_ref[r, :] = us[CHUNK:2 * CHUNK] + _dot(p_s[r, :], vnb)
            s_ref[0, 0, i] = s
            return s * dec_s[pl.ds(pl.multiple_of(i * 8, 8), 1), :] + _dot_tn(kd_s[r, :], vnb)

        lax.fori_loop(0, nc, chunk, jnp.zeros((DH, DH), F32))
        o = oraw_ref[...]
        rr = lax.rsqrt(jnp.mean(o * o, axis=1, keepdims=True) + EPS)
        z = z_ref[...]
        oa_ref[...] = _bf((o * rr * gn_ref[...]) * (z * _sigmoid(z)))

    return pl.pallas_call(
        body, name="gdn_fwd", grid=(b_loc, HEADS),
        in_specs=[col(0), col(HEADS), col(2 * HEADS), col(3 * HEADS), ps_spec, wcol(0), wcol(HEADS), wcol(2 * HEADS),
                  smem, smem, vec],
        out_specs=[pl.BlockSpec((t, DH), lambda b, h: (b, h)), pl.BlockSpec((t, DH), lambda b, h: (b, h)),
                   pl.BlockSpec((1, 1, nc, DH, DH), lambda b, h: (b, h, 0, 0, 0))],
        out_shape=[jax.ShapeDtypeStruct((n, HEADS * DH), BF16), jax.ShapeDtypeStruct((n, HEADS * DH), F32),
                   jax.ShapeDtypeStruct((b_loc, HEADS, nc, DH, DH), F32)],
        scratch_shapes=([pltpu.VMEM((t, DH), F32)] * 3 + [pltpu.VMEM((t, LANES), F32)] * 2
                        + [pltpu.VMEM((nc, 2 * CHUNK, DH), BF16), pltpu.VMEM((t, CHUNK), BF16), pltpu.VMEM((t, DH), BF16),
                           pltpu.VMEM((8 * nc, LANES), F32), pltpu.VMEM((t + 2 * PAD, LANES), F32)]),
        compiler_params=_cparams(("arbitrary", "arbitrary")),
    )(pg, pg, pg, pg, ps, convw, convw, convw, a_log, dt_bias, gnorm)


def _gdn_bwd(pg, ps, convw, a_log, dt_bias, gnorm, d_oa, o_raw, s_all, b_loc, t):
    n = b_loc * t
    assert t % (CHUNK * GDN_GROUP) == 0 and CHUNK == LANES, (t, CHUNK, GDN_GROUP)
    nc = t // CHUNK
    col, ps_spec, wcol, smem, vec = _gdn_specs(b_loc, t)

    def body(q_ref, k_ref, v_ref, z_ref, ps_ref, wq_ref, wk_ref, wv_ref, alog_ref, dtb_ref, gn_ref,
             doa_ref, oraw_ref, s_ref,
             dq_ref, dk_ref, dv_ref, dz_ref, dps_ref, dcw_ref, dsm_ref,
             qn, kn, vv, g128, b128, do_s, bg_s, u_s, vn_s, dvn_s, dcy_s, kk_s, qk_s, tm_s, dsn_s, pad_s):
        b, h = pl.program_id(0), pl.program_id(1)
        g, beta, _, _ = _gdn_gates(ps_ref[...], h, alog_ref, dtb_ref)
        g128[...] = jnp.broadcast_to(g, (t, LANES))
        b128[...] = jnp.broadcast_to(beta, (t, LANES))
        _pad_zero(pad_s)

        def prep(x_ref, w_ref):
            p = _conv(x_ref[...], w_ref, pad_s)
            sg = _sigmoid(p)
            return p, sg, p * sg

        _, _, yq = prep(q_ref, wq_ref)
        qn[...] = yq * (lax.rsqrt(jnp.sum(yq * yq, axis=1, keepdims=True) + EPS) * (DH ** -0.5))
        _, _, yk = prep(k_ref, wk_ref)
        kn[...] = yk * lax.rsqrt(jnp.sum(yk * yk, axis=1, keepdims=True) + EPS)
        _, _, yv = prep(v_ref, wv_ref)
        vv[...] = yv

        o = oraw_ref[...]
        z = z_ref[...]
        doa = doa_ref[...]
        gn = gn_ref[...]
        ro = lax.rsqrt(jnp.mean(o * o, axis=1, keepdims=True) + EPS)
        sz = _sigmoid(z)
        dz_ref[...] = _bf(doa * (o * ro * gn) * (sz * (1.0 + z * (1.0 - sz))))
        dn = doa * (z * sz)
        dgn = jnp.sum(dn * o * ro, axis=0, keepdims=True)
        gy = dn * gn
        do_s[...] = ro * gy - o * (ro * ro * ro * (1.0 / DH)) * jnp.sum(gy * o, axis=1, keepdims=True)

        cs = _chunk_consts()

        def pre_group(gi, _):
            idx = [gi * GDN_GROUP + c for c in range(GDN_GROUP)]
            rows = [pl.ds(pl.multiple_of(i * CHUNK, CHUNK), CHUNK) for i in idx]
            ins = [_V([ref[r, :] for r in rows]) for ref in (qn, kn, vv, g128, b128)]
            states = _V([_bf(s_ref[0, 0, i]) for i in idx])
            f = _gdn_chunk_pre(*ins, cs)
            v_new = f["w"] - _vdot(_vbf(f["u"]), states)
            for c, r in enumerate(rows):
                bg_s[r, :] = f["big_g"].xs[c]
                u_s[r, :] = f["u"].xs[c]
                vn_s[r, :] = v_new.xs[c]
                dcy_s[r, :] = f["decay"].xs[c]
                kk_s[r, :] = f["kk"].xs[c]
                qk_s[r, :] = f["qk"].xs[c]
                tm_s[r, :] = f["tm"].xs[c]
            return 0

        lax.fori_loop(0, nc // GDN_GROUP, pre_group, 0)

        def chunk(j, ds):
            i = nc - 1 - j
            r = pl.ds(pl.multiple_of(i * CHUNK, CHUNK), CHUNK)
            big_g = bg_s[r, :]
            g_last = jnp.sum(g128[r, :], axis=0, keepdims=True)
            dob = _bf(do_s[r, :])
            dv_new = (_dot_tn(_bf(qk_s[r, :] * dcy_s[r, :]), dob)
                      + _dot(_bf(kn[r, :] * jnp.exp(g_last - big_g)), _bf(ds)))
            dvn_s[r, :] = dv_new
            dsn_s[i] = ds
            return (_dot_tn(_bf(qn[r, :] * jnp.exp(big_g)), dob) + jnp.exp(g_last) * ds
                    - _dot_tn(_bf(u_s[r, :]), _bf(dv_new)))

        lax.fori_loop(0, nc, chunk, jnp.zeros((DH, DH), F32))

        def post_group(gi, _):
            idx = [gi * GDN_GROUP + c for c in range(GDN_GROUP)]
            rows = [pl.ds(pl.multiple_of(i * CHUNK, CHUNK), CHUNK) for i in idx]
            def rows_of(ref):
                return _V([ref[r, :] for r in rows])

            dq, dk, dv, dg, dbeta = _gdn_chunk_post(
                rows_of(qn), rows_of(kn), rows_of(vv), rows_of(g128), rows_of(b128),
                _V([s_ref[0, 0, i] for i in idx]), _V([dsn_s[i] for i in idx]), rows_of(do_s), rows_of(dvn_s),
                rows_of(bg_s), rows_of(dcy_s), rows_of(kk_s), rows_of(qk_s), rows_of(tm_s), rows_of(u_s), rows_of(vn_s),
                cs)
            for c, r in enumerate(rows):
                qn[r, :] = dq.xs[c]
                kn[r, :] = dk.xs[c]
                vv[r, :] = dv.xs[c]
                g128[r, :] = dg.xs[c]
                b128[r, :] = jnp.broadcast_to(dbeta.xs[c], (CHUNK, LANES))
            return 0

        lax.fori_loop(0, nc // GDN_GROUP, post_group, 0)
        dqh, dkh, dvh = qn, kn, vv

        g, beta, a, sp_in = _gdn_gates(ps_ref[...], h, alog_ref, dtb_ref)
        dg = g128[...]
        d_ga = dg * (-a) * _sigmoid(sp_in)
        d_alog = jnp.sum(dg * g, axis=0, keepdims=True)
        d_dtb = jnp.sum(d_ga, axis=0, keepdims=True)
        d_gb = b128[...] * (beta * (1.0 - beta))
        lane = lax.broadcasted_iota(jnp.int32, (t, LANES), 1)
        contrib = jnp.where(lane == LANE_GA + h, d_ga, 0.0) + jnp.where(lane == LANE_GB + h, d_gb, 0.0)

        @pl.when(h == 0)
        def _():
            dps_ref[...] = jnp.zeros_like(dps_ref)

        dps_ref[...] += contrib

        lane1 = lax.broadcasted_iota(jnp.int32, (1, LANES), 1)
        small = _stack_rows([jnp.where(lane1 == h, d_alog, 0.0), jnp.where(lane1 == h, d_dtb, 0.0), dgn], 8)

        @pl.when((b == 0) & (h == 0))
        def _():
            dsm_ref[...] = jnp.zeros_like(dsm_ref)
            dcw_ref[...] = jnp.zeros_like(dcw_ref)

        dsm_ref[...] += small

        def conv_bwd(dp, x, w_ref, slot):
            dw = _stack_rows([jnp.sum(dp * _shifted(pad_s, 3), axis=0, keepdims=True),
                              jnp.sum(dp * _shifted(pad_s, 2), axis=0, keepdims=True),
                              jnp.sum(dp * _shifted(pad_s, 1), axis=0, keepdims=True),
                              jnp.sum(dp * x, axis=0, keepdims=True)], CONV_K)
            dcw_ref[slot] += dw
            pad_s[PAD:PAD + t, :] = dp
            dx = _shifted(pad_s, -3) * w_ref[0:1, :]
            dx = dx + _shifted(pad_s, -2) * w_ref[1:2, :]
            dx = dx + _shifted(pad_s, -1) * w_ref[2:3, :]
            return dx + dp * w_ref[3:4, :]

        def l2_bwd(dqn, y, c):
            r = lax.rsqrt(jnp.sum(y * y, axis=1, keepdims=True) + EPS)
            s1 = jnp.sum(dqn * y, axis=1, keepdims=True)
            return c * r * dqn - (c * r * r * r) * s1 * y

        def silu_bwd(p, sg):
            return sg * (1.0 + p * (1.0 - sg))

        pq, sq, yq = prep(q_ref, wq_ref)
        dq_ref[...] = _bf(conv_bwd(l2_bwd(dqh[...], yq, DH ** -0.5) * silu_bwd(pq, sq), q_ref[...], wq_ref, h))
        pk, sk, yk = prep(k_ref, wk_ref)
        dk_ref[...] = _bf(conv_bwd(l2_bwd(dkh[...], yk, 1.0) * silu_bwd(pk, sk), k_ref[...], wk_ref, HEADS + h))
        pv, sv, _ = prep(v_ref, wv_ref)
        dv_ref[...] = _bf(conv_bwd(dvh[...] * silu_bwd(pv, sv), v_ref[...], wv_ref, 2 * HEADS + h))

    blk = pl.BlockSpec((t, DH), lambda b, h: (b, h))
    ob = jax.ShapeDtypeStruct((n, HEADS * DH), BF16)
    return pl.pallas_call(
        body, name="gdn_bwd", grid=(b_loc, HEADS),
        in_specs=[col(0), col(HEADS), col(2 * HEADS), col(3 * HEADS), ps_spec, wcol(0), wcol(HEADS), wcol(2 * HEADS),
                  smem, smem, vec, blk, blk, pl.BlockSpec((1, 1, nc, DH, DH), lambda b, h: (b, h, 0, 0, 0))],
        out_specs=[blk, blk, blk, blk, ps_spec,
                   pl.BlockSpec((3 * HEADS, CONV_K, DH), lambda b, h: (0, 0, 0)),
                   pl.BlockSpec((8, LANES), lambda b, h: (0, 0))],
        out_shape=[ob, ob, ob, ob, jax.ShapeDtypeStruct((n, LANES), F32),
                   jax.ShapeDtypeStruct((3 * HEADS, CONV_K, DH), F32), jax.ShapeDtypeStruct((8, LANES), F32)],
        scratch_shapes=([pltpu.VMEM((t, DH), F32)] * 3 + [pltpu.VMEM((t, LANES), F32)] * 2
                        + [pltpu.VMEM((t, DH), F32)] * 5 + [pltpu.VMEM((t, CHUNK), F32)] * 4
                        + [pltpu.VMEM((nc, DH, DH), F32), pltpu.VMEM((t + 2 * PAD, LANES), F32)]),
        compiler_params=_cparams(("arbitrary", "arbitrary")),
    )(pg, pg, pg, pg, ps, convw, convw, convw, a_log, dt_bias, gnorm, d_oa, o_raw, s_all)


def _fox_prologue(q_ref, k_ref, v_ref, ps_ref, fb_ref, gq_ref, gk_ref, h, t, qs, ks, vs, ccol, crow):
    nb = t // FOX_BLOCK
    q, k = q_ref[...], k_ref[...]
    rq = lax.rsqrt(jnp.mean(q * q, axis=1, keepdims=True) + EPS)
    rk = lax.rsqrt(jnp.mean(k * k, axis=1, keepdims=True) + EPS)
    qs[...] = _bf(q * rq * gq_ref[...])
    ks[...] = _bf(k * rk * gk_ref[...])
    vs[...] = _bf(v_ref[...])
    f_in = _head_lane(ps_ref[...], LANE_FF + h) + fb_ref[0, h]
    ccol[...] = jnp.broadcast_to(-_softplus(-f_in), (t, LANES))
    r = lax.broadcasted_iota(jnp.int32, (FOX_BLOCK, FOX_BLOCK), 0)
    c = lax.broadcasted_iota(jnp.int32, (FOX_BLOCK, FOX_BLOCK), 1)
    trilf, triuf = (r >= c).astype(F32), (r <= c).astype(F32)
    blocks = [pl.ds(j * FOX_BLOCK, FOX_BLOCK) for j in range(nb)]
    lfs = _V([ccol[rb, :] for rb in blocks])
    cc = _dot_exact_l(trilf, lfs, terms=3)
    cr = _dot_exact_r(lfs, triuf, _vdot_tn, terms=3)
    sums = _vsum(lfs, 0)
    carry = jnp.zeros((1, LANES), F32)
    for j, rb in enumerate(blocks):
        ccol[rb, :] = cc.xs[j] + carry
        crow[j] = (cr.xs[j] + carry)[0:8]
        carry = carry + sums.xs[j]
    return rq, rk, f_in


def _fox_scores(q_rows, k_rows, cc, cr, row0, col0):
    s = _dot_nt(q_rows, k_rows) * (DH ** -0.5) + cc - cr
    r = lax.broadcasted_iota(jnp.int32, s.shape, 0)
    c = lax.broadcasted_iota(jnp.int32, s.shape, 1)
    return jnp.where(row0 + r >= col0 + c, s, NEG)


def _fox_specs(t):
    def col(off):
        return pl.BlockSpec((t, DH), lambda b, h: (b, off + h))

    ps_spec = pl.BlockSpec((t, LANES), lambda b, h: (b, 0))
    smem = pl.BlockSpec(memory_space=pltpu.SMEM)
    vec = pl.BlockSpec((1, DH), lambda b, h: (0, 0))
    blk = pl.BlockSpec((t, DH), lambda b, h: (b, h))
    return col, ps_spec, smem, vec, blk


def _fox_fwd(pf, ps, f_bias, gq, gk, b_loc, t):
    n = b_loc * t
    nb = t // FOX_BLOCK
    assert t % FOX_TILE == 0 and FOX_TILE % FOX_SHORT == 0, (t, FOX_TILE, FOX_SHORT)
    kt = FOX_TILE
    nsub = kt // FOX_BLOCK
    col, ps_spec, smem, vec, blk = _fox_specs(t)

    def body(q_ref, k_ref, v_ref, ps_ref, fb_ref, gq_ref, gk_ref, o_ref, lse_ref, qs, ks, vs, ccol, crow):
        h = pl.program_id(1)
        _fox_prologue(q_ref, k_ref, v_ref, ps_ref, fb_ref, gq_ref, gk_ref, h, t, qs, ks, vs, ccol, crow)

        def qblock(i, _):
            ri = pl.ds(pl.multiple_of(i * FOX_SHORT, FOX_SHORT), FOX_SHORT)
            qi = qs[ri, :]
            cc = jnp.concatenate([ccol[ri, :]] * nsub, axis=1)

            def ktile(j, carry):
                m, l, acc = carry
                rj = pl.ds(pl.multiple_of(j * kt, kt), kt)
                cr = jnp.concatenate([crow[j * nsub + u, 0:1, :] for u in range(nsub)], axis=1)
                s = _fox_scores(qi, ks[rj, :], cc, cr, i * FOX_SHORT, j * kt)
                m_new = jnp.maximum(m, jnp.max(s, axis=1, keepdims=True))
                p = jnp.exp(s - m_new)
                alpha = jnp.exp(m - m_new)
                l = alpha * l + jnp.sum(p, axis=1, keepdims=True)
                acc = alpha * acc + _dot(_bf(p), vs[rj, :])
                return m_new, l, acc

            m, l, acc = lax.fori_loop(0, (i * FOX_SHORT) // kt + 1, ktile, (jnp.full((FOX_SHORT, 1), NEG, F32),
                                                                            jnp.zeros((FOX_SHORT, 1), F32),
                                                                            jnp.zeros((FOX_SHORT, DH), F32)))
            o_ref[ri, :] = acc / l
            lse_ref[ri, :] = jnp.broadcast_to(m + jnp.log(l), (FOX_SHORT, LANES))
            return 0

        lax.fori_loop(0, t // FOX_SHORT, qblock, 0)

    o = jax.ShapeDtypeStruct((n, HEADS * DH), F32)
    return pl.pallas_call(
        body, name="fox_fwd", grid=(b_loc, HEADS),
        in_specs=[col(0), col(HEADS), col(2 * HEADS), ps_spec, smem, vec, vec],
        out_specs=[blk, blk], out_shape=[o, o],
        scratch_shapes=[pltpu.VMEM((t, DH), BF16)] * 3 + [pltpu.VMEM((t, LANES), F32), pltpu.VMEM((nb, 8, LANES), F32)],
        compiler_params=_cparams(("arbitrary", "arbitrary")),
    )(pf, pf, pf, ps, f_bias, gq, gk)


def _fox_bwd(pf, ps, f_bias, gq, gk, d_ob, ob, lse, dps_in, b_loc, t):
    n = b_loc * t
    nb = t // FOX_BLOCK
    assert t % FOX_TILE == 0 and FOX_TILE % FOX_SHORT == 0, (t, FOX_TILE, FOX_SHORT)
    qt = FOX_TILE
    scale = DH ** -0.5
    col, ps_spec, smem, vec, blk = _fox_specs(t)

    def body(q_ref, k_ref, v_ref, ps_ref, fb_ref, gq_ref, gk_ref, do_ref, o_ref, lse_ref, dpsi_ref,
             dq_ref, dk_ref, dv_ref, dps_ref, dsm_ref, qs, ks, vs, ccol, crow, dos, dl, dqa, dcr, dcq):
        b, h = pl.program_id(0), pl.program_id(1)
        rq, _, f_in = _fox_prologue(q_ref, k_ref, v_ref, ps_ref, fb_ref, gq_ref, gk_ref, h, t, qs, ks, vs, ccol, crow)
        dov = do_ref[...]
        dos[...] = _bf(dov)
        dl[...] = jnp.broadcast_to(jnp.sum(dov * o_ref[...], axis=1, keepdims=True), (t, LANES))
        dqa[...] = jnp.zeros_like(dqa)
        dcq[...] = jnp.zeros_like(dcq)
        gkv = gk_ref[...]

        ksub = FOX_SHORT // FOX_BLOCK

        def kblock(j, dgk):
            rj = pl.ds(pl.multiple_of(j * FOX_SHORT, FOX_SHORT), FOX_SHORT)
            kj, vj = ks[rj, :], vs[rj, :]
            cr = jnp.concatenate([crow[j * ksub + u, 0:1, :] for u in range(ksub)], axis=1)

            def wide(x):
                return jnp.concatenate([x] * ksub, axis=1)

            def qtile(i, carry):
                dk_acc, dv_acc, dc = carry
                ri = pl.ds(pl.multiple_of(i * qt, qt), qt)
                qi, doi = qs[ri, :], dos[ri, :]
                s = _fox_scores(qi, kj, wide(ccol[ri, :]), cr, i * qt, j * FOX_SHORT)
                p = jnp.exp(s - wide(lse_ref[ri, :]))
                ds = p * (_dot_nt(doi, vj) - wide(dl[ri, :]))
                dsb = _bf(ds)
                dqa[ri, :] += _dot(dsb, kj)
                dcq[ri, :] += jnp.broadcast_to(jnp.sum(ds, axis=1, keepdims=True), (qt, LANES))
                return (dk_acc + _dot_tn(dsb, qi), dv_acc + _dot_tn(_bf(p), doi),
                        dc - jnp.sum(ds, axis=0, keepdims=True))

            z = jnp.zeros((FOX_SHORT, DH), F32)
            dk_acc, dv_acc, dc = lax.fori_loop((j * FOX_SHORT) // qt, t // qt, qtile,
                                               (z, z, jnp.zeros((1, FOX_SHORT), F32)))
            dv_ref[rj, :] = _bf(dv_acc)
            for u in range(ksub):
                dcr[pl.ds(pl.multiple_of((j * ksub + u) * 8, 8), 8), :] = jnp.broadcast_to(
                    dc[:, u * FOX_BLOCK:(u + 1) * FOX_BLOCK], (8, LANES))
            kraw = k_ref[rj, :]
            rk = lax.rsqrt(jnp.mean(kraw * kraw, axis=1, keepdims=True) + EPS)
            dkn = dk_acc * scale
            gy = dkn * gkv
            dk_ref[rj, :] = _bf(rk * gy - kraw * (rk * rk * rk * (1.0 / DH)) * jnp.sum(gy * kraw, axis=1, keepdims=True))
            return dgk + jnp.sum(dkn * kraw * rk, axis=0, keepdims=True)

        dgk = lax.fori_loop(0, t // FOX_SHORT, kblock, jnp.zeros((1, DH), F32))

        q = q_ref[...]
        dqn = dqa[...] * scale
        gy = dqn * gq_ref[...]
        dq_ref[...] = _bf(rq * gy - q * (rq * rq * rq * (1.0 / DH)) * jnp.sum(gy * q, axis=1, keepdims=True))
        dgq = jnp.sum(dqn * q * rq, axis=0, keepdims=True)

        r = lax.broadcasted_iota(jnp.int32, (FOX_BLOCK, FOX_BLOCK), 0)
        c = lax.broadcasted_iota(jnp.int32, (FOX_BLOCK, FOX_BLOCK), 1)
        triuf = (r <= c).astype(F32)

        def rev(jj, carry):
            j = nb - 1 - jj
            rows = pl.ds(pl.multiple_of(j * FOX_BLOCK, FOX_BLOCK), FOX_BLOCK)
            rowv = dcr[pl.ds(pl.multiple_of(j * 8, 8), 1), :]
            colv = jnp.sum(jnp.where(c >= r, jnp.broadcast_to(rowv, (FOX_BLOCK, LANES)), 0.0), axis=1, keepdims=True)
            qcol = dcq[rows, :]
            dl[rows, :] = colv + _dot_exact_l(triuf, qcol, terms=3) + carry
            return carry + jnp.sum(rowv, axis=1, keepdims=True) + jnp.sum(qcol, axis=0, keepdims=True)

        lax.fori_loop(0, nb, rev, jnp.zeros((1, LANES), F32))
        d_ff = dl[...] * _sigmoid(-f_in)
        lane = lax.broadcasted_iota(jnp.int32, (t, LANES), 1)

        @pl.when(h == 0)
        def _():
            dps_ref[...] = dpsi_ref[...]

        dps_ref[...] += jnp.where(lane == LANE_FF + h, d_ff, 0.0)

        lane1 = lax.broadcasted_iota(jnp.int32, (1, LANES), 1)
        d_fb = jnp.sum(d_ff, axis=0, keepdims=True)
        small = _stack_rows([dgq, dgk, jnp.where(lane1 == h, d_fb, 0.0)], 8)

        @pl.when((b == 0) & (h == 0))
        def _():
            dsm_ref[...] = jnp.zeros_like(dsm_ref)

        dsm_ref[...] += small

    ob_ = jax.ShapeDtypeStruct((n, HEADS * DH), BF16)
    return pl.pallas_call(
        body, name="fox_bwd", grid=(b_loc, HEADS),
        in_specs=[col(0), col(HEADS), col(2 * HEADS), ps_spec, smem, vec, vec, blk, blk, blk, ps_spec],
        out_specs=[blk, blk, blk, ps_spec, pl.BlockSpec((8, LANES), lambda b, h: (0, 0))],
        out_shape=[ob_, ob_, ob_, jax.ShapeDtypeStruct((n, LANES), F32), jax.ShapeDtypeStruct((8, LANES), F32)],
        scratch_shapes=([pltpu.VMEM((t, DH), BF16)] * 3 + [pltpu.VMEM((t, LANES), F32), pltpu.VMEM((nb, 8, LANES), F32)]
                        + [pltpu.VMEM((t, DH), BF16), pltpu.VMEM((t, LANES), F32), pltpu.VMEM((t, DH), F32),
                           pltpu.VMEM((8 * nb, LANES), F32), pltpu.VMEM((t, LANES), F32)]),
        compiler_params=_cparams(("arbitrary", "arbitrary")),
    )(pf, pf, pf, ps, f_bias, gq, gk, d_ob, ob, lse, dps_in)


class _NoExchange:
    def late_weights(self, after):
        return {}

    def grads_ready(self, grads, tie):
        return tie


def _local_step(x, target, w, b_loc, t, comm=None):
    comm = comm or _NoExchange()
    w = dict(w)
    xf = x
    u = _rms_fwd(xf, w["norm_mix_g"], "rms_mix")
    pg = _mm(u, w["w_gdn"], name="proj_gdn")
    pf = _mm(u, w["w_fox"], name="proj_fox")
    pgate = _mm(u, w["w_gate"], name="proj_gate")
    ps = _mm(u, w["w_small"], name="proj_small")
    oa, o_raw, s_all = _gdn_fwd(pg, ps, w["conv_w"], w["a_log"], w["dt_bias"], w["gdn_norm_g"], b_loc, t)
    ob, lse = _fox_fwd(pf, ps, w["f_bias"], w["fox_q_norm_g"], w["fox_k_norm_g"], b_loc, t)
    w.update(comm.late_weights(ob))
    ya = _mm(oa, w["w_proj_gdn"], name="proj_a")
    yb = _mm(ob, w["w_proj_fox"], name="proj_b")
    merged = _merge_fwd(ya, yb, pgate)
    h = _mm(merged, w["w_out"], name="proj_out", epi=lambda acc, xr: acc + xr, extras=(xf,))
    hn = _rms_fwd(h, w["norm_mlp_g"], "rms_mlp")
    up, act = _mm(hn, w["w_up"], name="mlp_up", out_dtype=BF16, out2=(_relu2, BF16))
    out = _mm(act, w["w_down"], name="mlp_down", epi=lambda acc, hr: acc + hr, extras=(h,))
    d_out, d_out16, loss_blk = _loss_bwd(out, target)

    g = {}
    g["w_down"] = _mm(act, d_out16, name="dw_down", ta=True, out_dtype=BF16)
    d_up = _mm(d_out16, w["w_down"], name="d_up", tb=True, out_dtype=BF16,
               epi=lambda acc, upr: acc * (2.0 * jnp.maximum(upr.astype(F32), 0.0)), extras=(up,))
    g["w_up"] = _mm(hn, d_up, name="dw_up", ta=True, out_dtype=BF16)
    mlp_gain = comm.grads_ready({"w_down": g["w_down"], "w_up": g["w_up"]}, w["norm_mlp_g"])
    d_hn = _mm(d_up, w["w_up"], name="d_hn", tb=True)
    dh, dh16, g["norm_mlp_g"] = _rms_bwd(d_hn, h, mlp_gain, d_out, "rms_mlp_bwd")
    g["w_out"] = _mm(merged, dh16, name="dw_out", ta=True, out_dtype=BF16)
    dm = _mm(dh16, w["w_out"], name="d_merged", tb=True)
    dya, dyb, dgate_a, dgate_b = _merge_bwd(dm, ya, yb, pgate)
    g["w_proj_gdn"] = _mm(oa, dya, name="dw_proj_a", ta=True, out_dtype=BF16)
    g["w_proj_fox"] = _mm(ob, dyb, name="dw_proj_b", ta=True, out_dtype=BF16)
    gdn_gain = comm.grads_ready({"w_out": g["w_out"], "w_proj_gdn": g["w_proj_gdn"], "w_proj_fox": g["w_proj_fox"]},
                                w["gdn_norm_g"])
    d_oa = _mm(dya, w["w_proj_gdn"], name="d_oa", tb=True)
    d_ob = _mm(dyb, w["w_proj_fox"], name="d_ob", tb=True)
    dgq, dgk, dgv, dgz, dps, dcw, gdn_small = _gdn_bwd(pg, ps, w["conv_w"], w["a_log"], w["dt_bias"], gdn_gain,
                                                       d_oa, o_raw, s_all, b_loc, t)
    dfq, dfk, dfv, dps, fox_small = _fox_bwd(pf, ps, w["f_bias"], w["fox_q_norm_g"], w["fox_k_norm_g"],
                                             d_ob, ob, lse, dps, b_loc, t)
    segs = [(dgq, "w_gdn", 0), (dgk, "w_gdn", 1024), (dgv, "w_gdn", 2048), (dgz, "w_gdn", 3072),
            (dfq, "w_fox", 0), (dfk, "w_fox", 1024), (dfv, "w_fox", 2048),
            (dgate_a, "w_gate", 0), (dgate_b, "w_gate", 1024)]
    dws = [_mm(u, dps, name="dw_small", ta=True, out_dtype=BF16)]
    dws += [_mm(u, dseg, name=f"dw_in_{idx}", ta=True, out_dtype=BF16) for idx, (dseg, _, _) in enumerate(segs)]
    g["w_in_parts"] = dws
    mix_gain = comm.grads_ready({"w_in_parts": dws}, w["norm_mix_g"])
    du = _du_all(dps, w["w_small"], segs, w)
    grad_x, _, g["norm_mix_g"] = _rms_bwd(du, xf, mix_gain, dh, "rms_mix_bwd")
    g["conv"] = dcw
    g["gdn_small"] = gdn_small
    g["fox_small"] = fox_small
    return loss_blk, grad_x, g


def _position():
    x, y, c = lax.axis_index("x"), lax.axis_index("y"), lax.axis_index("c")
    return x, y, c


def _to_bf16(arrs, name):
    n = len(arrs)

    def body(*refs):
        for i in range(n):
            refs[n + i][...] = _bf(refs[i][...])

    return pl.pallas_call(
        body, name=name,
        out_shape=[jax.ShapeDtypeStruct(a.shape, BF16) for a in arrs],
        compiler_params=_cparams(),
    )(*arrs)


def _all_gather(arrs, name):
    n = len(arrs)
    hbm = pl.BlockSpec(memory_space=pl.ANY)

    def body(*refs):
        ins, outs = refs[:n], refs[n:2 * n]
        send, recv, loc = refs[2 * n:]
        x, y, c = _position()
        me = 4 * x + 2 * y + c
        sibling = (x, y, 1 - c)
        chips = [(1 - x, y), (x, 1 - y), (1 - x, 1 - y)]

        def idx(px, py, pc):
            return 4 * px + 2 * py + pc

        def cp(a, k, block, to, src=None):
            return pltpu.make_async_remote_copy(
                src_ref=outs[a].at[block] if src is None else src, dst_ref=outs[a].at[block],
                send_sem=send.at[a, k], recv_sem=recv.at[a, k], device_id=to, device_id_type=MESH)

        mine = [pltpu.make_async_copy(ins[a], outs[a].at[me], loc.at[a]) for a in range(n)]
        for m in mine:
            m.start()
        first = []
        for a in range(n):
            first.append(cp(a, 0, me, sibling, src=ins[a]))
            first += [cp(a, 1 + j, me, (*chip, c), src=ins[a]) for j, chip in enumerate(chips)]
        for f in first:
            f.start()
        passed = []
        for j, chip in enumerate(chips):
            for a in range(n):
                cp(a, 1 + j, idx(*chip, c), (x, y, c)).wait_recv()
                p = cp(a, 4 + j, idx(*chip, c), sibling)
                p.start()
                passed.append(p)
        for a in range(n):
            cp(a, 0, idx(x, y, 1 - c), (x, y, c)).wait_recv()
            for j, chip in enumerate(chips):
                cp(a, 4 + j, idx(*chip, 1 - c), (x, y, c)).wait_recv()
        for f in first + passed:
            f.wait_send()
        for m in mine:
            m.wait()

    return pl.pallas_call(
        body, name=name,
        in_specs=[hbm] * n, out_specs=[hbm] * n,
        out_shape=[jax.ShapeDtypeStruct((N_DEV,) + a.shape, a.dtype) for a in arrs],
        scratch_shapes=[pltpu.SemaphoreType.DMA((n, 7)), pltpu.SemaphoreType.DMA((n, 7)), pltpu.SemaphoreType.DMA((n,))],
        compiler_params=pltpu.CompilerParams(has_side_effects=True),
    )(*arrs)


def _peer(x, y, c, rel):
    return ((1 - x) if rel & 4 else x, (1 - y) if rel & 2 else y, (1 - c) if rel & 1 else c)


HBM_SPEC = pl.BlockSpec(memory_space=pltpu.HBM)
SEM_SPEC = pl.BlockSpec(memory_space=pltpu.SEMAPHORE)
DATAFLOW = pltpu.SideEffectType.DATAFLOW_SIDE_EFFECTING


CHIP_RELS = (2, 4, 6)


def _push_start(arrs, slots, name, chips=False):
    n = len(arrs)
    n_slots = 4 if chips else N_DEV
    rels = CHIP_RELS if chips else tuple(range(1, N_DEV))
    land_shapes = [a.shape if slots else (n_slots,) + a.shape for a in arrs]

    def body(*refs):
        ins, lands, sends, recvs, token = refs[:n], refs[n:2 * n], refs[2 * n:3 * n], refs[3 * n:4 * n], refs[-1]
        x, y, c = _position()
        for rel in rels:
            px, py, pc = _peer(x, y, c, rel)
            mine, theirs = (2 * x + y, 2 * px + py) if chips else (4 * x + 2 * y + c, 4 * px + 2 * py + pc)
            for a in range(n):
                pltpu.make_async_remote_copy(
                    src_ref=ins[a].at[theirs] if slots else ins[a], dst_ref=lands[a].at[mine],
                    send_sem=sends[a], recv_sem=recvs[a], device_id=(px, py, pc), device_id_type=MESH).start()
        token[...] = jnp.zeros_like(token)

    sem = pltpu.SemaphoreType.DMA(())
    outs = pl.pallas_call(
        body, name=name,
        in_specs=[HBM_SPEC] * (2 * n),
        out_shape=(*[sem] * (2 * n), *[pltpu.HBM(a.shape, a.dtype) for a in arrs],
                   *[pltpu.HBM(s, a.dtype) for s, a in zip(land_shapes, arrs)], jax.ShapeDtypeStruct((8, LANES), F32)),
        out_specs=(*[SEM_SPEC] * (2 * n), *[HBM_SPEC] * (2 * n), pl.BlockSpec(memory_space=pltpu.VMEM)),
        input_output_aliases={i: 2 * n + i for i in range(2 * n)},
        compiler_params=pltpu.CompilerParams(has_side_effects=DATAFLOW),
    )(*[pltpu.with_memory_space_constraint(a, pltpu.HBM) for a in arrs],
      *[pltpu.with_memory_space_constraint(lax.empty(s, a.dtype), pltpu.HBM) for s, a in zip(land_shapes, arrs)])
    return dict(sends=list(outs[:n]), recvs=list(outs[n:2 * n]), ins=list(outs[2 * n:3 * n]),
                lands=list(outs[3 * n:4 * n]), token=outs[-1], copies=len(rels))


def _push_wait(started, after, name):
    n = len(started["ins"])
    copies = started["copies"]

    def body(*refs):
        lands, sends, recvs = refs[n:2 * n], refs[2 * n:3 * n], refs[3 * n:4 * n]
        x, y, c = _position()
        for a in range(n):
            every = lands[a].at[pl.ds(0, copies)]
            drain = pltpu.make_async_remote_copy(src_ref=every, dst_ref=every, send_sem=sends[a], recv_sem=recvs[a],
                                                 device_id=(x, y, c), device_id_type=MESH)
            drain.wait_send()
            drain.wait_recv()

    both = started["ins"] + started["lands"]
    outs = pl.pallas_call(
        body, name=name,
        in_specs=[HBM_SPEC] * (2 * n) + [SEM_SPEC] * (2 * n) + [pl.BlockSpec(memory_space=pl.ANY)],
        out_shape=tuple(pltpu.HBM(a.shape, a.dtype) for a in both), out_specs=tuple([HBM_SPEC] * (2 * n)),
        input_output_aliases={i: i for i in range(2 * n)},
        compiler_params=pltpu.CompilerParams(has_side_effects=DATAFLOW),
    )(*both, *started["sends"], *started["recvs"], after)
    return list(outs[:n]), list(outs[n:])


def _sibling_swap(arr, name):
    chips = N_DEV // 2

    def body(in_ref, out_ref, send, recv):
        x, y, c = _position()
        for s in range(chips):
            pltpu.make_async_remote_copy(src_ref=in_ref.at[2 * s + 1 - c], dst_ref=out_ref.at[s], send_sem=send,
                                         recv_sem=recv, device_id=(x, y, 1 - c), device_id_type=MESH).start()
        pltpu.make_async_remote_copy(src_ref=out_ref, dst_ref=out_ref, send_sem=send, recv_sem=recv,
                                     device_id=(x, y, 1 - c), device_id_type=MESH).wait()

    hbm = pl.BlockSpec(memory_space=pl.ANY)
    return pl.pallas_call(
        body, name=name, in_specs=[hbm], out_specs=hbm,
        out_shape=jax.ShapeDtypeStruct((chips,) + arr.shape[1:], arr.dtype),
        scratch_shapes=[pltpu.SemaphoreType.DMA, pltpu.SemaphoreType.DMA],
        compiler_params=pltpu.CompilerParams(has_side_effects=True),
    )(arr)


def _add_halves(core, arr, other, name):
    ns, r, c = other.shape
    tr = min(r, 256)

    def body(core_ref, a_ref, o_ref, out_ref):
        out_ref[...] = _bf(a_ref[...].astype(F32) + o_ref[...].astype(F32))

    blk = pl.BlockSpec((1, tr, c), lambda s, i, core_ref: (s, i, 0))
    return pl.pallas_call(
        body, name=name,
        grid_spec=pltpu.PrefetchScalarGridSpec(
            num_scalar_prefetch=1, grid=(ns, r // tr),
            in_specs=[pl.BlockSpec((1, tr, c), lambda s, i, core_ref: (2 * s + core_ref[0], i, 0)), blk],
            out_specs=blk),
        out_shape=jax.ShapeDtypeStruct((ns, r, c), BF16),
        compiler_params=_cparams(("parallel", "parallel")),
    )(core, arr, other)


def _all_reduce_small(buf, name):
    rows = buf.shape[0]

    def body(in_ref, out_ref, slots, send, recv):
        x, y, c = _position()
        me = 4 * x + 2 * y + c
        slots[me] = in_ref[...]
        copies = []
        for rel in range(1, N_DEV):
            copies.append(pltpu.make_async_remote_copy(
                src_ref=in_ref, dst_ref=slots.at[me], send_sem=send.at[rel - 1], recv_sem=recv.at[rel - 1],
                device_id=_peer(x, y, c, rel), device_id_type=MESH))
        for cpy in copies:
            cpy.start()
        for cpy in copies:
            cpy.wait()
        tot = slots[0]
        for d in range(1, N_DEV):
            tot = tot + slots[d]
        out_ref[...] = tot

    return pl.pallas_call(
        body, name=name,
        out_shape=jax.ShapeDtypeStruct((rows, LANES), F32),
        in_specs=[pl.BlockSpec(memory_space=pltpu.VMEM)], out_specs=pl.BlockSpec(memory_space=pltpu.VMEM),
        scratch_shapes=[pltpu.VMEM((N_DEV, rows, LANES), F32), pltpu.SemaphoreType.DMA((7,)),
                        pltpu.SemaphoreType.DMA((7,))],
        compiler_params=pltpu.CompilerParams(has_side_effects=True),
    )(buf)


def _adam_math(g, w, m, v):
    m = ADAM_B1 * m + (1.0 - ADAM_B1) * g
    v = ADAM_B2 * v + (1.0 - ADAM_B2) * (g * g)
    m_hat = m / (1.0 - ADAM_B1 ** ADAM_STEP)
    v_hat = v / (1.0 - ADAM_B2 ** ADAM_STEP)
    delta = -ADAM_LR * (m_hat / (jnp.sqrt(v_hat) + ADAM_EPS) + ADAM_WD * w)
    return delta, m, v


def _adam_shard(me, parts, mine, w, m, v, name):
    r, c = w.shape
    tr = min(r, 128)
    n_slots = parts.shape[0]

    def body(me_ref, p_ref, own_ref, w_ref, m_ref, v_ref, g_ref, d_ref, nm_ref, nv_ref):
        own = own_ref[0].astype(F32)
        g = None
        for s in range(n_slots):
            term = jnp.where(me_ref[0] == s, own, p_ref[s].astype(F32))
            g = term if g is None else g + term
        d, nm, nv = _adam_math(g, w_ref[...], m_ref[...], v_ref[...])
        g_ref[...] = g
        d_ref[...] = d
        nm_ref[...] = nm
        nv_ref[...] = nv

    row = pl.BlockSpec((tr, c), lambda i, me_ref: (i, 0))
    o = jax.ShapeDtypeStruct((r, c), F32)
    return pl.pallas_call(
        body, name=name,
        grid_spec=pltpu.PrefetchScalarGridSpec(
            num_scalar_prefetch=1, grid=(r // tr,),
            in_specs=[pl.BlockSpec((n_slots, tr, c), lambda i, me_ref: (0, i, 0)),
                      pl.BlockSpec((1, tr, c), lambda i, me_ref: (me_ref[0], i, 0)), row, row, row],
            out_specs=[row] * 4),
        out_shape=[o] * 4,
        compiler_params=_cparams(("parallel",)),
    )(me, parts, mine, w, m, v)


def _adam_small(g, w, m, v):
    def body(g_ref, w_ref, m_ref, v_ref, d_ref, nm_ref, nv_ref):
        d, nm, nv = _adam_math(g_ref[...], w_ref[...], m_ref[...], v_ref[...])
        d_ref[...] = d
        nm_ref[...] = nm
        nv_ref[...] = nv

    o = jax.ShapeDtypeStruct(g.shape, F32)
    return pl.pallas_call(body, name="adam_small", out_shape=[o] * 3, compiler_params=_cparams())(g, w, m, v)


def _split_w_in(w_full):
    o = IN_OFF
    w_gdn = w_full[:, o["gq"]:o["ga"]]
    w_fox = w_full[:, o["fq"]:o["ff"]]
    w_gate = w_full[:, o["gate_a"]:o["end"]]
    w_small = jnp.concatenate([w_full[:, o["ga"]:o["fq"]], w_full[:, o["ff"]:o["gate_a"]],
                               jnp.zeros((w_full.shape[0], LANES - 24), w_full.dtype)], axis=1)
    return w_gdn, w_fox, w_gate, w_small


def _w_in_pieces(g_in):
    nd, d, c = g_in.shape
    tr = 128
    widths = (IN_OFF["ga"] - IN_OFF["gq"], IN_OFF["ff"] - IN_OFF["fq"], IN_OFF["end"] - IN_OFF["gate_a"], LANES)

    def body(in_ref, gdn_ref, fox_ref, gate_ref, small_ref):
        full = jnp.concatenate([in_ref[dv] for dv in range(nd)], axis=1)
        for ref, piece in zip((gdn_ref, fox_ref, gate_ref, small_ref), _split_w_in(full)):
            ref[...] = piece

    return pl.pallas_call(
        body, name="w_in_pieces", grid=(d // tr,),
        in_specs=[pl.BlockSpec((nd, tr, c), lambda i: (0, i, 0))],
        out_specs=[pl.BlockSpec((tr, wd), lambda i: (i, 0)) for wd in widths],
        out_shape=[jax.ShapeDtypeStruct((d, wd), g_in.dtype) for wd in widths],
        compiler_params=_cparams(("parallel",)),
    )(g_in)


def _w_in_shards(parts, c):
    d = parts[0].shape[0]
    tr = 128

    def body(*refs):
        full = _join_w_in([r[...] for r in refs[:-1]])
        for dv in range(N_DEV):
            refs[-1][dv] = full[:, dv * c:(dv + 1) * c]

    return pl.pallas_call(
        body, name="w_in_shards", grid=(d // tr,),
        in_specs=[pl.BlockSpec((tr, p.shape[1]), lambda i: (i, 0)) for p in parts],
        out_specs=pl.BlockSpec((N_DEV, tr, c), lambda i: (0, i, 0)),
        out_shape=jax.ShapeDtypeStruct((N_DEV, d, c), parts[0].dtype),
        compiler_params=_cparams(("parallel",)),
    )(*parts)


def _join_w_in(parts):
    small = parts[0]
    return jnp.concatenate(parts[1:5] + [small[:, 0:16]] + parts[5:8] + [small[:, 16:24]] + parts[8:10], axis=1)


def _rows128(a, rows):
    flat = a.reshape(-1)
    flat = jnp.concatenate([flat, jnp.zeros((rows * LANES - flat.shape[0],), flat.dtype)])
    return flat.reshape(rows, LANES)


def kernel(x, norm_mix_g, w_in, gdn_conv_w, gdn_a_log, gdn_dt_bias, gdn_norm_g, fox_q_norm_g, fox_k_norm_g, fox_f_bias, w_proj_gdn, w_proj_fox, w_out, norm_mlp_g, w_up, w_down, loss_target, m_norm_mix_g, m_w_in, m_gdn_conv_w, m_gdn_a_log, m_gdn_dt_bias, m_gdn_norm_g, m_fox_q_norm_g, m_fox_k_norm_g, m_fox_f_bias, m_w_proj_gdn, m_w_proj_fox, m_w_out, m_norm_mlp_g, m_w_up, m_w_down, v_norm_mix_g, v_w_in, v_gdn_conv_w, v_gdn_a_log, v_gdn_dt_bias, v_gdn_norm_g, v_fox_q_norm_g, v_fox_k_norm_g, v_fox_f_bias, v_w_proj_gdn, v_w_proj_fox, v_w_out, v_norm_mlp_g, v_w_up, v_w_down):
    b_loc, t, d = x.shape
    n = b_loc * t
    me = 4 * lax.axis_index("x") + 2 * lax.axis_index("y") + lax.axis_index("c")

    late_names = ["w_proj_gdn", "w_proj_fox", "w_out", "w_up", "w_down"]
    big16 = _to_bf16([w_in[0], w_proj_gdn[0], w_proj_fox[0], w_out[0], w_up[0], w_down[0]], "weights_to_bf16")
    g_in, g_conv = _all_gather([big16[0], gdn_conv_w[0]], "gather_w_in")
    behind = (g_conv[0:1, 0, 0:1] * 0.0).astype(BF16)
    late = _push_start([big16[1] + behind] + list(big16[2:]), False, "gather_late_start")
    w_gdn, w_fox, w_gate, w_small = _w_in_pieces(g_in)
    weights = {
        "w_gdn": w_gdn, "w_fox": w_fox, "w_gate": w_gate, "w_small": w_small,
        "conv_w": g_conv.transpose(1, 0, 2).reshape(CONV_K, 3 * d),
        "norm_mix_g": norm_mix_g + late["token"][0:1, 0:1], "norm_mlp_g": norm_mlp_g, "a_log": gdn_a_log,
        "dt_bias": gdn_dt_bias, "gdn_norm_g": gdn_norm_g, "fox_q_norm_g": fox_q_norm_g, "fox_k_norm_g": fox_k_norm_g,
        "f_bias": fox_f_bias,
    }
    c_in, c_up = w_in.shape[2], w_up.shape[2]
    me1 = jnp.reshape(me, (1,)).astype(jnp.int32)
    chip1 = jnp.reshape(2 * lax.axis_index("x") + lax.axis_index("y"), (1,)).astype(jnp.int32)
    core1 = jnp.reshape(lax.axis_index("c"), (1,)).astype(jnp.int32)

    class _Exchange:
        def __init__(self):
            self.started = []

        def late_weights(self, after):
            shards, lands = _push_wait(late, after, "gather_late_wait")
            full = [lax.dynamic_update_index_in_dim(land, shard, me, 0) for land, shard in zip(lands, shards)]
            g_pa, g_pb, g_out, g_up, g_down = full
            return {"w_proj_gdn": g_pa.reshape(d, d), "w_proj_fox": g_pb.reshape(d, d), "w_out": g_out.reshape(d, d),
                    "w_up": g_up.transpose(1, 0, 2).reshape(d, D_FF), "w_down": g_down.reshape(D_FF, d)}

        def grads_ready(self, grads, tie):
            names = list(grads)
            if names == ["w_in_parts"]:
                halves = _w_in_shards(grads["w_in_parts"], c_in)
                other = _sibling_swap(halves, "grads_w_in_sibling")
                pair = _add_halves(core1, halves, other, "grads_w_in_pair")
                st = _push_start([pair], True, "grads_start_w_in_parts", chips=True)
            else:
                layout = {"w_up": lambda a: a.reshape(d, N_DEV, c_up).transpose(1, 0, 2),
                          "w_down": lambda a: a.reshape(N_DEV, D_FF // N_DEV, d)}
                arrs = [layout.get(k, lambda a: a.reshape(N_DEV, d // N_DEV, d))(grads[k]) for k in names]
                st = _push_start(arrs, True, "grads_start_" + names[0])
            self.started.append((names, st))
            return tie + st["token"][0:1, 0:1]

    comm = _Exchange()
    loss_blk, grad_x, g = _local_step(x.reshape(n, d), loss_target.reshape(n, d), weights, b_loc, t, comm)

    shards = {"w_in_parts": (w_in, m_w_in, v_w_in), "w_proj_gdn": (w_proj_gdn, m_w_proj_gdn, v_w_proj_gdn),
              "w_proj_fox": (w_proj_fox, m_w_proj_fox, v_w_proj_fox), "w_out": (w_out, m_w_out, v_w_out),
              "w_up": (w_up, m_w_up, v_w_up), "w_down": (w_down, m_w_down, v_w_down)}
    adam = {}

    def finish(names, st, after):
        mine, parts = _push_wait(st, after, "grads_wait_" + names[0])
        slot = chip1 if st["copies"] == len(CHIP_RELS) else me1
        for k, own, part in zip(names, mine, parts):
            wi, mi, vi = shards[k]
            adam[k] = [r[None] for r in _adam_shard(slot, part, own, wi[0], mi[0], vi[0], "adam_" + k)]

    for names, st in comm.started[:-1]:
        finish(names, st, grad_x)

    conv_rows = CONV_K * 3 * d // LANES
    conv_g = g["conv"].transpose(1, 0, 2).reshape(conv_rows, LANES)
    buf = jnp.concatenate([conv_g, g["norm_mix_g"].reshape(8, LANES), g["norm_mlp_g"].reshape(8, LANES),
                           g["gdn_small"], g["fox_small"], loss_blk], axis=0)
    anchor = sum(adam[k][1][0, 0:1, 0:LANES] for names, _ in comm.started[:-1] for k in names) * 0.0
    tot = _all_reduce_small(buf + anchor, "all_reduce_small")
    finish(*comm.started[-1], tot)
    big_out = [adam[k] for k in ["w_in_parts"] + late_names]
    o = conv_rows
    conv_full = tot[0:o].reshape(CONV_K, 3 * d)
    c_conv = gdn_conv_w.shape[2]
    g_conv_shard = lax.dynamic_slice(conv_full, (0, me * c_conv), (CONV_K, c_conv))
    g_mix = tot[o:o + 8].reshape(1, d)
    g_mlp = tot[o + 8:o + 16].reshape(1, d)
    gs, fs = tot[o + 16:o + 24], tot[o + 24:o + 32]
    loss = tot[o + 32, 0]
    small_g = [g_mix, g_conv_shard[None], gs[0:1, 0:HEADS], gs[1:2, 0:HEADS], gs[2:3], fs[0:1], fs[1:2], fs[2:3, 0:HEADS],
               g_mlp]
    small_w = [norm_mix_g, gdn_conv_w, gdn_a_log, gdn_dt_bias, gdn_norm_g, fox_q_norm_g, fox_k_norm_g, fox_f_bias,
               norm_mlp_g]
    small_m = [m_norm_mix_g, m_gdn_conv_w, m_gdn_a_log, m_gdn_dt_bias, m_gdn_norm_g, m_fox_q_norm_g, m_fox_k_norm_g,
               m_fox_f_bias, m_norm_mlp_g]
    small_v = [v_norm_mix_g, v_gdn_conv_w, v_gdn_a_log, v_gdn_dt_bias, v_gdn_norm_g, v_fox_q_norm_g, v_fox_k_norm_g,
               v_fox_f_bias, v_norm_mlp_g]
    row_counts = [-(-a.size // (8 * LANES)) * 8 for a in small_w]

    def pack(arrs):
        return jnp.concatenate([_rows128(a, rc) for a, rc in zip(arrs, row_counts)], axis=0)

    sd, sm, sv = _adam_small(pack(small_g), pack(small_w), pack(small_m), pack(small_v))

    def unpack(p):
        outs, r0 = [], 0
        for a, rc in zip(small_w, row_counts):
            outs.append(p[r0:r0 + rc].reshape(-1)[:a.size].reshape(a.shape))
            r0 += rc
        return outs

    small_out = [small_g_i.reshape(w_i.shape) for small_g_i, w_i in zip(small_g, small_w)], unpack(sd), unpack(sm), unpack(sv)

    def ordered(kind):
        s = small_out[kind]
        bo = [b[kind] for b in big_out]
        return [s[0], bo[0], s[1], s[2], s[3], s[4], s[5], s[6], s[7], bo[1], bo[2], bo[3], s[8], bo[4], bo[5]]

    return (loss, grad_x.reshape(b_loc, t, d), *ordered(0), *ordered(1), *ordered(2), *ordered(3))
```

```python
import functools

import jax
import jax.numpy as jnp
from jax import lax
from jax.experimental import pallas as pl
from jax.experimental.pallas import tpu as pltpu

F32 = jnp.float32
BF16 = jnp.bfloat16
MESH = pl.DeviceIdType.MESH

N_DEV = 8
D_MODEL = 1024
HEADS = 8
DH = 128
CONV_K = 4
CHUNK = 128
GDN_GROUP = 16
FOX_BLOCK = 128
FOX_TILE = 512
FOX_SHORT = 512
D_FF = 4 * D_MODEL
EPS = 1e-6
LANES = 128
NEG = -1e30
IN_OFF = {"gq": 0, "gk": 1024, "gv": 2048, "gz": 3072, "ga": 4096, "gb": 4104, "fq": 4112, "fk": 5136,
          "fv": 6160, "ff": 7184, "gate_a": 7192, "gate_b": 8216, "end": 9240}
LANE_GA, LANE_GB, LANE_FF = 0, 8, 16

ADAM_LR = 0.001
ADAM_B1 = 0.9
ADAM_B2 = 0.999
ADAM_EPS = 1e-08
ADAM_WD = 0.01
ADAM_STEP = 10

VMEM_LIMIT = 56 * 1024 * 1024


def _cparams(sem=None):
    return pltpu.CompilerParams(dimension_semantics=sem, vmem_limit_bytes=VMEM_LIMIT)


def _sigmoid(x):
    return 1.0 / (1.0 + jnp.exp(-x))


def _softplus(x):
    return jnp.maximum(x, 0.0) + jnp.log(1.0 + jnp.exp(-jnp.abs(x)))


def _dot(a, b, prec=None):
    return lax.dot_general(a, b, (((1,), (0,)), ((), ())), precision=prec, preferred_element_type=F32)


def _dot_nt(a, b, prec=None):
    return lax.dot_general(a, b, (((1,), (1,)), ((), ())), precision=prec, preferred_element_type=F32)


def _dot_tn(a, b, prec=None):
    return lax.dot_general(a, b, (((0,), (0,)), ((), ())), precision=prec, preferred_element_type=F32)


def _bf(x):
    return x.astype(BF16)


MM_TILE = 1024


def _mm(a, b, *, name, ta=False, tb=False, out_dtype=F32, epi=None, extras=(), out2=None,
        b_koff=0, tm=MM_TILE, tn=MM_TILE, tk=MM_TILE):
    m = a.shape[1] if ta else a.shape[0]
    kdim = a.shape[0] if ta else a.shape[1]
    n = b.shape[0] if tb else b.shape[1]
    tm, tn, tk = min(tm, m), min(tn, n), min(tk, kdim)
    nk = kdim // tk
    grid = (m // tm, n // tn, nk)
    koff = b_koff // tk
    a_spec = pl.BlockSpec((tk, tm), lambda i, j, k: (k, i)) if ta else pl.BlockSpec((tm, tk), lambda i, j, k: (i, k))
    if tb:
        b_spec = pl.BlockSpec((tn, tk), lambda i, j, k: (j, k + koff))
    else:
        b_spec = pl.BlockSpec((tk, tn), lambda i, j, k: (k + koff, j))
    o_spec = pl.BlockSpec((tm, tn), lambda i, j, k: (i, j))
    n_e = len(extras)
    n_o = 1 if out2 is None else 2
    dims = (((0 if ta else 1,), (1 if tb else 0,)), ((), ()))

    def body(a_ref, b_ref, *rest):
        e_refs, o_refs = rest[:n_e], rest[n_e:n_e + n_o]
        prod = lax.dot_general(_bf(a_ref[...]), _bf(b_ref[...]), dims, preferred_element_type=F32)

        def finish(r):
            if out2 is not None:
                o_refs[1][...] = out2[0](r).astype(out2[1])
            if epi is not None:
                r = epi(r, *[e[...] for e in e_refs])
            o_refs[0][...] = r.astype(out_dtype)

        if nk == 1:
            finish(prod)
        else:
            acc = rest[n_e + n_o]
            k = pl.program_id(2)

            @pl.when(k == 0)
            def _():
                acc[...] = prod

            @pl.when(k > 0)
            def _():
                acc[...] += prod

            @pl.when(k == nk - 1)
            def _():
                finish(acc[...])

    shapes = [jax.ShapeDtypeStruct((m, n), out_dtype)]
    if out2 is not None:
        shapes.append(jax.ShapeDtypeStruct((m, n), out2[1]))
    res = pl.pallas_call(
        body, name=name, grid=grid,
        in_specs=[a_spec, b_spec] + [o_spec] * n_e,
        out_specs=[o_spec] * n_o, out_shape=shapes,
        scratch_shapes=[] if nk == 1 else [pltpu.VMEM((tm, tn), F32)],
        compiler_params=_cparams(("parallel", "parallel", "arbitrary")),
    )(a, b, *extras)
    return res[0] if out2 is None else res


def _du_all(dps, w_small, segs, w, tm=512):
    n, d = dps.shape[0], w_small.shape[0]
    names = []
    for _, wname, _ in segs:
        if wname not in names:
            names.append(wname)
    first = {nm: min(i for i, s in enumerate(segs) if s[1] == nm) for nm in names}
    count = {nm: sum(1 for s in segs if s[1] == nm) for nm in names}
    n_seg, n_i = len(segs), n // tm

    def w_spec(nm):
        return pl.BlockSpec((d, d), lambda k, i: (0, jnp.clip(k - first[nm], 0, count[nm] - 1)))

    def rows_spec(cols, j):
        return pl.BlockSpec((tm, cols), lambda k, i: (jnp.where(k == j, i, jnp.where(k < j, 0, n_i - 1)), 0))

    def body(dps_ref, ws_ref, *rest):
        seg_refs, w_refs, o_ref, acc = rest[:n_seg], rest[n_seg:n_seg + len(names)], rest[-2], rest[-1]
        k, i = pl.program_id(0), pl.program_id(1)
        rows = pl.ds(pl.multiple_of(i * tm, tm), tm)

        @pl.when(k == 0)
        def _():
            acc[rows, :] = _dot_nt(_bf(dps_ref[...]), ws_ref[...])

        for idx, (_, wname, _) in enumerate(segs):
            @pl.when(k == idx)
            def _(idx=idx, wname=wname):
                acc[rows, :] += _dot_nt(seg_refs[idx][...], w_refs[names.index(wname)][...])

        @pl.when(k == n_seg - 1)
        def _():
            o_ref[...] = acc[rows, :]

    return pl.pallas_call(
        body, name="du_all", grid=(n_seg, n_i),
        in_specs=[rows_spec(dps.shape[1], 0), pl.BlockSpec(w_small.shape, lambda k, i: (0, 0))]
                 + [rows_spec(d, j) for j in range(n_seg)] + [w_spec(nm) for nm in names],
        out_specs=pl.BlockSpec((tm, d), lambda k, i: (jnp.where(k == n_seg - 1, i, 0), 0)),
        out_shape=jax.ShapeDtypeStruct((n, d), F32),
        scratch_shapes=[pltpu.VMEM((n, d), F32)],
        compiler_params=_cparams(("arbitrary", "arbitrary")),
    )(dps, w_small, *[s[0] for s in segs], *[w[nm] for nm in names])


def _relu2(x):
    r = jnp.maximum(x, 0.0)
    return r * r


ROWS = 512


def _rms_fwd(x, g, name):
    n, d = x.shape

    def body(x_ref, g_ref, u_ref):
        xv = x_ref[...]
        r = lax.rsqrt(jnp.mean(xv * xv, axis=1, keepdims=True) + EPS)
        u_ref[...] = _bf(xv * r * g_ref[...])

    return pl.pallas_call(
        body, name=name, grid=(n // ROWS,),
        in_specs=[pl.BlockSpec((ROWS, d), lambda i: (i, 0)), pl.BlockSpec((1, d), lambda i: (0, 0))],
        out_specs=pl.BlockSpec((ROWS, d), lambda i: (i, 0)),
        out_shape=jax.ShapeDtypeStruct((n, d), BF16),
        compiler_params=_cparams(("parallel",)),
    )(x, g)


def _rms_bwd(dy, x, g, dres, name):
    n, d = x.shape

    def body(dy_ref, x_ref, g_ref, dres_ref, dx_ref, dx16_ref, dg_ref):
        i = pl.program_id(0)
        xv, dyv = x_ref[...], dy_ref[...]
        r = lax.rsqrt(jnp.mean(xv * xv, axis=1, keepdims=True) + EPS)
        gy = dyv * g_ref[...]
        s = jnp.sum(gy * xv, axis=1, keepdims=True)
        dx = dres_ref[...] + r * gy - xv * (r * r * r * (1.0 / d)) * s
        dx_ref[...] = dx
        dx16_ref[...] = _bf(dx)

        @pl.when(i == 0)
        def _():
            dg_ref[...] = jnp.zeros_like(dg_ref)

        dg_ref[...] += jnp.sum(dyv * xv * r, axis=0, keepdims=True)

    row = pl.BlockSpec((ROWS, d), lambda i: (i, 0))
    vec = pl.BlockSpec((1, d), lambda i: (0, 0))
    return pl.pallas_call(
        body, name=name, grid=(n // ROWS,),
        in_specs=[row, row, vec, row], out_specs=[row, row, vec],
        out_shape=[jax.ShapeDtypeStruct((n, d), F32), jax.ShapeDtypeStruct((n, d), BF16),
                   jax.ShapeDtypeStruct((1, d), F32)],
        compiler_params=_cparams(("arbitrary",)),
    )(dy, x, g, dres)


def _merge_fwd(ya, yb, gate):
    n, d = ya.shape

    def body(ya_ref, yb_ref, ga_ref, gb_ref, o_ref):
        o_ref[...] = _bf(_sigmoid(ga_ref[...]) * ya_ref[...] + _sigmoid(gb_ref[...]) * yb_ref[...])

    row = pl.BlockSpec((ROWS, d), lambda i: (i, 0))
    return pl.pallas_call(
        body, name="merge_fwd", grid=(n // ROWS,),
        in_specs=[row, row, row, pl.BlockSpec((ROWS, d), lambda i: (i, 1))], out_specs=row,
        out_shape=jax.ShapeDtypeStruct((n, d), BF16),
        compiler_params=_cparams(("parallel",)),
    )(ya, yb, gate, gate)


def _merge_bwd(dm, ya, yb, gate):
    n, d = ya.shape

    def body(dm_ref, ya_ref, yb_ref, ga_ref, gb_ref, dya_ref, dyb_ref, dga_ref, dgb_ref):
        dmv = dm_ref[...]
        sa, sb = _sigmoid(ga_ref[...]), _sigmoid(gb_ref[...])
        dya_ref[...] = _bf(dmv * sa)
        dyb_ref[...] = _bf(dmv * sb)
        dga_ref[...] = _bf(dmv * ya_ref[...] * sa * (1.0 - sa))
        dgb_ref[...] = _bf(dmv * yb_ref[...] * sb * (1.0 - sb))

    row = pl.BlockSpec((ROWS, d), lambda i: (i, 0))
    o = jax.ShapeDtypeStruct((n, d), BF16)
    return pl.pallas_call(
        body, name="merge_bwd", grid=(n // ROWS,),
        in_specs=[row, row, row, row, pl.BlockSpec((ROWS, d), lambda i: (i, 1))], out_specs=[row] * 4,
        out_shape=[o] * 4,
        compiler_params=_cparams(("parallel",)),
    )(dm, ya, yb, gate, gate)


def _loss_bwd(out, target):
    n, d = out.shape

    def body(o_ref, t_ref, d_ref, d16_ref, l_ref):
        i = pl.program_id(0)
        err = o_ref[...] - t_ref[...]
        d_ref[...] = err * (1.0 / d)
        d16_ref[...] = _bf(err * (1.0 / d))

        @pl.when(i == 0)
        def _():
            l_ref[...] = jnp.zeros_like(l_ref)

        l_ref[...] += 0.5 * jnp.sum(jnp.mean(err * err, axis=1, keepdims=True), axis=0, keepdims=True)

    row = pl.BlockSpec((ROWS, d), lambda i: (i, 0))
    return pl.pallas_call(
        body, name="loss_bwd", grid=(n // ROWS,),
        in_specs=[row, row], out_specs=[row, row, pl.BlockSpec((8, LANES), lambda i: (0, 0))],
        out_shape=[jax.ShapeDtypeStruct((n, d), F32), jax.ShapeDtypeStruct((n, d), BF16),
                   jax.ShapeDtypeStruct((8, LANES), F32)],
        compiler_params=_cparams(("arbitrary",)),
    )(out, target)


PAD = 8


def _pad_zero(pad_ref):
    t = pad_ref.shape[0] - 2 * PAD
    pad_ref[0:PAD, :] = jnp.zeros((PAD, LANES), F32)
    pad_ref[PAD + t:2 * PAD + t, :] = jnp.zeros((PAD, LANES), F32)


def _shifted(pad_ref, s):
    t = pad_ref.shape[0] - 2 * PAD
    return pad_ref[PAD - s:PAD - s + t, :]


def _conv(x, w_ref, pad_ref):
    t = x.shape[0]
    pad_ref[PAD:PAD + t, :] = x
    y = _shifted(pad_ref, 3) * w_ref[0:1, :]
    y = y + _shifted(pad_ref, 2) * w_ref[1:2, :]
    y = y + _shifted(pad_ref, 1) * w_ref[2:3, :]
    return y + x * w_ref[3:4, :]


def _chunk_consts():
    r = lax.broadcasted_iota(jnp.int32, (CHUNK, CHUNK), 0)
    c = lax.broadcasted_iota(jnp.int32, (CHUNK, CHUNK), 1)
    incl, strict = r >= c, r > c
    return dict(incl=incl, strict=strict, trilf=incl.astype(F32), triuf=(r <= c).astype(F32),
                eye=(r == c).astype(F32))


class _V:
    def __init__(self, xs):
        self.xs = list(xs)

    def __add__(self, o):
        return _ap(lambda x, y: x + y, self, o)

    def __radd__(self, o):
        return _ap(lambda x, y: y + x, self, o)

    def __sub__(self, o):
        return _ap(lambda x, y: x - y, self, o)

    def __rsub__(self, o):
        return _ap(lambda x, y: y - x, self, o)

    def __mul__(self, o):
        return _ap(lambda x, y: x * y, self, o)

    def __rmul__(self, o):
        return _ap(lambda x, y: y * x, self, o)

    def __neg__(self):
        return _ap(lambda x: -x, self)

    def __getitem__(self, idx):
        return _ap(lambda x: x[idx], self)


def _ap(fn, *args):
    n = [len(a.xs) for a in args if isinstance(a, _V)]
    if not n:
        return fn(*args)
    return _V([fn(*[a.xs[i] if isinstance(a, _V) else a for a in args]) for i in range(n[0])])


def _vbf(x):
    return _ap(_bf, x)


def _vdot(a, b):
    return _ap(_dot, a, b)


def _vdot_nt(a, b):
    return _ap(_dot_nt, a, b)


def _vdot_tn(a, b):
    return _ap(_dot_tn, a, b)


def _vexp(x):
    return _ap(jnp.exp, x)


def _vsum(x, axis):
    return _ap(lambda v: jnp.sum(v, axis=axis, keepdims=True), x)


def _vcat(a, b, axis):
    return _ap(lambda x, y: jnp.concatenate([x, y], axis=axis), a, b)


def _vmask(mask, x):
    return _ap(lambda v: jnp.where(mask, v, 0.0), x)


def _split2(x):
    h = _vbf(x)
    return h, _vbf(x - _ap(lambda v: v.astype(F32), h))


def _dot3(a, b, kind=_vdot):
    ah, al = _split2(a)
    bh, bl = _split2(b)
    return kind(ah, bh) + (kind(ah, bl) + kind(al, bh))


def _split(x, terms):
    out = []
    for _ in range(terms):
        h = _vbf(x)
        out.append(h)
        x = x - _ap(lambda v: v.astype(F32), h)
    return out


def _dot_exact_l(m01, x, kind=_vdot, terms=2):
    mb = _bf(m01)
    parts = [kind(mb, xp) for xp in _split(x, terms)]
    return functools.reduce(lambda a, b: a + b, reversed(parts))


def _dot_exact_r(x, m01, kind=_vdot, terms=2):
    mb = _bf(m01)
    parts = [kind(xp, mb) for xp in _split(x, terms)]
    return functools.reduce(lambda a, b: a + b, reversed(parts))


def _inv_series(a, eye):
    m = eye.shape[0]
    levels = m.bit_length() - 1
    p = -a
    r = p + eye
    p = _dot3(p, p)
    for j in range(1, levels):
        if j < levels - 1:
            y = _dot3(p, _vcat(p, r, 1))
            p, r = y[:, 0:m], r + y[:, m:2 * m]
        else:
            r = r + _dot3(p, r)
    return r


def _inv_unit_lower(a, eye):
    return _inv_series(a, eye)


def _gdn_chunk_pre(q, k, v, g128, b128, cs):
    incl = cs["incl"]
    b64 = b128
    big_g = _dot_exact_l(cs["trilf"], g128)
    gc = big_g[:, 0:CHUNK]
    gr = _dot_exact_r(g128, cs["triuf"], _vdot_tn)
    decay = _ap(lambda d: jnp.where(incl, jnp.exp(jnp.where(incl, d, 0.0)), 0.0), gc - gr)
    kb, qb = _vbf(k), _vbf(q)
    qkk = _vdot_nt(_vcat(qb, kb, 0), kb)
    qk, kk = qkk[0:CHUNK], qkk[CHUNK:2 * CHUNK]
    tm = _inv_unit_lower(_vmask(cs["strict"], b64 * kk * decay), cs["eye"])
    e_g = _vexp(big_g)
    wu = _dot3(tm, _vcat(v * b128, k * (b128 * e_g), 1))
    w, u = wu[:, 0:DH], wu[:, DH:2 * DH]
    g_last = _vsum(g128, 0)
    return dict(big_g=big_g, decay=decay, kk=kk, qk=qk, tm=tm, w=w, u=u, p=qk * decay, q_dec=q * e_g,
                k_dec=k * _vexp(g_last - big_g), dec=_vexp(g_last))


def _gdn_chunk_post(q, k, v, g128, b128, s, ds_next, do, dv_new, big_g, decay, kk, qk, tm, u, v_new, cs):
    b64 = b128
    e_g = _vexp(big_g)
    vb = v * b128
    kbeta = k * (b128 * e_g)
    q_dec = q * e_g
    g_last = _vsum(g128, 0)
    ekg = _vexp(g_last - big_g)
    k_dec = k * ekg
    dec = _vexp(g_last)
    kb, qb, sb = _vbf(k), _vbf(q), _vbf(s)
    dob, dsb, vnb, dvnb = _vbf(do), _vbf(ds_next), _vbf(v_new), _vbf(dv_new)
    dp = _vmask(cs["incl"], _vdot_nt(dob, vnb))
    dq_dec = _vdot_nt(dob, sb)
    du = -_vdot_nt(dvnb, sb)
    ddec = _vsum(_vsum(s * ds_next, 1), 0)
    dk_dec = _vdot_nt(vnb, dsb)
    dwu = _vcat(dv_new, du, 1)
    dt = _dot3(dwu, _vcat(vb, kbeta, 1), _vdot_nt)
    dvk = _dot3(tm, dwu, _vdot_tn)
    dvb, dkbeta = dvk[:, 0:DH], dvk[:, DH:2 * DH]
    da = _vmask(cs["strict"], -_dot3(tm, _dot3(dt, tm, _vdot_nt), _vdot_tn))
    dkk = _vbf(da * b64 * decay)
    dqk = _vbf(dp * decay)
    ddd = (da * b64 * kk + dp * qk) * decay
    dq = _vdot(dqk, kb) + dq_dec * e_g
    dk = _vdot_tn(dqk, qb) + _vdot(dkk, kb) + _vdot_tn(dkk, kb) + dk_dec * ekg + dkbeta * (b128 * e_g)
    dv = dvb * b128
    dbeta = _vsum(da * kk * decay, 1) + _vsum(dvb * v, 1) + _vsum(dkbeta * k * e_g, 1)
    s_k = _vsum(dk_dec * k_dec, 1)
    dg_col = _vsum(ddd, 1) + _vsum(dq_dec * q_dec, 1) - s_k + _vsum(dkbeta * kbeta, 1)
    colsum = _dot_exact_r(ddd, jnp.ones((CHUNK, LANES), F32), _vdot_tn)
    dg_last = _vsum(s_k, 0) + ddec * dec
    dg = _dot_exact_l(cs["triuf"], dg_col - colsum) + dg_last
    return dq, dk, dv, dg, dbeta


def _stack_rows(vecs, nrows):
    row = lax.broadcasted_iota(jnp.int32, (nrows, LANES), 0)
    out = jnp.zeros((nrows, LANES), F32)
    for i, v in enumerate(vecs):
        out = out + jnp.where(row == i, jnp.broadcast_to(v, (nrows, LANES)), 0.0)
    return out


def _head_lane(x, lane_idx):
    lane = lax.broadcasted_iota(jnp.int32, x.shape, 1)
    return jnp.sum(jnp.where(lane == lane_idx, x, 0.0), axis=1, keepdims=True)


def _gdn_gates(ps, h, alog_ref, dtb_ref):
    ga = _head_lane(ps, LANE_GA + h)
    gb = _head_lane(ps, LANE_GB + h)
    a = jnp.exp(jnp.full((1, 1), alog_ref[0, h], F32))
    sp_in = ga + dtb_ref[0, h]
    g = -a * _softplus(sp_in)
    return g, _sigmoid(gb), a, sp_in


def _gdn_specs(b_loc, t):
    def col(off):
        return pl.BlockSpec((t, DH), lambda b, h: (b, off + h))

    ps_spec = pl.BlockSpec((t, LANES), lambda b, h: (b, 0))

    def wcol(off):
        return pl.BlockSpec((CONV_K, DH), lambda b, h: (0, off + h))

    smem = pl.BlockSpec(memory_space=pltpu.SMEM)
    vec = pl.BlockSpec((1, DH), lambda b, h: (0, 0))
    return col, ps_spec, wcol, smem, vec


def _gdn_fwd(pg, ps, convw, a_log, dt_bias, gnorm, b_loc, t):
    n = b_loc * t
    assert t % (CHUNK * GDN_GROUP) == 0 and CHUNK == LANES, (t, CHUNK, GDN_GROUP)
    nc = t // CHUNK
    col, ps_spec, wcol, smem, vec = _gdn_specs(b_loc, t)

    def body(q_ref, k_ref, v_ref, z_ref, ps_ref, wq_ref, wk_ref, wv_ref, alog_ref, dtb_ref, gn_ref,
             oa_ref, oraw_ref, s_ref, qn, kn, vv, g128, b128, uq_s, p_s, kd_s, dec_s, pad_s):
        h = pl.program_id(1)
        g, beta, _, _ = _gdn_gates(ps_ref[...], h, alog_ref, dtb_ref)
        g128[...] = jnp.broadcast_to(g, (t, LANES))
        b128[...] = jnp.broadcast_to(beta, (t, LANES))
        _pad_zero(pad_s)
        pq = _conv(q_ref[...], wq_ref, pad_s)
        yq = pq * _sigmoid(pq)
        qn[...] = yq * (lax.rsqrt(jnp.sum(yq * yq, axis=1, keepdims=True) + EPS) * (DH ** -0.5))
        pk = _conv(k_ref[...], wk_ref, pad_s)
        yk = pk * _sigmoid(pk)
        kn[...] = yk * lax.rsqrt(jnp.sum(yk * yk, axis=1, keepdims=True) + EPS)
        pv = _conv(v_ref[...], wv_ref, pad_s)
        vv[...] = pv * _sigmoid(pv)
        cs = _chunk_consts()

        def pre_group(gi, _):
            idx = [gi * GDN_GROUP + c for c in range(GDN_GROUP)]
            rows = [pl.ds(pl.multiple_of(i * CHUNK, CHUNK), CHUNK) for i in idx]
            ins = [_V([ref[r, :] for r in rows]) for ref in (qn, kn, vv, g128, b128)]
            f = _gdn_chunk_pre(*ins, cs)
            for c, (i, r) in enumerate(zip(idx, rows)):
                vv[r, :] = f["w"].xs[c]
                uq_s[i, 0:CHUNK, :] = _bf(f["u"].xs[c])
                uq_s[i, CHUNK:2 * CHUNK, :] = _bf(f["q_dec"].xs[c])
                p_s[r, :] = _bf(f["p"].xs[c])
                kd_s[r, :] = _bf(f["k_dec"].xs[c])
                dec_s[pl.ds(pl.multiple_of(i * 8, 8), 8), :] = jnp.broadcast_to(f["dec"].xs[c], (8, LANES))
            return 0

        lax.fori_loop(0, nc // GDN_GROUP, pre_group, 0)

        def chunk(i, s):
            r = pl.ds(pl.multiple_of(i * CHUNK, CHUNK), CHUNK)
            us = _dot(uq_s[i], _bf(s))
            vnb = _bf(vv[r, :] - us[0:CHUNK])
            oraw_ref[r, :] = us[CHUNK:2 * CHUNK] + _dot(p_s[r, :], vnb)
            s_ref[0, 0, i] = s
            return s * dec_s[pl.ds(pl.multiple_of(i * 8, 8), 1), :] + _dot_tn(kd_s[r, :], vnb)

        lax.fori_loop(0, nc, chunk, jnp.zeros((DH, DH), F32))
        o = oraw_ref[...]
        rr = lax.rsqrt(jnp.mean(o * o, axis=1, keepdims=True) + EPS)
        z = z_ref[...]
        oa_ref[...] = _bf((o * rr * gn_ref[...]) * (z * _sigmoid(z)))

    return pl.pallas_call(
        body, name="gdn_fwd", grid=(b_loc, HEADS),
        in_specs=[col(0), col(HEADS), col(2 * HEADS), col(3 * HEADS), ps_spec, wcol(0), wcol(HEADS), wcol(2 * HEADS),
                  smem, smem, vec],
        out_specs=[pl.BlockSpec((t, DH), lambda b, h: (b, h)), pl.BlockSpec((t, DH), lambda b, h: (b, h)),
                   pl.BlockSpec((1, 1, nc, DH, DH), lambda b, h: (b, h, 0, 0, 0))],
        out_shape=[jax.ShapeDtypeStruct((n, HEADS * DH), BF16), jax.ShapeDtypeStruct((n, HEADS * DH), F32),
                   jax.ShapeDtypeStruct((b_loc, HEADS, nc, DH, DH), F32)],
        scratch_shapes=([pltpu.VMEM((t, DH), F32)] * 3 + [pltpu.VMEM((t, LANES), F32)] * 2
                        + [pltpu.VMEM((nc, 2 * CHUNK, DH), BF16), pltpu.VMEM((t, CHUNK), BF16), pltpu.VMEM((t, DH), BF16),
                           pltpu.VMEM((8 * nc, LANES), F32), pltpu.VMEM((t + 2 * PAD, LANES), F32)]),
        compiler_params=_cparams(("arbitrary", "arbitrary")),
    )(pg, pg, pg, pg, ps, convw, convw, convw, a_log, dt_bias, gnorm)


def _gdn_bwd(pg, ps, convw, a_log, dt_bias, gnorm, d_oa, o_raw, s_all, b_loc, t):
    n = b_loc * t
    assert t % (CHUNK * GDN_GROUP) == 0 and CHUNK == LANES, (t, CHUNK, GDN_GROUP)
    nc = t // CHUNK
    col, ps_spec, wcol, smem, vec = _gdn_specs(b_loc, t)

    def body(q_ref, k_ref, v_ref, z_ref, ps_ref, wq_ref, wk_ref, wv_ref, alog_ref, dtb_ref, gn_ref,
             doa_ref, oraw_ref, s_ref,
             dq_ref, dk_ref, dv_ref, dz_ref, dps_ref, dcw_ref, dsm_ref,
             qn, kn, vv, g128, b128, do_s, bg_s, u_s, vn_s, dvn_s, dcy_s, kk_s, qk_s, tm_s, dsn_s, pad_s):
        b, h = pl.program_id(0), pl.program_id(1)
        g, beta, _, _ = _gdn_gates(ps_ref[...], h, alog_ref, dtb_ref)
        g128[...] = jnp.broadcast_to(g, (t, LANES))
        b128[...] = jnp.broadcast_to(beta, (t, LANES))
        _pad_zero(pad_s)

        def prep(x_ref, w_ref):
            p = _conv(x_ref[...], w_ref, pad_s)
            sg = _sigmoid(p)
            return p, sg, p * sg

        _, _, yq = prep(q_ref, wq_ref)
        qn[...] = yq * (lax.rsqrt(jnp.sum(yq * yq, axis=1, keepdims=True) + EPS) * (DH ** -0.5))
        _, _, yk = prep(k_ref, wk_ref)
        kn[...] = yk * lax.rsqrt(jnp.sum(yk * yk, axis=1, keepdims=True) + EPS)
        _, _, yv = prep(v_ref, wv_ref)
        vv[...] = yv

        o = oraw_ref[...]
        z = z_ref[...]
        doa = doa_ref[...]
        gn = gn_ref[...]
        ro = lax.rsqrt(jnp.mean(o * o, axis=1, keepdims=True) + EPS)
        sz = _sigmoid(z)
        dz_ref[...] = _bf(doa * (o * ro * gn) * (sz * (1.0 + z * (1.0 - sz))))
        dn = doa * (z * sz)
        dgn = jnp.sum(dn * o * ro, axis=0, keepdims=True)
        gy = dn * gn
        do_s[...] = ro * gy - o * (ro * ro * ro * (1.0 / DH)) * jnp.sum(gy * o, axis=1, keepdims=True)

        cs = _chunk_consts()

        def pre_group(gi, _):
            idx = [gi * GDN_GROUP + c for c in range(GDN_GROUP)]
            rows = [pl.ds(pl.multiple_of(i * CHUNK, CHUNK), CHUNK) for i in idx]
            ins = [_V([ref[r, :] for r in rows]) for ref in (qn, kn, vv, g128, b128)]
            states = _V([_bf(s_ref[0, 0, i]) for i in idx])
            f = _gdn_chunk_pre(*ins, cs)
            v_new = f["w"] - _vdot(_vbf(f["u"]), states)
            for c, r in enumerate(rows):
                bg_s[r, :] = f["big_g"].xs[c]
                u_s[r, :] = f["u"].xs[c]
                vn_s[r, :] = v_new.xs[c]
                dcy_s[r, :] = f["decay"].xs[c]
                kk_s[r, :] = f["kk"].xs[c]
                qk_s[r, :] = f["qk"].xs[c]
                tm_s[r, :] = f["tm"].xs[c]
            return 0

        lax.fori_loop(0, nc // GDN_GROUP, pre_group, 0)

        def chunk(j, ds):
            i = nc - 1 - j
            r = pl.ds(pl.multiple_of(i * CHUNK, CHUNK), CHUNK)
            big_g = bg_s[r, :]
            g_last = jnp.sum(g128[r, :], axis=0, keepdims=True)
            dob = _bf(do_s[r, :])
            dv_new = (_dot_tn(_bf(qk_s[r, :] * dcy_s[r, :]), dob)
                      + _dot(_bf(kn[r, :] * jnp.exp(g_last - big_g)), _bf(ds)))
            dvn_s[r, :] = dv_new
            dsn_s[i] = ds
            return (_dot_tn(_bf(qn[r, :] * jnp.exp(big_g)), dob) + jnp.exp(g_last) * ds
                    - _dot_tn(_bf(u_s[r, :]), _bf(dv_new)))

        lax.fori_loop(0, nc, chunk, jnp.zeros((DH, DH), F32))

        def post_group(gi, _):
            idx = [gi * GDN_GROUP + c for c in range(GDN_GROUP)]
            rows = [pl.ds(pl.multiple_of(i * CHUNK, CHUNK), CHUNK) for i in idx]
            def rows_of(ref):
                return _V([ref[r, :] for r in rows])

            dq, dk, dv, dg, dbeta = _gdn_chunk_post(
                rows_of(qn), rows_of(kn), rows_of(vv), rows_of(g128), rows_of(b128),
                _V([s_ref[0, 0, i] for i in idx]), _V([dsn_s[i] for i in idx]), rows_of(do_s), rows_of(dvn_s),
                rows_of(bg_s), rows_of(dcy_s), rows_of(kk_s), rows_of(qk_s), rows_of(tm_s), rows_of(u_s), rows_of(vn_s),
                cs)
            for c, r in enumerate(rows):
                qn[r, :] = dq.xs[c]
                kn[r, :] = dk.xs[c]
                vv[r, :] = dv.xs[c]
                g128[r, :] = dg.xs[c]
                b128[r, :] = jnp.broadcast_to(dbeta.xs[c], (CHUNK, LANES))
            return 0

        lax.fori_loop(0, nc // GDN_GROUP, post_group, 0)
        dqh, dkh, dvh = qn, kn, vv

        g, beta, a, sp_in = _gdn_gates(ps_ref[...], h, alog_ref, dtb_ref)
        dg = g128[...]
        d_ga = dg * (-a) * _sigmoid(sp_in)
        d_alog = jnp.sum(dg * g, axis=0, keepdims=True)
        d_dtb = jnp.sum(d_ga, axis=0, keepdims=True)
        d_gb = b128[...] * (beta * (1.0 - beta))
        lane = lax.broadcasted_iota(jnp.int32, (t, LANES), 1)
        contrib = jnp.where(lane == LANE_GA + h, d_ga, 0.0) + jnp.where(lane == LANE_GB + h, d_gb, 0.0)

        @pl.when(h == 0)
        def _():
            dps_ref[...] = jnp.zeros_like(dps_ref)

        dps_ref[...] += contrib

        lane1 = lax.broadcasted_iota(jnp.int32, (1, LANES), 1)
        small = _stack_rows([jnp.where(lane1 == h, d_alog, 0.0), jnp.where(lane1 == h, d_dtb, 0.0), dgn], 8)

        @pl.when((b == 0) & (h == 0))
        def _():
            dsm_ref[...] = jnp.zeros_like(dsm_ref)
            dcw_ref[...] = jnp.zeros_like(dcw_ref)

        dsm_ref[...] += small

        def conv_bwd(dp, x, w_ref, slot):
            dw = _stack_rows([jnp.sum(dp * _shifted(pad_s, 3), axis=0, keepdims=True),
                              jnp.sum(dp * _shifted(pad_s, 2), axis=0, keepdims=True),
                              jnp.sum(dp * _shifted(pad_s, 1), axis=0, keepdims=True),
                              jnp.sum(dp * x, axis=0, keepdims=True)], CONV_K)
            dcw_ref[slot] += dw
            pad_s[PAD:PAD + t, :] = dp
            dx = _shifted(pad_s, -3) * w_ref[0:1, :]
            dx = dx + _shifted(pad_s, -2) * w_ref[1:2, :]
            dx = dx + _shifted(pad_s, -1) * w_ref[2:3, :]
            return dx + dp * w_ref[3:4, :]

        def l2_bwd(dqn, y, c):
            r = lax.rsqrt(jnp.sum(y * y, axis=1, keepdims=True) + EPS)
            s1 = jnp.sum(dqn * y, axis=1, keepdims=True)
            return c * r * dqn - (c * r * r * r) * s1 * y

        def silu_bwd(p, sg):
            return sg * (1.0 + p * (1.0 - sg))

        pq, sq, yq = prep(q_ref, wq_ref)
        dq_ref[...] = _bf(conv_bwd(l2_bwd(dqh[...], yq, DH ** -0.5) * silu_bwd(pq, sq), q_ref[...], wq_ref, h))
        pk, sk, yk = prep(k_ref, wk_ref)
        dk_ref[...] = _bf(conv_bwd(l2_bwd(dkh[...], yk, 1.0) * silu_bwd(pk, sk), k_ref[...], wk_ref, HEADS + h))
        pv, sv, _ = prep(v_ref, wv_ref)
        dv_ref[...] = _bf(conv_bwd(dvh[...] * silu_bwd(pv, sv), v_ref[...], wv_ref, 2 * HEADS + h))

    blk = pl.BlockSpec((t, DH), lambda b, h: (b, h))
    ob = jax.ShapeDtypeStruct((n, HEADS * DH), BF16)
    return pl.pallas_call(
        body, name="gdn_bwd", grid=(b_loc, HEADS),
        in_specs=[col(0), col(HEADS), col(2 * HEADS), col(3 * HEADS), ps_spec, wcol(0), wcol(HEADS), wcol(2 * HEADS),
                  smem, smem, vec, blk, blk, pl.BlockSpec((1, 1, nc, DH, DH), lambda b, h: (b, h, 0, 0, 0))],
        out_specs=[blk, blk, blk, blk, ps_spec,
                   pl.BlockSpec((3 * HEADS, CONV_K, DH), lambda b, h: (0, 0, 0)),
                   pl.BlockSpec((8, LANES), lambda b, h: (0, 0))],
        out_shape=[ob, ob, ob, ob, jax.ShapeDtypeStruct((n, LANES), F32),
                   jax.ShapeDtypeStruct((3 * HEADS, CONV_K, DH), F32), jax.ShapeDtypeStruct((8, LANES), F32)],
        scratch_shapes=([pltpu.VMEM((t, DH), F32)] * 3 + [pltpu.VMEM((t, LANES), F32)] * 2
                        + [pltpu.VMEM((t, DH), F32)] * 5 + [pltpu.VMEM((t, CHUNK), F32)] * 4
                        + [pltpu.VMEM((nc, DH, DH), F32), pltpu.VMEM((t + 2 * PAD, LANES), F32)]),
        compiler_params=_cparams(("arbitrary", "arbitrary")),
    )(pg, pg, pg, pg, ps, convw, convw, convw, a_log, dt_bias, gnorm, d_oa, o_raw, s_all)


def _fox_prologue(q_ref, k_ref, v_ref, ps_ref, fb_ref, gq_ref, gk_ref, h, t, qs, ks, vs, ccol, crow):
    nb = t // FOX_BLOCK
    q, k = q_ref[...], k_ref[...]
    rq = lax.rsqrt(jnp.mean(q * q, axis=1, keepdims=True) + EPS)
    rk = lax.rsqrt(jnp.mean(k * k, axis=1, keepdims=True) + EPS)
    qs[...] = _bf(q * rq * gq_ref[...])
    ks[...] = _bf(k * rk * gk_ref[...])
    vs[...] = _bf(v_ref[...])
    f_in = _head_lane(ps_ref[...], LANE_FF + h) + fb_ref[0, h]
    ccol[...] = jnp.broadcast_to(-_softplus(-f_in), (t, LANES))
    r = lax.broadcasted_iota(jnp.int32, (FOX_BLOCK, FOX_BLOCK), 0)
    c = lax.broadcasted_iota(jnp.int32, (FOX_BLOCK, FOX_BLOCK), 1)
    trilf, triuf = (r >= c).astype(F32), (r <= c).astype(F32)
    blocks = [pl.ds(j * FOX_BLOCK, FOX_BLOCK) for j in range(nb)]
    lfs = _V([ccol[rb, :] for rb in blocks])
    cc = _dot_exact_l(trilf, lfs, terms=3)
    cr = _dot_exact_r(lfs, triuf, _vdot_tn, terms=3)
    sums = _vsum(lfs, 0)
    carry = jnp.zeros((1, LANES), F32)
    for j, rb in enumerate(blocks):
        ccol[rb, :] = cc.xs[j] + carry
        crow[j] = (cr.xs[j] + carry)[0:8]
        carry = carry + sums.xs[j]
    return rq, rk, f_in


def _fox_scores(q_rows, k_rows, cc, cr, row0, col0, masked=True):
    s = _dot_nt(q_rows, k_rows) * (DH ** -0.5) + cc - cr
    if not masked:
        return s
    r = lax.broadcasted_iota(jnp.int32, s.shape, 0)
    c = lax.broadcasted_iota(jnp.int32, s.shape, 1)
    return jnp.where(row0 + r >= col0 + c, s, NEG)


def _fox_specs(t):
    def col(off):
        return pl.BlockSpec((t, DH), lambda b, h: (b, off + h))

    ps_spec = pl.BlockSpec((t, LANES), lambda b, h: (b, 0))
    smem = pl.BlockSpec(memory_space=pltpu.SMEM)
    vec = pl.BlockSpec((1, DH), lambda b, h: (0, 0))
    blk = pl.BlockSpec((t, DH), lambda b, h: (b, h))
    return col, ps_spec, smem, vec, blk


def _fox_fwd(pf, ps, f_bias, gq, gk, b_loc, t):
    n = b_loc * t
    nb = t // FOX_BLOCK
    assert t % FOX_TILE == 0 and FOX_TILE % FOX_SHORT == 0, (t, FOX_TILE, FOX_SHORT)
    kt = FOX_TILE
    nsub = kt // FOX_BLOCK
    col, ps_spec, smem, vec, blk = _fox_specs(t)

    def body(q_ref, k_ref, v_ref, ps_ref, fb_ref, gq_ref, gk_ref, o_ref, lse_ref, qs, ks, vs, ccol, crow):
        h = pl.program_id(1)
        _fox_prologue(q_ref, k_ref, v_ref, ps_ref, fb_ref, gq_ref, gk_ref, h, t, qs, ks, vs, ccol, crow)

        def qblock(i, _):
            ri = pl.ds(pl.multiple_of(i * FOX_SHORT, FOX_SHORT), FOX_SHORT)
            qi = qs[ri, :]
            cc = jnp.concatenate([ccol[ri, :]] * nsub, axis=1)

            def ktile(j, carry, masked):
                m, l, acc = carry
                rj = pl.ds(pl.multiple_of(j * kt, kt), kt)
                cr = jnp.concatenate([crow[j * nsub + u, 0:1, :] for u in range(nsub)], axis=1)
                s = _fox_scores(qi, ks[rj, :], cc, cr, i * FOX_SHORT, j * kt, masked)
                m_new = jnp.maximum(m, jnp.max(s, axis=1, keepdims=True))
                p = jnp.exp(s - m_new)
                alpha = jnp.exp(m - m_new)
                l = alpha * l + jnp.sum(p, axis=1, keepdims=True)
                acc = alpha * acc + _dot(_bf(p), vs[rj, :])
                return m_new, l, acc

            last = (i * FOX_SHORT) // kt
            carry = lax.fori_loop(0, last, functools.partial(ktile, masked=False),
                                  (jnp.full((FOX_SHORT, 1), NEG, F32), jnp.zeros((FOX_SHORT, 1), F32),
                                   jnp.zeros((FOX_SHORT, DH), F32)))
            m, l, acc = ktile(last, carry, True)
            o_ref[ri, :] = acc / l
            lse_ref[ri, :] = jnp.broadcast_to(m + jnp.log(l), (FOX_SHORT, LANES))
            return 0

        lax.fori_loop(0, t // FOX_SHORT, qblock, 0)

    o = jax.ShapeDtypeStruct((n, HEADS * DH), F32)
    return pl.pallas_call(
        body, name="fox_fwd", grid=(b_loc, HEADS),
        in_specs=[col(0), col(HEADS), col(2 * HEADS), ps_spec, smem, vec, vec],
        out_specs=[blk, blk], out_shape=[o, o],
        scratch_shapes=[pltpu.VMEM((t, DH), BF16)] * 3 + [pltpu.VMEM((t, LANES), F32), pltpu.VMEM((nb, 8, LANES), F32)],
        compiler_params=_cparams(("arbitrary", "arbitrary")),
    )(pf, pf, pf, ps, f_bias, gq, gk)


def _fox_bwd(pf, ps, f_bias, gq, gk, d_ob, ob, lse, dps_in, b_loc, t):
    n = b_loc * t
    nb = t // FOX_BLOCK
    assert t % FOX_TILE == 0 and FOX_TILE % FOX_SHORT == 0, (t, FOX_TILE, FOX_SHORT)
    qt = FOX_TILE
    scale = DH ** -0.5
    col, ps_spec, smem, vec, blk = _fox_specs(t)

    def body(q_ref, k_ref, v_ref, ps_ref, fb_ref, gq_ref, gk_ref, do_ref, o_ref, lse_ref, dpsi_ref,
             dq_ref, dk_ref, dv_ref, dps_ref, dsm_ref, qs, ks, vs, ccol, crow, dos, dl, dqa, dcr, dcq):
        b, h = pl.program_id(0), pl.program_id(1)
        rq, _, f_in = _fox_prologue(q_ref, k_ref, v_ref, ps_ref, fb_ref, gq_ref, gk_ref, h, t, qs, ks, vs, ccol, crow)
        dov = do_ref[...]
        dos[...] = _bf(dov)
        dl[...] = jnp.broadcast_to(jnp.sum(dov * o_ref[...], axis=1, keepdims=True), (t, LANES))
        dqa[...] = jnp.zeros_like(dqa)
        dcq[...] = jnp.zeros_like(dcq)
        gkv = gk_ref[...]

        ksub = FOX_SHORT // FOX_BLOCK

        def kblock(j, dgk):
            rj = pl.ds(pl.multiple_of(j * FOX_SHORT, FOX_SHORT), FOX_SHORT)
            kj, vj = ks[rj, :], vs[rj, :]
            cr = jnp.concatenate([crow[j * ksub + u, 0:1, :] for u in range(ksub)], axis=1)

            def wide(x):
                return jnp.concatenate([x] * ksub, axis=1)

            def qtile(i, carry, masked):
                dk_acc, dv_acc, dc = carry
                ri = pl.ds(pl.multiple_of(i * qt, qt), qt)
                qi, doi = qs[ri, :], dos[ri, :]
                s = _fox_scores(qi, kj, wide(ccol[ri, :]), cr, i * qt, j * FOX_SHORT, masked)
                p = jnp.exp(s - wide(lse_ref[ri, :]))
                ds = p * (_dot_nt(doi, vj) - wide(dl[ri, :]))
                dsb = _bf(ds)
                dqa[ri, :] += _dot(dsb, kj)
                dcq[ri, :] += jnp.broadcast_to(jnp.sum(ds, axis=1, keepdims=True), (qt, LANES))
                return (dk_acc + _dot_tn(dsb, qi), dv_acc + _dot_tn(_bf(p), doi),
                        dc - jnp.sum(ds, axis=0, keepdims=True))

            z = jnp.zeros((FOX_SHORT, DH), F32)
            first = (j * FOX_SHORT) // qt
            carry = qtile(first, (z, z, jnp.zeros((1, FOX_SHORT), F32)), True)
            dk_acc, dv_acc, dc = lax.fori_loop(first + 1, t // qt, functools.partial(qtile, masked=False), carry)
            dv_ref[rj, :] = _bf(dv_acc)
            for u in range(ksub):
                dcr[pl.ds(pl.multiple_of((j * ksub + u) * 8, 8), 8), :] = jnp.broadcast_to(
                    dc[:, u * FOX_BLOCK:(u + 1) * FOX_BLOCK], (8, LANES))
            kraw = k_ref[rj, :]
            rk = lax.rsqrt(jnp.mean(kraw * kraw, axis=1, keepdims=True) + EPS)
            dkn = dk_acc * scale
            gy = dkn * gkv
            dk_ref[rj, :] = _bf(rk * gy - kraw * (rk * rk * rk * (1.0 / DH)) * jnp.sum(gy * kraw, axis=1, keepdims=True))
            return dgk + jnp.sum(dkn * kraw * rk, axis=0, keepdims=True)

        dgk = lax.fori_loop(0, t // FOX_SHORT, kblock, jnp.zeros((1, DH), F32))

        q = q_ref[...]
        dqn = dqa[...] * scale
        gy = dqn * gq_ref[...]
        dq_ref[...] = _bf(rq * gy - q * (rq * rq * rq * (1.0 / DH)) * jnp.sum(gy * q, axis=1, keepdims=True))
        dgq = jnp.sum(dqn * q * rq, axis=0, keepdims=True)

        r = lax.broadcasted_iota(jnp.int32, (FOX_BLOCK, FOX_BLOCK), 0)
        c = lax.broadcasted_iota(jnp.int32, (FOX_BLOCK, FOX_BLOCK), 1)
        triuf = (r <= c).astype(F32)

        def rev(jj, carry):
            j = nb - 1 - jj
            rows = pl.ds(pl.multiple_of(j * FOX_BLOCK, FOX_BLOCK), FOX_BLOCK)
            rowv = dcr[pl.ds(pl.multiple_of(j * 8, 8), 1), :]
            colv = jnp.sum(jnp.where(c >= r, jnp.broadcast_to(rowv, (FOX_BLOCK, LANES)), 0.0), axis=1, keepdims=True)
            qcol = dcq[rows, :]
            dl[rows, :] = colv + _dot_exact_l(triuf, qcol, terms=3) + carry
            return carry + jnp.sum(rowv, axis=1, keepdims=True) + jnp.sum(qcol, axis=0, keepdims=True)

        lax.fori_loop(0, nb, rev, jnp.zeros((1, LANES), F32))
        d_ff = dl[...] * _sigmoid(-f_in)
        lane = lax.broadcasted_iota(jnp.int32, (t, LANES), 1)

        @pl.when(h == 0)
        def _():
            dps_ref[...] = dpsi_ref[...]

        dps_ref[...] += jnp.where(lane == LANE_FF + h, d_ff, 0.0)

        lane1 = lax.broadcasted_iota(jnp.int32, (1, LANES), 1)
        d_fb = jnp.sum(d_ff, axis=0, keepdims=True)
        small = _stack_rows([dgq, dgk, jnp.where(lane1 == h, d_fb, 0.0)], 8)

        @pl.when((b == 0) & (h == 0))
        def _():
            dsm_ref[...] = jnp.zeros_like(dsm_ref)

        dsm_ref[...] += small

    ob_ = jax.ShapeDtypeStruct((n, HEADS * DH), BF16)
    return pl.pallas_call(
        body, name="fox_bwd", grid=(b_loc, HEADS),
        in_specs=[col(0), col(HEADS), col(2 * HEADS), ps_spec, smem, vec, vec, blk, blk, blk, ps_spec],
        out_specs=[blk, blk, blk, ps_spec, pl.BlockSpec((8, LANES), lambda b, h: (0, 0))],
        out_shape=[ob_, ob_, ob_, jax.ShapeDtypeStruct((n, LANES), F32), jax.ShapeDtypeStruct((8, LANES), F32)],
        scratch_shapes=([pltpu.VMEM((t, DH), BF16)] * 3 + [pltpu.VMEM((t, LANES), F32), pltpu.VMEM((nb, 8, LANES), F32)]
                        + [pltpu.VMEM((t, DH), BF16), pltpu.VMEM((t, LANES), F32), pltpu.VMEM((t, DH), F32),
                           pltpu.VMEM((8 * nb, LANES), F32), pltpu.VMEM((t, LANES), F32)]),
        compiler_params=_cparams(("arbitrary", "arbitrary")),
    )(pf, pf, pf, ps, f_bias, gq, gk, d_ob, ob, lse, dps_in)


class _NoExchange:
    def late_weights(self, after):
        return {}

    def grads_ready(self, grads, tie):
        return tie


def _local_step(x, target, w, b_loc, t, comm=None):
    comm = comm or _NoExchange()
    w = dict(w)
    xf = x
    u = _rms_fwd(xf, w["norm_mix_g"], "rms_mix")
    pg = _mm(u, w["w_gdn"], name="proj_gdn")
    pf = _mm(u, w["w_fox"], name="proj_fox")
    pgate = _mm(u, w["w_gate"], name="proj_gate")
    ps = _mm(u, w["w_small"], name="proj_small")
    oa, o_raw, s_all = _gdn_fwd(pg, ps, w["conv_w"], w["a_log"], w["dt_bias"], w["gdn_norm_g"], b_loc, t)
    ob, lse = _fox_fwd(pf, ps, w["f_bias"], w["fox_q_norm_g"], w["fox_k_norm_g"], b_loc, t)
    w.update(comm.late_weights(ob))
    ya = _mm(oa, w["w_proj_gdn"], name="proj_a")
    yb = _mm(ob, w["w_proj_fox"], name="proj_b")
    merged = _merge_fwd(ya, yb, pgate)
    h = _mm(merged, w["w_out"], name="proj_out", epi=lambda acc, xr: acc + xr, extras=(xf,))
    hn = _rms_fwd(h, w["norm_mlp_g"], "rms_mlp")
    up, act = _mm(hn, w["w_up"], name="mlp_up", out_dtype=BF16, out2=(_relu2, BF16))
    out = _mm(act, w["w_down"], name="mlp_down", epi=lambda acc, hr: acc + hr, extras=(h,))
    d_out, d_out16, loss_blk = _loss_bwd(out, target)

    g = {}
    g["w_down"] = _mm(act, d_out16, name="dw_down", ta=True, out_dtype=BF16)
    d_up = _mm(d_out16, w["w_down"], name="d_up", tb=True, out_dtype=BF16,
               epi=lambda acc, upr: acc * (2.0 * jnp.maximum(upr.astype(F32), 0.0)), extras=(up,))
    g["w_up"] = _mm(hn, d_up, name="dw_up", ta=True, out_dtype=BF16)
    mlp_gain = comm.grads_ready({"w_down": g["w_down"], "w_up": g["w_up"]}, w["norm_mlp_g"])
    d_hn = _mm(d_up, w["w_up"], name="d_hn", tb=True)
    dh, dh16, g["norm_mlp_g"] = _rms_bwd(d_hn, h, mlp_gain, d_out, "rms_mlp_bwd")
    g["w_out"] = _mm(merged, dh16, name="dw_out", ta=True, out_dtype=BF16)
    dm = _mm(dh16, w["w_out"], name="d_merged", tb=True)
    dya, dyb, dgate_a, dgate_b = _merge_bwd(dm, ya, yb, pgate)
    g["w_proj_gdn"] = _mm(oa, dya, name="dw_proj_a", ta=True, out_dtype=BF16)
    g["w_proj_fox"] = _mm(ob, dyb, name="dw_proj_b", ta=True, out_dtype=BF16)
    gdn_gain = comm.grads_ready({"w_out": g["w_out"], "w_proj_gdn": g["w_proj_gdn"], "w_proj_fox": g["w_proj_fox"]},
                                w["gdn_norm_g"])
    d_oa = _mm(dya, w["w_proj_gdn"], name="d_oa", tb=True)
    d_ob = _mm(dyb, w["w_proj_fox"], name="d_ob", tb=True)
    dgq, dgk, dgv, dgz, dps, dcw, gdn_small = _gdn_bwd(pg, ps, w["conv_w"], w["a_log"], w["dt_bias"], gdn_gain,
                                                       d_oa, o_raw, s_all, b_loc, t)
    dfq, dfk, dfv, dps, fox_small = _fox_bwd(pf, ps, w["f_bias"], w["fox_q_norm_g"], w["fox_k_norm_g"],
                                             d_ob, ob, lse, dps, b_loc, t)
    segs = [(dgq, "w_gdn", 0), (dgk, "w_gdn", 1024), (dgv, "w_gdn", 2048), (dgz, "w_gdn", 3072),
            (dfq, "w_fox", 0), (dfk, "w_fox", 1024), (dfv, "w_fox", 2048),
            (dgate_a, "w_gate", 0), (dgate_b, "w_gate", 1024)]
    dws = [_mm(u, dps, name="dw_small", ta=True, out_dtype=BF16)]
    dws += [_mm(u, dseg, name=f"dw_in_{idx}", ta=True, out_dtype=BF16) for idx, (dseg, _, _) in enumerate(segs)]
    g["w_in_parts"] = dws
    mix_gain = comm.grads_ready({"w_in_parts": dws}, w["norm_mix_g"])
    du = _du_all(dps, w["w_small"], segs, w)
    grad_x, _, g["norm_mix_g"] = _rms_bwd(du, xf, mix_gain, dh, "rms_mix_bwd")
    g["conv"] = dcw
    g["gdn_small"] = gdn_small
    g["fox_small"] = fox_small
    return loss_blk, grad_x, g


def _position():
    x, y, c = lax.axis_index("x"), lax.axis_index("y"), lax.axis_index("c")
    return x, y, c


def _to_bf16(arrs, name):
    n = len(arrs)

    def body(*refs):
        for i in range(n):
            refs[n + i][...] = _bf(refs[i][...])

    return pl.pallas_call(
        body, name=name,
        out_shape=[jax.ShapeDtypeStruct(a.shape, BF16) for a in arrs],
        compiler_params=_cparams(),
    )(*arrs)


def _all_gather(arrs, name):
    n = len(arrs)
    hbm = pl.BlockSpec(memory_space=pl.ANY)

    def body(*refs):
        ins, outs = refs[:n], refs[n:2 * n]
        send, recv, loc = refs[2 * n:]
        x, y, c = _position()
        me = 4 * x + 2 * y + c
        sibling = (x, y, 1 - c)
        chips = [(1 - x, y), (x, 1 - y), (1 - x, 1 - y)]

        def idx(px, py, pc):
            return 4 * px + 2 * py + pc

        def cp(a, k, block, to, src=None):
            return pltpu.make_async_remote_copy(
                src_ref=outs[a].at[block] if src is None else src, dst_ref=outs[a].at[block],
                send_sem=send.at[a, k], recv_sem=recv.at[a, k], device_id=to, device_id_type=MESH)

        mine = [pltpu.make_async_copy(ins[a], outs[a].at[me], loc.at[a]) for a in range(n)]
        for m in mine:
            m.start()
        first = []
        for a in range(n):
            first.append(cp(a, 0, me, sibling, src=ins[a]))
            first += [cp(a, 1 + j, me, (*chip, c), src=ins[a]) for j, chip in enumerate(chips)]
        for f in first:
            f.start()
        passed = []
        for j, chip in enumerate(chips):
            for a in range(n):
                cp(a, 1 + j, idx(*chip, c), (x, y, c)).wait_recv()
                p = cp(a, 4 + j, idx(*chip, c), sibling)
                p.start()
                passed.append(p)
        for a in range(n):
            cp(a, 0, idx(x, y, 1 - c), (x, y, c)).wait_recv()
            for j, chip in enumerate(chips):
                cp(a, 4 + j, idx(*chip, 1 - c), (x, y, c)).wait_recv()
        for f in first + passed:
            f.wait_send()
        for m in mine:
            m.wait()

    return pl.pallas_call(
        body, name=name,
        in_specs=[hbm] * n, out_specs=[hbm] * n,
        out_shape=[jax.ShapeDtypeStruct((N_DEV,) + a.shape, a.dtype) for a in arrs],
        scratch_shapes=[pltpu.SemaphoreType.DMA((n, 7)), pltpu.SemaphoreType.DMA((n, 7)), pltpu.SemaphoreType.DMA((n,))],
        compiler_params=pltpu.CompilerParams(has_side_effects=True),
    )(*arrs)


def _peer(x, y, c, rel):
    return ((1 - x) if rel & 4 else x, (1 - y) if rel & 2 else y, (1 - c) if rel & 1 else c)


HBM_SPEC = pl.BlockSpec(memory_space=pltpu.HBM)
SEM_SPEC = pl.BlockSpec(memory_space=pltpu.SEMAPHORE)
DATAFLOW = pltpu.SideEffectType.DATAFLOW_SIDE_EFFECTING


CHIP_RELS = (2, 4, 6)


def _push_start(arrs, slots, name, chips=False):
    n = len(arrs)
    n_slots = 4 if chips else N_DEV
    rels = CHIP_RELS if chips else tuple(range(1, N_DEV))
    land_shapes = [a.shape if slots else (n_slots,) + a.shape for a in arrs]

    def body(*refs):
        ins, lands, sends, recvs, token = refs[:n], refs[n:2 * n], refs[2 * n:3 * n], refs[3 * n:4 * n], refs[-1]
        x, y, c = _position()
        for rel in rels:
            px, py, pc = _peer(x, y, c, rel)
            mine, theirs = (2 * x + y, 2 * px + py) if chips else (4 * x + 2 * y + c, 4 * px + 2 * py + pc)
            for a in range(n):
                pltpu.make_async_remote_copy(
                    src_ref=ins[a].at[theirs] if slots else ins[a], dst_ref=lands[a].at[mine],
                    send_sem=sends[a], recv_sem=recvs[a], device_id=(px, py, pc), device_id_type=MESH).start()
        token[...] = jnp.zeros_like(token)

    sem = pltpu.SemaphoreType.DMA(())
    outs = pl.pallas_call(
        body, name=name,
        in_specs=[HBM_SPEC] * (2 * n),
        out_shape=(*[sem] * (2 * n), *[pltpu.HBM(a.shape, a.dtype) for a in arrs],
                   *[pltpu.HBM(s, a.dtype) for s, a in zip(land_shapes, arrs)], jax.ShapeDtypeStruct((8, LANES), F32)),
        out_specs=(*[SEM_SPEC] * (2 * n), *[HBM_SPEC] * (2 * n), pl.BlockSpec(memory_space=pltpu.VMEM)),
        input_output_aliases={i: 2 * n + i for i in range(2 * n)},
        compiler_params=pltpu.CompilerParams(has_side_effects=DATAFLOW),
    )(*[pltpu.with_memory_space_constraint(a, pltpu.HBM) for a in arrs],
      *[pltpu.with_memory_space_constraint(lax.empty(s, a.dtype), pltpu.HBM) for s, a in zip(land_shapes, arrs)])
    return dict(sends=list(outs[:n]), recvs=list(outs[n:2 * n]), ins=list(outs[2 * n:3 * n]),
                lands=list(outs[3 * n:4 * n]), token=outs[-1], copies=len(rels))


def _push_wait(started, after, name):
    n = len(started["ins"])
    copies = started["copies"]

    def body(*refs):
        lands, sends, recvs = refs[n:2 * n], refs[2 * n:3 * n], refs[3 * n:4 * n]
        x, y, c = _position()
        for a in range(n):
            every = lands[a].at[pl.ds(0, copies)]
            drain = pltpu.make_async_remote_copy(src_ref=every, dst_ref=every, send_sem=sends[a], recv_sem=recvs[a],
                                                 device_id=(x, y, c), device_id_type=MESH)
            drain.wait_send()
            drain.wait_recv()

    both = started["ins"] + started["lands"]
    outs = pl.pallas_call(
        body, name=name,
        in_specs=[HBM_SPEC] * (2 * n) + [SEM_SPEC] * (2 * n) + [pl.BlockSpec(memory_space=pl.ANY)],
        out_shape=tuple(pltpu.HBM(a.shape, a.dtype) for a in both), out_specs=tuple([HBM_SPEC] * (2 * n)),
        input_output_aliases={i: i for i in range(2 * n)},
        compiler_params=pltpu.CompilerParams(has_side_effects=DATAFLOW),
    )(*both, *started["sends"], *started["recvs"], after)
    return list(outs[:n]), list(outs[n:])


def _sibling_swap(arr, name):
    chips = N_DEV // 2

    def body(in_ref, out_ref, send, recv):
        x, y, c = _position()
        for s in range(chips):
            pltpu.make_async_remote_copy(src_ref=in_ref.at[2 * s + 1 - c], dst_ref=out_ref.at[s], send_sem=send,
                                         recv_sem=recv, device_id=(x, y, 1 - c), device_id_type=MESH).start()
        pltpu.make_async_remote_copy(src_ref=out_ref, dst_ref=out_ref, send_sem=send, recv_sem=recv,
                                     device_id=(x, y, 1 - c), device_id_type=MESH).wait()

    hbm = pl.BlockSpec(memory_space=pl.ANY)
    return pl.pallas_call(
        body, name=name, in_specs=[hbm], out_specs=hbm,
        out_shape=jax.ShapeDtypeStruct((chips,) + arr.shape[1:], arr.dtype),
        scratch_shapes=[pltpu.SemaphoreType.DMA, pltpu.SemaphoreType.DMA],
        compiler_params=pltpu.CompilerParams(has_side_effects=True),
    )(arr)


def _add_halves(core, arr, other, name):
    ns, r, c = other.shape
    tr = min(r, 256)

    def body(core_ref, a_ref, o_ref, out_ref):
        out_ref[...] = _bf(a_ref[...].astype(F32) + o_ref[...].astype(F32))

    blk = pl.BlockSpec((1, tr, c), lambda s, i, core_ref: (s, i, 0))
    return pl.pallas_call(
        body, name=name,
        grid_spec=pltpu.PrefetchScalarGridSpec(
            num_scalar_prefetch=1, grid=(ns, r // tr),
            in_specs=[pl.BlockSpec((1, tr, c), lambda s, i, core_ref: (2 * s + core_ref[0], i, 0)), blk],
            out_specs=blk),
        out_shape=jax.ShapeDtypeStruct((ns, r, c), BF16),
        compiler_params=_cparams(("parallel", "parallel")),
    )(core, arr, other)


def _all_reduce_small(buf, name):
    rows = buf.shape[0]

    def body(in_ref, out_ref, slots, send, recv):
        x, y, c = _position()
        me = 4 * x + 2 * y + c
        slots[me] = in_ref[...]
        copies = []
        for rel in range(1, N_DEV):
            copies.append(pltpu.make_async_remote_copy(
                src_ref=in_ref, dst_ref=slots.at[me], send_sem=send.at[rel - 1], recv_sem=recv.at[rel - 1],
                device_id=_peer(x, y, c, rel), device_id_type=MESH))
        for cpy in copies:
            cpy.start()
        for cpy in copies:
            cpy.wait()
        tot = slots[0]
        for d in range(1, N_DEV):
            tot = tot + slots[d]
        out_ref[...] = tot

    return pl.pallas_call(
        body, name=name,
        out_shape=jax.ShapeDtypeStruct((rows, LANES), F32),
        in_specs=[pl.BlockSpec(memory_space=pltpu.VMEM)], out_specs=pl.BlockSpec(memory_space=pltpu.VMEM),
        scratch_shapes=[pltpu.VMEM((N_DEV, rows, LANES), F32), pltpu.SemaphoreType.DMA((7,)),
                        pltpu.SemaphoreType.DMA((7,))],
        compiler_params=pltpu.CompilerParams(has_side_effects=True),
    )(buf)


def _adam_math(g, w, m, v):
    m = ADAM_B1 * m + (1.0 - ADAM_B1) * g
    v = ADAM_B2 * v + (1.0 - ADAM_B2) * (g * g)
    m_hat = m / (1.0 - ADAM_B1 ** ADAM_STEP)
    v_hat = v / (1.0 - ADAM_B2 ** ADAM_STEP)
    delta = -ADAM_LR * (m_hat / (jnp.sqrt(v_hat) + ADAM_EPS) + ADAM_WD * w)
    return delta, m, v


def _adam_shard(me, parts, mine, w, m, v, name):
    r, c = w.shape
    tr = min(r, 128)
    n_slots = parts.shape[0]

    def body(me_ref, p_ref, own_ref, w_ref, m_ref, v_ref, g_ref, d_ref, nm_ref, nv_ref):
        own = own_ref[0].astype(F32)
        g = None
        for s in range(n_slots):
            term = jnp.where(me_ref[0] == s, own, p_ref[s].astype(F32))
            g = term if g is None else g + term
        d, nm, nv = _adam_math(g, w_ref[...], m_ref[...], v_ref[...])
        g_ref[...] = g
        d_ref[...] = d
        nm_ref[...] = nm
        nv_ref[...] = nv

    row = pl.BlockSpec((tr, c), lambda i, me_ref: (i, 0))
    o = jax.ShapeDtypeStruct((r, c), F32)
    return pl.pallas_call(
        body, name=name,
        grid_spec=pltpu.PrefetchScalarGridSpec(
            num_scalar_prefetch=1, grid=(r // tr,),
            in_specs=[pl.BlockSpec((n_slots, tr, c), lambda i, me_ref: (0, i, 0)),
                      pl.BlockSpec((1, tr, c), lambda i, me_ref: (me_ref[0], i, 0)), row, row, row],
            out_specs=[row] * 4),
        out_shape=[o] * 4,
        compiler_params=_cparams(("parallel",)),
    )(me, parts, mine, w, m, v)


def _adam_small(g, w, m, v):
    def body(g_ref, w_ref, m_ref, v_ref, d_ref, nm_ref, nv_ref):
        d, nm, nv = _adam_math(g_ref[...], w_ref[...], m_ref[...], v_ref[...])
        d_ref[...] = d
        nm_ref[...] = nm
        nv_ref[...] = nv

    o = jax.ShapeDtypeStruct(g.shape, F32)
    return pl.pallas_call(body, name="adam_small", out_shape=[o] * 3, compiler_params=_cparams())(g, w, m, v)


def _split_w_in(w_full):
    o = IN_OFF
    w_gdn = w_full[:, o["gq"]:o["ga"]]
    w_fox = w_full[:, o["fq"]:o["ff"]]
    w_gate = w_full[:, o["gate_a"]:o["end"]]
    w_small = jnp.concatenate([w_full[:, o["ga"]:o["fq"]], w_full[:, o["ff"]:o["gate_a"]],
                               jnp.zeros((w_full.shape[0], LANES - 24), w_full.dtype)], axis=1)
    return w_gdn, w_fox, w_gate, w_small


def _w_in_pieces(g_in):
    nd, d, c = g_in.shape
    tr = 128
    widths = (IN_OFF["ga"] - IN_OFF["gq"], IN_OFF["ff"] - IN_OFF["fq"], IN_OFF["end"] - IN_OFF["gate_a"], LANES)

    def body(in_ref, gdn_ref, fox_ref, gate_ref, small_ref):
        full = jnp.concatenate([in_ref[dv] for dv in range(nd)], axis=1)
        for ref, piece in zip((gdn_ref, fox_ref, gate_ref, small_ref), _split_w_in(full)):
            ref[...] = piece

    return pl.pallas_call(
        body, name="w_in_pieces", grid=(d // tr,),
        in_specs=[pl.BlockSpec((nd, tr, c), lambda i: (0, i, 0))],
        out_specs=[pl.BlockSpec((tr, wd), lambda i: (i, 0)) for wd in widths],
        out_shape=[jax.ShapeDtypeStruct((d, wd), g_in.dtype) for wd in widths],
        compiler_params=_cparams(("parallel",)),
    )(g_in)


def _w_in_shards(parts, c):
    d = parts[0].shape[0]
    tr = 128

    def body(*refs):
        full = _join_w_in([r[...] for r in refs[:-1]])
        for dv in range(N_DEV):
            refs[-1][dv] = full[:, dv * c:(dv + 1) * c]

    return pl.pallas_call(
        body, name="w_in_shards", grid=(d // tr,),
        in_specs=[pl.BlockSpec((tr, p.shape[1]), lambda i: (i, 0)) for p in parts],
        out_specs=pl.BlockSpec((N_DEV, tr, c), lambda i: (0, i, 0)),
        out_shape=jax.ShapeDtypeStruct((N_DEV, d, c), parts[0].dtype),
        compiler_params=_cparams(("parallel",)),
    )(*parts)


def _join_w_in(parts):
    small = parts[0]
    return jnp.concatenate(parts[1:5] + [small[:, 0:16]] + parts[5:8] + [small[:, 16:24]] + parts[8:10], axis=1)


def _rows128(a, rows):
    flat = a.reshape(-1)
    flat = jnp.concatenate([flat, jnp.zeros((rows * LANES - flat.shape[0],), flat.dtype)])
    return flat.reshape(rows, LANES)


def kernel(x, norm_mix_g, w_in, gdn_conv_w, gdn_a_log, gdn_dt_bias, gdn_norm_g, fox_q_norm_g, fox_k_norm_g, fox_f_bias, w_proj_gdn, w_proj_fox, w_out, norm_mlp_g, w_up, w_down, loss_target, m_norm_mix_g, m_w_in, m_gdn_conv_w, m_gdn_a_log, m_gdn_dt_bias, m_gdn_norm_g, m_fox_q_norm_g, m_fox_k_norm_g, m_fox_f_bias, m_w_proj_gdn, m_w_proj_fox, m_w_out, m_norm_mlp_g, m_w_up, m_w_down, v_norm_mix_g, v_w_in, v_gdn_conv_w, v_gdn_a_log, v_gdn_dt_bias, v_gdn_norm_g, v_fox_q_norm_g, v_fox_k_norm_g, v_fox_f_bias, v_w_proj_gdn, v_w_proj_fox, v_w_out, v_norm_mlp_g, v_w_up, v_w_down):
    b_loc, t, d = x.shape
    n = b_loc * t
    me = 4 * lax.axis_index("x") + 2 * lax.axis_index("y") + lax.axis_index("c")

    late_names = ["w_proj_gdn", "w_proj_fox", "w_out", "w_up", "w_down"]
    big16 = _to_bf16([w_in[0], w_proj_gdn[0], w_proj_fox[0], w_out[0], w_up[0], w_down[0]], "weights_to_bf16")
    g_in, g_conv = _all_gather([big16[0], gdn_conv_w[0]], "gather_w_in")
    behind = (g_conv[0:1, 0, 0:1] * 0.0).astype(BF16)
    late = _push_start([big16[1] + behind] + list(big16[2:]), False, "gather_late_start")
    w_gdn, w_fox, w_gate, w_small = _w_in_pieces(g_in)
    weights = {
        "w_gdn": w_gdn, "w_fox": w_fox, "w_gate": w_gate, "w_small": w_small,
        "conv_w": g_conv.transpose(1, 0, 2).reshape(CONV_K, 3 * d),
        "norm_mix_g": norm_mix_g + late["token"][0:1, 0:1], "norm_mlp_g": norm_mlp_g, "a_log": gdn_a_log,
        "dt_bias": gdn_dt_bias, "gdn_norm_g": gdn_norm_g, "fox_q_norm_g": fox_q_norm_g, "fox_k_norm_g": fox_k_norm_g,
        "f_bias": fox_f_bias,
    }
    c_in, c_up = w_in.shape[2], w_up.shape[2]
    me1 = jnp.reshape(me, (1,)).astype(jnp.int32)
    chip1 = jnp.reshape(2 * lax.axis_index("x") + lax.axis_index("y"), (1,)).astype(jnp.int32)
    core1 = jnp.reshape(lax.axis_index("c"), (1,)).astype(jnp.int32)

    class _Exchange:
        def __init__(self):
            self.started = []

        def late_weights(self, after):
            shards, lands = _push_wait(late, after, "gather_late_wait")
            full = [lax.dynamic_update_index_in_dim(land, shard, me, 0) for land, shard in zip(lands, shards)]
            g_pa, g_pb, g_out, g_up, g_down = full
            return {"w_proj_gdn": g_pa.reshape(d, d), "w_proj_fox": g_pb.reshape(d, d), "w_out": g_out.reshape(d, d),
                    "w_up": g_up.transpose(1, 0, 2).reshape(d, D_FF), "w_down": g_down.reshape(D_FF, d)}

        def grads_ready(self, grads, tie):
            names = list(grads)
            if names == ["w_in_parts"]:
                halves = _w_in_shards(grads["w_in_parts"], c_in)
                other = _sibling_swap(halves, "grads_w_in_sibling")
                pair = _add_halves(core1, halves, other, "grads_w_in_pair")
                st = _push_start([pair], True, "grads_start_w_in_parts", chips=True)
            else:
                layout = {"w_up": lambda a: a.reshape(d, N_DEV, c_up).transpose(1, 0, 2),
                          "w_down": lambda a: a.reshape(N_DEV, D_FF // N_DEV, d)}
                arrs = [layout.get(k, lambda a: a.reshape(N_DEV, d // N_DEV, d))(grads[k]) for k in names]
                st = _push_start(arrs, True, "grads_start_" + names[0])
            self.started.append((names, st))
            return tie + st["token"][0:1, 0:1]

    comm = _Exchange()
    loss_blk, grad_x, g = _local_step(x.reshape(n, d), loss_target.reshape(n, d), weights, b_loc, t, comm)

    shards = {"w_in_parts": (w_in, m_w_in, v_w_in), "w_proj_gdn": (w_proj_gdn, m_w_proj_gdn, v_w_proj_gdn),
              "w_proj_fox": (w_proj_fox, m_w_proj_fox, v_w_proj_fox), "w_out": (w_out, m_w_out, v_w_out),
              "w_up": (w_up, m_w_up, v_w_up), "w_down": (w_down, m_w_down, v_w_down)}
    adam = {}

    def finish(names, st, after):
        mine, parts = _push_wait(st, after, "grads_wait_" + names[0])
        slot = chip1 if st["copies"] == len(CHIP_RELS) else me1
        for k, own, part in zip(names, mine, parts):
            wi, mi, vi = shards[k]
            adam[k] = [r[None] for r in _adam_shard(slot, part, own, wi[0], mi[0], vi[0], "adam_" + k)]

    for names, st in comm.started[:-1]:
        finish(names, st, grad_x)

    conv_rows = CONV_K * 3 * d // LANES
    conv_g = g["conv"].transpose(1, 0, 2).reshape(conv_rows, LANES)
    buf = jnp.concatenate([conv_g, g["norm_mix_g"].reshape(8, LANES), g["norm_mlp_g"].reshape(8, LANES),
                           g["gdn_small"], g["fox_small"], loss_blk], axis=0)
    anchor = sum(adam[k][1][0, 0:1, 0:LANES] for names, _ in comm.started[:-1] for k in names) * 0.0
    tot = _all_reduce_small(buf + anchor, "all_reduce_small")
    finish(*comm.started[-1], tot)
    big_out = [adam[k] for k in ["w_in_parts"] + late_names]
    o = conv_rows
    conv_full = tot[0:o].reshape(CONV_K, 3 * d)
    c_conv = gdn_conv_w.shape[2]
    g_conv_shard = lax.dynamic_slice(conv_full, (0, me * c_conv), (CONV_K, c_conv))
    g_mix = tot[o:o + 8].reshape(1, d)
    g_mlp = tot[o + 8:o + 16].reshape(1, d)
    gs, fs = tot[o + 16:o + 24], tot[o + 24:o + 32]
    loss = tot[o + 32, 0]
    small_g = [g_mix, g_conv_shard[None], gs[0:1, 0:HEADS], gs[1:2, 0:HEADS], gs[2:3], fs[0:1], fs[1:2], fs[2:3, 0:HEADS],
               g_mlp]
    small_w = [norm_mix_g, gdn_conv_w, gdn_a_log, gdn_dt_bias, gdn_norm_g, fox_q_norm_g, fox_k_norm_g, fox_f_bias,
               norm_mlp_g]
    small_m = [m_norm_mix_g, m_gdn_conv_w, m_gdn_a_log, m_gdn_dt_bias, m_gdn_norm_g, m_fox_q_norm_g, m_fox_k_norm_g,
               m_fox_f_bias, m_norm_mlp_g]
    small_v = [v_norm_mix_g, v_gdn_conv_w, v_gdn_a_log, v_gdn_dt_bias, v_gdn_norm_g, v_fox_q_norm_g, v_fox_k_norm_g,
               v_fox_f_bias, v_norm_mlp_g]
    row_counts = [-(-a.size // (8 * LANES)) * 8 for a in small_w]

    def pack(arrs):
        return jnp.concatenate([_rows128(a, rc) for a, rc in zip(arrs, row_counts)], axis=0)

    sd, sm, sv = _adam_small(pack(small_g), pack(small_w), pack(small_m), pack(small_v))

    def unpack(p):
        outs, r0 = [], 0
        for a, rc in zip(small_w, row_counts):
            outs.append(p[r0:r0 + rc].reshape(-1)[:a.size].reshape(a.shape))
            r0 += rc
        return outs

    small_out = [small_g_i.reshape(w_i.shape) for small_g_i, w_i in zip(small_g, small_w)], unpack(sd), unpack(sm), unpack(sv)

    def ordered(kind):
        s = small_out[kind]
        bo = [b[kind] for b in big_out]
        return [s[0], bo[0], s[1], s[2], s[3], s[4], s[5], s[6], s[7], bo[1], bo[2], bo[3], s[8], bo[4], bo[5]]

    return (loss, grad_x.reshape(b_loc, t, d), *ordered(0), *ordered(1), *ordered(2), *ordered(3))
```

```python
import functools

import jax
import jax.numpy as jnp
from jax import lax
from jax.experimental import pallas as pl
from jax.experimental.pallas import tpu as pltpu

F32 = jnp.float32
BF16 = jnp.bfloat16
MESH = pl.DeviceIdType.MESH

N_DEV = 8
D_MODEL = 1024
HEADS = 8
DH = 128
CONV_K = 4
CHUNK = 128
GDN_GROUP = 16
FOX_BLOCK = 128
FOX_TILE = 512
FOX_SHORT = 512
D_FF = 4 * D_MODEL
EPS = 1e-6
LANES = 128
NEG = -1e30
IN_OFF = {"gq": 0, "gk": 1024, "gv": 2048, "gz": 3072, "ga": 4096, "gb": 4104, "fq": 4112, "fk": 5136,
          "fv": 6160, "ff": 7184, "gate_a": 7192, "gate_b": 8216, "end": 9240}
LANE_GA, LANE_GB, LANE_FF = 0, 8, 16

ADAM_LR = 0.001
ADAM_B1 = 0.9
ADAM_B2 = 0.999
ADAM_EPS = 1e-08
ADAM_WD = 0.01
ADAM_STEP = 10

VMEM_LIMIT = 56 * 1024 * 1024


def _cparams(sem=None):
    return pltpu.CompilerParams(dimension_semantics=sem, vmem_limit_bytes=VMEM_LIMIT)


def _sigmoid(x):
    return 1.0 / (1.0 + jnp.exp(-x))


def _softplus(x):
    return jnp.maximum(x, 0.0) + jnp.log(1.0 + jnp.exp(-jnp.abs(x)))


def _dot(a, b, prec=None):
    return lax.dot_general(a, b, (((1,), (0,)), ((), ())), precision=prec, preferred_element_type=F32)


def _dot_nt(a, b, prec=None):
    return lax.dot_general(a, b, (((1,), (1,)), ((), ())), precision=prec, preferred_element_type=F32)


def _dot_tn(a, b, prec=None):
    return lax.dot_general(a, b, (((0,), (0,)), ((), ())), precision=prec, preferred_element_type=F32)


def _bf(x):
    return x.astype(BF16)


MM_TILE = 1024


def _mm(a, b, *, name, ta=False, tb=False, out_dtype=F32, epi=None, extras=(), out2=None,
        b_koff=0, tm=MM_TILE, tn=MM_TILE, tk=MM_TILE):
    m = a.shape[1] if ta else a.shape[0]
    kdim = a.shape[0] if ta else a.shape[1]
    n = b.shape[0] if tb else b.shape[1]
    if ta and tk == MM_TILE:
        tk = 2 * MM_TILE
    tm, tn, tk = min(tm, m), min(tn, n), min(tk, kdim)
    nk = kdim // tk
    grid = (m // tm, n // tn, nk)
    koff = b_koff // tk
    a_spec = pl.BlockSpec((tk, tm), lambda i, j, k: (k, i)) if ta else pl.BlockSpec((tm, tk), lambda i, j, k: (i, k))
    if tb:
        b_spec = pl.BlockSpec((tn, tk), lambda i, j, k: (j, k + koff))
    else:
        b_spec = pl.BlockSpec((tk, tn), lambda i, j, k: (k + koff, j))
    o_spec = pl.BlockSpec((tm, tn), lambda i, j, k: (i, j))
    n_e = len(extras)
    n_o = 1 if out2 is None else 2
    dims = (((0 if ta else 1,), (1 if tb else 0,)), ((), ()))

    def body(a_ref, b_ref, *rest):
        e_refs, o_refs = rest[:n_e], rest[n_e:n_e + n_o]
        prod = lax.dot_general(_bf(a_ref[...]), _bf(b_ref[...]), dims, preferred_element_type=F32)

        def finish(r):
            if out2 is not None:
                o_refs[1][...] = out2[0](r).astype(out2[1])
            if epi is not None:
                r = epi(r, *[e[...] for e in e_refs])
            o_refs[0][...] = r.astype(out_dtype)

        if nk == 1:
            finish(prod)
        else:
            acc = rest[n_e + n_o]
            k = pl.program_id(2)

            @pl.when(k == 0)
            def _():
                acc[...] = prod

            @pl.when(k > 0)
            def _():
                acc[...] += prod

            @pl.when(k == nk - 1)
            def _():
                finish(acc[...])

    shapes = [jax.ShapeDtypeStruct((m, n), out_dtype)]
    if out2 is not None:
        shapes.append(jax.ShapeDtypeStruct((m, n), out2[1]))
    res = pl.pallas_call(
        body, name=name, grid=grid,
        in_specs=[a_spec, b_spec] + [o_spec] * n_e,
        out_specs=[o_spec] * n_o, out_shape=shapes,
        scratch_shapes=[] if nk == 1 else [pltpu.VMEM((tm, tn), F32)],
        compiler_params=_cparams(("parallel", "parallel", "arbitrary")),
    )(a, b, *extras)
    return res[0] if out2 is None else res


def _du_all(dps, w_small, segs, w, tm=512):
    n, d = dps.shape[0], w_small.shape[0]
    names = []
    for _, wname, _ in segs:
        if wname not in names:
            names.append(wname)
    first = {nm: min(i for i, s in enumerate(segs) if s[1] == nm) for nm in names}
    count = {nm: sum(1 for s in segs if s[1] == nm) for nm in names}
    n_seg, n_i = len(segs), n // tm

    def w_spec(nm):
        return pl.BlockSpec((d, d), lambda k, i: (0, jnp.clip(k - first[nm], 0, count[nm] - 1)))

    def rows_spec(cols, j):
        return pl.BlockSpec((tm, cols), lambda k, i: (jnp.where(k == j, i, jnp.where(k < j, 0, n_i - 1)), 0))

    def body(dps_ref, ws_ref, *rest):
        seg_refs, w_refs, o_ref, acc = rest[:n_seg], rest[n_seg:n_seg + len(names)], rest[-2], rest[-1]
        k, i = pl.program_id(0), pl.program_id(1)
        rows = pl.ds(pl.multiple_of(i * tm, tm), tm)

        @pl.when(k == 0)
        def _():
            acc[rows, :] = _dot_nt(_bf(dps_ref[...]), ws_ref[...])

        for idx, (_, wname, _) in enumerate(segs):
            @pl.when(k == idx)
            def _(idx=idx, wname=wname):
                acc[rows, :] += _dot_nt(seg_refs[idx][...], w_refs[names.index(wname)][...])

        @pl.when(k == n_seg - 1)
        def _():
            o_ref[...] = acc[rows, :]

    return pl.pallas_call(
        body, name="du_all", grid=(n_seg, n_i),
        in_specs=[rows_spec(dps.shape[1], 0), pl.BlockSpec(w_small.shape, lambda k, i: (0, 0))]
                 + [rows_spec(d, j) for j in range(n_seg)] + [w_spec(nm) for nm in names],
        out_specs=pl.BlockSpec((tm, d), lambda k, i: (jnp.where(k == n_seg - 1, i, 0), 0)),
        out_shape=jax.ShapeDtypeStruct((n, d), F32),
        scratch_shapes=[pltpu.VMEM((n, d), F32)],
        compiler_params=_cparams(("arbitrary", "arbitrary")),
    )(dps, w_small, *[s[0] for s in segs], *[w[nm] for nm in names])


def _relu2(x):
    r = jnp.maximum(x, 0.0)
    return r * r


ROWS = 512


def _rms_fwd(x, g, name):
    n, d = x.shape

    def body(x_ref, g_ref, u_ref):
        xv = x_ref[...]
        r = lax.rsqrt(jnp.mean(xv * xv, axis=1, keepdims=True) + EPS)
        u_ref[...] = _bf(xv * r * g_ref[...])

    return pl.pallas_call(
        body, name=name, grid=(n // ROWS,),
        in_specs=[pl.BlockSpec((ROWS, d), lambda i: (i, 0)), pl.BlockSpec((1, d), lambda i: (0, 0))],
        out_specs=pl.BlockSpec((ROWS, d), lambda i: (i, 0)),
        out_shape=jax.ShapeDtypeStruct((n, d), BF16),
        compiler_params=_cparams(("parallel",)),
    )(x, g)


def _rms_bwd(dy, x, g, dres, name):
    n, d = x.shape

    def body(dy_ref, x_ref, g_ref, dres_ref, dx_ref, dx16_ref, dg_ref):
        i = pl.program_id(0)
        xv, dyv = x_ref[...], dy_ref[...]
        r = lax.rsqrt(jnp.mean(xv * xv, axis=1, keepdims=True) + EPS)
        gy = dyv * g_ref[...]
        s = jnp.sum(gy * xv, axis=1, keepdims=True)
        dx = dres_ref[...] + r * gy - xv * (r * r * r * (1.0 / d)) * s
        dx_ref[...] = dx
        dx16_ref[...] = _bf(dx)

        @pl.when(i == 0)
        def _():
            dg_ref[...] = jnp.zeros_like(dg_ref)

        dg_ref[...] += jnp.sum(dyv * xv * r, axis=0, keepdims=True)

    row = pl.BlockSpec((ROWS, d), lambda i: (i, 0))
    vec = pl.BlockSpec((1, d), lambda i: (0, 0))
    return pl.pallas_call(
        body, name=name, grid=(n // ROWS,),
        in_specs=[row, row, vec, row], out_specs=[row, row, vec],
        out_shape=[jax.ShapeDtypeStruct((n, d), F32), jax.ShapeDtypeStruct((n, d), BF16),
                   jax.ShapeDtypeStruct((1, d), F32)],
        compiler_params=_cparams(("arbitrary",)),
    )(dy, x, g, dres)


def _merge_fwd(ya, yb, gate):
    n, d = ya.shape

    def body(ya_ref, yb_ref, ga_ref, gb_ref, o_ref):
        o_ref[...] = _bf(_sigmoid(ga_ref[...]) * ya_ref[...] + _sigmoid(gb_ref[...]) * yb_ref[...])

    row = pl.BlockSpec((ROWS, d), lambda i: (i, 0))
    return pl.pallas_call(
        body, name="merge_fwd", grid=(n // ROWS,),
        in_specs=[row, row, row, pl.BlockSpec((ROWS, d), lambda i: (i, 1))], out_specs=row,
        out_shape=jax.ShapeDtypeStruct((n, d), BF16),
        compiler_params=_cparams(("parallel",)),
    )(ya, yb, gate, gate)


def _merge_bwd(dm, ya, yb, gate):
    n, d = ya.shape

    def body(dm_ref, ya_ref, yb_ref, ga_ref, gb_ref, dya_ref, dyb_ref, dga_ref, dgb_ref):
        dmv = dm_ref[...]
        sa, sb = _sigmoid(ga_ref[...]), _sigmoid(gb_ref[...])
        dya_ref[...] = _bf(dmv * sa)
        dyb_ref[...] = _bf(dmv * sb)
        dga_ref[...] = _bf(dmv * ya_ref[...] * sa * (1.0 - sa))
        dgb_ref[...] = _bf(dmv * yb_ref[...] * sb * (1.0 - sb))

    row = pl.BlockSpec((ROWS, d), lambda i: (i, 0))
    o = jax.ShapeDtypeStruct((n, d), BF16)
    return pl.pallas_call(
        body, name="merge_bwd", grid=(n // ROWS,),
        in_specs=[row, row, row, row, pl.BlockSpec((ROWS, d), lambda i: (i, 1))], out_specs=[row] * 4,
        out_shape=[o] * 4,
        compiler_params=_cparams(("parallel",)),
    )(dm, ya, yb, gate, gate)


def _loss_bwd(out, target):
    n, d = out.shape

    def body(o_ref, t_ref, d_ref, d16_ref, l_ref):
        i = pl.program_id(0)
        err = o_ref[...] - t_ref[...]
        d_ref[...] = err * (1.0 / d)
        d16_ref[...] = _bf(err * (1.0 / d))

        @pl.when(i == 0)
        def _():
            l_ref[...] = jnp.zeros_like(l_ref)

        l_ref[...] += 0.5 * jnp.sum(jnp.mean(err * err, axis=1, keepdims=True), axis=0, keepdims=True)

    row = pl.BlockSpec((ROWS, d), lambda i: (i, 0))
    return pl.pallas_call(
        body, name="loss_bwd", grid=(n // ROWS,),
        in_specs=[row, row], out_specs=[row, row, pl.BlockSpec((8, LANES), lambda i: (0, 0))],
        out_shape=[jax.ShapeDtypeStruct((n, d), F32), jax.ShapeDtypeStruct((n, d), BF16),
                   jax.ShapeDtypeStruct((8, LANES), F32)],
        compiler_params=_cparams(("arbitrary",)),
    )(out, target)


PAD = 8


def _pad_zero(pad_ref):
    t = pad_ref.shape[0] - 2 * PAD
    pad_ref[0:PAD, :] = jnp.zeros((PAD, LANES), F32)
    pad_ref[PAD + t:2 * PAD + t, :] = jnp.zeros((PAD, LANES), F32)


def _shifted(pad_ref, s):
    t = pad_ref.shape[0] - 2 * PAD
    return pad_ref[PAD - s:PAD - s + t, :]


def _conv(x, w_ref, pad_ref):
    t = x.shape[0]
    pad_ref[PAD:PAD + t, :] = x
    y = _shifted(pad_ref, 3) * w_ref[0:1, :]
    y = y + _shifted(pad_ref, 2) * w_ref[1:2, :]
    y = y + _shifted(pad_ref, 1) * w_ref[2:3, :]
    return y + x * w_ref[3:4, :]


def _chunk_consts():
    r = lax.broadcasted_iota(jnp.int32, (CHUNK, CHUNK), 0)
    c = lax.broadcasted_iota(jnp.int32, (CHUNK, CHUNK), 1)
    incl, strict = r >= c, r > c
    return dict(incl=incl, strict=strict, trilf=incl.astype(F32), triuf=(r <= c).astype(F32),
                eye=(r == c).astype(F32))


class _V:
    def __init__(self, xs):
        self.xs = list(xs)

    def __add__(self, o):
        return _ap(lambda x, y: x + y, self, o)

    def __radd__(self, o):
        return _ap(lambda x, y: y + x, self, o)

    def __sub__(self, o):
        return _ap(lambda x, y: x - y, self, o)

    def __rsub__(self, o):
        return _ap(lambda x, y: y - x, self, o)

    def __mul__(self, o):
        return _ap(lambda x, y: x * y, self, o)

    def __rmul__(self, o):
        return _ap(lambda x, y: y * x, self, o)

    def __neg__(self):
        return _ap(lambda x: -x, self)

    def __getitem__(self, idx):
        return _ap(lambda x: x[idx], self)


def _ap(fn, *args):
    n = [len(a.xs) for a in args if isinstance(a, _V)]
    if not n:
        return fn(*args)
    return _V([fn(*[a.xs[i] if isinstance(a, _V) else a for a in args]) for i in range(n[0])])


def _vbf(x):
    return _ap(_bf, x)


def _vdot(a, b):
    return _ap(_dot, a, b)


def _vdot_nt(a, b):
    return _ap(_dot_nt, a, b)


def _vdot_tn(a, b):
    return _ap(_dot_tn, a, b)


def _vexp(x):
    return _ap(jnp.exp, x)


def _vsum(x, axis):
    return _ap(lambda v: jnp.sum(v, axis=axis, keepdims=True), x)


def _vcat(a, b, axis):
    return _ap(lambda x, y: jnp.concatenate([x, y], axis=axis), a, b)


def _vmask(mask, x):
    return _ap(lambda v: jnp.where(mask, v, 0.0), x)


def _split2(x):
    h = _vbf(x)
    return h, _vbf(x - _ap(lambda v: v.astype(F32), h))


def _dot3(a, b, kind=_vdot):
    ah, al = _split2(a)
    bh, bl = _split2(b)
    return kind(ah, bh) + (kind(ah, bl) + kind(al, bh))


def _split(x, terms):
    out = []
    for _ in range(terms):
        h = _vbf(x)
        out.append(h)
        x = x - _ap(lambda v: v.astype(F32), h)
    return out


def _dot_exact_l(m01, x, kind=_vdot, terms=2):
    mb = _bf(m01)
    parts = [kind(mb, xp) for xp in _split(x, terms)]
    return functools.reduce(lambda a, b: a + b, reversed(parts))


def _dot_exact_r(x, m01, kind=_vdot, terms=2):
    mb = _bf(m01)
    parts = [kind(xp, mb) for xp in _split(x, terms)]
    return functools.reduce(lambda a, b: a + b, reversed(parts))


def _inv_series(a, eye):
    m = eye.shape[0]
    levels = m.bit_length() - 1
    p = -a
    r = p + eye
    p = _dot3(p, p)
    for j in range(1, levels):
        if j < levels - 1:
            y = _dot3(p, _vcat(p, r, 1))
            p, r = y[:, 0:m], r + y[:, m:2 * m]
        else:
            r = r + _dot3(p, r)
    return r


def _inv_unit_lower(a, eye):
    return _inv_series(a, eye)


def _gdn_chunk_pre(q, k, v, g128, b128, cs):
    incl = cs["incl"]
    b64 = b128
    big_g = _dot_exact_l(cs["trilf"], g128)
    gc = big_g[:, 0:CHUNK]
    gr = _dot_exact_r(g128, cs["triuf"], _vdot_tn)
    decay = _ap(lambda d: jnp.where(incl, jnp.exp(jnp.where(incl, d, 0.0)), 0.0), gc - gr)
    kb, qb = _vbf(k), _vbf(q)
    qkk = _vdot_nt(_vcat(qb, kb, 0), kb)
    qk, kk = qkk[0:CHUNK], qkk[CHUNK:2 * CHUNK]
    tm = _inv_unit_lower(_vmask(cs["strict"], b64 * kk * decay), cs["eye"])
    e_g = _vexp(big_g)
    wu = _dot3(tm, _vcat(v * b128, k * (b128 * e_g), 1))
    w, u = wu[:, 0:DH], wu[:, DH:2 * DH]
    g_last = _vsum(g128, 0)
    return dict(big_g=big_g, decay=decay, kk=kk, qk=qk, tm=tm, w=w, u=u, p=qk * decay, q_dec=q * e_g,
                k_dec=k * _vexp(g_last - big_g), dec=_vexp(g_last))


def _gdn_chunk_post(q, k, v, g128, b128, s, ds_next, do, dv_new, big_g, decay, kk, qk, tm, u, v_new, cs):
    b64 = b128
    e_g = _vexp(big_g)
    vb = v * b128
    kbeta = k * (b128 * e_g)
    q_dec = q * e_g
    g_last = _vsum(g128, 0)
    ekg = _vexp(g_last - big_g)
    k_dec = k * ekg
    dec = _vexp(g_last)
    kb, qb, sb = _vbf(k), _vbf(q), _vbf(s)
    dob, dsb, vnb, dvnb = _vbf(do), _vbf(ds_next), _vbf(v_new), _vbf(dv_new)
    dp = _vmask(cs["incl"], _vdot_nt(dob, vnb))
    dq_dec = _vdot_nt(dob, sb)
    du = -_vdot_nt(dvnb, sb)
    ddec = _vsum(_vsum(s * ds_next, 1), 0)
    dk_dec = _vdot_nt(vnb, dsb)
    dwu = _vcat(dv_new, du, 1)
    dt = _dot3(dwu, _vcat(vb, kbeta, 1), _vdot_nt)
    dvk = _dot3(tm, dwu, _vdot_tn)
    dvb, dkbeta = dvk[:, 0:DH], dvk[:, DH:2 * DH]
    da = _vmask(cs["strict"], -_dot3(tm, _dot3(dt, tm, _vdot_nt), _vdot_tn))
    dkk = _vbf(da * b64 * decay)
    dqk = _vbf(dp * decay)
    ddd = (da * b64 * kk + dp * qk) * decay
    dq = _vdot(dqk, kb) + dq_dec * e_g
    dk = _vdot_tn(dqk, qb) + _vdot(dkk, kb) + _vdot_tn(dkk, kb) + dk_dec * ekg + dkbeta * (b128 * e_g)
    dv = dvb * b128
    dbeta = _vsum(da * kk * decay, 1) + _vsum(dvb * v, 1) + _vsum(dkbeta * k * e_g, 1)
    s_k = _vsum(dk_dec * k_dec, 1)
    dg_col = _vsum(ddd, 1) + _vsum(dq_dec * q_dec, 1) - s_k + _vsum(dkbeta * kbeta, 1)
    colsum = _dot_exact_r(ddd, jnp.ones((CHUNK, LANES), F32), _vdot_tn)
    dg_last = _vsum(s_k, 0) + ddec * dec
    dg = _dot_exact_l(cs["triuf"], dg_col - colsum) + dg_last
    return dq, dk, dv, dg, dbeta


def _stack_rows(vecs, nrows):
    row = lax.broadcasted_iota(jnp.int32, (nrows, LANES), 0)
    out = jnp.zeros((nrows, LANES), F32)
    for i, v in enumerate(vecs):
        out = out + jnp.where(row == i, jnp.broadcast_to(v, (nrows, LANES)), 0.0)
    return out


def _head_lane(x, lane_idx):
    lane = lax.broadcasted_iota(jnp.int32, x.shape, 1)
    return jnp.sum(jnp.where(lane == lane_idx, x, 0.0), axis=1, keepdims=True)


def _gdn_gates(ps, h, alog_ref, dtb_ref):
    ga = _head_lane(ps, LANE_GA + h)
    gb = _head_lane(ps, LANE_GB + h)
    a = jnp.exp(jnp.full((1, 1), alog_ref[0, h], F32))
    sp_in = ga + dtb_ref[0, h]
    g = -a * _softplus(sp_in)
    return g, _sigmoid(gb), a, sp_in


def _gdn_specs(b_loc, t):
    def col(off):
        return pl.BlockSpec((t, DH), lambda b, h: (b, off + h))

    ps_spec = pl.BlockSpec((t, LANES), lambda b, h: (b, 0))

    def wcol(off):
        return pl.BlockSpec((CONV_K, DH), lambda b, h: (0, off + h))

    smem = pl.BlockSpec(memory_space=pltpu.SMEM)
    vec = pl.BlockSpec((1, DH), lambda b, h: (0, 0))
    return col, ps_spec, wcol, smem, vec


def _gdn_fwd(pg, ps, convw, a_log, dt_bias, gnorm, b_loc, t):
    n = b_loc * t
    assert t % (CHUNK * GDN_GROUP) == 0 and CHUNK == LANES, (t, CHUNK, GDN_GROUP)
    nc = t // CHUNK
    col, ps_spec, wcol, smem, vec = _gdn_specs(b_loc, t)

    def body(q_ref, k_ref, v_ref, z_ref, ps_ref, wq_ref, wk_ref, wv_ref, alog_ref, dtb_ref, gn_ref,
             oa_ref, oraw_ref, s_ref, qn, kn, vv, g128, b128, uq_s, p_s, kd_s, dec_s, pad_s):
        h = pl.program_id(1)
        g, beta, _, _ = _gdn_gates(ps_ref[...], h, alog_ref, dtb_ref)
        g128[...] = jnp.broadcast_to(g, (t, LANES))
        b128[...] = jnp.broadcast_to(beta, (t, LANES))
        _pad_zero(pad_s)
        pq = _conv(q_ref[...], wq_ref, pad_s)
        yq = pq * _sigmoid(pq)
        qn[...] = yq * (lax.rsqrt(jnp.sum(yq * yq, axis=1, keepdims=True) + EPS) * (DH ** -0.5))
        pk = _conv(k_ref[...], wk_ref, pad_s)
        yk = pk * _sigmoid(pk)
        kn[...] = yk * lax.rsqrt(jnp.sum(yk * yk, axis=1, keepdims=True) + EPS)
        pv = _conv(v_ref[...], wv_ref, pad_s)
        vv[...] = pv * _sigmoid(pv)
        cs = _chunk_consts()

        def pre_group(gi, _):
            idx = [gi * GDN_GROUP + c for c in range(GDN_GROUP)]
            rows = [pl.ds(pl.multiple_of(i * CHUNK, CHUNK), CHUNK) for i in idx]
            ins = [_V([ref[r, :] for r in rows]) for ref in (qn, kn, vv, g128, b128)]
            f = _gdn_chunk_pre(*ins, cs)
            for c, (i, r) in enumerate(zip(idx, rows)):
                vv[r, :] = f["w"].xs[c]
                uq_s[i, 0:CHUNK, :] = _bf(f["u"].xs[c])
                uq_s[i, CHUNK:2 * CHUNK, :] = _bf(f["q_dec"].xs[c])
                p_s[r, :] = _bf(f["p"].xs[c])
                kd_s[r, :] = _bf(f["k_dec"].xs[c])
                dec_s[pl.ds(pl.multiple_of(i * 8, 8), 8), :] = jnp.broadcast_to(f["dec"].xs[c], (8, LANES))
            return 0

        lax.fori_loop(0, nc // GDN_GROUP, pre_group, 0)

        def chunk(i, s):
            r = pl.ds(pl.multiple_of(i * CHUNK, CHUNK), CHUNK)
            us = _dot(uq_s[i], _bf(s))
            vnb = _bf(vv[r, :] - us[0:CHUNK])
            oraw_ref[r, :] = us[CHUNK:2 * CHUNK] + _dot(p_s[r, :], vnb)
            s_ref[0, 0, i] = s
            return s * dec_s[pl.ds(pl.multiple_of(i * 8, 8), 1), :] + _dot_tn(kd_s[r, :], vnb)

        lax.fori_loop(0, nc, chunk, jnp.zeros((DH, DH), F32))
        o = oraw_ref[...]
        rr = lax.rsqrt(jnp.mean(o * o, axis=1, keepdims=True) + EPS)
        z = z_ref[...]
        oa_ref[...] = _bf((o * rr * gn_ref[...]) * (z * _sigmoid(z)))

    return pl.pallas_call(
        body, name="gdn_fwd", grid=(b_loc, HEADS),
        in_specs=[col(0), col(HEADS), col(2 * HEADS), col(3 * HEADS), ps_spec, wcol(0), wcol(HEADS), wcol(2 * HEADS),
                  smem, smem, vec],
        out_specs=[pl.BlockSpec((t, DH), lambda b, h: (b, h)), pl.BlockSpec((t, DH), lambda b, h: (b, h)),
                   pl.BlockSpec((1, 1, nc, DH, DH), lambda b, h: (b, h, 0, 0, 0))],
        out_shape=[jax.ShapeDtypeStruct((n, HEADS * DH), BF16), jax.ShapeDtypeStruct((n, HEADS * DH), F32),
                   jax.ShapeDtypeStruct((b_loc, HEADS, nc, DH, DH), F32)],
        scratch_shapes=([pltpu.VMEM((t, DH), F32)] * 3 + [pltpu.VMEM((t, LANES), F32)] * 2
                        + [pltpu.VMEM((nc, 2 * CHUNK, DH), BF16), pltpu.VMEM((t, CHUNK), BF16), pltpu.VMEM((t, DH), BF16),
                           pltpu.VMEM((8 * nc, LANES), F32), pltpu.VMEM((t + 2 * PAD, LANES), F32)]),
        compiler_params=_cparams(("arbitrary", "arbitrary")),
    )(pg, pg, pg, pg, ps, convw, convw, convw, a_log, dt_bias, gnorm)


def _gdn_bwd(pg, ps, convw, a_log, dt_bias, gnorm, d_oa, o_raw, s_all, b_loc, t):
    n = b_loc * t
    assert t % (CHUNK * GDN_GROUP) == 0 and CHUNK == LANES, (t, CHUNK, GDN_GROUP)
    nc = t // CHUNK
    col, ps_spec, wcol, smem, vec = _gdn_specs(b_loc, t)

    def body(q_ref, k_ref, v_ref, z_ref, ps_ref, wq_ref, wk_ref, wv_ref, alog_ref, dtb_ref, gn_ref,
             doa_ref, oraw_ref, s_ref,
             dq_ref, dk_ref, dv_ref, dz_ref, dps_ref, dcw_ref, dsm_ref,
             qn, kn, vv, g128, b128, do_s, bg_s, u_s, vn_s, dvn_s, dcy_s, kk_s, qk_s, tm_s, dsn_s, pad_s):
        b, h = pl.program_id(0), pl.program_id(1)
        g, beta, _, _ = _gdn_gates(ps_ref[...], h, alog_ref, dtb_ref)
        g128[...] = jnp.broadcast_to(g, (t, LANES))
        b128[...] = jnp.broadcast_to(beta, (t, LANES))
        _pad_zero(pad_s)

        def prep(x_ref, w_ref):
            p = _conv(x_ref[...], w_ref, pad_s)
            sg = _sigmoid(p)
            return p, sg, p * sg

        _, _, yq = prep(q_ref, wq_ref)
        qn[...] = yq * (lax.rsqrt(jnp.sum(yq * yq, axis=1, keepdims=True) + EPS) * (DH ** -0.5))
        _, _, yk = prep(k_ref, wk_ref)
        kn[...] = yk * lax.rsqrt(jnp.sum(yk * yk, axis=1, keepdims=True) + EPS)
        _, _, yv = prep(v_ref, wv_ref)
        vv[...] = yv

        o = oraw_ref[...]
        z = z_ref[...]
        doa = doa_ref[...]
        gn = gn_ref[...]
        ro = lax.rsqrt(jnp.mean(o * o, axis=1, keepdims=True) + EPS)
        sz = _sigmoid(z)
        dz_ref[...] = _bf(doa * (o * ro * gn) * (sz * (1.0 + z * (1.0 - sz))))
        dn = doa * (z * sz)
        dgn = jnp.sum(dn * o * ro, axis=0, keepdims=True)
        gy = dn * gn
        do_s[...] = ro * gy - o * (ro * ro * ro * (1.0 / DH)) * jnp.sum(gy * o, axis=1, keepdims=True)

        cs = _chunk_consts()

        def pre_group(gi, _):
            idx = [gi * GDN_GROUP + c for c in range(GDN_GROUP)]
            rows = [pl.ds(pl.multiple_of(i * CHUNK, CHUNK), CHUNK) for i in idx]
            ins = [_V([ref[r, :] for r in rows]) for ref in (qn, kn, vv, g128, b128)]
            states = _V([_bf(s_ref[0, 0, i]) for i in idx])
            f = _gdn_chunk_pre(*ins, cs)
            v_new = f["w"] - _vdot(_vbf(f["u"]), states)
            for c, r in enumerate(rows):
                bg_s[r, :] = f["big_g"].xs[c]
                u_s[r, :] = f["u"].xs[c]
                vn_s[r, :] = v_new.xs[c]
                dcy_s[r, :] = f["decay"].xs[c]
                kk_s[r, :] = f["kk"].xs[c]
                qk_s[r, :] = f["qk"].xs[c]
                tm_s[r, :] = f["tm"].xs[c]
            return 0

        lax.fori_loop(0, nc // GDN_GROUP, pre_group, 0)

        def chunk(j, ds):
            i = nc - 1 - j
            r = pl.ds(pl.multiple_of(i * CHUNK, CHUNK), CHUNK)
            big_g = bg_s[r, :]
            g_last = jnp.sum(g128[r, :], axis=0, keepdims=True)
            dob = _bf(do_s[r, :])
            dv_new = (_dot_tn(_bf(qk_s[r, :] * dcy_s[r, :]), dob)
                      + _dot(_bf(kn[r, :] * jnp.exp(g_last - big_g)), _bf(ds)))
            dvn_s[r, :] = dv_new
            dsn_s[i] = ds
            return (_dot_tn(_bf(qn[r, :] * jnp.exp(big_g)), dob) + jnp.exp(g_last) * ds
                    - _dot_tn(_bf(u_s[r, :]), _bf(dv_new)))

        lax.fori_loop(0, nc, chunk, jnp.zeros((DH, DH), F32))

        def post_group(gi, _):
            idx = [gi * GDN_GROUP + c for c in range(GDN_GROUP)]
            rows = [pl.ds(pl.multiple_of(i * CHUNK, CHUNK), CHUNK) for i in idx]
            def rows_of(ref):
                return _V([ref[r, :] for r in rows])

            dq, dk, dv, dg, dbeta = _gdn_chunk_post(
                rows_of(qn), rows_of(kn), rows_of(vv), rows_of(g128), rows_of(b128),
                _V([s_ref[0, 0, i] for i in idx]), _V([dsn_s[i] for i in idx]), rows_of(do_s), rows_of(dvn_s),
                rows_of(bg_s), rows_of(dcy_s), rows_of(kk_s), rows_of(qk_s), rows_of(tm_s), rows_of(u_s), rows_of(vn_s),
                cs)
            for c, r in enumerate(rows):
                qn[r, :] = dq.xs[c]
                kn[r, :] = dk.xs[c]
                vv[r, :] = dv.xs[c]
                g128[r, :] = dg.xs[c]
                b128[r, :] = jnp.broadcast_to(dbeta.xs[c], (CHUNK, LANES))
            return 0

        lax.fori_loop(0, nc // GDN_GROUP, post_group, 0)
        dqh, dkh, dvh = qn, kn, vv

        g, beta, a, sp_in = _gdn_gates(ps_ref[...], h, alog_ref, dtb_ref)
        dg = g128[...]
        d_ga = dg * (-a) * _sigmoid(sp_in)
        d_alog = jnp.sum(dg * g, axis=0, keepdims=True)
        d_dtb = jnp.sum(d_ga, axis=0, keepdims=True)
        d_gb = b128[...] * (beta * (1.0 - beta))
        lane = lax.broadcasted_iota(jnp.int32, (t, LANES), 1)
        contrib = jnp.where(lane == LANE_GA + h, d_ga, 0.0) + jnp.where(lane == LANE_GB + h, d_gb, 0.0)

        @pl.when(h == 0)
        def _():
            dps_ref[...] = jnp.zeros_like(dps_ref)

        dps_ref[...] += contrib

        lane1 = lax.broadcasted_iota(jnp.int32, (1, LANES), 1)
        small = _stack_rows([jnp.where(lane1 == h, d_alog, 0.0), jnp.where(lane1 == h, d_dtb, 0.0), dgn], 8)

        @pl.when((b == 0) & (h == 0))
        def _():
            dsm_ref[...] = jnp.zeros_like(dsm_ref)
            dcw_ref[...] = jnp.zeros_like(dcw_ref)

        dsm_ref[...] += small

        def conv_bwd(dp, x, w_ref, slot):
            dw = _stack_rows([jnp.sum(dp * _shifted(pad_s, 3), axis=0, keepdims=True),
                              jnp.sum(dp * _shifted(pad_s, 2), axis=0, keepdims=True),
                              jnp.sum(dp * _shifted(pad_s, 1), axis=0, keepdims=True),
                              jnp.sum(dp * x, axis=0, keepdims=True)], CONV_K)
            dcw_ref[slot] += dw
            pad_s[PAD:PAD + t, :] = dp
            dx = _shifted(pad_s, -3) * w_ref[0:1, :]
            dx = dx + _shifted(pad_s, -2) * w_ref[1:2, :]
            dx = dx + _shifted(pad_s, -1) * w_ref[2:3, :]
            return dx + dp * w_ref[3:4, :]

        def l2_bwd(dqn, y, c):
            r = lax.rsqrt(jnp.sum(y * y, axis=1, keepdims=True) + EPS)
            s1 = jnp.sum(dqn * y, axis=1, keepdims=True)
            return c * r * dqn - (c * r * r * r) * s1 * y

        def silu_bwd(p, sg):
            return sg * (1.0 + p * (1.0 - sg))

        pq, sq, yq = prep(q_ref, wq_ref)
        dq_ref[...] = _bf(conv_bwd(l2_bwd(dqh[...], yq, DH ** -0.5) * silu_bwd(pq, sq), q_ref[...], wq_ref, h))
        pk, sk, yk = prep(k_ref, wk_ref)
        dk_ref[...] = _bf(conv_bwd(l2_bwd(dkh[...], yk, 1.0) * silu_bwd(pk, sk), k_ref[...], wk_ref, HEADS + h))
        pv, sv, _ = prep(v_ref, wv_ref)
        dv_ref[...] = _bf(conv_bwd(dvh[...] * silu_bwd(pv, sv), v_ref[...], wv_ref, 2 * HEADS + h))

    blk = pl.BlockSpec((t, DH), lambda b, h: (b, h))
    ob = jax.ShapeDtypeStruct((n, HEADS * DH), BF16)
    return pl.pallas_call(
        body, name="gdn_bwd", grid=(b_loc, HEADS),
        in_specs=[col(0), col(HEADS), col(2 * HEADS), col(3 * HEADS), ps_spec, wcol(0), wcol(HEADS), wcol(2 * HEADS),
                  smem, smem, vec, blk, blk, pl.BlockSpec((1, 1, nc, DH, DH), lambda b, h: (b, h, 0, 0, 0))],
        out_specs=[blk, blk, blk, blk, ps_spec,
                   pl.BlockSpec((3 * HEADS, CONV_K, DH), lambda b, h: (0, 0, 0)),
                   pl.BlockSpec((8, LANES), lambda b, h: (0, 0))],
        out_shape=[ob, ob, ob, ob, jax.ShapeDtypeStruct((n, LANES), F32),
                   jax.ShapeDtypeStruct((3 * HEADS, CONV_K, DH), F32), jax.ShapeDtypeStruct((8, LANES), F32)],
        scratch_shapes=([pltpu.VMEM((t, DH), F32)] * 3 + [pltpu.VMEM((t, LANES), F32)] * 2
                        + [pltpu.VMEM((t, DH), F32)] * 5 + [pltpu.VMEM((t, CHUNK), F32)] * 4
                        + [pltpu.VMEM((nc, DH, DH), F32), pltpu.VMEM((t + 2 * PAD, LANES), F32)]),
        compiler_params=_cparams(("arbitrary", "arbitrary")),
    )(pg, pg, pg, pg, ps, convw, convw, convw, a_log, dt_bias, gnorm, d_oa, o_raw, s_all)


def _fox_prologue(q_ref, k_ref, v_ref, ps_ref, fb_ref, gq_ref, gk_ref, h, t, qs, ks, vs, ccol, crow):
    nb = t // FOX_BLOCK
    q, k = q_ref[...], k_ref[...]
    rq = lax.rsqrt(jnp.mean(q * q, axis=1, keepdims=True) + EPS)
    rk = lax.rsqrt(jnp.mean(k * k, axis=1, keepdims=True) + EPS)
    qs[...] = _bf(q * rq * gq_ref[...])
    ks[...] = _bf(k * rk * gk_ref[...])
    vs[...] = _bf(v_ref[...])
    f_in = _head_lane(ps_ref[...], LANE_FF + h) + fb_ref[0, h]
    ccol[...] = jnp.broadcast_to(-_softplus(-f_in), (t, LANES))
    r = lax.broadcasted_iota(jnp.int32, (FOX_BLOCK, FOX_BLOCK), 0)
    c = lax.broadcasted_iota(jnp.int32, (FOX_BLOCK, FOX_BLOCK), 1)
    trilf, triuf = (r >= c).astype(F32), (r <= c).astype(F32)
    blocks = [pl.ds(j * FOX_BLOCK, FOX_BLOCK) for j in range(nb)]
    lfs = _V([ccol[rb, :] for rb in blocks])
    cc = _dot_exact_l(trilf, lfs, terms=3)
    cr = _dot_exact_r(lfs, triuf, _vdot_tn, terms=3)
    sums = _vsum(lfs, 0)
    carry = jnp.zeros((1, LANES), F32)
    for j, rb in enumerate(blocks):
        ccol[rb, :] = cc.xs[j] + carry
        crow[j] = (cr.xs[j] + carry)[0:8]
        carry = carry + sums.xs[j]
    return rq, rk, f_in


def _fox_scores(q_rows, k_rows, cc, cr, row0, col0, masked=True):
    s = _dot_nt(q_rows, k_rows) * (DH ** -0.5) + cc - cr
    if not masked:
        return s
    r = lax.broadcasted_iota(jnp.int32, s.shape, 0)
    c = lax.broadcasted_iota(jnp.int32, s.shape, 1)
    return jnp.where(row0 + r >= col0 + c, s, NEG)


def _fox_specs(t):
    def col(off):
        return pl.BlockSpec((t, DH), lambda b, h: (b, off + h))

    ps_spec = pl.BlockSpec((t, LANES), lambda b, h: (b, 0))
    smem = pl.BlockSpec(memory_space=pltpu.SMEM)
    vec = pl.BlockSpec((1, DH), lambda b, h: (0, 0))
    blk = pl.BlockSpec((t, DH), lambda b, h: (b, h))
    return col, ps_spec, smem, vec, blk


def _fox_fwd(pf, ps, f_bias, gq, gk, b_loc, t):
    n = b_loc * t
    nb = t // FOX_BLOCK
    assert t % FOX_TILE == 0 and FOX_TILE % FOX_SHORT == 0, (t, FOX_TILE, FOX_SHORT)
    kt = FOX_TILE
    nsub = kt // FOX_BLOCK
    col, ps_spec, smem, vec, blk = _fox_specs(t)

    def body(q_ref, k_ref, v_ref, ps_ref, fb_ref, gq_ref, gk_ref, o_ref, lse_ref, qs, ks, vs, ccol, crow):
        h = pl.program_id(1)
        _fox_prologue(q_ref, k_ref, v_ref, ps_ref, fb_ref, gq_ref, gk_ref, h, t, qs, ks, vs, ccol, crow)

        def qblock(i, _):
            ri = pl.ds(pl.multiple_of(i * FOX_SHORT, FOX_SHORT), FOX_SHORT)
            qi = qs[ri, :]
            cc = jnp.concatenate([ccol[ri, :]] * nsub, axis=1)

            def ktile(j, carry, masked):
                m, l, acc = carry
                rj = pl.ds(pl.multiple_of(j * kt, kt), kt)
                cr = jnp.concatenate([crow[j * nsub + u, 0:1, :] for u in range(nsub)], axis=1)
                s = _fox_scores(qi, ks[rj, :], cc, cr, i * FOX_SHORT, j * kt, masked)
                m_new = jnp.maximum(m, jnp.max(s, axis=1, keepdims=True))
                p = jnp.exp(s - m_new)
                alpha = jnp.exp(m - m_new)
                l = alpha * l + jnp.sum(p, axis=1, keepdims=True)
                acc = alpha * acc + _dot(_bf(p), vs[rj, :])
                return m_new, l, acc

            last = (i * FOX_SHORT) // kt
            carry = lax.fori_loop(0, last, functools.partial(ktile, masked=False),
                                  (jnp.full((FOX_SHORT, 1), NEG, F32), jnp.zeros((FOX_SHORT, 1), F32),
                                   jnp.zeros((FOX_SHORT, DH), F32)))
            m, l, acc = ktile(last, carry, True)
            o_ref[ri, :] = acc / l
            lse_ref[ri, :] = jnp.broadcast_to(m + jnp.log(l), (FOX_SHORT, LANES))
            return 0

        lax.fori_loop(0, t // FOX_SHORT, qblock, 0)

    o = jax.ShapeDtypeStruct((n, HEADS * DH), F32)
    return pl.pallas_call(
        body, name="fox_fwd", grid=(b_loc, HEADS),
        in_specs=[col(0), col(HEADS), col(2 * HEADS), ps_spec, smem, vec, vec],
        out_specs=[blk, blk], out_shape=[o, o],
        scratch_shapes=[pltpu.VMEM((t, DH), BF16)] * 3 + [pltpu.VMEM((t, LANES), F32), pltpu.VMEM((nb, 8, LANES), F32)],
        compiler_params=_cparams(("arbitrary", "arbitrary")),
    )(pf, pf, pf, ps, f_bias, gq, gk)


def _fox_bwd(pf, ps, f_bias, gq, gk, d_ob, ob, lse, dps_in, b_loc, t):
    n = b_loc * t
    nb = t // FOX_BLOCK
    assert t % FOX_TILE == 0 and FOX_TILE % FOX_SHORT == 0, (t, FOX_TILE, FOX_SHORT)
    qt = FOX_TILE
    scale = DH ** -0.5
    col, ps_spec, smem, vec, blk = _fox_specs(t)

    def body(q_ref, k_ref, v_ref, ps_ref, fb_ref, gq_ref, gk_ref, do_ref, o_ref, lse_ref, dpsi_ref,
             dq_ref, dk_ref, dv_ref, dps_ref, dsm_ref, qs, ks, vs, ccol, crow, dos, dl, dqa, dcr, dcq):
        b, h = pl.program_id(0), pl.program_id(1)
        rq, _, f_in = _fox_prologue(q_ref, k_ref, v_ref, ps_ref, fb_ref, gq_ref, gk_ref, h, t, qs, ks, vs, ccol, crow)
        dov = do_ref[...]
        dos[...] = _bf(dov)
        dl[...] = jnp.broadcast_to(jnp.sum(dov * o_ref[...], axis=1, keepdims=True), (t, LANES))
        dqa[...] = jnp.zeros_like(dqa)
        dcq[...] = jnp.zeros_like(dcq)
        gkv = gk_ref[...]

        ksub = FOX_SHORT // FOX_BLOCK

        def kblock(j, dgk):
            rj = pl.ds(pl.multiple_of(j * FOX_SHORT, FOX_SHORT), FOX_SHORT)
            kj, vj = ks[rj, :], vs[rj, :]
            cr = jnp.concatenate([crow[j * ksub + u, 0:1, :] for u in range(ksub)], axis=1)

            def wide(x):
                return jnp.concatenate([x] * ksub, axis=1)

            def qtile(i, carry, masked):
                dk_acc, dv_acc, dc = carry
                ri = pl.ds(pl.multiple_of(i * qt, qt), qt)
                qi, doi = qs[ri, :], dos[ri, :]
                s = _fox_scores(qi, kj, wide(ccol[ri, :]), cr, i * qt, j * FOX_SHORT, masked)
                p = jnp.exp(s - wide(lse_ref[ri, :]))
                ds = p * (_dot_nt(doi, vj) - wide(dl[ri, :]))
                dsb = _bf(ds)
                dqa[ri, :] += _dot(dsb, kj)
                dcq[ri, :] += jnp.broadcast_to(jnp.sum(ds, axis=1, keepdims=True), (qt, LANES))
                return (dk_acc + _dot_tn(dsb, qi), dv_acc + _dot_tn(_bf(p), doi),
                        dc - jnp.sum(ds, axis=0, keepdims=True))

            z = jnp.zeros((FOX_SHORT, DH), F32)
            first = (j * FOX_SHORT) // qt
            carry = qtile(first, (z, z, jnp.zeros((1, FOX_SHORT), F32)), True)
            dk_acc, dv_acc, dc = lax.fori_loop(first + 1, t // qt, functools.partial(qtile, masked=False), carry)
            dv_ref[rj, :] = _bf(dv_acc)
            for u in range(ksub):
                dcr[pl.ds(pl.multiple_of((j * ksub + u) * 8, 8), 8), :] = jnp.broadcast_to(
                    dc[:, u * FOX_BLOCK:(u + 1) * FOX_BLOCK], (8, LANES))
            kraw = k_ref[rj, :]
            rk = lax.rsqrt(jnp.mean(kraw * kraw, axis=1, keepdims=True) + EPS)
            dkn = dk_acc * scale
            gy = dkn * gkv
            dk_ref[rj, :] = _bf(rk * gy - kraw * (rk * rk * rk * (1.0 / DH)) * jnp.sum(gy * kraw, axis=1, keepdims=True))
            return dgk + jnp.sum(dkn * kraw * rk, axis=0, keepdims=True)

        dgk = lax.fori_loop(0, t // FOX_SHORT, kblock, jnp.zeros((1, DH), F32))

        q = q_ref[...]
        dqn = dqa[...] * scale
        gy = dqn * gq_ref[...]
        dq_ref[...] = _bf(rq * gy - q * (rq * rq * rq * (1.0 / DH)) * jnp.sum(gy * q, axis=1, keepdims=True))
        dgq = jnp.sum(dqn * q * rq, axis=0, keepdims=True)

        r = lax.broadcasted_iota(jnp.int32, (FOX_BLOCK, FOX_BLOCK), 0)
        c = lax.broadcasted_iota(jnp.int32, (FOX_BLOCK, FOX_BLOCK), 1)
        triuf = (r <= c).astype(F32)

        def rev(jj, carry):
            j = nb - 1 - jj
            rows = pl.ds(pl.multiple_of(j * FOX_BLOCK, FOX_BLOCK), FOX_BLOCK)
            rowv = dcr[pl.ds(pl.multiple_of(j * 8, 8), 1), :]
            colv = jnp.sum(jnp.where(c >= r, jnp.broadcast_to(rowv, (FOX_BLOCK, LANES)), 0.0), axis=1, keepdims=True)
            qcol = dcq[rows, :]
            dl[rows, :] = colv + _dot_exact_l(triuf, qcol, terms=3) + carry
            return carry + jnp.sum(rowv, axis=1, keepdims=True) + jnp.sum(qcol, axis=0, keepdims=True)

        lax.fori_loop(0, nb, rev, jnp.zeros((1, LANES), F32))
        d_ff = dl[...] * _sigmoid(-f_in)
        lane = lax.broadcasted_iota(jnp.int32, (t, LANES), 1)

        @pl.when(h == 0)
        def _():
            dps_ref[...] = dpsi_ref[...]

        dps_ref[...] += jnp.where(lane == LANE_FF + h, d_ff, 0.0)

        lane1 = lax.broadcasted_iota(jnp.int32, (1, LANES), 1)
        d_fb = jnp.sum(d_ff, axis=0, keepdims=True)
        small = _stack_rows([dgq, dgk, jnp.where(lane1 == h, d_fb, 0.0)], 8)

        @pl.when((b == 0) & (h == 0))
        def _():
            dsm_ref[...] = jnp.zeros_like(dsm_ref)

        dsm_ref[...] += small

    ob_ = jax.ShapeDtypeStruct((n, HEADS * DH), BF16)
    return pl.pallas_call(
        body, name="fox_bwd", grid=(b_loc, HEADS),
        in_specs=[col(0), col(HEADS), col(2 * HEADS), ps_spec, smem, vec, vec, blk, blk, blk, ps_spec],
        out_specs=[blk, blk, blk, ps_spec, pl.BlockSpec((8, LANES), lambda b, h: (0, 0))],
        out_shape=[ob_, ob_, ob_, jax.ShapeDtypeStruct((n, LANES), F32), jax.ShapeDtypeStruct((8, LANES), F32)],
        scratch_shapes=([pltpu.VMEM((t, DH), BF16)] * 3 + [pltpu.VMEM((t, LANES), F32), pltpu.VMEM((nb, 8, LANES), F32)]
                        + [pltpu.VMEM((t, DH), BF16), pltpu.VMEM((t, LANES), F32), pltpu.VMEM((t, DH), F32),
                           pltpu.VMEM((8 * nb, LANES), F32), pltpu.VMEM((t, LANES), F32)]),
        compiler_params=_cparams(("arbitrary", "arbitrary")),
    )(pf, pf, pf, ps, f_bias, gq, gk, d_ob, ob, lse, dps_in)


class _NoExchange:
    def late_weights(self, after):
        return {}

    def grads_ready(self, grads, tie):
        return tie


def _local_step(x, target, w, b_loc, t, comm=None):
    comm = comm or _NoExchange()
    w = dict(w)
    xf = x
    u = _rms_fwd(xf, w["norm_mix_g"], "rms_mix")
    pg = _mm(u, w["w_gdn"], name="proj_gdn")
    pf = _mm(u, w["w_fox"], name="proj_fox")
    pgate = _mm(u, w["w_gate"], name="proj_gate")
    ps = _mm(u, w["w_small"], name="proj_small")
    oa, o_raw, s_all = _gdn_fwd(pg, ps, w["conv_w"], w["a_log"], w["dt_bias"], w["gdn_norm_g"], b_loc, t)
    ob, lse = _fox_fwd(pf, ps, w["f_bias"], w["fox_q_norm_g"], w["fox_k_norm_g"], b_loc, t)
    w.update(comm.late_weights(ob))
    ya = _mm(oa, w["w_proj_gdn"], name="proj_a")
    yb = _mm(ob, w["w_proj_fox"], name="proj_b")
    merged = _merge_fwd(ya, yb, pgate)
    h = _mm(merged, w["w_out"], name="proj_out", epi=lambda acc, xr: acc + xr, extras=(xf,))
    hn = _rms_fwd(h, w["norm_mlp_g"], "rms_mlp")
    up, act = _mm(hn, w["w_up"], name="mlp_up", out_dtype=BF16, out2=(_relu2, BF16))
    out = _mm(act, w["w_down"], name="mlp_down", tk=2 * MM_TILE, epi=lambda acc, hr: acc + hr, extras=(h,))
    d_out, d_out16, loss_blk = _loss_bwd(out, target)

    g = {}
    g["w_down"] = _mm(act, d_out16, name="dw_down", ta=True, out_dtype=BF16)
    d_up = _mm(d_out16, w["w_down"], name="d_up", tb=True, out_dtype=BF16,
               epi=lambda acc, upr: acc * (2.0 * jnp.maximum(upr.astype(F32), 0.0)), extras=(up,))
    g["w_up"] = _mm(hn, d_up, name="dw_up", ta=True, out_dtype=BF16)
    mlp_gain = comm.grads_ready({"w_down": g["w_down"], "w_up": g["w_up"]}, w["norm_mlp_g"])
    d_hn = _mm(d_up, w["w_up"], name="d_hn", tb=True, tk=2 * MM_TILE)
    dh, dh16, g["norm_mlp_g"] = _rms_bwd(d_hn, h, mlp_gain, d_out, "rms_mlp_bwd")
    g["w_out"] = _mm(merged, dh16, name="dw_out", ta=True, out_dtype=BF16)
    dm = _mm(dh16, w["w_out"], name="d_merged", tb=True)
    dya, dyb, dgate_a, dgate_b = _merge_bwd(dm, ya, yb, pgate)
    g["w_proj_gdn"] = _mm(oa, dya, name="dw_proj_a", ta=True, out_dtype=BF16)
    g["w_proj_fox"] = _mm(ob, dyb, name="dw_proj_b", ta=True, out_dtype=BF16)
    gdn_gain = comm.grads_ready({"w_out": g["w_out"], "w_proj_gdn": g["w_proj_gdn"], "w_proj_fox": g["w_proj_fox"]},
                                w["gdn_norm_g"])
    d_oa = _mm(dya, w["w_proj_gdn"], name="d_oa", tb=True)
    d_ob = _mm(dyb, w["w_proj_fox"], name="d_ob", tb=True)
    dgq, dgk, dgv, dgz, dps, dcw, gdn_small = _gdn_bwd(pg, ps, w["conv_w"], w["a_log"], w["dt_bias"], gdn_gain,
                                                       d_oa, o_raw, s_all, b_loc, t)
    dfq, dfk, dfv, dps, fox_small = _fox_bwd(pf, ps, w["f_bias"], w["fox_q_norm_g"], w["fox_k_norm_g"],
                                             d_ob, ob, lse, dps, b_loc, t)
    segs = [(dgq, "w_gdn", 0), (dgk, "w_gdn", 1024), (dgv, "w_gdn", 2048), (dgz, "w_gdn", 3072),
            (dfq, "w_fox", 0), (dfk, "w_fox", 1024), (dfv, "w_fox", 2048),
            (dgate_a, "w_gate", 0), (dgate_b, "w_gate", 1024)]
    dws = [_mm(u, dps, name="dw_small", ta=True, out_dtype=BF16)]
    dws += [_mm(u, dseg, name=f"dw_in_{idx}", ta=True, out_dtype=BF16) for idx, (dseg, _, _) in enumerate(segs)]
    g["w_in_parts"] = dws
    mix_gain = comm.grads_ready({"w_in_parts": dws}, w["norm_mix_g"])
    du = _du_all(dps, w["w_small"], segs, w)
    grad_x, _, g["norm_mix_g"] = _rms_bwd(du, xf, mix_gain, dh, "rms_mix_bwd")
    g["conv"] = dcw
    g["gdn_small"] = gdn_small
    g["fox_small"] = fox_small
    return loss_blk, grad_x, g


def _position():
    x, y, c = lax.axis_index("x"), lax.axis_index("y"), lax.axis_index("c")
    return x, y, c


def _to_bf16(arrs, name):
    n = len(arrs)

    def body(*refs):
        for i in range(n):
            refs[n + i][...] = _bf(refs[i][...])

    return pl.pallas_call(
        body, name=name,
        out_shape=[jax.ShapeDtypeStruct(a.shape, BF16) for a in arrs],
        compiler_params=_cparams(),
    )(*arrs)


def _all_gather(arrs, name):
    n = len(arrs)
    hbm = pl.BlockSpec(memory_space=pl.ANY)

    def body(*refs):
        ins, outs = refs[:n], refs[n:2 * n]
        send, recv, loc = refs[2 * n:]
        x, y, c = _position()
        me = 4 * x + 2 * y + c
        sibling = (x, y, 1 - c)
        chips = [(1 - x, y), (x, 1 - y), (1 - x, 1 - y)]

        def idx(px, py, pc):
            return 4 * px + 2 * py + pc

        def cp(a, k, block, to, src=None):
            return pltpu.make_async_remote_copy(
                src_ref=outs[a].at[block] if src is None else src, dst_ref=outs[a].at[block],
                send_sem=send.at[a, k], recv_sem=recv.at[a, k], device_id=to, device_id_type=MESH)

        mine = [pltpu.make_async_copy(ins[a], outs[a].at[me], loc.at[a]) for a in range(n)]
        for m in mine:
            m.start()
        first = []
        for a in range(n):
            first.append(cp(a, 0, me, sibling, src=ins[a]))
            first += [cp(a, 1 + j, me, (*chip, c), src=ins[a]) for j, chip in enumerate(chips)]
        for f in first:
            f.start()
        passed = []
        for j, chip in enumerate(chips):
            for a in range(n):
                cp(a, 1 + j, idx(*chip, c), (x, y, c)).wait_recv()
                p = cp(a, 4 + j, idx(*chip, c), sibling)
                p.start()
                passed.append(p)
        for a in range(n):
            cp(a, 0, idx(x, y, 1 - c), (x, y, c)).wait_recv()
            for j, chip in enumerate(chips):
                cp(a, 4 + j, idx(*chip, 1 - c), (x, y, c)).wait_recv()
        for f in first + passed:
            f.wait_send()
        for m in mine:
            m.wait()

    return pl.pallas_call(
        body, name=name,
        in_specs=[hbm] * n, out_specs=[hbm] * n,
        out_shape=[jax.ShapeDtypeStruct((N_DEV,) + a.shape, a.dtype) for a in arrs],
        scratch_shapes=[pltpu.SemaphoreType.DMA((n, 7)), pltpu.SemaphoreType.DMA((n, 7)), pltpu.SemaphoreType.DMA((n,))],
        compiler_params=pltpu.CompilerParams(has_side_effects=True),
    )(*arrs)


def _peer(x, y, c, rel):
    return ((1 - x) if rel & 4 else x, (1 - y) if rel & 2 else y, (1 - c) if rel & 1 else c)


HBM_SPEC = pl.BlockSpec(memory_space=pltpu.HBM)
SEM_SPEC = pl.BlockSpec(memory_space=pltpu.SEMAPHORE)
DATAFLOW = pltpu.SideEffectType.DATAFLOW_SIDE_EFFECTING


CHIP_RELS = (2, 4, 6)


def _push_start(arrs, slots, name, chips=False):
    n = len(arrs)
    n_slots = 4 if chips else N_DEV
    rels = CHIP_RELS if chips else tuple(range(1, N_DEV))
    land_shapes = [a.shape if slots else (n_slots,) + a.shape for a in arrs]

    def body(*refs):
        ins, lands, sends, recvs, token = refs[:n], refs[n:2 * n], refs[2 * n:3 * n], refs[3 * n:4 * n], refs[-1]
        x, y, c = _position()
        for rel in rels:
            px, py, pc = _peer(x, y, c, rel)
            mine, theirs = (2 * x + y, 2 * px + py) if chips else (4 * x + 2 * y + c, 4 * px + 2 * py + pc)
            for a in range(n):
                pltpu.make_async_remote_copy(
                    src_ref=ins[a].at[theirs] if slots else ins[a], dst_ref=lands[a].at[mine],
                    send_sem=sends[a], recv_sem=recvs[a], device_id=(px, py, pc), device_id_type=MESH).start()
        token[...] = jnp.zeros_like(token)

    sem = pltpu.SemaphoreType.DMA(())
    outs = pl.pallas_call(
        body, name=name,
        in_specs=[HBM_SPEC] * (2 * n),
        out_shape=(*[sem] * (2 * n), *[pltpu.HBM(a.shape, a.dtype) for a in arrs],
                   *[pltpu.HBM(s, a.dtype) for s, a in zip(land_shapes, arrs)], jax.ShapeDtypeStruct((8, LANES), F32)),
        out_specs=(*[SEM_SPEC] * (2 * n), *[HBM_SPEC] * (2 * n), pl.BlockSpec(memory_space=pltpu.VMEM)),
        input_output_aliases={i: 2 * n + i for i in range(2 * n)},
        compiler_params=pltpu.CompilerParams(has_side_effects=DATAFLOW),
    )(*[pltpu.with_memory_space_constraint(a, pltpu.HBM) for a in arrs],
      *[pltpu.with_memory_space_constraint(lax.empty(s, a.dtype), pltpu.HBM) for s, a in zip(land_shapes, arrs)])
    return dict(sends=list(outs[:n]), recvs=list(outs[n:2 * n]), ins=list(outs[2 * n:3 * n]),
                lands=list(outs[3 * n:4 * n]), token=outs[-1], copies=len(rels))


def _push_wait(started, after, name):
    n = len(started["ins"])
    copies = started["copies"]

    def body(*refs):
        lands, sends, recvs = refs[n:2 * n], refs[2 * n:3 * n], refs[3 * n:4 * n]
        x, y, c = _position()
        for a in range(n):
            every = lands[a].at[pl.ds(0, copies)]
            drain = pltpu.make_async_remote_copy(src_ref=every, dst_ref=every, send_sem=sends[a], recv_sem=recvs[a],
                                                 device_id=(x, y, c), device_id_type=MESH)
            drain.wait_send()
            drain.wait_recv()

    both = started["ins"] + started["lands"]
    outs = pl.pallas_call(
        body, name=name,
        in_specs=[HBM_SPEC] * (2 * n) + [SEM_SPEC] * (2 * n) + [pl.BlockSpec(memory_space=pl.ANY)],
        out_shape=tuple(pltpu.HBM(a.shape, a.dtype) for a in both), out_specs=tuple([HBM_SPEC] * (2 * n)),
        input_output_aliases={i: i for i in range(2 * n)},
        compiler_params=pltpu.CompilerParams(has_side_effects=DATAFLOW),
    )(*both, *started["sends"], *started["recvs"], after)
    return list(outs[:n]), list(outs[n:])


def _sibling_swap(arr, name):
    chips = N_DEV // 2

    def body(in_ref, out_ref, send, recv):
        x, y, c = _position()
        for s in range(chips):
            pltpu.make_async_remote_copy(src_ref=in_ref.at[2 * s + 1 - c], dst_ref=out_ref.at[s], send_sem=send,
                                         recv_sem=recv, device_id=(x, y, 1 - c), device_id_type=MESH).start()
        pltpu.make_async_remote_copy(src_ref=out_ref, dst_ref=out_ref, send_sem=send, recv_sem=recv,
                                     device_id=(x, y, 1 - c), device_id_type=MESH).wait()

    hbm = pl.BlockSpec(memory_space=pl.ANY)
    return pl.pallas_call(
        body, name=name, in_specs=[hbm], out_specs=hbm,
        out_shape=jax.ShapeDtypeStruct((chips,) + arr.shape[1:], arr.dtype),
        scratch_shapes=[pltpu.SemaphoreType.DMA, pltpu.SemaphoreType.DMA],
        compiler_params=pltpu.CompilerParams(has_side_effects=True),
    )(arr)


def _add_halves(core, arr, other, name):
    ns, r, c = other.shape
    tr = min(r, 256)

    def body(core_ref, a_ref, o_ref, out_ref):
        out_ref[...] = _bf(a_ref[...].astype(F32) + o_ref[...].astype(F32))

    blk = pl.BlockSpec((1, tr, c), lambda s, i, core_ref: (s, i, 0))
    return pl.pallas_call(
        body, name=name,
        grid_spec=pltpu.PrefetchScalarGridSpec(
            num_scalar_prefetch=1, grid=(ns, r // tr),
            in_specs=[pl.BlockSpec((1, tr, c), lambda s, i, core_ref: (2 * s + core_ref[0], i, 0)), blk],
            out_specs=blk),
        out_shape=jax.ShapeDtypeStruct((ns, r, c), BF16),
        compiler_params=_cparams(("parallel", "parallel")),
    )(core, arr, other)


def _all_reduce_small(buf, name):
    rows = buf.shape[0]

    def body(in_ref, out_ref, slots, send, recv):
        x, y, c = _position()
        me = 4 * x + 2 * y + c
        slots[me] = in_ref[...]
        copies = []
        for rel in range(1, N_DEV):
            copies.append(pltpu.make_async_remote_copy(
                src_ref=in_ref, dst_ref=slots.at[me], send_sem=send.at[rel - 1], recv_sem=recv.at[rel - 1],
                device_id=_peer(x, y, c, rel), device_id_type=MESH))
        for cpy in copies:
            cpy.start()
        for cpy in copies:
            cpy.wait()
        tot = slots[0]
        for d in range(1, N_DEV):
            tot = tot + slots[d]
        out_ref[...] = tot

    return pl.pallas_call(
        body, name=name,
        out_shape=jax.ShapeDtypeStruct((rows, LANES), F32),
        in_specs=[pl.BlockSpec(memory_space=pltpu.VMEM)], out_specs=pl.BlockSpec(memory_space=pltpu.VMEM),
        scratch_shapes=[pltpu.VMEM((N_DEV, rows, LANES), F32), pltpu.SemaphoreType.DMA((7,)),
                        pltpu.SemaphoreType.DMA((7,))],
        compiler_params=pltpu.CompilerParams(has_side_effects=True),
    )(buf)


def _adam_math(g, w, m, v):
    m = ADAM_B1 * m + (1.0 - ADAM_B1) * g
    v = ADAM_B2 * v + (1.0 - ADAM_B2) * (g * g)
    m_hat = m / (1.0 - ADAM_B1 ** ADAM_STEP)
    v_hat = v / (1.0 - ADAM_B2 ** ADAM_STEP)
    delta = -ADAM_LR * (m_hat / (jnp.sqrt(v_hat) + ADAM_EPS) + ADAM_WD * w)
    return delta, m, v


def _adam_shard(me, parts, mine, w, m, v, name):
    r, c = w.shape
    tr = min(r, 128)
    n_slots = parts.shape[0]

    def body(me_ref, p_ref, own_ref, w_ref, m_ref, v_ref, g_ref, d_ref, nm_ref, nv_ref):
        own = own_ref[0].astype(F32)
        g = None
        for s in range(n_slots):
            term = jnp.where(me_ref[0] == s, own, p_ref[s].astype(F32))
            g = term if g is None else g + term
        d, nm, nv = _adam_math(g, w_ref[...], m_ref[...], v_ref[...])
        g_ref[...] = g
        d_ref[...] = d
        nm_ref[...] = nm
        nv_ref[...] = nv

    row = pl.BlockSpec((tr, c), lambda i, me_ref: (i, 0))
    o = jax.ShapeDtypeStruct((r, c), F32)
    return pl.pallas_call(
        body, name=name,
        grid_spec=pltpu.PrefetchScalarGridSpec(
            num_scalar_prefetch=1, grid=(r // tr,),
            in_specs=[pl.BlockSpec((n_slots, tr, c), lambda i, me_ref: (0, i, 0)),
                      pl.BlockSpec((1, tr, c), lambda i, me_ref: (me_ref[0], i, 0)), row, row, row],
            out_specs=[row] * 4),
        out_shape=[o] * 4,
        compiler_params=_cparams(("parallel",)),
    )(me, parts, mine, w, m, v)


def _adam_small(g, w, m, v):
    def body(g_ref, w_ref, m_ref, v_ref, d_ref, nm_ref, nv_ref):
        d, nm, nv = _adam_math(g_ref[...], w_ref[...], m_ref[...], v_ref[...])
        d_ref[...] = d
        nm_ref[...] = nm
        nv_ref[...] = nv

    o = jax.ShapeDtypeStruct(g.shape, F32)
    return pl.pallas_call(body, name="adam_small", out_shape=[o] * 3, compiler_params=_cparams())(g, w, m, v)


def _split_w_in(w_full):
    o = IN_OFF
    w_gdn = w_full[:, o["gq"]:o["ga"]]
    w_fox = w_full[:, o["fq"]:o["ff"]]
    w_gate = w_full[:, o["gate_a"]:o["end"]]
    w_small = jnp.concatenate([w_full[:, o["ga"]:o["fq"]], w_full[:, o["ff"]:o["gate_a"]],
                               jnp.zeros((w_full.shape[0], LANES - 24), w_full.dtype)], axis=1)
    return w_gdn, w_fox, w_gate, w_small


def _w_in_pieces(g_in):
    nd, d, c = g_in.shape
    tr = 128
    widths = (IN_OFF["ga"] - IN_OFF["gq"], IN_OFF["ff"] - IN_OFF["fq"], IN_OFF["end"] - IN_OFF["gate_a"], LANES)

    def body(in_ref, gdn_ref, fox_ref, gate_ref, small_ref):
        full = jnp.concatenate([in_ref[dv] for dv in range(nd)], axis=1)
        for ref, piece in zip((gdn_ref, fox_ref, gate_ref, small_ref), _split_w_in(full)):
            ref[...] = piece

    return pl.pallas_call(
        body, name="w_in_pieces", grid=(d // tr,),
        in_specs=[pl.BlockSpec((nd, tr, c), lambda i: (0, i, 0))],
        out_specs=[pl.BlockSpec((tr, wd), lambda i: (i, 0)) for wd in widths],
        out_shape=[jax.ShapeDtypeStruct((d, wd), g_in.dtype) for wd in widths],
        compiler_params=_cparams(("parallel",)),
    )(g_in)


def _w_in_shards(parts, c):
    d = parts[0].shape[0]
    tr = 128

    def body(*refs):
        full = _join_w_in([r[...] for r in refs[:-1]])
        for dv in range(N_DEV):
            refs[-1][dv] = full[:, dv * c:(dv + 1) * c]

    return pl.pallas_call(
        body, name="w_in_shards", grid=(d // tr,),
        in_specs=[pl.BlockSpec((tr, p.shape[1]), lambda i: (i, 0)) for p in parts],
        out_specs=pl.BlockSpec((N_DEV, tr, c), lambda i: (0, i, 0)),
        out_shape=jax.ShapeDtypeStruct((N_DEV, d, c), parts[0].dtype),
        compiler_params=_cparams(("parallel",)),
    )(*parts)


def _join_w_in(parts):
    small = parts[0]
    return jnp.concatenate(parts[1:5] + [small[:, 0:16]] + parts[5:8] + [small[:, 16:24]] + parts[8:10], axis=1)


def _rows128(a, rows):
    flat = a.reshape(-1)
    flat = jnp.concatenate([flat, jnp.zeros((rows * LANES - flat.shape[0],), flat.dtype)])
    return flat.reshape(rows, LANES)


def kernel(x, norm_mix_g, w_in, gdn_conv_w, gdn_a_log, gdn_dt_bias, gdn_norm_g, fox_q_norm_g, fox_k_norm_g, fox_f_bias, w_proj_gdn, w_proj_fox, w_out, norm_mlp_g, w_up, w_down, loss_target, m_norm_mix_g, m_w_in, m_gdn_conv_w, m_gdn_a_log, m_gdn_dt_bias, m_gdn_norm_g, m_fox_q_norm_g, m_fox_k_norm_g, m_fox_f_bias, m_w_proj_gdn, m_w_proj_fox, m_w_out, m_norm_mlp_g, m_w_up, m_w_down, v_norm_mix_g, v_w_in, v_gdn_conv_w, v_gdn_a_log, v_gdn_dt_bias, v_gdn_norm_g, v_fox_q_norm_g, v_fox_k_norm_g, v_fox_f_bias, v_w_proj_gdn, v_w_proj_fox, v_w_out, v_norm_mlp_g, v_w_up, v_w_down):
    b_loc, t, d = x.shape
    n = b_loc * t
    me = 4 * lax.axis_index("x") + 2 * lax.axis_index("y") + lax.axis_index("c")

    late_names = ["w_proj_gdn", "w_proj_fox", "w_out", "w_up", "w_down"]
    big16 = _to_bf16([w_in[0], w_proj_gdn[0], w_proj_fox[0], w_out[0], w_up[0], w_down[0]], "weights_to_bf16")
    g_in, g_conv = _all_gather([big16[0], gdn_conv_w[0]], "gather_w_in")
    behind = (g_conv[0:1, 0, 0:1] * 0.0).astype(BF16)
    late = _push_start([big16[1] + behind] + list(big16[2:]), False, "gather_late_start")
    w_gdn, w_fox, w_gate, w_small = _w_in_pieces(g_in)
    weights = {
        "w_gdn": w_gdn, "w_fox": w_fox, "w_gate": w_gate, "w_small": w_small,
        "conv_w": g_conv.transpose(1, 0, 2).reshape(CONV_K, 3 * d),
        "norm_mix_g": norm_mix_g + late["token"][0:1, 0:1], "norm_mlp_g": norm_mlp_g, "a_log": gdn_a_log,
        "dt_bias": gdn_dt_bias, "gdn_norm_g": gdn_norm_g, "fox_q_norm_g": fox_q_norm_g, "fox_k_norm_g": fox_k_norm_g,
        "f_bias": fox_f_bias,
    }
    c_in, c_up = w_in.shape[2], w_up.shape[2]
    me1 = jnp.reshape(me, (1,)).astype(jnp.int32)
    chip1 = jnp.reshape(2 * lax.axis_index("x") + lax.axis_index("y"), (1,)).astype(jnp.int32)
    core1 = jnp.reshape(lax.axis_index("c"), (1,)).astype(jnp.int32)

    class _Exchange:
        def __init__(self):
            self.started = []

        def late_weights(self, after):
            shards, lands = _push_wait(late, after, "gather_late_wait")
            full = [lax.dynamic_update_index_in_dim(land, shard, me, 0) for land, shard in zip(lands, shards)]
            g_pa, g_pb, g_out, g_up, g_down = full
            return {"w_proj_gdn": g_pa.reshape(d, d), "w_proj_fox": g_pb.reshape(d, d), "w_out": g_out.reshape(d, d),
                    "w_up": g_up.transpose(1, 0, 2).reshape(d, D_FF), "w_down": g_down.reshape(D_FF, d)}

        def grads_ready(self, grads, tie):
            names = list(grads)
            if names == ["w_in_parts"]:
                halves = _w_in_shards(grads["w_in_parts"], c_in)
                other = _sibling_swap(halves, "grads_w_in_sibling")
                pair = _add_halves(core1, halves, other, "grads_w_in_pair")
                st = _push_start([pair], True, "grads_start_w_in_parts", chips=True)
            else:
                layout = {"w_up": lambda a: a.reshape(d, N_DEV, c_up).transpose(1, 0, 2),
                          "w_down": lambda a: a.reshape(N_DEV, D_FF // N_DEV, d)}
                arrs = [layout.get(k, lambda a: a.reshape(N_DEV, d // N_DEV, d))(grads[k]) for k in names]
                st = _push_start(arrs, True, "grads_start_" + names[0])
            self.started.append((names, st))
            return tie + st["token"][0:1, 0:1]

    comm = _Exchange()
    loss_blk, grad_x, g = _local_step(x.reshape(n, d), loss_target.reshape(n, d), weights, b_loc, t, comm)

    shards = {"w_in_parts": (w_in, m_w_in, v_w_in), "w_proj_gdn": (w_proj_gdn, m_w_proj_gdn, v_w_proj_gdn),
              "w_proj_fox": (w_proj_fox, m_w_proj_fox, v_w_proj_fox), "w_out": (w_out, m_w_out, v_w_out),
              "w_up": (w_up, m_w_up, v_w_up), "w_down": (w_down, m_w_down, v_w_down)}
    adam = {}

    def finish(names, st, after):
        mine, parts = _push_wait(st, after, "grads_wait_" + names[0])
        slot = chip1 if st["copies"] == len(CHIP_RELS) else me1
        for k, own, part in zip(names, mine, parts):
            wi, mi, vi = shards[k]
            adam[k] = [r[None] for r in _adam_shard(slot, part, own, wi[0], mi[0], vi[0], "adam_" + k)]

    for names, st in comm.started[:-1]:
        finish(names, st, grad_x)

    conv_rows = CONV_K * 3 * d // LANES
    conv_g = g["conv"].transpose(1, 0, 2).reshape(conv_rows, LANES)
    buf = jnp.concatenate([conv_g, g["norm_mix_g"].reshape(8, LANES), g["norm_mlp_g"].reshape(8, LANES),
                           g["gdn_small"], g["fox_small"], loss_blk], axis=0)
    anchor = sum(adam[k][1][0, 0:1, 0:LANES] for names, _ in comm.started[:-1] for k in names) * 0.0
    tot = _all_reduce_small(buf + anchor, "all_reduce_small")
    finish(*comm.started[-1], tot)
    big_out = [adam[k] for k in ["w_in_parts"] + late_names]
    o = conv_rows
    conv_full = tot[0:o].reshape(CONV_K, 3 * d)
    c_conv = gdn_conv_w.shape[2]
    g_conv_shard = lax.dynamic_slice(conv_full, (0, me * c_conv), (CONV_K, c_conv))
    g_mix = tot[o:o + 8].reshape(1, d)
    g_mlp = tot[o + 8:o + 16].reshape(1, d)
    gs, fs = tot[o + 16:o + 24], tot[o + 24:o + 32]
    loss = tot[o + 32, 0]
    small_g = [g_mix, g_conv_shard[None], gs[0:1, 0:HEADS], gs[1:2, 0:HEADS], gs[2:3], fs[0:1], fs[1:2], fs[2:3, 0:HEADS],
               g_mlp]
    small_w = [norm_mix_g, gdn_conv_w, gdn_a_log, gdn_dt_bias, gdn_norm_g, fox_q_norm_g, fox_k_norm_g, fox_f_bias,
               norm_mlp_g]
    small_m = [m_norm_mix_g, m_gdn_conv_w, m_gdn_a_log, m_gdn_dt_bias, m_gdn_norm_g, m_fox_q_norm_g, m_fox_k_norm_g,
               m_fox_f_bias, m_norm_mlp_g]
    small_v = [v_norm_mix_g, v_gdn_conv_w, v_gdn_a_log, v_gdn_dt_bias, v_gdn_norm_g, v_fox_q_norm_g, v_fox_k_norm_g,
               v_fox_f_bias, v_norm_mlp_g]
    row_counts = [-(-a.size // (8 * LANES)) * 8 for a in small_w]

    def pack(arrs):
        return jnp.concatenate([_rows128(a, rc) for a, rc in zip(arrs, row_counts)], axis=0)

    sd, sm, sv = _adam_small(pack(small_g), pack(small_w), pack(small_m), pack(small_v))

    def unpack(p):
        outs, r0 = [], 0
        for a, rc in zip(small_w, row_counts):
            outs.append(p[r0:r0 + rc].reshape(-1)[:a.size].reshape(a.shape))
            r0 += rc
        return outs

    small_out = [small_g_i.reshape(w_i.shape) for small_g_i, w_i in zip(small_g, small_w)], unpack(sd), unpack(sm), unpack(sv)

    def ordered(kind):
        s = small_out[kind]
        bo = [b[kind] for b in big_out]
        return [s[0], bo[0], s[1], s[2], s[3], s[4], s[5], s[6], s[7], bo[1], bo[2], bo[3], s[8], bo[4], bo[5]]

    return (loss, grad_x.reshape(b_loc, t, d), *ordered(0), *ordered(1), *ordered(2), *ordered(3))
```

```python
import functools

import jax
import jax.numpy as jnp
from jax import lax
from jax.experimental import pallas as pl
from jax.experimental.pallas import tpu as pltpu

F32 = jnp.float32
BF16 = jnp.bfloat16
MESH = pl.DeviceIdType.MESH

N_DEV = 8
D_MODEL = 1024
HEADS = 8
DH = 128
CONV_K = 4
CHUNK = 128
GDN_GROUP = 16
FOX_BLOCK = 128
FOX_TILE = 512
FOX_SHORT = 512
D_FF = 4 * D_MODEL
EPS = 1e-6
LANES = 128
NEG = -1e30
IN_OFF = {"gq": 0, "gk": 1024, "gv": 2048, "gz": 3072, "ga": 4096, "gb": 4104, "fq": 4112, "fk": 5136,
          "fv": 6160, "ff": 7184, "gate_a": 7192, "gate_b": 8216, "end": 9240}
LANE_GA, LANE_GB, LANE_FF = 0, 8, 16

ADAM_LR = 0.001
ADAM_B1 = 0.9
ADAM_B2 = 0.999
ADAM_EPS = 1e-08
ADAM_WD = 0.01
ADAM_STEP = 10

VMEM_LIMIT = 56 * 1024 * 1024


def _cparams(sem=None):
    return pltpu.CompilerParams(dimension_semantics=sem, vmem_limit_bytes=VMEM_LIMIT)


def _sigmoid(x):
    return 1.0 / (1.0 + jnp.exp(-x))


def _softplus(x):
    return jnp.maximum(x, 0.0) + jnp.log(1.0 + jnp.exp(-jnp.abs(x)))


def _dot(a, b, prec=None):
    return lax.dot_general(a, b, (((1,), (0,)), ((), ())), precision=prec, preferred_element_type=F32)


def _dot_nt(a, b, prec=None):
    return lax.dot_general(a, b, (((1,), (1,)), ((), ())), precision=prec, preferred_element_type=F32)


def _dot_tn(a, b, prec=None):
    return lax.dot_general(a, b, (((0,), (0,)), ((), ())), precision=prec, preferred_element_type=F32)


def _bf(x):
    return x.astype(BF16)


MM_TILE = 1024


def _mm(a, b, *, name, ta=False, tb=False, out_dtype=F32, epi=None, extras=(), out2=None,
        b_koff=0, tm=MM_TILE, tn=MM_TILE, tk=MM_TILE):
    m = a.shape[1] if ta else a.shape[0]
    kdim = a.shape[0] if ta else a.shape[1]
    n = b.shape[0] if tb else b.shape[1]
    if ta and tk == MM_TILE:
        tk = 2 * MM_TILE
    tm, tn, tk = min(tm, m), min(tn, n), min(tk, kdim)
    nk = kdim // tk
    grid = (m // tm, n // tn, nk)
    koff = b_koff // tk
    a_spec = pl.BlockSpec((tk, tm), lambda i, j, k: (k, i)) if ta else pl.BlockSpec((tm, tk), lambda i, j, k: (i, k))
    if tb:
        b_spec = pl.BlockSpec((tn, tk), lambda i, j, k: (j, k + koff))
    else:
        b_spec = pl.BlockSpec((tk, tn), lambda i, j, k: (k + koff, j))
    o_spec = pl.BlockSpec((tm, tn), lambda i, j, k: (i, j))
    n_e = len(extras)
    n_o = 1 if out2 is None else 2
    dims = (((0 if ta else 1,), (1 if tb else 0,)), ((), ()))

    def body(a_ref, b_ref, *rest):
        e_refs, o_refs = rest[:n_e], rest[n_e:n_e + n_o]
        prod = lax.dot_general(_bf(a_ref[...]), _bf(b_ref[...]), dims, preferred_element_type=F32)

        def finish(r):
            if out2 is not None:
                o_refs[1][...] = out2[0](r).astype(out2[1])
            if epi is not None:
                r = epi(r, *[e[...] for e in e_refs])
            o_refs[0][...] = r.astype(out_dtype)

        if nk == 1:
            finish(prod)
        else:
            acc = rest[n_e + n_o]
            k = pl.program_id(2)

            @pl.when(k == 0)
            def _():
                acc[...] = prod

            @pl.when(k > 0)
            def _():
                acc[...] += prod

            @pl.when(k == nk - 1)
            def _():
                finish(acc[...])

    shapes = [jax.ShapeDtypeStruct((m, n), out_dtype)]
    if out2 is not None:
        shapes.append(jax.ShapeDtypeStruct((m, n), out2[1]))
    res = pl.pallas_call(
        body, name=name, grid=grid,
        in_specs=[a_spec, b_spec] + [o_spec] * n_e,
        out_specs=[o_spec] * n_o, out_shape=shapes,
        scratch_shapes=[] if nk == 1 else [pltpu.VMEM((tm, tn), F32)],
        compiler_params=_cparams(("parallel", "parallel", "arbitrary")),
    )(a, b, *extras)
    return res[0] if out2 is None else res


def _du_all(dps, w_small, segs, w, tm=512):
    n, d = dps.shape[0], w_small.shape[0]
    names = []
    for _, wname, _ in segs:
        if wname not in names:
            names.append(wname)
    first = {nm: min(i for i, s in enumerate(segs) if s[1] == nm) for nm in names}
    count = {nm: sum(1 for s in segs if s[1] == nm) for nm in names}
    n_seg, n_i = len(segs), n // tm

    def w_spec(nm):
        return pl.BlockSpec((d, d), lambda k, i: (0, jnp.clip(k - first[nm], 0, count[nm] - 1)))

    def rows_spec(cols, j):
        return pl.BlockSpec((tm, cols), lambda k, i: (jnp.where(k == j, i, jnp.where(k < j, 0, n_i - 1)), 0))

    def body(dps_ref, ws_ref, *rest):
        seg_refs, w_refs, o_ref, acc = rest[:n_seg], rest[n_seg:n_seg + len(names)], rest[-2], rest[-1]
        k, i = pl.program_id(0), pl.program_id(1)
        rows = pl.ds(pl.multiple_of(i * tm, tm), tm)

        @pl.when(k == 0)
        def _():
            acc[rows, :] = _dot_nt(_bf(dps_ref[...]), ws_ref[...])

        for idx, (_, wname, _) in enumerate(segs):
            @pl.when(k == idx)
            def _(idx=idx, wname=wname):
                acc[rows, :] += _dot_nt(seg_refs[idx][...], w_refs[names.index(wname)][...])

        @pl.when(k == n_seg - 1)
        def _():
            o_ref[...] = acc[rows, :]

    return pl.pallas_call(
        body, name="du_all", grid=(n_seg, n_i),
        in_specs=[rows_spec(dps.shape[1], 0), pl.BlockSpec(w_small.shape, lambda k, i: (0, 0))]
                 + [rows_spec(d, j) for j in range(n_seg)] + [w_spec(nm) for nm in names],
        out_specs=pl.BlockSpec((tm, d), lambda k, i: (jnp.where(k == n_seg - 1, i, 0), 0)),
        out_shape=jax.ShapeDtypeStruct((n, d), F32),
        scratch_shapes=[pltpu.VMEM((n, d), F32)],
        compiler_params=_cparams(("arbitrary", "arbitrary")),
    )(dps, w_small, *[s[0] for s in segs], *[w[nm] for nm in names])


def _relu2(x):
    r = jnp.maximum(x, 0.0)
    return r * r


ROWS = 512


def _rms_fwd(x, g, name):
    n, d = x.shape

    def body(x_ref, g_ref, u_ref):
        xv = x_ref[...]
        r = lax.rsqrt(jnp.mean(xv * xv, axis=1, keepdims=True) + EPS)
        u_ref[...] = _bf(xv * r * g_ref[...])

    return pl.pallas_call(
        body, name=name, grid=(n // ROWS,),
        in_specs=[pl.BlockSpec((ROWS, d), lambda i: (i, 0)), pl.BlockSpec((1, d), lambda i: (0, 0))],
        out_specs=pl.BlockSpec((ROWS, d), lambda i: (i, 0)),
        out_shape=jax.ShapeDtypeStruct((n, d), BF16),
        compiler_params=_cparams(("parallel",)),
    )(x, g)


def _rms_bwd(dy, x, g, dres, name):
    n, d = x.shape

    def body(dy_ref, x_ref, g_ref, dres_ref, dx_ref, dx16_ref, dg_ref):
        i = pl.program_id(0)
        xv, dyv = x_ref[...], dy_ref[...]
        r = lax.rsqrt(jnp.mean(xv * xv, axis=1, keepdims=True) + EPS)
        gy = dyv * g_ref[...]
        s = jnp.sum(gy * xv, axis=1, keepdims=True)
        dx = dres_ref[...] + r * gy - xv * (r * r * r * (1.0 / d)) * s
        dx_ref[...] = dx
        dx16_ref[...] = _bf(dx)

        @pl.when(i == 0)
        def _():
            dg_ref[...] = jnp.zeros_like(dg_ref)

        dg_ref[...] += jnp.sum(dyv * xv * r, axis=0, keepdims=True)

    row = pl.BlockSpec((ROWS, d), lambda i: (i, 0))
    vec = pl.BlockSpec((1, d), lambda i: (0, 0))
    return pl.pallas_call(
        body, name=name, grid=(n // ROWS,),
        in_specs=[row, row, vec, row], out_specs=[row, row, vec],
        out_shape=[jax.ShapeDtypeStruct((n, d), F32), jax.ShapeDtypeStruct((n, d), BF16),
                   jax.ShapeDtypeStruct((1, d), F32)],
        compiler_params=_cparams(("arbitrary",)),
    )(dy, x, g, dres)


def _merge_fwd(ya, yb, gate):
    n, d = ya.shape

    def body(ya_ref, yb_ref, ga_ref, gb_ref, o_ref):
        o_ref[...] = _bf(_sigmoid(ga_ref[...]) * ya_ref[...] + _sigmoid(gb_ref[...]) * yb_ref[...])

    row = pl.BlockSpec((ROWS, d), lambda i: (i, 0))
    return pl.pallas_call(
        body, name="merge_fwd", grid=(n // ROWS,),
        in_specs=[row, row, row, pl.BlockSpec((ROWS, d), lambda i: (i, 1))], out_specs=row,
        out_shape=jax.ShapeDtypeStruct((n, d), BF16),
        compiler_params=_cparams(("parallel",)),
    )(ya, yb, gate, gate)


def _merge_bwd(dm, ya, yb, gate):
    n, d = ya.shape

    def body(dm_ref, ya_ref, yb_ref, ga_ref, gb_ref, dya_ref, dyb_ref, dga_ref, dgb_ref):
        dmv = dm_ref[...]
        sa, sb = _sigmoid(ga_ref[...]), _sigmoid(gb_ref[...])
        dya_ref[...] = _bf(dmv * sa)
        dyb_ref[...] = _bf(dmv * sb)
        dga_ref[...] = _bf(dmv * ya_ref[...] * sa * (1.0 - sa))
        dgb_ref[...] = _bf(dmv * yb_ref[...] * sb * (1.0 - sb))

    row = pl.BlockSpec((ROWS, d), lambda i: (i, 0))
    o = jax.ShapeDtypeStruct((n, d), BF16)
    return pl.pallas_call(
        body, name="merge_bwd", grid=(n // ROWS,),
        in_specs=[row, row, row, row, pl.BlockSpec((ROWS, d), lambda i: (i, 1))], out_specs=[row] * 4,
        out_shape=[o] * 4,
        compiler_params=_cparams(("parallel",)),
    )(dm, ya, yb, gate, gate)


def _loss_bwd(out, target):
    n, d = out.shape

    def body(o_ref, t_ref, d_ref, d16_ref, l_ref):
        i = pl.program_id(0)
        err = o_ref[...] - t_ref[...]
        d_ref[...] = err * (1.0 / d)
        d16_ref[...] = _bf(err * (1.0 / d))

        @pl.when(i == 0)
        def _():
            l_ref[...] = jnp.zeros_like(l_ref)

        l_ref[...] += 0.5 * jnp.sum(jnp.mean(err * err, axis=1, keepdims=True), axis=0, keepdims=True)

    row = pl.BlockSpec((ROWS, d), lambda i: (i, 0))
    return pl.pallas_call(
        body, name="loss_bwd", grid=(n // ROWS,),
        in_specs=[row, row], out_specs=[row, row, pl.BlockSpec((8, LANES), lambda i: (0, 0))],
        out_shape=[jax.ShapeDtypeStruct((n, d), F32), jax.ShapeDtypeStruct((n, d), BF16),
                   jax.ShapeDtypeStruct((8, LANES), F32)],
        compiler_params=_cparams(("arbitrary",)),
    )(out, target)


PAD = 8


def _pad_zero(pad_ref):
    t = pad_ref.shape[0] - 2 * PAD
    pad_ref[0:PAD, :] = jnp.zeros((PAD, LANES), F32)
    pad_ref[PAD + t:2 * PAD + t, :] = jnp.zeros((PAD, LANES), F32)


def _shifted(pad_ref, s):
    t = pad_ref.shape[0] - 2 * PAD
    return pad_ref[PAD - s:PAD - s + t, :]


def _conv(x, w_ref, pad_ref):
    t = x.shape[0]
    pad_ref[PAD:PAD + t, :] = x
    y = _shifted(pad_ref, 3) * w_ref[0:1, :]
    y = y + _shifted(pad_ref, 2) * w_ref[1:2, :]
    y = y + _shifted(pad_ref, 1) * w_ref[2:3, :]
    return y + x * w_ref[3:4, :]


def _chunk_consts():
    r = lax.broadcasted_iota(jnp.int32, (CHUNK, CHUNK), 0)
    c = lax.broadcasted_iota(jnp.int32, (CHUNK, CHUNK), 1)
    incl, strict = r >= c, r > c
    return dict(incl=incl, strict=strict, trilf=incl.astype(F32), triuf=(r <= c).astype(F32),
                eye=(r == c).astype(F32))


class _V:
    def __init__(self, xs):
        self.xs = list(xs)

    def __add__(self, o):
        return _ap(lambda x, y: x + y, self, o)

    def __radd__(self, o):
        return _ap(lambda x, y: y + x, self, o)

    def __sub__(self, o):
        return _ap(lambda x, y: x - y, self, o)

    def __rsub__(self, o):
        return _ap(lambda x, y: y - x, self, o)

    def __mul__(self, o):
        return _ap(lambda x, y: x * y, self, o)

    def __rmul__(self, o):
        return _ap(lambda x, y: y * x, self, o)

    def __neg__(self):
        return _ap(lambda x: -x, self)

    def __getitem__(self, idx):
        return _ap(lambda x: x[idx], self)


def _ap(fn, *args):
    n = [len(a.xs) for a in args if isinstance(a, _V)]
    if not n:
        return fn(*args)
    return _V([fn(*[a.xs[i] if isinstance(a, _V) else a for a in args]) for i in range(n[0])])


def _vbf(x):
    return _ap(_bf, x)


def _vdot(a, b):
    return _ap(_dot, a, b)


def _vdot_nt(a, b):
    return _ap(_dot_nt, a, b)


def _vdot_tn(a, b):
    return _ap(_dot_tn, a, b)


def _vexp(x):
    return _ap(jnp.exp, x)


def _vsum(x, axis):
    return _ap(lambda v: jnp.sum(v, axis=axis, keepdims=True), x)


def _vcat(a, b, axis):
    return _ap(lambda x, y: jnp.concatenate([x, y], axis=axis), a, b)


def _vmask(mask, x):
    return _ap(lambda v: jnp.where(mask, v, 0.0), x)


def _split2(x):
    h = _vbf(x)
    return h, _vbf(x - _ap(lambda v: v.astype(F32), h))


def _dot3(a, b, kind=_vdot):
    ah, al = _split2(a)
    bh, bl = _split2(b)
    return kind(ah, bh) + (kind(ah, bl) + kind(al, bh))


def _split(x, terms):
    out = []
    for _ in range(terms):
        h = _vbf(x)
        out.append(h)
        x = x - _ap(lambda v: v.astype(F32), h)
    return out


def _dot_exact_l(m01, x, kind=_vdot, terms=2):
    mb = _bf(m01)
    parts = [kind(mb, xp) for xp in _split(x, terms)]
    return functools.reduce(lambda a, b: a + b, reversed(parts))


def _dot_exact_r(x, m01, kind=_vdot, terms=2):
    mb = _bf(m01)
    parts = [kind(xp, mb) for xp in _split(x, terms)]
    return functools.reduce(lambda a, b: a + b, reversed(parts))


def _inv_series(a, eye):
    m = eye.shape[0]
    levels = m.bit_length() - 1
    p = -a
    r = p + eye
    p = _dot3(p, p)
    for j in range(1, levels):
        if j < levels - 1:
            y = _dot3(p, _vcat(p, r, 1))
            p, r = y[:, 0:m], r + y[:, m:2 * m]
        else:
            r = r + _dot3(p, r)
    return r


def _inv_unit_lower(a, eye):
    return _inv_series(a, eye)


def _gdn_chunk_pre(q, k, v, g128, b128, cs):
    incl = cs["incl"]
    b64 = b128
    big_g = _dot_exact_l(cs["trilf"], g128)
    gc = big_g[:, 0:CHUNK]
    gr = _dot_exact_r(g128, cs["triuf"], _vdot_tn)
    decay = _ap(lambda d: jnp.where(incl, jnp.exp(jnp.where(incl, d, 0.0)), 0.0), gc - gr)
    kb, qb = _vbf(k), _vbf(q)
    qkk = _vdot_nt(_vcat(qb, kb, 0), kb)
    qk, kk = qkk[0:CHUNK], qkk[CHUNK:2 * CHUNK]
    tm = _inv_unit_lower(_vmask(cs["strict"], b64 * kk * decay), cs["eye"])
    e_g = _vexp(big_g)
    wu = _dot3(tm, _vcat(v * b128, k * (b128 * e_g), 1))
    w, u = wu[:, 0:DH], wu[:, DH:2 * DH]
    g_last = _vsum(g128, 0)
    return dict(big_g=big_g, decay=decay, kk=kk, qk=qk, tm=tm, w=w, u=u, p=qk * decay, q_dec=q * e_g,
                k_dec=k * _vexp(g_last - big_g), dec=_vexp(g_last))


def _gdn_chunk_post(q, k, v, g128, b128, s, ds_next, do, dv_new, big_g, decay, kk, qk, tm, u, v_new, cs):
    b64 = b128
    e_g = _vexp(big_g)
    vb = v * b128
    kbeta = k * (b128 * e_g)
    q_dec = q * e_g
    g_last = _vsum(g128, 0)
    ekg = _vexp(g_last - big_g)
    k_dec = k * ekg
    dec = _vexp(g_last)
    kb, qb, sb = _vbf(k), _vbf(q), _vbf(s)
    dob, dsb, vnb, dvnb = _vbf(do), _vbf(ds_next), _vbf(v_new), _vbf(dv_new)
    dp = _vmask(cs["incl"], _vdot_nt(dob, vnb))
    dq_dec = _vdot_nt(dob, sb)
    du = -_vdot_nt(dvnb, sb)
    ddec = _vsum(_vsum(s * ds_next, 1), 0)
    dk_dec = _vdot_nt(vnb, dsb)
    dwu = _vcat(dv_new, du, 1)
    dt = _dot3(dwu, _vcat(vb, kbeta, 1), _vdot_nt)
    dvk = _dot3(tm, dwu, _vdot_tn)
    dvb, dkbeta = dvk[:, 0:DH], dvk[:, DH:2 * DH]
    da = _vmask(cs["strict"], -_dot3(tm, _dot3(dt, tm, _vdot_nt), _vdot_tn))
    dkk = _vbf(da * b64 * decay)
    dqk = _vbf(dp * decay)
    ddd = (da * b64 * kk + dp * qk) * decay
    dq = _vdot(dqk, kb) + dq_dec * e_g
    dk = _vdot_tn(dqk, qb) + _vdot(dkk, kb) + _vdot_tn(dkk, kb) + dk_dec * ekg + dkbeta * (b128 * e_g)
    dv = dvb * b128
    dbeta = _vsum(da * kk * decay, 1) + _vsum(dvb * v, 1) + _vsum(dkbeta * k * e_g, 1)
    s_k = _vsum(dk_dec * k_dec, 1)
    dg_col = _vsum(ddd, 1) + _vsum(dq_dec * q_dec, 1) - s_k + _vsum(dkbeta * kbeta, 1)
    colsum = _dot_exact_r(ddd, jnp.ones((CHUNK, LANES), F32), _vdot_tn)
    dg_last = _vsum(s_k, 0) + ddec * dec
    dg = _dot_exact_l(cs["triuf"], dg_col - colsum) + dg_last
    return dq, dk, dv, dg, dbeta


def _stack_rows(vecs, nrows):
    row = lax.broadcasted_iota(jnp.int32, (nrows, LANES), 0)
    out = jnp.zeros((nrows, LANES), F32)
    for i, v in enumerate(vecs):
        out = out + jnp.where(row == i, jnp.broadcast_to(v, (nrows, LANES)), 0.0)
    return out


def _head_lane(x, lane_idx):
    lane = lax.broadcasted_iota(jnp.int32, x.shape, 1)
    return jnp.sum(jnp.where(lane == lane_idx, x, 0.0), axis=1, keepdims=True)


def _gdn_gates(ps, h, alog_ref, dtb_ref):
    ga = _head_lane(ps, LANE_GA + h)
    gb = _head_lane(ps, LANE_GB + h)
    a = jnp.exp(jnp.full((1, 1), alog_ref[0, h], F32))
    sp_in = ga + dtb_ref[0, h]
    g = -a * _softplus(sp_in)
    return g, _sigmoid(gb), a, sp_in


def _gdn_specs(b_loc, t):
    def col(off):
        return pl.BlockSpec((t, DH), lambda b, h: (b, off + h))

    ps_spec = pl.BlockSpec((t, LANES), lambda b, h: (b, 0))

    def wcol(off):
        return pl.BlockSpec((CONV_K, DH), lambda b, h: (0, off + h))

    smem = pl.BlockSpec(memory_space=pltpu.SMEM)
    vec = pl.BlockSpec((1, DH), lambda b, h: (0, 0))
    return col, ps_spec, wcol, smem, vec


def _gdn_fwd(pg, ps, convw, a_log, dt_bias, gnorm, b_loc, t):
    n = b_loc * t
    assert t % (CHUNK * GDN_GROUP) == 0 and CHUNK == LANES, (t, CHUNK, GDN_GROUP)
    nc = t // CHUNK
    _, _, _, smem, vec = _gdn_specs(b_loc, t)

    def body(q_ref, k_ref, v_ref, z_ref, ps_ref, wq_ref, wk_ref, wv_ref, alog_ref, dtb_ref, gn_ref,
             oa_ref, oraw_ref, s_ref, qn, kn, vv, g128, b128, uq_s, p_s, kd_s, dec_s, pad_s):
        h = pl.program_id(0)
        _pad_zero(pad_s)
        for b in range(b_loc):
            rb = slice(b * t, (b + 1) * t)
            g, beta, _, _ = _gdn_gates(ps_ref[rb, :], h, alog_ref, dtb_ref)
            g128[rb, :] = jnp.broadcast_to(g, (t, LANES))
            b128[rb, :] = jnp.broadcast_to(beta, (t, LANES))
            pq = _conv(q_ref[rb, :], wq_ref, pad_s)
            yq = pq * _sigmoid(pq)
            qn[rb, :] = yq * (lax.rsqrt(jnp.sum(yq * yq, axis=1, keepdims=True) + EPS) * (DH ** -0.5))
            pk = _conv(k_ref[rb, :], wk_ref, pad_s)
            yk = pk * _sigmoid(pk)
            kn[rb, :] = yk * lax.rsqrt(jnp.sum(yk * yk, axis=1, keepdims=True) + EPS)
            pv = _conv(v_ref[rb, :], wv_ref, pad_s)
            vv[rb, :] = pv * _sigmoid(pv)
        cs = _chunk_consts()

        def pre_group(gi, _):
            idx = [gi * GDN_GROUP + c for c in range(GDN_GROUP)]
            rows = [pl.ds(pl.multiple_of(i * CHUNK, CHUNK), CHUNK) for i in idx]
            ins = [_V([ref[r, :] for r in rows]) for ref in (qn, kn, vv, g128, b128)]
            f = _gdn_chunk_pre(*ins, cs)
            for c, (i, r) in enumerate(zip(idx, rows)):
                vv[r, :] = f["w"].xs[c]
                uq_s[i, 0:CHUNK, :] = _bf(f["u"].xs[c])
                uq_s[i, CHUNK:2 * CHUNK, :] = _bf(f["q_dec"].xs[c])
                p_s[r, :] = _bf(f["p"].xs[c])
                kd_s[r, :] = _bf(f["k_dec"].xs[c])
                dec_s[pl.ds(pl.multiple_of(i * 8, 8), 8), :] = jnp.broadcast_to(f["dec"].xs[c], (8, LANES))
            return 0

        lax.fori_loop(0, b_loc * nc // GDN_GROUP, pre_group, 0)

        def chunk(i, states):
            idx = [b * nc + i for b in range(b_loc)]
            rows = [pl.ds(pl.multiple_of(j * CHUNK, CHUNK), CHUNK) for j in idx]
            s = _V(states)
            us = _vdot(_V([uq_s[j] for j in idx]), _vbf(s))
            vnb = _vbf(_V([vv[r, :] for r in rows]) - us[0:CHUNK])
            o = us[CHUNK:2 * CHUNK] + _vdot(_V([p_s[r, :] for r in rows]), vnb)
            dec = _V([dec_s[pl.ds(pl.multiple_of(j * 8, 8), 1), :] for j in idx])
            s_new = s * dec + _vdot_tn(_V([kd_s[r, :] for r in rows]), vnb)
            for b, r in enumerate(rows):
                oraw_ref[r, :] = o.xs[b]
                s_ref[b, 0, i] = states[b]
            return tuple(s_new.xs)

        lax.fori_loop(0, nc, chunk, tuple(jnp.zeros((DH, DH), F32) for _ in range(b_loc)))
        o = oraw_ref[...]
        rr = lax.rsqrt(jnp.mean(o * o, axis=1, keepdims=True) + EPS)
        z = z_ref[...]
        oa_ref[...] = _bf((o * rr * gn_ref[...]) * (z * _sigmoid(z)))

    def col(off):
        return pl.BlockSpec((n, DH), lambda h: (0, off + h))

    def wcol(off):
        return pl.BlockSpec((CONV_K, DH), lambda h: (0, off + h))

    vec1 = pl.BlockSpec((1, DH), lambda h: (0, 0))
    return pl.pallas_call(
        body, name="gdn_fwd", grid=(HEADS,),
        in_specs=[col(0), col(HEADS), col(2 * HEADS), col(3 * HEADS), pl.BlockSpec((n, LANES), lambda h: (0, 0)),
                  wcol(0), wcol(HEADS), wcol(2 * HEADS), smem, smem, vec1],
        out_specs=[pl.BlockSpec((n, DH), lambda h: (0, h)), pl.BlockSpec((n, DH), lambda h: (0, h)),
                   pl.BlockSpec((b_loc, 1, nc, DH, DH), lambda h: (0, h, 0, 0, 0))],
        out_shape=[jax.ShapeDtypeStruct((n, HEADS * DH), BF16), jax.ShapeDtypeStruct((n, HEADS * DH), F32),
                   jax.ShapeDtypeStruct((b_loc, HEADS, nc, DH, DH), F32)],
        scratch_shapes=([pltpu.VMEM((n, DH), F32)] * 3 + [pltpu.VMEM((n, LANES), F32)] * 2
                        + [pltpu.VMEM((b_loc * nc, 2 * CHUNK, DH), BF16), pltpu.VMEM((n, CHUNK), BF16),
                           pltpu.VMEM((n, DH), BF16), pltpu.VMEM((8 * b_loc * nc, LANES), F32),
                           pltpu.VMEM((t + 2 * PAD, LANES), F32)]),
        compiler_params=_cparams(("arbitrary",)),
    )(pg, pg, pg, pg, ps, convw, convw, convw, a_log, dt_bias, gnorm)


def _gdn_bwd(pg, ps, convw, a_log, dt_bias, gnorm, d_oa, o_raw, s_all, b_loc, t):
    n = b_loc * t
    assert t % (CHUNK * GDN_GROUP) == 0 and CHUNK == LANES, (t, CHUNK, GDN_GROUP)
    nc = t // CHUNK
    col, ps_spec, wcol, smem, vec = _gdn_specs(b_loc, t)

    def body(q_ref, k_ref, v_ref, z_ref, ps_ref, wq_ref, wk_ref, wv_ref, alog_ref, dtb_ref, gn_ref,
             doa_ref, oraw_ref, s_ref,
             dq_ref, dk_ref, dv_ref, dz_ref, dps_ref, dcw_ref, dsm_ref,
             qn, kn, vv, g128, b128, do_s, bg_s, u_s, vn_s, dvn_s, dcy_s, kk_s, qk_s, tm_s, dsn_s, pad_s):
        b, h = pl.program_id(0), pl.program_id(1)
        g, beta, _, _ = _gdn_gates(ps_ref[...], h, alog_ref, dtb_ref)
        g128[...] = jnp.broadcast_to(g, (t, LANES))
        b128[...] = jnp.broadcast_to(beta, (t, LANES))
        _pad_zero(pad_s)

        def prep(x_ref, w_ref):
            p = _conv(x_ref[...], w_ref, pad_s)
            sg = _sigmoid(p)
            return p, sg, p * sg

        _, _, yq = prep(q_ref, wq_ref)
        qn[...] = yq * (lax.rsqrt(jnp.sum(yq * yq, axis=1, keepdims=True) + EPS) * (DH ** -0.5))
        _, _, yk = prep(k_ref, wk_ref)
        kn[...] = yk * lax.rsqrt(jnp.sum(yk * yk, axis=1, keepdims=True) + EPS)
        _, _, yv = prep(v_ref, wv_ref)
        vv[...] = yv

        o = oraw_ref[...]
        z = z_ref[...]
        doa = doa_ref[...]
        gn = gn_ref[...]
        ro = lax.rsqrt(jnp.mean(o * o, axis=1, keepdims=True) + EPS)
        sz = _sigmoid(z)
        dz_ref[...] = _bf(doa * (o * ro * gn) * (sz * (1.0 + z * (1.0 - sz))))
        dn = doa * (z * sz)
        dgn = jnp.sum(dn * o * ro, axis=0, keepdims=True)
        gy = dn * gn
        do_s[...] = ro * gy - o * (ro * ro * ro * (1.0 / DH)) * jnp.sum(gy * o, axis=1, keepdims=True)

        cs = _chunk_consts()

        def pre_group(gi, _):
            idx = [gi * GDN_GROUP + c for c in range(GDN_GROUP)]
            rows = [pl.ds(pl.multiple_of(i * CHUNK, CHUNK), CHUNK) for i in idx]
            ins = [_V([ref[r, :] for r in rows]) for ref in (qn, kn, vv, g128, b128)]
            states = _V([_bf(s_ref[0, 0, i]) for i in idx])
            f = _gdn_chunk_pre(*ins, cs)
            v_new = f["w"] - _vdot(_vbf(f["u"]), states)
            for c, r in enumerate(rows):
                bg_s[r, :] = f["big_g"].xs[c]
                u_s[r, :] = f["u"].xs[c]
                vn_s[r, :] = v_new.xs[c]
                dcy_s[r, :] = f["decay"].xs[c]
                kk_s[r, :] = f["kk"].xs[c]
                qk_s[r, :] = f["qk"].xs[c]
                tm_s[r, :] = f["tm"].xs[c]
            return 0

        lax.fori_loop(0, nc // GDN_GROUP, pre_group, 0)

        def chunk(j, ds):
            i = nc - 1 - j
            r = pl.ds(pl.multiple_of(i * CHUNK, CHUNK), CHUNK)
            big_g = bg_s[r, :]
            g_last = jnp.sum(g128[r, :], axis=0, keepdims=True)
            dob = _bf(do_s[r, :])
            dv_new = (_dot_tn(_bf(qk_s[r, :] * dcy_s[r, :]), dob)
                      + _dot(_bf(kn[r, :] * jnp.exp(g_last - big_g)), _bf(ds)))
            dvn_s[r, :] = dv_new
            dsn_s[i] = ds
            return (_dot_tn(_bf(qn[r, :] * jnp.exp(big_g)), dob) + jnp.exp(g_last) * ds
                    - _dot_tn(_bf(u_s[r, :]), _bf(dv_new)))

        lax.fori_loop(0, nc, chunk, jnp.zeros((DH, DH), F32))

        def post_group(gi, _):
            idx = [gi * GDN_GROUP + c for c in range(GDN_GROUP)]
            rows = [pl.ds(pl.multiple_of(i * CHUNK, CHUNK), CHUNK) for i in idx]
            def rows_of(ref):
                return _V([ref[r, :] for r in rows])

            dq, dk, dv, dg, dbeta = _gdn_chunk_post(
                rows_of(qn), rows_of(kn), rows_of(vv), rows_of(g128), rows_of(b128),
                _V([s_ref[0, 0, i] for i in idx]), _V([dsn_s[i] for i in idx]), rows_of(do_s), rows_of(dvn_s),
                rows_of(bg_s), rows_of(dcy_s), rows_of(kk_s), rows_of(qk_s), rows_of(tm_s), rows_of(u_s), rows_of(vn_s),
                cs)
            for c, r in enumerate(rows):
                qn[r, :] = dq.xs[c]
                kn[r, :] = dk.xs[c]
                vv[r, :] = dv.xs[c]
                g128[r, :] = dg.xs[c]
                b128[r, :] = jnp.broadcast_to(dbeta.xs[c], (CHUNK, LANES))
            return 0

        lax.fori_loop(0, nc // GDN_GROUP, post_group, 0)
        dqh, dkh, dvh = qn, kn, vv

        g, beta, a, sp_in = _gdn_gates(ps_ref[...], h, alog_ref, dtb_ref)
        dg = g128[...]
        d_ga = dg * (-a) * _sigmoid(sp_in)
        d_alog = jnp.sum(dg * g, axis=0, keepdims=True)
        d_dtb = jnp.sum(d_ga, axis=0, keepdims=True)
        d_gb = b128[...] * (beta * (1.0 - beta))
        lane = lax.broadcasted_iota(jnp.int32, (t, LANES), 1)
        contrib = jnp.where(lane == LANE_GA + h, d_ga, 0.0) + jnp.where(lane == LANE_GB + h, d_gb, 0.0)

        @pl.when(h == 0)
        def _():
            dps_ref[...] = jnp.zeros_like(dps_ref)

        dps_ref[...] += contrib

        lane1 = lax.broadcasted_iota(jnp.int32, (1, LANES), 1)
        small = _stack_rows([jnp.where(lane1 == h, d_alog, 0.0), jnp.where(lane1 == h, d_dtb, 0.0), dgn], 8)

        @pl.when((b == 0) & (h == 0))
        def _():
            dsm_ref[...] = jnp.zeros_like(dsm_ref)
            dcw_ref[...] = jnp.zeros_like(dcw_ref)

        dsm_ref[...] += small

        def conv_bwd(dp, x, w_ref, slot):
            dw = _stack_rows([jnp.sum(dp * _shifted(pad_s, 3), axis=0, keepdims=True),
                              jnp.sum(dp * _shifted(pad_s, 2), axis=0, keepdims=True),
                              jnp.sum(dp * _shifted(pad_s, 1), axis=0, keepdims=True),
                              jnp.sum(dp * x, axis=0, keepdims=True)], CONV_K)
            dcw_ref[slot] += dw
            pad_s[PAD:PAD + t, :] = dp
            dx = _shifted(pad_s, -3) * w_ref[0:1, :]
            dx = dx + _shifted(pad_s, -2) * w_ref[1:2, :]
            dx = dx + _shifted(pad_s, -1) * w_ref[2:3, :]
            return dx + dp * w_ref[3:4, :]

        def l2_bwd(dqn, y, c):
            r = lax.rsqrt(jnp.sum(y * y, axis=1, keepdims=True) + EPS)
            s1 = jnp.sum(dqn * y, axis=1, keepdims=True)
            return c * r * dqn - (c * r * r * r) * s1 * y

        def silu_bwd(p, sg):
            return sg * (1.0 + p * (1.0 - sg))

        pq, sq, yq = prep(q_ref, wq_ref)
        dq_ref[...] = _bf(conv_bwd(l2_bwd(dqh[...], yq, DH ** -0.5) * silu_bwd(pq, sq), q_ref[...], wq_ref, h))
        pk, sk, yk = prep(k_ref, wk_ref)
        dk_ref[...] = _bf(conv_bwd(l2_bwd(dkh[...], yk, 1.0) * silu_bwd(pk, sk), k_ref[...], wk_ref, HEADS + h))
        pv, sv, _ = prep(v_ref, wv_ref)
        dv_ref[...] = _bf(conv_bwd(dvh[...] * silu_bwd(pv, sv), v_ref[...], wv_ref, 2 * HEADS + h))

    blk = pl.BlockSpec((t, DH), lambda b, h: (b, h))
    ob = jax.ShapeDtypeStruct((n, HEADS * DH), BF16)
    return pl.pallas_call(
        body, name="gdn_bwd", grid=(b_loc, HEADS),
        in_specs=[col(0), col(HEADS), col(2 * HEADS), col(3 * HEADS), ps_spec, wcol(0), wcol(HEADS), wcol(2 * HEADS),
                  smem, smem, vec, blk, blk, pl.BlockSpec((1, 1, nc, DH, DH), lambda b, h: (b, h, 0, 0, 0))],
        out_specs=[blk, blk, blk, blk, ps_spec,
                   pl.BlockSpec((3 * HEADS, CONV_K, DH), lambda b, h: (0, 0, 0)),
                   pl.BlockSpec((8, LANES), lambda b, h: (0, 0))],
        out_shape=[ob, ob, ob, ob, jax.ShapeDtypeStruct((n, LANES), F32),
                   jax.ShapeDtypeStruct((3 * HEADS, CONV_K, DH), F32), jax.ShapeDtypeStruct((8, LANES), F32)],
        scratch_shapes=([pltpu.VMEM((t, DH), F32)] * 3 + [pltpu.VMEM((t, LANES), F32)] * 2
                        + [pltpu.VMEM((t, DH), F32)] * 5 + [pltpu.VMEM((t, CHUNK), F32)] * 4
                        + [pltpu.VMEM((nc, DH, DH), F32), pltpu.VMEM((t + 2 * PAD, LANES), F32)]),
        compiler_params=_cparams(("arbitrary", "arbitrary")),
    )(pg, pg, pg, pg, ps, convw, convw, convw, a_log, dt_bias, gnorm, d_oa, o_raw, s_all)


def _fox_prologue(q_ref, k_ref, v_ref, ps_ref, fb_ref, gq_ref, gk_ref, h, t, qs, ks, vs, ccol, crow):
    nb = t // FOX_BLOCK
    q, k = q_ref[...], k_ref[...]
    rq = lax.rsqrt(jnp.mean(q * q, axis=1, keepdims=True) + EPS)
    rk = lax.rsqrt(jnp.mean(k * k, axis=1, keepdims=True) + EPS)
    qs[...] = _bf(q * rq * gq_ref[...])
    ks[...] = _bf(k * rk * gk_ref[...])
    vs[...] = _bf(v_ref[...])
    f_in = _head_lane(ps_ref[...], LANE_FF + h) + fb_ref[0, h]
    ccol[...] = jnp.broadcast_to(-_softplus(-f_in), (t, LANES))
    r = lax.broadcasted_iota(jnp.int32, (FOX_BLOCK, FOX_BLOCK), 0)
    c = lax.broadcasted_iota(jnp.int32, (FOX_BLOCK, FOX_BLOCK), 1)
    trilf, triuf = (r >= c).astype(F32), (r <= c).astype(F32)
    blocks = [pl.ds(j * FOX_BLOCK, FOX_BLOCK) for j in range(nb)]
    lfs = _V([ccol[rb, :] for rb in blocks])
    cc = _dot_exact_l(trilf, lfs, terms=3)
    cr = _dot_exact_r(lfs, triuf, _vdot_tn, terms=3)
    sums = _vsum(lfs, 0)
    carry = jnp.zeros((1, LANES), F32)
    for j, rb in enumerate(blocks):
        ccol[rb, :] = cc.xs[j] + carry
        crow[j] = (cr.xs[j] + carry)[0:8]
        carry = carry + sums.xs[j]
    return rq, rk, f_in


def _fox_scores(q_rows, k_rows, cc, cr, row0, col0, masked=True):
    s = _dot_nt(q_rows, k_rows) * (DH ** -0.5) + cc - cr
    if not masked:
        return s
    r = lax.broadcasted_iota(jnp.int32, s.shape, 0)
    c = lax.broadcasted_iota(jnp.int32, s.shape, 1)
    return jnp.where(row0 + r >= col0 + c, s, NEG)


def _fox_specs(t):
    def col(off):
        return pl.BlockSpec((t, DH), lambda b, h: (b, off + h))

    ps_spec = pl.BlockSpec((t, LANES), lambda b, h: (b, 0))
    smem = pl.BlockSpec(memory_space=pltpu.SMEM)
    vec = pl.BlockSpec((1, DH), lambda b, h: (0, 0))
    blk = pl.BlockSpec((t, DH), lambda b, h: (b, h))
    return col, ps_spec, smem, vec, blk


def _fox_fwd(pf, ps, f_bias, gq, gk, b_loc, t):
    n = b_loc * t
    nb = t // FOX_BLOCK
    assert t % FOX_TILE == 0 and FOX_TILE % FOX_SHORT == 0, (t, FOX_TILE, FOX_SHORT)
    kt = FOX_TILE
    nsub = kt // FOX_BLOCK
    col, ps_spec, smem, vec, blk = _fox_specs(t)

    def body(q_ref, k_ref, v_ref, ps_ref, fb_ref, gq_ref, gk_ref, o_ref, lse_ref, qs, ks, vs, ccol, crow):
        h = pl.program_id(1)
        _fox_prologue(q_ref, k_ref, v_ref, ps_ref, fb_ref, gq_ref, gk_ref, h, t, qs, ks, vs, ccol, crow)

        def qblock(i, _):
            ri = pl.ds(pl.multiple_of(i * FOX_SHORT, FOX_SHORT), FOX_SHORT)
            qi = qs[ri, :]
            cc = jnp.concatenate([ccol[ri, :]] * nsub, axis=1)

            def ktile(j, carry, masked):
                m, l, acc = carry
                rj = pl.ds(pl.multiple_of(j * kt, kt), kt)
                cr = jnp.concatenate([crow[j * nsub + u, 0:1, :] for u in range(nsub)], axis=1)
                s = _fox_scores(qi, ks[rj, :], cc, cr, i * FOX_SHORT, j * kt, masked)
                m_new = jnp.maximum(m, jnp.max(s, axis=1, keepdims=True))
                p = jnp.exp(s - m_new)
                alpha = jnp.exp(m - m_new)
                l = alpha * l + jnp.sum(p, axis=1, keepdims=True)
                acc = alpha * acc + _dot(_bf(p), vs[rj, :])
                return m_new, l, acc

            last = (i * FOX_SHORT) // kt
            carry = lax.fori_loop(0, last, functools.partial(ktile, masked=False),
                                  (jnp.full((FOX_SHORT, 1), NEG, F32), jnp.zeros((FOX_SHORT, 1), F32),
                                   jnp.zeros((FOX_SHORT, DH), F32)))
            m, l, acc = ktile(last, carry, True)
            o_ref[ri, :] = acc / l
            lse_ref[ri, :] = jnp.broadcast_to(m + jnp.log(l), (FOX_SHORT, LANES))
            return 0

        lax.fori_loop(0, t // FOX_SHORT, qblock, 0)

    o = jax.ShapeDtypeStruct((n, HEADS * DH), F32)
    return pl.pallas_call(
        body, name="fox_fwd", grid=(b_loc, HEADS),
        in_specs=[col(0), col(HEADS), col(2 * HEADS), ps_spec, smem, vec, vec],
        out_specs=[blk, blk], out_shape=[o, o],
        scratch_shapes=[pltpu.VMEM((t, DH), BF16)] * 3 + [pltpu.VMEM((t, LANES), F32), pltpu.VMEM((nb, 8, LANES), F32)],
        compiler_params=_cparams(("arbitrary", "arbitrary")),
    )(pf, pf, pf, ps, f_bias, gq, gk)


def _fox_bwd(pf, ps, f_bias, gq, gk, d_ob, ob, lse, dps_in, b_loc, t):
    n = b_loc * t
    nb = t // FOX_BLOCK
    assert t % FOX_TILE == 0 and FOX_TILE % FOX_SHORT == 0, (t, FOX_TILE, FOX_SHORT)
    qt = FOX_TILE
    scale = DH ** -0.5
    col, ps_spec, smem, vec, blk = _fox_specs(t)

    def body(q_ref, k_ref, v_ref, ps_ref, fb_ref, gq_ref, gk_ref, do_ref, o_ref, lse_ref, dpsi_ref,
             dq_ref, dk_ref, dv_ref, dps_ref, dsm_ref, qs, ks, vs, ccol, crow, dos, dl, dqa, dcr, dcq):
        b, h = pl.program_id(0), pl.program_id(1)
        rq, _, f_in = _fox_prologue(q_ref, k_ref, v_ref, ps_ref, fb_ref, gq_ref, gk_ref, h, t, qs, ks, vs, ccol, crow)
        dov = do_ref[...]
        dos[...] = _bf(dov)
        dl[...] = jnp.broadcast_to(jnp.sum(dov * o_ref[...], axis=1, keepdims=True), (t, LANES))
        dqa[...] = jnp.zeros_like(dqa)
        dcq[...] = jnp.zeros_like(dcq)
        gkv = gk_ref[...]

        ksub = FOX_SHORT // FOX_BLOCK

        def kblock(j, dgk):
            rj = pl.ds(pl.multiple_of(j * FOX_SHORT, FOX_SHORT), FOX_SHORT)
            kj, vj = ks[rj, :], vs[rj, :]
            cr = jnp.concatenate([crow[j * ksub + u, 0:1, :] for u in range(ksub)], axis=1)

            def wide(x):
                return jnp.concatenate([x] * ksub, axis=1)

            def qtile(i, carry, masked):
                dk_acc, dv_acc, dc = carry
                ri = pl.ds(pl.multiple_of(i * qt, qt), qt)
                qi, doi = qs[ri, :], dos[ri, :]
                s = _fox_scores(qi, kj, wide(ccol[ri, :]), cr, i * qt, j * FOX_SHORT, masked)
                p = jnp.exp(s - wide(lse_ref[ri, :]))
                ds = p * (_dot_nt(doi, vj) - wide(dl[ri, :]))
                dsb = _bf(ds)
                dqa[ri, :] += _dot(dsb, kj)
                dcq[ri, :] += jnp.broadcast_to(jnp.sum(ds, axis=1, keepdims=True), (qt, LANES))
                return (dk_acc + _dot_tn(dsb, qi), dv_acc + _dot_tn(_bf(p), doi),
                        dc - jnp.sum(ds, axis=0, keepdims=True))

            z = jnp.zeros((FOX_SHORT, DH), F32)
            first = (j * FOX_SHORT) // qt
            carry = qtile(first, (z, z, jnp.zeros((1, FOX_SHORT), F32)), True)
            dk_acc, dv_acc, dc = lax.fori_loop(first + 1, t // qt, functools.partial(qtile, masked=False), carry)
            dv_ref[rj, :] = _bf(dv_acc)
            for u in range(ksub):
                dcr[pl.ds(pl.multiple_of((j * ksub + u) * 8, 8), 8), :] = jnp.broadcast_to(
                    dc[:, u * FOX_BLOCK:(u + 1) * FOX_BLOCK], (8, LANES))
            kraw = k_ref[rj, :]
            rk = lax.rsqrt(jnp.mean(kraw * kraw, axis=1, keepdims=True) + EPS)
            dkn = dk_acc * scale
            gy = dkn * gkv
            dk_ref[rj, :] = _bf(rk * gy - kraw * (rk * rk * rk * (1.0 / DH)) * jnp.sum(gy * kraw, axis=1, keepdims=True))
            return dgk + jnp.sum(dkn * kraw * rk, axis=0, keepdims=True)

        dgk = lax.fori_loop(0, t // FOX_SHORT, kblock, jnp.zeros((1, DH), F32))

        q = q_ref[...]
        dqn = dqa[...] * scale
        gy = dqn * gq_ref[...]
        dq_ref[...] = _bf(rq * gy - q * (rq * rq * rq * (1.0 / DH)) * jnp.sum(gy * q, axis=1, keepdims=True))
        dgq = jnp.sum(dqn * q * rq, axis=0, keepdims=True)

        r = lax.broadcasted_iota(jnp.int32, (FOX_BLOCK, FOX_BLOCK), 0)
        c = lax.broadcasted_iota(jnp.int32, (FOX_BLOCK, FOX_BLOCK), 1)
        triuf = (r <= c).astype(F32)

        def rev(jj, carry):
            j = nb - 1 - jj
            rows = pl.ds(pl.multiple_of(j * FOX_BLOCK, FOX_BLOCK), FOX_BLOCK)
            rowv = dcr[pl.ds(pl.multiple_of(j * 8, 8), 1), :]
            colv = jnp.sum(jnp.where(c >= r, jnp.broadcast_to(rowv, (FOX_BLOCK, LANES)), 0.0), axis=1, keepdims=True)
            qcol = dcq[rows, :]
            dl[rows, :] = colv + _dot_exact_l(triuf, qcol, terms=3) + carry
            return carry + jnp.sum(rowv, axis=1, keepdims=True) + jnp.sum(qcol, axis=0, keepdims=True)

        lax.fori_loop(0, nb, rev, jnp.zeros((1, LANES), F32))
        d_ff = dl[...] * _sigmoid(-f_in)
        lane = lax.broadcasted_iota(jnp.int32, (t, LANES), 1)

        @pl.when(h == 0)
        def _():
            dps_ref[...] = dpsi_ref[...]

        dps_ref[...] += jnp.where(lane == LANE_FF + h, d_ff, 0.0)

        lane1 = lax.broadcasted_iota(jnp.int32, (1, LANES), 1)
        d_fb = jnp.sum(d_ff, axis=0, keepdims=True)
        small = _stack_rows([dgq, dgk, jnp.where(lane1 == h, d_fb, 0.0)], 8)

        @pl.when((b == 0) & (h == 0))
        def _():
            dsm_ref[...] = jnp.zeros_like(dsm_ref)

        dsm_ref[...] += small

    ob_ = jax.ShapeDtypeStruct((n, HEADS * DH), BF16)
    return pl.pallas_call(
        body, name="fox_bwd", grid=(b_loc, HEADS),
        in_specs=[col(0), col(HEADS), col(2 * HEADS), ps_spec, smem, vec, vec, blk, blk, blk, ps_spec],
        out_specs=[blk, blk, blk, ps_spec, pl.BlockSpec((8, LANES), lambda b, h: (0, 0))],
        out_shape=[ob_, ob_, ob_, jax.ShapeDtypeStruct((n, LANES), F32), jax.ShapeDtypeStruct((8, LANES), F32)],
        scratch_shapes=([pltpu.VMEM((t, DH), BF16)] * 3 + [pltpu.VMEM((t, LANES), F32), pltpu.VMEM((nb, 8, LANES), F32)]
                        + [pltpu.VMEM((t, DH), BF16), pltpu.VMEM((t, LANES), F32), pltpu.VMEM((t, DH), F32),
                           pltpu.VMEM((8 * nb, LANES), F32), pltpu.VMEM((t, LANES), F32)]),
        compiler_params=_cparams(("arbitrary", "arbitrary")),
    )(pf, pf, pf, ps, f_bias, gq, gk, d_ob, ob, lse, dps_in)


class _NoExchange:
    def late_weights(self, after):
        return {}

    def grads_ready(self, grads, tie):
        return tie


def _local_step(x, target, w, b_loc, t, comm=None):
    comm = comm or _NoExchange()
    w = dict(w)
    xf = x
    u = _rms_fwd(xf, w["norm_mix_g"], "rms_mix")
    pg = _mm(u, w["w_gdn"], name="proj_gdn")
    pf = _mm(u, w["w_fox"], name="proj_fox")
    pgate = _mm(u, w["w_gate"], name="proj_gate")
    ps = _mm(u, w["w_small"], name="proj_small")
    oa, o_raw, s_all = _gdn_fwd(pg, ps, w["conv_w"], w["a_log"], w["dt_bias"], w["gdn_norm_g"], b_loc, t)
    ob, lse = _fox_fwd(pf, ps, w["f_bias"], w["fox_q_norm_g"], w["fox_k_norm_g"], b_loc, t)
    w.update(comm.late_weights(ob))
    ya = _mm(oa, w["w_proj_gdn"], name="proj_a")
    yb = _mm(ob, w["w_proj_fox"], name="proj_b")
    merged = _merge_fwd(ya, yb, pgate)
    h = _mm(merged, w["w_out"], name="proj_out", epi=lambda acc, xr: acc + xr, extras=(xf,))
    hn = _rms_fwd(h, w["norm_mlp_g"], "rms_mlp")
    up, act = _mm(hn, w["w_up"], name="mlp_up", out_dtype=BF16, out2=(_relu2, BF16))
    out = _mm(act, w["w_down"], name="mlp_down", tk=2 * MM_TILE, epi=lambda acc, hr: acc + hr, extras=(h,))
    d_out, d_out16, loss_blk = _loss_bwd(out, target)

    g = {}
    g["w_down"] = _mm(act, d_out16, name="dw_down", ta=True, out_dtype=BF16)
    d_up = _mm(d_out16, w["w_down"], name="d_up", tb=True, out_dtype=BF16,
               epi=lambda acc, upr: acc * (2.0 * jnp.maximum(upr.astype(F32), 0.0)), extras=(up,))
    g["w_up"] = _mm(hn, d_up, name="dw_up", ta=True, out_dtype=BF16)
    mlp_gain = comm.grads_ready({"w_down": g["w_down"], "w_up": g["w_up"]}, w["norm_mlp_g"])
    d_hn = _mm(d_up, w["w_up"], name="d_hn", tb=True, tk=2 * MM_TILE)
    dh, dh16, g["norm_mlp_g"] = _rms_bwd(d_hn, h, mlp_gain, d_out, "rms_mlp_bwd")
    g["w_out"] = _mm(merged, dh16, name="dw_out", ta=True, out_dtype=BF16)
    dm = _mm(dh16, w["w_out"], name="d_merged", tb=True)
    dya, dyb, dgate_a, dgate_b = _merge_bwd(dm, ya, yb, pgate)
    g["w_proj_gdn"] = _mm(oa, dya, name="dw_proj_a", ta=True, out_dtype=BF16)
    g["w_proj_fox"] = _mm(ob, dyb, name="dw_proj_b", ta=True, out_dtype=BF16)
    gdn_gain = comm.grads_ready({"w_out": g["w_out"], "w_proj_gdn": g["w_proj_gdn"], "w_proj_fox": g["w_proj_fox"]},
                                w["gdn_norm_g"])
    d_oa = _mm(dya, w["w_proj_gdn"], name="d_oa", tb=True)
    d_ob = _mm(dyb, w["w_proj_fox"], name="d_ob", tb=True)
    dgq, dgk, dgv, dgz, dps, dcw, gdn_small = _gdn_bwd(pg, ps, w["conv_w"], w["a_log"], w["dt_bias"], gdn_gain,
                                                       d_oa, o_raw, s_all, b_loc, t)
    dfq, dfk, dfv, dps, fox_small = _fox_bwd(pf, ps, w["f_bias"], w["fox_q_norm_g"], w["fox_k_norm_g"],
                                             d_ob, ob, lse, dps, b_loc, t)
    segs = [(dgq, "w_gdn", 0), (dgk, "w_gdn", 1024), (dgv, "w_gdn", 2048), (dgz, "w_gdn", 3072),
            (dfq, "w_fox", 0), (dfk, "w_fox", 1024), (dfv, "w_fox", 2048),
            (dgate_a, "w_gate", 0), (dgate_b, "w_gate", 1024)]
    dws = [_mm(u, dps, name="dw_small", ta=True, out_dtype=BF16)]
    dws += [_mm(u, dseg, name=f"dw_in_{idx}", ta=True, out_dtype=BF16) for idx, (dseg, _, _) in enumerate(segs)]
    g["w_in_parts"] = dws
    mix_gain = comm.grads_ready({"w_in_parts": dws}, w["norm_mix_g"])
    du = _du_all(dps, w["w_small"], segs, w)
    grad_x, _, g["norm_mix_g"] = _rms_bwd(du, xf, mix_gain, dh, "rms_mix_bwd")
    g["conv"] = dcw
    g["gdn_small"] = gdn_small
    g["fox_small"] = fox_small
    return loss_blk, grad_x, g


def _position():
    x, y, c = lax.axis_index("x"), lax.axis_index("y"), lax.axis_index("c")
    return x, y, c


def _to_bf16(arrs, name):
    n = len(arrs)

    def body(*refs):
        for i in range(n):
            refs[n + i][...] = _bf(refs[i][...])

    return pl.pallas_call(
        body, name=name,
        out_shape=[jax.ShapeDtypeStruct(a.shape, BF16) for a in arrs],
        compiler_params=_cparams(),
    )(*arrs)


def _all_gather(arrs, name):
    n = len(arrs)
    hbm = pl.BlockSpec(memory_space=pl.ANY)

    def body(*refs):
        ins, outs = refs[:n], refs[n:2 * n]
        send, recv, loc = refs[2 * n:]
        x, y, c = _position()
        me = 4 * x + 2 * y + c
        sibling = (x, y, 1 - c)
        chips = [(1 - x, y), (x, 1 - y), (1 - x, 1 - y)]

        def idx(px, py, pc):
            return 4 * px + 2 * py + pc

        def cp(a, k, block, to, src=None):
            return pltpu.make_async_remote_copy(
                src_ref=outs[a].at[block] if src is None else src, dst_ref=outs[a].at[block],
                send_sem=send.at[a, k], recv_sem=recv.at[a, k], device_id=to, device_id_type=MESH)

        mine = [pltpu.make_async_copy(ins[a], outs[a].at[me], loc.at[a]) for a in range(n)]
        for m in mine:
            m.start()
        first = []
        for a in range(n):
            first.append(cp(a, 0, me, sibling, src=ins[a]))
            first += [cp(a, 1 + j, me, (*chip, c), src=ins[a]) for j, chip in enumerate(chips)]
        for f in first:
            f.start()
        passed = []
        for j, chip in enumerate(chips):
            for a in range(n):
                cp(a, 1 + j, idx(*chip, c), (x, y, c)).wait_recv()
                p = cp(a, 4 + j, idx(*chip, c), sibling)
                p.start()
                passed.append(p)
        for a in range(n):
            cp(a, 0, idx(x, y, 1 - c), (x, y, c)).wait_recv()
            for j, chip in enumerate(chips):
                cp(a, 4 + j, idx(*chip, 1 - c), (x, y, c)).wait_recv()
        for f in first + passed:
            f.wait_send()
        for m in mine:
            m.wait()

    return pl.pallas_call(
        body, name=name,
        in_specs=[hbm] * n, out_specs=[hbm] * n,
        out_shape=[jax.ShapeDtypeStruct((N_DEV,) + a.shape, a.dtype) for a in arrs],
        scratch_shapes=[pltpu.SemaphoreType.DMA((n, 7)), pltpu.SemaphoreType.DMA((n, 7)), pltpu.SemaphoreType.DMA((n,))],
        compiler_params=pltpu.CompilerParams(has_side_effects=True),
    )(*arrs)


def _peer(x, y, c, rel):
    return ((1 - x) if rel & 4 else x, (1 - y) if rel & 2 else y, (1 - c) if rel & 1 else c)


HBM_SPEC = pl.BlockSpec(memory_space=pltpu.HBM)
SEM_SPEC = pl.BlockSpec(memory_space=pltpu.SEMAPHORE)
DATAFLOW = pltpu.SideEffectType.DATAFLOW_SIDE_EFFECTING


CHIP_RELS = (2, 4, 6)


def _push_start(arrs, slots, name, chips=False):
    n = len(arrs)
    n_slots = 4 if chips else N_DEV
    rels = CHIP_RELS if chips else tuple(range(1, N_DEV))
    land_shapes = [a.shape if slots else (n_slots,) + a.shape for a in arrs]

    def body(*refs):
        ins, lands, sends, recvs, token = refs[:n], refs[n:2 * n], refs[2 * n:3 * n], refs[3 * n:4 * n], refs[-1]
        x, y, c = _position()
        for rel in rels:
            px, py, pc = _peer(x, y, c, rel)
            mine, theirs = (2 * x + y, 2 * px + py) if chips else (4 * x + 2 * y + c, 4 * px + 2 * py + pc)
            for a in range(n):
                pltpu.make_async_remote_copy(
                    src_ref=ins[a].at[theirs] if slots else ins[a], dst_ref=lands[a].at[mine],
                    send_sem=sends[a], recv_sem=recvs[a], device_id=(px, py, pc), device_id_type=MESH).start()
        token[...] = jnp.zeros_like(token)

    sem = pltpu.SemaphoreType.DMA(())
    outs = pl.pallas_call(
        body, name=name,
        in_specs=[HBM_SPEC] * (2 * n),
        out_shape=(*[sem] * (2 * n), *[pltpu.HBM(a.shape, a.dtype) for a in arrs],
                   *[pltpu.HBM(s, a.dtype) for s, a in zip(land_shapes, arrs)], jax.ShapeDtypeStruct((8, LANES), F32)),
        out_specs=(*[SEM_SPEC] * (2 * n), *[HBM_SPEC] * (2 * n), pl.BlockSpec(memory_space=pltpu.VMEM)),
        input_output_aliases={i: 2 * n + i for i in range(2 * n)},
        compiler_params=pltpu.CompilerParams(has_side_effects=DATAFLOW),
    )(*[pltpu.with_memory_space_constraint(a, pltpu.HBM) for a in arrs],
      *[pltpu.with_memory_space_constraint(lax.empty(s, a.dtype), pltpu.HBM) for s, a in zip(land_shapes, arrs)])
    return dict(sends=list(outs[:n]), recvs=list(outs[n:2 * n]), ins=list(outs[2 * n:3 * n]),
                lands=list(outs[3 * n:4 * n]), token=outs[-1], copies=len(rels))


def _push_wait(started, after, name):
    n = len(started["ins"])
    copies = started["copies"]

    def body(*refs):
        lands, sends, recvs = refs[n:2 * n], refs[2 * n:3 * n], refs[3 * n:4 * n]
        x, y, c = _position()
        for a in range(n):
            every = lands[a].at[pl.ds(0, copies)]
            drain = pltpu.make_async_remote_copy(src_ref=every, dst_ref=every, send_sem=sends[a], recv_sem=recvs[a],
                                                 device_id=(x, y, c), device_id_type=MESH)
            drain.wait_send()
            drain.wait_recv()

    both = started["ins"] + started["lands"]
    outs = pl.pallas_call(
        body, name=name,
        in_specs=[HBM_SPEC] * (2 * n) + [SEM_SPEC] * (2 * n) + [pl.BlockSpec(memory_space=pl.ANY)],
        out_shape=tuple(pltpu.HBM(a.shape, a.dtype) for a in both), out_specs=tuple([HBM_SPEC] * (2 * n)),
        input_output_aliases={i: i for i in range(2 * n)},
        compiler_params=pltpu.CompilerParams(has_side_effects=DATAFLOW),
    )(*both, *started["sends"], *started["recvs"], after)
    return list(outs[:n]), list(outs[n:])


def _sibling_swap(arr, name):
    chips = N_DEV // 2

    def body(in_ref, out_ref, send, recv):
        x, y, c = _position()
        for s in range(chips):
            pltpu.make_async_remote_copy(src_ref=in_ref.at[2 * s + 1 - c], dst_ref=out_ref.at[s], send_sem=send,
                                         recv_sem=recv, device_id=(x, y, 1 - c), device_id_type=MESH).start()
        pltpu.make_async_remote_copy(src_ref=out_ref, dst_ref=out_ref, send_sem=send, recv_sem=recv,
                                     device_id=(x, y, 1 - c), device_id_type=MESH).wait()

    hbm = pl.BlockSpec(memory_space=pl.ANY)
    return pl.pallas_call(
        body, name=name, in_specs=[hbm], out_specs=hbm,
        out_shape=jax.ShapeDtypeStruct((chips,) + arr.shape[1:], arr.dtype),
        scratch_shapes=[pltpu.SemaphoreType.DMA, pltpu.SemaphoreType.DMA],
        compiler_params=pltpu.CompilerParams(has_side_effects=True),
    )(arr)


def _add_halves(core, arr, other, name):
    ns, r, c = other.shape
    tr = min(r, 256)

    def body(core_ref, a_ref, o_ref, out_ref):
        out_ref[...] = _bf(a_ref[...].astype(F32) + o_ref[...].astype(F32))

    blk = pl.BlockSpec((1, tr, c), lambda s, i, core_ref: (s, i, 0))
    return pl.pallas_call(
        body, name=name,
        grid_spec=pltpu.PrefetchScalarGridSpec(
            num_scalar_prefetch=1, grid=(ns, r // tr),
            in_specs=[pl.BlockSpec((1, tr, c), lambda s, i, core_ref: (2 * s + core_ref[0], i, 0)), blk],
            out_specs=blk),
        out_shape=jax.ShapeDtypeStruct((ns, r, c), BF16),
        compiler_params=_cparams(("parallel", "parallel")),
    )(core, arr, other)


def _all_reduce_small(buf, name):
    rows = buf.shape[0]

    def body(in_ref, out_ref, slots, send, recv):
        x, y, c = _position()
        me = 4 * x + 2 * y + c
        slots[me] = in_ref[...]
        copies = []
        for rel in range(1, N_DEV):
            copies.append(pltpu.make_async_remote_copy(
                src_ref=in_ref, dst_ref=slots.at[me], send_sem=send.at[rel - 1], recv_sem=recv.at[rel - 1],
                device_id=_peer(x, y, c, rel), device_id_type=MESH))
        for cpy in copies:
            cpy.start()
        for cpy in copies:
            cpy.wait()
        tot = slots[0]
        for d in range(1, N_DEV):
            tot = tot + slots[d]
        out_ref[...] = tot

    return pl.pallas_call(
        body, name=name,
        out_shape=jax.ShapeDtypeStruct((rows, LANES), F32),
        in_specs=[pl.BlockSpec(memory_space=pltpu.VMEM)], out_specs=pl.BlockSpec(memory_space=pltpu.VMEM),
        scratch_shapes=[pltpu.VMEM((N_DEV, rows, LANES), F32), pltpu.SemaphoreType.DMA((7,)),
                        pltpu.SemaphoreType.DMA((7,))],
        compiler_params=pltpu.CompilerParams(has_side_effects=True),
    )(buf)


def _adam_math(g, w, m, v):
    m = ADAM_B1 * m + (1.0 - ADAM_B1) * g
    v = ADAM_B2 * v + (1.0 - ADAM_B2) * (g * g)
    m_hat = m / (1.0 - ADAM_B1 ** ADAM_STEP)
    v_hat = v / (1.0 - ADAM_B2 ** ADAM_STEP)
    delta = -ADAM_LR * (m_hat / (jnp.sqrt(v_hat) + ADAM_EPS) + ADAM_WD * w)
    return delta, m, v


def _adam_shard(me, parts, mine, w, m, v, name):
    r, c = w.shape
    tr = min(r, 128)
    n_slots = parts.shape[0]

    def body(me_ref, p_ref, own_ref, w_ref, m_ref, v_ref, g_ref, d_ref, nm_ref, nv_ref):
        own = own_ref[0].astype(F32)
        g = None
        for s in range(n_slots):
            term = jnp.where(me_ref[0] == s, own, p_ref[s].astype(F32))
            g = term if g is None else g + term
        d, nm, nv = _adam_math(g, w_ref[...], m_ref[...], v_ref[...])
        g_ref[...] = g
        d_ref[...] = d
        nm_ref[...] = nm
        nv_ref[...] = nv

    row = pl.BlockSpec((tr, c), lambda i, me_ref: (i, 0))
    o = jax.ShapeDtypeStruct((r, c), F32)
    return pl.pallas_call(
        body, name=name,
        grid_spec=pltpu.PrefetchScalarGridSpec(
            num_scalar_prefetch=1, grid=(r // tr,),
            in_specs=[pl.BlockSpec((n_slots, tr, c), lambda i, me_ref: (0, i, 0)),
                      pl.BlockSpec((1, tr, c), lambda i, me_ref: (me_ref[0], i, 0)), row, row, row],
            out_specs=[row] * 4),
        out_shape=[o] * 4,
        compiler_params=_cparams(("parallel",)),
    )(me, parts, mine, w, m, v)


def _adam_small(g, w, m, v):
    def body(g_ref, w_ref, m_ref, v_ref, d_ref, nm_ref, nv_ref):
        d, nm, nv = _adam_math(g_ref[...], w_ref[...], m_ref[...], v_ref[...])
        d_ref[...] = d
        nm_ref[...] = nm
        nv_ref[...] = nv

    o = jax.ShapeDtypeStruct(g.shape, F32)
    return pl.pallas_call(body, name="adam_small", out_shape=[o] * 3, compiler_params=_cparams())(g, w, m, v)


def _split_w_in(w_full):
    o = IN_OFF
    w_gdn = w_full[:, o["gq"]:o["ga"]]
    w_fox = w_full[:, o["fq"]:o["ff"]]
    w_gate = w_full[:, o["gate_a"]:o["end"]]
    w_small = jnp.concatenate([w_full[:, o["ga"]:o["fq"]], w_full[:, o["ff"]:o["gate_a"]],
                               jnp.zeros((w_full.shape[0], LANES - 24), w_full.dtype)], axis=1)
    return w_gdn, w_fox, w_gate, w_small


def _w_in_pieces(g_in):
    nd, d, c = g_in.shape
    tr = 128
    widths = (IN_OFF["ga"] - IN_OFF["gq"], IN_OFF["ff"] - IN_OFF["fq"], IN_OFF["end"] - IN_OFF["gate_a"], LANES)

    def body(in_ref, gdn_ref, fox_ref, gate_ref, small_ref):
        full = jnp.concatenate([in_ref[dv] for dv in range(nd)], axis=1)
        for ref, piece in zip((gdn_ref, fox_ref, gate_ref, small_ref), _split_w_in(full)):
            ref[...] = piece

    return pl.pallas_call(
        body, name="w_in_pieces", grid=(d // tr,),
        in_specs=[pl.BlockSpec((nd, tr, c), lambda i: (0, i, 0))],
        out_specs=[pl.BlockSpec((tr, wd), lambda i: (i, 0)) for wd in widths],
        out_shape=[jax.ShapeDtypeStruct((d, wd), g_in.dtype) for wd in widths],
        compiler_params=_cparams(("parallel",)),
    )(g_in)


def _w_in_shards(parts, c):
    d = parts[0].shape[0]
    tr = 128

    def body(*refs):
        full = _join_w_in([r[...] for r in refs[:-1]])
        for dv in range(N_DEV):
            refs[-1][dv] = full[:, dv * c:(dv + 1) * c]

    return pl.pallas_call(
        body, name="w_in_shards", grid=(d // tr,),
        in_specs=[pl.BlockSpec((tr, p.shape[1]), lambda i: (i, 0)) for p in parts],
        out_specs=pl.BlockSpec((N_DEV, tr, c), lambda i: (0, i, 0)),
        out_shape=jax.ShapeDtypeStruct((N_DEV, d, c), parts[0].dtype),
        compiler_params=_cparams(("parallel",)),
    )(*parts)


def _join_w_in(parts):
    small = parts[0]
    return jnp.concatenate(parts[1:5] + [small[:, 0:16]] + parts[5:8] + [small[:, 16:24]] + parts[8:10], axis=1)


def _rows128(a, rows):
    flat = a.reshape(-1)
    flat = jnp.concatenate([flat, jnp.zeros((rows * LANES - flat.shape[0],), flat.dtype)])
    return flat.reshape(rows, LANES)


def kernel(x, norm_mix_g, w_in, gdn_conv_w, gdn_a_log, gdn_dt_bias, gdn_norm_g, fox_q_norm_g, fox_k_norm_g, fox_f_bias, w_proj_gdn, w_proj_fox, w_out, norm_mlp_g, w_up, w_down, loss_target, m_norm_mix_g, m_w_in, m_gdn_conv_w, m_gdn_a_log, m_gdn_dt_bias, m_gdn_norm_g, m_fox_q_norm_g, m_fox_k_norm_g, m_fox_f_bias, m_w_proj_gdn, m_w_proj_fox, m_w_out, m_norm_mlp_g, m_w_up, m_w_down, v_norm_mix_g, v_w_in, v_gdn_conv_w, v_gdn_a_log, v_gdn_dt_bias, v_gdn_norm_g, v_fox_q_norm_g, v_fox_k_norm_g, v_fox_f_bias, v_w_proj_gdn, v_w_proj_fox, v_w_out, v_norm_mlp_g, v_w_up, v_w_down):
    b_loc, t, d = x.shape
    n = b_loc * t
    me = 4 * lax.axis_index("x") + 2 * lax.axis_index("y") + lax.axis_index("c")

    late_names = ["w_proj_gdn", "w_proj_fox", "w_out", "w_up", "w_down"]
    big16 = _to_bf16([w_in[0], w_proj_gdn[0], w_proj_fox[0], w_out[0], w_up[0], w_down[0]], "weights_to_bf16")
    g_in, g_conv = _all_gather([big16[0], gdn_conv_w[0]], "gather_w_in")
    behind = (g_conv[0:1, 0, 0:1] * 0.0).astype(BF16)
    late = _push_start([big16[1] + behind] + list(big16[2:]), False, "gather_late_start")
    w_gdn, w_fox, w_gate, w_small = _w_in_pieces(g_in)
    weights = {
        "w_gdn": w_gdn, "w_fox": w_fox, "w_gate": w_gate, "w_small": w_small,
        "conv_w": g_conv.transpose(1, 0, 2).reshape(CONV_K, 3 * d),
        "norm_mix_g": norm_mix_g + late["token"][0:1, 0:1], "norm_mlp_g": norm_mlp_g, "a_log": gdn_a_log,
        "dt_bias": gdn_dt_bias, "gdn_norm_g": gdn_norm_g, "fox_q_norm_g": fox_q_norm_g, "fox_k_norm_g": fox_k_norm_g,
        "f_bias": fox_f_bias,
    }
    c_in, c_up = w_in.shape[2], w_up.shape[2]
    me1 = jnp.reshape(me, (1,)).astype(jnp.int32)
    chip1 = jnp.reshape(2 * lax.axis_index("x") + lax.axis_index("y"), (1,)).astype(jnp.int32)
    core1 = jnp.reshape(lax.axis_index("c"), (1,)).astype(jnp.int32)

    class _Exchange:
        def __init__(self):
            self.started = []

        def late_weights(self, after):
            shards, lands = _push_wait(late, after, "gather_late_wait")
            full = [lax.dynamic_update_index_in_dim(land, shard, me, 0) for land, shard in zip(lands, shards)]
            g_pa, g_pb, g_out, g_up, g_down = full
            return {"w_proj_gdn": g_pa.reshape(d, d), "w_proj_fox": g_pb.reshape(d, d), "w_out": g_out.reshape(d, d),
                    "w_up": g_up.transpose(1, 0, 2).reshape(d, D_FF), "w_down": g_down.reshape(D_FF, d)}

        def grads_ready(self, grads, tie):
            names = list(grads)
            if names == ["w_in_parts"]:
                halves = _w_in_shards(grads["w_in_parts"], c_in)
                other = _sibling_swap(halves, "grads_w_in_sibling")
                pair = _add_halves(core1, halves, other, "grads_w_in_pair")
                st = _push_start([pair], True, "grads_start_w_in_parts", chips=True)
            else:
                layout = {"w_up": lambda a: a.reshape(d, N_DEV, c_up).transpose(1, 0, 2),
                          "w_down": lambda a: a.reshape(N_DEV, D_FF // N_DEV, d)}
                arrs = [layout.get(k, lambda a: a.reshape(N_DEV, d // N_DEV, d))(grads[k]) for k in names]
                st = _push_start(arrs, True, "grads_start_" + names[0])
            self.started.append((names, st))
            return tie + st["token"][0:1, 0:1]

    comm = _Exchange()
    loss_blk, grad_x, g = _local_step(x.reshape(n, d), loss_target.reshape(n, d), weights, b_loc, t, comm)

    shards = {"w_in_parts": (w_in, m_w_in, v_w_in), "w_proj_gdn": (w_proj_gdn, m_w_proj_gdn, v_w_proj_gdn),
              "w_proj_fox": (w_proj_fox, m_w_proj_fox, v_w_proj_fox), "w_out": (w_out, m_w_out, v_w_out),
              "w_up": (w_up, m_w_up, v_w_up), "w_down": (w_down, m_w_down, v_w_down)}
    adam = {}

    def finish(names, st, after):
        mine, parts = _push_wait(st, after, "grads_wait_" + names[0])
        slot = chip1 if st["copies"] == len(CHIP_RELS) else me1
        for k, own, part in zip(names, mine, parts):
            wi, mi, vi = shards[k]
            adam[k] = [r[None] for r in _adam_shard(slot, part, own, wi[0], mi[0], vi[0], "adam_" + k)]

    for names, st in comm.started[:-1]:
        finish(names, st, grad_x)

    conv_rows = CONV_K * 3 * d // LANES
    conv_g = g["conv"].transpose(1, 0, 2).reshape(conv_rows, LANES)
    buf = jnp.concatenate([conv_g, g["norm_mix_g"].reshape(8, LANES), g["norm_mlp_g"].reshape(8, LANES),
                           g["gdn_small"], g["fox_small"], loss_blk], axis=0)
    anchor = sum(adam[k][1][0, 0:1, 0:LANES] for names, _ in comm.started[:-1] for k in names) * 0.0
    tot = _all_reduce_small(buf + anchor, "all_reduce_small")
    finish(*comm.started[-1], tot)
    big_out = [adam[k] for k in ["w_in_parts"] + late_names]
    o = conv_rows
    conv_full = tot[0:o].reshape(CONV_K, 3 * d)
    c_conv = gdn_conv_w.shape[2]
    g_conv_shard = lax.dynamic_slice(conv_full, (0, me * c_conv), (CONV_K, c_conv))
    g_mix = tot[o:o + 8].reshape(1, d)
    g_mlp = tot[o + 8:o + 16].reshape(1, d)
    gs, fs = tot[o + 16:o + 24], tot[o + 24:o + 32]
    loss = tot[o + 32, 0]
    small_g = [g_mix, g_conv_shard[None], gs[0:1, 0:HEADS], gs[1:2, 0:HEADS], gs[2:3], fs[0:1], fs[1:2], fs[2:3, 0:HEADS],
               g_mlp]
    small_w = [norm_mix_g, gdn_conv_w, gdn_a_log, gdn_dt_bias, gdn_norm_g, fox_q_norm_g, fox_k_norm_g, fox_f_bias,
               norm_mlp_g]
    small_m = [m_norm_mix_g, m_gdn_conv_w, m_gdn_a_log, m_gdn_dt_bias, m_gdn_norm_g, m_fox_q_norm_g, m_fox_k_norm_g,
               m_fox_f_bias, m_norm_mlp_g]
    small_v = [v_norm_mix_g, v_gdn_conv_w, v_gdn_a_log, v_gdn_dt_bias, v_gdn_norm_g, v_fox_q_norm_g, v_fox_k_norm_g,
               v_fox_f_bias, v_norm_mlp_g]
    row_counts = [-(-a.size // (8 * LANES)) * 8 for a in small_w]

    def pack(arrs):
        return jnp.concatenate([_rows128(a, rc) for a, rc in zip(arrs, row_counts)], axis=0)

    sd, sm, sv = _adam_small(pack(small_g), pack(small_w), pack(small_m), pack(small_v))

    def unpack(p):
        outs, r0 = [], 0
        for a, rc in zip(small_w, row_counts):
            outs.append(p[r0:r0 + rc].reshape(-1)[:a.size].reshape(a.shape))
            r0 += rc
        return outs

    small_out = [small_g_i.reshape(w_i.shape) for small_g_i, w_i in zip(small_g, small_w)], unpack(sd), unpack(sm), unpack(sv)

    def ordered(kind):
        s = small_out[kind]
        bo = [b[kind] for b in big_out]
        return [s[0], bo[0], s[1], s[2], s[3], s[4], s[5], s[6], s[7], bo[1], bo[2], bo[3], s[8], bo[4], bo[5]]

    return (loss, grad_x.reshape(b_loc, t, d), *ordered(0), *ordered(1), *ordered(2), *ordered(3))
```
